```python
import math
import jax, jax.numpy as jnp
from jax import lax
import numpy as np

D_MODEL = 1024
BATCH = 8
SEQ = 2048
DEPTH = 1

MLA_HEADS = 8
MLA_NOPE = 64
MLA_ROPE = 32
MLA_V = 64
MLA_QK = MLA_NOPE + MLA_ROPE
MLA_Q_RANK = 384
MLA_KV_RANK = 256
ROPE_THETA = 10000.0
NSA_HEADS = 8
NSA_KV_HEADS = 2
NSA_GROUP = NSA_HEADS // NSA_KV_HEADS
NSA_HEAD_DIM = 64
N_BRANCH = 3
CMP_BLOCK = 32
CMP_STRIDE = 16
CMP_HIDDEN = 128
SEL_BLOCK = 64
SEL_TOP_N = 16
WINDOW = 512
FORCE_SCORE = 1e4
REL_BUCKETS = 32
REL_MAX_DIST = 128
D_FF = 4 * D_MODEL
Q_BLOCK = 128
SEL_Q_BLOCK = 64
EPS = 1e-6

MLA_OUT = MLA_HEADS * MLA_V
NSA_OUT = NSA_HEADS * NSA_HEAD_DIM
MIX_WIDTH = MLA_OUT + NSA_OUT
IN_SPLIT_SIZES = (MLA_Q_RANK, MLA_KV_RANK, MLA_ROPE, NSA_HEADS * NSA_HEAD_DIM,
                  2 * N_BRANCH * NSA_KV_HEADS * NSA_HEAD_DIM, N_BRANCH * NSA_HEADS)
D_IN = sum(IN_SPLIT_SIZES)

kernel_name = "hymba_mla_nsa_hybrid_block"


def rms_norm(x, g):
    xf = x.astype(jnp.float32)
    y = xf * lax.rsqrt(jnp.mean(xf * xf, axis=-1, keepdims=True) + EPS)
    return (y * g.astype(jnp.float32)).astype(x.dtype)


def rope(x, pos):
    half = x.shape[-1] // 2
    inv = 1.0 / (ROPE_THETA ** (jnp.arange(half, dtype=jnp.float32) / half))
    ang = pos.astype(jnp.float32)[:, None] * inv[None, :]
    cos = jnp.cos(ang)[None, :, None, :]
    sin = jnp.sin(ang)[None, :, None, :]
    xf = x.astype(jnp.float32)
    x1, x2 = xf[..., :half], xf[..., half:]
    return jnp.concatenate([x1 * cos - x2 * sin, x2 * cos + x1 * sin], axis=-1).astype(x.dtype)


def t5_bucket(dist):
    n = jnp.maximum(dist, 0)
    max_exact = REL_BUCKETS // 2
    large = max_exact + (jnp.log(jnp.maximum(n, 1).astype(jnp.float32) / max_exact)
                         / math.log(REL_MAX_DIST / max_exact)
                         * (REL_BUCKETS - max_exact)).astype(jnp.int32)
    large = jnp.minimum(large, REL_BUCKETS - 1)
    return jnp.where(n < max_exact, n, large)


def masked_softmax(s, mask):
    s = jnp.where(mask, s.astype(jnp.float32), -jnp.inf)
    m = jnp.max(s, axis=-1, keepdims=True)
    m = jnp.where(jnp.isfinite(m), m, 0.0)
    p = jnp.exp(s - m)
    return p / jnp.maximum(jnp.sum(p, axis=-1, keepdims=True), 1e-30)


def mla_mixer(h_q, h_kv, h_kr, cq_g, ckv_g, w_uq, w_ukv, q_gain, k_gain):
    B, S, _ = h_q.shape
    pos = jnp.arange(S)
    q = (rms_norm(h_q, cq_g) @ w_uq).reshape(B, S, MLA_HEADS, MLA_QK)
    kv = (rms_norm(h_kv, ckv_g) @ w_ukv).reshape(B, S, MLA_HEADS, MLA_NOPE + MLA_V)
    k_nope, v = kv[..., :MLA_NOPE], kv[..., MLA_NOPE:]
    k_r = jnp.broadcast_to(h_kr[:, :, None, :], (B, S, MLA_HEADS, MLA_ROPE))
    k = jnp.concatenate([k_nope, k_r], axis=-1)
    q = rms_norm(q, q_gain)
    k = rms_norm(k, k_gain)
    q = jnp.concatenate([q[..., :MLA_NOPE], rope(q[..., MLA_NOPE:], pos)], axis=-1)
    k = jnp.concatenate([k[..., :MLA_NOPE], rope(k[..., MLA_NOPE:], pos)], axis=-1)
    scale = MLA_QK ** -0.5
    nb = S // Q_BLOCK
    qb = q.reshape(B, nb, Q_BLOCK, MLA_HEADS, MLA_QK).transpose(1, 0, 2, 3, 4)

    def block(args):
        qi, i = args
        s = jnp.einsum('bqhd,bkhd->bhqk', qi, k).astype(jnp.float32) * scale
        qpos = i * Q_BLOCK + jnp.arange(Q_BLOCK)
        p = masked_softmax(s, pos[None, :] <= qpos[:, None])
        return jnp.einsum('bhqk,bkhd->bqhd', p.astype(v.dtype), v)

    o = lax.map(block, (qb, jnp.arange(nb)))
    return o.transpose(1, 0, 2, 3, 4).reshape(B, S, MLA_OUT)


def nsa_mixer(q_in, kv_in, gate_logits, q_gain, k_gain, cmp_pe, cmp_w1, cmp_w2, rel_bias):
    B, S, _ = q_in.shape
    H, G, R, Dh = NSA_HEADS, NSA_KV_HEADS, NSA_GROUP, NSA_HEAD_DIM
    pos = jnp.arange(S)
    scale = Dh ** -0.5
    qg = rms_norm(q_in.reshape(B, S, H, Dh), q_gain).reshape(B, S, G, R, Dh)
    kv = kv_in.reshape(B, S, N_BRANCH, 2, G, Dh)

    n_chunk = S // CMP_STRIDE
    n_cmp = n_chunk - 1

    def compress(t, pe, w1, w2):
        c = t.reshape(B, n_chunk, CMP_STRIDE, G, Dh)
        blocks = jnp.concatenate([c[:, :-1], c[:, 1:]], axis=2)
        blocks = blocks + pe[None, None, :, None, :]
        flat = blocks.transpose(0, 1, 3, 2, 4).reshape(B, n_cmp, G, CMP_BLOCK * Dh)
        return jax.nn.silu(flat @ w1) @ w2

    k_cmp = rms_norm(compress(kv[:, :, 0, 0], cmp_pe[0], cmp_w1[0], cmp_w2[0]), k_gain[0])
    v_cmp = compress(kv[:, :, 0, 1], cmp_pe[1], cmp_w1[1], cmp_w2[1])
    cmp_start = jnp.arange(n_cmp) * CMP_STRIDE
    dist_c = pos[:, None] - (cmp_start + CMP_BLOCK - 1)[None, :]
    bias_c = rel_bias[t5_bucket(dist_c)].reshape(S, n_cmp, G, R).transpose(2, 3, 0, 1)
    s_c = jnp.einsum('bsgrd,bcgd->bgrsc', qg, k_cmp).astype(jnp.float32) * scale + bias_c
    p_c = masked_softmax(s_c, dist_c >= 0)
    o_cmp = jnp.einsum('bgrsc,bcgd->bsgrd', p_c.astype(v_cmp.dtype), v_cmp)

    n_sel = S // SEL_BLOCK
    sel_start = jnp.arange(n_sel) * SEL_BLOCK
    overlap = jnp.clip(jnp.minimum(cmp_start[:, None] + CMP_BLOCK, sel_start[None, :] + SEL_BLOCK)
                       - jnp.maximum(cmp_start[:, None], sel_start[None, :]), 0, None)
    overlap = overlap.astype(jnp.float32) / CMP_BLOCK
    imp = jnp.einsum('bgrsc,cj->bgsj', p_c, overlap)
    blk_t = pos // SEL_BLOCK
    j = jnp.arange(n_sel)[None, :]
    valid = j <= blk_t[:, None]
    forced = valid & ((j == 0) | (j == blk_t[:, None]) | (j == blk_t[:, None] - 1))
    score = jnp.where(forced, FORCE_SCORE, jnp.where(valid, imp, -jnp.inf))
    n_top = min(SEL_TOP_N, n_sel)
    _, sel_idx = lax.top_k(score, n_top)

    k_sel = rms_norm(kv[:, :, 1, 0], k_gain[1])
    v_sel = kv[:, :, 1, 1]
    ksb = k_sel.reshape(B, n_sel, SEL_BLOCK, G, Dh).transpose(0, 3, 1, 2, 4)
    vsb = v_sel.reshape(B, n_sel, SEL_BLOCK, G, Dh).transpose(0, 3, 1, 2, 4)
    table_g = rel_bias.reshape(REL_BUCKETS, G, R).transpose(1, 0, 2)
    n_qc = S // SEL_Q_BLOCK
    q_c = qg.reshape(B, n_qc, SEL_Q_BLOCK, G, R, Dh).transpose(1, 0, 2, 3, 4, 5)
    idx_c = sel_idx.reshape(B, G, n_qc, SEL_Q_BLOCK, n_top).transpose(2, 0, 1, 3, 4)
    pos_c = pos.reshape(n_qc, SEL_Q_BLOCK)
    bi = jnp.arange(B)[:, None, None, None]
    gi = jnp.arange(G)[None, :, None, None]
    m_sel = n_top * SEL_BLOCK

    def sel_block(args):
        qi, idx, qpos = args
        kb = ksb[bi, gi, idx]
        vb = vsb[bi, gi, idx].reshape(B, G, SEL_Q_BLOCK, m_sel, Dh)
        key_pos = idx[..., None] * SEL_BLOCK + jnp.arange(SEL_BLOCK)
        dist = qpos[None, None, :, None, None] - key_pos
        bias = table_g[gi[..., None], t5_bucket(dist)]
        s = jnp.einsum('bqgrd,bgqnkd->bgrqnk', qi, kb).astype(jnp.float32) * scale \
            + jnp.moveaxis(bias, -1, 2)
        s = s.reshape(B, G, R, SEL_Q_BLOCK, m_sel)
        mask = (dist >= 0).reshape(B, G, 1, SEL_Q_BLOCK, m_sel)
        p = masked_softmax(s, mask)
        return jnp.einsum('bgrqm,bgqmd->bqgrd', p.astype(vb.dtype), vb)

    o_sel = lax.map(sel_block, (q_c, idx_c, pos_c))
    o_sel = o_sel.transpose(1, 0, 2, 3, 4, 5).reshape(B, S, G, R, Dh)

    k_win = rms_norm(kv[:, :, 2, 0], k_gain[2])
    v_win = kv[:, :, 2, 1]
    kp = jnp.pad(k_win, ((0, 0), (WINDOW, 0), (0, 0), (0, 0)))
    vp = jnp.pad(v_win, ((0, 0), (WINDOW, 0), (0, 0), (0, 0)))
    span = Q_BLOCK + WINDOW
    n_wb = S // Q_BLOCK
    off = WINDOW + jnp.arange(Q_BLOCK)[:, None] - jnp.arange(span)[None, :]
    band = (off >= 0) & (off < WINDOW)
    bias_w = rel_bias[t5_bucket(off)].reshape(Q_BLOCK, span, G, R).transpose(2, 3, 0, 1)
    q_w = qg.reshape(B, n_wb, Q_BLOCK, G, R, Dh).transpose(1, 0, 2, 3, 4, 5)

    def win_block(args):
        qi, i = args
        start = i * Q_BLOCK
        kb = lax.dynamic_slice_in_dim(kp, start, span, axis=1)
        vb = lax.dynamic_slice_in_dim(vp, start, span, axis=1)
        key_pos = start - WINDOW + jnp.arange(span)
        mask = band & (key_pos >= 0)[None, :]
        s = jnp.einsum('bqgrd,bkgd->bgrqk', qi, kb).astype(jnp.float32) * scale + bias_w
        p = masked_softmax(s, mask)
        return jnp.einsum('bgrqk,bkgd->bqgrd', p.astype(vb.dtype), vb)

    o_win = lax.map(win_block, (q_w, jnp.arange(n_wb)))
    o_win = o_win.transpose(1, 0, 2, 3, 4, 5).reshape(B, S, G, R, Dh)

    g = jax.nn.sigmoid(gate_logits.astype(jnp.float32)).reshape(B, S, N_BRANCH, G, R, 1)
    o = g[:, :, 0] * o_cmp + g[:, :, 1] * o_sel + g[:, :, 2] * o_win
    return o.astype(q_in.dtype).reshape(B, S, NSA_OUT)


def setup_inputs(seed: int = 0) -> dict:
    key = jax.random.key(seed)
    ks = jax.random.split(key, 21)
    L = DEPTH

    def nrm(k, shape, scale):
        return jax.random.normal(k, shape, jnp.float32) * scale

    def gain(k, shape):
        return 1.0 + 0.01 * jax.random.normal(k, shape, jnp.float32)

    return {
        "x": nrm(ks[0], (BATCH, SEQ, D_MODEL), 1.0),
        "ln1_g": gain(ks[1], (L, D_MODEL)),
        "w_in": nrm(ks[2], (L, D_MODEL, D_IN), D_MODEL ** -0.5),
        "mla_cq_norm_g": gain(ks[3], (L, MLA_Q_RANK)),
        "mla_ckv_norm_g": gain(ks[4], (L, MLA_KV_RANK)),
        "mla_w_uq": nrm(ks[5], (L, MLA_Q_RANK, MLA_HEADS * MLA_QK), MLA_Q_RANK ** -0.5),
        "mla_w_ukv": nrm(ks[6], (L, MLA_KV_RANK, MLA_HEADS * (MLA_NOPE + MLA_V)), MLA_KV_RANK ** -0.5),
        "mla_q_gain": gain(ks[7], (L, MLA_QK)),
        "mla_k_gain": gain(ks[8], (L, MLA_QK)),
        "nsa_q_gain": gain(ks[9], (L, NSA_HEAD_DIM)),
        "nsa_k_gain": gain(ks[10], (L, N_BRANCH, NSA_HEAD_DIM)),
        "nsa_cmp_pe": nrm(ks[11], (L, 2, CMP_BLOCK, NSA_HEAD_DIM), 0.1),
        "nsa_cmp_w1": nrm(ks[12], (L, 2, CMP_BLOCK * NSA_HEAD_DIM, CMP_HIDDEN), (CMP_BLOCK * NSA_HEAD_DIM) ** -0.5),
        "nsa_cmp_w2": nrm(ks[13], (L, 2, CMP_HIDDEN, NSA_HEAD_DIM), CMP_HIDDEN ** -0.5),
        "rel_bias": nrm(ks[14], (REL_BUCKETS, NSA_HEADS), 0.5),
        "grp_norm_mla": gain(ks[15], (L, MLA_OUT)),
        "grp_norm_nsa": gain(ks[16], (L, NSA_OUT)),
        "w_o": nrm(ks[17], (L, MIX_WIDTH, D_MODEL), MIX_WIDTH ** -0.5),
        "ln2_g": gain(ks[18], (L, D_MODEL)),
        "w_up": nrm(ks[19], (L, D_MODEL, D_FF), D_MODEL ** -0.5),
        "w_down": nrm(ks[20], (L, D_FF, D_MODEL), D_FF ** -0.5),
    }


def reference(x, ln1_g, w_in, mla_cq_norm_g, mla_ckv_norm_g, mla_w_uq, mla_w_ukv, mla_q_gain,
              mla_k_gain, nsa_q_gain, nsa_k_gain, nsa_cmp_pe, nsa_cmp_w1, nsa_cmp_w2, rel_bias,
              grp_norm_mla, grp_norm_nsa, w_o, ln2_g, w_up, w_down):
    offsets = []
    acc = 0
    for size in IN_SPLIT_SIZES[:-1]:
        acc += size
        offsets.append(acc)
    for l in range(DEPTH):
        h = rms_norm(x, ln1_g[l])
        proj = h @ w_in[l]
        h_q, h_kv, h_kr, q_b, kv_b, gate_b = jnp.split(proj, offsets, axis=-1)
        o_a = mla_mixer(h_q, h_kv, h_kr, mla_cq_norm_g[l], mla_ckv_norm_g[l], mla_w_uq[l],
                        mla_w_ukv[l], mla_q_gain[l], mla_k_gain[l])
        o_b = nsa_mixer(q_b, kv_b, gate_b, nsa_q_gain[l], nsa_k_gain[l], nsa_cmp_pe[l],
                        nsa_cmp_w1[l], nsa_cmp_w2[l], rel_bias)
        mix = jnp.concatenate([rms_norm(o_a, grp_norm_mla[l]), rms_norm(o_b, grp_norm_nsa[l])], axis=-1)
        x = x + mix @ w_o[l]
        h2 = rms_norm(x, ln2_g[l])
        x = x + jnp.square(jax.nn.relu(h2 @ w_up[l])) @ w_down[l]
    return x
```

```python
import functools
import math

import numpy as np
import jax
import jax.numpy as jnp
from jax import lax
from jax.experimental import pallas as pl
from jax.experimental.pallas import tpu as pltpu

F32 = jnp.float32
BF16 = jnp.bfloat16

D_MODEL = 1024
SEQ = 2048
MLA_HEADS = 8
MLA_NOPE = 64
MLA_ROPE = 32
MLA_V = 64
MLA_QK = MLA_NOPE + MLA_ROPE
MLA_Q_RANK = 384
MLA_KV_RANK = 256
ROPE_THETA = 10000.0
NSA_HEADS = 8
NSA_KV_HEADS = 2
NSA_GROUP = NSA_HEADS // NSA_KV_HEADS
NSA_HEAD_DIM = 64
N_BRANCH = 3
CMP_BLOCK = 32
CMP_STRIDE = 16
CMP_HIDDEN = 128
SEL_BLOCK = 64
SEL_TOP_N = 16
WINDOW = 512
FORCE_SCORE = 1e4
REL_BUCKETS = 32
REL_MAX_DIST = 128
D_FF = 4 * D_MODEL
EPS = 1e-6

LANES = 128
NEG = -1e30
N_CHUNK = SEQ // CMP_STRIDE
N_SEL = SEQ // SEL_BLOCK
D_PROJ = 2048
NSA_TILE = 128
N_WIN_TILES = WINDOW // NSA_TILE + 1
MLA_TILE = 256
PROJ_ROWS = 256
TAIL_ROWS = 512
FF_CHUNK = 1024
VMEM_LIMIT = 56 * 1024 * 1024


def _dot(a, b):
    return jnp.dot(a, b, preferred_element_type=F32)


def _dot_nt(a, b):
    return lax.dot_general(a, b, (((1,), (1,)), ((), ())), preferred_element_type=F32)


def _split_dot(a, b, terms=2):
    out = None
    rem = a
    for _ in range(terms):
        piece = rem.astype(BF16)
        part = _dot(piece, b)
        out = part if out is None else out + part
        rem = rem - piece.astype(F32)
    return out


def _rms(x, width):
    return x * lax.rsqrt(jnp.sum(x * x, axis=-1, keepdims=True) * (1.0 / width) + EPS)


def _t5_bucket_np(dist):
    n = np.maximum(dist, 0)
    max_exact = REL_BUCKETS // 2
    large = max_exact + (np.log(np.maximum(n, 1).astype(np.float32) / max_exact)
                         / math.log(REL_MAX_DIST / max_exact)
                         * (REL_BUCKETS - max_exact)).astype(np.int32)
    large = np.minimum(large, REL_BUCKETS - 1)
    return np.where(n < max_exact, n, large).astype(np.int32)


def _bias_index_table():
    q = np.arange(SEQ)[:, None]
    c = np.arange(LANES)[None, :]
    dist_c = q - (c * CMP_STRIDE + CMP_BLOCK - 1)
    idx_c = np.where((dist_c >= 0) & (c < N_CHUNK - 1), _t5_bucket_np(dist_c), -1)
    tiles = []
    i = np.arange(NSA_TILE)[:, None]
    j = np.arange(NSA_TILE)[None, :]
    for off in range(N_WIN_TILES):
        d = off * NSA_TILE + i - j
        tiles.append(np.where((d >= 0) & (d < WINDOW), _t5_bucket_np(d), -1))
    return np.concatenate([idx_c] + tiles, axis=0).astype(np.int32)


def _bias_table_kernel(rb_ref, idx_ref, out_ref):
    h = pl.program_id(0)
    idx = idx_ref[...]
    acc = jnp.full(idx.shape, NEG, F32)
    for b in range(REL_BUCKETS):
        acc = jnp.where(idx == b, rb_ref[h, b], acc)
    out_ref[0] = acc


def _bias_tables(rel_bias):
    idx = jnp.asarray(_bias_index_table())
    rows = idx.shape[0]
    blk = NSA_TILE
    return pl.pallas_call(
        _bias_table_kernel,
        grid=(NSA_HEADS, rows // blk),
        in_specs=[pl.BlockSpec(memory_space=pltpu.SMEM),
                  pl.BlockSpec((blk, LANES), lambda h, r: (r, 0))],
        out_specs=pl.BlockSpec((1, blk, LANES), lambda h, r: (h, r, 0)),
        out_shape=jax.ShapeDtypeStruct((NSA_HEADS, rows, LANES), F32),
        name="bias_tables",
    )(rel_bias.T, idx)


def _rope_tables(scale):
    half = MLA_ROPE // 2
    inv = 1.0 / (ROPE_THETA ** (jnp.arange(half, dtype=F32) / half))
    ang = jnp.arange(SEQ, dtype=F32)[:, None] * inv[None, :]
    cos, sin = jnp.cos(ang), jnp.sin(ang)
    ones = jnp.ones((SEQ, MLA_NOPE), F32)
    z16 = jnp.zeros((SEQ, half), F32)
    z32 = jnp.zeros((SEQ, LANES - MLA_QK), F32)
    z64 = jnp.zeros((SEQ, MLA_NOPE), F32)
    c = jnp.concatenate([ones, cos, cos, z32], axis=1) * scale
    s_up = jnp.concatenate([z64, z16, sin, z32], axis=1) * scale
    s_dn = jnp.concatenate([z64, -sin, z16, z32], axis=1) * scale
    return c, s_up, s_dn


def _rope_slabs(x, c, s_up, s_dn):
    half = MLA_ROPE // 2
    outs = []
    for h in range(MLA_HEADS):
        sl = x[:, h * LANES:(h + 1) * LANES]
        outs.append(sl * c + pltpu.roll(sl, half, 1) * s_up + pltpu.roll(sl, LANES - half, 1) * s_dn)
    return jnp.concatenate(outs, axis=1)


def _proj_kernel(x_ref, ln1_ref, wp_ref, cqg_ref, ckvg_ref, wuq_ref, wukk_ref, wukv_ref,
                 qg_ref, kg_ref, qc_ref, qsu_ref, qsd_ref, kc_ref, ksu_ref, ksd_ref,
                 pkr_ref, seg128_ref, seg128t_ref, seg64_ref, seg64t_ref,
                 nq_gain_ref, nk_gain_ref, dup_ref,
                 qmla_ref, kmla_ref, vmla_ref, qnsa_ref, kdup_ref, vdup_ref, kvc_ref, gate_ref):
    x = x_ref[...]
    h = (_rms(x, D_MODEL) * ln1_ref[...]).astype(BF16)
    proj = _dot(h, wp_ref[...])

    def head_norm(v, seg_ref, segt_ref, width):
        ss = _split_dot(v * v, seg_ref[...])
        rs = lax.rsqrt(ss * (1.0 / width) + EPS)
        return v * _split_dot(rs, segt_ref[...])

    cq = (_rms(proj[:, :MLA_Q_RANK], MLA_Q_RANK) * cqg_ref[...]).astype(BF16)
    q = head_norm(_dot(cq, wuq_ref[...]), seg128_ref, seg128t_ref, MLA_QK) * qg_ref[...]
    qmla_ref[...] = _rope_slabs(q, qc_ref[...], qsu_ref[...], qsd_ref[...]).astype(BF16)

    o_kv = MLA_Q_RANK
    ckv = (_rms(proj[:, o_kv:o_kv + MLA_KV_RANK], MLA_KV_RANK) * ckvg_ref[...]).astype(BF16)
    misc = proj[:, 640:768]
    k = _dot(ckv, wukk_ref[...]) + _split_dot(misc, pkr_ref[...])
    k = head_norm(k, seg128_ref, seg128t_ref, MLA_QK) * kg_ref[...]
    kmla_ref[...] = _rope_slabs(k, kc_ref[...], ksu_ref[...], ksd_ref[...]).astype(BF16)
    vmla_ref[...] = _dot(ckv, wukv_ref[...]).astype(BF16)

    qb = head_norm(proj[:, 768:1280], seg64_ref, seg64t_ref, NSA_HEAD_DIM)
    qnsa_ref[...] = (qb * nq_gain_ref[...]).astype(BF16)

    kk = proj[:, 1280:1536]
    seg_k = seg64_ref[:256, :]
    segt_k = seg64t_ref[:, :256]
    ss = _split_dot(kk * kk, seg_k)
    kn = kk * _split_dot(lax.rsqrt(ss * (1.0 / NSA_HEAD_DIM) + EPS), segt_k) * nk_gain_ref[...]
    kdup_ref[...] = _dot(kn.astype(BF16), dup_ref[...]).astype(BF16)
    vdup_ref[...] = _dot(proj[:, 1536:1792].astype(BF16), dup_ref[...]).astype(BF16)
    kvc_ref[...] = proj[:, 1792:2048]
    gate_ref[...] = 1.0 / (1.0 + jnp.exp(-misc))


def _seg_matrix(width, seg, cols=LANES):
    m = np.zeros((width, cols), np.float32)
    m[np.arange(width), np.arange(width) // seg] = 1.0
    return m


def _projections(x2d, ln1_g, w_in, cq_g, ckv_g, w_uq, w_ukv, q_gain, k_gain, nq_gain, nk_gain):
    t = x2d.shape[0]
    kv0 = 1184
    seg = lambda a, b: w_in[:, a:b]
    kvb = lambda br, kv, g: seg(kv0 + ((br * 2 + kv) * 2 + g) * 64, kv0 + ((br * 2 + kv) * 2 + g) * 64 + 64)
    wp = jnp.concatenate([
        seg(0, 640),
        seg(640, 672), seg(1952, 1976), jnp.zeros((D_MODEL, 72), F32),
        seg(672, 1184),
        kvb(1, 0, 0), kvb(1, 0, 1), kvb(2, 0, 0), kvb(2, 0, 1),
        kvb(1, 1, 0), kvb(1, 1, 1), kvb(2, 1, 0), kvb(2, 1, 1),
        kvb(0, 0, 0), kvb(0, 0, 1), kvb(0, 1, 0), kvb(0, 1, 1),
    ], axis=1).astype(BF16)
    pad = LANES - MLA_QK
    wuq = jnp.pad(w_uq.reshape(MLA_Q_RANK, MLA_HEADS, MLA_QK), ((0, 0), (0, 0), (0, pad)))
    wuq = wuq.reshape(MLA_Q_RANK, MLA_HEADS * LANES).astype(BF16)
    wkv = w_ukv.reshape(MLA_KV_RANK, MLA_HEADS, MLA_NOPE + MLA_V)
    wukk = jnp.pad(wkv[:, :, :MLA_NOPE], ((0, 0), (0, 0), (0, LANES - MLA_NOPE)))
    wukk = wukk.reshape(MLA_KV_RANK, MLA_HEADS * LANES).astype(BF16)
    wukv = wkv[:, :, MLA_NOPE:].reshape(MLA_KV_RANK, MLA_HEADS * MLA_V).astype(BF16)
    slab_gain = lambda g: jnp.tile(jnp.pad(g, (0, pad)), MLA_HEADS)[None, :]
    qc, qsu, qsd = _rope_tables(MLA_QK ** -0.5)
    kc, ksu, ksd = _rope_tables(1.0)
    pkr = np.zeros((LANES, MLA_HEADS * LANES), np.float32)
    for hh in range(MLA_HEADS):
        pkr[np.arange(MLA_ROPE), hh * LANES + MLA_NOPE + np.arange(MLA_ROPE)] = 1.0
    seg128 = _seg_matrix(MLA_HEADS * LANES, LANES)
    seg64 = _seg_matrix(NSA_HEADS * NSA_HEAD_DIM, NSA_HEAD_DIM)
    dup = np.zeros((256, 512), np.float32)
    for s in range(4):
        for rep in range(2):
            dup[s * 64 + np.arange(64), s * 128 + rep * 64 + np.arange(64)] = 1.0
    nqg = (jnp.tile(nq_gain, NSA_HEADS) * (NSA_HEAD_DIM ** -0.5))[None, :]
    nkg = jnp.concatenate([nk_gain[1], nk_gain[1], nk_gain[2], nk_gain[2]])[None, :]

    rows = PROJ_ROWS
    n_pos = SEQ // rows
    full = lambda a: pl.BlockSpec(a.shape, lambda i: (0,) * a.ndim)
    tab = pl.BlockSpec((rows, LANES), lambda i: (i % n_pos, 0))
    consts = [ln1_g[None, :], wp, cq_g[None, :], ckv_g[None, :], wuq, wukk, wukv,
              slab_gain(q_gain), slab_gain(k_gain)]
    tabs = [qc, qsu, qsd, kc, ksu, ksd]
    mats = [jnp.asarray(pkr, BF16), jnp.asarray(seg128, BF16), jnp.asarray(seg128.T, BF16),
            jnp.asarray(seg64, BF16), jnp.asarray(seg64.T, BF16), nqg, nkg, jnp.asarray(dup, BF16)]
    row_spec = lambda w: pl.BlockSpec((rows, w), lambda i: (i, 0))
    out_w = [(1024, BF16), (1024, BF16), (512, BF16), (512, BF16), (512, BF16), (512, BF16),
             (256, F32), (128, F32)]
    return pl.pallas_call(
        _proj_kernel,
        grid=(t // rows,),
        in_specs=[row_spec(D_MODEL)] + [full(a) for a in consts] + [tab] * 6 + [full(a) for a in mats],
        out_specs=[row_spec(w) for w, _ in out_w],
        out_shape=[jax.ShapeDtypeStruct((t, w), dt) for w, dt in out_w],
        compiler_params=pltpu.CompilerParams(dimension_semantics=("parallel",),
                                             vmem_limit_bytes=VMEM_LIMIT),
        name="projections",
    )(x2d, *consts, *tabs, *mats)


def _compress_kernel(x_ref, pe_ref, w1_ref, w2_ref, kg_ref, dup_ref, out_ref):
    is_key = pl.program_id(1) < NSA_KV_HEADS
    x = x_ref[0, 0]
    pe = pe_ref[0]
    half = CMP_STRIDE * NSA_HEAD_DIM
    w1 = w1_ref[0]
    top = _dot((x + pe[:, :half]).astype(BF16), w1[:half])
    bot = _dot((x + pe[:, half:]).astype(BF16), w1[half:])
    hid = top + pltpu.roll(bot, N_CHUNK - 1, 0)
    act = hid * (1.0 / (1.0 + jnp.exp(-hid)))
    out = _dot(act.astype(BF16), w2_ref[0])
    out = jnp.where(is_key, _rms(out, NSA_HEAD_DIM) * kg_ref[...], out)
    out_ref[0, 0] = _dot(out.astype(BF16), dup_ref[...]).astype(BF16)


def _compress(kvc, cmp_pe, cmp_w1, cmp_w2, k_gain0):
    b = kvc.shape[0] // SEQ
    xc = kvc.reshape(b, SEQ, 4, NSA_HEAD_DIM).transpose(0, 2, 1, 3)
    xc = xc.reshape(b, 4, N_CHUNK, CMP_STRIDE * NSA_HEAD_DIM)
    pe = cmp_pe.reshape(2, 1, CMP_BLOCK * NSA_HEAD_DIM)
    dup = np.zeros((NSA_HEAD_DIM, LANES), np.float32)
    dup[np.arange(64), np.arange(64)] = 1.0
    dup[np.arange(64), 64 + np.arange(64)] = 1.0
    return pl.pallas_call(
        _compress_kernel,
        grid=(b, 4),
        in_specs=[pl.BlockSpec((1, 1, N_CHUNK, CMP_STRIDE * NSA_HEAD_DIM), lambda i, j: (i, j, 0, 0)),
                  pl.BlockSpec((1, 1, CMP_BLOCK * NSA_HEAD_DIM), lambda i, j: (j // 2, 0, 0)),
                  pl.BlockSpec((1, CMP_BLOCK * NSA_HEAD_DIM, CMP_HIDDEN), lambda i, j: (j // 2, 0, 0)),
                  pl.BlockSpec((1, CMP_HIDDEN, NSA_HEAD_DIM), lambda i, j: (j // 2, 0, 0)),
                  pl.BlockSpec((1, NSA_HEAD_DIM), lambda i, j: (0, 0)),
                  pl.BlockSpec((NSA_HEAD_DIM, LANES), lambda i, j: (0, 0))],
        out_specs=pl.BlockSpec((1, 1, N_CHUNK, LANES), lambda i, j: (i, j, 0, 0)),
        out_shape=jax.ShapeDtypeStruct((b, 4, N_CHUNK, LANES), BF16),
        name="compression",
    )(xc, pe, cmp_w1.astype(BF16), cmp_w2.astype(BF16), k_gain0[None, :], jnp.asarray(dup, BF16))


def _flash_init(m_ref, l_ref, acc_ref):
    m_ref[...] = jnp.full(m_ref.shape, NEG, F32)
    l_ref[...] = jnp.zeros(l_ref.shape, F32)
    acc_ref[...] = jnp.zeros(acc_ref.shape, F32)


def _flash_step(s, v, m_ref, l_ref, acc_ref):
    m_old = m_ref[...]
    m_new = jnp.maximum(m_old, jnp.max(s, axis=-1, keepdims=True))
    alpha = jnp.exp(m_old - m_new)
    p = jnp.exp(s - m_new)
    l_ref[...] = alpha * l_ref[...] + jnp.sum(p, axis=-1, keepdims=True)
    acc_ref[...] = alpha * acc_ref[...] + _dot(p.astype(BF16), v)
    m_ref[...] = m_new


def _mla_kernel(q_ref, k_ref, v_ref, o_ref, m_ref, l_ref, acc_ref):
    qt = pl.program_id(2)
    t = MLA_TILE
    rel = (lax.broadcasted_iota(jnp.int32, (t, t), 0) - lax.broadcasted_iota(jnp.int32, (t, t), 1))
    outs = []
    for hh in range(2):
        q = q_ref[:, hh * LANES:(hh + 1) * LANES]
        m, l, acc = m_ref.at[hh], l_ref.at[hh], acc_ref.at[hh]
        _flash_init(m, l, acc)

        def body(kt, carry, q=q, hh=hh, m=m, l=l, acc=acc):
            rows = pl.ds(pl.multiple_of(kt * t, t), t)
            s = _dot_nt(q, k_ref[rows, hh * LANES:(hh + 1) * LANES])
            s = jnp.where(rel >= (kt - qt) * t, s, NEG)
            _flash_step(s, v_ref[rows, :], m, l, acc)
            return carry

        lax.fori_loop(0, qt + 1, body, 0)
        outs.append(acc[...] / l[...])
    lane = lax.broadcasted_iota(jnp.int32, (t, LANES), 1)
    o_ref[...] = jnp.where(lane < MLA_V, outs[0], outs[1]).astype(BF16)


def _mla_attention(q, k, v):
    b = q.shape[0] // SEQ
    nq = SEQ // MLA_TILE
    return pl.pallas_call(
        _mla_kernel,
        grid=(b, MLA_HEADS // 2, nq),
        in_specs=[pl.BlockSpec((MLA_TILE, 2 * LANES), lambda i, hp, j: (i * nq + j, hp)),
                  pl.BlockSpec((SEQ, 2 * LANES), lambda i, hp, j: (i, hp)),
                  pl.BlockSpec((SEQ, LANES), lambda i, hp, j: (i, hp))],
        out_specs=pl.BlockSpec((MLA_TILE, LANES), lambda i, hp, j: (i * nq + j, hp)),
        out_shape=jax.ShapeDtypeStruct((q.shape[0], MLA_HEADS * MLA_V), BF16),
        scratch_shapes=[pltpu.VMEM((2, MLA_TILE, 1), F32), pltpu.VMEM((2, MLA_TILE, 1), F32),
                        pltpu.VMEM((2, MLA_TILE, LANES), F32)],
        compiler_params=pltpu.CompilerParams(
            dimension_semantics=("parallel", "parallel", "arbitrary"), vmem_limit_bytes=VMEM_LIMIT),
        name="mla_attention",
    )(q, k, v)


def _nsa_kernel(q_ref, ksel_ref, kwin_ref, vsel_ref, vwin_ref, kc_ref, vc_ref, gate_ref,
                tabc_ref, tabw_ref, ov_ref, expand_ref, gx_ref,
                o_ref, m_ref, l_ref, acc_ref, mb_ref):
    qt = pl.program_id(2)
    t = NSA_TILE
    rows4 = NSA_GROUP * t
    lane = lax.broadcasted_iota(jnp.int32, (t, LANES), 1)
    low = lane < NSA_HEAD_DIM
    q2 = q_ref[...]
    qa, qb = q2[:, :LANES], q2[:, LANES:]
    zero = jnp.zeros_like(qa)
    q4 = jnp.concatenate([jnp.where(low, qa, zero), jnp.where(low, zero, qa),
                          jnp.where(low, qb, zero), jnp.where(low, zero, qb)], axis=0)

    bias_c = tabc_ref[...].reshape(rows4, LANES)
    valid = bias_c > 0.5 * NEG
    sc = jnp.where(valid, _dot_nt(q4, kc_ref[0, 0]) + bias_c, NEG)
    mx = jnp.max(sc, axis=-1, keepdims=True)
    p = jnp.where(valid, jnp.exp(sc - mx), 0.0)
    pc = p / jnp.maximum(jnp.sum(p, axis=-1, keepdims=True), 1e-30)
    o_cmp = _dot(pc.astype(BF16), vc_ref[0, 0])

    psum = pc[0:t] + pc[t:2 * t] + pc[2 * t:3 * t] + pc[3 * t:4 * t]
    imp = _split_dot(psum, ov_ref[...], terms=3)
    jj = lax.broadcasted_iota(jnp.int32, (t, N_SEL), 1)
    blk = (qt * t + lax.broadcasted_iota(jnp.int32, (t, N_SEL), 0)) // SEL_BLOCK
    ok = jj <= blk
    forced = ok & ((jj == 0) | (jj == blk) | (jj == blk - 1))
    score = jnp.where(forced, FORCE_SCORE, jnp.where(ok, imp, -jnp.inf))
    rank = jnp.zeros((t, N_SEL), jnp.int32)
    for i in range(N_SEL):
        ci = score[:, i:i + 1]
        beats = (ci > score) | ((ci == score) & (jj > i))
        rank = rank + beats.astype(jnp.int32)
    chosen = ok & (rank < SEL_TOP_N)
    negsel = jnp.where(chosen, 0.0, NEG).astype(BF16)
    maskbias = _dot(negsel, expand_ref[...])
    for kt in range(SEQ // t):
        mb_ref[kt] = maskbias[:, kt * t:(kt + 1) * t]

    def run(k_ref, v_ref, first, use_mask):
        _flash_init(m_ref, l_ref, acc_ref)

        def body(kt, carry):
            rows = pl.ds(pl.multiple_of(kt * t, t), t)
            s = _dot_nt(q4, k_ref[rows, :]).reshape(NSA_GROUP, t, t)
            off = qt - kt
            if use_mask:
                off = jnp.minimum(off, 2)
            s = s + tabw_ref[0, off]
            if use_mask:
                s = s + mb_ref[kt][None]
            _flash_step(s.reshape(rows4, t), v_ref[rows, :], m_ref, l_ref, acc_ref)
            return carry

        lax.fori_loop(first, qt + 1, body, 0)
        return acc_ref[...] / l_ref[...]

    o_sel = run(ksel_ref, vsel_ref, 0, True)
    o_win = run(kwin_ref, vwin_ref, jnp.maximum(qt - (N_WIN_TILES - 1), 0), False)

    def heads_to_lanes(o4):
        a = jnp.where(low, o4[0:t], o4[t:2 * t])
        b = jnp.where(low, o4[2 * t:3 * t], o4[3 * t:4 * t])
        return jnp.concatenate([a, b], axis=1)

    g = _split_dot(gate_ref[...], gx_ref[0])
    w = NSA_GROUP * NSA_HEAD_DIM
    out = (g[:, :w] * heads_to_lanes(o_cmp) + g[:, w:2 * w] * heads_to_lanes(o_sel)
           + g[:, 2 * w:] * heads_to_lanes(o_win))
    o_ref[...] = out.astype(BF16)


def _nsa_attention(q, kdup, vdup, ckv, gates, tables):
    b = q.shape[0] // SEQ
    t = NSA_TILE
    nq = SEQ // t
    g_ = NSA_KV_HEADS
    tabc = tables[:, :SEQ]
    tabw = tables[:, SEQ:].reshape(g_, NSA_GROUP, N_WIN_TILES, t, t).transpose(0, 2, 1, 3, 4)
    c0 = np.arange(LANES)[:, None] * CMP_STRIDE
    j0 = np.arange(N_SEL)[None, :] * SEL_BLOCK
    ov = np.clip(np.minimum(c0 + CMP_BLOCK, j0 + SEL_BLOCK) - np.maximum(c0, j0), 0, None) / CMP_BLOCK
    ov[N_CHUNK - 1:] = 0.0
    expand = (np.arange(SEQ)[None, :] // SEL_BLOCK == np.arange(N_SEL)[:, None]).astype(np.float32)
    gx = np.zeros((g_, LANES, N_BRANCH * 256), np.float32)
    for gg in range(g_):
        for br in range(N_BRANCH):
            for r in range(NSA_GROUP):
                src = MLA_ROPE + br * NSA_HEADS + gg * NSA_GROUP + r
                gx[gg, src, br * 256 + r * 64: br * 256 + (r + 1) * 64] = 1.0
    kv_spec = lambda off: pl.BlockSpec((SEQ, LANES), lambda i, g, j: (i, off + g))
    return pl.pallas_call(
        _nsa_kernel,
        grid=(b, g_, nq),
        in_specs=[pl.BlockSpec((t, 2 * LANES), lambda i, g, j: (i * nq + j, g)),
                  kv_spec(0), kv_spec(2), kv_spec(0), kv_spec(2),
                  pl.BlockSpec((1, 1, N_CHUNK, LANES), lambda i, g, j: (i, g, 0, 0)),
                  pl.BlockSpec((1, 1, N_CHUNK, LANES), lambda i, g, j: (i, 2 + g, 0, 0)),
                  pl.BlockSpec((t, LANES), lambda i, g, j: (i * nq + j, 0)),
                  pl.BlockSpec((NSA_GROUP, t, LANES), lambda i, g, j: (g, j, 0)),
                  pl.BlockSpec((1, N_WIN_TILES, NSA_GROUP, t, t), lambda i, g, j: (g, 0, 0, 0, 0)),
                  pl.BlockSpec((LANES, N_SEL), lambda i, g, j: (0, 0)),
                  pl.BlockSpec((N_SEL, SEQ), lambda i, g, j: (0, 0)),
                  pl.BlockSpec((1, LANES, N_BRANCH * 256), lambda i, g, j: (g, 0, 0))],
        out_specs=pl.BlockSpec((t, 2 * LANES), lambda i, g, j: (i * nq + j, g)),
        out_shape=jax.ShapeDtypeStruct((q.shape[0], NSA_HEADS * NSA_HEAD_DIM), BF16),
        scratch_shapes=[pltpu.VMEM((NSA_GROUP * t, 1), F32), pltpu.VMEM((NSA_GROUP * t, 1), F32),
                        pltpu.VMEM((NSA_GROUP * t, LANES), F32), pltpu.VMEM((SEQ // t, t, t), F32)],
        compiler_params=pltpu.CompilerParams(
            dimension_semantics=("parallel", "parallel", "arbitrary"), vmem_limit_bytes=VMEM_LIMIT),
        name="nsa_attention",
    )(q, kdup, kdup, vdup, vdup, ckv, ckv, gates, tabc, tabw,
      jnp.asarray(ov, BF16), jnp.asarray(expand, BF16), jnp.asarray(gx, BF16))


def _tail_kernel(x_ref, oa_ref, ob_ref, ga_ref, gb_ref, woa_ref, wob_ref, ln2_ref, wup_ref, wdn_ref,
                 out_ref, h2_ref):
    width = MLA_HEADS * MLA_V
    a = (_rms(oa_ref[...].astype(F32), width) * ga_ref[...]).astype(BF16)
    b = (_rms(ob_ref[...].astype(F32), width) * gb_ref[...]).astype(BF16)
    x1 = x_ref[...] + _dot(a, woa_ref[...]) + _dot(b, wob_ref[...])
    h2_ref[...] = (_rms(x1, D_MODEL) * ln2_ref[...]).astype(BF16)
    out_ref[...] = x1

    def body(f, carry):
        u = jnp.maximum(_dot(h2_ref[...], wup_ref[f]), 0.0)
        out_ref[...] += _dot((u * u).astype(BF16), wdn_ref[f])
        return carry

    lax.fori_loop(0, D_FF // FF_CHUNK, body, 0)


def _tail(x2d, o_a, o_b, gn_a, gn_b, w_o, ln2_g, w_up, w_down):
    t = x2d.shape[0]
    rows = TAIL_ROWS
    width = MLA_HEADS * MLA_V
    once = lambda a: pl.BlockSpec(a.shape, lambda i: (0,) * a.ndim)
    nf = D_FF // FF_CHUNK
    wup = w_up.astype(BF16).reshape(D_MODEL, nf, FF_CHUNK).transpose(1, 0, 2)
    wdn = w_down.astype(BF16).reshape(nf, FF_CHUNK, D_MODEL)
    consts = [gn_a[None, :], gn_b[None, :], w_o[:width].astype(BF16), w_o[width:].astype(BF16),
              ln2_g[None, :], wup, wdn]
    row_spec = lambda w: pl.BlockSpec((rows, w), lambda i: (i, 0))
    return pl.pallas_call(
        _tail_kernel,
        grid=(t // rows,),
        in_specs=[row_spec(D_MODEL), row_spec(width), row_spec(width)] + [once(a) for a in consts],
        out_specs=row_spec(D_MODEL),
        out_shape=jax.ShapeDtypeStruct((t, D_MODEL), F32),
        scratch_shapes=[pltpu.VMEM((rows, D_MODEL), BF16)],
        compiler_params=pltpu.CompilerParams(dimension_semantics=("parallel",),
                                             vmem_limit_bytes=VMEM_LIMIT),
        name="tail",
    )(x2d, o_a, o_b, *consts)


def kernel(x, ln1_g, w_in, mla_cq_norm_g, mla_ckv_norm_g, mla_w_uq, mla_w_ukv, mla_q_gain, mla_k_gain,
           nsa_q_gain, nsa_k_gain, nsa_cmp_pe, nsa_cmp_w1, nsa_cmp_w2, rel_bias, grp_norm_mla,
           grp_norm_nsa, w_o, ln2_g, w_up, w_down):
    b, s, d = x.shape
    assert (s, d) == (SEQ, D_MODEL) and ln1_g.shape[0] == 1
    x2d = x.reshape(b * s, d)
    tables = _bias_tables(rel_bias)
    q_mla, k_mla, v_mla, q_nsa, kdup, vdup, kvc, gates = _projections(
        x2d, ln1_g[0], w_in[0], mla_cq_norm_g[0], mla_ckv_norm_g[0], mla_w_uq[0], mla_w_ukv[0],
        mla_q_gain[0], mla_k_gain[0], nsa_q_gain[0], nsa_k_gain[0])
    ckv = _compress(kvc, nsa_cmp_pe[0], nsa_cmp_w1[0], nsa_cmp_w2[0], nsa_k_gain[0, 0])
    o_a = _mla_attention(q_mla, k_mla, v_mla)
    o_b = _nsa_attention(q_nsa, kdup, vdup, ckv, gates, tables)
    out = _tail(x2d, o_a, o_b, grp_norm_mla[0], grp_norm_nsa[0], w_o[0], ln2_g[0], w_up[0], w_down[0])
    return out.reshape(b, s, d)
```

```python
import math

import numpy as np
import jax
import jax.numpy as jnp
from jax import lax
from jax.experimental import pallas as pl
from jax.experimental.pallas import tpu as pltpu

F32 = jnp.float32
BF16 = jnp.bfloat16

D_MODEL = 1024
SEQ = 2048
MLA_HEADS = 8
MLA_NOPE = 64
MLA_ROPE = 32
MLA_V = 64
MLA_QK = MLA_NOPE + MLA_ROPE
MLA_Q_RANK = 384
MLA_KV_RANK = 256
ROPE_THETA = 10000.0
NSA_HEADS = 8
NSA_KV_HEADS = 2
NSA_GROUP = NSA_HEADS // NSA_KV_HEADS
NSA_HEAD_DIM = 64
N_BRANCH = 3
CMP_BLOCK = 32
CMP_STRIDE = 16
CMP_HIDDEN = 128
SEL_BLOCK = 64
SEL_TOP_N = 16
WINDOW = 512
FORCE_SCORE = 1e4
REL_BUCKETS = 32
REL_MAX_DIST = 128
D_FF = 4 * D_MODEL
EPS = 1e-6

LANES = 128
NEG = -1e30
N_CHUNK = SEQ // CMP_STRIDE
N_SEL = SEQ // SEL_BLOCK
NSA_Q = 128
NSA_K = 256
N_BIAS_TILES = (WINDOW + NSA_K - NSA_Q) // NSA_Q + 1
SEL_FAR_TILE = (REL_MAX_DIST + NSA_K) // NSA_Q
MLA_Q = 256
MLA_K = 512
V_TILE = 256
PROJ_ROWS = 256
TAIL_ROWS = 512
FF_CHUNK = 1024
KEY_CHUNK = 128
VMEM_LIMIT = 56 * 1024 * 1024

assert PROJ_ROWS == V_TILE and MLA_K % V_TILE == 0 and NSA_K == V_TILE


def _dot(a, b):
    return jnp.dot(a, b, preferred_element_type=F32)


def _dot_nt(a, b):
    return lax.dot_general(a, b, (((1,), (1,)), ((), ())), preferred_element_type=F32)


def _split(a, terms):
    pieces = []
    rem = a
    for _ in range(terms):
        piece = rem.astype(BF16)
        pieces.append(piece)
        rem = rem - piece.astype(F32)
    return pieces


def _split_dot(a, b, terms=2):
    return sum(_dot(p, b) for p in _split(a, terms))


def _rms(x, width):
    return x * lax.rsqrt(jnp.sum(x * x, axis=-1, keepdims=True) * (1.0 / width) + EPS)


def _t5_bucket_np(dist):
    n = np.maximum(dist, 0)
    max_exact = REL_BUCKETS // 2
    large = max_exact + (np.log(np.maximum(n, 1).astype(np.float32) / max_exact)
                         / math.log(REL_MAX_DIST / max_exact)
                         * (REL_BUCKETS - max_exact)).astype(np.int32)
    large = np.minimum(large, REL_BUCKETS - 1)
    return np.where(n < max_exact, n, large).astype(np.int32)


def _bias_index_table():
    i = np.arange(NSA_Q)[None, :]
    parts = []
    c = np.arange(N_CHUNK)[:, None]
    for qt in range(SEQ // NSA_Q):
        dist_c = qt * NSA_Q + i - (c * CMP_STRIDE + CMP_BLOCK - 1)
        parts.append(np.where((dist_c >= 0) & (c < N_CHUNK - 1), _t5_bucket_np(dist_c), -1))
    j = np.arange(NSA_K)[:, None]
    for off in range(N_BIAS_TILES):
        d = off * NSA_Q + i - j
        parts.append(np.where((d >= 0) & (d < WINDOW), _t5_bucket_np(d), -1))
    return np.concatenate(parts, axis=0).astype(np.int32)


def _bias_table_kernel(rb_ref, idx_ref, out_ref):
    h = pl.program_id(0)
    idx = idx_ref[...]
    acc = jnp.full(idx.shape, NEG, F32)
    for b in range(REL_BUCKETS):
        acc = jnp.where(idx == b, rb_ref[h, b], acc)
    out_ref[0] = acc


def _bias_tables(rel_bias):
    idx = jnp.asarray(_bias_index_table())
    rows = idx.shape[0]
    blk = 512
    return pl.pallas_call(
        _bias_table_kernel,
        grid=(NSA_HEADS, rows // blk),
        in_specs=[pl.BlockSpec(memory_space=pltpu.SMEM),
                  pl.BlockSpec((blk, LANES), lambda h, r: (r, 0))],
        out_specs=pl.BlockSpec((1, blk, LANES), lambda h, r: (h, r, 0)),
        out_shape=jax.ShapeDtypeStruct((NSA_HEADS, rows, LANES), F32),
        name="bias_tables",
    )(rel_bias.T, idx)


def _rope_tables(scale):
    half = MLA_ROPE // 2
    inv = 1.0 / (ROPE_THETA ** (jnp.arange(half, dtype=F32) / half))
    ang = jnp.arange(SEQ, dtype=F32)[:, None] * inv[None, :]
    cos, sin = jnp.cos(ang), jnp.sin(ang)
    ones = jnp.ones((SEQ, MLA_NOPE), F32)
    z16 = jnp.zeros((SEQ, half), F32)
    z32 = jnp.zeros((SEQ, LANES - MLA_QK), F32)
    z64 = jnp.zeros((SEQ, MLA_NOPE), F32)
    c = jnp.concatenate([ones, cos, cos, z32], axis=1) * scale
    s_up = jnp.concatenate([z64, z16, sin, z32], axis=1) * scale
    s_dn = jnp.concatenate([z64, -sin, z16, z32], axis=1) * scale
    return c, s_up, s_dn


def _rope_slabs(x, c, s_up, s_dn):
    half = MLA_ROPE // 2
    outs = []
    for h in range(MLA_HEADS):
        sl = x[:, h * LANES:(h + 1) * LANES]
        outs.append(sl * c + pltpu.roll(sl, half, 1) * s_up + pltpu.roll(sl, LANES - half, 1) * s_dn)
    return jnp.concatenate(outs, axis=1)


def _proj_kernel(x_ref, ln1_ref, wp_ref, cqg_ref, ckvg_ref, wuq_ref, wukk_ref, wukvt_ref,
                 qg_ref, kg_ref, qc_ref, qsu_ref, qsd_ref, kc_ref, ksu_ref, ksd_ref,
                 pkr_ref, seg128_ref, seg128t_ref, seg64_ref, seg64t_ref,
                 nq_gain_ref, nk_gain_ref, dup_ref, eye_ref,
                 qmla_ref, kmla_ref, vmla_ref, qnsa_ref, kdup_ref, vnsa_ref, kvc_ref, gate_ref):
    x = x_ref[...]
    h = (_rms(x, D_MODEL) * ln1_ref[...]).astype(BF16)
    proj = _dot(h, wp_ref[...])

    def head_norm(v, seg_ref, segt_ref, width):
        ss = _split_dot(v * v, seg_ref[...])
        rs = lax.rsqrt(ss * (1.0 / width) + EPS)
        return v * _split_dot(rs, segt_ref[...])

    cq = (_rms(proj[:, :MLA_Q_RANK], MLA_Q_RANK) * cqg_ref[...]).astype(BF16)
    q = head_norm(_dot(cq, wuq_ref[...]), seg128_ref, seg128t_ref, MLA_QK) * qg_ref[...]
    qmla_ref[...] = _rope_slabs(q, qc_ref[...], qsu_ref[...], qsd_ref[...]).astype(BF16)

    o_kv = MLA_Q_RANK
    ckv = (_rms(proj[:, o_kv:o_kv + MLA_KV_RANK], MLA_KV_RANK) * ckvg_ref[...]).astype(BF16)
    misc = proj[:, 640:768]
    k = _dot(ckv, wukk_ref[...]) + _split_dot(misc, pkr_ref[...])
    k = head_norm(k, seg128_ref, seg128t_ref, MLA_QK) * kg_ref[...]
    kmla_ref[...] = _rope_slabs(k, kc_ref[...], ksu_ref[...], ksd_ref[...]).astype(BF16)
    vmla_ref[0] = _dot_nt(wukvt_ref[...], ckv).astype(BF16)

    qb = head_norm(proj[:, 768:1280], seg64_ref, seg64t_ref, NSA_HEAD_DIM)
    qnsa_ref[...] = (qb * nq_gain_ref[...]).astype(BF16)

    kk = proj[:, 1280:1536]
    ss = _split_dot(kk * kk, seg64_ref[:256, :])
    rs = lax.rsqrt(ss * (1.0 / NSA_HEAD_DIM) + EPS)
    kn = kk * _split_dot(rs, seg64t_ref[:, :256]) * nk_gain_ref[...]
    kdup_ref[...] = _dot(kn.astype(BF16), dup_ref[...]).astype(BF16)
    vnsa_ref[0] = _dot_nt(eye_ref[...], proj[:, 1536:1792].astype(BF16)).astype(BF16)
    kvc_ref[...] = proj[:, 1792:2048]
    gate_ref[...] = 1.0 / (1.0 + jnp.exp(-misc))


def _seg_matrix(width, seg, cols=LANES):
    m = np.zeros((width, cols), np.float32)
    m[np.arange(width), np.arange(width) // seg] = 1.0
    return m


def _projections(x2d, ln1_g, w_in, cq_g, ckv_g, w_uq, w_ukv, q_gain, k_gain, nq_gain, nk_gain):
    t = x2d.shape[0]
    kv0 = 1184
    seg = lambda a, b: w_in[:, a:b]
    kvb = lambda br, kv, g: seg(kv0 + ((br * 2 + kv) * 2 + g) * 64, kv0 + ((br * 2 + kv) * 2 + g) * 64 + 64)
    wp = jnp.concatenate([
        seg(0, 640),
        seg(640, 672), seg(1952, 1976), jnp.zeros((D_MODEL, 72), F32),
        seg(672, 1184),
        kvb(1, 0, 0), kvb(1, 0, 1), kvb(2, 0, 0), kvb(2, 0, 1),
        kvb(1, 1, 0), kvb(1, 1, 1), kvb(2, 1, 0), kvb(2, 1, 1),
        kvb(0, 0, 0), kvb(0, 0, 1), kvb(0, 1, 0), kvb(0, 1, 1),
    ], axis=1).astype(BF16)
    pad = LANES - MLA_QK
    wuq = jnp.pad(w_uq.reshape(MLA_Q_RANK, MLA_HEADS, MLA_QK), ((0, 0), (0, 0), (0, pad)))
    wuq = wuq.reshape(MLA_Q_RANK, MLA_HEADS * LANES).astype(BF16)
    wkv = w_ukv.reshape(MLA_KV_RANK, MLA_HEADS, MLA_NOPE + MLA_V)
    wukk = jnp.pad(wkv[:, :, :MLA_NOPE], ((0, 0), (0, 0), (0, LANES - MLA_NOPE)))
    wukk = wukk.reshape(MLA_KV_RANK, MLA_HEADS * LANES).astype(BF16)
    wukvt = wkv[:, :, MLA_NOPE:].reshape(MLA_KV_RANK, MLA_HEADS * MLA_V).T.astype(BF16)
    slab_gain = lambda g: jnp.tile(jnp.pad(g, (0, pad)), MLA_HEADS)[None, :]
    qc, qsu, qsd = _rope_tables(MLA_QK ** -0.5)
    kc, ksu, ksd = _rope_tables(1.0)
    pkr = np.zeros((LANES, MLA_HEADS * LANES), np.float32)
    for hh in range(MLA_HEADS):
        pkr[np.arange(MLA_ROPE), hh * LANES + MLA_NOPE + np.arange(MLA_ROPE)] = 1.0
    seg128 = _seg_matrix(MLA_HEADS * LANES, LANES)
    seg64 = _seg_matrix(NSA_HEADS * NSA_HEAD_DIM, NSA_HEAD_DIM)
    dup = np.zeros((256, 512), np.float32)
    for s in range(4):
        for rep in range(2):
            dup[s * 64 + np.arange(64), s * 128 + rep * 64 + np.arange(64)] = 1.0
    nqg = (jnp.tile(nq_gain, NSA_HEADS) * (NSA_HEAD_DIM ** -0.5))[None, :]
    nkg = jnp.concatenate([nk_gain[1], nk_gain[1], nk_gain[2], nk_gain[2]])[None, :]

    rows = PROJ_ROWS
    n_pos = SEQ // rows
    full = lambda a: pl.BlockSpec(a.shape, lambda i: (0,) * a.ndim)
    tab = pl.BlockSpec((rows, LANES), lambda i: (i % n_pos, 0))
    consts = [ln1_g[None, :], wp, cq_g[None, :], ckv_g[None, :], wuq, wukk, wukvt,
              slab_gain(q_gain), slab_gain(k_gain)]
    tabs = [qc, qsu, qsd, kc, ksu, ksd]
    mats = [jnp.asarray(pkr, BF16), jnp.asarray(seg128, BF16), jnp.asarray(seg128.T, BF16),
            jnp.asarray(seg64, BF16), jnp.asarray(seg64.T, BF16), nqg, nkg, jnp.asarray(dup, BF16),
            jnp.eye(256, dtype=BF16)]
    row_spec = lambda w: pl.BlockSpec((rows, w), lambda i: (i, 0))
    vt_spec = lambda w: pl.BlockSpec((1, w, rows), lambda i: (i, 0, 0))
    sds = jax.ShapeDtypeStruct
    return pl.pallas_call(
        _proj_kernel,
        grid=(t // rows,),
        in_specs=[row_spec(D_MODEL)] + [full(a) for a in consts] + [tab] * 6 + [full(a) for a in mats],
        out_specs=[row_spec(1024), row_spec(1024), vt_spec(512), row_spec(512), row_spec(512),
                   vt_spec(256), row_spec(256), row_spec(128)],
        out_shape=[sds((t, 1024), BF16), sds((t, 1024), BF16), sds((t // rows, 512, rows), BF16),
                   sds((t, 512), BF16), sds((t, 512), BF16), sds((t // rows, 256, rows), BF16),
                   sds((t, 256), F32), sds((t, 128), F32)],
        compiler_params=pltpu.CompilerParams(dimension_semantics=("parallel",),
                                             vmem_limit_bytes=VMEM_LIMIT),
        name="projections",
    )(x2d, *consts, *tabs, *mats)


def _compress_kernel(xk_ref, xv_ref, pe_ref, w1_ref, w2k_ref, w2vt_ref, kg_ref, dup_ref, kc_ref, vct_ref):
    half = CMP_STRIDE * NSA_HEAD_DIM

    def hidden(x, which):
        pe = pe_ref[which]
        w1 = w1_ref[which]
        top = _dot((x + pe[:, :half]).astype(BF16), w1[:half])
        bot = _dot((x + pe[:, half:]).astype(BF16), w1[half:])
        hid = top + pltpu.roll(bot, N_CHUNK - 1, 0)
        return (hid * (1.0 / (1.0 + jnp.exp(-hid)))).astype(BF16)

    k = _dot(hidden(xk_ref[0, 0], 0), w2k_ref[...])
    k = _rms(k, NSA_HEAD_DIM) * kg_ref[...]
    kc_ref[0, 0] = _dot(k.astype(BF16), dup_ref[...]).astype(BF16)
    vct_ref[0, 0] = _dot_nt(w2vt_ref[...], hidden(xv_ref[0, 0], 1)).astype(BF16)


def _compress(kvc, cmp_pe, cmp_w1, cmp_w2, k_gain0):
    b = kvc.shape[0] // SEQ
    g_ = NSA_KV_HEADS
    xc = kvc.reshape(b, SEQ, 4, NSA_HEAD_DIM).transpose(0, 2, 1, 3)
    xc = xc.reshape(b, 4, N_CHUNK, CMP_STRIDE * NSA_HEAD_DIM)
    pe = cmp_pe.reshape(2, 1, CMP_BLOCK * NSA_HEAD_DIM)
    dup = np.zeros((NSA_HEAD_DIM, LANES), np.float32)
    dup[np.arange(64), np.arange(64)] = 1.0
    dup[np.arange(64), 64 + np.arange(64)] = 1.0
    full = lambda a: pl.BlockSpec(a.shape, lambda i, g: (0,) * a.ndim)
    consts = [pe, cmp_w1.astype(BF16), cmp_w2[0].astype(BF16), cmp_w2[1].T.astype(BF16),
              k_gain0[None, :], jnp.asarray(dup, BF16)]
    x_spec = lambda off: pl.BlockSpec((1, 1, N_CHUNK, CMP_STRIDE * NSA_HEAD_DIM),
                                      lambda i, g: (i, off + g, 0, 0))
    return pl.pallas_call(
        _compress_kernel,
        grid=(b, g_),
        in_specs=[x_spec(0), x_spec(g_)] + [full(a) for a in consts],
        out_specs=[pl.BlockSpec((1, 1, N_CHUNK, LANES), lambda i, g: (i, g, 0, 0)),
                   pl.BlockSpec((1, 1, NSA_HEAD_DIM, N_CHUNK), lambda i, g: (i, g, 0, 0))],
        out_shape=[jax.ShapeDtypeStruct((b, g_, N_CHUNK, LANES), BF16),
                   jax.ShapeDtypeStruct((b, g_, NSA_HEAD_DIM, N_CHUNK), BF16)],
        name="compression",
    )(xc, xc, *consts)


def _flash_init(m_ref, l_ref, acc_ref):
    m_ref[...] = jnp.full(m_ref.shape, NEG, F32)
    l_ref[...] = jnp.zeros(l_ref.shape, F32)
    acc_ref[...] = jnp.zeros(acc_ref.shape, F32)


def _flash_step(s, v, m_ref, l_ref, acc_ref):
    for c in range(s.shape[0] // KEY_CHUNK):
        sc = s[c * KEY_CHUNK:(c + 1) * KEY_CHUNK]
        m_old = m_ref[c]
        m_new = jnp.maximum(m_old, jnp.max(sc, axis=0, keepdims=True))
        alpha = jnp.exp(m_old - m_new)
        p = jnp.exp(sc - m_new)
        l_ref[c] = alpha * l_ref[c] + jnp.sum(p, axis=0, keepdims=True)
        acc_ref[c] = alpha * acc_ref[c] + _dot(v[:, c * KEY_CHUNK:(c + 1) * KEY_CHUNK], p.astype(BF16))
        m_ref[c] = m_new


def _flash_finish(m_ref, l_ref, acc_ref):
    n = m_ref.shape[0]
    m = m_ref[0]
    for c in range(1, n):
        m = jnp.maximum(m, m_ref[c])
    w = [jnp.exp(m_ref[c] - m) for c in range(n)]
    l = sum(w[c] * l_ref[c] for c in range(n))
    acc = sum(w[c] * acc_ref[c] for c in range(n))
    return acc / l


def _mla_kernel(q_ref, k_ref, vt_ref, eye_ref, o_ref, m_ref, l_ref, acc_ref):
    qt = pl.program_id(2)
    sub = MLA_K // V_TILE
    rel = (lax.broadcasted_iota(jnp.int32, (MLA_K, MLA_Q), 1)
           - lax.broadcasted_iota(jnp.int32, (MLA_K, MLA_Q), 0))
    q = [q_ref[:, hh * LANES:(hh + 1) * LANES] for hh in range(2)]
    for hh in range(2):
        _flash_init(m_ref.at[hh], l_ref.at[hh], acc_ref.at[hh])

    def step(kt, masked):
        rows = pl.ds(pl.multiple_of(kt * MLA_K, MLA_K), MLA_K)
        for hh in range(2):
            s = _dot_nt(k_ref[rows, hh * LANES:(hh + 1) * LANES], q[hh])
            if masked:
                s = jnp.where(rel >= kt * MLA_K - qt * MLA_Q, s, NEG)
            v = jnp.concatenate([vt_ref[kt * sub + i, hh * MLA_V:(hh + 1) * MLA_V, :] for i in range(sub)],
                                axis=1)
            _flash_step(s, v, m_ref.at[hh], l_ref.at[hh], acc_ref.at[hh])

    n_full = (qt * MLA_Q) // MLA_K
    lax.fori_loop(0, n_full, lambda kt, c: (step(kt, False), c)[1], 0)
    step(n_full, True)
    o_t = jnp.concatenate([_flash_finish(m_ref.at[hh], l_ref.at[hh], acc_ref.at[hh]) for hh in range(2)],
                          axis=0)
    o_ref[...] = _dot_nt(eye_ref[...], o_t.astype(BF16)).astype(BF16)


def _mla_attention(q, k, vt):
    b = q.shape[0] // SEQ
    nq = SEQ // MLA_Q
    nv = SEQ // V_TILE
    return pl.pallas_call(
        _mla_kernel,
        grid=(b, MLA_HEADS // 2, nq),
        in_specs=[pl.BlockSpec((MLA_Q, 2 * LANES), lambda i, hp, j: (i * nq + j, hp)),
                  pl.BlockSpec((SEQ, 2 * LANES), lambda i, hp, j: (i, hp)),
                  pl.BlockSpec((nv, 2 * MLA_V, V_TILE), lambda i, hp, j: (i, hp, 0)),
                  pl.BlockSpec((MLA_Q, MLA_Q), lambda i, hp, j: (0, 0))],
        out_specs=pl.BlockSpec((MLA_Q, LANES), lambda i, hp, j: (i * nq + j, hp)),
        out_shape=jax.ShapeDtypeStruct((q.shape[0], MLA_HEADS * MLA_V), BF16),
        scratch_shapes=[pltpu.VMEM((2, MLA_K // KEY_CHUNK, 1, MLA_Q), F32),
                        pltpu.VMEM((2, MLA_K // KEY_CHUNK, 1, MLA_Q), F32),
                        pltpu.VMEM((2, MLA_K // KEY_CHUNK, MLA_V, MLA_Q), F32)],
        compiler_params=pltpu.CompilerParams(
            dimension_semantics=("parallel", "parallel", "arbitrary"), vmem_limit_bytes=VMEM_LIMIT),
        name="mla_attention",
    )(q, k, vt, jnp.eye(MLA_Q, dtype=BF16))


def _nsa_kernel(q_ref, ksel_ref, kwin_ref, vsel_ref, vwin_ref, kc_ref, vct_ref, gate_ref,
                tabc_ref, tabw_ref, ovt_ref, expand_ref, gx_ref, eye_ref,
                o_ref, m_ref, l_ref, acc_ref, mb_ref):
    qt = pl.program_id(2)
    t = NSA_Q
    heads = range(NSA_GROUP)
    lane = lax.broadcasted_iota(jnp.int32, (t, LANES), 1)
    low = lane < NSA_HEAD_DIM
    q2 = q_ref[...]
    qa, qb = q2[:, :LANES], q2[:, LANES:]
    zero = jnp.zeros_like(qa)
    q4 = jnp.concatenate([jnp.where(low, qa, zero), jnp.where(low, zero, qa),
                          jnp.where(low, qb, zero), jnp.where(low, zero, qb)], axis=0)

    bias_c = jnp.concatenate([tabc_ref[r] for r in heads], axis=1)
    valid = bias_c > 0.5 * NEG
    sc = jnp.where(valid, _dot_nt(kc_ref[0, 0], q4) + bias_c, NEG)
    mx = jnp.max(sc, axis=0, keepdims=True)
    p = jnp.where(valid, jnp.exp(sc - mx), 0.0)
    pc = p / jnp.maximum(jnp.sum(p, axis=0, keepdims=True), 1e-30)
    o_cmp = _dot(vct_ref[0, 0], pc.astype(BF16))

    psum = sum(pc[:, r * t:(r + 1) * t] for r in heads)
    imp = sum(_dot(ovt_ref[...], piece) for piece in _split(psum, 3))
    jj = lax.broadcasted_iota(jnp.int32, (N_SEL, t), 0)
    blk = (qt * t + lax.broadcasted_iota(jnp.int32, (N_SEL, t), 1)) // SEL_BLOCK
    ok = jj <= blk
    forced = ok & ((jj == 0) | (jj == blk) | (jj == blk - 1))
    score = jnp.where(forced, FORCE_SCORE, jnp.where(ok, imp, -jnp.inf))
    rank = jnp.zeros((N_SEL, t), jnp.int32)
    for i in range(N_SEL):
        ci = score[i:i + 1, :]
        beats = (ci > score) | ((ci == score) & (jj > i))
        rank = rank + beats.astype(jnp.int32)
    chosen = ok & (rank < SEL_TOP_N)
    negsel = jnp.where(chosen, 0.0, NEG).astype(BF16)
    mb_ref[...] = _dot(expand_ref[...], negsel)

    def run(k_ref, v_ref, first, selected):
        _flash_init(m_ref, l_ref, acc_ref)

        def body(kt, carry):
            rows = pl.ds(pl.multiple_of(kt * NSA_K, NSA_K), NSA_K)
            s = _dot_nt(k_ref[rows, :], q4)
            off = qt - kt * (NSA_K // NSA_Q)
            if selected:
                off = jnp.minimum(off, SEL_FAR_TILE)
            s = s + jnp.concatenate([tabw_ref[r, off] for r in heads], axis=1)
            if selected:
                mbt = mb_ref[rows, :]
                s = s + jnp.concatenate([mbt] * NSA_GROUP, axis=1)
            _flash_step(s, v_ref[kt], m_ref, l_ref, acc_ref)
            return carry

        lax.fori_loop(first, (qt * NSA_Q) // NSA_K + 1, body, 0)
        return _flash_finish(m_ref, l_ref, acc_ref)

    o_sel = run(ksel_ref, vsel_ref, 0, True)
    o_win = run(kwin_ref, vwin_ref, jnp.maximum(qt * NSA_Q - (WINDOW - 1), 0) // NSA_K, False)

    stacked = jnp.concatenate([o[:, r * t:(r + 1) * t] for o in (o_cmp, o_sel, o_win) for r in heads], axis=0)
    tok = _dot_nt(eye_ref[...], stacked.astype(BF16))
    g = _split_dot(gate_ref[...], gx_ref[0])
    w = NSA_GROUP * NSA_HEAD_DIM
    out = g[:, :w] * tok[:, :w] + g[:, w:2 * w] * tok[:, w:2 * w] + g[:, 2 * w:] * tok[:, 2 * w:]
    o_ref[...] = out.astype(BF16)


def _nsa_attention(q, kdup, vt, kc, vct, gates, tables):
    b = q.shape[0] // SEQ
    t = NSA_Q
    nq = SEQ // t
    nv = SEQ // V_TILE
    g_ = NSA_KV_HEADS
    tabw = tables[:, SEQ:].reshape(NSA_HEADS, N_BIAS_TILES, NSA_K, LANES)
    c0 = np.arange(N_CHUNK)[None, :] * CMP_STRIDE
    j0 = np.arange(N_SEL)[:, None] * SEL_BLOCK
    ovt = np.clip(np.minimum(c0 + CMP_BLOCK, j0 + SEL_BLOCK) - np.maximum(c0, j0), 0, None) / CMP_BLOCK
    ovt[:, N_CHUNK - 1:] = 0.0
    expand = (np.arange(SEQ)[:, None] // SEL_BLOCK == np.arange(N_SEL)[None, :]).astype(np.float32)
    gx = np.zeros((g_, LANES, N_BRANCH * 256), np.float32)
    for gg in range(g_):
        for br in range(N_BRANCH):
            for r in range(NSA_GROUP):
                src = MLA_ROPE + br * NSA_HEADS + gg * NSA_GROUP + r
                gx[gg, src, br * 256 + r * 64: br * 256 + (r + 1) * 64] = 1.0
    k_spec = lambda off: pl.BlockSpec((SEQ, LANES), lambda i, g, j: (i, off + g))
    v_spec = lambda off: pl.BlockSpec((nv, NSA_HEAD_DIM, V_TILE), lambda i, g, j: (i, off + g, 0))
    return pl.pallas_call(
        _nsa_kernel,
        grid=(b, g_, nq),
        in_specs=[pl.BlockSpec((t, 2 * LANES), lambda i, g, j: (i * nq + j, g)),
                  k_spec(0), k_spec(2), v_spec(0), v_spec(2),
                  pl.BlockSpec((1, 1, N_CHUNK, LANES), lambda i, g, j: (i, g, 0, 0)),
                  pl.BlockSpec((1, 1, NSA_HEAD_DIM, N_CHUNK), lambda i, g, j: (i, g, 0, 0)),
                  pl.BlockSpec((t, LANES), lambda i, g, j: (i * nq + j, 0)),
                  pl.BlockSpec((NSA_GROUP, N_CHUNK, LANES), lambda i, g, j: (g, j, 0)),
                  pl.BlockSpec((NSA_GROUP, N_BIAS_TILES, NSA_K, LANES), lambda i, g, j: (g, 0, 0, 0)),
                  pl.BlockSpec((N_SEL, N_CHUNK), lambda i, g, j: (0, 0)),
                  pl.BlockSpec((SEQ, N_SEL), lambda i, g, j: (0, 0)),
                  pl.BlockSpec((1, LANES, N_BRANCH * 256), lambda i, g, j: (g, 0, 0)),
                  pl.BlockSpec((t, t), lambda i, g, j: (0, 0))],
        out_specs=pl.BlockSpec((t, 2 * LANES), lambda i, g, j: (i * nq + j, g)),
        out_shape=jax.ShapeDtypeStruct((q.shape[0], NSA_HEADS * NSA_HEAD_DIM), BF16),
        scratch_shapes=[pltpu.VMEM((NSA_K // KEY_CHUNK, 1, NSA_GROUP * t), F32),
                        pltpu.VMEM((NSA_K // KEY_CHUNK, 1, NSA_GROUP * t), F32),
                        pltpu.VMEM((NSA_K // KEY_CHUNK, NSA_HEAD_DIM, NSA_GROUP * t), F32),
                        pltpu.VMEM((SEQ, t), F32)],
        compiler_params=pltpu.CompilerParams(
            dimension_semantics=("parallel", "parallel", "arbitrary"), vmem_limit_bytes=VMEM_LIMIT),
        name="nsa_attention",
    )(q, kdup, kdup, vt, vt, kc, vct, gates, tables, tabw,
      jnp.asarray(ovt, BF16), jnp.asarray(expand, BF16), jnp.asarray(gx, BF16), jnp.eye(t, dtype=BF16))


def _tail_kernel(x_ref, oa_ref, ob_ref, ga_ref, gb_ref, woa_ref, wob_ref, ln2_ref, wup_ref, wdn_ref,
                 out_ref, h2_ref):
    width = MLA_HEADS * MLA_V
    a = (_rms(oa_ref[...].astype(F32), width) * ga_ref[...]).astype(BF16)
    b = (_rms(ob_ref[...].astype(F32), width) * gb_ref[...]).astype(BF16)
    x1 = x_ref[...] + _dot(a, woa_ref[...]) + _dot(b, wob_ref[...])
    h2_ref[...] = (_rms(x1, D_MODEL) * ln2_ref[...]).astype(BF16)
    out_ref[...] = x1

    def body(f, carry):
        u = jnp.maximum(_dot(h2_ref[...], wup_ref[f]), 0.0)
        out_ref[...] += _dot((u * u).astype(BF16), wdn_ref[f])
        return carry

    lax.fori_loop(0, D_FF // FF_CHUNK, body, 0)


def _tail(x2d, o_a, o_b, gn_a, gn_b, w_o, ln2_g, w_up, w_down):
    t = x2d.shape[0]
    rows = TAIL_ROWS
    width = MLA_HEADS * MLA_V
    once = lambda a: pl.BlockSpec(a.shape, lambda i: (0,) * a.ndim)
    nf = D_FF // FF_CHUNK
    wup = w_up.astype(BF16).reshape(D_MODEL, nf, FF_CHUNK).transpose(1, 0, 2)
    wdn = w_down.astype(BF16).reshape(nf, FF_CHUNK, D_MODEL)
    consts = [gn_a[None, :], gn_b[None, :], w_o[:width].astype(BF16), w_o[width:].astype(BF16),
              ln2_g[None, :], wup, wdn]
    row_spec = lambda w: pl.BlockSpec((rows, w), lambda i: (i, 0))
    return pl.pallas_call(
        _tail_kernel,
        grid=(t // rows,),
        in_specs=[row_spec(D_MODEL), row_spec(width), row_spec(width)] + [once(a) for a in consts],
        out_specs=row_spec(D_MODEL),
        out_shape=jax.ShapeDtypeStruct((t, D_MODEL), F32),
        scratch_shapes=[pltpu.VMEM((rows, D_MODEL), BF16)],
        compiler_params=pltpu.CompilerParams(dimension_semantics=("parallel",),
                                             vmem_limit_bytes=VMEM_LIMIT),
        name="tail",
    )(x2d, o_a, o_b, *consts)


def kernel(x, ln1_g, w_in, mla_cq_norm_g, mla_ckv_norm_g, mla_w_uq, mla_w_ukv, mla_q_gain, mla_k_gain,
           nsa_q_gain, nsa_k_gain, nsa_cmp_pe, nsa_cmp_w1, nsa_cmp_w2, rel_bias, grp_norm_mla,
           grp_norm_nsa, w_o, ln2_g, w_up, w_down):
    b, s, d = x.shape
    assert (s, d) == (SEQ, D_MODEL) and ln1_g.shape[0] == 1
    x2d = x.reshape(b * s, d)
    tables = _bias_tables(rel_bias)
    q_mla, k_mla, vt_mla, q_nsa, kdup, vt_nsa, kvc, gates = _projections(
        x2d, ln1_g[0], w_in[0], mla_cq_norm_g[0], mla_ckv_norm_g[0], mla_w_uq[0], mla_w_ukv[0],
        mla_q_gain[0], mla_k_gain[0], nsa_q_gain[0], nsa_k_gain[0])
    kc, vct = _compress(kvc, nsa_cmp_pe[0], nsa_cmp_w1[0], nsa_cmp_w2[0], nsa_k_gain[0, 0])
    o_a = _mla_attention(q_mla, k_mla, vt_mla)
    o_b = _nsa_attention(q_nsa, kdup, vt_nsa, kc, vct, gates, tables)
    out = _tail(x2d, o_a, o_b, grp_norm_mla[0], grp_norm_nsa[0], w_o[0], ln2_g[0], w_up[0], w_down[0])
    return out.reshape(b, s, d)
```

```python
import math

import numpy as np
import jax
import jax.numpy as jnp
from jax import lax
from jax.experimental import pallas as pl
from jax.experimental.pallas import tpu as pltpu

F32 = jnp.float32
BF16 = jnp.bfloat16

D_MODEL = 1024
SEQ = 2048
MLA_HEADS = 8
MLA_NOPE = 64
MLA_ROPE = 32
MLA_V = 64
MLA_QK = MLA_NOPE + MLA_ROPE
MLA_Q_RANK = 384
MLA_KV_RANK = 256
ROPE_THETA = 10000.0
NSA_HEADS = 8
NSA_KV_HEADS = 2
NSA_GROUP = NSA_HEADS // NSA_KV_HEADS
NSA_HEAD_DIM = 64
N_BRANCH = 3
CMP_BLOCK = 32
CMP_STRIDE = 16
CMP_HIDDEN = 128
SEL_BLOCK = 64
SEL_TOP_N = 16
WINDOW = 512
FORCE_SCORE = 1e4
REL_BUCKETS = 32
REL_MAX_DIST = 128
D_FF = 4 * D_MODEL
EPS = 1e-6

LANES = 128
NEG = -1e30
N_CHUNK = SEQ // CMP_STRIDE
N_SEL = SEQ // SEL_BLOCK
NSA_Q = 128
NSA_K = 256
SELECT_Q = 512
N_BIAS_TILES = (WINDOW + NSA_K - NSA_Q) // NSA_Q + 1
SEL_FAR_TILE = (REL_MAX_DIST + NSA_K) // NSA_Q
MLA_Q = 256
MLA_K = 512
V_TILE = 256
PROJ_ROWS = 256
TAIL_ROWS = 512
FF_CHUNK = 1024
KEY_CHUNK = 128
VMEM_LIMIT = 56 * 1024 * 1024

assert PROJ_ROWS == V_TILE and MLA_K % V_TILE == 0 and NSA_K == V_TILE


def _dot(a, b):
    return jnp.dot(a, b, preferred_element_type=F32)


def _dot_nt(a, b):
    return lax.dot_general(a, b, (((1,), (1,)), ((), ())), preferred_element_type=F32)


def _split(a, terms):
    pieces = []
    rem = a
    for _ in range(terms):
        piece = rem.astype(BF16)
        pieces.append(piece)
        rem = rem - piece.astype(F32)
    return pieces


def _split_dot(a, b, terms=2):
    return sum(_dot(p, b) for p in _split(a, terms))


def _rms(x, width):
    return x * lax.rsqrt(jnp.sum(x * x, axis=-1, keepdims=True) * (1.0 / width) + EPS)


def _t5_bucket_np(dist):
    n = np.maximum(dist, 0)
    max_exact = REL_BUCKETS // 2
    large = max_exact + (np.log(np.maximum(n, 1).astype(np.float32) / max_exact)
                         / math.log(REL_MAX_DIST / max_exact)
                         * (REL_BUCKETS - max_exact)).astype(np.int32)
    large = np.minimum(large, REL_BUCKETS - 1)
    return np.where(n < max_exact, n, large).astype(np.int32)


def _bias_index_table():
    i = np.arange(NSA_Q)[None, :]
    parts = []
    c = np.arange(N_CHUNK)[:, None]
    for qt in range(SEQ // NSA_Q):
        dist_c = qt * NSA_Q + i - (c * CMP_STRIDE + CMP_BLOCK - 1)
        parts.append(np.where((dist_c >= 0) & (c < N_CHUNK - 1), _t5_bucket_np(dist_c), -1))
    j = np.arange(NSA_K)[:, None]
    for off in range(N_BIAS_TILES):
        d = off * NSA_Q + i - j
        parts.append(np.where((d >= 0) & (d < WINDOW), _t5_bucket_np(d), -1))
    return np.concatenate(parts, axis=0).astype(np.int32)


def _bias_table_kernel(rb_ref, idx_ref, out_ref):
    h = pl.program_id(0)
    idx = idx_ref[...]
    acc = jnp.full(idx.shape, NEG, F32)
    for b in range(REL_BUCKETS):
        acc = jnp.where(idx == b, rb_ref[h, b], acc)
    out_ref[0] = acc


def _bias_tables(rel_bias):
    idx = jnp.asarray(_bias_index_table())
    rows = idx.shape[0]
    blk = 512
    return pl.pallas_call(
        _bias_table_kernel,
        grid=(NSA_HEADS, rows // blk),
        in_specs=[pl.BlockSpec(memory_space=pltpu.SMEM),
                  pl.BlockSpec((blk, LANES), lambda h, r: (r, 0))],
        out_specs=pl.BlockSpec((1, blk, LANES), lambda h, r: (h, r, 0)),
        out_shape=jax.ShapeDtypeStruct((NSA_HEADS, rows, LANES), F32),
        name="bias_tables",
    )(rel_bias.T, idx)


def _rope_tables(scale):
    half = MLA_ROPE // 2
    inv = 1.0 / (ROPE_THETA ** (jnp.arange(half, dtype=F32) / half))
    ang = jnp.arange(SEQ, dtype=F32)[:, None] * inv[None, :]
    cos, sin = jnp.cos(ang), jnp.sin(ang)
    ones = jnp.ones((SEQ, MLA_NOPE), F32)
    z16 = jnp.zeros((SEQ, half), F32)
    z32 = jnp.zeros((SEQ, LANES - MLA_QK), F32)
    z64 = jnp.zeros((SEQ, MLA_NOPE), F32)
    c = jnp.concatenate([ones, cos, cos, z32], axis=1) * scale
    s_up = jnp.concatenate([z64, z16, sin, z32], axis=1) * scale
    s_dn = jnp.concatenate([z64, -sin, z16, z32], axis=1) * scale
    return c, s_up, s_dn


def _rope_slabs(x, c, s_up, s_dn):
    half = MLA_ROPE // 2
    outs = []
    for h in range(MLA_HEADS):
        sl = x[:, h * LANES:(h + 1) * LANES]
        outs.append(sl * c + pltpu.roll(sl, half, 1) * s_up + pltpu.roll(sl, LANES - half, 1) * s_dn)
    return jnp.concatenate(outs, axis=1)


def _proj_kernel(x_ref, ln1_ref, wp_ref, cqg_ref, ckvg_ref, wuq_ref, wukk_ref, wukvt_ref,
                 qg_ref, kg_ref, qc_ref, qsu_ref, qsd_ref, kc_ref, ksu_ref, ksd_ref,
                 pkr_ref, seg128_ref, seg128t_ref, seg64_ref, seg64t_ref,
                 nq_gain_ref, nk_gain_ref, dup_ref, eye_ref,
                 qmla_ref, kmla_ref, vmla_ref, qnsa_ref, kdup_ref, vnsa_ref, kvc_ref, gate_ref):
    x = x_ref[...]
    h = (_rms(x, D_MODEL) * ln1_ref[...]).astype(BF16)
    proj = _dot(h, wp_ref[...])

    def head_norm(v, seg_ref, segt_ref, width):
        ss = _split_dot(v * v, seg_ref[...])
        rs = lax.rsqrt(ss * (1.0 / width) + EPS)
        return v * _split_dot(rs, segt_ref[...])

    cq = (_rms(proj[:, :MLA_Q_RANK], MLA_Q_RANK) * cqg_ref[...]).astype(BF16)
    q = head_norm(_dot(cq, wuq_ref[...]), seg128_ref, seg128t_ref, MLA_QK) * qg_ref[...]
    qmla_ref[...] = _rope_slabs(q, qc_ref[...], qsu_ref[...], qsd_ref[...]).astype(BF16)

    o_kv = MLA_Q_RANK
    ckv = (_rms(proj[:, o_kv:o_kv + MLA_KV_RANK], MLA_KV_RANK) * ckvg_ref[...]).astype(BF16)
    misc = proj[:, 640:768]
    k = _dot(ckv, wukk_ref[...]) + _split_dot(misc, pkr_ref[...])
    k = head_norm(k, seg128_ref, seg128t_ref, MLA_QK) * kg_ref[...]
    kmla_ref[...] = _rope_slabs(k, kc_ref[...], ksu_ref[...], ksd_ref[...]).astype(BF16)
    vmla_ref[0] = _dot_nt(wukvt_ref[...], ckv).astype(BF16)

    qb = head_norm(proj[:, 768:1280], seg64_ref, seg64t_ref, NSA_HEAD_DIM)
    qnsa_ref[...] = (qb * nq_gain_ref[...]).astype(BF16)

    kk = proj[:, 1280:1536]
    ss = _split_dot(kk * kk, seg64_ref[:256, :])
    rs = lax.rsqrt(ss * (1.0 / NSA_HEAD_DIM) + EPS)
    kn = kk * _split_dot(rs, seg64t_ref[:, :256]) * nk_gain_ref[...]
    kdup_ref[...] = _dot(kn.astype(BF16), dup_ref[...]).astype(BF16)
    vnsa_ref[0] = _dot_nt(eye_ref[...], proj[:, 1536:1792].astype(BF16)).astype(BF16)
    kvc_ref[...] = proj[:, 1792:2048]
    gate_ref[...] = 1.0 / (1.0 + jnp.exp(-misc))


def _seg_matrix(width, seg, cols=LANES):
    m = np.zeros((width, cols), np.float32)
    m[np.arange(width), np.arange(width) // seg] = 1.0
    return m


def _projections(x2d, ln1_g, w_in, cq_g, ckv_g, w_uq, w_ukv, q_gain, k_gain, nq_gain, nk_gain):
    t = x2d.shape[0]
    kv0 = 1184
    seg = lambda a, b: w_in[:, a:b]
    kvb = lambda br, kv, g: seg(kv0 + ((br * 2 + kv) * 2 + g) * 64, kv0 + ((br * 2 + kv) * 2 + g) * 64 + 64)
    wp = jnp.concatenate([
        seg(0, 640),
        seg(640, 672), seg(1952, 1976), jnp.zeros((D_MODEL, 72), F32),
        seg(672, 1184),
        kvb(1, 0, 0), kvb(1, 0, 1), kvb(2, 0, 0), kvb(2, 0, 1),
        kvb(1, 1, 0), kvb(1, 1, 1), kvb(2, 1, 0), kvb(2, 1, 1),
        kvb(0, 0, 0), kvb(0, 0, 1), kvb(0, 1, 0), kvb(0, 1, 1),
    ], axis=1).astype(BF16)
    pad = LANES - MLA_QK
    wuq = jnp.pad(w_uq.reshape(MLA_Q_RANK, MLA_HEADS, MLA_QK), ((0, 0), (0, 0), (0, pad)))
    wuq = wuq.reshape(MLA_Q_RANK, MLA_HEADS * LANES).astype(BF16)
    wkv = w_ukv.reshape(MLA_KV_RANK, MLA_HEADS, MLA_NOPE + MLA_V)
    wukk = jnp.pad(wkv[:, :, :MLA_NOPE], ((0, 0), (0, 0), (0, LANES - MLA_NOPE)))
    wukk = wukk.reshape(MLA_KV_RANK, MLA_HEADS * LANES).astype(BF16)
    wukvt = wkv[:, :, MLA_NOPE:].reshape(MLA_KV_RANK, MLA_HEADS * MLA_V).T.astype(BF16)
    slab_gain = lambda g: jnp.tile(jnp.pad(g, (0, pad)), MLA_HEADS)[None, :]
    qc, qsu, qsd = _rope_tables(MLA_QK ** -0.5)
    kc, ksu, ksd = _rope_tables(1.0)
    pkr = np.zeros((LANES, MLA_HEADS * LANES), np.float32)
    for hh in range(MLA_HEADS):
        pkr[np.arange(MLA_ROPE), hh * LANES + MLA_NOPE + np.arange(MLA_ROPE)] = 1.0
    seg128 = _seg_matrix(MLA_HEADS * LANES, LANES)
    seg64 = _seg_matrix(NSA_HEADS * NSA_HEAD_DIM, NSA_HEAD_DIM)
    dup = np.zeros((256, 512), np.float32)
    for s in range(4):
        for rep in range(2):
            dup[s * 64 + np.arange(64), s * 128 + rep * 64 + np.arange(64)] = 1.0
    nqg = (jnp.tile(nq_gain, NSA_HEADS) * (NSA_HEAD_DIM ** -0.5))[None, :]
    nkg = jnp.concatenate([nk_gain[1], nk_gain[1], nk_gain[2], nk_gain[2]])[None, :]

    rows = PROJ_ROWS
    n_pos = SEQ // rows
    full = lambda a: pl.BlockSpec(a.shape, lambda i: (0,) * a.ndim)
    tab = pl.BlockSpec((rows, LANES), lambda i: (i % n_pos, 0))
    consts = [ln1_g[None, :], wp, cq_g[None, :], ckv_g[None, :], wuq, wukk, wukvt,
              slab_gain(q_gain), slab_gain(k_gain)]
    tabs = [qc, qsu, qsd, kc, ksu, ksd]
    mats = [jnp.asarray(pkr, BF16), jnp.asarray(seg128, BF16), jnp.asarray(seg128.T, BF16),
            jnp.asarray(seg64, BF16), jnp.asarray(seg64.T, BF16), nqg, nkg, jnp.asarray(dup, BF16),
            jnp.eye(256, dtype=BF16)]
    row_spec = lambda w: pl.BlockSpec((rows, w), lambda i: (i, 0))
    vt_spec = lambda w: pl.BlockSpec((1, w, rows), lambda i: (i, 0, 0))
    sds = jax.ShapeDtypeStruct
    return pl.pallas_call(
        _proj_kernel,
        grid=(t // rows,),
        in_specs=[row_spec(D_MODEL)] + [full(a) for a in consts] + [tab] * 6 + [full(a) for a in mats],
        out_specs=[row_spec(1024), row_spec(1024), vt_spec(512), row_spec(512), row_spec(512),
                   vt_spec(256), row_spec(256), row_spec(128)],
        out_shape=[sds((t, 1024), BF16), sds((t, 1024), BF16), sds((t // rows, 512, rows), BF16),
                   sds((t, 512), BF16), sds((t, 512), BF16), sds((t // rows, 256, rows), BF16),
                   sds((t, 256), F32), sds((t, 128), F32)],
        compiler_params=pltpu.CompilerParams(dimension_semantics=("parallel",),
                                             vmem_limit_bytes=VMEM_LIMIT),
        name="projections",
    )(x2d, *consts, *tabs, *mats)


def _compress_kernel(xk_ref, xv_ref, pe_ref, w1_ref, w2k_ref, w2vt_ref, kg_ref, dup_ref, kc_ref, vct_ref):
    half = CMP_STRIDE * NSA_HEAD_DIM

    def hidden(x, which):
        pe = pe_ref[which]
        w1 = w1_ref[which]
        top = _dot((x + pe[:, :half]).astype(BF16), w1[:half])
        bot = _dot((x + pe[:, half:]).astype(BF16), w1[half:])
        hid = top + pltpu.roll(bot, N_CHUNK - 1, 0)
        return (hid * (1.0 / (1.0 + jnp.exp(-hid)))).astype(BF16)

    k = _dot(hidden(xk_ref[0, 0], 0), w2k_ref[...])
    k = _rms(k, NSA_HEAD_DIM) * kg_ref[...]
    kc_ref[0, 0] = _dot(k.astype(BF16), dup_ref[...]).astype(BF16)
    vct_ref[0, 0] = _dot_nt(w2vt_ref[...], hidden(xv_ref[0, 0], 1)).astype(BF16)


def _compress(kvc, cmp_pe, cmp_w1, cmp_w2, k_gain0):
    b = kvc.shape[0] // SEQ
    g_ = NSA_KV_HEADS
    xc = kvc.reshape(b, SEQ, 4, NSA_HEAD_DIM).transpose(0, 2, 1, 3)
    xc = xc.reshape(b, 4, N_CHUNK, CMP_STRIDE * NSA_HEAD_DIM)
    pe = cmp_pe.reshape(2, 1, CMP_BLOCK * NSA_HEAD_DIM)
    dup = np.zeros((NSA_HEAD_DIM, LANES), np.float32)
    dup[np.arange(64), np.arange(64)] = 1.0
    dup[np.arange(64), 64 + np.arange(64)] = 1.0
    full = lambda a: pl.BlockSpec(a.shape, lambda i, g: (0,) * a.ndim)
    consts = [pe, cmp_w1.astype(BF16), cmp_w2[0].astype(BF16), cmp_w2[1].T.astype(BF16),
              k_gain0[None, :], jnp.asarray(dup, BF16)]
    x_spec = lambda off: pl.BlockSpec((1, 1, N_CHUNK, CMP_STRIDE * NSA_HEAD_DIM),
                                      lambda i, g: (i, off + g, 0, 0))
    return pl.pallas_call(
        _compress_kernel,
        grid=(b, g_),
        in_specs=[x_spec(0), x_spec(g_)] + [full(a) for a in consts],
        out_specs=[pl.BlockSpec((1, 1, N_CHUNK, LANES), lambda i, g: (i, g, 0, 0)),
                   pl.BlockSpec((1, 1, NSA_HEAD_DIM, N_CHUNK), lambda i, g: (i, g, 0, 0))],
        out_shape=[jax.ShapeDtypeStruct((b, g_, N_CHUNK, LANES), BF16),
                   jax.ShapeDtypeStruct((b, g_, NSA_HEAD_DIM, N_CHUNK), BF16)],
        name="compression",
    )(xc, xc, *consts)


def _flash_init(m_ref, l_ref, acc_ref):
    m_ref[...] = jnp.full(m_ref.shape, NEG, F32)
    l_ref[...] = jnp.zeros(l_ref.shape, F32)
    acc_ref[...] = jnp.zeros(acc_ref.shape, F32)


def _flash_step(s, v, m_ref, l_ref, acc_ref, chunk=KEY_CHUNK):
    for c in range(s.shape[0] // chunk):
        sc = s[c * chunk:(c + 1) * chunk]
        m_old = m_ref[c]
        m_new = jnp.maximum(m_old, jnp.max(sc, axis=0, keepdims=True))
        alpha = jnp.exp(m_old - m_new)
        p = jnp.exp(sc - m_new)
        l_ref[c] = alpha * l_ref[c] + jnp.sum(p, axis=0, keepdims=True)
        acc_ref[c] = alpha * acc_ref[c] + _dot(v[:, c * chunk:(c + 1) * chunk], p.astype(BF16))
        m_ref[c] = m_new


def _flash_finish(m_ref, l_ref, acc_ref):
    n = m_ref.shape[0]
    m = m_ref[0]
    for c in range(1, n):
        m = jnp.maximum(m, m_ref[c])
    w = [jnp.exp(m_ref[c] - m) for c in range(n)]
    l = sum(w[c] * l_ref[c] for c in range(n))
    acc = sum(w[c] * acc_ref[c] for c in range(n))
    return acc / l


def _pipelined_pairs(sched_ref, row, n, scores, update):
    assert n % 2 == 0

    def pair(i):
        i = jnp.minimum(i, n - 1)
        return sched_ref[row, i], sched_ref[row + 1, i]

    scores(*pair(0), 0)

    def body(j, carry):
        i = 2 * j
        scores(*pair(i + 1), 1)
        update(*pair(i), 0)
        scores(*pair(i + 2), 0)
        update(*pair(i + 1), 1)
        return carry

    lax.fori_loop(0, n // 2, body, 0)


def _mla_pairs():
    nq, nk = SEQ // MLA_Q, SEQ // MLA_K
    full = [(qt, kt) for kt in range(nk) for qt in range(nq) if (kt + 1) * MLA_K <= qt * MLA_Q]
    diag = [(qt, (qt * MLA_Q) // MLA_K) for qt in range(nq)]
    return full, diag


def _mla_kernel(sched_ref, q_ref, k_ref, vt_ref, eye_ref, o_ref, m_ref, l_ref, acc_ref, s_ref):
    sub = MLA_K // V_TILE
    full, diag = _mla_pairs()
    rel = (lax.broadcasted_iota(jnp.int32, (MLA_K, MLA_Q), 1)
           - lax.broadcasted_iota(jnp.int32, (MLA_K, MLA_Q), 0))
    _flash_init(m_ref, l_ref, acc_ref)

    def scores(qt, kt, slot, masked):
        qrows = pl.ds(pl.multiple_of(qt * MLA_Q, MLA_Q), MLA_Q)
        krows = pl.ds(pl.multiple_of(kt * MLA_K, MLA_K), MLA_K)
        for hh in range(2):
            cols = slice(hh * LANES, (hh + 1) * LANES)
            s = _dot_nt(k_ref[krows, cols], q_ref[qrows, cols])
            if masked:
                s = jnp.where(rel >= kt * MLA_K - qt * MLA_Q, s, NEG)
            s_ref[slot, hh] = s

    def update(qt, kt, slot):
        for hh in range(2):
            v = jnp.concatenate([vt_ref[kt * sub + i, hh * MLA_V:(hh + 1) * MLA_V, :] for i in range(sub)],
                                axis=1)
            _flash_step(s_ref[slot, hh], v, m_ref.at[hh, qt], l_ref.at[hh, qt], acc_ref.at[hh, qt])

    _pipelined_pairs(sched_ref, 0, len(full), lambda qt, kt, slot: scores(qt, kt, slot, False), update)
    _pipelined_pairs(sched_ref, 2, len(diag), lambda qt, kt, slot: scores(qt, kt, slot, True), update)

    for qt in range(SEQ // MLA_Q):
        o_t = jnp.concatenate([_flash_finish(m_ref.at[hh, qt], l_ref.at[hh, qt], acc_ref.at[hh, qt])
                               for hh in range(2)], axis=0)
        o_ref[qt * MLA_Q:(qt + 1) * MLA_Q, :] = _dot_nt(eye_ref[...], o_t.astype(BF16)).astype(BF16)


def _schedule(*pair_lists):
    n = max(len(p) for p in pair_lists)
    out = np.zeros((2 * len(pair_lists), n), np.int32)
    for i, pairs in enumerate(pair_lists):
        out[2 * i, :len(pairs)] = [a for a, _ in pairs]
        out[2 * i + 1, :len(pairs)] = [b for _, b in pairs]
    return jnp.asarray(out)


def _mla_attention(q, k, vt):
    b = q.shape[0] // SEQ
    nq = SEQ // MLA_Q
    nv = SEQ // V_TILE
    chunks = MLA_K // KEY_CHUNK
    return pl.pallas_call(
        _mla_kernel,
        grid=(b, MLA_HEADS // 2),
        in_specs=[pl.BlockSpec(memory_space=pltpu.SMEM),
                  pl.BlockSpec((SEQ, 2 * LANES), lambda i, hp: (i, hp)),
                  pl.BlockSpec((SEQ, 2 * LANES), lambda i, hp: (i, hp)),
                  pl.BlockSpec((nv, 2 * MLA_V, V_TILE), lambda i, hp: (i, hp, 0)),
                  pl.BlockSpec((MLA_Q, MLA_Q), lambda i, hp: (0, 0))],
        out_specs=pl.BlockSpec((SEQ, LANES), lambda i, hp: (i, hp)),
        out_shape=jax.ShapeDtypeStruct((q.shape[0], MLA_HEADS * MLA_V), BF16),
        scratch_shapes=[pltpu.VMEM((2, nq, chunks, 1, MLA_Q), F32),
                        pltpu.VMEM((2, nq, chunks, 1, MLA_Q), F32),
                        pltpu.VMEM((2, nq, chunks, MLA_V, MLA_Q), F32),
                        pltpu.VMEM((2, 2, MLA_K, MLA_Q), F32)],
        compiler_params=pltpu.CompilerParams(
            dimension_semantics=("parallel", "parallel"), vmem_limit_bytes=VMEM_LIMIT),
        name="mla_attention",
    )(_schedule(*_mla_pairs()), q, k, vt, jnp.eye(MLA_Q, dtype=BF16))


def _stack_heads(q2):
    low = lax.broadcasted_iota(jnp.int32, (q2.shape[0], LANES), 1) < NSA_HEAD_DIM
    qa, qb = q2[:, :LANES], q2[:, LANES:]
    zero = jnp.zeros_like(qa)
    return jnp.concatenate([jnp.where(low, qa, zero), jnp.where(low, zero, qa),
                            jnp.where(low, qb, zero), jnp.where(low, zero, qb)], axis=0)


def _select_kernel(q_ref, kc_ref, vct_ref, tabc_ref, ovt_ref, negsel_ref, ocmp_ref):
    step = pl.program_id(2)
    t = SELECT_Q
    heads = range(NSA_GROUP)
    tiles = range(t // NSA_Q)
    q4 = _stack_heads(q_ref[...])

    bias_c = jnp.concatenate([tabc_ref[r, u * N_CHUNK:(u + 1) * N_CHUNK, :] for r in heads for u in tiles],
                             axis=1)
    valid = bias_c > 0.5 * NEG
    sc = jnp.where(valid, _dot_nt(kc_ref[0, 0], q4) + bias_c, NEG)
    mx = jnp.max(sc, axis=0, keepdims=True)
    p = jnp.where(valid, jnp.exp(sc - mx), 0.0)
    pc = p / jnp.maximum(jnp.sum(p, axis=0, keepdims=True), 1e-30)
    o_cmp = _dot(vct_ref[0, 0], pc.astype(BF16)).astype(BF16)
    for u in tiles:
        ocmp_ref[0, 0, u] = jnp.concatenate(
            [o_cmp[:, r * t + u * NSA_Q:r * t + (u + 1) * NSA_Q] for r in heads], axis=0)

    psum = sum(pc[:, r * t:(r + 1) * t] for r in heads)
    imp = sum(_dot(ovt_ref[...], piece) for piece in _split(psum, 3))
    jj = lax.broadcasted_iota(jnp.int32, (N_SEL, t), 0)
    blk = (step * t + lax.broadcasted_iota(jnp.int32, (N_SEL, t), 1)) // SEL_BLOCK
    ok = jj <= blk
    forced = ok & ((jj == 0) | (jj == blk) | (jj == blk - 1))
    score = jnp.where(forced, FORCE_SCORE, jnp.where(ok, imp, -jnp.inf))
    rank = jnp.zeros((N_SEL, t), jnp.int32)
    for i in range(N_SEL):
        ci = score[i:i + 1, :]
        beats = (ci > score) | ((ci == score) & (jj > i))
        rank = rank + beats.astype(jnp.int32)
    negsel = jnp.where(ok & (rank < SEL_TOP_N), 0.0, NEG)
    for u in tiles:
        negsel_ref[0, 0, u] = negsel[:, u * NSA_Q:(u + 1) * NSA_Q]


def _nsa_select(q, kc, vct, tables):
    b = q.shape[0] // SEQ
    t = SELECT_Q
    steps = SEQ // t
    nq = SEQ // NSA_Q
    per = t // NSA_Q
    g_ = NSA_KV_HEADS
    c0 = np.arange(N_CHUNK)[None, :] * CMP_STRIDE
    j0 = np.arange(N_SEL)[:, None] * SEL_BLOCK
    ovt = np.clip(np.minimum(c0 + CMP_BLOCK, j0 + SEL_BLOCK) - np.maximum(c0, j0), 0, None) / CMP_BLOCK
    ovt[:, N_CHUNK - 1:] = 0.0
    return pl.pallas_call(
        _select_kernel,
        grid=(b, g_, steps),
        in_specs=[pl.BlockSpec((t, 2 * LANES), lambda i, g, j: (i * steps + j, g)),
                  pl.BlockSpec((1, 1, N_CHUNK, LANES), lambda i, g, j: (i, g, 0, 0)),
                  pl.BlockSpec((1, 1, NSA_HEAD_DIM, N_CHUNK), lambda i, g, j: (i, g, 0, 0)),
                  pl.BlockSpec((NSA_GROUP, per * N_CHUNK, LANES), lambda i, g, j: (g, j, 0)),
                  pl.BlockSpec((N_SEL, N_CHUNK), lambda i, g, j: (0, 0))],
        out_specs=[pl.BlockSpec((1, 1, per, N_SEL, NSA_Q), lambda i, g, j: (i, g, j, 0, 0)),
                   pl.BlockSpec((1, 1, per, NSA_GROUP * NSA_HEAD_DIM, NSA_Q), lambda i, g, j: (i, g, j, 0, 0))],
        out_shape=[jax.ShapeDtypeStruct((b, g_, nq, N_SEL, NSA_Q), F32),
                   jax.ShapeDtypeStruct((b, g_, nq, NSA_GROUP * NSA_HEAD_DIM, NSA_Q), BF16)],
        compiler_params=pltpu.CompilerParams(
            dimension_semantics=("parallel", "parallel", "parallel"), vmem_limit_bytes=VMEM_LIMIT),
        name="nsa_select",
    )(q, kc, vct, tables, jnp.asarray(ovt, BF16))


def _nsa_pairs():
    nq, nk = SEQ // NSA_Q, SEQ // NSA_K
    sel = [(qt, kt) for kt in range(nk) for qt in range(nq) if kt * NSA_K <= qt * NSA_Q]
    win = [(qt, kt) for kt in range(nk) for qt in range(nq)
           if max(qt * NSA_Q - (WINDOW - 1), 0) // NSA_K <= kt and kt * NSA_K <= qt * NSA_Q]
    return sel, win


def _attend_kernel(sched_ref, q_ref, ksel_ref, kwin_ref, vsel_ref, vwin_ref, negsel_ref, ocmp_ref, gate_ref,
                   tabw_ref, gx_ref, eye_ref, o_ref, m_ref, l_ref, acc_ref, s_ref):
    t = NSA_Q
    heads = range(NSA_GROUP)
    sel, win = _nsa_pairs()
    _flash_init(m_ref, l_ref, acc_ref)

    def scores(qt, kt, slot, k_ref, selected):
        q4 = _stack_heads(q_ref[pl.ds(pl.multiple_of(qt * t, t), t), :])
        s = _dot_nt(k_ref[pl.ds(pl.multiple_of(kt * NSA_K, NSA_K), NSA_K), :], q4)
        off = qt - kt * (NSA_K // NSA_Q)
        if selected:
            off = jnp.minimum(off, SEL_FAR_TILE)
        s = s + jnp.concatenate([tabw_ref[r, off] for r in heads], axis=1)
        if selected:
            per = NSA_K // SEL_BLOCK
            rows = [negsel_ref[0, 0, qt, pl.ds(kt * per + bb, 1), :] for bb in range(per)]
            s = jnp.concatenate(
                [s[bb * SEL_BLOCK:(bb + 1) * SEL_BLOCK] + jnp.concatenate([rows[bb]] * NSA_GROUP, axis=1)
                 for bb in range(per)], axis=0)
        s_ref[slot] = s

    def update(qt, kt, slot, v_ref, branch):
        _flash_step(s_ref[slot], v_ref[kt], m_ref.at[branch, qt], l_ref.at[branch, qt], acc_ref.at[branch, qt],
                    chunk=NSA_K)

    _pipelined_pairs(sched_ref, 0, len(sel),
                     lambda qt, kt, slot: scores(qt, kt, slot, ksel_ref, True),
                     lambda qt, kt, slot: update(qt, kt, slot, vsel_ref, 0))
    _pipelined_pairs(sched_ref, 2, len(win),
                     lambda qt, kt, slot: scores(qt, kt, slot, kwin_ref, False),
                     lambda qt, kt, slot: update(qt, kt, slot, vwin_ref, 1))

    w = NSA_GROUP * NSA_HEAD_DIM
    for qt in range(SEQ // t):
        branches = [_flash_finish(m_ref.at[br, qt], l_ref.at[br, qt], acc_ref.at[br, qt]).astype(BF16)
                    for br in range(2)]
        stacked = jnp.concatenate([ocmp_ref[0, 0, qt]] + [o[:, r * t:(r + 1) * t] for o in branches for r in heads],
                                  axis=0)
        tok = _dot_nt(eye_ref[...], stacked)
        g = _split_dot(gate_ref[qt * t:(qt + 1) * t, :], gx_ref[0])
        out = g[:, :w] * tok[:, :w] + g[:, w:2 * w] * tok[:, w:2 * w] + g[:, 2 * w:] * tok[:, 2 * w:]
        o_ref[qt * t:(qt + 1) * t, :] = out.astype(BF16)


def _nsa_attend(q, kdup, vt, negsel, ocmp, gates, tables):
    b = q.shape[0] // SEQ
    t = NSA_Q
    nq = SEQ // t
    nv = SEQ // V_TILE
    g_ = NSA_KV_HEADS
    tabw = tables[:, SEQ:].reshape(NSA_HEADS, N_BIAS_TILES, NSA_K, LANES)
    gx = np.zeros((g_, LANES, N_BRANCH * 256), np.float32)
    for gg in range(g_):
        for br in range(N_BRANCH):
            for r in range(NSA_GROUP):
                src = MLA_ROPE + br * NSA_HEADS + gg * NSA_GROUP + r
                gx[gg, src, br * 256 + r * 64: br * 256 + (r + 1) * 64] = 1.0
    k_spec = lambda off: pl.BlockSpec((SEQ, LANES), lambda i, g: (i, off + g))
    v_spec = lambda off: pl.BlockSpec((nv, NSA_HEAD_DIM, V_TILE), lambda i, g: (i, off + g, 0))
    return pl.pallas_call(
        _attend_kernel,
        grid=(b, g_),
        in_specs=[pl.BlockSpec(memory_space=pltpu.SMEM),
                  pl.BlockSpec((SEQ, 2 * LANES), lambda i, g: (i, g)),
                  k_spec(0), k_spec(2), v_spec(0), v_spec(2),
                  pl.BlockSpec((1, 1, nq, N_SEL, t), lambda i, g: (i, g, 0, 0, 0)),
                  pl.BlockSpec((1, 1, nq, NSA_GROUP * NSA_HEAD_DIM, t), lambda i, g: (i, g, 0, 0, 0)),
                  pl.BlockSpec((SEQ, LANES), lambda i, g: (i, 0)),
                  pl.BlockSpec((NSA_GROUP, N_BIAS_TILES, NSA_K, LANES), lambda i, g: (g, 0, 0, 0)),
                  pl.BlockSpec((1, LANES, N_BRANCH * 256), lambda i, g: (g, 0, 0)),
                  pl.BlockSpec((t, t), lambda i, g: (0, 0))],
        out_specs=pl.BlockSpec((SEQ, 2 * LANES), lambda i, g: (i, g)),
        out_shape=jax.ShapeDtypeStruct((q.shape[0], NSA_HEADS * NSA_HEAD_DIM), BF16),
        scratch_shapes=[pltpu.VMEM((2, nq, 1, 1, NSA_GROUP * t), F32),
                        pltpu.VMEM((2, nq, 1, 1, NSA_GROUP * t), F32),
                        pltpu.VMEM((2, nq, 1, NSA_HEAD_DIM, NSA_GROUP * t), F32),
                        pltpu.VMEM((2, NSA_K, NSA_GROUP * t), F32)],
        compiler_params=pltpu.CompilerParams(
            dimension_semantics=("parallel", "parallel"), vmem_limit_bytes=VMEM_LIMIT),
        name="nsa_attend",
    )(_schedule(*_nsa_pairs()), q, kdup, kdup, vt, vt, negsel, ocmp, gates, tabw,
      jnp.asarray(gx, BF16), jnp.eye(t, dtype=BF16))


def _tail_kernel(x_ref, oa_ref, ob_ref, ga_ref, gb_ref, woa_ref, wob_ref, ln2_ref, wup_ref, wdn_ref,
                 out_ref, h2_ref):
    width = MLA_HEADS * MLA_V
    a = (_rms(oa_ref[...].astype(F32), width) * ga_ref[...]).astype(BF16)
    b = (_rms(ob_ref[...].astype(F32), width) * gb_ref[...]).astype(BF16)
    x1 = x_ref[...] + _dot(a, woa_ref[...]) + _dot(b, wob_ref[...])
    h2_ref[...] = (_rms(x1, D_MODEL) * ln2_ref[...]).astype(BF16)
    out_ref[...] = x1

    def body(f, carry):
        u = jnp.maximum(_dot(h2_ref[...], wup_ref[f]), 0.0)
        out_ref[...] += _dot((u * u).astype(BF16), wdn_ref[f])
        return carry

    lax.fori_loop(0, D_FF // FF_CHUNK, body, 0)


def _tail(x2d, o_a, o_b, gn_a, gn_b, w_o, ln2_g, w_up, w_down):
    t = x2d.shape[0]
    rows = TAIL_ROWS
    width = MLA_HEADS * MLA_V
    once = lambda a: pl.BlockSpec(a.shape, lambda i: (0,) * a.ndim)
    nf = D_FF // FF_CHUNK
    wup = w_up.astype(BF16).reshape(D_MODEL, nf, FF_CHUNK).transpose(1, 0, 2)
    wdn = w_down.astype(BF16).reshape(nf, FF_CHUNK, D_MODEL)
    consts = [gn_a[None, :], gn_b[None, :], w_o[:width].astype(BF16), w_o[width:].astype(BF16),
              ln2_g[None, :], wup, wdn]
    row_spec = lambda w: pl.BlockSpec((rows, w), lambda i: (i, 0))
    return pl.pallas_call(
        _tail_kernel,
        grid=(t // rows,),
        in_specs=[row_spec(D_MODEL), row_spec(width), row_spec(width)] + [once(a) for a in consts],
        out_specs=row_spec(D_MODEL),
        out_shape=jax.ShapeDtypeStruct((t, D_MODEL), F32),
        scratch_shapes=[pltpu.VMEM((rows, D_MODEL), BF16)],
        compiler_params=pltpu.CompilerParams(dimension_semantics=("parallel",),
                                             vmem_limit_bytes=VMEM_LIMIT),
        name="tail",
    )(x2d, o_a, o_b, *consts)


def kernel(x, ln1_g, w_in, mla_cq_norm_g, mla_ckv_norm_g, mla_w_uq, mla_w_ukv, mla_q_gain, mla_k_gain,
           nsa_q_gain, nsa_k_gain, nsa_cmp_pe, nsa_cmp_w1, nsa_cmp_w2, rel_bias, grp_norm_mla,
           grp_norm_nsa, w_o, ln2_g, w_up, w_down):
    b, s, d = x.shape
    assert (s, d) == (SEQ, D_MODEL) and ln1_g.shape[0] == 1
    x2d = x.reshape(b * s, d)
    tables = _bias_tables(rel_bias)
    q_mla, k_mla, vt_mla, q_nsa, kdup, vt_nsa, kvc, gates = _projections(
        x2d, ln1_g[0], w_in[0], mla_cq_norm_g[0], mla_ckv_norm_g[0], mla_w_uq[0], mla_w_ukv[0],
        mla_q_gain[0], mla_k_gain[0], nsa_q_gain[0], nsa_k_gain[0])
    kc, vct = _compress(kvc, nsa_cmp_pe[0], nsa_cmp_w1[0], nsa_cmp_w2[0], nsa_k_gain[0, 0])
    o_a = _mla_attention(q_mla, k_mla, vt_mla)
    negsel, ocmp = _nsa_select(q_nsa, kc, vct, tables)
    o_b = _nsa_attend(q_nsa, kdup, vt_nsa, negsel, ocmp, gates, tables)
    out = _tail(x2d, o_a, o_b, grp_norm_mla[0], grp_norm_nsa[0], w_o[0], ln2_g[0], w_up[0], w_down[0])
    return out.reshape(b, s, d)
```

```python
import math

import numpy as np
import jax
import jax.numpy as jnp
from jax import lax
from jax.experimental import pallas as pl
from jax.experimental.pallas import tpu as pltpu

F32 = jnp.float32
BF16 = jnp.bfloat16

D_MODEL = 1024
SEQ = 2048
MLA_HEADS = 8
MLA_NOPE = 64
MLA_ROPE = 32
MLA_V = 64
MLA_QK = MLA_NOPE + MLA_ROPE
MLA_Q_RANK = 384
MLA_KV_RANK = 256
ROPE_THETA = 10000.0
NSA_HEADS = 8
NSA_KV_HEADS = 2
NSA_GROUP = NSA_HEADS // NSA_KV_HEADS
NSA_HEAD_DIM = 64
N_BRANCH = 3
CMP_BLOCK = 32
CMP_STRIDE = 16
CMP_HIDDEN = 128
SEL_BLOCK = 64
SEL_TOP_N = 16
WINDOW = 512
FORCE_SCORE = 1e4
REL_BUCKETS = 32
REL_MAX_DIST = 128
D_FF = 4 * D_MODEL
EPS = 1e-6

LANES = 128
NEG = -1e30
N_CHUNK = SEQ // CMP_STRIDE
N_SEL = SEQ // SEL_BLOCK
NSA_Q = 128
NSA_K = 256
SELECT_Q = 512
N_BIAS_TILES = (WINDOW + NSA_K - NSA_Q) // NSA_Q + 1
SEL_FAR_TILE = (REL_MAX_DIST + NSA_K) // NSA_Q
MLA_Q = 256
MLA_K = 512
V_TILE = 256
PROJ_ROWS = 256
TAIL_ROWS = 512
FF_CHUNK = 1024
KEY_CHUNK = 256
V_ROWS = 80
LOG2E = math.log2(math.e)
VMEM_LIMIT = 56 * 1024 * 1024

assert PROJ_ROWS == V_TILE and MLA_K % V_TILE == 0 and NSA_K == V_TILE


def _dot(a, b):
    return jnp.dot(a, b, preferred_element_type=F32)


def _dot_nt(a, b):
    return lax.dot_general(a, b, (((1,), (1,)), ((), ())), preferred_element_type=F32)


def _split(a, terms):
    pieces = []
    rem = a
    for _ in range(terms):
        piece = rem.astype(BF16)
        pieces.append(piece)
        rem = rem - piece.astype(F32)
    return pieces


def _split_dot(a, b, terms=2):
    return sum(_dot(p, b) for p in _split(a, terms))


def _rms(x, width):
    return x * lax.rsqrt(jnp.sum(x * x, axis=-1, keepdims=True) * (1.0 / width) + EPS)


def _t5_bucket_np(dist):
    n = np.maximum(dist, 0)
    max_exact = REL_BUCKETS // 2
    large = max_exact + (np.log(np.maximum(n, 1).astype(np.float32) / max_exact)
                         / math.log(REL_MAX_DIST / max_exact)
                         * (REL_BUCKETS - max_exact)).astype(np.int32)
    large = np.minimum(large, REL_BUCKETS - 1)
    return np.where(n < max_exact, n, large).astype(np.int32)


def _bias_index_table():
    i = np.arange(NSA_Q)[None, :]
    parts = []
    c = np.arange(N_CHUNK)[:, None]
    for qt in range(SEQ // NSA_Q):
        dist_c = qt * NSA_Q + i - (c * CMP_STRIDE + CMP_BLOCK - 1)
        parts.append(np.where((dist_c >= 0) & (c < N_CHUNK - 1), _t5_bucket_np(dist_c), -1))
    j = np.arange(NSA_K)[:, None]
    for off in range(N_BIAS_TILES):
        d = off * NSA_Q + i - j
        parts.append(np.where((d >= 0) & (d < WINDOW), _t5_bucket_np(d), -1))
    return np.concatenate(parts, axis=0).astype(np.int32)


def _bias_table_kernel(rb_ref, idx_ref, out_ref):
    h = pl.program_id(0)
    idx = idx_ref[...]
    acc = jnp.full(idx.shape, NEG, F32)
    for b in range(REL_BUCKETS):
        acc = jnp.where(idx == b, rb_ref[h, b] * LOG2E, acc)
    out_ref[0] = acc


def _bias_tables(rel_bias):
    idx = jnp.asarray(_bias_index_table())
    rows = idx.shape[0]
    blk = 512
    return pl.pallas_call(
        _bias_table_kernel,
        grid=(NSA_HEADS, rows // blk),
        in_specs=[pl.BlockSpec(memory_space=pltpu.SMEM),
                  pl.BlockSpec((blk, LANES), lambda h, r: (r, 0))],
        out_specs=pl.BlockSpec((1, blk, LANES), lambda h, r: (h, r, 0)),
        out_shape=jax.ShapeDtypeStruct((NSA_HEADS, rows, LANES), F32),
        name="bias_tables",
    )(rel_bias.T, idx)


def _rope_tables(scale):
    half = MLA_ROPE // 2
    inv = 1.0 / (ROPE_THETA ** (jnp.arange(half, dtype=F32) / half))
    ang = jnp.arange(SEQ, dtype=F32)[:, None] * inv[None, :]
    cos, sin = jnp.cos(ang), jnp.sin(ang)
    ones = jnp.ones((SEQ, MLA_NOPE), F32)
    z16 = jnp.zeros((SEQ, half), F32)
    z32 = jnp.zeros((SEQ, LANES - MLA_QK), F32)
    z64 = jnp.zeros((SEQ, MLA_NOPE), F32)
    c = jnp.concatenate([ones, cos, cos, z32], axis=1) * scale
    s_up = jnp.concatenate([z64, z16, sin, z32], axis=1) * scale
    s_dn = jnp.concatenate([z64, -sin, z16, z32], axis=1) * scale
    return c, s_up, s_dn


def _rope_slabs(x, c, s_up, s_dn):
    half = MLA_ROPE // 2
    outs = []
    for h in range(MLA_HEADS):
        sl = x[:, h * LANES:(h + 1) * LANES]
        outs.append(sl * c + pltpu.roll(sl, half, 1) * s_up + pltpu.roll(sl, LANES - half, 1) * s_dn)
    return jnp.concatenate(outs, axis=1)


def _proj_kernel(x_ref, ln1_ref, wp_ref, cqg_ref, ckvg_ref, wuq_ref, wukk_ref, wukvt_ref,
                 qg_ref, kg_ref, qc_ref, qsu_ref, qsd_ref, kc_ref, ksu_ref, ksd_ref,
                 pkr_ref, seg128_ref, seg128t_ref, seg64_ref, seg64t_ref,
                 nq_gain_ref, nk_gain_ref, dup_ref, vpick_ref, ones_mla_ref, ones_nsa_ref,
                 qmla_ref, kmla_ref, vmla_ref, qnsa_ref, kdup_ref, vnsa_ref, kvc_ref, gate_ref):
    x = x_ref[...]
    h = (_rms(x, D_MODEL) * ln1_ref[...]).astype(BF16)
    proj = _dot(h, wp_ref[...])

    def head_norm(v, seg_ref, segt_ref, width):
        ss = _split_dot(v * v, seg_ref[...])
        rs = lax.rsqrt(ss * (1.0 / width) + EPS)
        return v * _split_dot(rs, segt_ref[...])

    cq = (_rms(proj[:, :MLA_Q_RANK], MLA_Q_RANK) * cqg_ref[...]).astype(BF16)
    q = head_norm(_dot(cq, wuq_ref[...]), seg128_ref, seg128t_ref, MLA_QK) * qg_ref[...]
    qmla_ref[...] = _rope_slabs(q, qc_ref[...], qsu_ref[...], qsd_ref[...]).astype(BF16)

    o_kv = MLA_Q_RANK
    ckv = (_rms(proj[:, o_kv:o_kv + MLA_KV_RANK], MLA_KV_RANK) * ckvg_ref[...]).astype(BF16)
    misc = proj[:, 640:768]
    k = _dot(ckv, wukk_ref[...]) + _split_dot(misc, pkr_ref[...])
    k = head_norm(k, seg128_ref, seg128t_ref, MLA_QK) * kg_ref[...]
    kmla_ref[...] = _rope_slabs(k, kc_ref[...], ksu_ref[...], ksd_ref[...]).astype(BF16)
    vmla_ref[0] = (_dot_nt(wukvt_ref[...], ckv) + ones_mla_ref[...]).astype(BF16)

    qb = head_norm(proj[:, 768:1280], seg64_ref, seg64t_ref, NSA_HEAD_DIM)
    qnsa_ref[...] = (qb * nq_gain_ref[...]).astype(BF16)

    kk = proj[:, 1280:1536]
    ss = _split_dot(kk * kk, seg64_ref[:256, :])
    rs = lax.rsqrt(ss * (1.0 / NSA_HEAD_DIM) + EPS)
    kn = kk * _split_dot(rs, seg64t_ref[:, :256]) * nk_gain_ref[...]
    kdup_ref[...] = _dot(kn.astype(BF16), dup_ref[...]).astype(BF16)
    vnsa_ref[0] = (_dot_nt(vpick_ref[...], proj[:, 1536:1792].astype(BF16)) + ones_nsa_ref[...]).astype(BF16)
    kvc_ref[...] = proj[:, 1792:2048]
    gate_ref[...] = 1.0 / (1.0 + jnp.exp(-misc))


def _seg_matrix(width, seg, cols=LANES):
    m = np.zeros((width, cols), np.float32)
    m[np.arange(width), np.arange(width) // seg] = 1.0
    return m


def _projections(x2d, ln1_g, w_in, cq_g, ckv_g, w_uq, w_ukv, q_gain, k_gain, nq_gain, nk_gain):
    t = x2d.shape[0]
    kv0 = 1184
    seg = lambda a, b: w_in[:, a:b]
    kvb = lambda br, kv, g: seg(kv0 + ((br * 2 + kv) * 2 + g) * 64, kv0 + ((br * 2 + kv) * 2 + g) * 64 + 64)
    wp = jnp.concatenate([
        seg(0, 640),
        seg(640, 672), seg(1952, 1976), jnp.zeros((D_MODEL, 72), F32),
        seg(672, 1184),
        kvb(1, 0, 0), kvb(1, 0, 1), kvb(2, 0, 0), kvb(2, 0, 1),
        kvb(1, 1, 0), kvb(1, 1, 1), kvb(2, 1, 0), kvb(2, 1, 1),
        kvb(0, 0, 0), kvb(0, 0, 1), kvb(0, 1, 0), kvb(0, 1, 1),
    ], axis=1).astype(BF16)
    pad = LANES - MLA_QK
    wuq = jnp.pad(w_uq.reshape(MLA_Q_RANK, MLA_HEADS, MLA_QK), ((0, 0), (0, 0), (0, pad)))
    wuq = wuq.reshape(MLA_Q_RANK, MLA_HEADS * LANES).astype(BF16)
    wkv = w_ukv.reshape(MLA_KV_RANK, MLA_HEADS, MLA_NOPE + MLA_V)
    wukk = jnp.pad(wkv[:, :, :MLA_NOPE], ((0, 0), (0, 0), (0, LANES - MLA_NOPE)))
    wukk = wukk.reshape(MLA_KV_RANK, MLA_HEADS * LANES).astype(BF16)
    wukvt = jnp.pad(wkv[:, :, MLA_NOPE:], ((0, 0), (0, 0), (0, V_ROWS - MLA_V)))
    wukvt = wukvt.reshape(MLA_KV_RANK, MLA_HEADS * V_ROWS).T.astype(BF16)
    slab_gain = lambda g: jnp.tile(jnp.pad(g, (0, pad)), MLA_HEADS)[None, :]
    qc, qsu, qsd = _rope_tables(MLA_QK ** -0.5 * LOG2E)
    kc, ksu, ksd = _rope_tables(1.0)
    pkr = np.zeros((LANES, MLA_HEADS * LANES), np.float32)
    for hh in range(MLA_HEADS):
        pkr[np.arange(MLA_ROPE), hh * LANES + MLA_NOPE + np.arange(MLA_ROPE)] = 1.0
    seg128 = _seg_matrix(MLA_HEADS * LANES, LANES)
    seg64 = _seg_matrix(NSA_HEADS * NSA_HEAD_DIM, NSA_HEAD_DIM)
    dup = np.zeros((256, 512), np.float32)
    for s in range(4):
        for rep in range(2):
            dup[s * 64 + np.arange(64), s * 128 + rep * 64 + np.arange(64)] = 1.0
    nqg = (jnp.tile(nq_gain, NSA_HEADS) * (NSA_HEAD_DIM ** -0.5 * LOG2E))[None, :]
    nkg = jnp.concatenate([nk_gain[1], nk_gain[1], nk_gain[2], nk_gain[2]])[None, :]
    vpick = np.zeros((4 * V_ROWS, 256), np.float32)
    for s in range(4):
        vpick[s * V_ROWS + np.arange(64), s * 64 + np.arange(64)] = 1.0

    def ones_col(heads):
        col = np.zeros((heads * V_ROWS, 1), np.float32)
        col[np.arange(heads) * V_ROWS + MLA_V] = 1.0
        return col

    rows = PROJ_ROWS
    n_pos = SEQ // rows
    full = lambda a: pl.BlockSpec(a.shape, lambda i: (0,) * a.ndim)
    tab = pl.BlockSpec((rows, LANES), lambda i: (i % n_pos, 0))
    consts = [ln1_g[None, :], wp, cq_g[None, :], ckv_g[None, :], wuq, wukk, wukvt,
              slab_gain(q_gain), slab_gain(k_gain)]
    tabs = [qc, qsu, qsd, kc, ksu, ksd]
    mats = [jnp.asarray(pkr, BF16), jnp.asarray(seg128, BF16), jnp.asarray(seg128.T, BF16),
            jnp.asarray(seg64, BF16), jnp.asarray(seg64.T, BF16), nqg, nkg, jnp.asarray(dup, BF16),
            jnp.asarray(vpick, BF16), jnp.asarray(ones_col(MLA_HEADS)), jnp.asarray(ones_col(4))]
    row_spec = lambda w: pl.BlockSpec((rows, w), lambda i: (i, 0))
    vt_spec = lambda w: pl.BlockSpec((1, w, rows), lambda i: (i, 0, 0))
    sds = jax.ShapeDtypeStruct
    return pl.pallas_call(
        _proj_kernel,
        grid=(t // rows,),
        in_specs=[row_spec(D_MODEL)] + [full(a) for a in consts] + [tab] * 6 + [full(a) for a in mats],
        out_specs=[row_spec(1024), row_spec(1024), vt_spec(MLA_HEADS * V_ROWS), row_spec(512), row_spec(512),
                   vt_spec(4 * V_ROWS), row_spec(256), row_spec(128)],
        out_shape=[sds((t, 1024), BF16), sds((t, 1024), BF16), sds((t // rows, MLA_HEADS * V_ROWS, rows), BF16),
                   sds((t, 512), BF16), sds((t, 512), BF16), sds((t // rows, 4 * V_ROWS, rows), BF16),
                   sds((t, 256), F32), sds((t, 128), F32)],
        compiler_params=pltpu.CompilerParams(dimension_semantics=("parallel",),
                                             vmem_limit_bytes=VMEM_LIMIT),
        name="projections",
    )(x2d, *consts, *tabs, *mats)


def _compress_kernel(xk_ref, xv_ref, pe_ref, w1_ref, w2k_ref, w2vt_ref, kg_ref, dup_ref, kc_ref, vct_ref):
    half = CMP_STRIDE * NSA_HEAD_DIM

    def hidden(x, which):
        pe = pe_ref[which]
        w1 = w1_ref[which]
        top = _dot((x + pe[:, :half]).astype(BF16), w1[:half])
        bot = _dot((x + pe[:, half:]).astype(BF16), w1[half:])
        hid = top + pltpu.roll(bot, N_CHUNK - 1, 0)
        return (hid * (1.0 / (1.0 + jnp.exp(-hid)))).astype(BF16)

    k = _dot(hidden(xk_ref[0, 0], 0), w2k_ref[...])
    k = _rms(k, NSA_HEAD_DIM) * kg_ref[...]
    kc_ref[0, 0] = _dot(k.astype(BF16), dup_ref[...]).astype(BF16)
    vct_ref[0, 0] = _dot_nt(w2vt_ref[...], hidden(xv_ref[0, 0], 1)).astype(BF16)


def _compress(kvc, cmp_pe, cmp_w1, cmp_w2, k_gain0):
    b = kvc.shape[0] // SEQ
    g_ = NSA_KV_HEADS
    xc = kvc.reshape(b, SEQ, 4, NSA_HEAD_DIM).transpose(0, 2, 1, 3)
    xc = xc.reshape(b, 4, N_CHUNK, CMP_STRIDE * NSA_HEAD_DIM)
    pe = cmp_pe.reshape(2, 1, CMP_BLOCK * NSA_HEAD_DIM)
    dup = np.zeros((NSA_HEAD_DIM, LANES), np.float32)
    dup[np.arange(64), np.arange(64)] = 1.0
    dup[np.arange(64), 64 + np.arange(64)] = 1.0
    full = lambda a: pl.BlockSpec(a.shape, lambda i, g: (0,) * a.ndim)
    consts = [pe, cmp_w1.astype(BF16), cmp_w2[0].astype(BF16), cmp_w2[1].T.astype(BF16),
              k_gain0[None, :], jnp.asarray(dup, BF16)]
    x_spec = lambda off: pl.BlockSpec((1, 1, N_CHUNK, CMP_STRIDE * NSA_HEAD_DIM),
                                      lambda i, g: (i, off + g, 0, 0))
    return pl.pallas_call(
        _compress_kernel,
        grid=(b, g_),
        in_specs=[x_spec(0), x_spec(g_)] + [full(a) for a in consts],
        out_specs=[pl.BlockSpec((1, 1, N_CHUNK, LANES), lambda i, g: (i, g, 0, 0)),
                   pl.BlockSpec((1, 1, NSA_HEAD_DIM, N_CHUNK), lambda i, g: (i, g, 0, 0))],
        out_shape=[jax.ShapeDtypeStruct((b, g_, N_CHUNK, LANES), BF16),
                   jax.ShapeDtypeStruct((b, g_, NSA_HEAD_DIM, N_CHUNK), BF16)],
        name="compression",
    )(xc, xc, *consts)


def _flash_init(m_ref, acc_ref):
    m_ref[...] = jnp.full(m_ref.shape, NEG, F32)
    acc_ref[...] = jnp.zeros(acc_ref.shape, F32)


def _flash_step(s, v, m_ref, acc_ref, chunk=KEY_CHUNK):
    for c in range(s.shape[0] // chunk):
        sc = s[c * chunk:(c + 1) * chunk]
        m_old = m_ref[c]
        m_new = jnp.maximum(m_old, jnp.max(sc, axis=0, keepdims=True))
        p = jnp.exp2(sc - m_new).astype(BF16)
        acc_ref[c] = jnp.exp2(m_old - m_new) * acc_ref[c] + _dot(v[:, c * chunk:(c + 1) * chunk], p)
        m_ref[c] = m_new


def _flash_softmax(s, m_ref, p_ref, alpha_ref, chunk=KEY_CHUNK):
    for c in range(s.shape[0] // chunk):
        sc = s[c * chunk:(c + 1) * chunk]
        m_old = m_ref[c]
        m_new = jnp.maximum(m_old, jnp.max(sc, axis=0, keepdims=True))
        alpha_ref[c] = jnp.exp2(m_old - m_new)
        p_ref[c * chunk:(c + 1) * chunk, :] = jnp.exp2(sc - m_new).astype(BF16)
        m_ref[c] = m_new


def _flash_accumulate(v, p_ref, alpha_ref, acc_ref, chunk=KEY_CHUNK):
    for c in range(p_ref.shape[0] // chunk):
        rows = slice(c * chunk, (c + 1) * chunk)
        acc_ref[c] = alpha_ref[c] * acc_ref[c] + _dot(v[:, rows], p_ref[rows, :])


def _flash_finish(m_ref, acc_ref):
    n = m_ref.shape[0]
    m = m_ref[0]
    for c in range(1, n):
        m = jnp.maximum(m, m_ref[c])
    acc = sum(jnp.exp2(m_ref[c] - m) * acc_ref[c] for c in range(n))
    return acc[:MLA_V] / acc[MLA_V:MLA_V + 1]


def _pipelined_pairs(sched_ref, row, n, scores, softmax, accumulate):
    assert n % 2 == 0

    def pair(i):
        i = jnp.minimum(i, n - 1)
        return sched_ref[row, i], sched_ref[row + 1, i]

    if accumulate is None:
        scores(*pair(0), 0)

        def body2(j, carry):
            i = 2 * j
            scores(*pair(i + 1), 1)
            softmax(*pair(i), 0)
            scores(*pair(i + 2), 0)
            softmax(*pair(i + 1), 1)
            return carry

        lax.fori_loop(0, n // 2, body2, 0)
        return

    scores(*pair(0), 0)
    scores(*pair(1), 1)
    softmax(*pair(0), 0)

    def body(j, carry):
        i = 2 * j
        scores(*pair(i + 2), 0)
        softmax(*pair(i + 1), 1)
        accumulate(*pair(i), 0)
        scores(*pair(i + 3), 1)
        softmax(*pair(i + 2), 0)
        accumulate(*pair(i + 1), 1)
        return carry

    lax.fori_loop(0, n // 2, body, 0)


def _mla_pairs():
    nq, nk = SEQ // MLA_Q, SEQ // MLA_K
    full = [(qt, kt) for kt in range(nk) for qt in range(nq) if (kt + 1) * MLA_K <= qt * MLA_Q]
    diag = [(qt, (qt * MLA_Q) // MLA_K) for qt in range(nq)]
    return full, diag


def _mla_kernel(sched_ref, q_ref, k_ref, vt_ref, eye_ref, o_ref, m_ref, acc_ref, s_ref):
    sub = MLA_K // V_TILE
    full, diag = _mla_pairs()
    rel = (lax.broadcasted_iota(jnp.int32, (MLA_K, MLA_Q), 1)
           - lax.broadcasted_iota(jnp.int32, (MLA_K, MLA_Q), 0))
    _flash_init(m_ref, acc_ref)

    def scores(qt, kt, slot, masked):
        qrows = pl.ds(pl.multiple_of(qt * MLA_Q, MLA_Q), MLA_Q)
        krows = pl.ds(pl.multiple_of(kt * MLA_K, MLA_K), MLA_K)
        for hh in range(2):
            cols = slice(hh * LANES, (hh + 1) * LANES)
            s = _dot_nt(k_ref[krows, cols], q_ref[qrows, cols])
            if masked:
                s = jnp.where(rel >= kt * MLA_K - qt * MLA_Q, s, NEG)
            s_ref[slot, hh] = s

    def update(qt, kt, slot):
        for hh in range(2):
            v = jnp.concatenate([vt_ref[kt * sub + i, hh * V_ROWS:(hh + 1) * V_ROWS, :] for i in range(sub)],
                                axis=1)
            _flash_step(s_ref[slot, hh], v, m_ref.at[hh, qt], acc_ref.at[hh, qt])

    _pipelined_pairs(sched_ref, 0, len(full), lambda qt, kt, slot: scores(qt, kt, slot, False), update, None)
    _pipelined_pairs(sched_ref, 2, len(diag), lambda qt, kt, slot: scores(qt, kt, slot, True), update, None)

    for qt in range(SEQ // MLA_Q):
        o_t = jnp.concatenate([_flash_finish(m_ref.at[hh, qt], acc_ref.at[hh, qt])
                               for hh in range(2)], axis=0)
        o_ref[qt * MLA_Q:(qt + 1) * MLA_Q, :] = _dot_nt(eye_ref[...], o_t.astype(BF16)).astype(BF16)


def _schedule(*pair_lists):
    n = max(len(p) for p in pair_lists)
    out = np.zeros((2 * len(pair_lists), n), np.int32)
    for i, pairs in enumerate(pair_lists):
        out[2 * i, :len(pairs)] = [a for a, _ in pairs]
        out[2 * i + 1, :len(pairs)] = [b for _, b in pairs]
    return jnp.asarray(out)


def _mla_attention(q, k, vt):
    b = q.shape[0] // SEQ
    nq = SEQ // MLA_Q
    nv = SEQ // V_TILE
    chunks = MLA_K // KEY_CHUNK
    return pl.pallas_call(
        _mla_kernel,
        grid=(b, MLA_HEADS // 2),
        in_specs=[pl.BlockSpec(memory_space=pltpu.SMEM),
                  pl.BlockSpec((SEQ, 2 * LANES), lambda i, hp: (i, hp)),
                  pl.BlockSpec((SEQ, 2 * LANES), lambda i, hp: (i, hp)),
                  pl.BlockSpec((nv, 2 * V_ROWS, V_TILE), lambda i, hp: (i, hp, 0)),
                  pl.BlockSpec((MLA_Q, MLA_Q), lambda i, hp: (0, 0))],
        out_specs=pl.BlockSpec((SEQ, LANES), lambda i, hp: (i, hp)),
        out_shape=jax.ShapeDtypeStruct((q.shape[0], MLA_HEADS * MLA_V), BF16),
        scratch_shapes=[pltpu.VMEM((2, nq, chunks, 1, MLA_Q), F32),
                        pltpu.VMEM((2, nq, chunks, V_ROWS, MLA_Q), F32),
                        pltpu.VMEM((2, 2, MLA_K, MLA_Q), F32)],
        compiler_params=pltpu.CompilerParams(
            dimension_semantics=("parallel", "parallel"), vmem_limit_bytes=VMEM_LIMIT),
        name="mla_attention",
    )(_schedule(*_mla_pairs()), q, k, vt, jnp.eye(MLA_Q, dtype=BF16))


def _stack_heads(q2):
    low = lax.broadcasted_iota(jnp.int32, (q2.shape[0], LANES), 1) < NSA_HEAD_DIM
    qa, qb = q2[:, :LANES], q2[:, LANES:]
    zero = jnp.zeros_like(qa)
    return jnp.concatenate([jnp.where(low, qa, zero), jnp.where(low, zero, qa),
                            jnp.where(low, qb, zero), jnp.where(low, zero, qb)], axis=0)


def _select_kernel(q_ref, kc_ref, vct_ref, tabc_ref, ovt_ref, negsel_ref, ocmp_ref):
    step = pl.program_id(2)
    t = SELECT_Q
    heads = range(NSA_GROUP)
    tiles = range(t // NSA_Q)
    q4 = _stack_heads(q_ref[...])

    bias_c = jnp.concatenate([tabc_ref[r, u * N_CHUNK:(u + 1) * N_CHUNK, :] for r in heads for u in tiles],
                             axis=1)
    valid = bias_c > 0.5 * NEG
    sc = jnp.where(valid, _dot_nt(kc_ref[0, 0], q4) + bias_c, NEG)
    mx = jnp.max(sc, axis=0, keepdims=True)
    p = jnp.where(valid, jnp.exp2(sc - mx), 0.0)
    pc = p / jnp.maximum(jnp.sum(p, axis=0, keepdims=True), 1e-30)
    o_cmp = _dot(vct_ref[0, 0], pc.astype(BF16)).astype(BF16)
    for u in tiles:
        ocmp_ref[0, 0, u] = jnp.concatenate(
            [o_cmp[:, r * t + u * NSA_Q:r * t + (u + 1) * NSA_Q] for r in heads], axis=0)

    psum = sum(pc[:, r * t:(r + 1) * t] for r in heads)
    imp = sum(_dot(ovt_ref[...], piece) for piece in _split(psum, 3))
    jj = lax.broadcasted_iota(jnp.int32, (N_SEL, t), 0)
    blk = (step * t + lax.broadcasted_iota(jnp.int32, (N_SEL, t), 1)) // SEL_BLOCK
    ok = jj <= blk
    forced = ok & ((jj == 0) | (jj == blk) | (jj == blk - 1))
    score = jnp.where(forced, FORCE_SCORE, jnp.where(ok, imp, -jnp.inf))
    rank = jnp.zeros((N_SEL, t), jnp.int32)
    for i in range(N_SEL):
        ci = score[i:i + 1, :]
        beats = (ci > score) | ((ci == score) & (jj > i))
        rank = rank + beats.astype(jnp.int32)
    negsel = jnp.where(ok & (rank < SEL_TOP_N), 0.0, NEG)
    for u in tiles:
        negsel_ref[0, 0, u] = negsel[:, u * NSA_Q:(u + 1) * NSA_Q]


def _nsa_select(q, kc, vct, tables):
    b = q.shape[0] // SEQ
    t = SELECT_Q
    steps = SEQ // t
    nq = SEQ // NSA_Q
    per = t // NSA_Q
    g_ = NSA_KV_HEADS
    c0 = np.arange(N_CHUNK)[None, :] * CMP_STRIDE
    j0 = np.arange(N_SEL)[:, None] * SEL_BLOCK
    ovt = np.clip(np.minimum(c0 + CMP_BLOCK, j0 + SEL_BLOCK) - np.maximum(c0, j0), 0, None) / CMP_BLOCK
    ovt[:, N_CHUNK - 1:] = 0.0
    return pl.pallas_call(
        _select_kernel,
        grid=(b, g_, steps),
        in_specs=[pl.BlockSpec((t, 2 * LANES), lambda i, g, j: (i * steps + j, g)),
                  pl.BlockSpec((1, 1, N_CHUNK, LANES), lambda i, g, j: (i, g, 0, 0)),
                  pl.BlockSpec((1, 1, NSA_HEAD_DIM, N_CHUNK), lambda i, g, j: (i, g, 0, 0)),
                  pl.BlockSpec((NSA_GROUP, per * N_CHUNK, LANES), lambda i, g, j: (g, j, 0)),
                  pl.BlockSpec((N_SEL, N_CHUNK), lambda i, g, j: (0, 0))],
        out_specs=[pl.BlockSpec((1, 1, per, N_SEL, NSA_Q), lambda i, g, j: (i, g, j, 0, 0)),
                   pl.BlockSpec((1, 1, per, NSA_GROUP * NSA_HEAD_DIM, NSA_Q), lambda i, g, j: (i, g, j, 0, 0))],
        out_shape=[jax.ShapeDtypeStruct((b, g_, nq, N_SEL, NSA_Q), F32),
                   jax.ShapeDtypeStruct((b, g_, nq, NSA_GROUP * NSA_HEAD_DIM, NSA_Q), BF16)],
        compiler_params=pltpu.CompilerParams(
            dimension_semantics=("parallel", "parallel", "parallel"), vmem_limit_bytes=VMEM_LIMIT),
        name="nsa_select",
    )(q, kc, vct, tables, jnp.asarray(ovt, BF16))


def _nsa_pairs():
    nq, nk = SEQ // NSA_Q, SEQ // NSA_K
    sel = [(qt, kt) for kt in range(nk) for qt in range(nq) if kt * NSA_K <= qt * NSA_Q]
    win = [(qt, kt) for kt in range(nk) for qt in range(nq)
           if max(qt * NSA_Q - (WINDOW - 1), 0) // NSA_K <= kt and kt * NSA_K <= qt * NSA_Q]
    return sel, win


def _attend_kernel(sched_ref, q_ref, ksel_ref, kwin_ref, vsel_ref, vwin_ref, negsel_ref, ocmp_ref, gate_ref,
                   tabw_ref, gx_ref, eye_ref, o_ref, m_ref, acc_ref, s_ref, p_ref, alpha_ref):
    t = NSA_Q
    heads = range(NSA_GROUP)
    sel, win = _nsa_pairs()
    _flash_init(m_ref, acc_ref)

    def scores(qt, kt, slot, k_ref, selected):
        q4 = _stack_heads(q_ref[pl.ds(pl.multiple_of(qt * t, t), t), :])
        s = _dot_nt(k_ref[pl.ds(pl.multiple_of(kt * NSA_K, NSA_K), NSA_K), :], q4)
        off = qt - kt * (NSA_K // NSA_Q)
        if selected:
            off = jnp.minimum(off, SEL_FAR_TILE)
        s = s + jnp.concatenate([tabw_ref[r, off] for r in heads], axis=1)
        if selected:
            per = NSA_K // SEL_BLOCK
            rows = [negsel_ref[0, 0, qt, pl.ds(kt * per + bb, 1), :] for bb in range(per)]
            s = jnp.concatenate(
                [s[bb * SEL_BLOCK:(bb + 1) * SEL_BLOCK] + jnp.concatenate([rows[bb]] * NSA_GROUP, axis=1)
                 for bb in range(per)], axis=0)
        s_ref[slot] = s

    def softmax(qt, kt, slot, branch):
        _flash_softmax(s_ref[slot], m_ref.at[branch, qt], p_ref.at[slot], alpha_ref.at[slot], chunk=NSA_K)

    def accumulate(qt, kt, slot, v_ref, branch):
        _flash_accumulate(v_ref[kt], p_ref.at[slot], alpha_ref.at[slot], acc_ref.at[branch, qt], chunk=NSA_K)

    _pipelined_pairs(sched_ref, 0, len(sel),
                     lambda qt, kt, slot: scores(qt, kt, slot, ksel_ref, True),
                     lambda qt, kt, slot: softmax(qt, kt, slot, 0),
                     lambda qt, kt, slot: accumulate(qt, kt, slot, vsel_ref, 0))
    _pipelined_pairs(sched_ref, 2, len(win),
                     lambda qt, kt, slot: scores(qt, kt, slot, kwin_ref, False),
                     lambda qt, kt, slot: softmax(qt, kt, slot, 1),
                     lambda qt, kt, slot: accumulate(qt, kt, slot, vwin_ref, 1))

    w = NSA_GROUP * NSA_HEAD_DIM
    for qt in range(SEQ // t):
        branches = [_flash_finish(m_ref.at[br, qt], acc_ref.at[br, qt]).astype(BF16)
                    for br in range(2)]
        stacked = jnp.concatenate([ocmp_ref[0, 0, qt]] + [o[:, r * t:(r + 1) * t] for o in branches for r in heads],
                                  axis=0)
        tok = _dot_nt(eye_ref[...], stacked)
        g = _split_dot(gate_ref[qt * t:(qt + 1) * t, :], gx_ref[0])
        out = g[:, :w] * tok[:, :w] + g[:, w:2 * w] * tok[:, w:2 * w] + g[:, 2 * w:] * tok[:, 2 * w:]
        o_ref[qt * t:(qt + 1) * t, :] = out.astype(BF16)


def _nsa_attend(q, kdup, vt, negsel, ocmp, gates, tables):
    b = q.shape[0] // SEQ
    t = NSA_Q
    nq = SEQ // t
    nv = SEQ // V_TILE
    g_ = NSA_KV_HEADS
    tabw = tables[:, SEQ:].reshape(NSA_HEADS, N_BIAS_TILES, NSA_K, LANES)
    gx = np.zeros((g_, LANES, N_BRANCH * 256), np.float32)
    for gg in range(g_):
        for br in range(N_BRANCH):
            for r in range(NSA_GROUP):
                src = MLA_ROPE + br * NSA_HEADS + gg * NSA_GROUP + r
                gx[gg, src, br * 256 + r * 64: br * 256 + (r + 1) * 64] = 1.0
    k_spec = lambda off: pl.BlockSpec((SEQ, LANES), lambda i, g: (i, off + g))
    v_spec = lambda off: pl.BlockSpec((nv, V_ROWS, V_TILE), lambda i, g: (i, off + g, 0))
    return pl.pallas_call(
        _attend_kernel,
        grid=(b, g_),
        in_specs=[pl.BlockSpec(memory_space=pltpu.SMEM),
                  pl.BlockSpec((SEQ, 2 * LANES), lambda i, g: (i, g)),
                  k_spec(0), k_spec(2), v_spec(0), v_spec(2),
                  pl.BlockSpec((1, 1, nq, N_SEL, t), lambda i, g: (i, g, 0, 0, 0)),
                  pl.BlockSpec((1, 1, nq, NSA_GROUP * NSA_HEAD_DIM, t), lambda i, g: (i, g, 0, 0, 0)),
                  pl.BlockSpec((SEQ, LANES), lambda i, g: (i, 0)),
                  pl.BlockSpec((NSA_GROUP, N_BIAS_TILES, NSA_K, LANES), lambda i, g: (g, 0, 0, 0)),
                  pl.BlockSpec((1, LANES, N_BRANCH * 256), lambda i, g: (g, 0, 0)),
                  pl.BlockSpec((t, t), lambda i, g: (0, 0))],
        out_specs=pl.BlockSpec((SEQ, 2 * LANES), lambda i, g: (i, g)),
        out_shape=jax.ShapeDtypeStruct((q.shape[0], NSA_HEADS * NSA_HEAD_DIM), BF16),
        scratch_shapes=[pltpu.VMEM((2, nq, 1, 1, NSA_GROUP * t), F32),
                        pltpu.VMEM((2, nq, 1, V_ROWS, NSA_GROUP * t), F32),
                        pltpu.VMEM((2, NSA_K, NSA_GROUP * t), F32),
                        pltpu.VMEM((2, NSA_K, NSA_GROUP * t), BF16),
                        pltpu.VMEM((2, 1, 1, NSA_GROUP * t), F32)],
        compiler_params=pltpu.CompilerParams(
            dimension_semantics=("parallel", "parallel"), vmem_limit_bytes=VMEM_LIMIT),
        name="nsa_attend",
    )(_schedule(*_nsa_pairs()), q, kdup, kdup, vt, vt, negsel, ocmp, gates, tabw,
      jnp.asarray(gx, BF16), jnp.eye(t, dtype=BF16))


def _tail_kernel(x_ref, oa_ref, ob_ref, ga_ref, gb_ref, woa_ref, wob_ref, ln2_ref, wup_ref, wdn_ref,
                 out_ref, h2_ref):
    width = MLA_HEADS * MLA_V
    a = (_rms(oa_ref[...].astype(F32), width) * ga_ref[...]).astype(BF16)
    b = (_rms(ob_ref[...].astype(F32), width) * gb_ref[...]).astype(BF16)
    x1 = x_ref[...] + _dot(a, woa_ref[...]) + _dot(b, wob_ref[...])
    h2_ref[...] = (_rms(x1, D_MODEL) * ln2_ref[...]).astype(BF16)
    out_ref[...] = x1

    def body(f, carry):
        u = jnp.maximum(_dot(h2_ref[...], wup_ref[f]), 0.0)
        out_ref[...] += _dot((u * u).astype(BF16), wdn_ref[f])
        return carry

    lax.fori_loop(0, D_FF // FF_CHUNK, body, 0)


def _tail(x2d, o_a, o_b, gn_a, gn_b, w_o, ln2_g, w_up, w_down):
    t = x2d.shape[0]
    rows = TAIL_ROWS
    width = MLA_HEADS * MLA_V
    once = lambda a: pl.BlockSpec(a.shape, lambda i: (0,) * a.ndim)
    nf = D_FF // FF_CHUNK
    wup = w_up.astype(BF16).reshape(D_MODEL, nf, FF_CHUNK).transpose(1, 0, 2)
    wdn = w_down.astype(BF16).reshape(nf, FF_CHUNK, D_MODEL)
    consts = [gn_a[None, :], gn_b[None, :], w_o[:width].astype(BF16), w_o[width:].astype(BF16),
              ln2_g[None, :], wup, wdn]
    row_spec = lambda w: pl.BlockSpec((rows, w), lambda i: (i, 0))
    return pl.pallas_call(
        _tail_kernel,
        grid=(t // rows,),
        in_specs=[row_spec(D_MODEL), row_spec(width), row_spec(width)] + [once(a) for a in consts],
        out_specs=row_spec(D_MODEL),
        out_shape=jax.ShapeDtypeStruct((t, D_MODEL), F32),
        scratch_shapes=[pltpu.VMEM((rows, D_MODEL), BF16)],
        compiler_params=pltpu.CompilerParams(dimension_semantics=("parallel",),
                                             vmem_limit_bytes=VMEM_LIMIT),
        name="tail",
    )(x2d, o_a, o_b, *consts)


def kernel(x, ln1_g, w_in, mla_cq_norm_g, mla_ckv_norm_g, mla_w_uq, mla_w_ukv, mla_q_gain, mla_k_gain,
           nsa_q_gain, nsa_k_gain, nsa_cmp_pe, nsa_cmp_w1, nsa_cmp_w2, rel_bias, grp_norm_mla,
           grp_norm_nsa, w_o, ln2_g, w_up, w_down):
    b, s, d = x.shape
    assert (s, d) == (SEQ, D_MODEL) and ln1_g.shape[0] == 1
    x2d = x.reshape(b * s, d)
    tables = _bias_tables(rel_bias)
    q_mla, k_mla, vt_mla, q_nsa, kdup, vt_nsa, kvc, gates = _projections(
        x2d, ln1_g[0], w_in[0], mla_cq_norm_g[0], mla_ckv_norm_g[0], mla_w_uq[0], mla_w_ukv[0],
        mla_q_gain[0], mla_k_gain[0], nsa_q_gain[0], nsa_k_gain[0])
    kc, vct = _compress(kvc, nsa_cmp_pe[0], nsa_cmp_w1[0], nsa_cmp_w2[0], nsa_k_gain[0, 0])
    o_a = _mla_attention(q_mla, k_mla, vt_mla)
    negsel, ocmp = _nsa_select(q_nsa, kc, vct, tables)
    o_b = _nsa_attend(q_nsa, kdup, vt_nsa, negsel, ocmp, gates, tables)
    out = _tail(x2d, o_a, o_b, grp_norm_mla[0], grp_norm_nsa[0], w_o[0], ln2_g[0], w_up[0], w_down[0])
    return out.reshape(b, s, d)
```

```python
import math

import numpy as np
import jax
import jax.numpy as jnp
from jax import lax
from jax.experimental import pallas as pl
from jax.experimental.pallas import tpu as pltpu

F32 = jnp.float32
BF16 = jnp.bfloat16

D_MODEL = 1024
SEQ = 2048
MLA_HEADS = 8
MLA_NOPE = 64
MLA_ROPE = 32
MLA_V = 64
MLA_QK = MLA_NOPE + MLA_ROPE
MLA_Q_RANK = 384
MLA_KV_RANK = 256
ROPE_THETA = 10000.0
NSA_HEADS = 8
NSA_KV_HEADS = 2
NSA_GROUP = NSA_HEADS // NSA_KV_HEADS
NSA_HEAD_DIM = 64
N_BRANCH = 3
CMP_BLOCK = 32
CMP_STRIDE = 16
CMP_HIDDEN = 128
SEL_BLOCK = 64
SEL_TOP_N = 16
WINDOW = 512
FORCE_SCORE = 1e4
REL_BUCKETS = 32
REL_MAX_DIST = 128
D_FF = 4 * D_MODEL
EPS = 1e-6

LANES = 128
NEG = -1e30
N_CHUNK = SEQ // CMP_STRIDE
N_SEL = SEQ // SEL_BLOCK
NSA_Q = 128
NSA_K = 256
SELECT_Q = 512
N_BIAS_TILES = (WINDOW + NSA_K - NSA_Q) // NSA_Q + 1
SEL_FAR_TILE = (REL_MAX_DIST + NSA_K) // NSA_Q
WIN_ROWS = N_BIAS_TILES * NSA_K
MLA_Q = 256
MLA_K = 512
V_TILE = 256
PROJ_ROWS = 256
TAIL_ROWS = 512
FF_CHUNK = 1024
KEY_CHUNK = 256
V_ROWS = 80
LOG2E = math.log2(math.e)
VMEM_LIMIT = 56 * 1024 * 1024

assert PROJ_ROWS == V_TILE and MLA_K % V_TILE == 0 and NSA_K == V_TILE


def _dot(a, b):
    return jnp.dot(a, b, preferred_element_type=F32)


def _dot_nt(a, b):
    return lax.dot_general(a, b, (((1,), (1,)), ((), ())), preferred_element_type=F32)


def _split(a, terms):
    pieces = []
    rem = a
    for _ in range(terms):
        piece = rem.astype(BF16)
        pieces.append(piece)
        rem = rem - piece.astype(F32)
    return pieces


def _split_dot(a, b, terms=2):
    return sum(_dot(p, b) for p in _split(a, terms))


def _rms(x, width):
    return x * lax.rsqrt(jnp.sum(x * x, axis=-1, keepdims=True) * (1.0 / width) + EPS)


def _t5_bucket_np(dist):
    n = np.maximum(dist, 0)
    max_exact = REL_BUCKETS // 2
    large = max_exact + (np.log(np.maximum(n, 1).astype(np.float32) / max_exact)
                         / math.log(REL_MAX_DIST / max_exact)
                         * (REL_BUCKETS - max_exact)).astype(np.int32)
    large = np.minimum(large, REL_BUCKETS - 1)
    return np.where(n < max_exact, n, large).astype(np.int32)


def _bias_index_table():
    i = np.arange(NSA_Q)[None, :]
    parts = []
    j = np.arange(NSA_K)[:, None]
    for off in range(N_BIAS_TILES):
        d = off * NSA_Q + i - j
        parts.append(np.where((d >= 0) & (d < WINDOW), _t5_bucket_np(d), -1))
    c = np.arange(N_CHUNK)[:, None]
    for qt in range(SEQ // NSA_Q):
        dist_c = qt * NSA_Q + i - (c * CMP_STRIDE + CMP_BLOCK - 1)
        parts.append(np.where((dist_c >= 0) & (c < N_CHUNK - 1), _t5_bucket_np(dist_c), -1))
    return np.concatenate(parts, axis=0).astype(np.int32)


def _bias_table_kernel(rb_ref, idx_ref, out_ref):
    h = pl.program_id(0)
    idx = idx_ref[...]
    acc = jnp.full(idx.shape, NEG, F32)
    for b in range(REL_BUCKETS):
        acc = jnp.where(idx == b, rb_ref[h, b] * LOG2E, acc)
    out_ref[0] = acc


def _bias_tables(rel_bias):
    idx = jnp.asarray(_bias_index_table())
    rows = idx.shape[0]
    blk = SELECT_Q
    return pl.pallas_call(
        _bias_table_kernel,
        grid=(NSA_HEADS, rows // blk),
        in_specs=[pl.BlockSpec(memory_space=pltpu.SMEM),
                  pl.BlockSpec((blk, LANES), lambda h, r: (r, 0))],
        out_specs=pl.BlockSpec((1, blk, LANES), lambda h, r: (h, r, 0)),
        out_shape=jax.ShapeDtypeStruct((NSA_HEADS, rows, LANES), F32),
        name="bias_tables",
    )(rel_bias.T, idx)


ROPE_HALF = MLA_ROPE // 2
MISC_KR = 64


def _rope_tables(scale):
    inv = 1.0 / (ROPE_THETA ** (jnp.arange(ROPE_HALF, dtype=F32) / ROPE_HALF))
    ang = jnp.arange(SEQ, dtype=F32)[:, None] * inv[None, :]
    cos, sin = jnp.cos(ang), jnp.sin(ang)
    ones = jnp.ones((SEQ, MLA_NOPE), F32)
    z32 = jnp.zeros((SEQ, LANES - MLA_QK), F32)
    z64 = jnp.zeros((SEQ, MLA_NOPE), F32)
    c = jnp.concatenate([ones, cos, cos, z32], axis=1) * scale
    s = jnp.concatenate([z64, -sin, sin, z32], axis=1) * scale
    return c, s


def _mla_heads(x, extra, gain_ref, c, s):
    in_head = lax.broadcasted_iota(jnp.int32, (x.shape[0], LANES), 1) < MLA_QK
    outs = []
    for h in range(MLA_HEADS):
        cols = slice(h * LANES, (h + 1) * LANES)
        sl = x[:, cols] if extra is None else x[:, cols] + extra
        ss = jnp.sum(jnp.where(in_head, sl * sl, 0.0), axis=-1, keepdims=True)
        xn = sl * lax.rsqrt(ss * (1.0 / MLA_QK) + EPS) * gain_ref[:, cols]
        outs.append(xn * c + pltpu.roll(xn, LANES - ROPE_HALF, 1) * s)
    return jnp.concatenate(outs, axis=1)


def _pair_norm(x, gain_ref):
    low = lax.broadcasted_iota(jnp.int32, (x.shape[0], LANES), 1) < NSA_HEAD_DIM
    outs = []
    for j in range(x.shape[1] // LANES):
        cols = slice(j * LANES, (j + 1) * LANES)
        sl = x[:, cols]
        sq = sl * sl
        tot = jnp.sum(sq, axis=-1, keepdims=True)
        lo = jnp.sum(jnp.where(low, sq, 0.0), axis=-1, keepdims=True)
        rs = jnp.where(low, lax.rsqrt(lo * (1.0 / NSA_HEAD_DIM) + EPS),
                       lax.rsqrt((tot - lo) * (1.0 / NSA_HEAD_DIM) + EPS))
        outs.append(sl * rs * gain_ref[:, cols])
    return jnp.concatenate(outs, axis=1)


def _proj_kernel(x_ref, ln1_ref, wp_ref, cqg_ref, ckvg_ref, wuq_ref, wukk_ref, wukvt_ref,
                 qg_ref, kg_ref, qc_ref, qs_ref, kc_ref, ks_ref,
                 nq_gain_ref, nk_gain_ref, dup_ref, vpick_ref, ones_mla_ref, ones_nsa_ref,
                 qmla_ref, kmla_ref, vmla_ref, qnsa_ref, kdup_ref, vnsa_ref, kck_ref, kcv_ref, gate_ref):
    x = x_ref[...]
    h = (_rms(x, D_MODEL) * ln1_ref[...]).astype(BF16)
    proj = _dot(h, wp_ref[...])

    cq = (_rms(proj[:, :MLA_Q_RANK], MLA_Q_RANK) * cqg_ref[...]).astype(BF16)
    qmla_ref[...] = _mla_heads(_dot(cq, wuq_ref[...]), None, qg_ref, qc_ref[...], qs_ref[...]).astype(BF16)

    o_kv = MLA_Q_RANK
    ckv = (_rms(proj[:, o_kv:o_kv + MLA_KV_RANK], MLA_KV_RANK) * ckvg_ref[...]).astype(BF16)
    misc = proj[:, 640:768]
    lane = lax.broadcasted_iota(jnp.int32, misc.shape, 1)
    k_rope = jnp.where((lane >= MISC_KR) & (lane < MISC_KR + MLA_ROPE + ROPE_HALF), misc, 0.0)
    kmla_ref[...] = _mla_heads(_dot(ckv, wukk_ref[...]), k_rope, kg_ref, kc_ref[...], ks_ref[...]).astype(BF16)
    vmla_ref[0] = (_dot_nt(wukvt_ref[...], ckv) + ones_mla_ref[...]).astype(BF16)

    qnsa_ref[...] = _pair_norm(proj[:, 768:1280], nq_gain_ref).astype(BF16)

    kn = _pair_norm(proj[:, 1280:1536], nk_gain_ref)
    kdup_ref[...] = _dot(kn.astype(BF16), dup_ref[...]).astype(BF16)
    vnsa_ref[0] = (_dot_nt(vpick_ref[...], proj[:, 1536:1792].astype(BF16)) + ones_nsa_ref[...]).astype(BF16)
    kck_ref[...] = proj[:, 1792:1920]
    kcv_ref[...] = proj[:, 1920:2048]
    gate_ref[...] = 1.0 / (1.0 + jnp.exp(-misc))


def _projections(x2d, ln1_g, w_in, cq_g, ckv_g, w_uq, w_ukv, q_gain, k_gain, nq_gain, nk_gain):
    t = x2d.shape[0]
    kv0 = 1184
    seg = lambda a, b: w_in[:, a:b]
    kvb = lambda br, kv, g: seg(kv0 + ((br * 2 + kv) * 2 + g) * 64, kv0 + ((br * 2 + kv) * 2 + g) * 64 + 64)
    zeros = lambda n: jnp.zeros((D_MODEL, n), F32)
    wp = jnp.concatenate([
        seg(0, 640),
        seg(1952, 1976), zeros(MISC_KR - 24), seg(640, 672), seg(640, 640 + ROPE_HALF), zeros(ROPE_HALF),
        seg(672, 1184),
        kvb(1, 0, 0), kvb(1, 0, 1), kvb(2, 0, 0), kvb(2, 0, 1),
        kvb(1, 1, 0), kvb(1, 1, 1), kvb(2, 1, 0), kvb(2, 1, 1),
        kvb(0, 0, 0), kvb(0, 0, 1), kvb(0, 1, 0), kvb(0, 1, 1),
    ], axis=1).astype(BF16)
    wq3 = w_uq.reshape(MLA_Q_RANK, MLA_HEADS, MLA_QK)
    wuq = jnp.concatenate([wq3, wq3[:, :, MLA_NOPE:MLA_NOPE + ROPE_HALF],
                           jnp.zeros((MLA_Q_RANK, MLA_HEADS, ROPE_HALF), F32)], axis=2)
    wuq = wuq.reshape(MLA_Q_RANK, MLA_HEADS * LANES).astype(BF16)
    wkv = w_ukv.reshape(MLA_KV_RANK, MLA_HEADS, MLA_NOPE + MLA_V)
    wukk = jnp.pad(wkv[:, :, :MLA_NOPE], ((0, 0), (0, 0), (0, LANES - MLA_NOPE)))
    wukk = wukk.reshape(MLA_KV_RANK, MLA_HEADS * LANES).astype(BF16)
    wukvt = jnp.pad(wkv[:, :, MLA_NOPE:], ((0, 0), (0, 0), (0, V_ROWS - MLA_V)))
    wukvt = wukvt.reshape(MLA_KV_RANK, MLA_HEADS * V_ROWS).T.astype(BF16)
    slab_gain = lambda g: jnp.tile(jnp.concatenate(
        [g, g[MLA_NOPE:MLA_NOPE + ROPE_HALF], jnp.zeros((ROPE_HALF,), F32)]), MLA_HEADS)[None, :]
    qc, qs = _rope_tables(MLA_QK ** -0.5 * LOG2E)
    kc, ks = _rope_tables(1.0)
    dup = np.zeros((256, 512), np.float32)
    for s in range(4):
        for rep in range(2):
            dup[s * 64 + np.arange(64), s * 128 + rep * 64 + np.arange(64)] = 1.0
    nqg = (jnp.tile(nq_gain, NSA_HEADS) * (NSA_HEAD_DIM ** -0.5 * LOG2E))[None, :]
    nkg = jnp.concatenate([nk_gain[1], nk_gain[1], nk_gain[2], nk_gain[2]])[None, :]
    vpick = np.zeros((4 * V_ROWS, 256), np.float32)
    for s in range(4):
        vpick[s * V_ROWS + np.arange(64), s * 64 + np.arange(64)] = 1.0

    def ones_col(heads):
        col = np.zeros((heads * V_ROWS, 1), np.float32)
        col[np.arange(heads) * V_ROWS + MLA_V] = 1.0
        return col

    rows = PROJ_ROWS
    n_pos = SEQ // rows
    full = lambda a: pl.BlockSpec(a.shape, lambda i: (0,) * a.ndim)
    tab = pl.BlockSpec((rows, LANES), lambda i: (i % n_pos, 0))
    consts = [ln1_g[None, :], wp, cq_g[None, :], ckv_g[None, :], wuq, wukk, wukvt,
              slab_gain(q_gain), slab_gain(k_gain)]
    tabs = [qc, qs, kc, ks]
    mats = [nqg, nkg, jnp.asarray(dup, BF16),
            jnp.asarray(vpick, BF16), jnp.asarray(ones_col(MLA_HEADS)), jnp.asarray(ones_col(4))]
    row_spec = lambda w: pl.BlockSpec((rows, w), lambda i: (i, 0))
    vt_spec = lambda w: pl.BlockSpec((1, w, rows), lambda i: (i, 0, 0))
    sds = jax.ShapeDtypeStruct
    return pl.pallas_call(
        _proj_kernel,
        grid=(t // rows,),
        in_specs=[row_spec(D_MODEL)] + [full(a) for a in consts] + [tab] * 4 + [full(a) for a in mats],
        out_specs=[row_spec(1024), row_spec(1024), vt_spec(MLA_HEADS * V_ROWS), row_spec(512), row_spec(512),
                   vt_spec(4 * V_ROWS), row_spec(128), row_spec(128), row_spec(128)],
        out_shape=[sds((t, 1024), BF16), sds((t, 1024), BF16), sds((t // rows, MLA_HEADS * V_ROWS, rows), BF16),
                   sds((t, 512), BF16), sds((t, 512), BF16), sds((t // rows, 4 * V_ROWS, rows), BF16),
                   sds((t, 128), F32), sds((t, 128), F32), sds((t, 128), F32)],
        compiler_params=pltpu.CompilerParams(dimension_semantics=("parallel",),
                                             vmem_limit_bytes=VMEM_LIMIT),
        name="projections",
    )(x2d, *consts, *tabs, *mats)


def _compress_kernel(xk_ref, xv_ref, pe_ref, w1_ref, w2_ref, kg_ref, dupk_ref, pickv_ref, kc_ref, vct_ref):
    def mlp(x_ref, which):
        top = bot = None
        for t in range(CMP_STRIDE):
            xt = x_ref[pl.ds(t, N_CHUNK, stride=CMP_STRIDE), :]
            a = _dot((xt + pe_ref[which, t]).astype(BF16), w1_ref[which, t])
            b = _dot((xt + pe_ref[which, CMP_STRIDE + t]).astype(BF16), w1_ref[which, CMP_STRIDE + t])
            top = a if top is None else top + a
            bot = b if bot is None else bot + b
        hid = top + pltpu.roll(bot, N_CHUNK - 1, 0)
        act = (hid * (1.0 / (1.0 + jnp.exp(-hid)))).astype(BF16)
        return _dot(act, w2_ref[which])

    k = _pair_norm(mlp(xk_ref, 0), kg_ref).astype(BF16)
    v = mlp(xv_ref, 1).astype(BF16)
    for g in range(NSA_KV_HEADS):
        kc_ref[0, g] = _dot(k, dupk_ref[g]).astype(BF16)
        vct_ref[0, g] = _dot_nt(pickv_ref[g], v).astype(BF16)


def _compress(kck, kcv, cmp_pe, cmp_w1, cmp_w2, k_gain0):
    b = kck.shape[0] // SEQ
    g_ = NSA_KV_HEADS
    d = NSA_HEAD_DIM
    pe = jnp.tile(cmp_pe, (1, 1, g_))[:, :, None, :]
    w1 = cmp_w1.reshape(2, CMP_BLOCK, d, CMP_HIDDEN)
    z1 = jnp.zeros_like(w1)
    w1 = jnp.concatenate([jnp.concatenate([w1, z1], axis=3), jnp.concatenate([z1, w1], axis=3)], axis=2)
    z2 = jnp.zeros_like(cmp_w2)
    w2 = jnp.concatenate([jnp.concatenate([cmp_w2, z2], axis=2), jnp.concatenate([z2, cmp_w2], axis=2)], axis=1)
    dupk = np.zeros((g_, LANES, LANES), np.float32)
    pickv = np.zeros((g_, d, LANES), np.float32)
    for g in range(g_):
        dupk[g, g * d + np.arange(d), np.arange(d)] = 1.0
        dupk[g, g * d + np.arange(d), d + np.arange(d)] = 1.0
        pickv[g, np.arange(d), g * d + np.arange(d)] = 1.0
    full = lambda a: pl.BlockSpec(a.shape, lambda i: (0,) * a.ndim)
    consts = [pe, w1.astype(BF16), w2.astype(BF16), jnp.tile(k_gain0, g_)[None, :],
              jnp.asarray(dupk, BF16), jnp.asarray(pickv, BF16)]
    x_spec = pl.BlockSpec((SEQ, LANES), lambda i: (i, 0))
    return pl.pallas_call(
        _compress_kernel,
        grid=(b,),
        in_specs=[x_spec, x_spec] + [full(a) for a in consts],
        out_specs=[pl.BlockSpec((1, g_, N_CHUNK, LANES), lambda i: (i, 0, 0, 0)),
                   pl.BlockSpec((1, g_, d, N_CHUNK), lambda i: (i, 0, 0, 0))],
        out_shape=[jax.ShapeDtypeStruct((b, g_, N_CHUNK, LANES), BF16),
                   jax.ShapeDtypeStruct((b, g_, d, N_CHUNK), BF16)],
        compiler_params=pltpu.CompilerParams(dimension_semantics=("parallel",)),
        name="compression",
    )(kck, kcv, *consts)


def _flash_init(m_ref, acc_ref):
    m_ref[...] = jnp.full(m_ref.shape, NEG, F32)
    acc_ref[...] = jnp.zeros(acc_ref.shape, F32)


def _flash_step(s, v, m_ref, acc_ref, chunk=KEY_CHUNK):
    for c in range(s.shape[0] // chunk):
        sc = s[c * chunk:(c + 1) * chunk]
        m_old = m_ref[c]
        m_new = jnp.maximum(m_old, jnp.max(sc, axis=0, keepdims=True))
        p = jnp.exp2(sc - m_new).astype(BF16)
        acc_ref[c] = jnp.exp2(m_old - m_new) * acc_ref[c] + _dot(v[:, c * chunk:(c + 1) * chunk], p)
        m_ref[c] = m_new


def _flash_softmax(s, m_ref, p_ref, alpha_ref, chunk=KEY_CHUNK):
    for c in range(s.shape[0] // chunk):
        sc = s[c * chunk:(c + 1) * chunk]
        m_old = m_ref[c]
        m_new = jnp.maximum(m_old, jnp.max(sc, axis=0, keepdims=True))
        alpha_ref[c] = jnp.exp2(m_old - m_new)
        p_ref[c * chunk:(c + 1) * chunk, :] = jnp.exp2(sc - m_new).astype(BF16)
        m_ref[c] = m_new


def _flash_accumulate(v, p_ref, alpha_ref, acc_ref, chunk=KEY_CHUNK):
    for c in range(p_ref.shape[0] // chunk):
        rows = slice(c * chunk, (c + 1) * chunk)
        acc_ref[c] = alpha_ref[c] * acc_ref[c] + _dot(v[:, rows], p_ref[rows, :])


def _flash_finish(m_ref, acc_ref):
    n = m_ref.shape[0]
    m = m_ref[0]
    for c in range(1, n):
        m = jnp.maximum(m, m_ref[c])
    acc = sum(jnp.exp2(m_ref[c] - m) * acc_ref[c] for c in range(n))
    return acc[:MLA_V] / acc[MLA_V:MLA_V + 1]


def _pipelined_pairs(sched_ref, row, n, scores, softmax, accumulate):
    assert n % 2 == 0

    def pair(i):
        i = jnp.minimum(i, n - 1)
        return sched_ref[row, i], sched_ref[row + 1, i]

    if accumulate is None:
        scores(*pair(0), 0)

        def body2(j, carry):
            i = 2 * j
            scores(*pair(i + 1), 1)
            softmax(*pair(i), 0)
            scores(*pair(i + 2), 0)
            softmax(*pair(i + 1), 1)
            return carry

        lax.fori_loop(0, n // 2, body2, 0)
        return

    scores(*pair(0), 0)
    scores(*pair(1), 1)
    softmax(*pair(0), 0)

    def body(j, carry):
        i = 2 * j
        scores(*pair(i + 2), 0)
        softmax(*pair(i + 1), 1)
        accumulate(*pair(i), 0)
        scores(*pair(i + 3), 1)
        softmax(*pair(i + 2), 0)
        accumulate(*pair(i + 1), 1)
        return carry

    lax.fori_loop(0, n // 2, body, 0)


def _mla_pairs():
    nq, nk = SEQ // MLA_Q, SEQ // MLA_K
    full = [(qt, kt) for kt in range(nk) for qt in range(nq) if (kt + 1) * MLA_K <= qt * MLA_Q]
    diag = [(qt, (qt * MLA_Q) // MLA_K) for qt in range(nq)]
    return full, diag


def _mla_kernel(sched_ref, q_ref, k_ref, vt_ref, eye_ref, o_ref, m_ref, acc_ref, s_ref):
    sub = MLA_K // V_TILE
    full, diag = _mla_pairs()
    rel = (lax.broadcasted_iota(jnp.int32, (MLA_K, MLA_Q), 1)
           - lax.broadcasted_iota(jnp.int32, (MLA_K, MLA_Q), 0))
    _flash_init(m_ref, acc_ref)

    def scores(qt, kt, slot, masked):
        qrows = pl.ds(pl.multiple_of(qt * MLA_Q, MLA_Q), MLA_Q)
        krows = pl.ds(pl.multiple_of(kt * MLA_K, MLA_K), MLA_K)
        for hh in range(2):
            cols = slice(hh * LANES, (hh + 1) * LANES)
            s = _dot_nt(k_ref[krows, cols], q_ref[qrows, cols])
            if masked:
                s = jnp.where(rel >= kt * MLA_K - qt * MLA_Q, s, NEG)
            s_ref[slot, hh] = s

    def update(qt, kt, slot):
        for hh in range(2):
            v = jnp.concatenate([vt_ref[kt * sub + i, hh * V_ROWS:(hh + 1) * V_ROWS, :] for i in range(sub)],
                                axis=1)
            _flash_step(s_ref[slot, hh], v, m_ref.at[hh, qt], acc_ref.at[hh, qt])

    _pipelined_pairs(sched_ref, 0, len(full), lambda qt, kt, slot: scores(qt, kt, slot, False), update, None)
    _pipelined_pairs(sched_ref, 2, len(diag), lambda qt, kt, slot: scores(qt, kt, slot, True), update, None)

    for qt in range(SEQ // MLA_Q):
        o_t = jnp.concatenate([_flash_finish(m_ref.at[hh, qt], acc_ref.at[hh, qt])
                               for hh in range(2)], axis=0)
        o_ref[qt * MLA_Q:(qt + 1) * MLA_Q, :] = _dot_nt(eye_ref[...], o_t.astype(BF16)).astype(BF16)


def _schedule(*pair_lists):
    n = max(len(p) for p in pair_lists)
    out = np.zeros((2 * len(pair_lists), n), np.int32)
    for i, pairs in enumerate(pair_lists):
        out[2 * i, :len(pairs)] = [a for a, _ in pairs]
        out[2 * i + 1, :len(pairs)] = [b for _, b in pairs]
    return jnp.asarray(out)


def _mla_attention(q, k, vt):
    b = q.shape[0] // SEQ
    nq = SEQ // MLA_Q
    nv = SEQ // V_TILE
    chunks = MLA_K // KEY_CHUNK
    return pl.pallas_call(
        _mla_kernel,
        grid=(b, MLA_HEADS // 2),
        in_specs=[pl.BlockSpec(memory_space=pltpu.SMEM),
                  pl.BlockSpec((SEQ, 2 * LANES), lambda i, hp: (i, hp)),
                  pl.BlockSpec((SEQ, 2 * LANES), lambda i, hp: (i, hp)),
                  pl.BlockSpec((nv, 2 * V_ROWS, V_TILE), lambda i, hp: (i, hp, 0)),
                  pl.BlockSpec((MLA_Q, MLA_Q), lambda i, hp: (0, 0))],
        out_specs=pl.BlockSpec((SEQ, LANES), lambda i, hp: (i, hp)),
        out_shape=jax.ShapeDtypeStruct((q.shape[0], MLA_HEADS * MLA_V), BF16),
        scratch_shapes=[pltpu.VMEM((2, nq, chunks, 1, MLA_Q), F32),
                        pltpu.VMEM((2, nq, chunks, V_ROWS, MLA_Q), F32),
                        pltpu.VMEM((2, 2, MLA_K, MLA_Q), F32)],
        compiler_params=pltpu.CompilerParams(
            dimension_semantics=("parallel", "parallel"), vmem_limit_bytes=VMEM_LIMIT),
        name="mla_attention",
    )(_schedule(*_mla_pairs()), q, k, vt, jnp.eye(MLA_Q, dtype=BF16))


def _stack_heads(q2):
    low = lax.broadcasted_iota(jnp.int32, (q2.shape[0], LANES), 1) < NSA_HEAD_DIM
    qa, qb = q2[:, :LANES], q2[:, LANES:]
    zero = jnp.zeros_like(qa)
    return jnp.concatenate([jnp.where(low, qa, zero), jnp.where(low, zero, qa),
                            jnp.where(low, qb, zero), jnp.where(low, zero, qb)], axis=0)


def _select_kernel(q_ref, kc_ref, vct_ref, tabc_ref, ovt_ref, negsel_ref, ocmp_ref):
    step = pl.program_id(2)
    t = SELECT_Q
    heads = range(NSA_GROUP)
    tiles = range(t // NSA_Q)
    q4 = _stack_heads(q_ref[...])

    bias_c = jnp.concatenate([tabc_ref[r, u * N_CHUNK:(u + 1) * N_CHUNK, :] for r in heads for u in tiles],
                             axis=1)
    valid = bias_c > 0.5 * NEG
    sc = jnp.where(valid, _dot_nt(kc_ref[0, 0], q4) + bias_c, NEG)
    mx = jnp.max(sc, axis=0, keepdims=True)
    p = jnp.where(valid, jnp.exp2(sc - mx), 0.0)
    pc = p / jnp.maximum(jnp.sum(p, axis=0, keepdims=True), 1e-30)
    o_cmp = _dot(vct_ref[0, 0], pc.astype(BF16)).astype(BF16)
    for u in tiles:
        ocmp_ref[0, 0, u] = jnp.concatenate(
            [o_cmp[:, r * t + u * NSA_Q:r * t + (u + 1) * NSA_Q] for r in heads], axis=0)

    psum = sum(pc[:, r * t:(r + 1) * t] for r in heads)
    imp = sum(_dot(ovt_ref[...], piece) for piece in _split(psum, 3))
    jj = lax.broadcasted_iota(jnp.int32, (N_SEL, t), 0)
    blk = (step * t + lax.broadcasted_iota(jnp.int32, (N_SEL, t), 1)) // SEL_BLOCK
    ok = jj <= blk
    forced = ok & ((jj == 0) | (jj == blk) | (jj == blk - 1))
    score = jnp.where(forced, FORCE_SCORE, jnp.where(ok, imp, -jnp.inf))
    rank = jnp.zeros((N_SEL, t), jnp.int32)
    for i in range(N_SEL):
        ci = score[i:i + 1, :]
        beats = (ci > score) | ((ci == score) & (jj > i))
        rank = rank + beats.astype(jnp.int32)
    negsel = jnp.where(ok & (rank < SEL_TOP_N), 0.0, NEG)
    for u in tiles:
        negsel_ref[0, 0, u] = negsel[:, u * NSA_Q:(u + 1) * NSA_Q]


def _nsa_select(q, kc, vct, tables):
    b = q.shape[0] // SEQ
    t = SELECT_Q
    steps = SEQ // t
    nq = SEQ // NSA_Q
    per = t // NSA_Q
    g_ = NSA_KV_HEADS
    c0 = np.arange(N_CHUNK)[None, :] * CMP_STRIDE
    j0 = np.arange(N_SEL)[:, None] * SEL_BLOCK
    ovt = np.clip(np.minimum(c0 + CMP_BLOCK, j0 + SEL_BLOCK) - np.maximum(c0, j0), 0, None) / CMP_BLOCK
    ovt[:, N_CHUNK - 1:] = 0.0
    return pl.pallas_call(
        _select_kernel,
        grid=(b, g_, steps),
        in_specs=[pl.BlockSpec((t, 2 * LANES), lambda i, g, j: (i * steps + j, g)),
                  pl.BlockSpec((1, 1, N_CHUNK, LANES), lambda i, g, j: (i, g, 0, 0)),
                  pl.BlockSpec((1, 1, NSA_HEAD_DIM, N_CHUNK), lambda i, g, j: (i, g, 0, 0)),
                  pl.BlockSpec((NSA_GROUP, per * N_CHUNK, LANES), lambda i, g, j: (g, WIN_ROWS // t + j, 0)),
                  pl.BlockSpec((N_SEL, N_CHUNK), lambda i, g, j: (0, 0))],
        out_specs=[pl.BlockSpec((1, 1, per, N_SEL, NSA_Q), lambda i, g, j: (i, g, j, 0, 0)),
                   pl.BlockSpec((1, 1, per, NSA_GROUP * NSA_HEAD_DIM, NSA_Q), lambda i, g, j: (i, g, j, 0, 0))],
        out_shape=[jax.ShapeDtypeStruct((b, g_, nq, N_SEL, NSA_Q), F32),
                   jax.ShapeDtypeStruct((b, g_, nq, NSA_GROUP * NSA_HEAD_DIM, NSA_Q), BF16)],
        compiler_params=pltpu.CompilerParams(
            dimension_semantics=("parallel", "parallel", "parallel"), vmem_limit_bytes=VMEM_LIMIT),
        name="nsa_select",
    )(q, kc, vct, tables, jnp.asarray(ovt, BF16))


def _nsa_pairs():
    nq, nk = SEQ // NSA_Q, SEQ // NSA_K
    sel = [(qt, kt) for kt in range(nk) for qt in range(nq) if kt * NSA_K <= qt * NSA_Q]
    win = [(qt, kt) for kt in range(nk) for qt in range(nq)
           if max(qt * NSA_Q - (WINDOW - 1), 0) // NSA_K <= kt and kt * NSA_K <= qt * NSA_Q]
    return sel, win


def _attend_kernel(sched_ref, q_ref, ksel_ref, kwin_ref, vsel_ref, vwin_ref, negsel_ref, ocmp_ref, gate_ref,
                   tabw_ref, gx_ref, eye_ref, o_ref, m_ref, acc_ref, s_ref, p_ref, alpha_ref):
    t = NSA_Q
    heads = range(NSA_GROUP)
    sel, win = _nsa_pairs()
    _flash_init(m_ref, acc_ref)

    def scores(qt, kt, slot, k_ref, selected):
        q4 = _stack_heads(q_ref[pl.ds(pl.multiple_of(qt * t, t), t), :])
        s = _dot_nt(k_ref[pl.ds(pl.multiple_of(kt * NSA_K, NSA_K), NSA_K), :], q4)
        off = qt - kt * (NSA_K // NSA_Q)
        if selected:
            off = jnp.minimum(off, SEL_FAR_TILE)
        bias_rows = pl.ds(pl.multiple_of(off * NSA_K, NSA_K), NSA_K)
        s = s + jnp.concatenate([tabw_ref[r, bias_rows, :] for r in heads], axis=1)
        if selected:
            per = NSA_K // SEL_BLOCK
            rows = [negsel_ref[0, 0, qt, pl.ds(kt * per + bb, 1), :] for bb in range(per)]
            s = jnp.concatenate(
                [s[bb * SEL_BLOCK:(bb + 1) * SEL_BLOCK] + jnp.concatenate([rows[bb]] * NSA_GROUP, axis=1)
                 for bb in range(per)], axis=0)
        s_ref[slot] = s

    def softmax(qt, kt, slot, branch):
        _flash_softmax(s_ref[slot], m_ref.at[branch, qt], p_ref.at[slot], alpha_ref.at[slot], chunk=NSA_K)

    def accumulate(qt, kt, slot, v_ref, branch):
        _flash_accumulate(v_ref[kt], p_ref.at[slot], alpha_ref.at[slot], acc_ref.at[branch, qt], chunk=NSA_K)

    _pipelined_pairs(sched_ref, 0, len(sel),
                     lambda qt, kt, slot: scores(qt, kt, slot, ksel_ref, True),
                     lambda qt, kt, slot: softmax(qt, kt, slot, 0),
                     lambda qt, kt, slot: accumulate(qt, kt, slot, vsel_ref, 0))
    _pipelined_pairs(sched_ref, 2, len(win),
                     lambda qt, kt, slot: scores(qt, kt, slot, kwin_ref, False),
                     lambda qt, kt, slot: softmax(qt, kt, slot, 1),
                     lambda qt, kt, slot: accumulate(qt, kt, slot, vwin_ref, 1))

    w = NSA_GROUP * NSA_HEAD_DIM
    for qt in range(SEQ // t):
        branches = [_flash_finish(m_ref.at[br, qt], acc_ref.at[br, qt]).astype(BF16)
                    for br in range(2)]
        stacked = jnp.concatenate([ocmp_ref[0, 0, qt]] + [o[:, r * t:(r + 1) * t] for o in branches for r in heads],
                                  axis=0)
        tok = _dot_nt(eye_ref[...], stacked)
        g = _split_dot(gate_ref[qt * t:(qt + 1) * t, :], gx_ref[0])
        out = g[:, :w] * tok[:, :w] + g[:, w:2 * w] * tok[:, w:2 * w] + g[:, 2 * w:] * tok[:, 2 * w:]
        o_ref[qt * t:(qt + 1) * t, :] = out.astype(BF16)


def _nsa_attend(q, kdup, vt, negsel, ocmp, gates, tables):
    b = q.shape[0] // SEQ
    t = NSA_Q
    nq = SEQ // t
    nv = SEQ // V_TILE
    g_ = NSA_KV_HEADS
    gx = np.zeros((g_, LANES, N_BRANCH * 256), np.float32)
    for gg in range(g_):
        for br in range(N_BRANCH):
            for r in range(NSA_GROUP):
                src = br * NSA_HEADS + gg * NSA_GROUP + r
                gx[gg, src, br * 256 + r * 64: br * 256 + (r + 1) * 64] = 1.0
    k_spec = lambda off: pl.BlockSpec((SEQ, LANES), lambda i, g: (i, off + g))
    v_spec = lambda off: pl.BlockSpec((nv, V_ROWS, V_TILE), lambda i, g: (i, off + g, 0))
    return pl.pallas_call(
        _attend_kernel,
        grid=(b, g_),
        in_specs=[pl.BlockSpec(memory_space=pltpu.SMEM),
                  pl.BlockSpec((SEQ, 2 * LANES), lambda i, g: (i, g)),
                  k_spec(0), k_spec(2), v_spec(0), v_spec(2),
                  pl.BlockSpec((1, 1, nq, N_SEL, t), lambda i, g: (i, g, 0, 0, 0)),
                  pl.BlockSpec((1, 1, nq, NSA_GROUP * NSA_HEAD_DIM, t), lambda i, g: (i, g, 0, 0, 0)),
                  pl.BlockSpec((SEQ, LANES), lambda i, g: (i, 0)),
                  pl.BlockSpec((NSA_GROUP, WIN_ROWS, LANES), lambda i, g: (g, 0, 0)),
                  pl.BlockSpec((1, LANES, N_BRANCH * 256), lambda i, g: (g, 0, 0)),
                  pl.BlockSpec((t, t), lambda i, g: (0, 0))],
        out_specs=pl.BlockSpec((SEQ, 2 * LANES), lambda i, g: (i, g)),
        out_shape=jax.ShapeDtypeStruct((q.shape[0], NSA_HEADS * NSA_HEAD_DIM), BF16),
        scratch_shapes=[pltpu.VMEM((2, nq, 1, 1, NSA_GROUP * t), F32),
                        pltpu.VMEM((2, nq, 1, V_ROWS, NSA_GROUP * t), F32),
                        pltpu.VMEM((2, NSA_K, NSA_GROUP * t), F32),
                        pltpu.VMEM((2, NSA_K, NSA_GROUP * t), BF16),
                        pltpu.VMEM((2, 1, 1, NSA_GROUP * t), F32)],
        compiler_params=pltpu.CompilerParams(
            dimension_semantics=("parallel", "parallel"), vmem_limit_bytes=VMEM_LIMIT),
        name="nsa_attend",
    )(_schedule(*_nsa_pairs()), q, kdup, kdup, vt, vt, negsel, ocmp, gates, tables,
      jnp.asarray(gx, BF16), jnp.eye(t, dtype=BF16))


def _tail_kernel(x_ref, oa_ref, ob_ref, ga_ref, gb_ref, woa_ref, wob_ref, ln2_ref, wup_ref, wdn_ref,
                 out_ref, h2_ref):
    width = MLA_HEADS * MLA_V
    a = (_rms(oa_ref[...].astype(F32), width) * ga_ref[...]).astype(BF16)
    b = (_rms(ob_ref[...].astype(F32), width) * gb_ref[...]).astype(BF16)
    x1 = x_ref[...] + _dot(a, woa_ref[...]) + _dot(b, wob_ref[...])
    h2_ref[...] = (_rms(x1, D_MODEL) * ln2_ref[...]).astype(BF16)
    out_ref[...] = x1

    def body(f, carry):
        u = jnp.maximum(_dot(h2_ref[...], wup_ref[f]), 0.0)
        out_ref[...] += _dot((u * u).astype(BF16), wdn_ref[f])
        return carry

    lax.fori_loop(0, D_FF // FF_CHUNK, body, 0)


def _tail(x2d, o_a, o_b, gn_a, gn_b, w_o, ln2_g, w_up, w_down):
    t = x2d.shape[0]
    rows = TAIL_ROWS
    width = MLA_HEADS * MLA_V
    once = lambda a: pl.BlockSpec(a.shape, lambda i: (0,) * a.ndim)
    nf = D_FF // FF_CHUNK
    wup = w_up.astype(BF16).reshape(D_MODEL, nf, FF_CHUNK).transpose(1, 0, 2)
    wdn = w_down.astype(BF16).reshape(nf, FF_CHUNK, D_MODEL)
    consts = [gn_a[None, :], gn_b[None, :], w_o[:width].astype(BF16), w_o[width:].astype(BF16),
              ln2_g[None, :], wup, wdn]
    row_spec = lambda w: pl.BlockSpec((rows, w), lambda i: (i, 0))
    return pl.pallas_call(
        _tail_kernel,
        grid=(t // rows,),
        in_specs=[row_spec(D_MODEL), row_spec(width), row_spec(width)] + [once(a) for a in consts],
        out_specs=row_spec(D_MODEL),
        out_shape=jax.ShapeDtypeStruct((t, D_MODEL), F32),
        scratch_shapes=[pltpu.VMEM((rows, D_MODEL), BF16)],
        compiler_params=pltpu.CompilerParams(dimension_semantics=("parallel",),
                                             vmem_limit_bytes=VMEM_LIMIT),
        name="tail",
    )(x2d, o_a, o_b, *consts)


def kernel(x, ln1_g, w_in, mla_cq_norm_g, mla_ckv_norm_g, mla_w_uq, mla_w_ukv, mla_q_gain, mla_k_gain,
           nsa_q_gain, nsa_k_gain, nsa_cmp_pe, nsa_cmp_w1, nsa_cmp_w2, rel_bias, grp_norm_mla,
           grp_norm_nsa, w_o, ln2_g, w_up, w_down):
    b, s, d = x.shape
    assert (s, d) == (SEQ, D_MODEL) and ln1_g.shape[0] == 1
    x2d = x.reshape(b * s, d)
    tables = _bias_tables(rel_bias)
    q_mla, k_mla, vt_mla, q_nsa, kdup, vt_nsa, kck, kcv, gates = _projections(
        x2d, ln1_g[0], w_in[0], mla_cq_norm_g[0], mla_ckv_norm_g[0], mla_w_uq[0], mla_w_ukv[0],
        mla_q_gain[0], mla_k_gain[0], nsa_q_gain[0], nsa_k_gain[0])
    kc, vct = _compress(kck, kcv, nsa_cmp_pe[0], nsa_cmp_w1[0], nsa_cmp_w2[0], nsa_k_gain[0, 0])
    o_a = _mla_attention(q_mla, k_mla, vt_mla)
    negsel, ocmp = _nsa_select(q_nsa, kc, vct, tables)
    o_b = _nsa_attend(q_nsa, kdup, vt_nsa, negsel, ocmp, gates, tables)
    out = _tail(x2d, o_a, o_b, grp_norm_mla[0], grp_norm_nsa[0], w_o[0], ln2_g[0], w_up[0], w_down[0])
    return out.reshape(b, s, d)
```

```python
import math

import numpy as np
import jax
import jax.numpy as jnp
from jax import lax
from jax.experimental import pallas as pl
from jax.experimental.pallas import tpu as pltpu

F32 = jnp.float32
BF16 = jnp.bfloat16

D_MODEL = 1024
SEQ = 2048
MLA_HEADS = 8
MLA_NOPE = 64
MLA_ROPE = 32
MLA_V = 64
MLA_QK = MLA_NOPE + MLA_ROPE
MLA_Q_RANK = 384
MLA_KV_RANK = 256
ROPE_THETA = 10000.0
NSA_HEADS = 8
NSA_KV_HEADS = 2
NSA_GROUP = NSA_HEADS // NSA_KV_HEADS
NSA_HEAD_DIM = 64
N_BRANCH = 3
CMP_BLOCK = 32
CMP_STRIDE = 16
CMP_HIDDEN = 128
SEL_BLOCK = 64
SEL_TOP_N = 16
WINDOW = 512
FORCE_SCORE = 1e4
REL_BUCKETS = 32
REL_MAX_DIST = 128
D_FF = 4 * D_MODEL
EPS = 1e-6

LANES = 128
NEG = -1e30
N_CHUNK = SEQ // CMP_STRIDE
N_SEL = SEQ // SEL_BLOCK
NSA_Q = 128
NSA_K = 256
SELECT_Q = 512
N_BIAS_TILES = (WINDOW + NSA_K - NSA_Q) // NSA_Q + 1
SEL_FAR_TILE = (REL_MAX_DIST + NSA_K) // NSA_Q
WIN_ROWS = N_BIAS_TILES * NSA_K
CMP_TAB_BASE = (SEQ // NSA_Q - 1) * (NSA_Q // CMP_STRIDE)
CMP_TAB_ROWS = 256
MLA_Q = 256
MLA_K = 512
V_TILE = 256
PROJ_ROWS = 256
TAIL_ROWS = 512
FF_CHUNK = 1024
KEY_CHUNK = 256
V_ROWS = 80
LOG2E = math.log2(math.e)
VMEM_LIMIT = 56 * 1024 * 1024

assert PROJ_ROWS == V_TILE and MLA_K % V_TILE == 0 and NSA_K == V_TILE


def _dot(a, b):
    return jnp.dot(a, b, preferred_element_type=F32)


def _dot_nt(a, b):
    return lax.dot_general(a, b, (((1,), (1,)), ((), ())), preferred_element_type=F32)


def _split(a, terms):
    pieces = []
    rem = a
    for _ in range(terms):
        piece = rem.astype(BF16)
        pieces.append(piece)
        rem = rem - piece.astype(F32)
    return pieces


def _split_dot(a, b, terms=2):
    return sum(_dot(p, b) for p in _split(a, terms))


def _rms(x, width):
    return x * lax.rsqrt(jnp.sum(x * x, axis=-1, keepdims=True) * (1.0 / width) + EPS)


def _t5_bucket_np(dist):
    n = np.maximum(dist, 0)
    max_exact = REL_BUCKETS // 2
    large = max_exact + (np.log(np.maximum(n, 1).astype(np.float32) / max_exact)
                         / math.log(REL_MAX_DIST / max_exact)
                         * (REL_BUCKETS - max_exact)).astype(np.int32)
    large = np.minimum(large, REL_BUCKETS - 1)
    return np.where(n < max_exact, n, large).astype(np.int32)


def _bias_index_table():
    i = np.arange(NSA_Q)[None, :]
    parts = []
    j = np.arange(NSA_K)[:, None]
    for off in range(N_BIAS_TILES):
        d = off * NSA_Q + i - j
        parts.append(np.where((d >= 0) & (d < WINDOW), _t5_bucket_np(d), -1))
    r = np.arange(CMP_TAB_ROWS)[:, None]
    dist_c = (CMP_TAB_BASE - r) * CMP_STRIDE + i - (CMP_BLOCK - 1)
    parts.append(np.where(dist_c >= 0, _t5_bucket_np(dist_c), -1))
    return np.concatenate(parts, axis=0).astype(np.int32)


def _bias_table_kernel(rb_ref, idx_ref, out_ref):
    h = pl.program_id(0)
    idx = idx_ref[...]
    acc = jnp.full(idx.shape, NEG, F32)
    for b in range(REL_BUCKETS):
        acc = jnp.where(idx == b, rb_ref[h, b] * LOG2E, acc)
    out_ref[0] = acc


def _bias_tables(rel_bias):
    idx = jnp.asarray(_bias_index_table())
    rows = idx.shape[0]
    blk = CMP_TAB_ROWS
    return pl.pallas_call(
        _bias_table_kernel,
        grid=(NSA_HEADS, rows // blk),
        in_specs=[pl.BlockSpec(memory_space=pltpu.SMEM),
                  pl.BlockSpec((blk, LANES), lambda h, r: (r, 0))],
        out_specs=pl.BlockSpec((1, blk, LANES), lambda h, r: (h, r, 0)),
        out_shape=jax.ShapeDtypeStruct((NSA_HEADS, rows, LANES), F32),
        name="bias_tables",
    )(rel_bias.T, idx)


ROPE_HALF = MLA_ROPE // 2
MISC_KR = 64


def _rope_tables(scale):
    inv = 1.0 / (ROPE_THETA ** (jnp.arange(ROPE_HALF, dtype=F32) / ROPE_HALF))
    ang = jnp.arange(SEQ, dtype=F32)[:, None] * inv[None, :]
    cos, sin = jnp.cos(ang), jnp.sin(ang)
    ones = jnp.ones((SEQ, MLA_NOPE), F32)
    z32 = jnp.zeros((SEQ, LANES - MLA_QK), F32)
    z64 = jnp.zeros((SEQ, MLA_NOPE), F32)
    c = jnp.concatenate([ones, cos, cos, z32], axis=1) * scale
    s = jnp.concatenate([z64, -sin, sin, z32], axis=1) * scale
    return c, s


def _mla_heads(x, extra, gain_ref, c, s):
    in_head = lax.broadcasted_iota(jnp.int32, (x.shape[0], LANES), 1) < MLA_QK
    outs = []
    for h in range(MLA_HEADS):
        cols = slice(h * LANES, (h + 1) * LANES)
        sl = x[:, cols] if extra is None else x[:, cols] + extra
        ss = jnp.sum(jnp.where(in_head, sl * sl, 0.0), axis=-1, keepdims=True)
        xn = sl * lax.rsqrt(ss * (1.0 / MLA_QK) + EPS) * gain_ref[:, cols]
        outs.append(xn * c + pltpu.roll(xn, LANES - ROPE_HALF, 1) * s)
    return jnp.concatenate(outs, axis=1)


def _pair_norm(x, gain_ref):
    low = lax.broadcasted_iota(jnp.int32, (x.shape[0], LANES), 1) < NSA_HEAD_DIM
    outs = []
    for j in range(x.shape[1] // LANES):
        cols = slice(j * LANES, (j + 1) * LANES)
        sl = x[:, cols]
        sq = sl * sl
        tot = jnp.sum(sq, axis=-1, keepdims=True)
        lo = jnp.sum(jnp.where(low, sq, 0.0), axis=-1, keepdims=True)
        rs = jnp.where(low, lax.rsqrt(lo * (1.0 / NSA_HEAD_DIM) + EPS),
                       lax.rsqrt((tot - lo) * (1.0 / NSA_HEAD_DIM) + EPS))
        outs.append(sl * rs * gain_ref[:, cols])
    return jnp.concatenate(outs, axis=1)


def _proj_kernel(x_ref, ln1_ref, wp_ref, cqg_ref, ckvg_ref, wuq_ref, wukk_ref, wukvt_ref,
                 qg_ref, kg_ref, qc_ref, qs_ref, kc_ref, ks_ref,
                 nq_gain_ref, nk_gain_ref, dup_ref, vpick_ref, ones_mla_ref, ones_nsa_ref, eye_ref,
                 qmla_ref, kmla_ref, vmla_ref, qnsa_ref, kdup_ref, vnsa_ref, kck_ref, kcv_ref, gate_ref):
    x = x_ref[...]
    h = (_rms(x, D_MODEL) * ln1_ref[...]).astype(BF16)
    proj = _dot(h, wp_ref[...])

    cq = (_rms(proj[:, :MLA_Q_RANK], MLA_Q_RANK) * cqg_ref[...]).astype(BF16)
    qmla_ref[...] = _mla_heads(_dot(cq, wuq_ref[...]), None, qg_ref, qc_ref[...], qs_ref[...]).astype(BF16)

    o_kv = MLA_Q_RANK
    ckv = (_rms(proj[:, o_kv:o_kv + MLA_KV_RANK], MLA_KV_RANK) * ckvg_ref[...]).astype(BF16)
    misc = proj[:, 640:768]
    lane = lax.broadcasted_iota(jnp.int32, misc.shape, 1)
    k_rope = jnp.where((lane >= MISC_KR) & (lane < MISC_KR + MLA_ROPE + ROPE_HALF), misc, 0.0)
    kmla_ref[...] = _mla_heads(_dot(ckv, wukk_ref[...]), k_rope, kg_ref, kc_ref[...], ks_ref[...]).astype(BF16)
    vmla_ref[0] = (_dot_nt(wukvt_ref[...], ckv) + ones_mla_ref[...]).astype(BF16)

    qnsa_ref[...] = _pair_norm(proj[:, 768:1280], nq_gain_ref).astype(BF16)

    kn = _pair_norm(proj[:, 1280:1536], nk_gain_ref)
    kdup_ref[...] = _dot(kn.astype(BF16), dup_ref[...]).astype(BF16)
    vnsa_ref[0] = (_dot_nt(vpick_ref[...], proj[:, 1536:1792].astype(BF16)) + ones_nsa_ref[...]).astype(BF16)
    kck_ref[...] = proj[:, 1792:1920]
    kcv_ref[...] = proj[:, 1920:2048]
    gate = 1.0 / (1.0 + jnp.exp(-misc))
    gate_t = sum(_dot_nt(eye_ref[...], piece) for piece in _split(gate, 2))
    for u in range(PROJ_ROWS // NSA_Q):
        gate_ref[0, u] = gate_t[:, u * NSA_Q:(u + 1) * NSA_Q]


def _projections(x2d, ln1_g, w_in, cq_g, ckv_g, w_uq, w_ukv, q_gain, k_gain, nq_gain, nk_gain):
    t = x2d.shape[0]
    kv0 = 1184
    seg = lambda a, b: w_in[:, a:b]
    kvb = lambda br, kv, g: seg(kv0 + ((br * 2 + kv) * 2 + g) * 64, kv0 + ((br * 2 + kv) * 2 + g) * 64 + 64)
    zeros = lambda n: jnp.zeros((D_MODEL, n), F32)
    wp = jnp.concatenate([
        seg(0, 640),
        seg(1952, 1976), zeros(MISC_KR - 24), seg(640, 672), seg(640, 640 + ROPE_HALF), zeros(ROPE_HALF),
        seg(672, 1184),
        kvb(1, 0, 0), kvb(1, 0, 1), kvb(2, 0, 0), kvb(2, 0, 1),
        kvb(1, 1, 0), kvb(1, 1, 1), kvb(2, 1, 0), kvb(2, 1, 1),
        kvb(0, 0, 0), kvb(0, 0, 1), kvb(0, 1, 0), kvb(0, 1, 1),
    ], axis=1).astype(BF16)
    wq3 = w_uq.reshape(MLA_Q_RANK, MLA_HEADS, MLA_QK)
    wuq = jnp.concatenate([wq3, wq3[:, :, MLA_NOPE:MLA_NOPE + ROPE_HALF],
                           jnp.zeros((MLA_Q_RANK, MLA_HEADS, ROPE_HALF), F32)], axis=2)
    wuq = wuq.reshape(MLA_Q_RANK, MLA_HEADS * LANES).astype(BF16)
    wkv = w_ukv.reshape(MLA_KV_RANK, MLA_HEADS, MLA_NOPE + MLA_V)
    wukk = jnp.pad(wkv[:, :, :MLA_NOPE], ((0, 0), (0, 0), (0, LANES - MLA_NOPE)))
    wukk = wukk.reshape(MLA_KV_RANK, MLA_HEADS * LANES).astype(BF16)
    wukvt = jnp.pad(wkv[:, :, MLA_NOPE:], ((0, 0), (0, 0), (0, V_ROWS - MLA_V)))
    wukvt = wukvt.reshape(MLA_KV_RANK, MLA_HEADS * V_ROWS).T.astype(BF16)
    slab_gain = lambda g: jnp.tile(jnp.concatenate(
        [g, g[MLA_NOPE:MLA_NOPE + ROPE_HALF], jnp.zeros((ROPE_HALF,), F32)]), MLA_HEADS)[None, :]
    qc, qs = _rope_tables(MLA_QK ** -0.5 * LOG2E)
    kc, ks = _rope_tables(1.0)
    dup = np.zeros((256, 512), np.float32)
    for s in range(4):
        for rep in range(2):
            dup[s * 64 + np.arange(64), s * 128 + rep * 64 + np.arange(64)] = 1.0
    nqg = (jnp.tile(nq_gain, NSA_HEADS) * (NSA_HEAD_DIM ** -0.5 * LOG2E))[None, :]
    nkg = jnp.concatenate([nk_gain[1], nk_gain[1], nk_gain[2], nk_gain[2]])[None, :]
    vpick = np.zeros((4 * V_ROWS, 256), np.float32)
    for s in range(4):
        vpick[s * V_ROWS + np.arange(64), s * 64 + np.arange(64)] = 1.0

    def ones_col(heads):
        col = np.zeros((heads * V_ROWS, 1), np.float32)
        col[np.arange(heads) * V_ROWS + MLA_V] = 1.0
        return col

    rows = PROJ_ROWS
    n_pos = SEQ // rows
    full = lambda a: pl.BlockSpec(a.shape, lambda i: (0,) * a.ndim)
    tab = pl.BlockSpec((rows, LANES), lambda i: (i % n_pos, 0))
    consts = [ln1_g[None, :], wp, cq_g[None, :], ckv_g[None, :], wuq, wukk, wukvt,
              slab_gain(q_gain), slab_gain(k_gain)]
    tabs = [qc, qs, kc, ks]
    mats = [nqg, nkg, jnp.asarray(dup, BF16),
            jnp.asarray(vpick, BF16), jnp.asarray(ones_col(MLA_HEADS)), jnp.asarray(ones_col(4)),
            jnp.eye(LANES, dtype=BF16)]
    row_spec = lambda w: pl.BlockSpec((rows, w), lambda i: (i, 0))
    vt_spec = lambda w: pl.BlockSpec((1, w, rows), lambda i: (i, 0, 0))
    sds = jax.ShapeDtypeStruct
    return pl.pallas_call(
        _proj_kernel,
        grid=(t // rows,),
        in_specs=[row_spec(D_MODEL)] + [full(a) for a in consts] + [tab] * 4 + [full(a) for a in mats],
        out_specs=[row_spec(1024), row_spec(1024), vt_spec(MLA_HEADS * V_ROWS), row_spec(512), row_spec(512),
                   vt_spec(4 * V_ROWS), row_spec(128), row_spec(128),
                   pl.BlockSpec((1, rows // NSA_Q, LANES, NSA_Q), lambda i: (i, 0, 0, 0))],
        out_shape=[sds((t, 1024), BF16), sds((t, 1024), BF16), sds((t // rows, MLA_HEADS * V_ROWS, rows), BF16),
                   sds((t, 512), BF16), sds((t, 512), BF16), sds((t // rows, 4 * V_ROWS, rows), BF16),
                   sds((t, 128), F32), sds((t, 128), F32), sds((t // rows, rows // NSA_Q, LANES, NSA_Q), F32)],
        compiler_params=pltpu.CompilerParams(dimension_semantics=("parallel",),
                                             vmem_limit_bytes=VMEM_LIMIT),
        name="projections",
    )(x2d, *consts, *tabs, *mats)


def _compress_kernel(xk_ref, xv_ref, pe_ref, w1_ref, w2_ref, kg_ref, dupk_ref, pickv_ref, kc_ref, vct_ref):
    def mlp(x_ref, which):
        top = bot = None
        for t in range(CMP_STRIDE):
            xt = x_ref[pl.ds(t, N_CHUNK, stride=CMP_STRIDE), :]
            a = _dot((xt + pe_ref[which, t]).astype(BF16), w1_ref[which, t])
            b = _dot((xt + pe_ref[which, CMP_STRIDE + t]).astype(BF16), w1_ref[which, CMP_STRIDE + t])
            top = a if top is None else top + a
            bot = b if bot is None else bot + b
        hid = top + pltpu.roll(bot, N_CHUNK - 1, 0)
        act = (hid * (1.0 / (1.0 + jnp.exp(-hid)))).astype(BF16)
        return _dot(act, w2_ref[which])

    k = _pair_norm(mlp(xk_ref, 0), kg_ref).astype(BF16)
    v = mlp(xv_ref, 1).astype(BF16)
    for g in range(NSA_KV_HEADS):
        kc_ref[0, g] = _dot(k, dupk_ref[g]).astype(BF16)
        vct_ref[0, g] = _dot_nt(pickv_ref[g], v).astype(BF16)


def _compress(kck, kcv, cmp_pe, cmp_w1, cmp_w2, k_gain0):
    b = kck.shape[0] // SEQ
    g_ = NSA_KV_HEADS
    d = NSA_HEAD_DIM
    pe = jnp.tile(cmp_pe, (1, 1, g_))[:, :, None, :]
    w1 = cmp_w1.reshape(2, CMP_BLOCK, d, CMP_HIDDEN)
    z1 = jnp.zeros_like(w1)
    w1 = jnp.concatenate([jnp.concatenate([w1, z1], axis=3), jnp.concatenate([z1, w1], axis=3)], axis=2)
    z2 = jnp.zeros_like(cmp_w2)
    w2 = jnp.concatenate([jnp.concatenate([cmp_w2, z2], axis=2), jnp.concatenate([z2, cmp_w2], axis=2)], axis=1)
    dupk = np.zeros((g_, LANES, LANES), np.float32)
    pickv = np.zeros((g_, d, LANES), np.float32)
    for g in range(g_):
        dupk[g, g * d + np.arange(d), np.arange(d)] = 1.0
        dupk[g, g * d + np.arange(d), d + np.arange(d)] = 1.0
        pickv[g, np.arange(d), g * d + np.arange(d)] = 1.0
    full = lambda a: pl.BlockSpec(a.shape, lambda i: (0,) * a.ndim)
    consts = [pe, w1.astype(BF16), w2.astype(BF16), jnp.tile(k_gain0, g_)[None, :],
              jnp.asarray(dupk, BF16), jnp.asarray(pickv, BF16)]
    x_spec = pl.BlockSpec((SEQ, LANES), lambda i: (i, 0))
    return pl.pallas_call(
        _compress_kernel,
        grid=(b,),
        in_specs=[x_spec, x_spec] + [full(a) for a in consts],
        out_specs=[pl.BlockSpec((1, g_, N_CHUNK, LANES), lambda i: (i, 0, 0, 0)),
                   pl.BlockSpec((1, g_, d, N_CHUNK), lambda i: (i, 0, 0, 0))],
        out_shape=[jax.ShapeDtypeStruct((b, g_, N_CHUNK, LANES), BF16),
                   jax.ShapeDtypeStruct((b, g_, d, N_CHUNK), BF16)],
        compiler_params=pltpu.CompilerParams(dimension_semantics=("parallel",)),
        name="compression",
    )(kck, kcv, *consts)


def _flash_init(m_ref, acc_ref):
    m_ref[...] = jnp.full(m_ref.shape, NEG, F32)
    acc_ref[...] = jnp.zeros(acc_ref.shape, F32)


def _flash_step(s, v, m_ref, acc_ref, chunk=KEY_CHUNK):
    for c in range(s.shape[0] // chunk):
        sc = s[c * chunk:(c + 1) * chunk]
        m_old = m_ref[c]
        m_new = jnp.maximum(m_old, jnp.max(sc, axis=0, keepdims=True))
        p = jnp.exp2(sc - m_new).astype(BF16)
        acc_ref[c] = jnp.exp2(m_old - m_new) * acc_ref[c] + _dot(v[:, c * chunk:(c + 1) * chunk], p)
        m_ref[c] = m_new


def _flash_softmax(s, m_ref, p_ref, alpha_ref, chunk=KEY_CHUNK):
    for c in range(s.shape[0] // chunk):
        sc = s[c * chunk:(c + 1) * chunk]
        m_old = m_ref[c]
        m_new = jnp.maximum(m_old, jnp.max(sc, axis=0, keepdims=True))
        alpha_ref[c] = jnp.exp2(m_old - m_new)
        p_ref[c * chunk:(c + 1) * chunk, :] = jnp.exp2(sc - m_new).astype(BF16)
        m_ref[c] = m_new


def _flash_accumulate(v, p_ref, alpha_ref, acc_ref, chunk=KEY_CHUNK):
    for c in range(p_ref.shape[0] // chunk):
        rows = slice(c * chunk, (c + 1) * chunk)
        acc_ref[c] = alpha_ref[c] * acc_ref[c] + _dot(v[:, rows], p_ref[rows, :])


def _flash_finish(m_ref, acc_ref):
    n = m_ref.shape[0]
    m = m_ref[0]
    for c in range(1, n):
        m = jnp.maximum(m, m_ref[c])
    acc = sum(jnp.exp2(m_ref[c] - m) * acc_ref[c] for c in range(n))
    return acc[:MLA_V] / acc[MLA_V:MLA_V + 1]


def _pipelined_pairs(sched_ref, row, n, scores, softmax, accumulate):
    assert n % 2 == 0

    def pair(i):
        i = jnp.minimum(i, n - 1)
        return sched_ref[row, i], sched_ref[row + 1, i]

    if accumulate is None:
        scores(*pair(0), 0)

        def body2(j, carry):
            i = 2 * j
            scores(*pair(i + 1), 1)
            softmax(*pair(i), 0)
            scores(*pair(i + 2), 0)
            softmax(*pair(i + 1), 1)
            return carry

        lax.fori_loop(0, n // 2, body2, 0)
        return

    scores(*pair(0), 0)
    scores(*pair(1), 1)
    softmax(*pair(0), 0)

    def body(j, carry):
        i = 2 * j
        scores(*pair(i + 2), 0)
        softmax(*pair(i + 1), 1)
        accumulate(*pair(i), 0)
        scores(*pair(i + 3), 1)
        softmax(*pair(i + 2), 0)
        accumulate(*pair(i + 1), 1)
        return carry

    lax.fori_loop(0, n // 2, body, 0)


def _mla_pairs():
    nq, nk = SEQ // MLA_Q, SEQ // MLA_K
    full = [(qt, kt) for kt in range(nk) for qt in range(nq) if (kt + 1) * MLA_K <= qt * MLA_Q]
    diag = [(qt, (qt * MLA_Q) // MLA_K) for qt in range(nq)]
    return full, diag


def _mla_kernel(sched_ref, q_ref, k_ref, vt_ref, eye_ref, o_ref, m_ref, acc_ref, s_ref):
    sub = MLA_K // V_TILE
    full, diag = _mla_pairs()
    rel = (lax.broadcasted_iota(jnp.int32, (MLA_K, MLA_Q), 1)
           - lax.broadcasted_iota(jnp.int32, (MLA_K, MLA_Q), 0))
    _flash_init(m_ref, acc_ref)

    def scores(qt, kt, slot, masked):
        qrows = pl.ds(pl.multiple_of(qt * MLA_Q, MLA_Q), MLA_Q)
        krows = pl.ds(pl.multiple_of(kt * MLA_K, MLA_K), MLA_K)
        for hh in range(2):
            cols = slice(hh * LANES, (hh + 1) * LANES)
            s = _dot_nt(k_ref[krows, cols], q_ref[qrows, cols])
            if masked:
                s = jnp.where(rel >= kt * MLA_K - qt * MLA_Q, s, NEG)
            s_ref[slot, hh] = s

    def update(qt, kt, slot):
        for hh in range(2):
            v = jnp.concatenate([vt_ref[kt * sub + i, hh * V_ROWS:(hh + 1) * V_ROWS, :] for i in range(sub)],
                                axis=1)
            _flash_step(s_ref[slot, hh], v, m_ref.at[hh, qt], acc_ref.at[hh, qt])

    _pipelined_pairs(sched_ref, 0, len(full), lambda qt, kt, slot: scores(qt, kt, slot, False), update, None)
    _pipelined_pairs(sched_ref, 2, len(diag), lambda qt, kt, slot: scores(qt, kt, slot, True), update, None)

    for qt in range(SEQ // MLA_Q):
        o_t = jnp.concatenate([_flash_finish(m_ref.at[hh, qt], acc_ref.at[hh, qt])
                               for hh in range(2)], axis=0)
        o_ref[qt * MLA_Q:(qt + 1) * MLA_Q, :] = _dot_nt(eye_ref[...], o_t.astype(BF16)).astype(BF16)


def _schedule(*pair_lists):
    n = max(len(p) for p in pair_lists)
    out = np.zeros((2 * len(pair_lists), n), np.int32)
    for i, pairs in enumerate(pair_lists):
        out[2 * i, :len(pairs)] = [a for a, _ in pairs]
        out[2 * i + 1, :len(pairs)] = [b for _, b in pairs]
    return jnp.asarray(out)


def _mla_attention(q, k, vt):
    b = q.shape[0] // SEQ
    nq = SEQ // MLA_Q
    nv = SEQ // V_TILE
    chunks = MLA_K // KEY_CHUNK
    return pl.pallas_call(
        _mla_kernel,
        grid=(b, MLA_HEADS // 2),
        in_specs=[pl.BlockSpec(memory_space=pltpu.SMEM),
                  pl.BlockSpec((SEQ, 2 * LANES), lambda i, hp: (i, hp)),
                  pl.BlockSpec((SEQ, 2 * LANES), lambda i, hp: (i, hp)),
                  pl.BlockSpec((nv, 2 * V_ROWS, V_TILE), lambda i, hp: (i, hp, 0)),
                  pl.BlockSpec((MLA_Q, MLA_Q), lambda i, hp: (0, 0))],
        out_specs=pl.BlockSpec((SEQ, LANES), lambda i, hp: (i, hp)),
        out_shape=jax.ShapeDtypeStruct((q.shape[0], MLA_HEADS * MLA_V), BF16),
        scratch_shapes=[pltpu.VMEM((2, nq, chunks, 1, MLA_Q), F32),
                        pltpu.VMEM((2, nq, chunks, V_ROWS, MLA_Q), F32),
                        pltpu.VMEM((2, 2, MLA_K, MLA_Q), F32)],
        compiler_params=pltpu.CompilerParams(
            dimension_semantics=("parallel", "parallel"), vmem_limit_bytes=VMEM_LIMIT),
        name="mla_attention",
    )(_schedule(*_mla_pairs()), q, k, vt, jnp.eye(MLA_Q, dtype=BF16))


def _stack_heads(q2):
    low = lax.broadcasted_iota(jnp.int32, (q2.shape[0], LANES), 1) < NSA_HEAD_DIM
    qa, qb = q2[:, :LANES], q2[:, LANES:]
    zero = jnp.zeros_like(qa)
    return jnp.concatenate([jnp.where(low, qa, zero), jnp.where(low, zero, qa),
                            jnp.where(low, qb, zero), jnp.where(low, zero, qb)], axis=0)


def _select_kernel(q_ref, kc_ref, vct_ref, tabc_ref, ovt_ref, negsel_ref, ocmp_ref):
    step = pl.program_id(2)
    t = SELECT_Q
    heads = range(NSA_GROUP)
    tiles = range(t // NSA_Q)
    q4 = _stack_heads(q_ref[...])

    per_tile = NSA_Q // CMP_STRIDE
    rows = [pl.ds(pl.multiple_of(CMP_TAB_BASE - (step * len(tiles) + u) * per_tile, per_tile), N_CHUNK)
            for u in tiles]
    bias_c = jnp.concatenate([tabc_ref[r, rows[u], :] for r in heads for u in tiles], axis=1)
    valid = bias_c > 0.5 * NEG
    sc = jnp.where(valid, _dot_nt(kc_ref[0, 0], q4) + bias_c, NEG)
    mx = jnp.max(sc, axis=0, keepdims=True)
    p = jnp.where(valid, jnp.exp2(sc - mx), 0.0)
    pc = p / jnp.maximum(jnp.sum(p, axis=0, keepdims=True), 1e-30)
    o_cmp = _dot(vct_ref[0, 0], pc.astype(BF16)).astype(BF16)
    for u in tiles:
        ocmp_ref[0, 0, u] = jnp.concatenate(
            [o_cmp[:, r * t + u * NSA_Q:r * t + (u + 1) * NSA_Q] for r in heads], axis=0)

    psum = sum(pc[:, r * t:(r + 1) * t] for r in heads)
    imp = sum(_dot(ovt_ref[...], piece) for piece in _split(psum, 3))
    jj = lax.broadcasted_iota(jnp.int32, (N_SEL, t), 0)
    blk = (step * t + lax.broadcasted_iota(jnp.int32, (N_SEL, t), 1)) // SEL_BLOCK
    ok = jj <= blk
    forced = ok & ((jj == 0) | (jj == blk) | (jj == blk - 1))
    score = jnp.where(forced, FORCE_SCORE, jnp.where(ok, imp, -jnp.inf))
    rank = jnp.zeros((N_SEL, t), jnp.int32)
    for i in range(N_SEL):
        ci = score[i:i + 1, :]
        beats = (ci > score) | ((ci == score) & (jj > i))
        rank = rank + beats.astype(jnp.int32)
    negsel = jnp.where(ok & (rank < SEL_TOP_N), 0.0, NEG)
    for u in tiles:
        negsel_ref[0, 0, u] = negsel[:, u * NSA_Q:(u + 1) * NSA_Q]


def _nsa_select(q, kc, vct, tables):
    b = q.shape[0] // SEQ
    t = SELECT_Q
    steps = SEQ // t
    nq = SEQ // NSA_Q
    per = t // NSA_Q
    g_ = NSA_KV_HEADS
    c0 = np.arange(N_CHUNK)[None, :] * CMP_STRIDE
    j0 = np.arange(N_SEL)[:, None] * SEL_BLOCK
    ovt = np.clip(np.minimum(c0 + CMP_BLOCK, j0 + SEL_BLOCK) - np.maximum(c0, j0), 0, None) / CMP_BLOCK
    ovt[:, N_CHUNK - 1:] = 0.0
    return pl.pallas_call(
        _select_kernel,
        grid=(b, g_, steps),
        in_specs=[pl.BlockSpec((t, 2 * LANES), lambda i, g, j: (i * steps + j, g)),
                  pl.BlockSpec((1, 1, N_CHUNK, LANES), lambda i, g, j: (i, g, 0, 0)),
                  pl.BlockSpec((1, 1, NSA_HEAD_DIM, N_CHUNK), lambda i, g, j: (i, g, 0, 0)),
                  pl.BlockSpec((NSA_GROUP, CMP_TAB_ROWS, LANES), lambda i, g, j: (g, WIN_ROWS // CMP_TAB_ROWS, 0)),
                  pl.BlockSpec((N_SEL, N_CHUNK), lambda i, g, j: (0, 0))],
        out_specs=[pl.BlockSpec((1, 1, per, N_SEL, NSA_Q), lambda i, g, j: (i, g, j, 0, 0)),
                   pl.BlockSpec((1, 1, per, NSA_GROUP * NSA_HEAD_DIM, NSA_Q), lambda i, g, j: (i, g, j, 0, 0))],
        out_shape=[jax.ShapeDtypeStruct((b, g_, nq, N_SEL, NSA_Q), F32),
                   jax.ShapeDtypeStruct((b, g_, nq, NSA_GROUP * NSA_HEAD_DIM, NSA_Q), BF16)],
        compiler_params=pltpu.CompilerParams(
            dimension_semantics=("parallel", "parallel", "parallel"), vmem_limit_bytes=VMEM_LIMIT),
        name="nsa_select",
    )(q, kc, vct, tables, jnp.asarray(ovt, BF16))


def _nsa_pairs():
    nq, nk = SEQ // NSA_Q, SEQ // NSA_K
    sel = [(qt, kt) for kt in range(nk) for qt in range(nq) if kt * NSA_K <= qt * NSA_Q]
    win = [(qt, kt) for kt in range(nk) for qt in range(nq)
           if max(qt * NSA_Q - (WINDOW - 1), 0) // NSA_K <= kt and kt * NSA_K <= qt * NSA_Q]
    return sel, win


def _attend_kernel(sched_ref, q_ref, ksel_ref, kwin_ref, vsel_ref, vwin_ref, negsel_ref, ocmp_ref, gate_ref,
                   tabw_ref, eye_ref, o_ref, m_ref, acc_ref, s_ref, p_ref, alpha_ref):
    t = NSA_Q
    heads = range(NSA_GROUP)
    sel, win = _nsa_pairs()
    _flash_init(m_ref, acc_ref)

    def scores(qt, kt, slot, k_ref, selected):
        q4 = _stack_heads(q_ref[pl.ds(pl.multiple_of(qt * t, t), t), :])
        s = _dot_nt(k_ref[pl.ds(pl.multiple_of(kt * NSA_K, NSA_K), NSA_K), :], q4)
        off = qt - kt * (NSA_K // NSA_Q)
        if selected:
            off = jnp.minimum(off, SEL_FAR_TILE)
        bias_rows = pl.ds(pl.multiple_of(off * NSA_K, NSA_K), NSA_K)
        s = s + jnp.concatenate([tabw_ref[r, bias_rows, :] for r in heads], axis=1)
        if selected:
            per = NSA_K // SEL_BLOCK
            rows = [negsel_ref[0, 0, qt, pl.ds(kt * per + bb, 1), :] for bb in range(per)]
            s = jnp.concatenate(
                [s[bb * SEL_BLOCK:(bb + 1) * SEL_BLOCK] + jnp.concatenate([rows[bb]] * NSA_GROUP, axis=1)
                 for bb in range(per)], axis=0)
        s_ref[slot] = s

    def softmax(qt, kt, slot, branch):
        _flash_softmax(s_ref[slot], m_ref.at[branch, qt], p_ref.at[slot], alpha_ref.at[slot], chunk=NSA_K)

    def accumulate(qt, kt, slot, v_ref, branch):
        _flash_accumulate(v_ref[kt], p_ref.at[slot], alpha_ref.at[slot], acc_ref.at[branch, qt], chunk=NSA_K)

    _pipelined_pairs(sched_ref, 0, len(sel),
                     lambda qt, kt, slot: scores(qt, kt, slot, ksel_ref, True),
                     lambda qt, kt, slot: softmax(qt, kt, slot, 0),
                     lambda qt, kt, slot: accumulate(qt, kt, slot, vsel_ref, 0))
    _pipelined_pairs(sched_ref, 2, len(win),
                     lambda qt, kt, slot: scores(qt, kt, slot, kwin_ref, False),
                     lambda qt, kt, slot: softmax(qt, kt, slot, 1),
                     lambda qt, kt, slot: accumulate(qt, kt, slot, vwin_ref, 1))

    d = NSA_HEAD_DIM
    group = pl.program_id(1)
    for qt in range(SEQ // t):
        o_cmp = ocmp_ref[0, 0, qt].astype(F32)
        o_sel, o_win = [_flash_finish(m_ref.at[br, qt], acc_ref.at[br, qt]) for br in range(2)]
        per = PROJ_ROWS // t
        mixed = []
        for r in heads:
            gate = [gate_ref[qt // per, qt % per, pl.ds(br * NSA_HEADS + group * NSA_GROUP + r, 1), :]
                    for br in range(N_BRANCH)]
            mixed.append(gate[0] * o_cmp[r * d:(r + 1) * d] + gate[1] * o_sel[:, r * t:(r + 1) * t]
                         + gate[2] * o_win[:, r * t:(r + 1) * t])
        o_ref[qt * t:(qt + 1) * t, :] = _dot_nt(eye_ref[...], jnp.concatenate(mixed, axis=0).astype(BF16)).astype(BF16)


def _nsa_attend(q, kdup, vt, negsel, ocmp, gates, tables):
    b = q.shape[0] // SEQ
    t = NSA_Q
    nq = SEQ // t
    nv = SEQ // V_TILE
    g_ = NSA_KV_HEADS
    k_spec = lambda off: pl.BlockSpec((SEQ, LANES), lambda i, g: (i, off + g))
    v_spec = lambda off: pl.BlockSpec((nv, V_ROWS, V_TILE), lambda i, g: (i, off + g, 0))
    return pl.pallas_call(
        _attend_kernel,
        grid=(b, g_),
        in_specs=[pl.BlockSpec(memory_space=pltpu.SMEM),
                  pl.BlockSpec((SEQ, 2 * LANES), lambda i, g: (i, g)),
                  k_spec(0), k_spec(2), v_spec(0), v_spec(2),
                  pl.BlockSpec((1, 1, nq, N_SEL, t), lambda i, g: (i, g, 0, 0, 0)),
                  pl.BlockSpec((1, 1, nq, NSA_GROUP * NSA_HEAD_DIM, t), lambda i, g: (i, g, 0, 0, 0)),
                  pl.BlockSpec((SEQ // PROJ_ROWS, PROJ_ROWS // t, LANES, t), lambda i, g: (i, 0, 0, 0)),
                  pl.BlockSpec((NSA_GROUP, WIN_ROWS, LANES), lambda i, g: (g, 0, 0)),
                  pl.BlockSpec((t, t), lambda i, g: (0, 0))],
        out_specs=pl.BlockSpec((SEQ, 2 * LANES), lambda i, g: (i, g)),
        out_shape=jax.ShapeDtypeStruct((q.shape[0], NSA_HEADS * NSA_HEAD_DIM), BF16),
        scratch_shapes=[pltpu.VMEM((2, nq, 1, 1, NSA_GROUP * t), F32),
                        pltpu.VMEM((2, nq, 1, V_ROWS, NSA_GROUP * t), F32),
                        pltpu.VMEM((2, NSA_K, NSA_GROUP * t), F32),
                        pltpu.VMEM((2, NSA_K, NSA_GROUP * t), BF16),
                        pltpu.VMEM((2, 1, 1, NSA_GROUP * t), F32)],
        compiler_params=pltpu.CompilerParams(
            dimension_semantics=("parallel", "parallel"), vmem_limit_bytes=VMEM_LIMIT),
        name="nsa_attend",
    )(_schedule(*_nsa_pairs()), q, kdup, kdup, vt, vt, negsel, ocmp, gates, tables, jnp.eye(t, dtype=BF16))


def _tail_kernel(x_ref, oa_ref, ob_ref, ga_ref, gb_ref, woa_ref, wob_ref, ln2_ref, wup_ref, wdn_ref,
                 out_ref, h2_ref):
    width = MLA_HEADS * MLA_V
    a = (_rms(oa_ref[...].astype(F32), width) * ga_ref[...]).astype(BF16)
    b = (_rms(ob_ref[...].astype(F32), width) * gb_ref[...]).astype(BF16)
    x1 = x_ref[...] + _dot(a, woa_ref[...]) + _dot(b, wob_ref[...])
    h2_ref[...] = (_rms(x1, D_MODEL) * ln2_ref[...]).astype(BF16)
    out_ref[...] = x1

    def body(f, carry):
        u = jnp.maximum(_dot(h2_ref[...], wup_ref[f]), 0.0)
        out_ref[...] += _dot((u * u).astype(BF16), wdn_ref[f])
        return carry

    lax.fori_loop(0, D_FF // FF_CHUNK, body, 0)


def _tail(x2d, o_a, o_b, gn_a, gn_b, w_o, ln2_g, w_up, w_down):
    t = x2d.shape[0]
    rows = TAIL_ROWS
    width = MLA_HEADS * MLA_V
    once = lambda a: pl.BlockSpec(a.shape, lambda i: (0,) * a.ndim)
    nf = D_FF // FF_CHUNK
    wup = w_up.astype(BF16).reshape(D_MODEL, nf, FF_CHUNK).transpose(1, 0, 2)
    wdn = w_down.astype(BF16).reshape(nf, FF_CHUNK, D_MODEL)
    consts = [gn_a[None, :], gn_b[None, :], w_o[:width].astype(BF16), w_o[width:].astype(BF16),
              ln2_g[None, :], wup, wdn]
    row_spec = lambda w: pl.BlockSpec((rows, w), lambda i: (i, 0))
    return pl.pallas_call(
        _tail_kernel,
        grid=(t // rows,),
        in_specs=[row_spec(D_MODEL), row_spec(width), row_spec(width)] + [once(a) for a in consts],
        out_specs=row_spec(D_MODEL),
        out_shape=jax.ShapeDtypeStruct((t, D_MODEL), F32),
        scratch_shapes=[pltpu.VMEM((rows, D_MODEL), BF16)],
        compiler_params=pltpu.CompilerParams(dimension_semantics=("parallel",),
                                             vmem_limit_bytes=VMEM_LIMIT),
        name="tail",
    )(x2d, o_a, o_b, *consts)


def kernel(x, ln1_g, w_in, mla_cq_norm_g, mla_ckv_norm_g, mla_w_uq, mla_w_ukv, mla_q_gain, mla_k_gain,
           nsa_q_gain, nsa_k_gain, nsa_cmp_pe, nsa_cmp_w1, nsa_cmp_w2, rel_bias, grp_norm_mla,
           grp_norm_nsa, w_o, ln2_g, w_up, w_down):
    b, s, d = x.shape
    assert (s, d) == (SEQ, D_MODEL) and ln1_g.shape[0] == 1
    x2d = x.reshape(b * s, d)
    tables = _bias_tables(rel_bias)
    q_mla, k_mla, vt_mla, q_nsa, kdup, vt_nsa, kck, kcv, gates = _projections(
        x2d, ln1_g[0], w_in[0], mla_cq_norm_g[0], mla_ckv_norm_g[0], mla_w_uq[0], mla_w_ukv[0],
        mla_q_gain[0], mla_k_gain[0], nsa_q_gain[0], nsa_k_gain[0])
    kc, vct = _compress(kck, kcv, nsa_cmp_pe[0], nsa_cmp_w1[0], nsa_cmp_w2[0], nsa_k_gain[0, 0])
    o_a = _mla_attention(q_mla, k_mla, vt_mla)
    negsel, ocmp = _nsa_select(q_nsa, kc, vct, tables)
    o_b = _nsa_attend(q_nsa, kdup, vt_nsa, negsel, ocmp, gates, tables)
    out = _tail(x2d, o_a, o_b, grp_norm_mla[0], grp_norm_nsa[0], w_o[0], ln2_g[0], w_up[0], w_down[0])
    return out.reshape(b, s, d)
```

```python
import math

import numpy as np
import jax
import jax.numpy as jnp
from jax import lax
from jax.experimental import pallas as pl
from jax.experimental.pallas import tpu as pltpu

F32 = jnp.float32
BF16 = jnp.bfloat16

D_MODEL = 1024
SEQ = 2048
MLA_HEADS = 8
MLA_NOPE = 64
MLA_ROPE = 32
MLA_V = 64
MLA_QK = MLA_NOPE + MLA_ROPE
MLA_Q_RANK = 384
MLA_KV_RANK = 256
ROPE_THETA = 10000.0
NSA_HEADS = 8
NSA_KV_HEADS = 2
NSA_GROUP = NSA_HEADS // NSA_KV_HEADS
NSA_HEAD_DIM = 64
N_BRANCH = 3
CMP_BLOCK = 32
CMP_STRIDE = 16
CMP_HIDDEN = 128
SEL_BLOCK = 64
SEL_TOP_N = 16
WINDOW = 512
FORCE_SCORE = 1e4
REL_BUCKETS = 32
REL_MAX_DIST = 128
D_FF = 4 * D_MODEL
EPS = 1e-6

LANES = 128
NEG = -1e30
N_CHUNK = SEQ // CMP_STRIDE
N_SEL = SEQ // SEL_BLOCK
NSA_Q = 128
NSA_K = 256
SELECT_Q = 512
N_BIAS_TILES = (WINDOW + NSA_K - NSA_Q) // NSA_Q + 1
SEL_FAR_TILE = (REL_MAX_DIST + NSA_K) // NSA_Q
WIN_ROWS = N_BIAS_TILES * NSA_K
CMP_TAB_BASE = (SEQ // NSA_Q - 1) * (NSA_Q // CMP_STRIDE)
CMP_TAB_ROWS = 256
MLA_Q = 256
MLA_K = 512
V_TILE = 256
PROJ_ROWS = 256
TAIL_ROWS = 512
FF_CHUNK = 1024
KEY_CHUNK = 256
PAIR_UNROLL = 6
NSA_UNROLL = 12
V_ROWS = 80
LOG2E = math.log2(math.e)
VMEM_LIMIT = 56 * 1024 * 1024

assert PROJ_ROWS % V_TILE == 0 and MLA_K % V_TILE == 0 and NSA_K == V_TILE


def _dot(a, b):
    return jnp.dot(a, b, preferred_element_type=F32)


def _dot_nt(a, b):
    return lax.dot_general(a, b, (((1,), (1,)), ((), ())), preferred_element_type=F32)


def _split(a, terms):
    pieces = []
    rem = a
    for _ in range(terms):
        piece = rem.astype(BF16)
        pieces.append(piece)
        rem = rem - piece.astype(F32)
    return pieces


def _split_dot(a, b, terms=2):
    return sum(_dot(p, b) for p in _split(a, terms))


def _rms(x, width):
    return x * lax.rsqrt(jnp.sum(x * x, axis=-1, keepdims=True) * (1.0 / width) + EPS)


def _t5_bucket_np(dist):
    n = np.maximum(dist, 0)
    max_exact = REL_BUCKETS // 2
    large = max_exact + (np.log(np.maximum(n, 1).astype(np.float32) / max_exact)
                         / math.log(REL_MAX_DIST / max_exact)
                         * (REL_BUCKETS - max_exact)).astype(np.int32)
    large = np.minimum(large, REL_BUCKETS - 1)
    return np.where(n < max_exact, n, large).astype(np.int32)


def _bias_index_table():
    i = np.arange(NSA_Q)[None, :]
    parts = []
    j = np.arange(NSA_K)[:, None]
    for off in range(N_BIAS_TILES):
        d = off * NSA_Q + i - j
        parts.append(np.where((d >= 0) & (d < WINDOW), _t5_bucket_np(d), -1))
    r = np.arange(CMP_TAB_ROWS)[:, None]
    dist_c = (CMP_TAB_BASE - r) * CMP_STRIDE + i - (CMP_BLOCK - 1)
    parts.append(np.where(dist_c >= 0, _t5_bucket_np(dist_c), -1))
    return np.concatenate(parts, axis=0).astype(np.int32)


def _bias_table_kernel(rb_ref, idx_ref, out_ref):
    idx = idx_ref[...]
    hit = [idx == b for b in range(REL_BUCKETS)]
    for h in range(NSA_HEADS):
        acc = jnp.full(idx.shape, NEG, F32)
        for b in range(REL_BUCKETS):
            acc = jnp.where(hit[b], rb_ref[h, b] * LOG2E, acc)
        out_ref[h] = acc


def _bias_tables(rel_bias):
    idx = jnp.asarray(_bias_index_table())
    rows = idx.shape[0]
    blk = CMP_TAB_ROWS
    return pl.pallas_call(
        _bias_table_kernel,
        grid=(rows // blk,),
        in_specs=[pl.BlockSpec(memory_space=pltpu.SMEM),
                  pl.BlockSpec((blk, LANES), lambda r: (r, 0))],
        out_specs=pl.BlockSpec((NSA_HEADS, blk, LANES), lambda r: (0, r, 0)),
        out_shape=jax.ShapeDtypeStruct((NSA_HEADS, rows, LANES), F32),
        name="bias_tables",
    )(rel_bias.T, idx)


ROPE_HALF = MLA_ROPE // 2
MISC_KR = 64


def _rope_tables(scale):
    inv = 1.0 / (ROPE_THETA ** (jnp.arange(ROPE_HALF, dtype=F32) / ROPE_HALF))
    ang = jnp.arange(SEQ, dtype=F32)[:, None] * inv[None, :]
    cos, sin = jnp.cos(ang), jnp.sin(ang)
    ones = jnp.ones((SEQ, MLA_NOPE), F32)
    z32 = jnp.zeros((SEQ, LANES - MLA_QK), F32)
    z64 = jnp.zeros((SEQ, MLA_NOPE), F32)
    c = jnp.concatenate([ones, cos, cos, z32], axis=1) * scale
    s = jnp.concatenate([z64, -sin, sin, z32], axis=1) * scale
    return c, s


def _mla_heads(x, extra, gain_ref, c, s):
    in_head = lax.broadcasted_iota(jnp.int32, (x.shape[0], LANES), 1) < MLA_QK
    outs = []
    for h in range(MLA_HEADS):
        cols = slice(h * LANES, (h + 1) * LANES)
        sl = x[:, cols] if extra is None else x[:, cols] + extra
        ss = jnp.sum(jnp.where(in_head, sl * sl, 0.0), axis=-1, keepdims=True)
        xn = sl * lax.rsqrt(ss * (1.0 / MLA_QK) + EPS) * gain_ref[:, cols]
        outs.append(xn * c + pltpu.roll(xn, LANES - ROPE_HALF, 1) * s)
    return jnp.concatenate(outs, axis=1)


def _pair_norm(x, gain_ref):
    low = lax.broadcasted_iota(jnp.int32, (x.shape[0], LANES), 1) < NSA_HEAD_DIM
    outs = []
    for j in range(x.shape[1] // LANES):
        cols = slice(j * LANES, (j + 1) * LANES)
        sl = x[:, cols]
        sq = sl * sl
        tot = jnp.sum(sq, axis=-1, keepdims=True)
        lo = jnp.sum(jnp.where(low, sq, 0.0), axis=-1, keepdims=True)
        rs = jnp.where(low, lax.rsqrt(lo * (1.0 / NSA_HEAD_DIM) + EPS),
                       lax.rsqrt((tot - lo) * (1.0 / NSA_HEAD_DIM) + EPS))
        outs.append(sl * rs * gain_ref[:, cols])
    return jnp.concatenate(outs, axis=1)


def _proj_kernel(x_ref, ln1_ref, wp_ref, cqg_ref, ckvg_ref, wuq_ref, wukk_ref, wukvt_ref,
                 qg_ref, kg_ref, qc_ref, qs_ref, kc_ref, ks_ref,
                 nq_gain_ref, nk_gain_ref, dup_ref, vpick_ref, ones_mla_ref, ones_nsa_ref, eye_ref,
                 qmla_ref, kmla_ref, vmla_ref, qnsa_ref, kdup_ref, vnsa_ref, kck_ref, kcv_ref, gate_ref):
    for sub in range(PROJ_ROWS // V_TILE):
        rows = slice(sub * V_TILE, (sub + 1) * V_TILE)
        x = x_ref[rows, :]
        h = (_rms(x, D_MODEL) * ln1_ref[...]).astype(BF16)
        part = lambda a, b: _dot(h, wp_ref[:, a:b])

        cq = (_rms(part(0, MLA_Q_RANK), MLA_Q_RANK) * cqg_ref[...]).astype(BF16)
        qmla_ref[rows, :] = _mla_heads(_dot(cq, wuq_ref[...]), None, qg_ref,
                                       qc_ref[rows, :], qs_ref[rows, :]).astype(BF16)

        kv_misc = part(MLA_Q_RANK, 768)
        ckv = (_rms(kv_misc[:, :MLA_KV_RANK], MLA_KV_RANK) * ckvg_ref[...]).astype(BF16)
        misc = kv_misc[:, MLA_KV_RANK:]
        lane = lax.broadcasted_iota(jnp.int32, misc.shape, 1)
        k_rope = jnp.where((lane >= MISC_KR) & (lane < MISC_KR + MLA_ROPE + ROPE_HALF), misc, 0.0)
        kmla_ref[rows, :] = _mla_heads(_dot(ckv, wukk_ref[...]), k_rope, kg_ref,
                                       kc_ref[rows, :], ks_ref[rows, :]).astype(BF16)
        vmla_ref[sub] = (_dot_nt(wukvt_ref[...], ckv) + ones_mla_ref[...]).astype(BF16)

        qnsa_ref[rows, :] = _pair_norm(part(768, 1280), nq_gain_ref).astype(BF16)

        kn = _pair_norm(part(1280, 1536), nk_gain_ref)
        kdup_ref[rows, :] = _dot(kn.astype(BF16), dup_ref[...]).astype(BF16)
        vnsa_ref[sub] = (_dot_nt(vpick_ref[...], part(1536, 1792).astype(BF16)) + ones_nsa_ref[...]).astype(BF16)
        cmp_kv = part(1792, 2048)
        kck_ref[rows, :] = cmp_kv[:, :LANES]
        kcv_ref[rows, :] = cmp_kv[:, LANES:]
        gate = 1.0 / (1.0 + jnp.exp(-misc))
        gate_t = sum(_dot_nt(eye_ref[...], piece) for piece in _split(gate, 2))
        for u in range(V_TILE // NSA_Q):
            gate_ref[sub * (V_TILE // NSA_Q) + u] = gate_t[:, u * NSA_Q:(u + 1) * NSA_Q]


def _projections(x2d, ln1_g, w_in, cq_g, ckv_g, w_uq, w_ukv, q_gain, k_gain, nq_gain, nk_gain):
    t = x2d.shape[0]
    kv0 = 1184
    seg = lambda a, b: w_in[:, a:b]
    kvb = lambda br, kv, g: seg(kv0 + ((br * 2 + kv) * 2 + g) * 64, kv0 + ((br * 2 + kv) * 2 + g) * 64 + 64)
    zeros = lambda n: jnp.zeros((D_MODEL, n), F32)
    wp = jnp.concatenate([
        seg(0, 640),
        seg(1952, 1976), zeros(MISC_KR - 24), seg(640, 672), seg(640, 640 + ROPE_HALF), zeros(ROPE_HALF),
        seg(672, 1184),
        kvb(1, 0, 0), kvb(1, 0, 1), kvb(2, 0, 0), kvb(2, 0, 1),
        kvb(1, 1, 0), kvb(1, 1, 1), kvb(2, 1, 0), kvb(2, 1, 1),
        kvb(0, 0, 0), kvb(0, 0, 1), kvb(0, 1, 0), kvb(0, 1, 1),
    ], axis=1).astype(BF16)
    wq3 = w_uq.reshape(MLA_Q_RANK, MLA_HEADS, MLA_QK)
    wuq = jnp.concatenate([wq3, wq3[:, :, MLA_NOPE:MLA_NOPE + ROPE_HALF],
                           jnp.zeros((MLA_Q_RANK, MLA_HEADS, ROPE_HALF), F32)], axis=2)
    wuq = wuq.reshape(MLA_Q_RANK, MLA_HEADS * LANES).astype(BF16)
    wkv = w_ukv.reshape(MLA_KV_RANK, MLA_HEADS, MLA_NOPE + MLA_V)
    wukk = jnp.pad(wkv[:, :, :MLA_NOPE], ((0, 0), (0, 0), (0, LANES - MLA_NOPE)))
    wukk = wukk.reshape(MLA_KV_RANK, MLA_HEADS * LANES).astype(BF16)
    wukvt = jnp.pad(wkv[:, :, MLA_NOPE:], ((0, 0), (0, 0), (0, V_ROWS - MLA_V)))
    wukvt = wukvt.reshape(MLA_KV_RANK, MLA_HEADS * V_ROWS).T.astype(BF16)
    slab_gain = lambda g: jnp.tile(jnp.concatenate(
        [g, g[MLA_NOPE:MLA_NOPE + ROPE_HALF], jnp.zeros((ROPE_HALF,), F32)]), MLA_HEADS)[None, :]
    qc, qs = _rope_tables(MLA_QK ** -0.5 * LOG2E)
    kc, ks = _rope_tables(1.0)
    dup = np.zeros((256, 512), np.float32)
    for s in range(4):
        for rep in range(2):
            dup[s * 64 + np.arange(64), s * 128 + rep * 64 + np.arange(64)] = 1.0
    nqg = (jnp.tile(nq_gain, NSA_HEADS) * (NSA_HEAD_DIM ** -0.5 * LOG2E))[None, :]
    nkg = jnp.concatenate([nk_gain[1], nk_gain[1], nk_gain[2], nk_gain[2]])[None, :]
    vpick = np.zeros((4 * V_ROWS, 256), np.float32)
    for s in range(4):
        vpick[s * V_ROWS + np.arange(64), s * 64 + np.arange(64)] = 1.0

    def ones_col(heads):
        col = np.zeros((heads * V_ROWS, 1), np.float32)
        col[np.arange(heads) * V_ROWS + MLA_V] = 1.0
        return col

    rows = PROJ_ROWS
    n_pos = SEQ // rows
    full = lambda a: pl.BlockSpec(a.shape, lambda i: (0,) * a.ndim)
    tab = pl.BlockSpec((rows, LANES), lambda i: (i % n_pos, 0))
    consts = [ln1_g[None, :], wp, cq_g[None, :], ckv_g[None, :], wuq, wukk, wukvt,
              slab_gain(q_gain), slab_gain(k_gain)]
    tabs = [qc, qs, kc, ks]
    mats = [nqg, nkg, jnp.asarray(dup, BF16),
            jnp.asarray(vpick, BF16), jnp.asarray(ones_col(MLA_HEADS)), jnp.asarray(ones_col(4)),
            jnp.eye(LANES, dtype=BF16)]
    row_spec = lambda w: pl.BlockSpec((rows, w), lambda i: (i, 0))
    vt_spec = lambda w, tile: pl.BlockSpec((rows // tile, w, tile), lambda i: (i, 0, 0))
    sds = jax.ShapeDtypeStruct
    return pl.pallas_call(
        _proj_kernel,
        grid=(t // rows,),
        in_specs=[row_spec(D_MODEL)] + [full(a) for a in consts] + [tab] * 4 + [full(a) for a in mats],
        out_specs=[row_spec(1024), row_spec(1024), vt_spec(MLA_HEADS * V_ROWS, V_TILE), row_spec(512),
                   row_spec(512), vt_spec(4 * V_ROWS, V_TILE), row_spec(128), row_spec(128),
                   vt_spec(LANES, NSA_Q)],
        out_shape=[sds((t, 1024), BF16), sds((t, 1024), BF16), sds((t // V_TILE, MLA_HEADS * V_ROWS, V_TILE), BF16),
                   sds((t, 512), BF16), sds((t, 512), BF16), sds((t // V_TILE, 4 * V_ROWS, V_TILE), BF16),
                   sds((t, 128), F32), sds((t, 128), F32), sds((t // NSA_Q, LANES, NSA_Q), F32)],
        compiler_params=pltpu.CompilerParams(dimension_semantics=("parallel",),
                                             vmem_limit_bytes=VMEM_LIMIT),
        name="projections",
    )(x2d, *consts, *tabs, *mats)


def _compress_kernel(xk_ref, xv_ref, pe_ref, w1_ref, w2_ref, kg_ref, dupk_ref, pickv_ref, kc_ref, vct_ref):
    def mlp(x_ref, which):
        top = bot = None
        for t in range(CMP_STRIDE):
            xt = x_ref[pl.ds(t, N_CHUNK, stride=CMP_STRIDE), :]
            a = _dot((xt + pe_ref[which, t]).astype(BF16), w1_ref[which, t])
            b = _dot((xt + pe_ref[which, CMP_STRIDE + t]).astype(BF16), w1_ref[which, CMP_STRIDE + t])
            top = a if top is None else top + a
            bot = b if bot is None else bot + b
        hid = top + pltpu.roll(bot, N_CHUNK - 1, 0)
        act = (hid * (1.0 / (1.0 + jnp.exp(-hid)))).astype(BF16)
        return _dot(act, w2_ref[which])

    k = _pair_norm(mlp(xk_ref, 0), kg_ref).astype(BF16)
    v = mlp(xv_ref, 1).astype(BF16)
    for g in range(NSA_KV_HEADS):
        kc_ref[0, g] = _dot(k, dupk_ref[g]).astype(BF16)
        vct_ref[0, g] = _dot_nt(pickv_ref[g], v).astype(BF16)


def _compress(kck, kcv, cmp_pe, cmp_w1, cmp_w2, k_gain0):
    b = kck.shape[0] // SEQ
    g_ = NSA_KV_HEADS
    d = NSA_HEAD_DIM
    pe = jnp.tile(cmp_pe, (1, 1, g_))[:, :, None, :]
    w1 = cmp_w1.reshape(2, CMP_BLOCK, d, CMP_HIDDEN)
    z1 = jnp.zeros_like(w1)
    w1 = jnp.concatenate([jnp.concatenate([w1, z1], axis=3), jnp.concatenate([z1, w1], axis=3)], axis=2)
    z2 = jnp.zeros_like(cmp_w2)
    w2 = jnp.concatenate([jnp.concatenate([cmp_w2, z2], axis=2), jnp.concatenate([z2, cmp_w2], axis=2)], axis=1)
    dupk = np.zeros((g_, LANES, LANES), np.float32)
    pickv = np.zeros((g_, d, LANES), np.float32)
    for g in range(g_):
        dupk[g, g * d + np.arange(d), np.arange(d)] = 1.0
        dupk[g, g * d + np.arange(d), d + np.arange(d)] = 1.0
        pickv[g, np.arange(d), g * d + np.arange(d)] = 1.0
    full = lambda a: pl.BlockSpec(a.shape, lambda i: (0,) * a.ndim)
    consts = [pe, w1.astype(BF16), w2.astype(BF16), jnp.tile(k_gain0, g_)[None, :],
              jnp.asarray(dupk, BF16), jnp.asarray(pickv, BF16)]
    x_spec = pl.BlockSpec((SEQ, LANES), lambda i: (i, 0))
    return pl.pallas_call(
        _compress_kernel,
        grid=(b,),
        in_specs=[x_spec, x_spec] + [full(a) for a in consts],
        out_specs=[pl.BlockSpec((1, g_, N_CHUNK, LANES), lambda i: (i, 0, 0, 0)),
                   pl.BlockSpec((1, g_, d, N_CHUNK), lambda i: (i, 0, 0, 0))],
        out_shape=[jax.ShapeDtypeStruct((b, g_, N_CHUNK, LANES), BF16),
                   jax.ShapeDtypeStruct((b, g_, d, N_CHUNK), BF16)],
        compiler_params=pltpu.CompilerParams(dimension_semantics=("parallel",)),
        name="compression",
    )(kck, kcv, *consts)


def _flash_init(m_ref, acc_ref):
    m_ref[...] = jnp.full(m_ref.shape, NEG, F32)
    acc_ref[...] = jnp.zeros(acc_ref.shape, F32)


def _flash_step(s, v, m_ref, acc_ref, chunk=KEY_CHUNK):
    for c in range(s.shape[0] // chunk):
        sc = s[c * chunk:(c + 1) * chunk]
        m_old = m_ref[c]
        m_new = jnp.maximum(m_old, jnp.max(sc, axis=0, keepdims=True))
        p = jnp.exp2(sc - m_new).astype(BF16)
        acc_ref[c] = jnp.exp2(m_old - m_new) * acc_ref[c] + _dot(v[:, c * chunk:(c + 1) * chunk], p)
        m_ref[c] = m_new


def _flash_softmax(s, m_ref, p_ref, alpha_ref, chunk=KEY_CHUNK):
    for c in range(s.shape[0] // chunk):
        sc = s[c * chunk:(c + 1) * chunk]
        m_old = m_ref[c]
        m_new = jnp.maximum(m_old, jnp.max(sc, axis=0, keepdims=True))
        alpha_ref[c] = jnp.exp2(m_old - m_new)
        p_ref[c * chunk:(c + 1) * chunk, :] = jnp.exp2(sc - m_new).astype(BF16)
        m_ref[c] = m_new


def _flash_accumulate(v, p_ref, alpha_ref, acc_ref, chunk=KEY_CHUNK):
    for c in range(p_ref.shape[0] // chunk):
        rows = slice(c * chunk, (c + 1) * chunk)
        acc_ref[c] = alpha_ref[c] * acc_ref[c] + _dot(v[:, rows], p_ref[rows, :])


def _flash_finish(m_ref, acc_ref):
    n = m_ref.shape[0]
    m = m_ref[0]
    for c in range(1, n):
        m = jnp.maximum(m, m_ref[c])
    acc = sum(jnp.exp2(m_ref[c] - m) * acc_ref[c] for c in range(n))
    return acc[:MLA_V] / acc[MLA_V:MLA_V + 1]


def _pipelined_pairs(sched_ref, row, n, scores, softmax, accumulate, unroll=PAIR_UNROLL):
    assert n % 2 == 0 and unroll % 2 == 0

    def pair(i):
        i = jnp.minimum(i, n - 1)
        return sched_ref[row, i], sched_ref[row + 1, i]

    def steps(first, count):
        for u in range(count):
            i, slot = first + u, u % 2
            if accumulate is None:
                scores(*pair(i + 1), 1 - slot)
                softmax(*pair(i), slot)
            else:
                scores(*pair(i + 2), slot)
                softmax(*pair(i + 1), 1 - slot)
                accumulate(*pair(i), slot)

    scores(*pair(0), 0)
    if accumulate is not None:
        scores(*pair(1), 1)
        softmax(*pair(0), 0)
    loops = n // unroll
    lax.fori_loop(0, loops, lambda j, carry: (steps(j * unroll, unroll), carry)[1], 0)
    steps(loops * unroll, n - loops * unroll)


def _mla_pairs():
    nq, nk = SEQ // MLA_Q, SEQ // MLA_K
    full = [(qt, kt) for kt in range(nk) for qt in range(nq) if (kt + 1) * MLA_K <= qt * MLA_Q]
    diag = [(qt, (qt * MLA_Q) // MLA_K) for qt in range(nq)]
    return full, diag


def _mla_kernel(sched_ref, q_ref, k_ref, vt_ref, eye_ref, o_ref, m_ref, acc_ref, s_ref):
    sub = MLA_K // V_TILE
    full, diag = _mla_pairs()
    rel = (lax.broadcasted_iota(jnp.int32, (MLA_K, MLA_Q), 1)
           - lax.broadcasted_iota(jnp.int32, (MLA_K, MLA_Q), 0))
    _flash_init(m_ref, acc_ref)

    def scores(qt, kt, slot, masked):
        qrows = pl.ds(pl.multiple_of(qt * MLA_Q, MLA_Q), MLA_Q)
        krows = pl.ds(pl.multiple_of(kt * MLA_K, MLA_K), MLA_K)
        for hh in range(2):
            cols = slice(hh * LANES, (hh + 1) * LANES)
            s = _dot_nt(k_ref[krows, cols], q_ref[qrows, cols])
            if masked:
                s = jnp.where(rel >= kt * MLA_K - qt * MLA_Q, s, NEG)
            s_ref[slot, hh] = s

    def update(qt, kt, slot):
        for hh in range(2):
            v = jnp.concatenate([vt_ref[kt * sub + i, hh * V_ROWS:(hh + 1) * V_ROWS, :] for i in range(sub)],
                                axis=1)
            _flash_step(s_ref[slot, hh], v, m_ref.at[hh, qt], acc_ref.at[hh, qt])

    _pipelined_pairs(sched_ref, 0, len(full), lambda qt, kt, slot: scores(qt, kt, slot, False), update, None)
    _pipelined_pairs(sched_ref, 2, len(diag), lambda qt, kt, slot: scores(qt, kt, slot, True), update, None)

    for qt in range(SEQ // MLA_Q):
        o_t = jnp.concatenate([_flash_finish(m_ref.at[hh, qt], acc_ref.at[hh, qt])
                               for hh in range(2)], axis=0)
        o_ref[qt * MLA_Q:(qt + 1) * MLA_Q, :] = _dot_nt(eye_ref[...], o_t.astype(BF16)).astype(BF16)


def _schedule(*pair_lists):
    n = max(len(p) for p in pair_lists)
    out = np.zeros((2 * len(pair_lists), n), np.int32)
    for i, pairs in enumerate(pair_lists):
        out[2 * i, :len(pairs)] = [a for a, _ in pairs]
        out[2 * i + 1, :len(pairs)] = [b for _, b in pairs]
    return jnp.asarray(out)


def _mla_attention(q, k, vt):
    b = q.shape[0] // SEQ
    nq = SEQ // MLA_Q
    nv = SEQ // V_TILE
    chunks = MLA_K // KEY_CHUNK
    return pl.pallas_call(
        _mla_kernel,
        grid=(b, MLA_HEADS // 2),
        in_specs=[pl.BlockSpec(memory_space=pltpu.SMEM),
                  pl.BlockSpec((SEQ, 2 * LANES), lambda i, hp: (i, hp)),
                  pl.BlockSpec((SEQ, 2 * LANES), lambda i, hp: (i, hp)),
                  pl.BlockSpec((nv, 2 * V_ROWS, V_TILE), lambda i, hp: (i, hp, 0)),
                  pl.BlockSpec((MLA_Q, MLA_Q), lambda i, hp: (0, 0))],
        out_specs=pl.BlockSpec((SEQ, LANES), lambda i, hp: (i, hp)),
        out_shape=jax.ShapeDtypeStruct((q.shape[0], MLA_HEADS * MLA_V), BF16),
        scratch_shapes=[pltpu.VMEM((2, nq, chunks, 1, MLA_Q), F32),
                        pltpu.VMEM((2, nq, chunks, V_ROWS, MLA_Q), F32),
                        pltpu.VMEM((2, 2, MLA_K, MLA_Q), F32)],
        compiler_params=pltpu.CompilerParams(
            dimension_semantics=("parallel", "parallel"), vmem_limit_bytes=VMEM_LIMIT),
        name="mla_attention",
    )(_schedule(*_mla_pairs()), q, k, vt, jnp.eye(MLA_Q, dtype=BF16))


def _stack_heads(q2):
    low = lax.broadcasted_iota(jnp.int32, (q2.shape[0], LANES), 1) < NSA_HEAD_DIM
    qa, qb = q2[:, :LANES], q2[:, LANES:]
    zero = jnp.zeros_like(qa)
    return jnp.concatenate([jnp.where(low, qa, zero), jnp.where(low, zero, qa),
                            jnp.where(low, qb, zero), jnp.where(low, zero, qb)], axis=0)


def _select_kernel(q_ref, kc_ref, vct_ref, tabc_ref, ovt_ref, negsel_ref, ocmp_ref):
    step = pl.program_id(2)
    t = SELECT_Q
    heads = range(NSA_GROUP)
    tiles = range(t // NSA_Q)
    q4 = _stack_heads(q_ref[...])

    per_tile = NSA_Q // CMP_STRIDE
    rows = [pl.ds(pl.multiple_of(CMP_TAB_BASE - (step * len(tiles) + u) * per_tile, per_tile), N_CHUNK)
            for u in tiles]
    bias_c = jnp.concatenate([tabc_ref[r, rows[u], :] for r in heads for u in tiles], axis=1)
    valid = bias_c > 0.5 * NEG
    sc = jnp.where(valid, _dot_nt(kc_ref[0, 0], q4) + bias_c, NEG)
    mx = jnp.max(sc, axis=0, keepdims=True)
    p = jnp.where(valid, jnp.exp2(sc - mx), 0.0)
    pc = p / jnp.maximum(jnp.sum(p, axis=0, keepdims=True), 1e-30)
    o_cmp = _dot(vct_ref[0, 0], pc.astype(BF16)).astype(BF16)
    for u in tiles:
        ocmp_ref[0, 0, u] = jnp.concatenate(
            [o_cmp[:, r * t + u * NSA_Q:r * t + (u + 1) * NSA_Q] for r in heads], axis=0)

    psum = sum(pc[:, r * t:(r + 1) * t] for r in heads)
    imp = sum(_dot(ovt_ref[...], piece) for piece in _split(psum, 3))
    jj = lax.broadcasted_iota(jnp.int32, (N_SEL, t), 0)
    blk = (step * t + lax.broadcasted_iota(jnp.int32, (N_SEL, t), 1)) // SEL_BLOCK
    ok = jj <= blk
    forced = ok & ((jj == 0) | (jj == blk) | (jj == blk - 1))
    score = jnp.where(forced, FORCE_SCORE, jnp.where(ok, imp, -jnp.inf))
    rank = jnp.zeros((N_SEL, t), jnp.int32)
    for i in range(N_SEL):
        ci = score[i:i + 1, :]
        beats = (ci > score) | ((ci == score) & (jj > i))
        rank = rank + beats.astype(jnp.int32)
    negsel = jnp.where(ok & (rank < SEL_TOP_N), 0.0, NEG)
    for u in tiles:
        negsel_ref[0, 0, u] = negsel[:, u * NSA_Q:(u + 1) * NSA_Q]


def _nsa_select(q, kc, vct, tables):
    b = q.shape[0] // SEQ
    t = SELECT_Q
    steps = SEQ // t
    nq = SEQ // NSA_Q
    per = t // NSA_Q
    g_ = NSA_KV_HEADS
    c0 = np.arange(N_CHUNK)[None, :] * CMP_STRIDE
    j0 = np.arange(N_SEL)[:, None] * SEL_BLOCK
    ovt = np.clip(np.minimum(c0 + CMP_BLOCK, j0 + SEL_BLOCK) - np.maximum(c0, j0), 0, None) / CMP_BLOCK
    ovt[:, N_CHUNK - 1:] = 0.0
    return pl.pallas_call(
        _select_kernel,
        grid=(b, g_, steps),
        in_specs=[pl.BlockSpec((t, 2 * LANES), lambda i, g, j: (i * steps + j, g)),
                  pl.BlockSpec((1, 1, N_CHUNK, LANES), lambda i, g, j: (i, g, 0, 0)),
                  pl.BlockSpec((1, 1, NSA_HEAD_DIM, N_CHUNK), lambda i, g, j: (i, g, 0, 0)),
                  pl.BlockSpec((NSA_GROUP, CMP_TAB_ROWS, LANES), lambda i, g, j: (g, WIN_ROWS // CMP_TAB_ROWS, 0)),
                  pl.BlockSpec((N_SEL, N_CHUNK), lambda i, g, j: (0, 0))],
        out_specs=[pl.BlockSpec((1, 1, per, N_SEL, NSA_Q), lambda i, g, j: (i, g, j, 0, 0)),
                   pl.BlockSpec((1, 1, per, NSA_GROUP * NSA_HEAD_DIM, NSA_Q), lambda i, g, j: (i, g, j, 0, 0))],
        out_shape=[jax.ShapeDtypeStruct((b, g_, nq, N_SEL, NSA_Q), F32),
                   jax.ShapeDtypeStruct((b, g_, nq, NSA_GROUP * NSA_HEAD_DIM, NSA_Q), BF16)],
        compiler_params=pltpu.CompilerParams(
            dimension_semantics=("parallel", "parallel", "parallel"), vmem_limit_bytes=VMEM_LIMIT),
        name="nsa_select",
    )(q, kc, vct, tables, jnp.asarray(ovt, BF16))


def _nsa_pairs():
    nq, nk = SEQ // NSA_Q, SEQ // NSA_K
    sel = [(qt, kt) for kt in range(nk) for qt in range(nq) if kt * NSA_K <= qt * NSA_Q]
    win = [(qt, kt) for kt in range(nk) for qt in range(nq)
           if max(qt * NSA_Q - (WINDOW - 1), 0) // NSA_K <= kt and kt * NSA_K <= qt * NSA_Q]
    return sel, win


def _attend_kernel(sched_ref, q_ref, ksel_ref, kwin_ref, vsel_ref, vwin_ref, negsel_ref, ocmp_ref, gate_ref,
                   tabw_ref, eye_ref, o_ref, m_ref, acc_ref, s_ref, p_ref, alpha_ref):
    t = NSA_Q
    heads = range(NSA_GROUP)
    sel, win = _nsa_pairs()
    _flash_init(m_ref, acc_ref)

    def scores(qt, kt, slot, k_ref, selected):
        q4 = _stack_heads(q_ref[pl.ds(pl.multiple_of(qt * t, t), t), :])
        s = _dot_nt(k_ref[pl.ds(pl.multiple_of(kt * NSA_K, NSA_K), NSA_K), :], q4)
        off = qt - kt * (NSA_K // NSA_Q)
        if selected:
            off = jnp.minimum(off, SEL_FAR_TILE)
        bias_rows = pl.ds(pl.multiple_of(off * NSA_K, NSA_K), NSA_K)
        s = s + jnp.concatenate([tabw_ref[r, bias_rows, :] for r in heads], axis=1)
        if selected:
            per = NSA_K // SEL_BLOCK
            rows = [negsel_ref[0, 0, qt, pl.ds(kt * per + bb, 1), :] for bb in range(per)]
            s = jnp.concatenate(
                [s[bb * SEL_BLOCK:(bb + 1) * SEL_BLOCK] + jnp.concatenate([rows[bb]] * NSA_GROUP, axis=1)
                 for bb in range(per)], axis=0)
        s_ref[slot] = s

    def softmax(qt, kt, slot, branch):
        _flash_softmax(s_ref[slot], m_ref.at[branch, qt], p_ref.at[slot], alpha_ref.at[slot], chunk=NSA_K)

    def accumulate(qt, kt, slot, v_ref, branch):
        _flash_accumulate(v_ref[kt], p_ref.at[slot], alpha_ref.at[slot], acc_ref.at[branch, qt], chunk=NSA_K)

    _pipelined_pairs(sched_ref, 0, len(sel),
                     lambda qt, kt, slot: scores(qt, kt, slot, ksel_ref, True),
                     lambda qt, kt, slot: softmax(qt, kt, slot, 0),
                     lambda qt, kt, slot: accumulate(qt, kt, slot, vsel_ref, 0), unroll=NSA_UNROLL)
    _pipelined_pairs(sched_ref, 2, len(win),
                     lambda qt, kt, slot: scores(qt, kt, slot, kwin_ref, False),
                     lambda qt, kt, slot: softmax(qt, kt, slot, 1),
                     lambda qt, kt, slot: accumulate(qt, kt, slot, vwin_ref, 1), unroll=NSA_UNROLL)

    d = NSA_HEAD_DIM
    group = pl.program_id(1)
    for qt in range(SEQ // t):
        o_cmp = ocmp_ref[0, 0, qt].astype(F32)
        o_sel, o_win = [_flash_finish(m_ref.at[br, qt], acc_ref.at[br, qt]) for br in range(2)]
        mixed = []
        for r in heads:
            gate = [gate_ref[qt, pl.ds(br * NSA_HEADS + group * NSA_GROUP + r, 1), :]
                    for br in range(N_BRANCH)]
            mixed.append(gate[0] * o_cmp[r * d:(r + 1) * d] + gate[1] * o_sel[:, r * t:(r + 1) * t]
                         + gate[2] * o_win[:, r * t:(r + 1) * t])
        o_ref[qt * t:(qt + 1) * t, :] = _dot_nt(eye_ref[...], jnp.concatenate(mixed, axis=0).astype(BF16)).astype(BF16)


def _nsa_attend(q, kdup, vt, negsel, ocmp, gates, tables):
    b = q.shape[0] // SEQ
    t = NSA_Q
    nq = SEQ // t
    nv = SEQ // V_TILE
    g_ = NSA_KV_HEADS
    k_spec = lambda off: pl.BlockSpec((SEQ, LANES), lambda i, g: (i, off + g))
    v_spec = lambda off: pl.BlockSpec((nv, V_ROWS, V_TILE), lambda i, g: (i, off + g, 0))
    return pl.pallas_call(
        _attend_kernel,
        grid=(b, g_),
        in_specs=[pl.BlockSpec(memory_space=pltpu.SMEM),
                  pl.BlockSpec((SEQ, 2 * LANES), lambda i, g: (i, g)),
                  k_spec(0), k_spec(2), v_spec(0), v_spec(2),
                  pl.BlockSpec((1, 1, nq, N_SEL, t), lambda i, g: (i, g, 0, 0, 0)),
                  pl.BlockSpec((1, 1, nq, NSA_GROUP * NSA_HEAD_DIM, t), lambda i, g: (i, g, 0, 0, 0)),
                  pl.BlockSpec((nq, LANES, t), lambda i, g: (i, 0, 0)),
                  pl.BlockSpec((NSA_GROUP, WIN_ROWS, LANES), lambda i, g: (g, 0, 0)),
                  pl.BlockSpec((t, t), lambda i, g: (0, 0))],
        out_specs=pl.BlockSpec((SEQ, 2 * LANES), lambda i, g: (i, g)),
        out_shape=jax.ShapeDtypeStruct((q.shape[0], NSA_HEADS * NSA_HEAD_DIM), BF16),
        scratch_shapes=[pltpu.VMEM((2, nq, 1, 1, NSA_GROUP * t), F32),
                        pltpu.VMEM((2, nq, 1, V_ROWS, NSA_GROUP * t), F32),
                        pltpu.VMEM((2, NSA_K, NSA_GROUP * t), F32),
                        pltpu.VMEM((2, NSA_K, NSA_GROUP * t), BF16),
                        pltpu.VMEM((2, 1, 1, NSA_GROUP * t), F32)],
        compiler_params=pltpu.CompilerParams(
            dimension_semantics=("parallel", "parallel"), vmem_limit_bytes=VMEM_LIMIT),
        name="nsa_attend",
    )(_schedule(*_nsa_pairs()), q, kdup, kdup, vt, vt, negsel, ocmp, gates, tables, jnp.eye(t, dtype=BF16))


def _tail_kernel(x_ref, oa_ref, ob_ref, ga_ref, gb_ref, woa_ref, wob_ref, ln2_ref, wup_ref, wdn_ref,
                 out_ref, h2_ref):
    width = MLA_HEADS * MLA_V
    a = (_rms(oa_ref[...].astype(F32), width) * ga_ref[...]).astype(BF16)
    b = (_rms(ob_ref[...].astype(F32), width) * gb_ref[...]).astype(BF16)
    x1 = x_ref[...] + _dot(a, woa_ref[...]) + _dot(b, wob_ref[...])
    h2_ref[...] = (_rms(x1, D_MODEL) * ln2_ref[...]).astype(BF16)
    out_ref[...] = x1

    def body(f, carry):
        u = jnp.maximum(_dot(h2_ref[...], wup_ref[f]), 0.0)
        out_ref[...] += _dot((u * u).astype(BF16), wdn_ref[f])
        return carry

    lax.fori_loop(0, D_FF // FF_CHUNK, body, 0)


def _tail(x2d, o_a, o_b, gn_a, gn_b, w_o, ln2_g, w_up, w_down):
    t = x2d.shape[0]
    rows = TAIL_ROWS
    width = MLA_HEADS * MLA_V
    once = lambda a: pl.BlockSpec(a.shape, lambda i: (0,) * a.ndim)
    nf = D_FF // FF_CHUNK
    wup = w_up.astype(BF16).reshape(D_MODEL, nf, FF_CHUNK).transpose(1, 0, 2)
    wdn = w_down.astype(BF16).reshape(nf, FF_CHUNK, D_MODEL)
    consts = [gn_a[None, :], gn_b[None, :], w_o[:width].astype(BF16), w_o[width:].astype(BF16),
              ln2_g[None, :], wup, wdn]
    row_spec = lambda w: pl.BlockSpec((rows, w), lambda i: (i, 0))
    return pl.pallas_call(
        _tail_kernel,
        grid=(t // rows,),
        in_specs=[row_spec(D_MODEL), row_spec(width), row_spec(width)] + [once(a) for a in consts],
        out_specs=row_spec(D_MODEL),
        out_shape=jax.ShapeDtypeStruct((t, D_MODEL), F32),
        scratch_shapes=[pltpu.VMEM((rows, D_MODEL), BF16)],
        compiler_params=pltpu.CompilerParams(dimension_semantics=("parallel",),
                                             vmem_limit_bytes=VMEM_LIMIT),
        name="tail",
    )(x2d, o_a, o_b, *consts)


def kernel(x, ln1_g, w_in, mla_cq_norm_g, mla_ckv_norm_g, mla_w_uq, mla_w_ukv, mla_q_gain, mla_k_gain,
           nsa_q_gain, nsa_k_gain, nsa_cmp_pe, nsa_cmp_w1, nsa_cmp_w2, rel_bias, grp_norm_mla,
           grp_norm_nsa, w_o, ln2_g, w_up, w_down):
    b, s, d = x.shape
    assert (s, d) == (SEQ, D_MODEL) and ln1_g.shape[0] == 1
    x2d = x.reshape(b * s, d)
    tables = _bias_tables(rel_bias)
    q_mla, k_mla, vt_mla, q_nsa, kdup, vt_nsa, kck, kcv, gates = _projections(
        x2d, ln1_g[0], w_in[0], mla_cq_norm_g[0], mla_ckv_norm_g[0], mla_w_uq[0], mla_w_ukv[0],
        mla_q_gain[0], mla_k_gain[0], nsa_q_gain[0], nsa_k_gain[0])
    kc, vct = _compress(kck, kcv, nsa_cmp_pe[0], nsa_cmp_w1[0], nsa_cmp_w2[0], nsa_k_gain[0, 0])
    o_a = _mla_attention(q_mla, k_mla, vt_mla)
    negsel, ocmp = _nsa_select(q_nsa, kc, vct, tables)
    o_b = _nsa_attend(q_nsa, kdup, vt_nsa, negsel, ocmp, gates, tables)
    out = _tail(x2d, o_a, o_b, grp_norm_mla[0], grp_norm_nsa[0], w_o[0], ln2_g[0], w_up[0], w_down[0])
    return out.reshape(b, s, d)
```

```python
import math

import numpy as np
import jax
import jax.numpy as jnp
from jax import lax
from jax.experimental import pallas as pl
from jax.experimental.pallas import tpu as pltpu

F32 = jnp.float32
BF16 = jnp.bfloat16

D_MODEL = 1024
SEQ = 2048
MLA_HEADS = 8
MLA_NOPE = 64
MLA_ROPE = 32
MLA_V = 64
MLA_QK = MLA_NOPE + MLA_ROPE
MLA_Q_RANK = 384
MLA_KV_RANK = 256
ROPE_THETA = 10000.0
NSA_HEADS = 8
NSA_KV_HEADS = 2
NSA_GROUP = NSA_HEADS // NSA_KV_HEADS
NSA_HEAD_DIM = 64
N_BRANCH = 3
CMP_BLOCK = 32
CMP_STRIDE = 16
CMP_HIDDEN = 128
SEL_BLOCK = 64
SEL_TOP_N = 16
WINDOW = 512
FORCE_SCORE = 1e4
REL_BUCKETS = 32
REL_MAX_DIST = 128
D_FF = 4 * D_MODEL
EPS = 1e-6

LANES = 128
NEG = -1e30
N_CHUNK = SEQ // CMP_STRIDE
N_SEL = SEQ // SEL_BLOCK
NSA_Q = 128
NSA_K = 256
SELECT_Q = 512
N_BIAS_TILES = (WINDOW + NSA_K - NSA_Q) // NSA_Q + 1
SEL_FAR_TILE = (REL_MAX_DIST + NSA_K) // NSA_Q
WIN_ROWS = N_BIAS_TILES * NSA_K
AUG_BLOCK = NSA_HEAD_DIM
AUG_ONE = AUG_BLOCK + N_SEL
CMP_TAB_BASE = (SEQ // NSA_Q - 1) * (NSA_Q // CMP_STRIDE)
CMP_TAB_ROWS = 256
MLA_Q = 256
MLA_K = 512
V_TILE = 256
PROJ_ROWS = 256
TAIL_ROWS = 512
FF_CHUNK = 1024
KEY_CHUNK = 256
PAIR_UNROLL = 6
NSA_UNROLL = 12
V_ROWS = 80
LOG2E = math.log2(math.e)
VMEM_LIMIT = 56 * 1024 * 1024

assert PROJ_ROWS % V_TILE == 0 and MLA_K % V_TILE == 0 and NSA_K == V_TILE


def _dot(a, b):
    return jnp.dot(a, b, preferred_element_type=F32)


def _dot_nt(a, b):
    return lax.dot_general(a, b, (((1,), (1,)), ((), ())), preferred_element_type=F32)


def _split(a, terms):
    pieces = []
    rem = a
    for _ in range(terms):
        piece = rem.astype(BF16)
        pieces.append(piece)
        rem = rem - piece.astype(F32)
    return pieces


def _split_dot(a, b, terms=2):
    return sum(_dot(p, b) for p in _split(a, terms))


def _rms(x, width):
    return x * lax.rsqrt(jnp.sum(x * x, axis=-1, keepdims=True) * (1.0 / width) + EPS)


def _t5_bucket_np(dist):
    n = np.maximum(dist, 0)
    max_exact = REL_BUCKETS // 2
    large = max_exact + (np.log(np.maximum(n, 1).astype(np.float32) / max_exact)
                         / math.log(REL_MAX_DIST / max_exact)
                         * (REL_BUCKETS - max_exact)).astype(np.int32)
    large = np.minimum(large, REL_BUCKETS - 1)
    return np.where(n < max_exact, n, large).astype(np.int32)


def _bias_index_table():
    i = np.arange(NSA_Q)[None, :]
    parts = []
    j = np.arange(NSA_K)[:, None]
    for off in range(N_BIAS_TILES):
        d = off * NSA_Q + i - j
        parts.append(np.where((d >= 0) & (d < WINDOW), _t5_bucket_np(d), -1))
    r = np.arange(CMP_TAB_ROWS)[:, None]
    dist_c = (CMP_TAB_BASE - r) * CMP_STRIDE + i - (CMP_BLOCK - 1)
    parts.append(np.where(dist_c >= 0, _t5_bucket_np(dist_c), -1))
    return np.concatenate(parts, axis=0).astype(np.int32)


def _bias_table_kernel(rb_ref, idx_ref, out_ref):
    idx = idx_ref[...]
    hit = [idx == b for b in range(REL_BUCKETS)]
    is_window = pl.program_id(0) < WIN_ROWS // idx.shape[0]
    for h in range(NSA_HEADS):
        far = jnp.where(is_window, rb_ref[h, REL_BUCKETS - 1], 0.0)
        acc = jnp.full(idx.shape, NEG, F32)
        for b in range(REL_BUCKETS):
            acc = jnp.where(hit[b], (rb_ref[h, b] - far) * LOG2E, acc)
        out_ref[h] = acc


def _bias_tables(rel_bias):
    idx = jnp.asarray(_bias_index_table())
    rows = idx.shape[0]
    blk = CMP_TAB_ROWS
    return pl.pallas_call(
        _bias_table_kernel,
        grid=(rows // blk,),
        in_specs=[pl.BlockSpec(memory_space=pltpu.SMEM),
                  pl.BlockSpec((blk, LANES), lambda r: (r, 0))],
        out_specs=pl.BlockSpec((NSA_HEADS, blk, LANES), lambda r: (0, r, 0)),
        out_shape=jax.ShapeDtypeStruct((NSA_HEADS, rows, LANES), F32),
        name="bias_tables",
    )(rel_bias.T, idx)


ROPE_HALF = MLA_ROPE // 2
MISC_KR = 64


def _rope_tables(scale):
    inv = 1.0 / (ROPE_THETA ** (jnp.arange(ROPE_HALF, dtype=F32) / ROPE_HALF))
    ang = jnp.arange(SEQ, dtype=F32)[:, None] * inv[None, :]
    cos, sin = jnp.cos(ang), jnp.sin(ang)
    ones = jnp.ones((SEQ, MLA_NOPE), F32)
    z32 = jnp.zeros((SEQ, LANES - MLA_QK), F32)
    z64 = jnp.zeros((SEQ, MLA_NOPE), F32)
    c = jnp.concatenate([ones, cos, cos, z32], axis=1) * scale
    s = jnp.concatenate([z64, -sin, sin, z32], axis=1) * scale
    return c, s


def _mla_heads(x, extra, gain_ref, c, s):
    in_head = lax.broadcasted_iota(jnp.int32, (x.shape[0], LANES), 1) < MLA_QK
    outs = []
    for h in range(MLA_HEADS):
        cols = slice(h * LANES, (h + 1) * LANES)
        sl = x[:, cols] if extra is None else x[:, cols] + extra
        ss = jnp.sum(jnp.where(in_head, sl * sl, 0.0), axis=-1, keepdims=True)
        xn = sl * lax.rsqrt(ss * (1.0 / MLA_QK) + EPS) * gain_ref[:, cols]
        outs.append(xn * c + pltpu.roll(xn, LANES - ROPE_HALF, 1) * s)
    return jnp.concatenate(outs, axis=1)


def _pair_norm(x, gain_ref):
    low = lax.broadcasted_iota(jnp.int32, (x.shape[0], LANES), 1) < NSA_HEAD_DIM
    outs = []
    for j in range(x.shape[1] // LANES):
        cols = slice(j * LANES, (j + 1) * LANES)
        sl = x[:, cols]
        sq = sl * sl
        tot = jnp.sum(sq, axis=-1, keepdims=True)
        lo = jnp.sum(jnp.where(low, sq, 0.0), axis=-1, keepdims=True)
        rs = jnp.where(low, lax.rsqrt(lo * (1.0 / NSA_HEAD_DIM) + EPS),
                       lax.rsqrt((tot - lo) * (1.0 / NSA_HEAD_DIM) + EPS))
        outs.append(sl * rs * gain_ref[:, cols])
    return jnp.concatenate(outs, axis=1)


def _proj_kernel(x_ref, ln1_ref, wp_ref, cqg_ref, ckvg_ref, wuq_ref, wukk_ref, wukvt_ref,
                 qg_ref, kg_ref, qc_ref, qs_ref, kc_ref, ks_ref,
                 nq_gain_ref, nk_gain_ref, qpad_ref, kpad_ref, kaug_ref, vpick_ref, ones_mla_ref, ones_nsa_ref,
                 eye_ref,
                 qmla_ref, kmla_ref, vmla_ref, qnsa_ref, kdup_ref, vnsa_ref, kck_ref, kcv_ref, gate_ref):
    for sub in range(PROJ_ROWS // V_TILE):
        rows = slice(sub * V_TILE, (sub + 1) * V_TILE)
        x = x_ref[rows, :]
        h = (_rms(x, D_MODEL) * ln1_ref[...]).astype(BF16)
        part = lambda a, b: _dot(h, wp_ref[:, a:b])

        cq = (_rms(part(0, MLA_Q_RANK), MLA_Q_RANK) * cqg_ref[...]).astype(BF16)
        qmla_ref[rows, :] = _mla_heads(_dot(cq, wuq_ref[...]), None, qg_ref,
                                       qc_ref[rows, :], qs_ref[rows, :]).astype(BF16)

        kv_misc = part(MLA_Q_RANK, 768)
        ckv = (_rms(kv_misc[:, :MLA_KV_RANK], MLA_KV_RANK) * ckvg_ref[...]).astype(BF16)
        misc = kv_misc[:, MLA_KV_RANK:]
        lane = lax.broadcasted_iota(jnp.int32, misc.shape, 1)
        k_rope = jnp.where((lane >= MISC_KR) & (lane < MISC_KR + MLA_ROPE + ROPE_HALF), misc, 0.0)
        kmla_ref[rows, :] = _mla_heads(_dot(ckv, wukk_ref[...]), k_rope, kg_ref,
                                       kc_ref[rows, :], ks_ref[rows, :]).astype(BF16)
        vmla_ref[sub] = (_dot_nt(wukvt_ref[...], ckv) + ones_mla_ref[...]).astype(BF16)

        qn = _pair_norm(part(768, 1280), nq_gain_ref).astype(BF16)
        qnsa_ref[rows, :] = _dot(qn, qpad_ref[...]).astype(BF16)

        kn = _pair_norm(part(1280, 1536), nk_gain_ref)
        kdup_ref[rows, :] = (_dot(kn.astype(BF16), kpad_ref[...]) + kaug_ref[rows, :]).astype(BF16)
        vnsa_ref[sub] = (_dot_nt(vpick_ref[...], part(1536, 1792).astype(BF16)) + ones_nsa_ref[...]).astype(BF16)
        cmp_kv = part(1792, 2048)
        kck_ref[rows, :] = cmp_kv[:, :LANES]
        kcv_ref[rows, :] = cmp_kv[:, LANES:]
        gate = 1.0 / (1.0 + jnp.exp(-misc))
        gate_t = sum(_dot_nt(eye_ref[...], piece) for piece in _split(gate, 2))
        for u in range(V_TILE // NSA_Q):
            gate_ref[sub * (V_TILE // NSA_Q) + u] = gate_t[:, u * NSA_Q:(u + 1) * NSA_Q]


def _projections(x2d, ln1_g, w_in, cq_g, ckv_g, w_uq, w_ukv, q_gain, k_gain, nq_gain, nk_gain):
    t = x2d.shape[0]
    kv0 = 1184
    seg = lambda a, b: w_in[:, a:b]
    kvb = lambda br, kv, g: seg(kv0 + ((br * 2 + kv) * 2 + g) * 64, kv0 + ((br * 2 + kv) * 2 + g) * 64 + 64)
    zeros = lambda n: jnp.zeros((D_MODEL, n), F32)
    wp = jnp.concatenate([
        seg(0, 640),
        seg(1952, 1976), zeros(MISC_KR - 24), seg(640, 672), seg(640, 640 + ROPE_HALF), zeros(ROPE_HALF),
        seg(672, 1184),
        kvb(1, 0, 0), kvb(1, 0, 1), kvb(2, 0, 0), kvb(2, 0, 1),
        kvb(1, 1, 0), kvb(1, 1, 1), kvb(2, 1, 0), kvb(2, 1, 1),
        kvb(0, 0, 0), kvb(0, 0, 1), kvb(0, 1, 0), kvb(0, 1, 1),
    ], axis=1).astype(BF16)
    wq3 = w_uq.reshape(MLA_Q_RANK, MLA_HEADS, MLA_QK)
    wuq = jnp.concatenate([wq3, wq3[:, :, MLA_NOPE:MLA_NOPE + ROPE_HALF],
                           jnp.zeros((MLA_Q_RANK, MLA_HEADS, ROPE_HALF), F32)], axis=2)
    wuq = wuq.reshape(MLA_Q_RANK, MLA_HEADS * LANES).astype(BF16)
    wkv = w_ukv.reshape(MLA_KV_RANK, MLA_HEADS, MLA_NOPE + MLA_V)
    wukk = jnp.pad(wkv[:, :, :MLA_NOPE], ((0, 0), (0, 0), (0, LANES - MLA_NOPE)))
    wukk = wukk.reshape(MLA_KV_RANK, MLA_HEADS * LANES).astype(BF16)
    wukvt = jnp.pad(wkv[:, :, MLA_NOPE:], ((0, 0), (0, 0), (0, V_ROWS - MLA_V)))
    wukvt = wukvt.reshape(MLA_KV_RANK, MLA_HEADS * V_ROWS).T.astype(BF16)
    slab_gain = lambda g: jnp.tile(jnp.concatenate(
        [g, g[MLA_NOPE:MLA_NOPE + ROPE_HALF], jnp.zeros((ROPE_HALF,), F32)]), MLA_HEADS)[None, :]
    qc, qs = _rope_tables(MLA_QK ** -0.5 * LOG2E)
    kc, ks = _rope_tables(1.0)
    d = NSA_HEAD_DIM
    qpad = np.zeros((NSA_HEADS * d, NSA_HEADS * LANES), np.float32)
    for hh in range(NSA_HEADS):
        qpad[hh * d + np.arange(d), hh * LANES + np.arange(d)] = 1.0
    kpad = qpad[:4 * d, :4 * LANES]
    kaug = np.zeros((SEQ, 4 * LANES), np.float32)
    for s in range(4):
        kaug[:, s * LANES + AUG_ONE:s * LANES + AUG_ONE + 2] = 1.0
        if s < NSA_KV_HEADS:
            kaug[np.arange(SEQ), s * LANES + AUG_BLOCK + np.arange(SEQ) // SEL_BLOCK] = 1.0
    nqg = (jnp.tile(nq_gain, NSA_HEADS) * (NSA_HEAD_DIM ** -0.5 * LOG2E))[None, :]
    nkg = jnp.concatenate([nk_gain[1], nk_gain[1], nk_gain[2], nk_gain[2]])[None, :]
    vpick = np.zeros((4 * V_ROWS, 256), np.float32)
    for s in range(4):
        vpick[s * V_ROWS + np.arange(64), s * 64 + np.arange(64)] = 1.0

    def ones_col(heads):
        col = np.zeros((heads * V_ROWS, 1), np.float32)
        col[np.arange(heads) * V_ROWS + MLA_V] = 1.0
        return col

    rows = PROJ_ROWS
    n_pos = SEQ // rows
    full = lambda a: pl.BlockSpec(a.shape, lambda i: (0,) * a.ndim)
    tab = pl.BlockSpec((rows, LANES), lambda i: (i % n_pos, 0))
    consts = [ln1_g[None, :], wp, cq_g[None, :], ckv_g[None, :], wuq, wukk, wukvt,
              slab_gain(q_gain), slab_gain(k_gain)]
    tabs = [qc, qs, kc, ks]
    mats = [nqg, nkg, jnp.asarray(qpad, BF16), jnp.asarray(kpad, BF16),
            jnp.asarray(vpick, BF16), jnp.asarray(ones_col(MLA_HEADS)), jnp.asarray(ones_col(4)),
            jnp.eye(LANES, dtype=BF16)]
    row_spec = lambda w: pl.BlockSpec((rows, w), lambda i: (i, 0))
    vt_spec = lambda w, tile: pl.BlockSpec((rows // tile, w, tile), lambda i: (i, 0, 0))
    sds = jax.ShapeDtypeStruct
    return pl.pallas_call(
        _proj_kernel,
        grid=(t // rows,),
        in_specs=([row_spec(D_MODEL)] + [full(a) for a in consts] + [tab] * 4 + [full(a) for a in mats[:4]]
                  + [pl.BlockSpec((rows, 4 * LANES), lambda i: (i % n_pos, 0))] + [full(a) for a in mats[4:]]),
        out_specs=[row_spec(1024), row_spec(1024), vt_spec(MLA_HEADS * V_ROWS, V_TILE), row_spec(1024),
                   row_spec(512), vt_spec(4 * V_ROWS, V_TILE), row_spec(128), row_spec(128),
                   vt_spec(LANES, NSA_Q)],
        out_shape=[sds((t, 1024), BF16), sds((t, 1024), BF16), sds((t // V_TILE, MLA_HEADS * V_ROWS, V_TILE), BF16),
                   sds((t, 1024), BF16), sds((t, 512), BF16), sds((t // V_TILE, 4 * V_ROWS, V_TILE), BF16),
                   sds((t, 128), F32), sds((t, 128), F32), sds((t // NSA_Q, LANES, NSA_Q), F32)],
        compiler_params=pltpu.CompilerParams(dimension_semantics=("parallel",),
                                             vmem_limit_bytes=VMEM_LIMIT),
        name="projections",
    )(x2d, *consts, *tabs, *mats[:4], jnp.asarray(kaug, BF16), *mats[4:])


def _compress_kernel(xk_ref, xv_ref, pe_ref, w1_ref, w2_ref, kg_ref, dupk_ref, pickv_ref, kc_ref, vct_ref):
    def mlp(x_ref, which):
        top = bot = None
        for t in range(CMP_STRIDE):
            xt = x_ref[pl.ds(t, N_CHUNK, stride=CMP_STRIDE), :]
            a = _dot((xt + pe_ref[which, t]).astype(BF16), w1_ref[which, t])
            b = _dot((xt + pe_ref[which, CMP_STRIDE + t]).astype(BF16), w1_ref[which, CMP_STRIDE + t])
            top = a if top is None else top + a
            bot = b if bot is None else bot + b
        hid = top + pltpu.roll(bot, N_CHUNK - 1, 0)
        act = (hid * (1.0 / (1.0 + jnp.exp(-hid)))).astype(BF16)
        return _dot(act, w2_ref[which])

    k = _pair_norm(mlp(xk_ref, 0), kg_ref).astype(BF16)
    v = mlp(xv_ref, 1).astype(BF16)
    for g in range(NSA_KV_HEADS):
        kc_ref[0, g] = _dot(k, dupk_ref[g]).astype(BF16)
        vct_ref[0, g] = _dot_nt(pickv_ref[g], v).astype(BF16)


def _compress(kck, kcv, cmp_pe, cmp_w1, cmp_w2, k_gain0):
    b = kck.shape[0] // SEQ
    g_ = NSA_KV_HEADS
    d = NSA_HEAD_DIM
    pe = jnp.tile(cmp_pe, (1, 1, g_))[:, :, None, :]
    w1 = cmp_w1.reshape(2, CMP_BLOCK, d, CMP_HIDDEN)
    z1 = jnp.zeros_like(w1)
    w1 = jnp.concatenate([jnp.concatenate([w1, z1], axis=3), jnp.concatenate([z1, w1], axis=3)], axis=2)
    z2 = jnp.zeros_like(cmp_w2)
    w2 = jnp.concatenate([jnp.concatenate([cmp_w2, z2], axis=2), jnp.concatenate([z2, cmp_w2], axis=2)], axis=1)
    dupk = np.zeros((g_, LANES, LANES), np.float32)
    pickv = np.zeros((g_, d, LANES), np.float32)
    for g in range(g_):
        dupk[g, g * d + np.arange(d), np.arange(d)] = 1.0
        dupk[g, g * d + np.arange(d), d + np.arange(d)] = 1.0
        pickv[g, np.arange(d), g * d + np.arange(d)] = 1.0
    full = lambda a: pl.BlockSpec(a.shape, lambda i: (0,) * a.ndim)
    consts = [pe, w1.astype(BF16), w2.astype(BF16), jnp.tile(k_gain0, g_)[None, :],
              jnp.asarray(dupk, BF16), jnp.asarray(pickv, BF16)]
    x_spec = pl.BlockSpec((SEQ, LANES), lambda i: (i, 0))
    return pl.pallas_call(
        _compress_kernel,
        grid=(b,),
        in_specs=[x_spec, x_spec] + [full(a) for a in consts],
        out_specs=[pl.BlockSpec((1, g_, N_CHUNK, LANES), lambda i: (i, 0, 0, 0)),
                   pl.BlockSpec((1, g_, d, N_CHUNK), lambda i: (i, 0, 0, 0))],
        out_shape=[jax.ShapeDtypeStruct((b, g_, N_CHUNK, LANES), BF16),
                   jax.ShapeDtypeStruct((b, g_, d, N_CHUNK), BF16)],
        compiler_params=pltpu.CompilerParams(dimension_semantics=("parallel",)),
        name="compression",
    )(kck, kcv, *consts)


def _flash_init(m_ref, acc_ref):
    m_ref[...] = jnp.full(m_ref.shape, NEG, F32)
    acc_ref[...] = jnp.zeros(acc_ref.shape, F32)


def _flash_step(s, v, m_ref, acc_ref, chunk=KEY_CHUNK):
    for c in range(s.shape[0] // chunk):
        sc = s[c * chunk:(c + 1) * chunk]
        m_old = m_ref[c]
        m_new = jnp.maximum(m_old, jnp.max(sc, axis=0, keepdims=True))
        p = jnp.exp2(sc - m_new).astype(BF16)
        acc_ref[c] = jnp.exp2(m_old - m_new) * acc_ref[c] + _dot(v[:, c * chunk:(c + 1) * chunk], p)
        m_ref[c] = m_new


def _flash_softmax(s, m_ref, p_ref, alpha_ref, chunk=KEY_CHUNK):
    for c in range(s.shape[0] // chunk):
        sc = s[c * chunk:(c + 1) * chunk]
        m_old = m_ref[c]
        m_new = jnp.maximum(m_old, jnp.max(sc, axis=0, keepdims=True))
        alpha_ref[c] = jnp.exp2(m_old - m_new)
        p_ref[c * chunk:(c + 1) * chunk, :] = jnp.exp2(sc - m_new).astype(BF16)
        m_ref[c] = m_new


def _flash_accumulate(v, p_ref, alpha_ref, acc_ref, chunk=KEY_CHUNK):
    for c in range(p_ref.shape[0] // chunk):
        rows = slice(c * chunk, (c + 1) * chunk)
        acc_ref[c] = alpha_ref[c] * acc_ref[c] + _dot(v[:, rows], p_ref[rows, :])


def _flash_finish(m_ref, acc_ref):
    n = m_ref.shape[0]
    m = m_ref[0]
    for c in range(1, n):
        m = jnp.maximum(m, m_ref[c])
    acc = sum(jnp.exp2(m_ref[c] - m) * acc_ref[c] for c in range(n))
    return acc[:MLA_V] / acc[MLA_V:MLA_V + 1]


def _pipelined_pairs(sched_ref, row, n, scores, softmax, accumulate, unroll=PAIR_UNROLL):
    assert n >= 2 and unroll % 2 == 0

    def pair(i):
        i = jnp.minimum(i, n - 1)
        return sched_ref[row, i], sched_ref[row + 1, i]

    def steps(first, count):
        for u in range(count):
            i, slot = first + u, u % 2
            if accumulate is None:
                scores(*pair(i + 1), 1 - slot)
                softmax(*pair(i), slot)
            else:
                scores(*pair(i + 2), slot)
                softmax(*pair(i + 1), 1 - slot)
                accumulate(*pair(i), slot)

    scores(*pair(0), 0)
    if accumulate is not None:
        scores(*pair(1), 1)
        softmax(*pair(0), 0)
    loops = n // unroll
    lax.fori_loop(0, loops, lambda j, carry: (steps(j * unroll, unroll), carry)[1], 0)
    steps(loops * unroll, n - loops * unroll)


def _mla_pairs():
    nq, nk = SEQ // MLA_Q, SEQ // MLA_K
    full = [(qt, kt) for kt in range(nk) for qt in range(nq) if (kt + 1) * MLA_K <= qt * MLA_Q]
    diag = [(qt, (qt * MLA_Q) // MLA_K) for qt in range(nq)]
    return full, diag


def _mla_kernel(sched_ref, q_ref, k_ref, vt_ref, eye_ref, o_ref, m_ref, acc_ref, s_ref):
    sub = MLA_K // V_TILE
    full, diag = _mla_pairs()
    rel = (lax.broadcasted_iota(jnp.int32, (MLA_K, MLA_Q), 1)
           - lax.broadcasted_iota(jnp.int32, (MLA_K, MLA_Q), 0))
    _flash_init(m_ref, acc_ref)

    def scores(qt, kt, slot, masked):
        qrows = pl.ds(pl.multiple_of(qt * MLA_Q, MLA_Q), MLA_Q)
        krows = pl.ds(pl.multiple_of(kt * MLA_K, MLA_K), MLA_K)
        for hh in range(2):
            cols = slice(hh * LANES, (hh + 1) * LANES)
            s = _dot_nt(k_ref[krows, cols], q_ref[qrows, cols])
            if masked:
                s = jnp.where(rel >= kt * MLA_K - qt * MLA_Q, s, NEG)
            s_ref[slot, hh] = s

    def update(qt, kt, slot):
        for hh in range(2):
            v = jnp.concatenate([vt_ref[kt * sub + i, hh * V_ROWS:(hh + 1) * V_ROWS, :] for i in range(sub)],
                                axis=1)
            _flash_step(s_ref[slot, hh], v, m_ref.at[hh, qt], acc_ref.at[hh, qt])

    _pipelined_pairs(sched_ref, 0, len(full), lambda qt, kt, slot: scores(qt, kt, slot, False), update, None)
    _pipelined_pairs(sched_ref, 2, len(diag), lambda qt, kt, slot: scores(qt, kt, slot, True), update, None)

    for qt in range(SEQ // MLA_Q):
        o_t = jnp.concatenate([_flash_finish(m_ref.at[hh, qt], acc_ref.at[hh, qt])
                               for hh in range(2)], axis=0)
        o_ref[qt * MLA_Q:(qt + 1) * MLA_Q, :] = _dot_nt(eye_ref[...], o_t.astype(BF16)).astype(BF16)


def _schedule(*pair_lists):
    n = max(len(p) for p in pair_lists)
    out = np.zeros((2 * len(pair_lists), n), np.int32)
    for i, pairs in enumerate(pair_lists):
        out[2 * i, :len(pairs)] = [a for a, _ in pairs]
        out[2 * i + 1, :len(pairs)] = [b for _, b in pairs]
    return jnp.asarray(out)


def _mla_attention(q, k, vt):
    b = q.shape[0] // SEQ
    nq = SEQ // MLA_Q
    nv = SEQ // V_TILE
    chunks = MLA_K // KEY_CHUNK
    return pl.pallas_call(
        _mla_kernel,
        grid=(b, MLA_HEADS // 2),
        in_specs=[pl.BlockSpec(memory_space=pltpu.SMEM),
                  pl.BlockSpec((SEQ, 2 * LANES), lambda i, hp: (i, hp)),
                  pl.BlockSpec((SEQ, 2 * LANES), lambda i, hp: (i, hp)),
                  pl.BlockSpec((nv, 2 * V_ROWS, V_TILE), lambda i, hp: (i, hp, 0)),
                  pl.BlockSpec((MLA_Q, MLA_Q), lambda i, hp: (0, 0))],
        out_specs=pl.BlockSpec((SEQ, LANES), lambda i, hp: (i, hp)),
        out_shape=jax.ShapeDtypeStruct((q.shape[0], MLA_HEADS * MLA_V), BF16),
        scratch_shapes=[pltpu.VMEM((2, nq, chunks, 1, MLA_Q), F32),
                        pltpu.VMEM((2, nq, chunks, V_ROWS, MLA_Q), F32),
                        pltpu.VMEM((2, 2, MLA_K, MLA_Q), F32)],
        compiler_params=pltpu.CompilerParams(
            dimension_semantics=("parallel", "parallel"), vmem_limit_bytes=VMEM_LIMIT),
        name="mla_attention",
    )(_schedule(*_mla_pairs()), q, k, vt, jnp.eye(MLA_Q, dtype=BF16))


def _select_kernel(q_ref, kc_ref, vct_ref, tabc_ref, ovt_ref, eye_ref, qaug_ref, ocmp_ref):
    step = pl.program_id(2)
    t = SELECT_Q
    heads = range(NSA_GROUP)
    tiles = range(t // NSA_Q)
    q4 = jnp.concatenate([q_ref[:, r * LANES:(r + 1) * LANES] for r in heads], axis=0)

    per_tile = NSA_Q // CMP_STRIDE
    rows = [pl.ds(pl.multiple_of(CMP_TAB_BASE - (step * len(tiles) + u) * per_tile, per_tile), N_CHUNK)
            for u in tiles]
    bias_c = jnp.concatenate([tabc_ref[r, rows[u], :] for r in heads for u in tiles], axis=1)
    valid = bias_c > 0.5 * NEG
    sc = jnp.where(valid, _dot_nt(kc_ref[0, 0], q4) + bias_c, NEG)
    mx = jnp.max(sc, axis=0, keepdims=True)
    p = jnp.where(valid, jnp.exp2(sc - mx), 0.0)
    pc = p / jnp.maximum(jnp.sum(p, axis=0, keepdims=True), 1e-30)
    o_cmp = _dot(vct_ref[0, 0], pc.astype(BF16)).astype(BF16)
    for u in tiles:
        ocmp_ref[0, 0, u] = jnp.concatenate(
            [o_cmp[:, r * t + u * NSA_Q:r * t + (u + 1) * NSA_Q] for r in heads], axis=0)

    psum = sum(pc[:, r * t:(r + 1) * t] for r in heads)
    imp = sum(_dot(ovt_ref[...], piece) for piece in _split(psum, 3))
    jj = lax.broadcasted_iota(jnp.int32, (N_SEL, t), 0)
    blk = (step * t + lax.broadcasted_iota(jnp.int32, (N_SEL, t), 1)) // SEL_BLOCK
    ok = jj <= blk
    forced = ok & ((jj == 0) | (jj == blk) | (jj == blk - 1))
    score = jnp.where(forced, FORCE_SCORE, jnp.where(ok, imp, -jnp.inf))
    rank = jnp.zeros((N_SEL, t), jnp.int32)
    for i in range(N_SEL):
        ci = score[i:i + 1, :]
        beats = (ci > score) | ((ci == score) & (jj > i))
        rank = rank + beats.astype(jnp.int32)
    negsel = jnp.where(ok & (rank < SEL_TOP_N), 0.0, NEG).astype(BF16)
    spread = jnp.concatenate([jnp.zeros((AUG_BLOCK, t), BF16), negsel,
                              jnp.zeros((LANES - AUG_ONE, t), BF16)], axis=0)
    mask_lanes = _dot_nt(eye_ref[...], spread)
    lane = lax.broadcasted_iota(jnp.int32, (1, LANES), 1)
    for r in heads:
        far = tabc_ref[r, 0:1, :]
        far_hi = far.astype(BF16).astype(F32)
        far_lanes = jnp.where(lane == AUG_ONE, far_hi, jnp.where(lane == AUG_ONE + 1, far - far_hi, 0.0))
        q_aug = (q_ref[:, r * LANES:(r + 1) * LANES].astype(F32) + mask_lanes + far_lanes).astype(BF16)
        for u in tiles:
            qaug_ref[0, 0, u, r * NSA_Q:(r + 1) * NSA_Q, :] = q_aug[u * NSA_Q:(u + 1) * NSA_Q]


def _nsa_select(q, kc, vct, tables):
    b = q.shape[0] // SEQ
    t = SELECT_Q
    steps = SEQ // t
    nq = SEQ // NSA_Q
    per = t // NSA_Q
    g_ = NSA_KV_HEADS
    c0 = np.arange(N_CHUNK)[None, :] * CMP_STRIDE
    j0 = np.arange(N_SEL)[:, None] * SEL_BLOCK
    ovt = np.clip(np.minimum(c0 + CMP_BLOCK, j0 + SEL_BLOCK) - np.maximum(c0, j0), 0, None) / CMP_BLOCK
    ovt[:, N_CHUNK - 1:] = 0.0
    return pl.pallas_call(
        _select_kernel,
        grid=(b, g_, steps),
        in_specs=[pl.BlockSpec((t, NSA_GROUP * LANES), lambda i, g, j: (i * steps + j, g)),
                  pl.BlockSpec((1, 1, N_CHUNK, LANES), lambda i, g, j: (i, g, 0, 0)),
                  pl.BlockSpec((1, 1, NSA_HEAD_DIM, N_CHUNK), lambda i, g, j: (i, g, 0, 0)),
                  pl.BlockSpec((NSA_GROUP, CMP_TAB_ROWS, LANES), lambda i, g, j: (g, WIN_ROWS // CMP_TAB_ROWS, 0)),
                  pl.BlockSpec((N_SEL, N_CHUNK), lambda i, g, j: (0, 0)),
                  pl.BlockSpec((t, t), lambda i, g, j: (0, 0))],
        out_specs=[pl.BlockSpec((1, 1, per, NSA_GROUP * NSA_Q, LANES), lambda i, g, j: (i, g, j, 0, 0)),
                   pl.BlockSpec((1, 1, per, NSA_GROUP * NSA_HEAD_DIM, NSA_Q), lambda i, g, j: (i, g, j, 0, 0))],
        out_shape=[jax.ShapeDtypeStruct((b, g_, nq, NSA_GROUP * NSA_Q, LANES), BF16),
                   jax.ShapeDtypeStruct((b, g_, nq, NSA_GROUP * NSA_HEAD_DIM, NSA_Q), BF16)],
        compiler_params=pltpu.CompilerParams(
            dimension_semantics=("parallel", "parallel", "parallel"), vmem_limit_bytes=VMEM_LIMIT),
        name="nsa_select",
    )(q, kc, vct, tables, jnp.asarray(ovt, BF16), jnp.eye(t, dtype=BF16))


def _nsa_pairs():
    nq, nk, per = SEQ // NSA_Q, SEQ // NSA_K, NSA_K // NSA_Q
    sel = [(qt, kt) for kt in range(nk) for qt in range(nq) if kt * NSA_K <= qt * NSA_Q]
    far = [(qt, kt) for qt, kt in sel if qt - kt * per >= SEL_FAR_TILE]
    near = [(qt, kt) for qt, kt in sel if qt - kt * per < SEL_FAR_TILE]
    win = [(qt, kt) for kt in range(nk) for qt in range(nq)
           if max(qt * NSA_Q - (WINDOW - 1), 0) // NSA_K <= kt and kt * NSA_K <= qt * NSA_Q]
    return far, near, win


def _attend_kernel(sched_ref, qaug_ref, ksel_ref, kwin_ref, vsel_ref, vwin_ref, ocmp_ref, gate_ref,
                   tabw_ref, eye_ref, o_ref, m_ref, acc_ref, s_ref, p_ref, alpha_ref):
    t = NSA_Q
    heads = range(NSA_GROUP)
    far, near, win = _nsa_pairs()
    _flash_init(m_ref, acc_ref)

    def scores(qt, kt, slot, k_ref, add_bias):
        s = _dot_nt(k_ref[pl.ds(pl.multiple_of(kt * NSA_K, NSA_K), NSA_K), :], qaug_ref[0, 0, qt])
        if add_bias:
            off = qt - kt * (NSA_K // NSA_Q)
            bias_rows = pl.ds(pl.multiple_of(off * NSA_K, NSA_K), NSA_K)
            s = s + jnp.concatenate([tabw_ref[r, bias_rows, :] for r in heads], axis=1)
        s_ref[slot] = s

    def softmax(qt, kt, slot, branch):
        _flash_softmax(s_ref[slot], m_ref.at[branch, qt], p_ref.at[slot], alpha_ref.at[slot], chunk=NSA_K)

    def accumulate(qt, kt, slot, v_ref, branch):
        _flash_accumulate(v_ref[kt], p_ref.at[slot], alpha_ref.at[slot], acc_ref.at[branch, qt], chunk=NSA_K)

    for row, pairs, k_ref, v_ref, branch, add_bias in ((0, far, ksel_ref, vsel_ref, 0, False),
                                                       (2, near, ksel_ref, vsel_ref, 0, True),
                                                       (4, win, kwin_ref, vwin_ref, 1, True)):
        _pipelined_pairs(sched_ref, row, len(pairs),
                         lambda qt, kt, slot, k_ref=k_ref, add_bias=add_bias: scores(qt, kt, slot, k_ref, add_bias),
                         lambda qt, kt, slot, branch=branch: softmax(qt, kt, slot, branch),
                         lambda qt, kt, slot, v_ref=v_ref, branch=branch: accumulate(qt, kt, slot, v_ref, branch),
                         unroll=NSA_UNROLL)

    d = NSA_HEAD_DIM
    group = pl.program_id(1)
    for qt in range(SEQ // t):
        o_cmp = ocmp_ref[0, 0, qt].astype(F32)
        o_sel, o_win = [_flash_finish(m_ref.at[br, qt], acc_ref.at[br, qt]) for br in range(2)]
        mixed = []
        for r in heads:
            gate = [gate_ref[qt, pl.ds(br * NSA_HEADS + group * NSA_GROUP + r, 1), :]
                    for br in range(N_BRANCH)]
            mixed.append(gate[0] * o_cmp[r * d:(r + 1) * d] + gate[1] * o_sel[:, r * t:(r + 1) * t]
                         + gate[2] * o_win[:, r * t:(r + 1) * t])
        o_ref[qt * t:(qt + 1) * t, :] = _dot_nt(eye_ref[...], jnp.concatenate(mixed, axis=0).astype(BF16)).astype(BF16)


def _nsa_attend(qaug, kaug, vt, ocmp, gates, tables):
    b = kaug.shape[0] // SEQ
    t = NSA_Q
    nq = SEQ // t
    nv = SEQ // V_TILE
    g_ = NSA_KV_HEADS
    k_spec = lambda off: pl.BlockSpec((SEQ, LANES), lambda i, g: (i, off + g))
    v_spec = lambda off: pl.BlockSpec((nv, V_ROWS, V_TILE), lambda i, g: (i, off + g, 0))
    return pl.pallas_call(
        _attend_kernel,
        grid=(b, g_),
        in_specs=[pl.BlockSpec(memory_space=pltpu.SMEM),
                  pl.BlockSpec((1, 1, nq, NSA_GROUP * t, LANES), lambda i, g: (i, g, 0, 0, 0)),
                  k_spec(0), k_spec(2), v_spec(0), v_spec(2),
                  pl.BlockSpec((1, 1, nq, NSA_GROUP * NSA_HEAD_DIM, t), lambda i, g: (i, g, 0, 0, 0)),
                  pl.BlockSpec((nq, LANES, t), lambda i, g: (i, 0, 0)),
                  pl.BlockSpec((NSA_GROUP, WIN_ROWS, LANES), lambda i, g: (g, 0, 0)),
                  pl.BlockSpec((t, t), lambda i, g: (0, 0))],
        out_specs=pl.BlockSpec((SEQ, 2 * LANES), lambda i, g: (i, g)),
        out_shape=jax.ShapeDtypeStruct((kaug.shape[0], NSA_HEADS * NSA_HEAD_DIM), BF16),
        scratch_shapes=[pltpu.VMEM((2, nq, 1, 1, NSA_GROUP * t), F32),
                        pltpu.VMEM((2, nq, 1, V_ROWS, NSA_GROUP * t), F32),
                        pltpu.VMEM((2, NSA_K, NSA_GROUP * t), F32),
                        pltpu.VMEM((2, NSA_K, NSA_GROUP * t), BF16),
                        pltpu.VMEM((2, 1, 1, NSA_GROUP * t), F32)],
        compiler_params=pltpu.CompilerParams(
            dimension_semantics=("parallel", "parallel"), vmem_limit_bytes=VMEM_LIMIT),
        name="nsa_attend",
    )(_schedule(*_nsa_pairs()), qaug, kaug, kaug, vt, vt, ocmp, gates, tables, jnp.eye(t, dtype=BF16))


def _tail_kernel(x_ref, oa_ref, ob_ref, ga_ref, gb_ref, woa_ref, wob_ref, ln2_ref, wup_ref, wdn_ref,
                 out_ref, h2_ref):
    width = MLA_HEADS * MLA_V
    a = (_rms(oa_ref[...].astype(F32), width) * ga_ref[...]).astype(BF16)
    b = (_rms(ob_ref[...].astype(F32), width) * gb_ref[...]).astype(BF16)
    x1 = x_ref[...] + _dot(a, woa_ref[...]) + _dot(b, wob_ref[...])
    h2_ref[...] = (_rms(x1, D_MODEL) * ln2_ref[...]).astype(BF16)
    out_ref[...] = x1

    def body(f, carry):
        u = jnp.maximum(_dot(h2_ref[...], wup_ref[f]), 0.0)
        out_ref[...] += _dot((u * u).astype(BF16), wdn_ref[f])
        return carry

    lax.fori_loop(0, D_FF // FF_CHUNK, body, 0)


def _tail(x2d, o_a, o_b, gn_a, gn_b, w_o, ln2_g, w_up, w_down):
    t = x2d.shape[0]
    rows = TAIL_ROWS
    width = MLA_HEADS * MLA_V
    once = lambda a: pl.BlockSpec(a.shape, lambda i: (0,) * a.ndim)
    nf = D_FF // FF_CHUNK
    wup = w_up.astype(BF16).reshape(D_MODEL, nf, FF_CHUNK).transpose(1, 0, 2)
    wdn = w_down.astype(BF16).reshape(nf, FF_CHUNK, D_MODEL)
    consts = [gn_a[None, :], gn_b[None, :], w_o[:width].astype(BF16), w_o[width:].astype(BF16),
              ln2_g[None, :], wup, wdn]
    row_spec = lambda w: pl.BlockSpec((rows, w), lambda i: (i, 0))
    return pl.pallas_call(
        _tail_kernel,
        grid=(t // rows,),
        in_specs=[row_spec(D_MODEL), row_spec(width), row_spec(width)] + [once(a) for a in consts],
        out_specs=row_spec(D_MODEL),
        out_shape=jax.ShapeDtypeStruct((t, D_MODEL), F32),
        scratch_shapes=[pltpu.VMEM((rows, D_MODEL), BF16)],
        compiler_params=pltpu.CompilerParams(dimension_semantics=("parallel",),
                                             vmem_limit_bytes=VMEM_LIMIT),
        name="tail",
    )(x2d, o_a, o_b, *consts)


def kernel(x, ln1_g, w_in, mla_cq_norm_g, mla_ckv_norm_g, mla_w_uq, mla_w_ukv, mla_q_gain, mla_k_gain,
           nsa_q_gain, nsa_k_gain, nsa_cmp_pe, nsa_cmp_w1, nsa_cmp_w2, rel_bias, grp_norm_mla,
           grp_norm_nsa, w_o, ln2_g, w_up, w_down):
    b, s, d = x.shape
    assert (s, d) == (SEQ, D_MODEL) and ln1_g.shape[0] == 1
    x2d = x.reshape(b * s, d)
    tables = _bias_tables(rel_bias)
    q_mla, k_mla, vt_mla, q_nsa, kdup, vt_nsa, kck, kcv, gates = _projections(
        x2d, ln1_g[0], w_in[0], mla_cq_norm_g[0], mla_ckv_norm_g[0], mla_w_uq[0], mla_w_ukv[0],
        mla_q_gain[0], mla_k_gain[0], nsa_q_gain[0], nsa_k_gain[0])
    kc, vct = _compress(kck, kcv, nsa_cmp_pe[0], nsa_cmp_w1[0], nsa_cmp_w2[0], nsa_k_gain[0, 0])
    o_a = _mla_attention(q_mla, k_mla, vt_mla)
    q_aug, ocmp = _nsa_select(q_nsa, kc, vct, tables)
    o_b = _nsa_attend(q_aug, kdup, vt_nsa, ocmp, gates, tables)
    out = _tail(x2d, o_a, o_b, grp_norm_mla[0], grp_norm_nsa[0], w_o[0], ln2_g[0], w_up[0], w_down[0])
    return out.reshape(b, s, d)
```

```python
import math

import numpy as np
import jax
import jax.numpy as jnp
from jax import lax
from jax.experimental import pallas as pl
from jax.experimental.pallas import tpu as pltpu

F32 = jnp.float32
BF16 = jnp.bfloat16

D_MODEL = 1024
SEQ = 2048
MLA_HEADS = 8
MLA_NOPE = 64
MLA_ROPE = 32
MLA_V = 64
MLA_QK = MLA_NOPE + MLA_ROPE
MLA_Q_RANK = 384
MLA_KV_RANK = 256
ROPE_THETA = 10000.0
NSA_HEADS = 8
NSA_KV_HEADS = 2
NSA_GROUP = NSA_HEADS // NSA_KV_HEADS
NSA_HEAD_DIM = 64
N_BRANCH = 3
CMP_BLOCK = 32
CMP_STRIDE = 16
CMP_HIDDEN = 128
SEL_BLOCK = 64
SEL_TOP_N = 16
WINDOW = 512
FORCE_SCORE = 1e4
REL_BUCKETS = 32
REL_MAX_DIST = 128
D_FF = 4 * D_MODEL
EPS = 1e-6

LANES = 128
NEG = -1e30
N_CHUNK = SEQ // CMP_STRIDE
N_SEL = SEQ // SEL_BLOCK
NSA_Q = 128
NSA_K = 256
SELECT_Q = 512
N_BIAS_TILES = (WINDOW + NSA_K - NSA_Q) // NSA_Q + 1
SEL_FAR_TILE = (REL_MAX_DIST + NSA_K) // NSA_Q
WIN_ROWS = N_BIAS_TILES * NSA_K
AUG_BLOCK = NSA_HEAD_DIM
AUG_ONE = AUG_BLOCK + N_SEL
CMP_TAB_BASE = (SEQ // NSA_Q - 1) * (NSA_Q // CMP_STRIDE)
CMP_TAB_ROWS = 256
MLA_Q = 256
MLA_K = 512
V_TILE = 256
PROJ_ROWS = 256
TAIL_ROWS = 512
FF_CHUNK = 1024
KEY_CHUNK = 256
PAIR_UNROLL = 6
NSA_UNROLL = 12
V_ROWS = 80
LOG2E = math.log2(math.e)
VMEM_LIMIT = 56 * 1024 * 1024

assert PROJ_ROWS % V_TILE == 0 and MLA_K % V_TILE == 0 and NSA_K == V_TILE


def _dot(a, b):
    return jnp.dot(a, b, preferred_element_type=F32)


def _dot_nt(a, b):
    return lax.dot_general(a, b, (((1,), (1,)), ((), ())), preferred_element_type=F32)


def _split(a, terms):
    pieces = []
    rem = a
    for _ in range(terms):
        piece = rem.astype(BF16)
        pieces.append(piece)
        rem = rem - piece.astype(F32)
    return pieces


def _split_dot(a, b, terms=2):
    return sum(_dot(p, b) for p in _split(a, terms))


def _rms(x, width):
    return x * lax.rsqrt(jnp.sum(x * x, axis=-1, keepdims=True) * (1.0 / width) + EPS)


def _t5_bucket_np(dist):
    n = np.maximum(dist, 0)
    max_exact = REL_BUCKETS // 2
    large = max_exact + (np.log(np.maximum(n, 1).astype(np.float32) / max_exact)
                         / math.log(REL_MAX_DIST / max_exact)
                         * (REL_BUCKETS - max_exact)).astype(np.int32)
    large = np.minimum(large, REL_BUCKETS - 1)
    return np.where(n < max_exact, n, large).astype(np.int32)


def _bias_index_table():
    i = np.arange(NSA_Q)[None, :]
    parts = []
    j = np.arange(NSA_K)[:, None]
    for off in range(N_BIAS_TILES):
        d = off * NSA_Q + i - j
        parts.append(np.where((d >= 0) & (d < WINDOW), _t5_bucket_np(d), -1))
    r = np.arange(CMP_TAB_ROWS)[:, None]
    dist_c = (CMP_TAB_BASE - r) * CMP_STRIDE + i - (CMP_BLOCK - 1)
    parts.append(np.where(dist_c >= 0, _t5_bucket_np(dist_c), -1))
    return np.concatenate(parts, axis=0).astype(np.int32)


def _bias_table_kernel(rb_ref, idx_ref, out_ref):
    idx = idx_ref[...]
    hit = [idx == b for b in range(REL_BUCKETS)]
    is_window = pl.program_id(0) < WIN_ROWS // idx.shape[0]
    for h in range(NSA_HEADS):
        far = jnp.where(is_window, rb_ref[h, REL_BUCKETS - 1], 0.0)
        acc = jnp.full(idx.shape, NEG, F32)
        for b in range(REL_BUCKETS):
            acc = jnp.where(hit[b], (rb_ref[h, b] - far) * LOG2E, acc)
        out_ref[h] = acc


def _bias_tables(rel_bias):
    idx = jnp.asarray(_bias_index_table())
    rows = idx.shape[0]
    blk = CMP_TAB_ROWS
    return pl.pallas_call(
        _bias_table_kernel,
        grid=(rows // blk,),
        in_specs=[pl.BlockSpec(memory_space=pltpu.SMEM),
                  pl.BlockSpec((blk, LANES), lambda r: (r, 0))],
        out_specs=pl.BlockSpec((NSA_HEADS, blk, LANES), lambda r: (0, r, 0)),
        out_shape=jax.ShapeDtypeStruct((NSA_HEADS, rows, LANES), F32),
        name="bias_tables",
    )(rel_bias.T, idx)


ROPE_HALF = MLA_ROPE // 2
MISC_KR = 64


def _rope_tables(scale):
    inv = 1.0 / (ROPE_THETA ** (jnp.arange(ROPE_HALF, dtype=F32) / ROPE_HALF))
    ang = jnp.arange(SEQ, dtype=F32)[:, None] * inv[None, :]
    cos, sin = jnp.cos(ang), jnp.sin(ang)
    ones = jnp.ones((SEQ, MLA_NOPE), F32)
    z32 = jnp.zeros((SEQ, LANES - MLA_QK), F32)
    z64 = jnp.zeros((SEQ, MLA_NOPE), F32)
    c = jnp.concatenate([ones, cos, cos, z32], axis=1) * scale
    s = jnp.concatenate([z64, -sin, sin, z32], axis=1) * scale
    return c, s


def _mla_heads(x, extra, gain_ref, c, s):
    in_head = lax.broadcasted_iota(jnp.int32, (x.shape[0], LANES), 1) < MLA_QK
    outs = []
    for h in range(MLA_HEADS):
        cols = slice(h * LANES, (h + 1) * LANES)
        sl = x[:, cols] if extra is None else x[:, cols] + extra
        ss = jnp.sum(jnp.where(in_head, sl * sl, 0.0), axis=-1, keepdims=True)
        xn = sl * lax.rsqrt(ss * (1.0 / MLA_QK) + EPS) * gain_ref[:, cols]
        outs.append(xn * c + pltpu.roll(xn, LANES - ROPE_HALF, 1) * s)
    return jnp.concatenate(outs, axis=1)


def _pair_norm(x, gain_ref):
    low = lax.broadcasted_iota(jnp.int32, (x.shape[0], LANES), 1) < NSA_HEAD_DIM
    outs = []
    for j in range(x.shape[1] // LANES):
        cols = slice(j * LANES, (j + 1) * LANES)
        sl = x[:, cols]
        sq = sl * sl
        tot = jnp.sum(sq, axis=-1, keepdims=True)
        lo = jnp.sum(jnp.where(low, sq, 0.0), axis=-1, keepdims=True)
        rs = jnp.where(low, lax.rsqrt(lo * (1.0 / NSA_HEAD_DIM) + EPS),
                       lax.rsqrt((tot - lo) * (1.0 / NSA_HEAD_DIM) + EPS))
        outs.append(sl * rs * gain_ref[:, cols])
    return jnp.concatenate(outs, axis=1)


def _proj_kernel(x_ref, ln1_ref, wp_ref, cqg_ref, ckvg_ref, wuq_ref, wukk_ref, wukvt_ref,
                 qg_ref, kg_ref, qc_ref, qs_ref, kc_ref, ks_ref,
                 nq_gain_ref, nk_gain_ref, kpad_ref, kaug_ref, vpick_ref, ones_mla_ref, ones_nsa_ref, eye_ref,
                 qmla_ref, kmla_ref, vmla_ref, qnsa_ref, kdup_ref, vnsa_ref, kck_ref, kcv_ref, gate_ref):
    for sub in range(PROJ_ROWS // V_TILE):
        rows = slice(sub * V_TILE, (sub + 1) * V_TILE)
        x = x_ref[rows, :]
        h = (_rms(x, D_MODEL) * ln1_ref[...]).astype(BF16)
        part = lambda a, b: _dot(h, wp_ref[:, a:b])

        cq = (_rms(part(0, MLA_Q_RANK), MLA_Q_RANK) * cqg_ref[...]).astype(BF16)
        qmla_ref[rows, :] = _mla_heads(_dot(cq, wuq_ref[...]), None, qg_ref,
                                       qc_ref[rows, :], qs_ref[rows, :]).astype(BF16)

        kv_misc = part(MLA_Q_RANK, 768)
        ckv = (_rms(kv_misc[:, :MLA_KV_RANK], MLA_KV_RANK) * ckvg_ref[...]).astype(BF16)
        misc = kv_misc[:, MLA_KV_RANK:]
        lane = lax.broadcasted_iota(jnp.int32, misc.shape, 1)
        k_rope = jnp.where((lane >= MISC_KR) & (lane < MISC_KR + MLA_ROPE + ROPE_HALF), misc, 0.0)
        kmla_ref[rows, :] = _mla_heads(_dot(ckv, wukk_ref[...]), k_rope, kg_ref,
                                       kc_ref[rows, :], ks_ref[rows, :]).astype(BF16)
        vmla_ref[sub] = (_dot_nt(wukvt_ref[...], ckv) + ones_mla_ref[...]).astype(BF16)

        qnsa_ref[rows, :] = _pair_norm(part(768, 1280), nq_gain_ref).astype(BF16)

        kn = _pair_norm(part(1280, 1536), nk_gain_ref)
        kdup_ref[rows, :] = (_dot(kn.astype(BF16), kpad_ref[...]) + kaug_ref[rows, :]).astype(BF16)
        vnsa_ref[sub] = (_dot_nt(vpick_ref[...], part(1536, 1792).astype(BF16)) + ones_nsa_ref[...]).astype(BF16)
        cmp_kv = part(1792, 2048)
        kck_ref[rows, :] = cmp_kv[:, :LANES]
        kcv_ref[rows, :] = cmp_kv[:, LANES:]
        gate = 1.0 / (1.0 + jnp.exp(-misc))
        gate_t = sum(_dot_nt(eye_ref[...], piece) for piece in _split(gate, 2))
        for u in range(V_TILE // NSA_Q):
            gate_ref[sub * (V_TILE // NSA_Q) + u] = gate_t[:, u * NSA_Q:(u + 1) * NSA_Q]


def _head_padding(heads):
    d = NSA_HEAD_DIM
    m = np.zeros((heads * d, heads * LANES), np.float32)
    for hh in range(heads):
        m[hh * d + np.arange(d), hh * LANES + np.arange(d)] = 1.0
    return m


def _projections(x2d, ln1_g, w_in, cq_g, ckv_g, w_uq, w_ukv, q_gain, k_gain, nq_gain, nk_gain):
    t = x2d.shape[0]
    kv0 = 1184
    seg = lambda a, b: w_in[:, a:b]
    kvb = lambda br, kv, g: seg(kv0 + ((br * 2 + kv) * 2 + g) * 64, kv0 + ((br * 2 + kv) * 2 + g) * 64 + 64)
    zeros = lambda n: jnp.zeros((D_MODEL, n), F32)
    wp = jnp.concatenate([
        seg(0, 640),
        seg(1952, 1976), zeros(MISC_KR - 24), seg(640, 672), seg(640, 640 + ROPE_HALF), zeros(ROPE_HALF),
        seg(672, 1184),
        kvb(1, 0, 0), kvb(1, 0, 1), kvb(2, 0, 0), kvb(2, 0, 1),
        kvb(1, 1, 0), kvb(1, 1, 1), kvb(2, 1, 0), kvb(2, 1, 1),
        kvb(0, 0, 0), kvb(0, 0, 1), kvb(0, 1, 0), kvb(0, 1, 1),
    ], axis=1).astype(BF16)
    wq3 = w_uq.reshape(MLA_Q_RANK, MLA_HEADS, MLA_QK)
    wuq = jnp.concatenate([wq3, wq3[:, :, MLA_NOPE:MLA_NOPE + ROPE_HALF],
                           jnp.zeros((MLA_Q_RANK, MLA_HEADS, ROPE_HALF), F32)], axis=2)
    wuq = wuq.reshape(MLA_Q_RANK, MLA_HEADS * LANES).astype(BF16)
    wkv = w_ukv.reshape(MLA_KV_RANK, MLA_HEADS, MLA_NOPE + MLA_V)
    wukk = jnp.pad(wkv[:, :, :MLA_NOPE], ((0, 0), (0, 0), (0, LANES - MLA_NOPE)))
    wukk = wukk.reshape(MLA_KV_RANK, MLA_HEADS * LANES).astype(BF16)
    wukvt = jnp.pad(wkv[:, :, MLA_NOPE:], ((0, 0), (0, 0), (0, V_ROWS - MLA_V)))
    wukvt = wukvt.reshape(MLA_KV_RANK, MLA_HEADS * V_ROWS).T.astype(BF16)
    slab_gain = lambda g: jnp.tile(jnp.concatenate(
        [g, g[MLA_NOPE:MLA_NOPE + ROPE_HALF], jnp.zeros((ROPE_HALF,), F32)]), MLA_HEADS)[None, :]
    qc, qs = _rope_tables(MLA_QK ** -0.5 * LOG2E)
    kc, ks = _rope_tables(1.0)
    kpad = _head_padding(4)
    kaug = np.zeros((SEQ, 4 * LANES), np.float32)
    for s in range(4):
        kaug[:, s * LANES + AUG_ONE:s * LANES + AUG_ONE + 2] = 1.0
        if s < NSA_KV_HEADS:
            kaug[np.arange(SEQ), s * LANES + AUG_BLOCK + np.arange(SEQ) // SEL_BLOCK] = 1.0
    nqg = (jnp.tile(nq_gain, NSA_HEADS) * (NSA_HEAD_DIM ** -0.5 * LOG2E))[None, :]
    nkg = jnp.concatenate([nk_gain[1], nk_gain[1], nk_gain[2], nk_gain[2]])[None, :]
    vpick = np.zeros((4 * V_ROWS, 256), np.float32)
    for s in range(4):
        vpick[s * V_ROWS + np.arange(64), s * 64 + np.arange(64)] = 1.0

    def ones_col(heads):
        col = np.zeros((heads * V_ROWS, 1), np.float32)
        col[np.arange(heads) * V_ROWS + MLA_V] = 1.0
        return col

    rows = PROJ_ROWS
    n_pos = SEQ // rows
    full = lambda a: pl.BlockSpec(a.shape, lambda i: (0,) * a.ndim)
    tab = pl.BlockSpec((rows, LANES), lambda i: (i % n_pos, 0))
    consts = [ln1_g[None, :], wp, cq_g[None, :], ckv_g[None, :], wuq, wukk, wukvt,
              slab_gain(q_gain), slab_gain(k_gain)]
    tabs = [qc, qs, kc, ks]
    mats = [nqg, nkg, jnp.asarray(kpad, BF16),
            jnp.asarray(vpick, BF16), jnp.asarray(ones_col(MLA_HEADS)), jnp.asarray(ones_col(4)),
            jnp.eye(LANES, dtype=BF16)]
    row_spec = lambda w: pl.BlockSpec((rows, w), lambda i: (i, 0))
    vt_spec = lambda w, tile: pl.BlockSpec((rows // tile, w, tile), lambda i: (i, 0, 0))
    sds = jax.ShapeDtypeStruct
    return pl.pallas_call(
        _proj_kernel,
        grid=(t // rows,),
        in_specs=([row_spec(D_MODEL)] + [full(a) for a in consts] + [tab] * 4 + [full(a) for a in mats[:3]]
                  + [pl.BlockSpec((rows, 4 * LANES), lambda i: (i % n_pos, 0))] + [full(a) for a in mats[3:]]),
        out_specs=[row_spec(1024), row_spec(1024), vt_spec(MLA_HEADS * V_ROWS, V_TILE), row_spec(512),
                   row_spec(512), vt_spec(4 * V_ROWS, V_TILE), row_spec(128), row_spec(128),
                   vt_spec(LANES, NSA_Q)],
        out_shape=[sds((t, 1024), BF16), sds((t, 1024), BF16), sds((t // V_TILE, MLA_HEADS * V_ROWS, V_TILE), BF16),
                   sds((t, 512), BF16), sds((t, 512), BF16), sds((t // V_TILE, 4 * V_ROWS, V_TILE), BF16),
                   sds((t, 128), F32), sds((t, 128), F32), sds((t // NSA_Q, LANES, NSA_Q), F32)],
        compiler_params=pltpu.CompilerParams(dimension_semantics=("parallel",),
                                             vmem_limit_bytes=VMEM_LIMIT),
        name="projections",
    )(x2d, *consts, *tabs, *mats[:3], jnp.asarray(kaug, BF16), *mats[3:])


def _compress_kernel(xk_ref, xv_ref, pe_ref, w1_ref, w2_ref, kg_ref, dupk_ref, pickv_ref, kc_ref, vct_ref):
    def mlp(x_ref, which):
        top = bot = None
        for t in range(CMP_STRIDE):
            xt = x_ref[pl.ds(t, N_CHUNK, stride=CMP_STRIDE), :]
            a = _dot((xt + pe_ref[which, t]).astype(BF16), w1_ref[which, t])
            b = _dot((xt + pe_ref[which, CMP_STRIDE + t]).astype(BF16), w1_ref[which, CMP_STRIDE + t])
            top = a if top is None else top + a
            bot = b if bot is None else bot + b
        hid = top + pltpu.roll(bot, N_CHUNK - 1, 0)
        act = (hid * (1.0 / (1.0 + jnp.exp(-hid)))).astype(BF16)
        return _dot(act, w2_ref[which])

    k = _pair_norm(mlp(xk_ref, 0), kg_ref).astype(BF16)
    v = mlp(xv_ref, 1).astype(BF16)
    for g in range(NSA_KV_HEADS):
        kc_ref[0, g] = _dot(k, dupk_ref[g]).astype(BF16)
        vct_ref[0, g] = _dot_nt(pickv_ref[g], v).astype(BF16)


def _compress(kck, kcv, cmp_pe, cmp_w1, cmp_w2, k_gain0):
    b = kck.shape[0] // SEQ
    g_ = NSA_KV_HEADS
    d = NSA_HEAD_DIM
    pe = jnp.tile(cmp_pe, (1, 1, g_))[:, :, None, :]
    w1 = cmp_w1.reshape(2, CMP_BLOCK, d, CMP_HIDDEN)
    z1 = jnp.zeros_like(w1)
    w1 = jnp.concatenate([jnp.concatenate([w1, z1], axis=3), jnp.concatenate([z1, w1], axis=3)], axis=2)
    z2 = jnp.zeros_like(cmp_w2)
    w2 = jnp.concatenate([jnp.concatenate([cmp_w2, z2], axis=2), jnp.concatenate([z2, cmp_w2], axis=2)], axis=1)
    dupk = np.zeros((g_, LANES, LANES), np.float32)
    pickv = np.zeros((g_, d, LANES), np.float32)
    for g in range(g_):
        dupk[g, g * d + np.arange(d), np.arange(d)] = 1.0
        dupk[g, g * d + np.arange(d), d + np.arange(d)] = 1.0
        pickv[g, np.arange(d), g * d + np.arange(d)] = 1.0
    full = lambda a: pl.BlockSpec(a.shape, lambda i: (0,) * a.ndim)
    consts = [pe, w1.astype(BF16), w2.astype(BF16), jnp.tile(k_gain0, g_)[None, :],
              jnp.asarray(dupk, BF16), jnp.asarray(pickv, BF16)]
    x_spec = pl.BlockSpec((SEQ, LANES), lambda i: (i, 0))
    return pl.pallas_call(
        _compress_kernel,
        grid=(b,),
        in_specs=[x_spec, x_spec] + [full(a) for a in consts],
        out_specs=[pl.BlockSpec((1, g_, N_CHUNK, LANES), lambda i: (i, 0, 0, 0)),
                   pl.BlockSpec((1, g_, d, N_CHUNK), lambda i: (i, 0, 0, 0))],
        out_shape=[jax.ShapeDtypeStruct((b, g_, N_CHUNK, LANES), BF16),
                   jax.ShapeDtypeStruct((b, g_, d, N_CHUNK), BF16)],
        compiler_params=pltpu.CompilerParams(dimension_semantics=("parallel",)),
        name="compression",
    )(kck, kcv, *consts)


def _flash_init(m_ref, acc_ref):
    m_ref[...] = jnp.full(m_ref.shape, NEG, F32)
    acc_ref[...] = jnp.zeros(acc_ref.shape, F32)


def _flash_step(s, v, m_ref, acc_ref, chunk=KEY_CHUNK):
    for c in range(s.shape[0] // chunk):
        sc = s[c * chunk:(c + 1) * chunk]
        m_old = m_ref[c]
        m_new = jnp.maximum(m_old, jnp.max(sc, axis=0, keepdims=True))
        p = jnp.exp2(sc - m_new).astype(BF16)
        acc_ref[c] = jnp.exp2(m_old - m_new) * acc_ref[c] + _dot(v[:, c * chunk:(c + 1) * chunk], p)
        m_ref[c] = m_new


def _flash_softmax(s, m_ref, p_ref, alpha_ref, chunk=KEY_CHUNK):
    for c in range(s.shape[0] // chunk):
        sc = s[c * chunk:(c + 1) * chunk]
        m_old = m_ref[c]
        m_new = jnp.maximum(m_old, jnp.max(sc, axis=0, keepdims=True))
        alpha_ref[c] = jnp.exp2(m_old - m_new)
        p_ref[c * chunk:(c + 1) * chunk, :] = jnp.exp2(sc - m_new).astype(BF16)
        m_ref[c] = m_new


def _flash_accumulate(v, p_ref, alpha_ref, acc_ref, chunk=KEY_CHUNK):
    for c in range(p_ref.shape[0] // chunk):
        rows = slice(c * chunk, (c + 1) * chunk)
        acc_ref[c] = alpha_ref[c] * acc_ref[c] + _dot(v[:, rows], p_ref[rows, :])


def _flash_finish(m_ref, acc_ref):
    n = m_ref.shape[0]
    m = m_ref[0]
    for c in range(1, n):
        m = jnp.maximum(m, m_ref[c])
    acc = sum(jnp.exp2(m_ref[c] - m) * acc_ref[c] for c in range(n))
    return acc[:MLA_V] / acc[MLA_V:MLA_V + 1]


def _pipelined_pairs(sched_ref, row, n, scores, softmax, accumulate, unroll=PAIR_UNROLL):
    assert n >= 2 and unroll % 2 == 0

    def pair(i):
        i = jnp.minimum(i, n - 1)
        return sched_ref[row, i], sched_ref[row + 1, i]

    def steps(first, count):
        for u in range(count):
            i, slot = first + u, u % 2
            if accumulate is None:
                scores(*pair(i + 1), 1 - slot)
                softmax(*pair(i), slot)
            else:
                scores(*pair(i + 2), slot)
                softmax(*pair(i + 1), 1 - slot)
                accumulate(*pair(i), slot)

    scores(*pair(0), 0)
    if accumulate is not None:
        scores(*pair(1), 1)
        softmax(*pair(0), 0)
    loops = n // unroll
    lax.fori_loop(0, loops, lambda j, carry: (steps(j * unroll, unroll), carry)[1], 0)
    steps(loops * unroll, n - loops * unroll)


def _mla_pairs():
    nq, nk = SEQ // MLA_Q, SEQ // MLA_K
    full = [(qt, kt) for kt in range(nk) for qt in range(nq) if (kt + 1) * MLA_K <= qt * MLA_Q]
    diag = [(qt, (qt * MLA_Q) // MLA_K) for qt in range(nq)]
    return full, diag


def _mla_kernel(sched_ref, q_ref, k_ref, vt_ref, eye_ref, o_ref, m_ref, acc_ref, s_ref):
    sub = MLA_K // V_TILE
    full, diag = _mla_pairs()
    rel = (lax.broadcasted_iota(jnp.int32, (MLA_K, MLA_Q), 1)
           - lax.broadcasted_iota(jnp.int32, (MLA_K, MLA_Q), 0))
    _flash_init(m_ref, acc_ref)

    def scores(qt, kt, slot, masked):
        qrows = pl.ds(pl.multiple_of(qt * MLA_Q, MLA_Q), MLA_Q)
        krows = pl.ds(pl.multiple_of(kt * MLA_K, MLA_K), MLA_K)
        for hh in range(2):
            cols = slice(hh * LANES, (hh + 1) * LANES)
            s = _dot_nt(k_ref[krows, cols], q_ref[qrows, cols])
            if masked:
                s = jnp.where(rel >= kt * MLA_K - qt * MLA_Q, s, NEG)
            s_ref[slot, hh] = s

    def update(qt, kt, slot):
        for hh in range(2):
            v = jnp.concatenate([vt_ref[kt * sub + i, hh * V_ROWS:(hh + 1) * V_ROWS, :] for i in range(sub)],
                                axis=1)
            _flash_step(s_ref[slot, hh], v, m_ref.at[hh, qt], acc_ref.at[hh, qt])

    _pipelined_pairs(sched_ref, 0, len(full), lambda qt, kt, slot: scores(qt, kt, slot, False), update, None)
    _pipelined_pairs(sched_ref, 2, len(diag), lambda qt, kt, slot: scores(qt, kt, slot, True), update, None)

    for qt in range(SEQ // MLA_Q):
        o_t = jnp.concatenate([_flash_finish(m_ref.at[hh, qt], acc_ref.at[hh, qt])
                               for hh in range(2)], axis=0)
        o_ref[qt * MLA_Q:(qt + 1) * MLA_Q, :] = _dot_nt(eye_ref[...], o_t.astype(BF16)).astype(BF16)


def _schedule(*pair_lists):
    n = max(len(p) for p in pair_lists)
    out = np.zeros((2 * len(pair_lists), n), np.int32)
    for i, pairs in enumerate(pair_lists):
        out[2 * i, :len(pairs)] = [a for a, _ in pairs]
        out[2 * i + 1, :len(pairs)] = [b for _, b in pairs]
    return jnp.asarray(out)


def _mla_attention(q, k, vt):
    b = q.shape[0] // SEQ
    nq = SEQ // MLA_Q
    nv = SEQ // V_TILE
    chunks = MLA_K // KEY_CHUNK
    return pl.pallas_call(
        _mla_kernel,
        grid=(b, MLA_HEADS // 2),
        in_specs=[pl.BlockSpec(memory_space=pltpu.SMEM),
                  pl.BlockSpec((SEQ, 2 * LANES), lambda i, hp: (i, hp)),
                  pl.BlockSpec((SEQ, 2 * LANES), lambda i, hp: (i, hp)),
                  pl.BlockSpec((nv, 2 * V_ROWS, V_TILE), lambda i, hp: (i, hp, 0)),
                  pl.BlockSpec((MLA_Q, MLA_Q), lambda i, hp: (0, 0))],
        out_specs=pl.BlockSpec((SEQ, LANES), lambda i, hp: (i, hp)),
        out_shape=jax.ShapeDtypeStruct((q.shape[0], MLA_HEADS * MLA_V), BF16),
        scratch_shapes=[pltpu.VMEM((2, nq, chunks, 1, MLA_Q), F32),
                        pltpu.VMEM((2, nq, chunks, V_ROWS, MLA_Q), F32),
                        pltpu.VMEM((2, 2, MLA_K, MLA_Q), F32)],
        compiler_params=pltpu.CompilerParams(
            dimension_semantics=("parallel", "parallel"), vmem_limit_bytes=VMEM_LIMIT),
        name="mla_attention",
    )(_schedule(*_mla_pairs()), q, k, vt, jnp.eye(MLA_Q, dtype=BF16))


def _select_kernel(q_ref, kc_ref, vct_ref, tabc_ref, ovt_ref, eye_ref, qpad_ref, qaug_ref, ocmp_ref):
    step = pl.program_id(1)
    t = SELECT_Q
    heads = range(NSA_GROUP)
    tiles = range(t // NSA_Q)

    def group(g):
        q_pad = _dot(q_ref[:, g * 2 * LANES:(g + 1) * 2 * LANES], qpad_ref[...]).astype(BF16)
        low = lax.broadcasted_iota(jnp.int32, (t, LANES), 1) < NSA_HEAD_DIM
        zero = jnp.zeros((t, LANES), BF16)
        pairs = [q_ref[:, (2 * g + j) * LANES:(2 * g + j + 1) * LANES] for j in range(NSA_GROUP // 2)]
        q4 = jnp.concatenate([jnp.where(low == (r % 2 == 0), pairs[r // 2], zero) for r in heads], axis=0)

        per_tile = NSA_Q // CMP_STRIDE
        rows = [pl.ds(pl.multiple_of(CMP_TAB_BASE - (step * len(tiles) + u) * per_tile, per_tile), N_CHUNK)
                for u in tiles]
        bias_c = jnp.concatenate([tabc_ref[g * NSA_GROUP + r, rows[u], :] for r in heads for u in tiles], axis=1)
        valid = bias_c > 0.5 * NEG
        sc = jnp.where(valid, _dot_nt(kc_ref[0, g], q4) + bias_c, NEG)
        mx = jnp.max(sc, axis=0, keepdims=True)
        p = jnp.where(valid, jnp.exp2(sc - mx), 0.0)
        pc = p / jnp.maximum(jnp.sum(p, axis=0, keepdims=True), 1e-30)
        o_cmp = _dot(vct_ref[0, g], pc.astype(BF16)).astype(BF16)
        for u in tiles:
            ocmp_ref[0, g, u] = jnp.concatenate(
                [o_cmp[:, r * t + u * NSA_Q:r * t + (u + 1) * NSA_Q] for r in heads], axis=0)

        psum = sum(pc[:, r * t:(r + 1) * t] for r in heads)
        imp = sum(_dot(ovt_ref[...], piece) for piece in _split(psum, 3))
        jj = lax.broadcasted_iota(jnp.int32, (N_SEL, t), 0)
        blk = (step * t + lax.broadcasted_iota(jnp.int32, (N_SEL, t), 1)) // SEL_BLOCK
        ok = jj <= blk
        forced = ok & ((jj == 0) | (jj == blk) | (jj == blk - 1))
        score = jnp.where(forced, FORCE_SCORE, jnp.where(ok, imp, -jnp.inf))
        rank = jnp.zeros((N_SEL, t), jnp.int32)
        for i in range(N_SEL):
            ci = score[i:i + 1, :]
            beats = (ci > score) | ((ci == score) & (jj > i))
            rank = rank + beats.astype(jnp.int32)
        negsel = jnp.where(ok & (rank < SEL_TOP_N), 0.0, NEG).astype(BF16)
        spread = jnp.concatenate([jnp.zeros((AUG_BLOCK, t), BF16), negsel,
                                  jnp.zeros((LANES - AUG_ONE, t), BF16)], axis=0)
        mask_lanes = _dot_nt(eye_ref[...], spread).astype(BF16)
        lane = lax.broadcasted_iota(jnp.int32, (t, LANES), 1)
        for r in heads:
            far = tabc_ref[g * NSA_GROUP + r, 0:1, :]
            far_hi = far.astype(BF16)
            far_lo = (far - far_hi.astype(F32)).astype(BF16)
            aug = jnp.where(lane == AUG_ONE, far_hi, jnp.where(lane == AUG_ONE + 1, far_lo, mask_lanes))
            q_aug = jnp.where(lane < AUG_BLOCK, q_pad[:, r * LANES:(r + 1) * LANES], aug)
            for u in tiles:
                qaug_ref[0, g, u, r * NSA_Q:(r + 1) * NSA_Q, :] = q_aug[u * NSA_Q:(u + 1) * NSA_Q]

    for g in range(NSA_KV_HEADS):
        group(g)


def _nsa_select(q, kc, vct, tables):
    b = q.shape[0] // SEQ
    t = SELECT_Q
    steps = SEQ // t
    nq = SEQ // NSA_Q
    per = t // NSA_Q
    g_ = NSA_KV_HEADS
    c0 = np.arange(N_CHUNK)[None, :] * CMP_STRIDE
    j0 = np.arange(N_SEL)[:, None] * SEL_BLOCK
    ovt = np.clip(np.minimum(c0 + CMP_BLOCK, j0 + SEL_BLOCK) - np.maximum(c0, j0), 0, None) / CMP_BLOCK
    ovt[:, N_CHUNK - 1:] = 0.0
    return pl.pallas_call(
        _select_kernel,
        grid=(b, steps),
        in_specs=[pl.BlockSpec((t, g_ * 2 * LANES), lambda i, j: (i * steps + j, 0)),
                  pl.BlockSpec((1, g_, N_CHUNK, LANES), lambda i, j: (i, 0, 0, 0)),
                  pl.BlockSpec((1, g_, NSA_HEAD_DIM, N_CHUNK), lambda i, j: (i, 0, 0, 0)),
                  pl.BlockSpec((NSA_HEADS, CMP_TAB_ROWS, LANES), lambda i, j: (0, WIN_ROWS // CMP_TAB_ROWS, 0)),
                  pl.BlockSpec((N_SEL, N_CHUNK), lambda i, j: (0, 0)),
                  pl.BlockSpec((t, t), lambda i, j: (0, 0)),
                  pl.BlockSpec((2 * LANES, NSA_GROUP * LANES), lambda i, j: (0, 0))],
        out_specs=[pl.BlockSpec((1, g_, per, NSA_GROUP * NSA_Q, LANES), lambda i, j: (i, 0, j, 0, 0)),
                   pl.BlockSpec((1, g_, per, NSA_GROUP * NSA_HEAD_DIM, NSA_Q), lambda i, j: (i, 0, j, 0, 0))],
        out_shape=[jax.ShapeDtypeStruct((b, g_, nq, NSA_GROUP * NSA_Q, LANES), BF16),
                   jax.ShapeDtypeStruct((b, g_, nq, NSA_GROUP * NSA_HEAD_DIM, NSA_Q), BF16)],
        compiler_params=pltpu.CompilerParams(
            dimension_semantics=("parallel", "parallel"), vmem_limit_bytes=VMEM_LIMIT),
        name="nsa_select",
    )(q, kc, vct, tables, jnp.asarray(ovt, BF16), jnp.eye(t, dtype=BF16),
      jnp.asarray(_head_padding(NSA_GROUP), BF16))


def _nsa_pairs():
    nq, nk, per = SEQ // NSA_Q, SEQ // NSA_K, NSA_K // NSA_Q
    sel = [(qt, kt) for kt in range(nk) for qt in range(nq) if kt * NSA_K <= qt * NSA_Q]
    far = [(qt, kt) for qt, kt in sel if qt - kt * per >= SEL_FAR_TILE]
    near = [(qt, kt) for qt, kt in sel if qt - kt * per < SEL_FAR_TILE]
    win = [(qt, kt) for kt in range(nk) for qt in range(nq)
           if max(qt * NSA_Q - (WINDOW - 1), 0) // NSA_K <= kt and kt * NSA_K <= qt * NSA_Q]
    return far, near, win


def _attend_kernel(sched_ref, qaug_ref, ksel_ref, kwin_ref, vsel_ref, vwin_ref, ocmp_ref, gate_ref,
                   tabw_ref, eye_ref, o_ref, m_ref, acc_ref, s_ref, p_ref, alpha_ref):
    t = NSA_Q
    heads = range(NSA_GROUP)
    far, near, win = _nsa_pairs()
    _flash_init(m_ref, acc_ref)

    def scores(qt, kt, slot, k_ref, add_bias):
        s = _dot_nt(k_ref[pl.ds(pl.multiple_of(kt * NSA_K, NSA_K), NSA_K), :], qaug_ref[0, 0, qt])
        if add_bias:
            off = qt - kt * (NSA_K // NSA_Q)
            bias_rows = pl.ds(pl.multiple_of(off * NSA_K, NSA_K), NSA_K)
            s = s + jnp.concatenate([tabw_ref[r, bias_rows, :] for r in heads], axis=1)
        s_ref[slot] = s

    def softmax(qt, kt, slot, branch):
        _flash_softmax(s_ref[slot], m_ref.at[branch, qt], p_ref.at[slot], alpha_ref.at[slot], chunk=NSA_K)

    def accumulate(qt, kt, slot, v_ref, branch):
        _flash_accumulate(v_ref[kt], p_ref.at[slot], alpha_ref.at[slot], acc_ref.at[branch, qt], chunk=NSA_K)

    for row, pairs, k_ref, v_ref, branch, add_bias in ((0, far, ksel_ref, vsel_ref, 0, False),
                                                       (2, near, ksel_ref, vsel_ref, 0, True),
                                                       (4, win, kwin_ref, vwin_ref, 1, True)):
        _pipelined_pairs(sched_ref, row, len(pairs),
                         lambda qt, kt, slot, k_ref=k_ref, add_bias=add_bias: scores(qt, kt, slot, k_ref, add_bias),
                         lambda qt, kt, slot, branch=branch: softmax(qt, kt, slot, branch),
                         lambda qt, kt, slot, v_ref=v_ref, branch=branch: accumulate(qt, kt, slot, v_ref, branch),
                         unroll=NSA_UNROLL)

    d = NSA_HEAD_DIM
    group = pl.program_id(1)
    for qt in range(SEQ // t):
        o_cmp = ocmp_ref[0, 0, qt].astype(F32)
        o_sel, o_win = [_flash_finish(m_ref.at[br, qt], acc_ref.at[br, qt]) for br in range(2)]
        mixed = []
        for r in heads:
            gate = [gate_ref[qt, pl.ds(br * NSA_HEADS + group * NSA_GROUP + r, 1), :]
                    for br in range(N_BRANCH)]
            mixed.append(gate[0] * o_cmp[r * d:(r + 1) * d] + gate[1] * o_sel[:, r * t:(r + 1) * t]
                         + gate[2] * o_win[:, r * t:(r + 1) * t])
        o_ref[qt * t:(qt + 1) * t, :] = _dot_nt(eye_ref[...], jnp.concatenate(mixed, axis=0).astype(BF16)).astype(BF16)


def _nsa_attend(qaug, kaug, vt, ocmp, gates, tables):
    b = kaug.shape[0] // SEQ
    t = NSA_Q
    nq = SEQ // t
    nv = SEQ // V_TILE
    g_ = NSA_KV_HEADS
    k_spec = lambda off: pl.BlockSpec((SEQ, LANES), lambda i, g: (i, off + g))
    v_spec = lambda off: pl.BlockSpec((nv, V_ROWS, V_TILE), lambda i, g: (i, off + g, 0))
    return pl.pallas_call(
        _attend_kernel,
        grid=(b, g_),
        in_specs=[pl.BlockSpec(memory_space=pltpu.SMEM),
                  pl.BlockSpec((1, 1, nq, NSA_GROUP * t, LANES), lambda i, g: (i, g, 0, 0, 0)),
                  k_spec(0), k_spec(2), v_spec(0), v_spec(2),
                  pl.BlockSpec((1, 1, nq, NSA_GROUP * NSA_HEAD_DIM, t), lambda i, g: (i, g, 0, 0, 0)),
                  pl.BlockSpec((nq, LANES, t), lambda i, g: (i, 0, 0)),
                  pl.BlockSpec((NSA_GROUP, WIN_ROWS, LANES), lambda i, g: (g, 0, 0)),
                  pl.BlockSpec((t, t), lambda i, g: (0, 0))],
        out_specs=pl.BlockSpec((SEQ, 2 * LANES), lambda i, g: (i, g)),
        out_shape=jax.ShapeDtypeStruct((kaug.shape[0], NSA_HEADS * NSA_HEAD_DIM), BF16),
        scratch_shapes=[pltpu.VMEM((2, nq, 1, 1, NSA_GROUP * t), F32),
                        pltpu.VMEM((2, nq, 1, V_ROWS, NSA_GROUP * t), F32),
                        pltpu.VMEM((2, NSA_K, NSA_GROUP * t), F32),
                        pltpu.VMEM((2, NSA_K, NSA_GROUP * t), BF16),
                        pltpu.VMEM((2, 1, 1, NSA_GROUP * t), F32)],
        compiler_params=pltpu.CompilerParams(
            dimension_semantics=("parallel", "parallel"), vmem_limit_bytes=VMEM_LIMIT),
        name="nsa_attend",
    )(_schedule(*_nsa_pairs()), qaug, kaug, kaug, vt, vt, ocmp, gates, tables, jnp.eye(t, dtype=BF16))


def _tail_kernel(x_ref, oa_ref, ob_ref, ga_ref, gb_ref, woa_ref, wob_ref, ln2_ref, wup_ref, wdn_ref,
                 out_ref, h2_ref):
    width = MLA_HEADS * MLA_V
    a = (_rms(oa_ref[...].astype(F32), width) * ga_ref[...]).astype(BF16)
    b = (_rms(ob_ref[...].astype(F32), width) * gb_ref[...]).astype(BF16)
    x1 = x_ref[...] + _dot(a, woa_ref[...]) + _dot(b, wob_ref[...])
    h2_ref[...] = (_rms(x1, D_MODEL) * ln2_ref[...]).astype(BF16)
    out_ref[...] = x1

    def body(f, carry):
        u = jnp.maximum(_dot(h2_ref[...], wup_ref[f]), 0.0)
        out_ref[...] += _dot((u * u).astype(BF16), wdn_ref[f])
        return carry

    lax.fori_loop(0, D_FF // FF_CHUNK, body, 0)


def _tail(x2d, o_a, o_b, gn_a, gn_b, w_o, ln2_g, w_up, w_down):
    t = x2d.shape[0]
    rows = TAIL_ROWS
    width = MLA_HEADS * MLA_V
    once = lambda a: pl.BlockSpec(a.shape, lambda i: (0,) * a.ndim)
    nf = D_FF // FF_CHUNK
    wup = w_up.astype(BF16).reshape(D_MODEL, nf, FF_CHUNK).transpose(1, 0, 2)
    wdn = w_down.astype(BF16).reshape(nf, FF_CHUNK, D_MODEL)
    consts = [gn_a[None, :], gn_b[None, :], w_o[:width].astype(BF16), w_o[width:].astype(BF16),
              ln2_g[None, :], wup, wdn]
    row_spec = lambda w: pl.BlockSpec((rows, w), lambda i: (i, 0))
    return pl.pallas_call(
        _tail_kernel,
        grid=(t // rows,),
        in_specs=[row_spec(D_MODEL), row_spec(width), row_spec(width)] + [once(a) for a in consts],
        out_specs=row_spec(D_MODEL),
        out_shape=jax.ShapeDtypeStruct((t, D_MODEL), F32),
        scratch_shapes=[pltpu.VMEM((rows, D_MODEL), BF16)],
        compiler_params=pltpu.CompilerParams(dimension_semantics=("parallel",),
                                             vmem_limit_bytes=VMEM_LIMIT),
        name="tail",
    )(x2d, o_a, o_b, *consts)


def kernel(x, ln1_g, w_in, mla_cq_norm_g, mla_ckv_norm_g, mla_w_uq, mla_w_ukv, mla_q_gain, mla_k_gain,
           nsa_q_gain, nsa_k_gain, nsa_cmp_pe, nsa_cmp_w1, nsa_cmp_w2, rel_bias, grp_norm_mla,
           grp_norm_nsa, w_o, ln2_g, w_up, w_down):
    b, s, d = x.shape
    assert (s, d) == (SEQ, D_MODEL) and ln1_g.shape[0] == 1
    x2d = x.reshape(b * s, d)
    tables = _bias_tables(rel_bias)
    q_mla, k_mla, vt_mla, q_nsa, kdup, vt_nsa, kck, kcv, gates = _projections(
        x2d, ln1_g[0], w_in[0], mla_cq_norm_g[0], mla_ckv_norm_g[0], mla_w_uq[0], mla_w_ukv[0],
        mla_q_gain[0], mla_k_gain[0], nsa_q_gain[0], nsa_k_gain[0])
    kc, vct = _compress(kck, kcv, nsa_cmp_pe[0], nsa_cmp_w1[0], nsa_cmp_w2[0], nsa_k_gain[0, 0])
    o_a = _mla_attention(q_mla, k_mla, vt_mla)
    q_aug, ocmp = _nsa_select(q_nsa, kc, vct, tables)
    o_b = _nsa_attend(q_aug, kdup, vt_nsa, ocmp, gates, tables)
    out = _tail(x2d, o_a, o_b, grp_norm_mla[0], grp_norm_nsa[0], w_o[0], ln2_g[0], w_up[0], w_down[0])
    return out.reshape(b, s, d)
```

```python
import math

import numpy as np
import jax
import jax.numpy as jnp
from jax import lax
from jax.experimental import pallas as pl
from jax.experimental.pallas import tpu as pltpu

F32 = jnp.float32
BF16 = jnp.bfloat16

D_MODEL = 1024
SEQ = 2048
MLA_HEADS = 8
MLA_NOPE = 64
MLA_ROPE = 32
MLA_V = 64
MLA_QK = MLA_NOPE + MLA_ROPE
MLA_Q_RANK = 384
MLA_KV_RANK = 256
ROPE_THETA = 10000.0
NSA_HEADS = 8
NSA_KV_HEADS = 2
NSA_GROUP = NSA_HEADS // NSA_KV_HEADS
NSA_HEAD_DIM = 64
N_BRANCH = 3
CMP_BLOCK = 32
CMP_STRIDE = 16
CMP_HIDDEN = 128
SEL_BLOCK = 64
SEL_TOP_N = 16
WINDOW = 512
FORCE_SCORE = 1e4
REL_BUCKETS = 32
REL_MAX_DIST = 128
D_FF = 4 * D_MODEL
EPS = 1e-6

LANES = 128
NEG = -1e30
N_CHUNK = SEQ // CMP_STRIDE
N_SEL = SEQ // SEL_BLOCK
NSA_Q = 128
NSA_K = 256
SELECT_Q = 512
N_BIAS_TILES = (WINDOW + NSA_K - NSA_Q) // NSA_Q + 1
SEL_FAR_TILE = (REL_MAX_DIST + NSA_K) // NSA_Q
WIN_ROWS = N_BIAS_TILES * NSA_K
AUG_BLOCK = NSA_HEAD_DIM
AUG_ONE = AUG_BLOCK + N_SEL
CMP_TAB_BASE = (SEQ // NSA_Q - 1) * (NSA_Q // CMP_STRIDE)
CMP_TAB_ROWS = 256
MLA_Q = 256
MLA_K = 512
V_TILE = 256
PROJ_ROWS = 256
TAIL_ROWS = 512
FF_CHUNK = 1024
KEY_CHUNK = 256
PAIR_UNROLL = 6
NSA_UNROLL = 12
V_ROWS = 80
LOG2E = math.log2(math.e)
VMEM_LIMIT = 56 * 1024 * 1024

assert PROJ_ROWS % V_TILE == 0 and MLA_K % V_TILE == 0 and NSA_K == V_TILE


def _dot(a, b):
    return jnp.dot(a, b, preferred_element_type=F32)


def _dot_nt(a, b):
    return lax.dot_general(a, b, (((1,), (1,)), ((), ())), preferred_element_type=F32)


def _split(a, terms):
    pieces = []
    rem = a
    for _ in range(terms):
        piece = rem.astype(BF16)
        pieces.append(piece)
        rem = rem - piece.astype(F32)
    return pieces


def _split_dot(a, b, terms=2):
    return sum(_dot(p, b) for p in _split(a, terms))


def _rms(x, width):
    return x * lax.rsqrt(jnp.sum(x * x, axis=-1, keepdims=True) * (1.0 / width) + EPS)


def _t5_bucket_np(dist):
    n = np.maximum(dist, 0)
    max_exact = REL_BUCKETS // 2
    large = max_exact + (np.log(np.maximum(n, 1).astype(np.float32) / max_exact)
                         / math.log(REL_MAX_DIST / max_exact)
                         * (REL_BUCKETS - max_exact)).astype(np.int32)
    large = np.minimum(large, REL_BUCKETS - 1)
    return np.where(n < max_exact, n, large).astype(np.int32)


def _bias_index_table():
    i = np.arange(NSA_Q)[None, :]
    parts = []
    j = np.arange(NSA_K)[:, None]
    for off in range(N_BIAS_TILES):
        d = off * NSA_Q + i - j
        parts.append(np.where((d >= 0) & (d < WINDOW), _t5_bucket_np(d), -1))
    r = np.arange(CMP_TAB_ROWS)[:, None]
    dist_c = (CMP_TAB_BASE - r) * CMP_STRIDE + i - (CMP_BLOCK - 1)
    parts.append(np.where(dist_c >= 0, _t5_bucket_np(dist_c), -1))
    return np.concatenate(parts, axis=0).astype(np.int32)


def _bias_table_kernel(rb_ref, idx_ref, out_ref):
    idx = idx_ref[...]
    hit = [idx == b for b in range(REL_BUCKETS)]
    is_window = pl.program_id(0) < WIN_ROWS // idx.shape[0]
    for h in range(NSA_HEADS):
        far = jnp.where(is_window, rb_ref[h, REL_BUCKETS - 1], 0.0)
        acc = jnp.full(idx.shape, NEG, F32)
        for b in range(REL_BUCKETS):
            acc = jnp.where(hit[b], (rb_ref[h, b] - far) * LOG2E, acc)
        out_ref[h] = acc


def _bias_tables(rel_bias):
    idx = jnp.asarray(_bias_index_table())
    rows = idx.shape[0]
    blk = CMP_TAB_ROWS
    return pl.pallas_call(
        _bias_table_kernel,
        grid=(rows // blk,),
        in_specs=[pl.BlockSpec(memory_space=pltpu.SMEM),
                  pl.BlockSpec((blk, LANES), lambda r: (r, 0))],
        out_specs=pl.BlockSpec((NSA_HEADS, blk, LANES), lambda r: (0, r, 0)),
        out_shape=jax.ShapeDtypeStruct((NSA_HEADS, rows, LANES), F32),
        name="bias_tables",
    )(rel_bias.T, idx)


ROPE_HALF = MLA_ROPE // 2
MISC_KR = 64


def _rope_tables(scale):
    inv = 1.0 / (ROPE_THETA ** (jnp.arange(ROPE_HALF, dtype=F32) / ROPE_HALF))
    ang = jnp.arange(SEQ, dtype=F32)[:, None] * inv[None, :]
    cos, sin = jnp.cos(ang), jnp.sin(ang)
    ones = jnp.ones((SEQ, MLA_NOPE), F32)
    z32 = jnp.zeros((SEQ, LANES - MLA_QK), F32)
    z64 = jnp.zeros((SEQ, MLA_NOPE), F32)
    c = jnp.concatenate([ones, cos, cos, z32], axis=1) * scale
    s = jnp.concatenate([z64, -sin, sin, z32], axis=1) * scale
    return c, s


def _pair_norm(x, gain_ref):
    low = lax.broadcasted_iota(jnp.int32, (x.shape[0], LANES), 1) < NSA_HEAD_DIM
    outs = []
    for j in range(x.shape[1] // LANES):
        cols = slice(j * LANES, (j + 1) * LANES)
        sl = x[:, cols]
        sq = sl * sl
        tot = jnp.sum(sq, axis=-1, keepdims=True)
        lo = jnp.sum(jnp.where(low, sq, 0.0), axis=-1, keepdims=True)
        rs = jnp.where(low, lax.rsqrt(lo * (1.0 / NSA_HEAD_DIM) + EPS),
                       lax.rsqrt((tot - lo) * (1.0 / NSA_HEAD_DIM) + EPS))
        outs.append(sl * rs * gain_ref[:, cols])
    return jnp.concatenate(outs, axis=1)


def _mla_query_heads(x, gain_ref, c, s):
    in_head = lax.broadcasted_iota(jnp.int32, (x.shape[0], LANES), 1) < MLA_QK
    outs = []
    for h in range(MLA_HEADS):
        cols = slice(h * LANES, (h + 1) * LANES)
        sl = x[:, cols]
        ss = jnp.sum(jnp.where(in_head, sl * sl, 0.0), axis=-1, keepdims=True)
        xn = sl * lax.rsqrt(ss * (1.0 / MLA_QK) + EPS) * gain_ref[:, cols]
        outs.append(xn * c + pltpu.roll(xn, LANES - ROPE_HALF, 1) * s)
    return jnp.concatenate(outs, axis=1)


def _mla_key_heads(x, k_rope, k_swap, c, s):
    in_head = lax.broadcasted_iota(jnp.int32, (x.shape[0], LANES), 1) < MLA_QK
    turned = k_swap * s
    outs = []
    for h in range(MLA_HEADS):
        sl = x[:, h * LANES:(h + 1) * LANES] + k_rope
        ss = jnp.sum(jnp.where(in_head, sl * sl, 0.0), axis=-1, keepdims=True)
        outs.append(lax.rsqrt(ss * (1.0 / MLA_QK) + EPS) * (sl * c + turned))
    return jnp.concatenate(outs, axis=1)


COL_KV = MLA_Q_RANK
COL_MISC = COL_KV + MLA_KV_RANK
COL_SWAP = COL_MISC + LANES
COL_NQ = COL_SWAP + LANES
COL_NK = COL_NQ + NSA_HEADS * NSA_HEAD_DIM
COL_NV = COL_NK + 4 * NSA_HEAD_DIM
COL_CMP = COL_NV + 4 * NSA_HEAD_DIM
COL_END = COL_CMP + 4 * NSA_HEAD_DIM


def _proj_kernel(x_ref, ln1_ref, wp_ref, cqg_ref, ckvg_ref, wuq_ref, wukk_ref, wukvt_ref,
                 qg_ref, qc_ref, qs_ref, kc_ref, ks_ref,
                 nq_gain_ref, nk_gain_ref, kpad_ref, kaug_ref, vpick_ref, ones_mla_ref, ones_nsa_ref, eye_ref,
                 qmla_ref, kmla_ref, vmla_ref, qnsa_ref, kdup_ref, vnsa_ref, kck_ref, kcv_ref, gate_ref):
    for sub in range(PROJ_ROWS // V_TILE):
        rows = slice(sub * V_TILE, (sub + 1) * V_TILE)
        x = x_ref[rows, :]
        h = (_rms(x, D_MODEL) * ln1_ref[...]).astype(BF16)
        part = lambda a, b: _dot(h, wp_ref[:, a:b])

        cq = (_rms(part(0, COL_KV), MLA_Q_RANK) * cqg_ref[...]).astype(BF16)
        qmla_ref[rows, :] = _mla_query_heads(_dot(cq, wuq_ref[...]), qg_ref,
                                             qc_ref[rows, :], qs_ref[rows, :]).astype(BF16)

        kv_misc = part(COL_KV, COL_NQ)
        ckv = (_rms(kv_misc[:, :MLA_KV_RANK], MLA_KV_RANK) * ckvg_ref[...]).astype(BF16)
        misc = kv_misc[:, COL_MISC - COL_KV:COL_SWAP - COL_KV]
        lane = lax.broadcasted_iota(jnp.int32, misc.shape, 1)
        k_rope = jnp.where((lane >= MISC_KR) & (lane < MISC_KR + MLA_ROPE), misc, 0.0)
        kmla_ref[rows, :] = _mla_key_heads(_dot(ckv, wukk_ref[...]), k_rope, kv_misc[:, COL_SWAP - COL_KV:],
                                           kc_ref[rows, :], ks_ref[rows, :]).astype(BF16)
        vmla_ref[sub] = (_dot_nt(wukvt_ref[...], ckv) + ones_mla_ref[...]).astype(BF16)

        qnsa_ref[rows, :] = _pair_norm(part(COL_NQ, COL_NK), nq_gain_ref).astype(BF16)

        kn = _pair_norm(part(COL_NK, COL_NV), nk_gain_ref)
        kdup_ref[rows, :] = (_dot(kn.astype(BF16), kpad_ref[...]) + kaug_ref[rows, :]).astype(BF16)
        vnsa_ref[sub] = (_dot_nt(vpick_ref[...], part(COL_NV, COL_CMP).astype(BF16)) + ones_nsa_ref[...]).astype(BF16)
        cmp_kv = part(COL_CMP, COL_END)
        kck_ref[rows, :] = cmp_kv[:, :LANES]
        kcv_ref[rows, :] = cmp_kv[:, LANES:]
        gate = 1.0 / (1.0 + jnp.exp(-misc))
        gate_t = sum(_dot_nt(eye_ref[...], piece) for piece in _split(gate, 2))
        for u in range(V_TILE // NSA_Q):
            gate_ref[sub * (V_TILE // NSA_Q) + u] = gate_t[:, u * NSA_Q:(u + 1) * NSA_Q]


def _head_padding(heads):
    d = NSA_HEAD_DIM
    m = np.zeros((heads * d, heads * LANES), np.float32)
    for hh in range(heads):
        m[hh * d + np.arange(d), hh * LANES + np.arange(d)] = 1.0
    return m


def _projections(x2d, ln1_g, w_in, cq_g, ckv_g, w_uq, w_ukv, q_gain, k_gain, nq_gain, nk_gain):
    t = x2d.shape[0]
    kv0 = 1184
    seg = lambda a, b: w_in[:, a:b].astype(BF16)
    kvb = lambda br, kv, g: seg(kv0 + ((br * 2 + kv) * 2 + g) * 64, kv0 + ((br * 2 + kv) * 2 + g) * 64 + 64)
    zeros = lambda n: jnp.zeros((D_MODEL, n), BF16)
    kr = COL_MISC
    gate_cols = N_BRANCH * NSA_HEADS
    wp = jnp.concatenate([
        seg(0, COL_MISC),
        seg(1952, 1952 + gate_cols), zeros(MISC_KR - gate_cols), seg(kr, kr + MLA_ROPE),
        zeros(LANES - MISC_KR - MLA_ROPE),
        zeros(MISC_KR), seg(kr + ROPE_HALF, kr + MLA_ROPE), seg(kr, kr + ROPE_HALF), zeros(LANES - MISC_KR - MLA_ROPE),
        seg(672, 1184),
        kvb(1, 0, 0), kvb(1, 0, 1), kvb(2, 0, 0), kvb(2, 0, 1),
        kvb(1, 1, 0), kvb(1, 1, 1), kvb(2, 1, 0), kvb(2, 1, 1),
        kvb(0, 0, 0), kvb(0, 0, 1), kvb(0, 1, 0), kvb(0, 1, 1),
    ], axis=1)
    assert wp.shape[1] == COL_END
    wq3 = w_uq.reshape(MLA_Q_RANK, MLA_HEADS, MLA_QK).astype(BF16)
    wuq = jnp.concatenate([wq3, wq3[:, :, MLA_NOPE:MLA_NOPE + ROPE_HALF],
                           jnp.zeros((MLA_Q_RANK, MLA_HEADS, ROPE_HALF), BF16)], axis=2)
    wuq = wuq.reshape(MLA_Q_RANK, MLA_HEADS * LANES)
    wkv = w_ukv.reshape(MLA_KV_RANK, MLA_HEADS, MLA_NOPE + MLA_V)
    wukk = jnp.pad(wkv[:, :, :MLA_NOPE], ((0, 0), (0, 0), (0, LANES - MLA_NOPE)))
    wukk = wukk.reshape(MLA_KV_RANK, MLA_HEADS * LANES).astype(BF16)
    wukvt = jnp.pad(wkv[:, :, MLA_NOPE:], ((0, 0), (0, 0), (0, V_ROWS - MLA_V)))
    wukvt = wukvt.reshape(MLA_KV_RANK, MLA_HEADS * V_ROWS).T.astype(BF16)
    q_slab_gain = jnp.tile(jnp.concatenate(
        [q_gain, q_gain[MLA_NOPE:MLA_NOPE + ROPE_HALF], jnp.zeros((ROPE_HALF,), F32)]), MLA_HEADS)[None, :]
    qc, qs = _rope_tables(MLA_QK ** -0.5 * LOG2E)
    kc, ks = _rope_tables(1.0)
    k_own = jnp.pad(k_gain, (0, LANES - MLA_QK))[None, :]
    k_partner = jnp.concatenate([k_gain[:MLA_NOPE], k_gain[MLA_NOPE + ROPE_HALF:],
                                 k_gain[MLA_NOPE:MLA_NOPE + ROPE_HALF], jnp.zeros((LANES - MLA_QK,), F32)])[None, :]
    kc, ks = kc * k_own, ks * k_partner
    kpad = _head_padding(4)
    kaug = np.zeros((SEQ, 4 * LANES), np.float32)
    for s in range(4):
        kaug[:, s * LANES + AUG_ONE:s * LANES + AUG_ONE + 2] = 1.0
        if s < NSA_KV_HEADS:
            kaug[np.arange(SEQ), s * LANES + AUG_BLOCK + np.arange(SEQ) // SEL_BLOCK] = 1.0
    nqg = (jnp.tile(nq_gain, NSA_HEADS) * (NSA_HEAD_DIM ** -0.5 * LOG2E))[None, :]
    nkg = jnp.concatenate([nk_gain[1], nk_gain[1], nk_gain[2], nk_gain[2]])[None, :]
    vpick = np.zeros((4 * V_ROWS, 256), np.float32)
    for s in range(4):
        vpick[s * V_ROWS + np.arange(64), s * 64 + np.arange(64)] = 1.0

    def ones_col(heads):
        col = np.zeros((heads * V_ROWS, 1), np.float32)
        col[np.arange(heads) * V_ROWS + MLA_V] = 1.0
        return col

    rows = PROJ_ROWS
    n_pos = SEQ // rows
    full = lambda a: pl.BlockSpec(a.shape, lambda i: (0,) * a.ndim)
    tab = pl.BlockSpec((rows, LANES), lambda i: (i % n_pos, 0))
    consts = [ln1_g[None, :], wp, cq_g[None, :], ckv_g[None, :], wuq, wukk, wukvt, q_slab_gain]
    tabs = [qc, qs, kc, ks]
    mats = [nqg, nkg, jnp.asarray(kpad, BF16),
            jnp.asarray(vpick, BF16), jnp.asarray(ones_col(MLA_HEADS)), jnp.asarray(ones_col(4)),
            jnp.eye(LANES, dtype=BF16)]
    row_spec = lambda w: pl.BlockSpec((rows, w), lambda i: (i, 0))
    vt_spec = lambda w, tile: pl.BlockSpec((rows // tile, w, tile), lambda i: (i, 0, 0))
    sds = jax.ShapeDtypeStruct
    return pl.pallas_call(
        _proj_kernel,
        grid=(t // rows,),
        in_specs=([row_spec(D_MODEL)] + [full(a) for a in consts] + [tab] * 4 + [full(a) for a in mats[:3]]
                  + [pl.BlockSpec((rows, 4 * LANES), lambda i: (i % n_pos, 0))] + [full(a) for a in mats[3:]]),
        out_specs=[row_spec(1024), row_spec(1024), vt_spec(MLA_HEADS * V_ROWS, V_TILE), row_spec(512),
                   row_spec(512), vt_spec(4 * V_ROWS, V_TILE), row_spec(128), row_spec(128),
                   vt_spec(LANES, NSA_Q)],
        out_shape=[sds((t, 1024), BF16), sds((t, 1024), BF16), sds((t // V_TILE, MLA_HEADS * V_ROWS, V_TILE), BF16),
                   sds((t, 512), BF16), sds((t, 512), BF16), sds((t // V_TILE, 4 * V_ROWS, V_TILE), BF16),
                   sds((t, 128), F32), sds((t, 128), F32), sds((t // NSA_Q, LANES, NSA_Q), F32)],
        compiler_params=pltpu.CompilerParams(dimension_semantics=("parallel",),
                                             vmem_limit_bytes=VMEM_LIMIT),
        name="projections",
    )(x2d, *consts, *tabs, *mats[:3], jnp.asarray(kaug, BF16), *mats[3:])


def _compress_kernel(xk_ref, xv_ref, pe_ref, w1_ref, w2_ref, kg_ref, dupk_ref, pickv_ref, kc_ref, vct_ref):
    def mlp(x_ref, which):
        top = bot = None
        for t in range(CMP_STRIDE):
            xt = x_ref[pl.ds(t, N_CHUNK, stride=CMP_STRIDE), :]
            a = _dot((xt + pe_ref[which, t]).astype(BF16), w1_ref[which, t])
            b = _dot((xt + pe_ref[which, CMP_STRIDE + t]).astype(BF16), w1_ref[which, CMP_STRIDE + t])
            top = a if top is None else top + a
            bot = b if bot is None else bot + b
        hid = top + pltpu.roll(bot, N_CHUNK - 1, 0)
        act = (hid * (1.0 / (1.0 + jnp.exp(-hid)))).astype(BF16)
        return _dot(act, w2_ref[which])

    k = _pair_norm(mlp(xk_ref, 0), kg_ref).astype(BF16)
    v = mlp(xv_ref, 1).astype(BF16)
    for g in range(NSA_KV_HEADS):
        kc_ref[0, g] = _dot(k, dupk_ref[g]).astype(BF16)
        vct_ref[0, g] = _dot_nt(pickv_ref[g], v).astype(BF16)


def _compress(kck, kcv, cmp_pe, cmp_w1, cmp_w2, k_gain0):
    b = kck.shape[0] // SEQ
    g_ = NSA_KV_HEADS
    d = NSA_HEAD_DIM
    pe = jnp.tile(cmp_pe, (1, 1, g_))[:, :, None, :]
    w1 = cmp_w1.reshape(2, CMP_BLOCK, d, CMP_HIDDEN)
    z1 = jnp.zeros_like(w1)
    w1 = jnp.concatenate([jnp.concatenate([w1, z1], axis=3), jnp.concatenate([z1, w1], axis=3)], axis=2)
    z2 = jnp.zeros_like(cmp_w2)
    w2 = jnp.concatenate([jnp.concatenate([cmp_w2, z2], axis=2), jnp.concatenate([z2, cmp_w2], axis=2)], axis=1)
    dupk = np.zeros((g_, LANES, LANES), np.float32)
    pickv = np.zeros((g_, d, LANES), np.float32)
    for g in range(g_):
        dupk[g, g * d + np.arange(d), np.arange(d)] = 1.0
        dupk[g, g * d + np.arange(d), d + np.arange(d)] = 1.0
        pickv[g, np.arange(d), g * d + np.arange(d)] = 1.0
    full = lambda a: pl.BlockSpec(a.shape, lambda i: (0,) * a.ndim)
    consts = [pe, w1.astype(BF16), w2.astype(BF16), jnp.tile(k_gain0, g_)[None, :],
              jnp.asarray(dupk, BF16), jnp.asarray(pickv, BF16)]
    x_spec = pl.BlockSpec((SEQ, LANES), lambda i: (i, 0))
    return pl.pallas_call(
        _compress_kernel,
        grid=(b,),
        in_specs=[x_spec, x_spec] + [full(a) for a in consts],
        out_specs=[pl.BlockSpec((1, g_, N_CHUNK, LANES), lambda i: (i, 0, 0, 0)),
                   pl.BlockSpec((1, g_, d, N_CHUNK), lambda i: (i, 0, 0, 0))],
        out_shape=[jax.ShapeDtypeStruct((b, g_, N_CHUNK, LANES), BF16),
                   jax.ShapeDtypeStruct((b, g_, d, N_CHUNK), BF16)],
        compiler_params=pltpu.CompilerParams(dimension_semantics=("parallel",)),
        name="compression",
    )(kck, kcv, *consts)


def _flash_init(m_ref, acc_ref):
    m_ref[...] = jnp.full(m_ref.shape, NEG, F32)
    acc_ref[...] = jnp.zeros(acc_ref.shape, F32)


def _flash_step(s, v, m_ref, acc_ref, chunk=KEY_CHUNK):
    for c in range(s.shape[0] // chunk):
        sc = s[c * chunk:(c + 1) * chunk]
        m_old = m_ref[c]
        m_new = jnp.maximum(m_old, jnp.max(sc, axis=0, keepdims=True))
        p = jnp.exp2(sc - m_new).astype(BF16)
        acc_ref[c] = jnp.exp2(m_old - m_new) * acc_ref[c] + _dot(v[:, c * chunk:(c + 1) * chunk], p)
        m_ref[c] = m_new


def _flash_softmax(s, m_ref, p_ref, alpha_ref, chunk=KEY_CHUNK):
    for c in range(s.shape[0] // chunk):
        sc = s[c * chunk:(c + 1) * chunk]
        m_old = m_ref[c]
        m_new = jnp.maximum(m_old, jnp.max(sc, axis=0, keepdims=True))
        alpha_ref[c] = jnp.exp2(m_old - m_new)
        p_ref[c * chunk:(c + 1) * chunk, :] = jnp.exp2(sc - m_new).astype(BF16)
        m_ref[c] = m_new


def _flash_accumulate(v, p_ref, alpha_ref, acc_ref, chunk=KEY_CHUNK):
    for c in range(p_ref.shape[0] // chunk):
        rows = slice(c * chunk, (c + 1) * chunk)
        acc_ref[c] = alpha_ref[c] * acc_ref[c] + _dot(v[:, rows], p_ref[rows, :])


def _flash_finish(m_ref, acc_ref):
    n = m_ref.shape[0]
    m = m_ref[0]
    for c in range(1, n):
        m = jnp.maximum(m, m_ref[c])
    acc = sum(jnp.exp2(m_ref[c] - m) * acc_ref[c] for c in range(n))
    return acc[:MLA_V] / acc[MLA_V:MLA_V + 1]


def _pipelined_pairs(sched_ref, row, n, scores, softmax, accumulate, unroll=PAIR_UNROLL):
    assert n >= 2 and unroll % 2 == 0

    def pair(i):
        i = jnp.minimum(i, n - 1)
        return sched_ref[row, i], sched_ref[row + 1, i]

    def steps(first, count):
        for u in range(count):
            i, slot = first + u, u % 2
            if accumulate is None:
                scores(*pair(i + 1), 1 - slot)
                softmax(*pair(i), slot)
            else:
                scores(*pair(i + 2), slot)
                softmax(*pair(i + 1), 1 - slot)
                accumulate(*pair(i), slot)

    scores(*pair(0), 0)
    if accumulate is not None:
        scores(*pair(1), 1)
        softmax(*pair(0), 0)
    loops = n // unroll
    lax.fori_loop(0, loops, lambda j, carry: (steps(j * unroll, unroll), carry)[1], 0)
    steps(loops * unroll, n - loops * unroll)


def _mla_pairs():
    nq, nk = SEQ // MLA_Q, SEQ // MLA_K
    full = [(qt, kt) for kt in range(nk) for qt in range(nq) if (kt + 1) * MLA_K <= qt * MLA_Q]
    diag = [(qt, (qt * MLA_Q) // MLA_K) for qt in range(nq)]
    return full, diag


def _mla_kernel(sched_ref, q_ref, k_ref, vt_ref, eye_ref, o_ref, m_ref, acc_ref, s_ref):
    sub = MLA_K // V_TILE
    full, diag = _mla_pairs()
    rel = (lax.broadcasted_iota(jnp.int32, (MLA_K, MLA_Q), 1)
           - lax.broadcasted_iota(jnp.int32, (MLA_K, MLA_Q), 0))
    _flash_init(m_ref, acc_ref)

    def scores(qt, kt, slot, masked):
        qrows = pl.ds(pl.multiple_of(qt * MLA_Q, MLA_Q), MLA_Q)
        krows = pl.ds(pl.multiple_of(kt * MLA_K, MLA_K), MLA_K)
        for hh in range(2):
            cols = slice(hh * LANES, (hh + 1) * LANES)
            s = _dot_nt(k_ref[krows, cols], q_ref[qrows, cols])
            if masked:
                s = jnp.where(rel >= kt * MLA_K - qt * MLA_Q, s, NEG)
            s_ref[slot, hh] = s

    def update(qt, kt, slot):
        for hh in range(2):
            v = jnp.concatenate([vt_ref[kt * sub + i, hh * V_ROWS:(hh + 1) * V_ROWS, :] for i in range(sub)],
                                axis=1)
            _flash_step(s_ref[slot, hh], v, m_ref.at[hh, qt], acc_ref.at[hh, qt])

    _pipelined_pairs(sched_ref, 0, len(full), lambda qt, kt, slot: scores(qt, kt, slot, False), update, None)
    _pipelined_pairs(sched_ref, 2, len(diag), lambda qt, kt, slot: scores(qt, kt, slot, True), update, None)

    for qt in range(SEQ // MLA_Q):
        o_t = jnp.concatenate([_flash_finish(m_ref.at[hh, qt], acc_ref.at[hh, qt])
                               for hh in range(2)], axis=0)
        o_ref[qt * MLA_Q:(qt + 1) * MLA_Q, :] = _dot_nt(eye_ref[...], o_t.astype(BF16)).astype(BF16)


def _schedule(*pair_lists):
    n = max(len(p) for p in pair_lists)
    out = np.zeros((2 * len(pair_lists), n), np.int32)
    for i, pairs in enumerate(pair_lists):
        out[2 * i, :len(pairs)] = [a for a, _ in pairs]
        out[2 * i + 1, :len(pairs)] = [b for _, b in pairs]
    return jnp.asarray(out)


def _mla_attention(q, k, vt):
    b = q.shape[0] // SEQ
    nq = SEQ // MLA_Q
    nv = SEQ // V_TILE
    chunks = MLA_K // KEY_CHUNK
    return pl.pallas_call(
        _mla_kernel,
        grid=(b, MLA_HEADS // 2),
        in_specs=[pl.BlockSpec(memory_space=pltpu.SMEM),
                  pl.BlockSpec((SEQ, 2 * LANES), lambda i, hp: (i, hp)),
                  pl.BlockSpec((SEQ, 2 * LANES), lambda i, hp: (i, hp)),
                  pl.BlockSpec((nv, 2 * V_ROWS, V_TILE), lambda i, hp: (i, hp, 0)),
                  pl.BlockSpec((MLA_Q, MLA_Q), lambda i, hp: (0, 0))],
        out_specs=pl.BlockSpec((SEQ, LANES), lambda i, hp: (i, hp)),
        out_shape=jax.ShapeDtypeStruct((q.shape[0], MLA_HEADS * MLA_V), BF16),
        scratch_shapes=[pltpu.VMEM((2, nq, chunks, 1, MLA_Q), F32),
                        pltpu.VMEM((2, nq, chunks, V_ROWS, MLA_Q), F32),
                        pltpu.VMEM((2, 2, MLA_K, MLA_Q), F32)],
        compiler_params=pltpu.CompilerParams(
            dimension_semantics=("parallel", "parallel"), vmem_limit_bytes=VMEM_LIMIT),
        name="mla_attention",
    )(_schedule(*_mla_pairs()), q, k, vt, jnp.eye(MLA_Q, dtype=BF16))


def _select_kernel(q_ref, kc_ref, vct_ref, tabc_ref, ovt_ref, eye_ref, qpad_ref, qaug_ref, ocmp_ref):
    step = pl.program_id(1)
    t = SELECT_Q
    heads = range(NSA_GROUP)
    tiles = range(t // NSA_Q)

    def group(g):
        q_pad = _dot(q_ref[:, g * 2 * LANES:(g + 1) * 2 * LANES], qpad_ref[...]).astype(BF16)
        low = lax.broadcasted_iota(jnp.int32, (t, LANES), 1) < NSA_HEAD_DIM
        zero = jnp.zeros((t, LANES), BF16)
        pairs = [q_ref[:, (2 * g + j) * LANES:(2 * g + j + 1) * LANES] for j in range(NSA_GROUP // 2)]
        q4 = jnp.concatenate([jnp.where(low == (r % 2 == 0), pairs[r // 2], zero) for r in heads], axis=0)

        per_tile = NSA_Q // CMP_STRIDE
        rows = [pl.ds(pl.multiple_of(CMP_TAB_BASE - (step * len(tiles) + u) * per_tile, per_tile), N_CHUNK)
                for u in tiles]
        bias_c = jnp.concatenate([tabc_ref[g * NSA_GROUP + r, rows[u], :] for r in heads for u in tiles], axis=1)
        valid = bias_c > 0.5 * NEG
        sc = jnp.where(valid, _dot_nt(kc_ref[0, g], q4) + bias_c, NEG)
        mx = jnp.max(sc, axis=0, keepdims=True)
        p = jnp.where(valid, jnp.exp2(sc - mx), 0.0)
        pc = p / jnp.maximum(jnp.sum(p, axis=0, keepdims=True), 1e-30)
        o_cmp = _dot(vct_ref[0, g], pc.astype(BF16)).astype(BF16)
        for u in tiles:
            ocmp_ref[0, g, u] = jnp.concatenate(
                [o_cmp[:, r * t + u * NSA_Q:r * t + (u + 1) * NSA_Q] for r in heads], axis=0)

        psum = sum(pc[:, r * t:(r + 1) * t] for r in heads)
        imp = sum(_dot(ovt_ref[...], piece) for piece in _split(psum, 3))
        jj = lax.broadcasted_iota(jnp.int32, (N_SEL, t), 0)
        blk = (step * t + lax.broadcasted_iota(jnp.int32, (N_SEL, t), 1)) // SEL_BLOCK
        ok = jj <= blk
        forced = ok & ((jj == 0) | (jj == blk) | (jj == blk - 1))
        score = jnp.where(forced, FORCE_SCORE, jnp.where(ok, imp, -jnp.inf))
        rank = jnp.zeros((N_SEL, t), jnp.int32)
        for i in range(N_SEL):
            ci = score[i:i + 1, :]
            beats = (ci > score) | ((ci == score) & (jj > i))
            rank = rank + beats.astype(jnp.int32)
        negsel = jnp.where(ok & (rank < SEL_TOP_N), 0.0, NEG).astype(BF16)
        spread = jnp.concatenate([jnp.zeros((AUG_BLOCK, t), BF16), negsel,
                                  jnp.zeros((LANES - AUG_ONE, t), BF16)], axis=0)
        mask_lanes = _dot_nt(eye_ref[...], spread).astype(BF16)
        lane = lax.broadcasted_iota(jnp.int32, (t, LANES), 1)
        for r in heads:
            far = tabc_ref[g * NSA_GROUP + r, 0:1, :]
            far_hi = far.astype(BF16)
            far_lo = (far - far_hi.astype(F32)).astype(BF16)
            aug = jnp.where(lane == AUG_ONE, far_hi, jnp.where(lane == AUG_ONE + 1, far_lo, mask_lanes))
            q_aug = jnp.where(lane < AUG_BLOCK, q_pad[:, r * LANES:(r + 1) * LANES], aug)
            for u in tiles:
                qaug_ref[0, g, u, r * NSA_Q:(r + 1) * NSA_Q, :] = q_aug[u * NSA_Q:(u + 1) * NSA_Q]

    for g in range(NSA_KV_HEADS):
        group(g)


def _nsa_select(q, kc, vct, tables):
    b = q.shape[0] // SEQ
    t = SELECT_Q
    steps = SEQ // t
    nq = SEQ // NSA_Q
    per = t // NSA_Q
    g_ = NSA_KV_HEADS
    c0 = np.arange(N_CHUNK)[None, :] * CMP_STRIDE
    j0 = np.arange(N_SEL)[:, None] * SEL_BLOCK
    ovt = np.clip(np.minimum(c0 + CMP_BLOCK, j0 + SEL_BLOCK) - np.maximum(c0, j0), 0, None) / CMP_BLOCK
    ovt[:, N_CHUNK - 1:] = 0.0
    return pl.pallas_call(
        _select_kernel,
        grid=(b, steps),
        in_specs=[pl.BlockSpec((t, g_ * 2 * LANES), lambda i, j: (i * steps + j, 0)),
                  pl.BlockSpec((1, g_, N_CHUNK, LANES), lambda i, j: (i, 0, 0, 0)),
                  pl.BlockSpec((1, g_, NSA_HEAD_DIM, N_CHUNK), lambda i, j: (i, 0, 0, 0)),
                  pl.BlockSpec((NSA_HEADS, CMP_TAB_ROWS, LANES), lambda i, j: (0, WIN_ROWS // CMP_TAB_ROWS, 0)),
                  pl.BlockSpec((N_SEL, N_CHUNK), lambda i, j: (0, 0)),
                  pl.BlockSpec((t, t), lambda i, j: (0, 0)),
                  pl.BlockSpec((2 * LANES, NSA_GROUP * LANES), lambda i, j: (0, 0))],
        out_specs=[pl.BlockSpec((1, g_, per, NSA_GROUP * NSA_Q, LANES), lambda i, j: (i, 0, j, 0, 0)),
                   pl.BlockSpec((1, g_, per, NSA_GROUP * NSA_HEAD_DIM, NSA_Q), lambda i, j: (i, 0, j, 0, 0))],
        out_shape=[jax.ShapeDtypeStruct((b, g_, nq, NSA_GROUP * NSA_Q, LANES), BF16),
                   jax.ShapeDtypeStruct((b, g_, nq, NSA_GROUP * NSA_HEAD_DIM, NSA_Q), BF16)],
        compiler_params=pltpu.CompilerParams(
            dimension_semantics=("parallel", "parallel"), vmem_limit_bytes=VMEM_LIMIT),
        name="nsa_select",
    )(q, kc, vct, tables, jnp.asarray(ovt, BF16), jnp.eye(t, dtype=BF16),
      jnp.asarray(_head_padding(NSA_GROUP), BF16))


def _nsa_pairs():
    nq, nk, per = SEQ // NSA_Q, SEQ // NSA_K, NSA_K // NSA_Q
    sel = [(qt, kt) for kt in range(nk) for qt in range(nq) if kt * NSA_K <= qt * NSA_Q]
    far = [(qt, kt) for qt, kt in sel if qt - kt * per >= SEL_FAR_TILE]
    near = [(qt, kt) for qt, kt in sel if qt - kt * per < SEL_FAR_TILE]
    win = [(qt, kt) for kt in range(nk) for qt in range(nq)
           if max(qt * NSA_Q - (WINDOW - 1), 0) // NSA_K <= kt and kt * NSA_K <= qt * NSA_Q]
    return far, near, win


def _attend_kernel(sched_ref, qaug_ref, ksel_ref, kwin_ref, vsel_ref, vwin_ref, ocmp_ref, gate_ref,
                   tabw_ref, eye_ref, o_ref, m_ref, acc_ref, s_ref, p_ref, alpha_ref):
    t = NSA_Q
    heads = range(NSA_GROUP)
    far, near, win = _nsa_pairs()
    _flash_init(m_ref, acc_ref)

    def scores(qt, kt, slot, k_ref, add_bias):
        s = _dot_nt(k_ref[pl.ds(pl.multiple_of(kt * NSA_K, NSA_K), NSA_K), :], qaug_ref[0, 0, qt])
        if add_bias:
            off = qt - kt * (NSA_K // NSA_Q)
            bias_rows = pl.ds(pl.multiple_of(off * NSA_K, NSA_K), NSA_K)
            s = s + jnp.concatenate([tabw_ref[r, bias_rows, :] for r in heads], axis=1)
        s_ref[slot] = s

    def softmax(qt, kt, slot, branch):
        _flash_softmax(s_ref[slot], m_ref.at[branch, qt], p_ref.at[slot], alpha_ref.at[slot], chunk=NSA_K)

    def accumulate(qt, kt, slot, v_ref, branch):
        _flash_accumulate(v_ref[kt], p_ref.at[slot], alpha_ref.at[slot], acc_ref.at[branch, qt], chunk=NSA_K)

    for row, pairs, k_ref, v_ref, branch, add_bias in ((0, far, ksel_ref, vsel_ref, 0, False),
                                                       (2, near, ksel_ref, vsel_ref, 0, True),
                                                       (4, win, kwin_ref, vwin_ref, 1, True)):
        _pipelined_pairs(sched_ref, row, len(pairs),
                         lambda qt, kt, slot, k_ref=k_ref, add_bias=add_bias: scores(qt, kt, slot, k_ref, add_bias),
                         lambda qt, kt, slot, branch=branch: softmax(qt, kt, slot, branch),
                         lambda qt, kt, slot, v_ref=v_ref, branch=branch: accumulate(qt, kt, slot, v_ref, branch),
                         unroll=NSA_UNROLL)

    d = NSA_HEAD_DIM
    group = pl.program_id(1)
    for qt in range(SEQ // t):
        o_cmp = ocmp_ref[0, 0, qt].astype(F32)
        o_sel, o_win = [_flash_finish(m_ref.at[br, qt], acc_ref.at[br, qt]) for br in range(2)]
        mixed = []
        for r in heads:
            gate = [gate_ref[qt, pl.ds(br * NSA_HEADS + group * NSA_GROUP + r, 1), :]
                    for br in range(N_BRANCH)]
            mixed.append(gate[0] * o_cmp[r * d:(r + 1) * d] + gate[1] * o_sel[:, r * t:(r + 1) * t]
                         + gate[2] * o_win[:, r * t:(r + 1) * t])
        o_ref[qt * t:(qt + 1) * t, :] = _dot_nt(eye_ref[...], jnp.concatenate(mixed, axis=0).astype(BF16)).astype(BF16)


def _nsa_attend(qaug, kaug, vt, ocmp, gates, tables):
    b = kaug.shape[0] // SEQ
    t = NSA_Q
    nq = SEQ // t
    nv = SEQ // V_TILE
    g_ = NSA_KV_HEADS
    k_spec = lambda off: pl.BlockSpec((SEQ, LANES), lambda i, g: (i, off + g))
    v_spec = lambda off: pl.BlockSpec((nv, V_ROWS, V_TILE), lambda i, g: (i, off + g, 0))
    return pl.pallas_call(
        _attend_kernel,
        grid=(b, g_),
        in_specs=[pl.BlockSpec(memory_space=pltpu.SMEM),
                  pl.BlockSpec((1, 1, nq, NSA_GROUP * t, LANES), lambda i, g: (i, g, 0, 0, 0)),
                  k_spec(0), k_spec(2), v_spec(0), v_spec(2),
                  pl.BlockSpec((1, 1, nq, NSA_GROUP * NSA_HEAD_DIM, t), lambda i, g: (i, g, 0, 0, 0)),
                  pl.BlockSpec((nq, LANES, t), lambda i, g: (i, 0, 0)),
                  pl.BlockSpec((NSA_GROUP, WIN_ROWS, LANES), lambda i, g: (g, 0, 0)),
                  pl.BlockSpec((t, t), lambda i, g: (0, 0))],
        out_specs=pl.BlockSpec((SEQ, 2 * LANES), lambda i, g: (i, g)),
        out_shape=jax.ShapeDtypeStruct((kaug.shape[0], NSA_HEADS * NSA_HEAD_DIM), BF16),
        scratch_shapes=[pltpu.VMEM((2, nq, 1, 1, NSA_GROUP * t), F32),
                        pltpu.VMEM((2, nq, 1, V_ROWS, NSA_GROUP * t), F32),
                        pltpu.VMEM((2, NSA_K, NSA_GROUP * t), F32),
                        pltpu.VMEM((2, NSA_K, NSA_GROUP * t), BF16),
                        pltpu.VMEM((2, 1, 1, NSA_GROUP * t), F32)],
        compiler_params=pltpu.CompilerParams(
            dimension_semantics=("parallel", "parallel"), vmem_limit_bytes=VMEM_LIMIT),
        name="nsa_attend",
    )(_schedule(*_nsa_pairs()), qaug, kaug, kaug, vt, vt, ocmp, gates, tables, jnp.eye(t, dtype=BF16))


def _tail_kernel(x_ref, oa_ref, ob_ref, ga_ref, gb_ref, woa_ref, wob_ref, ln2_ref, wup_ref, wdn_ref,
                 out_ref, h2_ref):
    width = MLA_HEADS * MLA_V
    a = (_rms(oa_ref[...].astype(F32), width) * ga_ref[...]).astype(BF16)
    b = (_rms(ob_ref[...].astype(F32), width) * gb_ref[...]).astype(BF16)
    x1 = x_ref[...] + _dot(a, woa_ref[...]) + _dot(b, wob_ref[...])
    h2_ref[...] = (_rms(x1, D_MODEL) * ln2_ref[...]).astype(BF16)
    out_ref[...] = x1

    def body(f, carry):
        u = jnp.maximum(_dot(h2_ref[...], wup_ref[f]), 0.0)
        out_ref[...] += _dot((u * u).astype(BF16), wdn_ref[f])
        return carry

    lax.fori_loop(0, D_FF // FF_CHUNK, body, 0)


def _tail(x2d, o_a, o_b, gn_a, gn_b, w_o, ln2_g, w_up, w_down):
    t = x2d.shape[0]
    rows = TAIL_ROWS
    width = MLA_HEADS * MLA_V
    once = lambda a: pl.BlockSpec(a.shape, lambda i: (0,) * a.ndim)
    nf = D_FF // FF_CHUNK
    wup = w_up.astype(BF16).reshape(D_MODEL, nf, FF_CHUNK).transpose(1, 0, 2)
    wdn = w_down.astype(BF16).reshape(nf, FF_CHUNK, D_MODEL)
    consts = [gn_a[None, :], gn_b[None, :], w_o[:width].astype(BF16), w_o[width:].astype(BF16),
              ln2_g[None, :], wup, wdn]
    row_spec = lambda w: pl.BlockSpec((rows, w), lambda i: (i, 0))
    return pl.pallas_call(
        _tail_kernel,
        grid=(t // rows,),
        in_specs=[row_spec(D_MODEL), row_spec(width), row_spec(width)] + [once(a) for a in consts],
        out_specs=row_spec(D_MODEL),
        out_shape=jax.ShapeDtypeStruct((t, D_MODEL), F32),
        scratch_shapes=[pltpu.VMEM((rows, D_MODEL), BF16)],
        compiler_params=pltpu.CompilerParams(dimension_semantics=("parallel",),
                                             vmem_limit_bytes=VMEM_LIMIT),
        name="tail",
    )(x2d, o_a, o_b, *consts)


def kernel(x, ln1_g, w_in, mla_cq_norm_g, mla_ckv_norm_g, mla_w_uq, mla_w_ukv, mla_q_gain, mla_k_gain,
           nsa_q_gain, nsa_k_gain, nsa_cmp_pe, nsa_cmp_w1, nsa_cmp_w2, rel_bias, grp_norm_mla,
           grp_norm_nsa, w_o, ln2_g, w_up, w_down):
    b, s, d = x.shape
    assert (s, d) == (SEQ, D_MODEL) and ln1_g.shape[0] == 1
    x2d = x.reshape(b * s, d)
    tables = _bias_tables(rel_bias)
    q_mla, k_mla, vt_mla, q_nsa, kdup, vt_nsa, kck, kcv, gates = _projections(
        x2d, ln1_g[0], w_in[0], mla_cq_norm_g[0], mla_ckv_norm_g[0], mla_w_uq[0], mla_w_ukv[0],
        mla_q_gain[0], mla_k_gain[0], nsa_q_gain[0], nsa_k_gain[0])
    kc, vct = _compress(kck, kcv, nsa_cmp_pe[0], nsa_cmp_w1[0], nsa_cmp_w2[0], nsa_k_gain[0, 0])
    o_a = _mla_attention(q_mla, k_mla, vt_mla)
    q_aug, ocmp = _nsa_select(q_nsa, kc, vct, tables)
    o_b = _nsa_attend(q_aug, kdup, vt_nsa, ocmp, gates, tables)
    out = _tail(x2d, o_a, o_b, grp_norm_mla[0], grp_norm_nsa[0], w_o[0], ln2_g[0], w_up[0], w_down[0])
    return out.reshape(b, s, d)
```

```python
import math

import numpy as np
import jax
import jax.numpy as jnp
from jax import lax
from jax.experimental import pallas as pl
from jax.experimental.pallas import tpu as pltpu

F32 = jnp.float32
BF16 = jnp.bfloat16

D_MODEL = 1024
SEQ = 2048
MLA_HEADS = 8
MLA_NOPE = 64
MLA_ROPE = 32
MLA_V = 64
MLA_QK = MLA_NOPE + MLA_ROPE
MLA_Q_RANK = 384
MLA_KV_RANK = 256
ROPE_THETA = 10000.0
NSA_HEADS = 8
NSA_KV_HEADS = 2
NSA_GROUP = NSA_HEADS // NSA_KV_HEADS
NSA_HEAD_DIM = 64
N_BRANCH = 3
CMP_BLOCK = 32
CMP_STRIDE = 16
CMP_HIDDEN = 128
SEL_BLOCK = 64
SEL_TOP_N = 16
WINDOW = 512
FORCE_SCORE = 1e4
REL_BUCKETS = 32
REL_MAX_DIST = 128
D_FF = 4 * D_MODEL
EPS = 1e-6

LANES = 128
NEG = -1e30
N_CHUNK = SEQ // CMP_STRIDE
N_SEL = SEQ // SEL_BLOCK
NSA_Q = 128
NSA_K = 256
SELECT_Q = 512
N_BIAS_TILES = (WINDOW + NSA_K - NSA_Q) // NSA_Q + 1
SEL_FAR_TILE = (REL_MAX_DIST + NSA_K) // NSA_Q
WIN_ROWS = N_BIAS_TILES * NSA_K
AUG_BLOCK = NSA_HEAD_DIM
AUG_ONE = AUG_BLOCK + N_SEL
CMP_TAB_BASE = (SEQ // NSA_Q - 1) * (NSA_Q // CMP_STRIDE)
CMP_TAB_ROWS = 256
MLA_Q = 256
MLA_K = 512
V_TILE = 256
PROJ_ROWS = 256
PACK_ROWS = 256
TAIL_ROWS = 512
FF_CHUNK = 2048
KEY_CHUNK = 256
PAIR_UNROLL = 6
NSA_UNROLL = 12
V_ROWS = 80
LOG2E = math.log2(math.e)
VMEM_LIMIT = 56 * 1024 * 1024

assert PROJ_ROWS % V_TILE == 0 and MLA_K % V_TILE == 0 and NSA_K == V_TILE


def _dot(a, b):
    return jnp.dot(a, b, preferred_element_type=F32)


def _dot_nt(a, b):
    return lax.dot_general(a, b, (((1,), (1,)), ((), ())), preferred_element_type=F32)


def _split(a, terms):
    pieces = []
    rem = a
    for _ in range(terms):
        piece = rem.astype(BF16)
        pieces.append(piece)
        rem = rem - piece.astype(F32)
    return pieces


def _split_dot(a, b, terms=2):
    return sum(_dot(p, b) for p in _split(a, terms))


def _rms(x, width):
    return x * lax.rsqrt(jnp.sum(x * x, axis=-1, keepdims=True) * (1.0 / width) + EPS)


def _t5_bucket_np(dist):
    n = np.maximum(dist, 0)
    max_exact = REL_BUCKETS // 2
    large = max_exact + (np.log(np.maximum(n, 1).astype(np.float32) / max_exact)
                         / math.log(REL_MAX_DIST / max_exact)
                         * (REL_BUCKETS - max_exact)).astype(np.int32)
    large = np.minimum(large, REL_BUCKETS - 1)
    return np.where(n < max_exact, n, large).astype(np.int32)


def _bias_index_table():
    i = np.arange(NSA_Q)[None, :]
    parts = []
    j = np.arange(NSA_K)[:, None]
    for off in range(N_BIAS_TILES):
        d = off * NSA_Q + i - j
        parts.append(np.where((d >= 0) & (d < WINDOW), _t5_bucket_np(d), -1))
    r = np.arange(CMP_TAB_ROWS)[:, None]
    dist_c = (CMP_TAB_BASE - r) * CMP_STRIDE + i - (CMP_BLOCK - 1)
    parts.append(np.where(dist_c >= 0, _t5_bucket_np(dist_c), -1))
    return np.concatenate(parts, axis=0).astype(np.int32)


def _bias_table_kernel(rb_ref, idx_ref, out_ref):
    idx = idx_ref[...]
    hit = [idx == b for b in range(REL_BUCKETS)]
    is_window = pl.program_id(0) < WIN_ROWS // idx.shape[0]
    for h in range(NSA_HEADS):
        far = jnp.where(is_window, rb_ref[h, REL_BUCKETS - 1], 0.0)
        acc = jnp.full(idx.shape, NEG, F32)
        for b in range(REL_BUCKETS):
            acc = jnp.where(hit[b], (rb_ref[h, b] - far) * LOG2E, acc)
        out_ref[h] = acc


def _bias_tables(rel_bias):
    idx = jnp.asarray(_bias_index_table())
    rows = idx.shape[0]
    blk = CMP_TAB_ROWS
    return pl.pallas_call(
        _bias_table_kernel,
        grid=(rows // blk,),
        in_specs=[pl.BlockSpec(memory_space=pltpu.SMEM),
                  pl.BlockSpec((blk, LANES), lambda r: (r, 0))],
        out_specs=pl.BlockSpec((NSA_HEADS, blk, LANES), lambda r: (0, r, 0)),
        out_shape=jax.ShapeDtypeStruct((NSA_HEADS, rows, LANES), F32),
        name="bias_tables",
    )(rel_bias.T, idx)


ROPE_HALF = MLA_ROPE // 2
MISC_KR = 64


def _rope_tables(scale):
    inv = 1.0 / (ROPE_THETA ** (jnp.arange(ROPE_HALF, dtype=F32) / ROPE_HALF))
    ang = jnp.arange(SEQ, dtype=F32)[:, None] * inv[None, :]
    cos, sin = jnp.cos(ang), jnp.sin(ang)
    ones = jnp.ones((SEQ, MLA_NOPE), F32)
    z32 = jnp.zeros((SEQ, LANES - MLA_QK), F32)
    z64 = jnp.zeros((SEQ, MLA_NOPE), F32)
    c = jnp.concatenate([ones, cos, cos, z32], axis=1) * scale
    s = jnp.concatenate([z64, -sin, sin, z32], axis=1) * scale
    return c, s


def _pair_norm(x, gain_ref):
    low = lax.broadcasted_iota(jnp.int32, (x.shape[0], LANES), 1) < NSA_HEAD_DIM
    outs = []
    for j in range(x.shape[1] // LANES):
        cols = slice(j * LANES, (j + 1) * LANES)
        sl = x[:, cols]
        sq = sl * sl
        tot = jnp.sum(sq, axis=-1, keepdims=True)
        lo = jnp.sum(jnp.where(low, sq, 0.0), axis=-1, keepdims=True)
        rs = jnp.where(low, lax.rsqrt(lo * (1.0 / NSA_HEAD_DIM) + EPS),
                       lax.rsqrt((tot - lo) * (1.0 / NSA_HEAD_DIM) + EPS))
        outs.append(sl * rs * gain_ref[:, cols])
    return jnp.concatenate(outs, axis=1)


def _mla_query_heads(x, gain_ref, c, s):
    in_head = lax.broadcasted_iota(jnp.int32, (x.shape[0], LANES), 1) < MLA_QK
    outs = []
    for h in range(MLA_HEADS):
        cols = slice(h * LANES, (h + 1) * LANES)
        sl = x[:, cols]
        ss = jnp.sum(jnp.where(in_head, sl * sl, 0.0), axis=-1, keepdims=True)
        xn = sl * lax.rsqrt(ss * (1.0 / MLA_QK) + EPS) * gain_ref[:, cols]
        outs.append(xn * c + pltpu.roll(xn, LANES - ROPE_HALF, 1) * s)
    return jnp.concatenate(outs, axis=1)


def _mla_key_heads(x, k_rope, k_swap, c, s):
    in_head = lax.broadcasted_iota(jnp.int32, (x.shape[0], LANES), 1) < MLA_QK
    turned = k_swap * s
    outs = []
    for h in range(MLA_HEADS):
        sl = x[:, h * LANES:(h + 1) * LANES] + k_rope
        ss = jnp.sum(jnp.where(in_head, sl * sl, 0.0), axis=-1, keepdims=True)
        outs.append(lax.rsqrt(ss * (1.0 / MLA_QK) + EPS) * (sl * c + turned))
    return jnp.concatenate(outs, axis=1)


COL_KV = MLA_Q_RANK
COL_MISC = COL_KV + MLA_KV_RANK
COL_SWAP = COL_MISC + LANES
COL_NQ = COL_SWAP + LANES
COL_NK = COL_NQ + NSA_HEADS * NSA_HEAD_DIM
COL_NV = COL_NK + 4 * NSA_HEAD_DIM
COL_CMP = COL_NV + 4 * NSA_HEAD_DIM
COL_END = COL_CMP + 4 * NSA_HEAD_DIM


def _proj_kernel(x_ref, ln1_ref, wp_ref, cqg_ref, ckvg_ref, wuq_ref, wukk_ref, wukvt_ref,
                 qg_ref, qc_ref, qs_ref, kc_ref, ks_ref,
                 nq_gain_ref, nk_gain_ref, kpad_ref, kaug_ref, vpick_ref, ones_mla_ref, ones_nsa_ref, eye_ref,
                 qmla_ref, kmla_ref, vmla_ref, qnsa_ref, kdup_ref, vnsa_ref, kck_ref, kcv_ref, gate_ref):
    for sub in range(PROJ_ROWS // V_TILE):
        rows = slice(sub * V_TILE, (sub + 1) * V_TILE)
        x = x_ref[rows, :]
        h = (_rms(x, D_MODEL) * ln1_ref[...]).astype(BF16)
        part = lambda a, b: _dot(h, wp_ref[:, a:b])

        cq = (_rms(part(0, COL_KV), MLA_Q_RANK) * cqg_ref[...]).astype(BF16)
        qmla_ref[rows, :] = _mla_query_heads(_dot(cq, wuq_ref[...]), qg_ref,
                                             qc_ref[rows, :], qs_ref[rows, :]).astype(BF16)

        kv_misc = part(COL_KV, COL_NQ)
        ckv = (_rms(kv_misc[:, :MLA_KV_RANK], MLA_KV_RANK) * ckvg_ref[...]).astype(BF16)
        misc = kv_misc[:, COL_MISC - COL_KV:COL_SWAP - COL_KV]
        lane = lax.broadcasted_iota(jnp.int32, misc.shape, 1)
        k_rope = jnp.where((lane >= MISC_KR) & (lane < MISC_KR + MLA_ROPE), misc, 0.0)
        kmla_ref[rows, :] = _mla_key_heads(_dot(ckv, wukk_ref[...]), k_rope, kv_misc[:, COL_SWAP - COL_KV:],
                                           kc_ref[rows, :], ks_ref[rows, :]).astype(BF16)
        vmla_ref[sub] = (_dot_nt(wukvt_ref[...], ckv) + ones_mla_ref[...]).astype(BF16)

        qnsa_ref[rows, :] = _pair_norm(part(COL_NQ, COL_NK), nq_gain_ref).astype(BF16)

        kn = _pair_norm(part(COL_NK, COL_NV), nk_gain_ref)
        kdup_ref[rows, :] = (_dot(kn.astype(BF16), kpad_ref[...]) + kaug_ref[rows, :]).astype(BF16)
        vnsa_ref[sub] = (_dot_nt(vpick_ref[...], part(COL_NV, COL_CMP).astype(BF16)) + ones_nsa_ref[...]).astype(BF16)
        cmp_kv = part(COL_CMP, COL_END)
        kck_ref[rows, :] = cmp_kv[:, :LANES]
        kcv_ref[rows, :] = cmp_kv[:, LANES:]
        gate = 1.0 / (1.0 + jnp.exp(-misc))
        gate_t = sum(_dot_nt(eye_ref[...], piece) for piece in _split(gate, 2))
        for u in range(V_TILE // NSA_Q):
            gate_ref[sub * (V_TILE // NSA_Q) + u] = gate_t[:, u * NSA_Q:(u + 1) * NSA_Q]


def _head_padding(heads):
    d = NSA_HEAD_DIM
    m = np.zeros((heads * d, heads * LANES), np.float32)
    for hh in range(heads):
        m[hh * d + np.arange(d), hh * LANES + np.arange(d)] = 1.0
    return m


def _packed_segments():
    kv0 = 1184
    seg = lambda a, b: (a, b - a)
    kvb = lambda br, kv, g: (kv0 + ((br * 2 + kv) * 2 + g) * 64, 64)
    zeros = lambda n: (None, n)
    kr = COL_MISC
    gate_cols = N_BRANCH * NSA_HEADS
    pieces = [
        seg(0, COL_MISC),
        seg(1952, 1952 + gate_cols), zeros(MISC_KR - gate_cols), seg(kr, kr + MLA_ROPE),
        zeros(LANES - MISC_KR - MLA_ROPE),
        zeros(MISC_KR), seg(kr + ROPE_HALF, kr + MLA_ROPE), seg(kr, kr + ROPE_HALF), zeros(LANES - MISC_KR - MLA_ROPE),
        seg(672, 1184),
        kvb(1, 0, 0), kvb(1, 0, 1), kvb(2, 0, 0), kvb(2, 0, 1),
        kvb(1, 1, 0), kvb(1, 1, 1), kvb(2, 1, 0), kvb(2, 1, 1),
        kvb(0, 0, 0), kvb(0, 0, 1), kvb(0, 1, 0), kvb(0, 1, 1),
    ]
    assert sum(n for _, n in pieces) == COL_END
    return pieces


def _pack_kernel(w_ref, out_ref):
    dst = 0
    for src, n in _packed_segments():
        if src is None:
            out_ref[:, dst:dst + n] = jnp.zeros((out_ref.shape[0], n), BF16)
        else:
            out_ref[:, dst:dst + n] = w_ref[:, src:src + n].astype(BF16)
        dst += n


def _pack_input_projection(w_in):
    rows = PACK_ROWS
    return pl.pallas_call(
        _pack_kernel,
        grid=(D_MODEL // rows,),
        in_specs=[pl.BlockSpec((rows, w_in.shape[1]), lambda i: (i, 0))],
        out_specs=pl.BlockSpec((rows, COL_END), lambda i: (i, 0)),
        out_shape=jax.ShapeDtypeStruct((D_MODEL, COL_END), BF16),
        compiler_params=pltpu.CompilerParams(dimension_semantics=("parallel",),
                                             vmem_limit_bytes=VMEM_LIMIT),
        name="pack_input_projection",
    )(w_in)


def _projections(x2d, ln1_g, w_in, cq_g, ckv_g, w_uq, w_ukv, q_gain, k_gain, nq_gain, nk_gain):
    t = x2d.shape[0]
    wp = _pack_input_projection(w_in)
    wq3 = w_uq.reshape(MLA_Q_RANK, MLA_HEADS, MLA_QK).astype(BF16)
    wuq = jnp.concatenate([wq3, wq3[:, :, MLA_NOPE:MLA_NOPE + ROPE_HALF],
                           jnp.zeros((MLA_Q_RANK, MLA_HEADS, ROPE_HALF), BF16)], axis=2)
    wuq = wuq.reshape(MLA_Q_RANK, MLA_HEADS * LANES)
    wkv = w_ukv.reshape(MLA_KV_RANK, MLA_HEADS, MLA_NOPE + MLA_V)
    wukk = jnp.pad(wkv[:, :, :MLA_NOPE], ((0, 0), (0, 0), (0, LANES - MLA_NOPE)))
    wukk = wukk.reshape(MLA_KV_RANK, MLA_HEADS * LANES).astype(BF16)
    wukvt = jnp.pad(wkv[:, :, MLA_NOPE:], ((0, 0), (0, 0), (0, V_ROWS - MLA_V)))
    wukvt = wukvt.reshape(MLA_KV_RANK, MLA_HEADS * V_ROWS).T.astype(BF16)
    q_slab_gain = jnp.tile(jnp.concatenate(
        [q_gain, q_gain[MLA_NOPE:MLA_NOPE + ROPE_HALF], jnp.zeros((ROPE_HALF,), F32)]), MLA_HEADS)[None, :]
    qc, qs = _rope_tables(MLA_QK ** -0.5 * LOG2E)
    kc, ks = _rope_tables(1.0)
    k_own = jnp.pad(k_gain, (0, LANES - MLA_QK))[None, :]
    k_partner = jnp.concatenate([k_gain[:MLA_NOPE], k_gain[MLA_NOPE + ROPE_HALF:],
                                 k_gain[MLA_NOPE:MLA_NOPE + ROPE_HALF], jnp.zeros((LANES - MLA_QK,), F32)])[None, :]
    kc, ks = kc * k_own, ks * k_partner
    kpad = _head_padding(4)
    kaug = np.zeros((SEQ, 4 * LANES), np.float32)
    for s in range(4):
        kaug[:, s * LANES + AUG_ONE:s * LANES + AUG_ONE + 2] = 1.0
        if s < NSA_KV_HEADS:
            kaug[np.arange(SEQ), s * LANES + AUG_BLOCK + np.arange(SEQ) // SEL_BLOCK] = 1.0
    nqg = (jnp.tile(nq_gain, NSA_HEADS) * (NSA_HEAD_DIM ** -0.5 * LOG2E))[None, :]
    nkg = jnp.concatenate([nk_gain[1], nk_gain[1], nk_gain[2], nk_gain[2]])[None, :]
    vpick = np.zeros((4 * V_ROWS, 256), np.float32)
    for s in range(4):
        vpick[s * V_ROWS + np.arange(64), s * 64 + np.arange(64)] = 1.0

    def ones_col(heads):
        col = np.zeros((heads * V_ROWS, 1), np.float32)
        col[np.arange(heads) * V_ROWS + MLA_V] = 1.0
        return col

    rows = PROJ_ROWS
    n_pos = SEQ // rows
    full = lambda a: pl.BlockSpec(a.shape, lambda i: (0,) * a.ndim)
    tab = pl.BlockSpec((rows, LANES), lambda i: (i % n_pos, 0))
    consts = [ln1_g[None, :], wp, cq_g[None, :], ckv_g[None, :], wuq, wukk, wukvt, q_slab_gain]
    tabs = [qc, qs, kc, ks]
    mats = [nqg, nkg, jnp.asarray(kpad, BF16),
            jnp.asarray(vpick, BF16), jnp.asarray(ones_col(MLA_HEADS)), jnp.asarray(ones_col(4)),
            jnp.eye(LANES, dtype=BF16)]
    row_spec = lambda w: pl.BlockSpec((rows, w), lambda i: (i, 0))
    vt_spec = lambda w, tile: pl.BlockSpec((rows // tile, w, tile), lambda i: (i, 0, 0))
    sds = jax.ShapeDtypeStruct
    return pl.pallas_call(
        _proj_kernel,
        grid=(t // rows,),
        in_specs=([row_spec(D_MODEL)] + [full(a) for a in consts] + [tab] * 4 + [full(a) for a in mats[:3]]
                  + [pl.BlockSpec((rows, 4 * LANES), lambda i: (i % n_pos, 0))] + [full(a) for a in mats[3:]]),
        out_specs=[row_spec(1024), row_spec(1024), vt_spec(MLA_HEADS * V_ROWS, V_TILE), row_spec(512),
                   row_spec(512), vt_spec(4 * V_ROWS, V_TILE), row_spec(128), row_spec(128),
                   vt_spec(LANES, NSA_Q)],
        out_shape=[sds((t, 1024), BF16), sds((t, 1024), BF16), sds((t // V_TILE, MLA_HEADS * V_ROWS, V_TILE), BF16),
                   sds((t, 512), BF16), sds((t, 512), BF16), sds((t // V_TILE, 4 * V_ROWS, V_TILE), BF16),
                   sds((t, 128), F32), sds((t, 128), F32), sds((t // NSA_Q, LANES, NSA_Q), F32)],
        compiler_params=pltpu.CompilerParams(dimension_semantics=("parallel",),
                                             vmem_limit_bytes=VMEM_LIMIT),
        name="projections",
    )(x2d, *consts, *tabs, *mats[:3], jnp.asarray(kaug, BF16), *mats[3:])


def _compress_kernel(xk_ref, xv_ref, pe_ref, w1_ref, w2_ref, kg_ref, dupk_ref, pickv_ref, kc_ref, vct_ref):
    def mlp(x_ref, which):
        top = bot = None
        for t in range(CMP_STRIDE):
            xt = x_ref[pl.ds(t, N_CHUNK, stride=CMP_STRIDE), :]
            a = _dot((xt + pe_ref[which, t]).astype(BF16), w1_ref[which, t])
            b = _dot((xt + pe_ref[which, CMP_STRIDE + t]).astype(BF16), w1_ref[which, CMP_STRIDE + t])
            top = a if top is None else top + a
            bot = b if bot is None else bot + b
        hid = top + pltpu.roll(bot, N_CHUNK - 1, 0)
        act = (hid * (1.0 / (1.0 + jnp.exp(-hid)))).astype(BF16)
        return _dot(act, w2_ref[which])

    k = _pair_norm(mlp(xk_ref, 0), kg_ref).astype(BF16)
    v = mlp(xv_ref, 1).astype(BF16)
    for g in range(NSA_KV_HEADS):
        kc_ref[0, g] = _dot(k, dupk_ref[g]).astype(BF16)
        vct_ref[0, g] = _dot_nt(pickv_ref[g], v).astype(BF16)


def _compress(kck, kcv, cmp_pe, cmp_w1, cmp_w2, k_gain0):
    b = kck.shape[0] // SEQ
    g_ = NSA_KV_HEADS
    d = NSA_HEAD_DIM
    pe = jnp.tile(cmp_pe, (1, 1, g_))[:, :, None, :]
    w1 = cmp_w1.reshape(2, CMP_BLOCK, d, CMP_HIDDEN)
    z1 = jnp.zeros_like(w1)
    w1 = jnp.concatenate([jnp.concatenate([w1, z1], axis=3), jnp.concatenate([z1, w1], axis=3)], axis=2)
    z2 = jnp.zeros_like(cmp_w2)
    w2 = jnp.concatenate([jnp.concatenate([cmp_w2, z2], axis=2), jnp.concatenate([z2, cmp_w2], axis=2)], axis=1)
    dupk = np.zeros((g_, LANES, LANES), np.float32)
    pickv = np.zeros((g_, d, LANES), np.float32)
    for g in range(g_):
        dupk[g, g * d + np.arange(d), np.arange(d)] = 1.0
        dupk[g, g * d + np.arange(d), d + np.arange(d)] = 1.0
        pickv[g, np.arange(d), g * d + np.arange(d)] = 1.0
    full = lambda a: pl.BlockSpec(a.shape, lambda i: (0,) * a.ndim)
    consts = [pe, w1.astype(BF16), w2.astype(BF16), jnp.tile(k_gain0, g_)[None, :],
              jnp.asarray(dupk, BF16), jnp.asarray(pickv, BF16)]
    x_spec = pl.BlockSpec((SEQ, LANES), lambda i: (i, 0))
    return pl.pallas_call(
        _compress_kernel,
        grid=(b,),
        in_specs=[x_spec, x_spec] + [full(a) for a in consts],
        out_specs=[pl.BlockSpec((1, g_, N_CHUNK, LANES), lambda i: (i, 0, 0, 0)),
                   pl.BlockSpec((1, g_, d, N_CHUNK), lambda i: (i, 0, 0, 0))],
        out_shape=[jax.ShapeDtypeStruct((b, g_, N_CHUNK, LANES), BF16),
                   jax.ShapeDtypeStruct((b, g_, d, N_CHUNK), BF16)],
        compiler_params=pltpu.CompilerParams(dimension_semantics=("parallel",)),
        name="compression",
    )(kck, kcv, *consts)


def _flash_init(m_ref, acc_ref):
    m_ref[...] = jnp.full(m_ref.shape, NEG, F32)
    acc_ref[...] = jnp.zeros(acc_ref.shape, F32)


def _flash_step(s, v, m_ref, acc_ref, chunk=KEY_CHUNK):
    for c in range(s.shape[0] // chunk):
        sc = s[c * chunk:(c + 1) * chunk]
        m_old = m_ref[c]
        m_new = jnp.maximum(m_old, jnp.max(sc, axis=0, keepdims=True))
        p = jnp.exp2(sc - m_new).astype(BF16)
        acc_ref[c] = jnp.exp2(m_old - m_new) * acc_ref[c] + _dot(v[:, c * chunk:(c + 1) * chunk], p)
        m_ref[c] = m_new


def _flash_softmax(s, m_ref, p_ref, alpha_ref, chunk=KEY_CHUNK):
    for c in range(s.shape[0] // chunk):
        sc = s[c * chunk:(c + 1) * chunk]
        m_old = m_ref[c]
        m_new = jnp.maximum(m_old, jnp.max(sc, axis=0, keepdims=True))
        alpha_ref[c] = jnp.exp2(m_old - m_new)
        p_ref[c * chunk:(c + 1) * chunk, :] = jnp.exp2(sc - m_new).astype(BF16)
        m_ref[c] = m_new


def _flash_accumulate(v, p_ref, alpha_ref, acc_ref, chunk=KEY_CHUNK):
    for c in range(p_ref.shape[0] // chunk):
        rows = slice(c * chunk, (c + 1) * chunk)
        acc_ref[c] = alpha_ref[c] * acc_ref[c] + _dot(v[:, rows], p_ref[rows, :])


def _flash_finish(m_ref, acc_ref):
    n = m_ref.shape[0]
    m = m_ref[0]
    for c in range(1, n):
        m = jnp.maximum(m, m_ref[c])
    acc = sum(jnp.exp2(m_ref[c] - m) * acc_ref[c] for c in range(n))
    return acc[:MLA_V] / acc[MLA_V:MLA_V + 1]


def _pipelined_pairs(sched_ref, row, n, scores, softmax, accumulate, unroll=PAIR_UNROLL):
    assert n >= 2 and unroll % 2 == 0

    def pair(i):
        i = jnp.minimum(i, n - 1)
        return sched_ref[row, i], sched_ref[row + 1, i]

    def steps(first, count):
        for u in range(count):
            i, slot = first + u, u % 2
            if accumulate is None:
                scores(*pair(i + 1), 1 - slot)
                softmax(*pair(i), slot)
            else:
                scores(*pair(i + 2), slot)
                softmax(*pair(i + 1), 1 - slot)
                accumulate(*pair(i), slot)

    scores(*pair(0), 0)
    if accumulate is not None:
        scores(*pair(1), 1)
        softmax(*pair(0), 0)
    loops = n // unroll
    lax.fori_loop(0, loops, lambda j, carry: (steps(j * unroll, unroll), carry)[1], 0)
    steps(loops * unroll, n - loops * unroll)


def _mla_pairs():
    nq, nk = SEQ // MLA_Q, SEQ // MLA_K
    full = [(qt, kt) for kt in range(nk) for qt in range(nq) if (kt + 1) * MLA_K <= qt * MLA_Q]
    diag = [(qt, (qt * MLA_Q) // MLA_K) for qt in range(nq)]
    return full, diag


def _mla_kernel(sched_ref, q_ref, k_ref, vt_ref, eye_ref, o_ref, m_ref, acc_ref, s_ref):
    sub = MLA_K // V_TILE
    full, diag = _mla_pairs()
    rel = (lax.broadcasted_iota(jnp.int32, (MLA_K, MLA_Q), 1)
           - lax.broadcasted_iota(jnp.int32, (MLA_K, MLA_Q), 0))
    _flash_init(m_ref, acc_ref)

    def scores(qt, kt, slot, masked):
        qrows = pl.ds(pl.multiple_of(qt * MLA_Q, MLA_Q), MLA_Q)
        krows = pl.ds(pl.multiple_of(kt * MLA_K, MLA_K), MLA_K)
        for hh in range(2):
            cols = slice(hh * LANES, (hh + 1) * LANES)
            s = _dot_nt(k_ref[krows, cols], q_ref[qrows, cols])
            if masked:
                s = jnp.where(rel >= kt * MLA_K - qt * MLA_Q, s, NEG)
            s_ref[slot, hh] = s

    def update(qt, kt, slot):
        for hh in range(2):
            v = jnp.concatenate([vt_ref[kt * sub + i, hh * V_ROWS:(hh + 1) * V_ROWS, :] for i in range(sub)],
                                axis=1)
            _flash_step(s_ref[slot, hh], v, m_ref.at[hh, qt], acc_ref.at[hh, qt])

    _pipelined_pairs(sched_ref, 0, len(full), lambda qt, kt, slot: scores(qt, kt, slot, False), update, None)
    _pipelined_pairs(sched_ref, 2, len(diag), lambda qt, kt, slot: scores(qt, kt, slot, True), update, None)

    for qt in range(SEQ // MLA_Q):
        o_t = jnp.concatenate([_flash_finish(m_ref.at[hh, qt], acc_ref.at[hh, qt])
                               for hh in range(2)], axis=0)
        o_ref[qt * MLA_Q:(qt + 1) * MLA_Q, :] = _dot_nt(eye_ref[...], o_t.astype(BF16)).astype(BF16)


def _schedule(*pair_lists):
    n = max(len(p) for p in pair_lists)
    out = np.zeros((2 * len(pair_lists), n), np.int32)
    for i, pairs in enumerate(pair_lists):
        out[2 * i, :len(pairs)] = [a for a, _ in pairs]
        out[2 * i + 1, :len(pairs)] = [b for _, b in pairs]
    return jnp.asarray(out)


def _mla_attention(q, k, vt):
    b = q.shape[0] // SEQ
    nq = SEQ // MLA_Q
    nv = SEQ // V_TILE
    chunks = MLA_K // KEY_CHUNK
    return pl.pallas_call(
        _mla_kernel,
        grid=(b, MLA_HEADS // 2),
        in_specs=[pl.BlockSpec(memory_space=pltpu.SMEM),
                  pl.BlockSpec((SEQ, 2 * LANES), lambda i, hp: (i, hp)),
                  pl.BlockSpec((SEQ, 2 * LANES), lambda i, hp: (i, hp)),
                  pl.BlockSpec((nv, 2 * V_ROWS, V_TILE), lambda i, hp: (i, hp, 0)),
                  pl.BlockSpec((MLA_Q, MLA_Q), lambda i, hp: (0, 0))],
        out_specs=pl.BlockSpec((SEQ, LANES), lambda i, hp: (i, hp)),
        out_shape=jax.ShapeDtypeStruct((q.shape[0], MLA_HEADS * MLA_V), BF16),
        scratch_shapes=[pltpu.VMEM((2, nq, chunks, 1, MLA_Q), F32),
                        pltpu.VMEM((2, nq, chunks, V_ROWS, MLA_Q), F32),
                        pltpu.VMEM((2, 2, MLA_K, MLA_Q), F32)],
        compiler_params=pltpu.CompilerParams(
            dimension_semantics=("parallel", "parallel"), vmem_limit_bytes=VMEM_LIMIT),
        name="mla_attention",
    )(_schedule(*_mla_pairs()), q, k, vt, jnp.eye(MLA_Q, dtype=BF16))


def _select_kernel(q_ref, kc_ref, vct_ref, tabc_ref, ovt_ref, eye_ref, qpad_ref, qaug_ref, ocmp_ref):
    step = pl.program_id(1)
    t = SELECT_Q
    heads = range(NSA_GROUP)
    tiles = range(t // NSA_Q)

    def group(g):
        q_pad = _dot(q_ref[:, g * 2 * LANES:(g + 1) * 2 * LANES], qpad_ref[...]).astype(BF16)
        low = lax.broadcasted_iota(jnp.int32, (t, LANES), 1) < NSA_HEAD_DIM
        zero = jnp.zeros((t, LANES), BF16)
        pairs = [q_ref[:, (2 * g + j) * LANES:(2 * g + j + 1) * LANES] for j in range(NSA_GROUP // 2)]
        q4 = jnp.concatenate([jnp.where(low == (r % 2 == 0), pairs[r // 2], zero) for r in heads], axis=0)

        per_tile = NSA_Q // CMP_STRIDE
        rows = [pl.ds(pl.multiple_of(CMP_TAB_BASE - (step * len(tiles) + u) * per_tile, per_tile), N_CHUNK)
                for u in tiles]
        bias_c = jnp.concatenate([tabc_ref[g * NSA_GROUP + r, rows[u], :] for r in heads for u in tiles], axis=1)
        valid = bias_c > 0.5 * NEG
        sc = jnp.where(valid, _dot_nt(kc_ref[0, g], q4) + bias_c, NEG)
        mx = jnp.max(sc, axis=0, keepdims=True)
        p = jnp.where(valid, jnp.exp2(sc - mx), 0.0)
        pc = p / jnp.maximum(jnp.sum(p, axis=0, keepdims=True), 1e-30)
        o_cmp = _dot(vct_ref[0, g], pc.astype(BF16)).astype(BF16)
        for u in tiles:
            ocmp_ref[0, g, u] = jnp.concatenate(
                [o_cmp[:, r * t + u * NSA_Q:r * t + (u + 1) * NSA_Q] for r in heads], axis=0)

        psum = sum(pc[:, r * t:(r + 1) * t] for r in heads)
        imp = sum(_dot(ovt_ref[...], piece) for piece in _split(psum, 3))
        jj = lax.broadcasted_iota(jnp.int32, (N_SEL, t), 0)
        blk = (step * t + lax.broadcasted_iota(jnp.int32, (N_SEL, t), 1)) // SEL_BLOCK
        ok = jj <= blk
        forced = ok & ((jj == 0) | (jj == blk) | (jj == blk - 1))
        score = jnp.where(forced, FORCE_SCORE, jnp.where(ok, imp, -jnp.inf))
        rank = jnp.zeros((N_SEL, t), jnp.int32)
        for i in range(N_SEL):
            ci = score[i:i + 1, :]
            beats = (ci > score) | ((ci == score) & (jj > i))
            rank = rank + beats.astype(jnp.int32)
        negsel = jnp.where(ok & (rank < SEL_TOP_N), 0.0, NEG).astype(BF16)
        spread = jnp.concatenate([jnp.zeros((AUG_BLOCK, t), BF16), negsel,
                                  jnp.zeros((LANES - AUG_ONE, t), BF16)], axis=0)
        mask_lanes = _dot_nt(eye_ref[...], spread).astype(BF16)
        lane = lax.broadcasted_iota(jnp.int32, (t, LANES), 1)
        for r in heads:
            far = tabc_ref[g * NSA_GROUP + r, 0:1, :]
            far_hi = far.astype(BF16)
            far_lo = (far - far_hi.astype(F32)).astype(BF16)
            aug = jnp.where(lane == AUG_ONE, far_hi, jnp.where(lane == AUG_ONE + 1, far_lo, mask_lanes))
            q_aug = jnp.where(lane < AUG_BLOCK, q_pad[:, r * LANES:(r + 1) * LANES], aug)
            for u in tiles:
                qaug_ref[0, g, u, r * NSA_Q:(r + 1) * NSA_Q, :] = q_aug[u * NSA_Q:(u + 1) * NSA_Q]

    for g in range(NSA_KV_HEADS):
        group(g)


def _nsa_select(q, kc, vct, tables):
    b = q.shape[0] // SEQ
    t = SELECT_Q
    steps = SEQ // t
    nq = SEQ // NSA_Q
    per = t // NSA_Q
    g_ = NSA_KV_HEADS
    c0 = np.arange(N_CHUNK)[None, :] * CMP_STRIDE
    j0 = np.arange(N_SEL)[:, None] * SEL_BLOCK
    ovt = np.clip(np.minimum(c0 + CMP_BLOCK, j0 + SEL_BLOCK) - np.maximum(c0, j0), 0, None) / CMP_BLOCK
    ovt[:, N_CHUNK - 1:] = 0.0
    return pl.pallas_call(
        _select_kernel,
        grid=(b, steps),
        in_specs=[pl.BlockSpec((t, g_ * 2 * LANES), lambda i, j: (i * steps + j, 0)),
                  pl.BlockSpec((1, g_, N_CHUNK, LANES), lambda i, j: (i, 0, 0, 0)),
                  pl.BlockSpec((1, g_, NSA_HEAD_DIM, N_CHUNK), lambda i, j: (i, 0, 0, 0)),
                  pl.BlockSpec((NSA_HEADS, CMP_TAB_ROWS, LANES), lambda i, j: (0, WIN_ROWS // CMP_TAB_ROWS, 0)),
                  pl.BlockSpec((N_SEL, N_CHUNK), lambda i, j: (0, 0)),
                  pl.BlockSpec((t, t), lambda i, j: (0, 0)),
                  pl.BlockSpec((2 * LANES, NSA_GROUP * LANES), lambda i, j: (0, 0))],
        out_specs=[pl.BlockSpec((1, g_, per, NSA_GROUP * NSA_Q, LANES), lambda i, j: (i, 0, j, 0, 0)),
                   pl.BlockSpec((1, g_, per, NSA_GROUP * NSA_HEAD_DIM, NSA_Q), lambda i, j: (i, 0, j, 0, 0))],
        out_shape=[jax.ShapeDtypeStruct((b, g_, nq, NSA_GROUP * NSA_Q, LANES), BF16),
                   jax.ShapeDtypeStruct((b, g_, nq, NSA_GROUP * NSA_HEAD_DIM, NSA_Q), BF16)],
        compiler_params=pltpu.CompilerParams(
            dimension_semantics=("parallel", "parallel"), vmem_limit_bytes=VMEM_LIMIT),
        name="nsa_select",
    )(q, kc, vct, tables, jnp.asarray(ovt, BF16), jnp.eye(t, dtype=BF16),
      jnp.asarray(_head_padding(NSA_GROUP), BF16))


def _nsa_pairs():
    nq, nk, per = SEQ // NSA_Q, SEQ // NSA_K, NSA_K // NSA_Q
    sel = [(qt, kt) for kt in range(nk) for qt in range(nq) if kt * NSA_K <= qt * NSA_Q]
    far = [(qt, kt) for qt, kt in sel if qt - kt * per >= SEL_FAR_TILE]
    near = [(qt, kt) for qt, kt in sel if qt - kt * per < SEL_FAR_TILE]
    win = [(qt, kt) for kt in range(nk) for qt in range(nq)
           if max(qt * NSA_Q - (WINDOW - 1), 0) // NSA_K <= kt and kt * NSA_K <= qt * NSA_Q]
    return far, near, win


def _attend_kernel(sched_ref, qaug_ref, ksel_ref, kwin_ref, vsel_ref, vwin_ref, ocmp_ref, gate_ref,
                   tabw_ref, eye_ref, o_ref, m_ref, acc_ref, s_ref, p_ref, alpha_ref):
    t = NSA_Q
    heads = range(NSA_GROUP)
    far, near, win = _nsa_pairs()
    _flash_init(m_ref, acc_ref)

    def scores(qt, kt, slot, k_ref, add_bias):
        s = _dot_nt(k_ref[pl.ds(pl.multiple_of(kt * NSA_K, NSA_K), NSA_K), :], qaug_ref[0, 0, qt])
        if add_bias:
            off = qt - kt * (NSA_K // NSA_Q)
            bias_rows = pl.ds(pl.multiple_of(off * NSA_K, NSA_K), NSA_K)
            s = s + jnp.concatenate([tabw_ref[r, bias_rows, :] for r in heads], axis=1)
        s_ref[slot] = s

    def softmax(qt, kt, slot, branch):
        _flash_softmax(s_ref[slot], m_ref.at[branch, qt], p_ref.at[slot], alpha_ref.at[slot], chunk=NSA_K)

    def accumulate(qt, kt, slot, v_ref, branch):
        _flash_accumulate(v_ref[kt], p_ref.at[slot], alpha_ref.at[slot], acc_ref.at[branch, qt], chunk=NSA_K)

    for row, pairs, k_ref, v_ref, branch, add_bias in ((0, far, ksel_ref, vsel_ref, 0, False),
                                                       (2, near, ksel_ref, vsel_ref, 0, True),
                                                       (4, win, kwin_ref, vwin_ref, 1, True)):
        _pipelined_pairs(sched_ref, row, len(pairs),
                         lambda qt, kt, slot, k_ref=k_ref, add_bias=add_bias: scores(qt, kt, slot, k_ref, add_bias),
                         lambda qt, kt, slot, branch=branch: softmax(qt, kt, slot, branch),
                         lambda qt, kt, slot, v_ref=v_ref, branch=branch: accumulate(qt, kt, slot, v_ref, branch),
                         unroll=NSA_UNROLL)

    d = NSA_HEAD_DIM
    group = pl.program_id(1)
    for qt in range(SEQ // t):
        o_cmp = ocmp_ref[0, 0, qt].astype(F32)
        o_sel, o_win = [_flash_finish(m_ref.at[br, qt], acc_ref.at[br, qt]) for br in range(2)]
        mixed = []
        for r in heads:
            gate = [gate_ref[qt, pl.ds(br * NSA_HEADS + group * NSA_GROUP + r, 1), :]
                    for br in range(N_BRANCH)]
            mixed.append(gate[0] * o_cmp[r * d:(r + 1) * d] + gate[1] * o_sel[:, r * t:(r + 1) * t]
                         + gate[2] * o_win[:, r * t:(r + 1) * t])
        o_ref[qt * t:(qt + 1) * t, :] = _dot_nt(eye_ref[...], jnp.concatenate(mixed, axis=0).astype(BF16)).astype(BF16)


def _nsa_attend(qaug, kaug, vt, ocmp, gates, tables):
    b = kaug.shape[0] // SEQ
    t = NSA_Q
    nq = SEQ // t
    nv = SEQ // V_TILE
    g_ = NSA_KV_HEADS
    k_spec = lambda off: pl.BlockSpec((SEQ, LANES), lambda i, g: (i, off + g))
    v_spec = lambda off: pl.BlockSpec((nv, V_ROWS, V_TILE), lambda i, g: (i, off + g, 0))
    return pl.pallas_call(
        _attend_kernel,
        grid=(b, g_),
        in_specs=[pl.BlockSpec(memory_space=pltpu.SMEM),
                  pl.BlockSpec((1, 1, nq, NSA_GROUP * t, LANES), lambda i, g: (i, g, 0, 0, 0)),
                  k_spec(0), k_spec(2), v_spec(0), v_spec(2),
                  pl.BlockSpec((1, 1, nq, NSA_GROUP * NSA_HEAD_DIM, t), lambda i, g: (i, g, 0, 0, 0)),
                  pl.BlockSpec((nq, LANES, t), lambda i, g: (i, 0, 0)),
                  pl.BlockSpec((NSA_GROUP, WIN_ROWS, LANES), lambda i, g: (g, 0, 0)),
                  pl.BlockSpec((t, t), lambda i, g: (0, 0))],
        out_specs=pl.BlockSpec((SEQ, 2 * LANES), lambda i, g: (i, g)),
        out_shape=jax.ShapeDtypeStruct((kaug.shape[0], NSA_HEADS * NSA_HEAD_DIM), BF16),
        scratch_shapes=[pltpu.VMEM((2, nq, 1, 1, NSA_GROUP * t), F32),
                        pltpu.VMEM((2, nq, 1, V_ROWS, NSA_GROUP * t), F32),
                        pltpu.VMEM((2, NSA_K, NSA_GROUP * t), F32),
                        pltpu.VMEM((2, NSA_K, NSA_GROUP * t), BF16),
                        pltpu.VMEM((2, 1, 1, NSA_GROUP * t), F32)],
        compiler_params=pltpu.CompilerParams(
            dimension_semantics=("parallel", "parallel"), vmem_limit_bytes=VMEM_LIMIT),
        name="nsa_attend",
    )(_schedule(*_nsa_pairs()), qaug, kaug, kaug, vt, vt, ocmp, gates, tables, jnp.eye(t, dtype=BF16))


def _tail_kernel(x_ref, oa_ref, ob_ref, ga_ref, gb_ref, woa_ref, wob_ref, ln2_ref, wup_ref, wdn_ref,
                 out_ref, h2_ref):
    width = MLA_HEADS * MLA_V
    a = (_rms(oa_ref[...].astype(F32), width) * ga_ref[...]).astype(BF16)
    b = (_rms(ob_ref[...].astype(F32), width) * gb_ref[...]).astype(BF16)
    x1 = x_ref[...] + _dot(a, woa_ref[...]) + _dot(b, wob_ref[...])
    h2_ref[...] = (_rms(x1, D_MODEL) * ln2_ref[...]).astype(BF16)
    out_ref[...] = x1

    def body(f, carry):
        cols = pl.ds(pl.multiple_of(f * FF_CHUNK, FF_CHUNK), FF_CHUNK)
        u = jnp.maximum(_dot(h2_ref[...], wup_ref[:, cols]), 0.0)
        out_ref[...] += _dot((u * u).astype(BF16), wdn_ref[f])
        return carry

    lax.fori_loop(0, D_FF // FF_CHUNK, body, 0)


def _tail(x2d, o_a, o_b, gn_a, gn_b, w_o, ln2_g, w_up, w_down):
    t = x2d.shape[0]
    rows = TAIL_ROWS
    width = MLA_HEADS * MLA_V
    once = lambda a: pl.BlockSpec(a.shape, lambda i: (0,) * a.ndim)
    wdn = w_down.astype(BF16).reshape(D_FF // FF_CHUNK, FF_CHUNK, D_MODEL)
    consts = [gn_a[None, :], gn_b[None, :], w_o[:width].astype(BF16), w_o[width:].astype(BF16),
              ln2_g[None, :], w_up.astype(BF16), wdn]
    row_spec = lambda w: pl.BlockSpec((rows, w), lambda i: (i, 0))
    return pl.pallas_call(
        _tail_kernel,
        grid=(t // rows,),
        in_specs=[row_spec(D_MODEL), row_spec(width), row_spec(width)] + [once(a) for a in consts],
        out_specs=row_spec(D_MODEL),
        out_shape=jax.ShapeDtypeStruct((t, D_MODEL), F32),
        scratch_shapes=[pltpu.VMEM((rows, D_MODEL), BF16)],
        compiler_params=pltpu.CompilerParams(dimension_semantics=("parallel",),
                                             vmem_limit_bytes=VMEM_LIMIT),
        name="tail",
    )(x2d, o_a, o_b, *consts)


def kernel(x, ln1_g, w_in, mla_cq_norm_g, mla_ckv_norm_g, mla_w_uq, mla_w_ukv, mla_q_gain, mla_k_gain,
           nsa_q_gain, nsa_k_gain, nsa_cmp_pe, nsa_cmp_w1, nsa_cmp_w2, rel_bias, grp_norm_mla,
           grp_norm_nsa, w_o, ln2_g, w_up, w_down):
    b, s, d = x.shape
    assert (s, d) == (SEQ, D_MODEL) and ln1_g.shape[0] == 1
    x2d = x.reshape(b * s, d)
    tables = _bias_tables(rel_bias)
    q_mla, k_mla, vt_mla, q_nsa, kdup, vt_nsa, kck, kcv, gates = _projections(
        x2d, ln1_g[0], w_in[0], mla_cq_norm_g[0], mla_ckv_norm_g[0], mla_w_uq[0], mla_w_ukv[0],
        mla_q_gain[0], mla_k_gain[0], nsa_q_gain[0], nsa_k_gain[0])
    kc, vct = _compress(kck, kcv, nsa_cmp_pe[0], nsa_cmp_w1[0], nsa_cmp_w2[0], nsa_k_gain[0, 0])
    o_a = _mla_attention(q_mla, k_mla, vt_mla)
    q_aug, ocmp = _nsa_select(q_nsa, kc, vct, tables)
    o_b = _nsa_attend(q_aug, kdup, vt_nsa, ocmp, gates, tables)
    out = _tail(x2d, o_a, o_b, grp_norm_mla[0], grp_norm_nsa[0], w_o[0], ln2_g[0], w_up[0], w_down[0])
    return out.reshape(b, s, d)
```

```python
import math

import numpy as np
import jax
import jax.numpy as jnp
from jax import lax
from jax.experimental import pallas as pl
from jax.experimental.pallas import tpu as pltpu

F32 = jnp.float32
BF16 = jnp.bfloat16

D_MODEL = 1024
SEQ = 2048
MLA_HEADS = 8
MLA_NOPE = 64
MLA_ROPE = 32
MLA_V = 64
MLA_QK = MLA_NOPE + MLA_ROPE
MLA_Q_RANK = 384
MLA_KV_RANK = 256
ROPE_THETA = 10000.0
NSA_HEADS = 8
NSA_KV_HEADS = 2
NSA_GROUP = NSA_HEADS // NSA_KV_HEADS
NSA_HEAD_DIM = 64
N_BRANCH = 3
CMP_BLOCK = 32
CMP_STRIDE = 16
CMP_HIDDEN = 128
SEL_BLOCK = 64
SEL_TOP_N = 16
WINDOW = 512
FORCE_SCORE = 1e4
REL_BUCKETS = 32
REL_MAX_DIST = 128
D_FF = 4 * D_MODEL
EPS = 1e-6

LANES = 128
BF16_ROWS = 16
NEG = -1e30
N_CHUNK = SEQ // CMP_STRIDE
N_SEL = SEQ // SEL_BLOCK
NSA_Q = 128
NSA_K = 256
SELECT_Q = 512
N_BIAS_TILES = (WINDOW + NSA_K - NSA_Q) // NSA_Q + 1
SEL_FAR_TILE = (REL_MAX_DIST + NSA_K) // NSA_Q
WIN_ROWS = N_BIAS_TILES * NSA_K
AUG_BLOCK = NSA_HEAD_DIM
AUG_ONE = AUG_BLOCK + N_SEL
CMP_TAB_BASE = (SEQ // NSA_Q - 1) * (NSA_Q // CMP_STRIDE)
CMP_TAB_ROWS = 256
MLA_Q = 256
MLA_K = 512
V_TILE = 256
PROJ_ROWS = 256
PACK_ROWS = 256
TAIL_ROWS = 512
FF_CHUNK = 2048
KEY_CHUNK = 256
PAIR_UNROLL = 6
NSA_UNROLL = 12
V_ROWS = 80
LOG2E = math.log2(math.e)
VMEM_LIMIT = 56 * 1024 * 1024

assert PROJ_ROWS % V_TILE == 0 and MLA_K % V_TILE == 0 and NSA_K == V_TILE


def _dot(a, b):
    return jnp.dot(a, b, preferred_element_type=F32)


def _dot_nt(a, b):
    return lax.dot_general(a, b, (((1,), (1,)), ((), ())), preferred_element_type=F32)


def _split(a, terms):
    pieces = []
    rem = a
    for _ in range(terms):
        piece = rem.astype(BF16)
        pieces.append(piece)
        rem = rem - piece.astype(F32)
    return pieces


def _split_dot(a, b, terms=2):
    return sum(_dot(p, b) for p in _split(a, terms))


def _rms(x, width):
    return x * lax.rsqrt(jnp.sum(x * x, axis=-1, keepdims=True) * (1.0 / width) + EPS)


def _t5_bucket_np(dist):
    n = np.maximum(dist, 0)
    max_exact = REL_BUCKETS // 2
    large = max_exact + (np.log(np.maximum(n, 1).astype(np.float32) / max_exact)
                         / math.log(REL_MAX_DIST / max_exact)
                         * (REL_BUCKETS - max_exact)).astype(np.int32)
    large = np.minimum(large, REL_BUCKETS - 1)
    return np.where(n < max_exact, n, large).astype(np.int32)


def _bias_index_table():
    i = np.arange(NSA_Q)[None, :]
    parts = []
    j = np.arange(NSA_K)[:, None]
    for off in range(N_BIAS_TILES):
        d = off * NSA_Q + i - j
        parts.append(np.where((d >= 0) & (d < WINDOW), _t5_bucket_np(d), -1))
    r = np.arange(CMP_TAB_ROWS)[:, None]
    dist_c = (CMP_TAB_BASE - r) * CMP_STRIDE + i - (CMP_BLOCK - 1)
    parts.append(np.where(dist_c >= 0, _t5_bucket_np(dist_c), -1))
    return np.concatenate(parts, axis=0).astype(np.int32)


def _bias_table_kernel(rb_ref, idx_ref, out_ref):
    idx = idx_ref[...]
    hit = [idx == b for b in range(REL_BUCKETS)]
    is_window = pl.program_id(0) < WIN_ROWS // idx.shape[0]
    for h in range(NSA_HEADS):
        far = jnp.where(is_window, rb_ref[h, REL_BUCKETS - 1], 0.0)
        acc = jnp.full(idx.shape, NEG, F32)
        for b in range(REL_BUCKETS):
            acc = jnp.where(hit[b], (rb_ref[h, b] - far) * LOG2E, acc)
        out_ref[h] = acc


def _bias_tables(rel_bias):
    idx = jnp.asarray(_bias_index_table())
    rows = idx.shape[0]
    blk = CMP_TAB_ROWS
    return pl.pallas_call(
        _bias_table_kernel,
        grid=(rows // blk,),
        in_specs=[pl.BlockSpec(memory_space=pltpu.SMEM),
                  pl.BlockSpec((blk, LANES), lambda r: (r, 0))],
        out_specs=pl.BlockSpec((NSA_HEADS, blk, LANES), lambda r: (0, r, 0)),
        out_shape=jax.ShapeDtypeStruct((NSA_HEADS, rows, LANES), F32),
        name="bias_tables",
    )(rel_bias.T, idx)


ROPE_HALF = MLA_ROPE // 2
MISC_KR = 64


def _rope_tables(scale):
    inv = 1.0 / (ROPE_THETA ** (jnp.arange(ROPE_HALF, dtype=F32) / ROPE_HALF))
    ang = jnp.arange(SEQ, dtype=F32)[:, None] * inv[None, :]
    cos, sin = jnp.cos(ang), jnp.sin(ang)
    ones = jnp.ones((SEQ, MLA_NOPE), F32)
    z32 = jnp.zeros((SEQ, LANES - MLA_QK), F32)
    z64 = jnp.zeros((SEQ, MLA_NOPE), F32)
    c = jnp.concatenate([ones, cos, cos, z32], axis=1) * scale
    s = jnp.concatenate([z64, -sin, sin, z32], axis=1) * scale
    return c, s


def _pair_norm(x, gain_ref):
    low = lax.broadcasted_iota(jnp.int32, (x.shape[0], LANES), 1) < NSA_HEAD_DIM
    outs = []
    for j in range(x.shape[1] // LANES):
        cols = slice(j * LANES, (j + 1) * LANES)
        sl = x[:, cols]
        sq = sl * sl
        tot = jnp.sum(sq, axis=-1, keepdims=True)
        lo = jnp.sum(jnp.where(low, sq, 0.0), axis=-1, keepdims=True)
        rs = jnp.where(low, lax.rsqrt(lo * (1.0 / NSA_HEAD_DIM) + EPS),
                       lax.rsqrt((tot - lo) * (1.0 / NSA_HEAD_DIM) + EPS))
        outs.append(sl * rs * gain_ref[:, cols])
    return jnp.concatenate(outs, axis=1)


def _mla_query_heads(x, gain_ref, c, s):
    in_head = lax.broadcasted_iota(jnp.int32, (x.shape[0], LANES), 1) < MLA_QK
    outs = []
    for h in range(MLA_HEADS):
        cols = slice(h * LANES, (h + 1) * LANES)
        sl = x[:, cols]
        ss = jnp.sum(jnp.where(in_head, sl * sl, 0.0), axis=-1, keepdims=True)
        xn = sl * lax.rsqrt(ss * (1.0 / MLA_QK) + EPS) * gain_ref[:, cols]
        outs.append(xn * c + pltpu.roll(xn, LANES - ROPE_HALF, 1) * s)
    return jnp.concatenate(outs, axis=1)


def _mla_key_heads(x, k_rope, k_swap, c, s):
    in_head = lax.broadcasted_iota(jnp.int32, (x.shape[0], LANES), 1) < MLA_QK
    turned = k_swap * s
    outs = []
    for h in range(MLA_HEADS):
        sl = x[:, h * LANES:(h + 1) * LANES] + k_rope
        ss = jnp.sum(jnp.where(in_head, sl * sl, 0.0), axis=-1, keepdims=True)
        outs.append(lax.rsqrt(ss * (1.0 / MLA_QK) + EPS) * (sl * c + turned))
    return jnp.concatenate(outs, axis=1)


COL_KV = MLA_Q_RANK
COL_MISC = COL_KV + MLA_KV_RANK
COL_SWAP = COL_MISC + LANES
COL_NQ = COL_SWAP + LANES
COL_NK = COL_NQ + NSA_HEADS * NSA_HEAD_DIM
COL_NV = COL_NK + 4 * NSA_HEAD_DIM
COL_CMP = COL_NV + 4 * NSA_HEAD_DIM
COL_END = COL_CMP + 4 * NSA_HEAD_DIM


def _proj_kernel(x_ref, ln1_ref, wp_ref, cqg_ref, ckvg_ref, wuq_ref, wukk_ref, wukvt_ref,
                 qg_ref, qc_ref, qs_ref, kc_ref, ks_ref,
                 nq_gain_ref, nk_gain_ref, kpad_ref, kaug_ref, vpick_ref, ones_mla_ref, ones_nsa_ref, eye_ref,
                 qmla_ref, kmla_ref, vmla_ref, qnsa_ref, kdup_ref, vnsa_ref, kck_ref, kcv_ref, gate_ref):
    for sub in range(PROJ_ROWS // V_TILE):
        rows = slice(sub * V_TILE, (sub + 1) * V_TILE)
        x = x_ref[rows, :]
        h = (_rms(x, D_MODEL) * ln1_ref[...]).astype(BF16)
        part = lambda a, b: _dot(h, wp_ref[:, a:b])

        cq = (_rms(part(0, COL_KV), MLA_Q_RANK) * cqg_ref[...]).astype(BF16)
        qmla_ref[rows, :] = _mla_query_heads(_dot(cq, wuq_ref[...]), qg_ref,
                                             qc_ref[rows, :], qs_ref[rows, :]).astype(BF16)

        kv_misc = part(COL_KV, COL_NQ)
        ckv = (_rms(kv_misc[:, :MLA_KV_RANK], MLA_KV_RANK) * ckvg_ref[...]).astype(BF16)
        misc = kv_misc[:, COL_MISC - COL_KV:COL_SWAP - COL_KV]
        lane = lax.broadcasted_iota(jnp.int32, misc.shape, 1)
        k_rope = jnp.where((lane >= MISC_KR) & (lane < MISC_KR + MLA_ROPE), misc, 0.0)
        kmla_ref[rows, :] = _mla_key_heads(_dot(ckv, wukk_ref[...]), k_rope, kv_misc[:, COL_SWAP - COL_KV:],
                                           kc_ref[rows, :], ks_ref[rows, :]).astype(BF16)
        vmla_ref[sub] = (_dot_nt(wukvt_ref[...], ckv) + ones_mla_ref[...]).astype(BF16)

        qnsa_ref[rows, :] = _pair_norm(part(COL_NQ, COL_NK), nq_gain_ref).astype(BF16)

        kn = _pair_norm(part(COL_NK, COL_NV), nk_gain_ref)
        kdup_ref[rows, :] = (_dot(kn.astype(BF16), kpad_ref[...]) + kaug_ref[rows, :]).astype(BF16)
        vnsa_ref[sub] = (_dot_nt(vpick_ref[...], part(COL_NV, COL_CMP).astype(BF16)) + ones_nsa_ref[...]).astype(BF16)
        cmp_kv = part(COL_CMP, COL_END)
        kck_ref[rows, :] = cmp_kv[:, :LANES]
        kcv_ref[rows, :] = cmp_kv[:, LANES:]
        gate = 1.0 / (1.0 + jnp.exp(-misc))
        gate_t = sum(_dot_nt(eye_ref[...], piece) for piece in _split(gate, 2))
        for u in range(V_TILE // NSA_Q):
            gate_ref[sub * (V_TILE // NSA_Q) + u] = gate_t[:, u * NSA_Q:(u + 1) * NSA_Q]


def _head_padding(heads):
    d = NSA_HEAD_DIM
    m = np.zeros((heads * d, heads * LANES), np.float32)
    for hh in range(heads):
        m[hh * d + np.arange(d), hh * LANES + np.arange(d)] = 1.0
    return m


def _packed_segments():
    kv0 = 1184
    seg = lambda a, b: (a, b - a)
    kvb = lambda br, kv, g: (kv0 + ((br * 2 + kv) * 2 + g) * 64, 64)
    zeros = lambda n: (None, n)
    kr = COL_MISC
    gate_cols = N_BRANCH * NSA_HEADS
    pieces = [
        seg(0, COL_MISC),
        seg(1952, 1952 + gate_cols), zeros(MISC_KR - gate_cols), seg(kr, kr + MLA_ROPE),
        zeros(LANES - MISC_KR - MLA_ROPE),
        zeros(MISC_KR), seg(kr + ROPE_HALF, kr + MLA_ROPE), seg(kr, kr + ROPE_HALF), zeros(LANES - MISC_KR - MLA_ROPE),
        seg(672, 1184),
        kvb(1, 0, 0), kvb(1, 0, 1), kvb(2, 0, 0), kvb(2, 0, 1),
        kvb(1, 1, 0), kvb(1, 1, 1), kvb(2, 1, 0), kvb(2, 1, 1),
        kvb(0, 0, 0), kvb(0, 0, 1), kvb(0, 1, 0), kvb(0, 1, 1),
    ]
    assert sum(n for _, n in pieces) == COL_END
    return pieces


def _pack_kernel(w_ref, out_ref):
    dst = 0
    for src, n in _packed_segments():
        if src is None:
            out_ref[:, dst:dst + n] = jnp.zeros((out_ref.shape[0], n), BF16)
        else:
            out_ref[:, dst:dst + n] = w_ref[:, src:src + n].astype(BF16)
        dst += n


def _pack_input_projection(w_in):
    rows = PACK_ROWS
    return pl.pallas_call(
        _pack_kernel,
        grid=(D_MODEL // rows,),
        in_specs=[pl.BlockSpec((rows, w_in.shape[1]), lambda i: (i, 0))],
        out_specs=pl.BlockSpec((rows, COL_END), lambda i: (i, 0)),
        out_shape=jax.ShapeDtypeStruct((D_MODEL, COL_END), BF16),
        compiler_params=pltpu.CompilerParams(dimension_semantics=("parallel",),
                                             vmem_limit_bytes=VMEM_LIMIT),
        name="pack_input_projection",
    )(w_in)


def _projections(x2d, ln1_g, w_in, cq_g, ckv_g, w_uq, w_ukv, q_gain, k_gain, nq_gain, nk_gain):
    t = x2d.shape[0]
    wp = _pack_input_projection(w_in)
    wq3 = w_uq.reshape(MLA_Q_RANK, MLA_HEADS, MLA_QK).astype(BF16)
    wuq = jnp.concatenate([wq3, wq3[:, :, MLA_NOPE:MLA_NOPE + ROPE_HALF],
                           jnp.zeros((MLA_Q_RANK, MLA_HEADS, ROPE_HALF), BF16)], axis=2)
    wuq = wuq.reshape(MLA_Q_RANK, MLA_HEADS * LANES)
    wkv = w_ukv.reshape(MLA_KV_RANK, MLA_HEADS, MLA_NOPE + MLA_V)
    wukk = jnp.pad(wkv[:, :, :MLA_NOPE], ((0, 0), (0, 0), (0, LANES - MLA_NOPE)))
    wukk = wukk.reshape(MLA_KV_RANK, MLA_HEADS * LANES).astype(BF16)
    wukvt = jnp.pad(wkv[:, :, MLA_NOPE:], ((0, 0), (0, 0), (0, V_ROWS - MLA_V)))
    wukvt = wukvt.reshape(MLA_KV_RANK, MLA_HEADS * V_ROWS).T.astype(BF16)
    q_slab_gain = jnp.tile(jnp.concatenate(
        [q_gain, q_gain[MLA_NOPE:MLA_NOPE + ROPE_HALF], jnp.zeros((ROPE_HALF,), F32)]), MLA_HEADS)[None, :]
    qc, qs = _rope_tables(MLA_QK ** -0.5 * LOG2E)
    kc, ks = _rope_tables(1.0)
    k_own = jnp.pad(k_gain, (0, LANES - MLA_QK))[None, :]
    k_partner = jnp.concatenate([k_gain[:MLA_NOPE], k_gain[MLA_NOPE + ROPE_HALF:],
                                 k_gain[MLA_NOPE:MLA_NOPE + ROPE_HALF], jnp.zeros((LANES - MLA_QK,), F32)])[None, :]
    kc, ks = kc * k_own, ks * k_partner
    kpad = _head_padding(4)
    kaug = np.zeros((SEQ, 4 * LANES), np.float32)
    for s in range(4):
        kaug[:, s * LANES + AUG_ONE:s * LANES + AUG_ONE + 2] = 1.0
        if s < NSA_KV_HEADS:
            kaug[np.arange(SEQ), s * LANES + AUG_BLOCK + np.arange(SEQ) // SEL_BLOCK] = 1.0
    nqg = (jnp.tile(nq_gain, NSA_HEADS) * (NSA_HEAD_DIM ** -0.5 * LOG2E))[None, :]
    nkg = jnp.concatenate([nk_gain[1], nk_gain[1], nk_gain[2], nk_gain[2]])[None, :]
    vpick = np.zeros((4 * V_ROWS, 256), np.float32)
    for s in range(4):
        vpick[s * V_ROWS + np.arange(64), s * 64 + np.arange(64)] = 1.0

    def ones_col(heads):
        col = np.zeros((heads * V_ROWS, 1), np.float32)
        col[np.arange(heads) * V_ROWS + MLA_V] = 1.0
        return col

    rows = PROJ_ROWS
    n_pos = SEQ // rows
    full = lambda a: pl.BlockSpec(a.shape, lambda i: (0,) * a.ndim)
    tab = pl.BlockSpec((rows, LANES), lambda i: (i % n_pos, 0))
    consts = [ln1_g[None, :], wp, cq_g[None, :], ckv_g[None, :], wuq, wukk, wukvt, q_slab_gain]
    tabs = [qc, qs, kc, ks]
    mats = [nqg, nkg, jnp.asarray(kpad, BF16),
            jnp.asarray(vpick, BF16), jnp.asarray(ones_col(MLA_HEADS)), jnp.asarray(ones_col(4)),
            jnp.eye(LANES, dtype=BF16)]
    row_spec = lambda w: pl.BlockSpec((rows, w), lambda i: (i, 0))
    vt_spec = lambda w, tile: pl.BlockSpec((rows // tile, w, tile), lambda i: (i, 0, 0))
    sds = jax.ShapeDtypeStruct
    return pl.pallas_call(
        _proj_kernel,
        grid=(t // rows,),
        in_specs=([row_spec(D_MODEL)] + [full(a) for a in consts] + [tab] * 4 + [full(a) for a in mats[:3]]
                  + [pl.BlockSpec((rows, 4 * LANES), lambda i: (i % n_pos, 0))] + [full(a) for a in mats[3:]]),
        out_specs=[row_spec(1024), row_spec(1024), vt_spec(MLA_HEADS * V_ROWS, V_TILE), row_spec(512),
                   row_spec(512), vt_spec(4 * V_ROWS, V_TILE), row_spec(128), row_spec(128),
                   vt_spec(LANES, NSA_Q)],
        out_shape=[sds((t, 1024), BF16), sds((t, 1024), BF16), sds((t // V_TILE, MLA_HEADS * V_ROWS, V_TILE), BF16),
                   sds((t, 512), BF16), sds((t, 512), BF16), sds((t // V_TILE, 4 * V_ROWS, V_TILE), BF16),
                   sds((t, 128), F32), sds((t, 128), F32), sds((t // NSA_Q, LANES, NSA_Q), F32)],
        compiler_params=pltpu.CompilerParams(dimension_semantics=("parallel",),
                                             vmem_limit_bytes=VMEM_LIMIT),
        name="projections",
    )(x2d, *consts, *tabs, *mats[:3], jnp.asarray(kaug, BF16), *mats[3:])


def _compress_kernel(xk_ref, xv_ref, pe_ref, w1_ref, w2_ref, kg_ref, dupk_ref, pickv_ref, kc_ref, vct_ref):
    def mlp(x_ref, which):
        top = bot = None
        for t in range(CMP_STRIDE):
            xt = x_ref[pl.ds(t, N_CHUNK, stride=CMP_STRIDE), :]
            a = _dot((xt + pe_ref[which, t]).astype(BF16), w1_ref[which, t])
            b = _dot((xt + pe_ref[which, CMP_STRIDE + t]).astype(BF16), w1_ref[which, CMP_STRIDE + t])
            top = a if top is None else top + a
            bot = b if bot is None else bot + b
        hid = top + pltpu.roll(bot, N_CHUNK - 1, 0)
        act = (hid * (1.0 / (1.0 + jnp.exp(-hid)))).astype(BF16)
        return _dot(act, w2_ref[which])

    k = _pair_norm(mlp(xk_ref, 0), kg_ref).astype(BF16)
    v = mlp(xv_ref, 1).astype(BF16)
    for g in range(NSA_KV_HEADS):
        kc_ref[0, g] = _dot(k, dupk_ref[g]).astype(BF16)
        vct_ref[0, g] = _dot_nt(pickv_ref[g], v).astype(BF16)


def _compress(kck, kcv, cmp_pe, cmp_w1, cmp_w2, k_gain0):
    b = kck.shape[0] // SEQ
    g_ = NSA_KV_HEADS
    d = NSA_HEAD_DIM
    pe = jnp.tile(cmp_pe, (1, 1, g_))[:, :, None, :]
    w1 = cmp_w1.reshape(2, CMP_BLOCK, d, CMP_HIDDEN)
    z1 = jnp.zeros_like(w1)
    w1 = jnp.concatenate([jnp.concatenate([w1, z1], axis=3), jnp.concatenate([z1, w1], axis=3)], axis=2)
    z2 = jnp.zeros_like(cmp_w2)
    w2 = jnp.concatenate([jnp.concatenate([cmp_w2, z2], axis=2), jnp.concatenate([z2, cmp_w2], axis=2)], axis=1)
    dupk = np.zeros((g_, LANES, LANES), np.float32)
    pickv = np.zeros((g_, d, LANES), np.float32)
    for g in range(g_):
        dupk[g, g * d + np.arange(d), np.arange(d)] = 1.0
        dupk[g, g * d + np.arange(d), d + np.arange(d)] = 1.0
        pickv[g, np.arange(d), g * d + np.arange(d)] = 1.0
    full = lambda a: pl.BlockSpec(a.shape, lambda i: (0,) * a.ndim)
    consts = [pe, w1.astype(BF16), w2.astype(BF16), jnp.tile(k_gain0, g_)[None, :],
              jnp.asarray(dupk, BF16), jnp.asarray(pickv, BF16)]
    x_spec = pl.BlockSpec((SEQ, LANES), lambda i: (i, 0))
    return pl.pallas_call(
        _compress_kernel,
        grid=(b,),
        in_specs=[x_spec, x_spec] + [full(a) for a in consts],
        out_specs=[pl.BlockSpec((1, g_, N_CHUNK, LANES), lambda i: (i, 0, 0, 0)),
                   pl.BlockSpec((1, g_, d, N_CHUNK), lambda i: (i, 0, 0, 0))],
        out_shape=[jax.ShapeDtypeStruct((b, g_, N_CHUNK, LANES), BF16),
                   jax.ShapeDtypeStruct((b, g_, d, N_CHUNK), BF16)],
        compiler_params=pltpu.CompilerParams(dimension_semantics=("parallel",)),
        name="compression",
    )(kck, kcv, *consts)


def _flash_init(m_ref, acc_ref):
    m_ref[...] = jnp.full(m_ref.shape, NEG, F32)
    acc_ref[...] = jnp.zeros(acc_ref.shape, F32)


def _flash_step(s, v, m_ref, acc_ref, chunk=KEY_CHUNK):
    for c in range(s.shape[0] // chunk):
        sc = s[c * chunk:(c + 1) * chunk]
        m_old = m_ref[c]
        m_new = jnp.maximum(m_old, jnp.max(sc, axis=0, keepdims=True))
        p = jnp.exp2(sc - m_new).astype(BF16)
        acc_ref[c] = jnp.exp2(m_old - m_new) * acc_ref[c] + _dot(v[:, c * chunk:(c + 1) * chunk], p)
        m_ref[c] = m_new


def _flash_softmax(s, m_ref, p_ref, alpha_ref, chunk=KEY_CHUNK):
    for c in range(s.shape[0] // chunk):
        sc = s[c * chunk:(c + 1) * chunk]
        m_old = m_ref[c]
        m_new = jnp.maximum(m_old, jnp.max(sc, axis=0, keepdims=True))
        alpha_ref[c] = jnp.exp2(m_old - m_new)
        p_ref[c * chunk:(c + 1) * chunk, :] = jnp.exp2(sc - m_new).astype(BF16)
        m_ref[c] = m_new


def _flash_accumulate(v, p_ref, alpha_ref, acc_ref, chunk=KEY_CHUNK):
    for c in range(p_ref.shape[0] // chunk):
        rows = slice(c * chunk, (c + 1) * chunk)
        acc_ref[c] = alpha_ref[c] * acc_ref[c] + _dot(v[:, rows], p_ref[rows, :])


def _flash_finish(m_ref, acc_ref):
    n = m_ref.shape[0]
    m = m_ref[0]
    for c in range(1, n):
        m = jnp.maximum(m, m_ref[c])
    acc = sum(jnp.exp2(m_ref[c] - m) * acc_ref[c] for c in range(n))
    return acc[:MLA_V] / acc[MLA_V:MLA_V + 1]


def _pipelined_pairs(sched_ref, row, n, scores, softmax, accumulate, unroll=PAIR_UNROLL):
    assert n >= 2 and unroll % 2 == 0

    def pair(i):
        i = jnp.minimum(i, n - 1)
        return sched_ref[row, i], sched_ref[row + 1, i]

    def steps(first, count):
        for u in range(count):
            i, slot = first + u, u % 2
            if accumulate is None:
                scores(*pair(i + 1), 1 - slot)
                softmax(*pair(i), slot)
            else:
                scores(*pair(i + 2), slot)
                softmax(*pair(i + 1), 1 - slot)
                accumulate(*pair(i), slot)

    scores(*pair(0), 0)
    if accumulate is not None:
        scores(*pair(1), 1)
        softmax(*pair(0), 0)
    loops = n // unroll
    lax.fori_loop(0, loops, lambda j, carry: (steps(j * unroll, unroll), carry)[1], 0)
    steps(loops * unroll, n - loops * unroll)


def _mla_pairs():
    nq, nk = SEQ // MLA_Q, SEQ // MLA_K
    full = [(qt, kt) for kt in range(nk) for qt in range(nq) if (kt + 1) * MLA_K <= qt * MLA_Q]
    diag = [(qt, (qt * MLA_Q) // MLA_K) for qt in range(nq)]
    return full, diag


def _mla_kernel(sched_ref, q_ref, k_ref, vt_ref, eye_ref, wo_ref, wup_ref, wdn_ref,
                o_ref, wo16_ref, wup16_ref, wdn16_ref, m_ref, acc_ref, s_ref):
    for w_ref, w16_ref in ((wo_ref, wo16_ref), (wup_ref, wup16_ref), (wdn_ref, wdn16_ref)):
        w16_ref[...] = w_ref[...].astype(BF16)
    sub = MLA_K // V_TILE
    full, diag = _mla_pairs()
    rel = (lax.broadcasted_iota(jnp.int32, (MLA_K, MLA_Q), 1)
           - lax.broadcasted_iota(jnp.int32, (MLA_K, MLA_Q), 0))
    _flash_init(m_ref, acc_ref)

    def scores(qt, kt, slot, masked):
        qrows = pl.ds(pl.multiple_of(qt * MLA_Q, MLA_Q), MLA_Q)
        krows = pl.ds(pl.multiple_of(kt * MLA_K, MLA_K), MLA_K)
        for hh in range(2):
            cols = slice(hh * LANES, (hh + 1) * LANES)
            s = _dot_nt(k_ref[krows, cols], q_ref[qrows, cols])
            if masked:
                s = jnp.where(rel >= kt * MLA_K - qt * MLA_Q, s, NEG)
            s_ref[slot, hh] = s

    def update(qt, kt, slot):
        for hh in range(2):
            v = jnp.concatenate([vt_ref[kt * sub + i, hh * V_ROWS:(hh + 1) * V_ROWS, :] for i in range(sub)],
                                axis=1)
            _flash_step(s_ref[slot, hh], v, m_ref.at[hh, qt], acc_ref.at[hh, qt])

    _pipelined_pairs(sched_ref, 0, len(full), lambda qt, kt, slot: scores(qt, kt, slot, False), update, None)
    _pipelined_pairs(sched_ref, 2, len(diag), lambda qt, kt, slot: scores(qt, kt, slot, True), update, None)

    for qt in range(SEQ // MLA_Q):
        o_t = jnp.concatenate([_flash_finish(m_ref.at[hh, qt], acc_ref.at[hh, qt])
                               for hh in range(2)], axis=0)
        o_ref[qt * MLA_Q:(qt + 1) * MLA_Q, :] = _dot_nt(eye_ref[...], o_t.astype(BF16)).astype(BF16)


def _schedule(*pair_lists):
    n = max(len(p) for p in pair_lists)
    out = np.zeros((2 * len(pair_lists), n), np.int32)
    for i, pairs in enumerate(pair_lists):
        out[2 * i, :len(pairs)] = [a for a, _ in pairs]
        out[2 * i + 1, :len(pairs)] = [b for _, b in pairs]
    return jnp.asarray(out)


def _mla_attention(q, k, vt, w_o, w_up, w_down):
    b = q.shape[0] // SEQ
    nq = SEQ // MLA_Q
    nv = SEQ // V_TILE
    chunks = MLA_K // KEY_CHUNK
    head_pairs = MLA_HEADS // 2
    steps = b * head_pairs
    weights = [w_o, w_up, w_down]
    assert all(w.shape[0] % (steps * BF16_ROWS) == 0 for w in weights)
    w_specs = [pl.BlockSpec((w.shape[0] // steps, w.shape[1]), lambda i, hp: (i * head_pairs + hp, 0))
               for w in weights]
    return pl.pallas_call(
        _mla_kernel,
        grid=(b, head_pairs),
        in_specs=[pl.BlockSpec(memory_space=pltpu.SMEM),
                  pl.BlockSpec((SEQ, 2 * LANES), lambda i, hp: (i, hp)),
                  pl.BlockSpec((SEQ, 2 * LANES), lambda i, hp: (i, hp)),
                  pl.BlockSpec((nv, 2 * V_ROWS, V_TILE), lambda i, hp: (i, hp, 0)),
                  pl.BlockSpec((MLA_Q, MLA_Q), lambda i, hp: (0, 0))] + w_specs,
        out_specs=[pl.BlockSpec((SEQ, LANES), lambda i, hp: (i, hp))] + w_specs,
        out_shape=[jax.ShapeDtypeStruct((q.shape[0], MLA_HEADS * MLA_V), BF16)]
                  + [jax.ShapeDtypeStruct(w.shape, BF16) for w in weights],
        scratch_shapes=[pltpu.VMEM((2, nq, chunks, 1, MLA_Q), F32),
                        pltpu.VMEM((2, nq, chunks, V_ROWS, MLA_Q), F32),
                        pltpu.VMEM((2, 2, MLA_K, MLA_Q), F32)],
        compiler_params=pltpu.CompilerParams(
            dimension_semantics=("parallel", "parallel"), vmem_limit_bytes=VMEM_LIMIT),
        name="mla_attention",
    )(_schedule(*_mla_pairs()), q, k, vt, jnp.eye(MLA_Q, dtype=BF16), *weights)


def _select_kernel(q_ref, kc_ref, vct_ref, tabc_ref, ovt_ref, eye_ref, qpad_ref, qaug_ref, ocmp_ref):
    step = pl.program_id(1)
    t = SELECT_Q
    heads = range(NSA_GROUP)
    tiles = range(t // NSA_Q)

    def group(g):
        q_pad = _dot(q_ref[:, g * 2 * LANES:(g + 1) * 2 * LANES], qpad_ref[...]).astype(BF16)
        low = lax.broadcasted_iota(jnp.int32, (t, LANES), 1) < NSA_HEAD_DIM
        zero = jnp.zeros((t, LANES), BF16)
        pairs = [q_ref[:, (2 * g + j) * LANES:(2 * g + j + 1) * LANES] for j in range(NSA_GROUP // 2)]
        q4 = jnp.concatenate([jnp.where(low == (r % 2 == 0), pairs[r // 2], zero) for r in heads], axis=0)

        per_tile = NSA_Q // CMP_STRIDE
        rows = [pl.ds(pl.multiple_of(CMP_TAB_BASE - (step * len(tiles) + u) * per_tile, per_tile), N_CHUNK)
                for u in tiles]
        bias_c = jnp.concatenate([tabc_ref[g * NSA_GROUP + r, rows[u], :] for r in heads for u in tiles], axis=1)
        valid = bias_c > 0.5 * NEG
        sc = jnp.where(valid, _dot_nt(kc_ref[0, g], q4) + bias_c, NEG)
        mx = jnp.max(sc, axis=0, keepdims=True)
        p = jnp.where(valid, jnp.exp2(sc - mx), 0.0)
        pc = p / jnp.maximum(jnp.sum(p, axis=0, keepdims=True), 1e-30)
        o_cmp = _dot(vct_ref[0, g], pc.astype(BF16)).astype(BF16)
        for u in tiles:
            ocmp_ref[0, g, u] = jnp.concatenate(
                [o_cmp[:, r * t + u * NSA_Q:r * t + (u + 1) * NSA_Q] for r in heads], axis=0)

        psum = sum(pc[:, r * t:(r + 1) * t] for r in heads)
        imp = sum(_dot(ovt_ref[...], piece) for piece in _split(psum, 3))
        jj = lax.broadcasted_iota(jnp.int32, (N_SEL, t), 0)
        blk = (step * t + lax.broadcasted_iota(jnp.int32, (N_SEL, t), 1)) // SEL_BLOCK
        ok = jj <= blk
        forced = ok & ((jj == 0) | (jj == blk) | (jj == blk - 1))
        score = jnp.where(forced, FORCE_SCORE, jnp.where(ok, imp, -jnp.inf))
        rank = jnp.zeros((N_SEL, t), jnp.int32)
        for i in range(N_SEL):
            ci = score[i:i + 1, :]
            beats = (ci > score) | ((ci == score) & (jj > i))
            rank = rank + beats.astype(jnp.int32)
        negsel = jnp.where(ok & (rank < SEL_TOP_N), 0.0, NEG).astype(BF16)
        spread = jnp.concatenate([jnp.zeros((AUG_BLOCK, t), BF16), negsel,
                                  jnp.zeros((LANES - AUG_ONE, t), BF16)], axis=0)
        mask_lanes = _dot_nt(eye_ref[...], spread).astype(BF16)
        lane = lax.broadcasted_iota(jnp.int32, (t, LANES), 1)
        for r in heads:
            far = tabc_ref[g * NSA_GROUP + r, 0:1, :]
            far_hi = far.astype(BF16)
            far_lo = (far - far_hi.astype(F32)).astype(BF16)
            aug = jnp.where(lane == AUG_ONE, far_hi, jnp.where(lane == AUG_ONE + 1, far_lo, mask_lanes))
            q_aug = jnp.where(lane < AUG_BLOCK, q_pad[:, r * LANES:(r + 1) * LANES], aug)
            for u in tiles:
                qaug_ref[0, g, u, r * NSA_Q:(r + 1) * NSA_Q, :] = q_aug[u * NSA_Q:(u + 1) * NSA_Q]

    for g in range(NSA_KV_HEADS):
        group(g)


def _nsa_select(q, kc, vct, tables):
    b = q.shape[0] // SEQ
    t = SELECT_Q
    steps = SEQ // t
    nq = SEQ // NSA_Q
    per = t // NSA_Q
    g_ = NSA_KV_HEADS
    c0 = np.arange(N_CHUNK)[None, :] * CMP_STRIDE
    j0 = np.arange(N_SEL)[:, None] * SEL_BLOCK
    ovt = np.clip(np.minimum(c0 + CMP_BLOCK, j0 + SEL_BLOCK) - np.maximum(c0, j0), 0, None) / CMP_BLOCK
    ovt[:, N_CHUNK - 1:] = 0.0
    return pl.pallas_call(
        _select_kernel,
        grid=(b, steps),
        in_specs=[pl.BlockSpec((t, g_ * 2 * LANES), lambda i, j: (i * steps + j, 0)),
                  pl.BlockSpec((1, g_, N_CHUNK, LANES), lambda i, j: (i, 0, 0, 0)),
                  pl.BlockSpec((1, g_, NSA_HEAD_DIM, N_CHUNK), lambda i, j: (i, 0, 0, 0)),
                  pl.BlockSpec((NSA_HEADS, CMP_TAB_ROWS, LANES), lambda i, j: (0, WIN_ROWS // CMP_TAB_ROWS, 0)),
                  pl.BlockSpec((N_SEL, N_CHUNK), lambda i, j: (0, 0)),
                  pl.BlockSpec((t, t), lambda i, j: (0, 0)),
                  pl.BlockSpec((2 * LANES, NSA_GROUP * LANES), lambda i, j: (0, 0))],
        out_specs=[pl.BlockSpec((1, g_, per, NSA_GROUP * NSA_Q, LANES), lambda i, j: (i, 0, j, 0, 0)),
                   pl.BlockSpec((1, g_, per, NSA_GROUP * NSA_HEAD_DIM, NSA_Q), lambda i, j: (i, 0, j, 0, 0))],
        out_shape=[jax.ShapeDtypeStruct((b, g_, nq, NSA_GROUP * NSA_Q, LANES), BF16),
                   jax.ShapeDtypeStruct((b, g_, nq, NSA_GROUP * NSA_HEAD_DIM, NSA_Q), BF16)],
        compiler_params=pltpu.CompilerParams(
            dimension_semantics=("parallel", "parallel"), vmem_limit_bytes=VMEM_LIMIT),
        name="nsa_select",
    )(q, kc, vct, tables, jnp.asarray(ovt, BF16), jnp.eye(t, dtype=BF16),
      jnp.asarray(_head_padding(NSA_GROUP), BF16))


def _nsa_pairs():
    nq, nk, per = SEQ // NSA_Q, SEQ // NSA_K, NSA_K // NSA_Q
    sel = [(qt, kt) for kt in range(nk) for qt in range(nq) if kt * NSA_K <= qt * NSA_Q]
    far = [(qt, kt) for qt, kt in sel if qt - kt * per >= SEL_FAR_TILE]
    near = [(qt, kt) for qt, kt in sel if qt - kt * per < SEL_FAR_TILE]
    win = [(qt, kt) for kt in range(nk) for qt in range(nq)
           if max(qt * NSA_Q - (WINDOW - 1), 0) // NSA_K <= kt and kt * NSA_K <= qt * NSA_Q]
    return far, near, win


def _attend_kernel(sched_ref, qaug_ref, ksel_ref, kwin_ref, vsel_ref, vwin_ref, ocmp_ref, gate_ref,
                   tabw_ref, eye_ref, o_ref, m_ref, acc_ref, s_ref, p_ref, alpha_ref):
    t = NSA_Q
    heads = range(NSA_GROUP)
    far, near, win = _nsa_pairs()
    _flash_init(m_ref, acc_ref)

    def scores(qt, kt, slot, k_ref, add_bias):
        s = _dot_nt(k_ref[pl.ds(pl.multiple_of(kt * NSA_K, NSA_K), NSA_K), :], qaug_ref[0, 0, qt])
        if add_bias:
            off = qt - kt * (NSA_K // NSA_Q)
            bias_rows = pl.ds(pl.multiple_of(off * NSA_K, NSA_K), NSA_K)
            s = s + jnp.concatenate([tabw_ref[r, bias_rows, :] for r in heads], axis=1)
        s_ref[slot] = s

    def softmax(qt, kt, slot, branch):
        _flash_softmax(s_ref[slot], m_ref.at[branch, qt], p_ref.at[slot], alpha_ref.at[slot], chunk=NSA_K)

    def accumulate(qt, kt, slot, v_ref, branch):
        _flash_accumulate(v_ref[kt], p_ref.at[slot], alpha_ref.at[slot], acc_ref.at[branch, qt], chunk=NSA_K)

    for row, pairs, k_ref, v_ref, branch, add_bias in ((0, far, ksel_ref, vsel_ref, 0, False),
                                                       (2, near, ksel_ref, vsel_ref, 0, True),
                                                       (4, win, kwin_ref, vwin_ref, 1, True)):
        _pipelined_pairs(sched_ref, row, len(pairs),
                         lambda qt, kt, slot, k_ref=k_ref, add_bias=add_bias: scores(qt, kt, slot, k_ref, add_bias),
                         lambda qt, kt, slot, branch=branch: softmax(qt, kt, slot, branch),
                         lambda qt, kt, slot, v_ref=v_ref, branch=branch: accumulate(qt, kt, slot, v_ref, branch),
                         unroll=NSA_UNROLL)

    d = NSA_HEAD_DIM
    group = pl.program_id(1)
    for qt in range(SEQ // t):
        o_cmp = ocmp_ref[0, 0, qt].astype(F32)
        o_sel, o_win = [_flash_finish(m_ref.at[br, qt], acc_ref.at[br, qt]) for br in range(2)]
        mixed = []
        for r in heads:
            gate = [gate_ref[qt, pl.ds(br * NSA_HEADS + group * NSA_GROUP + r, 1), :]
                    for br in range(N_BRANCH)]
            mixed.append(gate[0] * o_cmp[r * d:(r + 1) * d] + gate[1] * o_sel[:, r * t:(r + 1) * t]
                         + gate[2] * o_win[:, r * t:(r + 1) * t])
        o_ref[qt * t:(qt + 1) * t, :] = _dot_nt(eye_ref[...], jnp.concatenate(mixed, axis=0).astype(BF16)).astype(BF16)


def _nsa_attend(qaug, kaug, vt, ocmp, gates, tables):
    b = kaug.shape[0] // SEQ
    t = NSA_Q
    nq = SEQ // t
    nv = SEQ // V_TILE
    g_ = NSA_KV_HEADS
    k_spec = lambda off: pl.BlockSpec((SEQ, LANES), lambda i, g: (i, off + g))
    v_spec = lambda off: pl.BlockSpec((nv, V_ROWS, V_TILE), lambda i, g: (i, off + g, 0))
    return pl.pallas_call(
        _attend_kernel,
        grid=(b, g_),
        in_specs=[pl.BlockSpec(memory_space=pltpu.SMEM),
                  pl.BlockSpec((1, 1, nq, NSA_GROUP * t, LANES), lambda i, g: (i, g, 0, 0, 0)),
                  k_spec(0), k_spec(2), v_spec(0), v_spec(2),
                  pl.BlockSpec((1, 1, nq, NSA_GROUP * NSA_HEAD_DIM, t), lambda i, g: (i, g, 0, 0, 0)),
                  pl.BlockSpec((nq, LANES, t), lambda i, g: (i, 0, 0)),
                  pl.BlockSpec((NSA_GROUP, WIN_ROWS, LANES), lambda i, g: (g, 0, 0)),
                  pl.BlockSpec((t, t), lambda i, g: (0, 0))],
        out_specs=pl.BlockSpec((SEQ, 2 * LANES), lambda i, g: (i, g)),
        out_shape=jax.ShapeDtypeStruct((kaug.shape[0], NSA_HEADS * NSA_HEAD_DIM), BF16),
        scratch_shapes=[pltpu.VMEM((2, nq, 1, 1, NSA_GROUP * t), F32),
                        pltpu.VMEM((2, nq, 1, V_ROWS, NSA_GROUP * t), F32),
                        pltpu.VMEM((2, NSA_K, NSA_GROUP * t), F32),
                        pltpu.VMEM((2, NSA_K, NSA_GROUP * t), BF16),
                        pltpu.VMEM((2, 1, 1, NSA_GROUP * t), F32)],
        compiler_params=pltpu.CompilerParams(
            dimension_semantics=("parallel", "parallel"), vmem_limit_bytes=VMEM_LIMIT),
        name="nsa_attend",
    )(_schedule(*_nsa_pairs()), qaug, kaug, kaug, vt, vt, ocmp, gates, tables, jnp.eye(t, dtype=BF16))


def _tail_kernel(x_ref, oa_ref, ob_ref, ga_ref, gb_ref, wo_ref, ln2_ref, wup_ref, wdn_ref,
                 out_ref, h2_ref):
    width = MLA_HEADS * MLA_V
    a = (_rms(oa_ref[...].astype(F32), width) * ga_ref[...]).astype(BF16)
    b = (_rms(ob_ref[...].astype(F32), width) * gb_ref[...]).astype(BF16)
    x1 = x_ref[...] + _dot(a, wo_ref[:width]) + _dot(b, wo_ref[width:])
    h2_ref[...] = (_rms(x1, D_MODEL) * ln2_ref[...]).astype(BF16)
    out_ref[...] = x1

    def body(f, carry):
        cols = pl.ds(pl.multiple_of(f * FF_CHUNK, FF_CHUNK), FF_CHUNK)
        u = jnp.maximum(_dot(h2_ref[...], wup_ref[:, cols]), 0.0)
        out_ref[...] += _dot((u * u).astype(BF16), wdn_ref[f])
        return carry

    lax.fori_loop(0, D_FF // FF_CHUNK, body, 0)


def _tail(x2d, o_a, o_b, gn_a, gn_b, w_o, ln2_g, w_up, w_down):
    t = x2d.shape[0]
    rows = TAIL_ROWS
    width = MLA_HEADS * MLA_V
    once = lambda a: pl.BlockSpec(a.shape, lambda i: (0,) * a.ndim)
    assert w_o.dtype == w_up.dtype == w_down.dtype == BF16
    consts = [gn_a[None, :], gn_b[None, :], w_o, ln2_g[None, :], w_up,
              w_down.reshape(D_FF // FF_CHUNK, FF_CHUNK, D_MODEL)]
    row_spec = lambda w: pl.BlockSpec((rows, w), lambda i: (i, 0))
    return pl.pallas_call(
        _tail_kernel,
        grid=(t // rows,),
        in_specs=[row_spec(D_MODEL), row_spec(width), row_spec(width)] + [once(a) for a in consts],
        out_specs=row_spec(D_MODEL),
        out_shape=jax.ShapeDtypeStruct((t, D_MODEL), F32),
        scratch_shapes=[pltpu.VMEM((rows, D_MODEL), BF16)],
        compiler_params=pltpu.CompilerParams(dimension_semantics=("parallel",),
                                             vmem_limit_bytes=VMEM_LIMIT),
        name="tail",
    )(x2d, o_a, o_b, *consts)


def kernel(x, ln1_g, w_in, mla_cq_norm_g, mla_ckv_norm_g, mla_w_uq, mla_w_ukv, mla_q_gain, mla_k_gain,
           nsa_q_gain, nsa_k_gain, nsa_cmp_pe, nsa_cmp_w1, nsa_cmp_w2, rel_bias, grp_norm_mla,
           grp_norm_nsa, w_o, ln2_g, w_up, w_down):
    b, s, d = x.shape
    assert (s, d) == (SEQ, D_MODEL) and ln1_g.shape[0] == 1
    x2d = x.reshape(b * s, d)
    tables = _bias_tables(rel_bias)
    q_mla, k_mla, vt_mla, q_nsa, kdup, vt_nsa, kck, kcv, gates = _projections(
        x2d, ln1_g[0], w_in[0], mla_cq_norm_g[0], mla_ckv_norm_g[0], mla_w_uq[0], mla_w_ukv[0],
        mla_q_gain[0], mla_k_gain[0], nsa_q_gain[0], nsa_k_gain[0])
    kc, vct = _compress(kck, kcv, nsa_cmp_pe[0], nsa_cmp_w1[0], nsa_cmp_w2[0], nsa_k_gain[0, 0])
    o_a, w_o16, w_up16, w_down16 = _mla_attention(q_mla, k_mla, vt_mla, w_o[0], w_up[0], w_down[0])
    q_aug, ocmp = _nsa_select(q_nsa, kc, vct, tables)
    o_b = _nsa_attend(q_aug, kdup, vt_nsa, ocmp, gates, tables)
    out = _tail(x2d, o_a, o_b, grp_norm_mla[0], grp_norm_nsa[0], w_o16, ln2_g[0], w_up16, w_down16)
    return out.reshape(b, s, d)
```

```python
import math

import numpy as np
import jax
import jax.numpy as jnp
from jax import lax
from jax.experimental import pallas as pl
from jax.experimental.pallas import tpu as pltpu

F32 = jnp.float32
BF16 = jnp.bfloat16

D_MODEL = 1024
SEQ = 2048
MLA_HEADS = 8
MLA_NOPE = 64
MLA_ROPE = 32
MLA_V = 64
MLA_QK = MLA_NOPE + MLA_ROPE
MLA_Q_RANK = 384
MLA_KV_RANK = 256
ROPE_THETA = 10000.0
NSA_HEADS = 8
NSA_KV_HEADS = 2
NSA_GROUP = NSA_HEADS // NSA_KV_HEADS
NSA_HEAD_DIM = 64
N_BRANCH = 3
CMP_BLOCK = 32
CMP_STRIDE = 16
CMP_HIDDEN = 128
SEL_BLOCK = 64
SEL_TOP_N = 16
WINDOW = 512
FORCE_SCORE = 1e4
REL_BUCKETS = 32
REL_MAX_DIST = 128
D_FF = 4 * D_MODEL
EPS = 1e-6

LANES = 128
BF16_ROWS = 16
NEG = -1e30
N_CHUNK = SEQ // CMP_STRIDE
N_SEL = SEQ // SEL_BLOCK
NSA_Q = 128
NSA_K = 256
SELECT_Q = 512
N_BIAS_TILES = (WINDOW + NSA_K - NSA_Q) // NSA_Q + 1
SEL_FAR_TILE = (REL_MAX_DIST + NSA_K) // NSA_Q
WIN_ROWS = N_BIAS_TILES * NSA_K
AUG_BLOCK = NSA_HEAD_DIM
AUG_ONE = AUG_BLOCK + N_SEL
CMP_TAB_BASE = (SEQ // NSA_Q - 1) * (NSA_Q // CMP_STRIDE)
CMP_TAB_ROWS = 256
MLA_Q = 256
MLA_K = 512
V_TILE = 256
PROJ_ROWS = 256
PACK_ROWS = 256
TAIL_ROWS = 512
FF_CHUNK = 2048
KEY_CHUNK = 256
PAIR_UNROLL = 6
NSA_UNROLL = 12
V_ROWS = 80
LOG2E = math.log2(math.e)
VMEM_LIMIT = 56 * 1024 * 1024

assert PROJ_ROWS % V_TILE == 0 and MLA_K % V_TILE == 0 and NSA_K == V_TILE


def _dot(a, b):
    return jnp.dot(a, b, preferred_element_type=F32)


def _dot_nt(a, b):
    return lax.dot_general(a, b, (((1,), (1,)), ((), ())), preferred_element_type=F32)


def _split(a, terms):
    pieces = []
    rem = a
    for _ in range(terms):
        piece = rem.astype(BF16)
        pieces.append(piece)
        rem = rem - piece.astype(F32)
    return pieces


def _split_dot(a, b, terms=2):
    return sum(_dot(p, b) for p in _split(a, terms))


def _rms(x, width):
    return x * lax.rsqrt(jnp.sum(x * x, axis=-1, keepdims=True) * (1.0 / width) + EPS)


def _t5_bucket_np(dist):
    n = np.maximum(dist, 0)
    max_exact = REL_BUCKETS // 2
    large = max_exact + (np.log(np.maximum(n, 1).astype(np.float32) / max_exact)
                         / math.log(REL_MAX_DIST / max_exact)
                         * (REL_BUCKETS - max_exact)).astype(np.int32)
    large = np.minimum(large, REL_BUCKETS - 1)
    return np.where(n < max_exact, n, large).astype(np.int32)


def _bias_index_table():
    i = np.arange(NSA_Q)[None, :]
    parts = []
    j = np.arange(NSA_K)[:, None]
    for off in range(N_BIAS_TILES):
        d = off * NSA_Q + i - j
        parts.append(np.where((d >= 0) & (d < WINDOW), _t5_bucket_np(d), -1))
    r = np.arange(CMP_TAB_ROWS)[:, None]
    dist_c = (CMP_TAB_BASE - r) * CMP_STRIDE + i - (CMP_BLOCK - 1)
    parts.append(np.where(dist_c >= 0, _t5_bucket_np(dist_c), -1))
    return np.concatenate(parts, axis=0).astype(np.int32)


def _bias_table_kernel(rb_ref, idx_ref, out_ref):
    idx = idx_ref[...]
    bucket = jnp.maximum(idx, 0)
    is_window = pl.program_id(0) < WIN_ROWS // idx.shape[0]
    for h in range(NSA_HEADS):
        row = rb_ref[h:h + 1, :]
        far = jnp.where(is_window, row[:, REL_BUCKETS - 1:REL_BUCKETS], 0.0)
        values = jnp.broadcast_to((row - far) * LOG2E, idx.shape)
        picked = jnp.take_along_axis(values, bucket, axis=1, mode="promise_in_bounds")
        out_ref[h] = jnp.where(idx >= 0, picked, NEG)


def _bias_tables(rel_bias):
    idx = jnp.asarray(_bias_index_table())
    rows = idx.shape[0]
    blk = CMP_TAB_ROWS
    return pl.pallas_call(
        _bias_table_kernel,
        grid=(rows // blk,),
        in_specs=[pl.BlockSpec((NSA_HEADS, LANES), lambda r: (0, 0)),
                  pl.BlockSpec((blk, LANES), lambda r: (r, 0))],
        out_specs=pl.BlockSpec((NSA_HEADS, blk, LANES), lambda r: (0, r, 0)),
        out_shape=jax.ShapeDtypeStruct((NSA_HEADS, rows, LANES), F32),
        name="bias_tables",
    )(jnp.pad(rel_bias.T, ((0, 0), (0, LANES - REL_BUCKETS))), idx)


ROPE_HALF = MLA_ROPE // 2
MISC_KR = 64


def _rope_tables(scale):
    inv = 1.0 / (ROPE_THETA ** (jnp.arange(ROPE_HALF, dtype=F32) / ROPE_HALF))
    ang = jnp.arange(SEQ, dtype=F32)[:, None] * inv[None, :]
    cos, sin = jnp.cos(ang), jnp.sin(ang)
    ones = jnp.ones((SEQ, MLA_NOPE), F32)
    z32 = jnp.zeros((SEQ, LANES - MLA_QK), F32)
    z64 = jnp.zeros((SEQ, MLA_NOPE), F32)
    c = jnp.concatenate([ones, cos, cos, z32], axis=1) * scale
    s = jnp.concatenate([z64, -sin, sin, z32], axis=1) * scale
    return c, s


def _pair_norm(x, gain_ref):
    low = lax.broadcasted_iota(jnp.int32, (x.shape[0], LANES), 1) < NSA_HEAD_DIM
    outs = []
    for j in range(x.shape[1] // LANES):
        cols = slice(j * LANES, (j + 1) * LANES)
        sl = x[:, cols]
        sq = sl * sl
        tot = jnp.sum(sq, axis=-1, keepdims=True)
        lo = jnp.sum(jnp.where(low, sq, 0.0), axis=-1, keepdims=True)
        rs = jnp.where(low, lax.rsqrt(lo * (1.0 / NSA_HEAD_DIM) + EPS),
                       lax.rsqrt((tot - lo) * (1.0 / NSA_HEAD_DIM) + EPS))
        outs.append(sl * rs * gain_ref[:, cols])
    return jnp.concatenate(outs, axis=1)


def _mla_query_heads(x, gain_ref, c, s):
    in_head = lax.broadcasted_iota(jnp.int32, (x.shape[0], LANES), 1) < MLA_QK
    outs = []
    for h in range(MLA_HEADS):
        cols = slice(h * LANES, (h + 1) * LANES)
        sl = x[:, cols]
        ss = jnp.sum(jnp.where(in_head, sl * sl, 0.0), axis=-1, keepdims=True)
        xn = sl * lax.rsqrt(ss * (1.0 / MLA_QK) + EPS) * gain_ref[:, cols]
        outs.append(xn * c + pltpu.roll(xn, LANES - ROPE_HALF, 1) * s)
    return jnp.concatenate(outs, axis=1)


def _mla_key_heads(x, k_rope, k_swap, c, s):
    in_head = lax.broadcasted_iota(jnp.int32, (x.shape[0], LANES), 1) < MLA_QK
    turned = k_swap * s
    outs = []
    for h in range(MLA_HEADS):
        sl = x[:, h * LANES:(h + 1) * LANES] + k_rope
        ss = jnp.sum(jnp.where(in_head, sl * sl, 0.0), axis=-1, keepdims=True)
        outs.append(lax.rsqrt(ss * (1.0 / MLA_QK) + EPS) * (sl * c + turned))
    return jnp.concatenate(outs, axis=1)


COL_KV = MLA_Q_RANK
COL_MISC = COL_KV + MLA_KV_RANK
COL_SWAP = COL_MISC + LANES
COL_NQ = COL_SWAP + LANES
COL_NK = COL_NQ + NSA_HEADS * NSA_HEAD_DIM
COL_NV = COL_NK + 4 * NSA_HEAD_DIM
COL_CMP = COL_NV + 4 * NSA_HEAD_DIM
COL_END = COL_CMP + 4 * NSA_HEAD_DIM


def _proj_kernel(x_ref, ln1_ref, wp_ref, cqg_ref, ckvg_ref, wuq_ref, wukk_ref, wukvt_ref,
                 qg_ref, qc_ref, qs_ref, kc_ref, ks_ref,
                 nq_gain_ref, nk_gain_ref, kpad_ref, kaug_ref, vpick_ref, ones_mla_ref, ones_nsa_ref, eye_ref,
                 qmla_ref, kmla_ref, vmla_ref, qnsa_ref, kdup_ref, vnsa_ref, kck_ref, kcv_ref, gate_ref):
    for sub in range(PROJ_ROWS // V_TILE):
        rows = slice(sub * V_TILE, (sub + 1) * V_TILE)
        x = x_ref[rows, :]
        h = (_rms(x, D_MODEL) * ln1_ref[...]).astype(BF16)
        part = lambda a, b: _dot(h, wp_ref[:, a:b])

        cq = (_rms(part(0, COL_KV), MLA_Q_RANK) * cqg_ref[...]).astype(BF16)
        qmla_ref[rows, :] = _mla_query_heads(_dot(cq, wuq_ref[...]), qg_ref,
                                             qc_ref[rows, :], qs_ref[rows, :]).astype(BF16)

        kv_misc = part(COL_KV, COL_NQ)
        ckv = (_rms(kv_misc[:, :MLA_KV_RANK], MLA_KV_RANK) * ckvg_ref[...]).astype(BF16)
        misc = kv_misc[:, COL_MISC - COL_KV:COL_SWAP - COL_KV]
        lane = lax.broadcasted_iota(jnp.int32, misc.shape, 1)
        k_rope = jnp.where((lane >= MISC_KR) & (lane < MISC_KR + MLA_ROPE), misc, 0.0)
        kmla_ref[rows, :] = _mla_key_heads(_dot(ckv, wukk_ref[...]), k_rope, kv_misc[:, COL_SWAP - COL_KV:],
                                           kc_ref[rows, :], ks_ref[rows, :]).astype(BF16)
        vmla_ref[sub] = (_dot_nt(wukvt_ref[...], ckv) + ones_mla_ref[...]).astype(BF16)

        qnsa_ref[rows, :] = _pair_norm(part(COL_NQ, COL_NK), nq_gain_ref).astype(BF16)

        kn = _pair_norm(part(COL_NK, COL_NV), nk_gain_ref)
        kdup_ref[rows, :] = (_dot(kn.astype(BF16), kpad_ref[...]) + kaug_ref[rows, :]).astype(BF16)
        vnsa_ref[sub] = (_dot_nt(vpick_ref[...], part(COL_NV, COL_CMP).astype(BF16)) + ones_nsa_ref[...]).astype(BF16)
        cmp_kv = part(COL_CMP, COL_END)
        kck_ref[rows, :] = cmp_kv[:, :LANES]
        kcv_ref[rows, :] = cmp_kv[:, LANES:]
        gate = 1.0 / (1.0 + jnp.exp(-misc))
        gate_t = sum(_dot_nt(eye_ref[...], piece) for piece in _split(gate, 2))
        for u in range(V_TILE // NSA_Q):
            gate_ref[sub * (V_TILE // NSA_Q) + u] = gate_t[:, u * NSA_Q:(u + 1) * NSA_Q]


def _head_padding(heads):
    d = NSA_HEAD_DIM
    m = np.zeros((heads * d, heads * LANES), np.float32)
    for hh in range(heads):
        m[hh * d + np.arange(d), hh * LANES + np.arange(d)] = 1.0
    return m


def _packed_segments():
    kv0 = 1184
    seg = lambda a, b: (a, b - a)
    kvb = lambda br, kv, g: (kv0 + ((br * 2 + kv) * 2 + g) * 64, 64)
    zeros = lambda n: (None, n)
    kr = COL_MISC
    gate_cols = N_BRANCH * NSA_HEADS
    pieces = [
        seg(0, COL_MISC),
        seg(1952, 1952 + gate_cols), zeros(MISC_KR - gate_cols), seg(kr, kr + MLA_ROPE),
        zeros(LANES - MISC_KR - MLA_ROPE),
        zeros(MISC_KR), seg(kr + ROPE_HALF, kr + MLA_ROPE), seg(kr, kr + ROPE_HALF), zeros(LANES - MISC_KR - MLA_ROPE),
        seg(672, 1184),
        kvb(1, 0, 0), kvb(1, 0, 1), kvb(2, 0, 0), kvb(2, 0, 1),
        kvb(1, 1, 0), kvb(1, 1, 1), kvb(2, 1, 0), kvb(2, 1, 1),
        kvb(0, 0, 0), kvb(0, 0, 1), kvb(0, 1, 0), kvb(0, 1, 1),
    ]
    assert sum(n for _, n in pieces) == COL_END
    return pieces


def _pack_kernel(wt_ref, out_ref, packed_ref):
    dst = 0
    for src, n in _packed_segments():
        assert n % 8 == 0 and dst % 8 == 0 and (src is None or src % 8 == 0)
        if src is None:
            packed_ref[dst:dst + n, :] = jnp.zeros((n, packed_ref.shape[1]), F32)
        else:
            packed_ref[dst:dst + n, :] = wt_ref[src:src + n, :]
        dst += n
    for g in range(COL_END // LANES):
        cols = slice(g * LANES, (g + 1) * LANES)
        out_ref[:, cols] = packed_ref[cols, :].T.astype(BF16)


def _pack_input_projection(w_in):
    rows = PACK_ROWS
    return pl.pallas_call(
        _pack_kernel,
        grid=(D_MODEL // rows,),
        in_specs=[pl.BlockSpec((w_in.shape[1], rows), lambda i: (0, i))],
        out_specs=pl.BlockSpec((rows, COL_END), lambda i: (i, 0)),
        out_shape=jax.ShapeDtypeStruct((D_MODEL, COL_END), BF16),
        scratch_shapes=[pltpu.VMEM((COL_END, rows), F32)],
        compiler_params=pltpu.CompilerParams(dimension_semantics=("parallel",),
                                             vmem_limit_bytes=VMEM_LIMIT),
        name="pack_input_projection",
    )(w_in.T)


def _projections(x2d, ln1_g, w_in, cq_g, ckv_g, w_uq, w_ukv, q_gain, k_gain, nq_gain, nk_gain):
    t = x2d.shape[0]
    wp = _pack_input_projection(w_in)
    wq3 = w_uq.reshape(MLA_Q_RANK, MLA_HEADS, MLA_QK).astype(BF16)
    wuq = jnp.concatenate([wq3, wq3[:, :, MLA_NOPE:MLA_NOPE + ROPE_HALF],
                           jnp.zeros((MLA_Q_RANK, MLA_HEADS, ROPE_HALF), BF16)], axis=2)
    wuq = wuq.reshape(MLA_Q_RANK, MLA_HEADS * LANES)
    wkv = w_ukv.reshape(MLA_KV_RANK, MLA_HEADS, MLA_NOPE + MLA_V)
    wukk = jnp.pad(wkv[:, :, :MLA_NOPE], ((0, 0), (0, 0), (0, LANES - MLA_NOPE)))
    wukk = wukk.reshape(MLA_KV_RANK, MLA_HEADS * LANES).astype(BF16)
    wukvt = jnp.pad(wkv[:, :, MLA_NOPE:], ((0, 0), (0, 0), (0, V_ROWS - MLA_V)))
    wukvt = wukvt.reshape(MLA_KV_RANK, MLA_HEADS * V_ROWS).T.astype(BF16)
    q_slab_gain = jnp.tile(jnp.concatenate(
        [q_gain, q_gain[MLA_NOPE:MLA_NOPE + ROPE_HALF], jnp.zeros((ROPE_HALF,), F32)]), MLA_HEADS)[None, :]
    qc, qs = _rope_tables(MLA_QK ** -0.5 * LOG2E)
    kc, ks = _rope_tables(1.0)
    k_own = jnp.pad(k_gain, (0, LANES - MLA_QK))[None, :]
    k_partner = jnp.concatenate([k_gain[:MLA_NOPE], k_gain[MLA_NOPE + ROPE_HALF:],
                                 k_gain[MLA_NOPE:MLA_NOPE + ROPE_HALF], jnp.zeros((LANES - MLA_QK,), F32)])[None, :]
    kc, ks = kc * k_own, ks * k_partner
    kpad = _head_padding(4)
    kaug = np.zeros((SEQ, 4 * LANES), np.float32)
    for s in range(4):
        kaug[:, s * LANES + AUG_ONE:s * LANES + AUG_ONE + 2] = 1.0
        if s < NSA_KV_HEADS:
            kaug[np.arange(SEQ), s * LANES + AUG_BLOCK + np.arange(SEQ) // SEL_BLOCK] = 1.0
    nqg = (jnp.tile(nq_gain, NSA_HEADS) * (NSA_HEAD_DIM ** -0.5 * LOG2E))[None, :]
    nkg = jnp.concatenate([nk_gain[1], nk_gain[1], nk_gain[2], nk_gain[2]])[None, :]
    vpick = np.zeros((4 * V_ROWS, 256), np.float32)
    for s in range(4):
        vpick[s * V_ROWS + np.arange(64), s * 64 + np.arange(64)] = 1.0

    def ones_col(heads):
        col = np.zeros((heads * V_ROWS, 1), np.float32)
        col[np.arange(heads) * V_ROWS + MLA_V] = 1.0
        return col

    rows = PROJ_ROWS
    n_pos = SEQ // rows
    full = lambda a: pl.BlockSpec(a.shape, lambda i: (0,) * a.ndim)
    tab = pl.BlockSpec((rows, LANES), lambda i: (i % n_pos, 0))
    consts = [ln1_g[None, :], wp, cq_g[None, :], ckv_g[None, :], wuq, wukk, wukvt, q_slab_gain]
    tabs = [qc, qs, kc, ks]
    mats = [nqg, nkg, jnp.asarray(kpad, BF16),
            jnp.asarray(vpick, BF16), jnp.asarray(ones_col(MLA_HEADS)), jnp.asarray(ones_col(4)),
            jnp.eye(LANES, dtype=BF16)]
    row_spec = lambda w: pl.BlockSpec((rows, w), lambda i: (i, 0))
    vt_spec = lambda w, tile: pl.BlockSpec((rows // tile, w, tile), lambda i: (i, 0, 0))
    sds = jax.ShapeDtypeStruct
    return pl.pallas_call(
        _proj_kernel,
        grid=(t // rows,),
        in_specs=([row_spec(D_MODEL)] + [full(a) for a in consts] + [tab] * 4 + [full(a) for a in mats[:3]]
                  + [pl.BlockSpec((rows, 4 * LANES), lambda i: (i % n_pos, 0))] + [full(a) for a in mats[3:]]),
        out_specs=[row_spec(1024), row_spec(1024), vt_spec(MLA_HEADS * V_ROWS, V_TILE), row_spec(512),
                   row_spec(512), vt_spec(4 * V_ROWS, V_TILE), row_spec(128), row_spec(128),
                   vt_spec(LANES, NSA_Q)],
        out_shape=[sds((t, 1024), BF16), sds((t, 1024), BF16), sds((t // V_TILE, MLA_HEADS * V_ROWS, V_TILE), BF16),
                   sds((t, 512), BF16), sds((t, 512), BF16), sds((t // V_TILE, 4 * V_ROWS, V_TILE), BF16),
                   sds((t, 128), F32), sds((t, 128), F32), sds((t // NSA_Q, LANES, NSA_Q), F32)],
        compiler_params=pltpu.CompilerParams(dimension_semantics=("parallel",),
                                             vmem_limit_bytes=VMEM_LIMIT),
        name="projections",
    )(x2d, *consts, *tabs, *mats[:3], jnp.asarray(kaug, BF16), *mats[3:])


def _compress_kernel(xk_ref, xv_ref, pe_ref, w1_ref, w2_ref, kg_ref, dupk_ref, pickv_ref, kc_ref, vct_ref):
    def mlp(x_ref, which):
        top = bot = None
        for t in range(CMP_STRIDE):
            xt = x_ref[pl.ds(t, N_CHUNK, stride=CMP_STRIDE), :]
            a = _dot((xt + pe_ref[which, t]).astype(BF16), w1_ref[which, t])
            b = _dot((xt + pe_ref[which, CMP_STRIDE + t]).astype(BF16), w1_ref[which, CMP_STRIDE + t])
            top = a if top is None else top + a
            bot = b if bot is None else bot + b
        hid = top + pltpu.roll(bot, N_CHUNK - 1, 0)
        act = (hid * (1.0 / (1.0 + jnp.exp(-hid)))).astype(BF16)
        return _dot(act, w2_ref[which])

    k = _pair_norm(mlp(xk_ref, 0), kg_ref).astype(BF16)
    v = mlp(xv_ref, 1).astype(BF16)
    for g in range(NSA_KV_HEADS):
        kc_ref[0, g] = _dot(k, dupk_ref[g]).astype(BF16)
        vct_ref[0, g] = _dot_nt(pickv_ref[g], v).astype(BF16)


def _compress(kck, kcv, cmp_pe, cmp_w1, cmp_w2, k_gain0):
    b = kck.shape[0] // SEQ
    g_ = NSA_KV_HEADS
    d = NSA_HEAD_DIM
    pe = jnp.tile(cmp_pe, (1, 1, g_))[:, :, None, :]
    w1 = cmp_w1.reshape(2, CMP_BLOCK, d, CMP_HIDDEN)
    z1 = jnp.zeros_like(w1)
    w1 = jnp.concatenate([jnp.concatenate([w1, z1], axis=3), jnp.concatenate([z1, w1], axis=3)], axis=2)
    z2 = jnp.zeros_like(cmp_w2)
    w2 = jnp.concatenate([jnp.concatenate([cmp_w2, z2], axis=2), jnp.concatenate([z2, cmp_w2], axis=2)], axis=1)
    dupk = np.zeros((g_, LANES, LANES), np.float32)
    pickv = np.zeros((g_, d, LANES), np.float32)
    for g in range(g_):
        dupk[g, g * d + np.arange(d), np.arange(d)] = 1.0
        dupk[g, g * d + np.arange(d), d + np.arange(d)] = 1.0
        pickv[g, np.arange(d), g * d + np.arange(d)] = 1.0
    full = lambda a: pl.BlockSpec(a.shape, lambda i: (0,) * a.ndim)
    consts = [pe, w1.astype(BF16), w2.astype(BF16), jnp.tile(k_gain0, g_)[None, :],
              jnp.asarray(dupk, BF16), jnp.asarray(pickv, BF16)]
    x_spec = pl.BlockSpec((SEQ, LANES), lambda i: (i, 0))
    return pl.pallas_call(
        _compress_kernel,
        grid=(b,),
        in_specs=[x_spec, x_spec] + [full(a) for a in consts],
        out_specs=[pl.BlockSpec((1, g_, N_CHUNK, LANES), lambda i: (i, 0, 0, 0)),
                   pl.BlockSpec((1, g_, d, N_CHUNK), lambda i: (i, 0, 0, 0))],
        out_shape=[jax.ShapeDtypeStruct((b, g_, N_CHUNK, LANES), BF16),
                   jax.ShapeDtypeStruct((b, g_, d, N_CHUNK), BF16)],
        compiler_params=pltpu.CompilerParams(dimension_semantics=("parallel",)),
        name="compression",
    )(kck, kcv, *consts)


def _flash_init(m_ref, acc_ref):
    m_ref[...] = jnp.full(m_ref.shape, NEG, F32)
    acc_ref[...] = jnp.zeros(acc_ref.shape, F32)


def _flash_step(s, v, m_ref, acc_ref, chunk=KEY_CHUNK):
    for c in range(s.shape[0] // chunk):
        sc = s[c * chunk:(c + 1) * chunk]
        m_old = m_ref[c]
        m_new = jnp.maximum(m_old, jnp.max(sc, axis=0, keepdims=True))
        p = jnp.exp2(sc - m_new).astype(BF16)
        acc_ref[c] = jnp.exp2(m_old - m_new) * acc_ref[c] + _dot(v[:, c * chunk:(c + 1) * chunk], p)
        m_ref[c] = m_new


def _flash_softmax(s, m_ref, p_ref, alpha_ref, chunk=KEY_CHUNK):
    for c in range(s.shape[0] // chunk):
        sc = s[c * chunk:(c + 1) * chunk]
        m_old = m_ref[c]
        m_new = jnp.maximum(m_old, jnp.max(sc, axis=0, keepdims=True))
        alpha_ref[c] = jnp.exp2(m_old - m_new)
        p_ref[c * chunk:(c + 1) * chunk, :] = jnp.exp2(sc - m_new).astype(BF16)
        m_ref[c] = m_new


def _flash_accumulate(v, p_ref, alpha_ref, acc_ref, chunk=KEY_CHUNK):
    for c in range(p_ref.shape[0] // chunk):
        rows = slice(c * chunk, (c + 1) * chunk)
        acc_ref[c] = alpha_ref[c] * acc_ref[c] + _dot(v[:, rows], p_ref[rows, :])


def _flash_finish(m_ref, acc_ref):
    n = m_ref.shape[0]
    m = m_ref[0]
    for c in range(1, n):
        m = jnp.maximum(m, m_ref[c])
    acc = sum(jnp.exp2(m_ref[c] - m) * acc_ref[c] for c in range(n))
    return acc[:MLA_V] / acc[MLA_V:MLA_V + 1]


def _pipelined_pairs(sched_ref, row, n, scores, softmax, accumulate, unroll=PAIR_UNROLL):
    assert n >= 2 and unroll % 2 == 0

    def pair(i):
        i = jnp.minimum(i, n - 1)
        return sched_ref[row, i], sched_ref[row + 1, i]

    def steps(first, count):
        for u in range(count):
            i, slot = first + u, u % 2
            if accumulate is None:
                scores(*pair(i + 1), 1 - slot)
                softmax(*pair(i), slot)
            else:
                scores(*pair(i + 2), slot)
                softmax(*pair(i + 1), 1 - slot)
                accumulate(*pair(i), slot)

    scores(*pair(0), 0)
    if accumulate is not None:
        scores(*pair(1), 1)
        softmax(*pair(0), 0)
    loops = n // unroll
    lax.fori_loop(0, loops, lambda j, carry: (steps(j * unroll, unroll), carry)[1], 0)
    steps(loops * unroll, n - loops * unroll)


def _mla_pairs():
    nq, nk = SEQ // MLA_Q, SEQ // MLA_K
    full = [(qt, kt) for kt in range(nk) for qt in range(nq) if (kt + 1) * MLA_K <= qt * MLA_Q]
    diag = [(qt, (qt * MLA_Q) // MLA_K) for qt in range(nq)]
    return full, diag


def _mla_kernel(sched_ref, q_ref, k_ref, vt_ref, eye_ref, wo_ref, wup_ref, wdn_ref,
                o_ref, wo16_ref, wup16_ref, wdn16_ref, m_ref, acc_ref, s_ref):
    for w_ref, w16_ref in ((wo_ref, wo16_ref), (wup_ref, wup16_ref), (wdn_ref, wdn16_ref)):
        w16_ref[...] = w_ref[...].astype(BF16)
    sub = MLA_K // V_TILE
    full, diag = _mla_pairs()
    rel = (lax.broadcasted_iota(jnp.int32, (MLA_K, MLA_Q), 1)
           - lax.broadcasted_iota(jnp.int32, (MLA_K, MLA_Q), 0))
    _flash_init(m_ref, acc_ref)

    def scores(qt, kt, slot, masked):
        qrows = pl.ds(pl.multiple_of(qt * MLA_Q, MLA_Q), MLA_Q)
        krows = pl.ds(pl.multiple_of(kt * MLA_K, MLA_K), MLA_K)
        for hh in range(2):
            cols = slice(hh * LANES, (hh + 1) * LANES)
            s = _dot_nt(k_ref[krows, cols], q_ref[qrows, cols])
            if masked:
                s = jnp.where(rel >= kt * MLA_K - qt * MLA_Q, s, NEG)
            s_ref[slot, hh] = s

    def update(qt, kt, slot):
        for hh in range(2):
            v = jnp.concatenate([vt_ref[kt * sub + i, hh * V_ROWS:(hh + 1) * V_ROWS, :] for i in range(sub)],
                                axis=1)
            _flash_step(s_ref[slot, hh], v, m_ref.at[hh, qt], acc_ref.at[hh, qt])

    _pipelined_pairs(sched_ref, 0, len(full), lambda qt, kt, slot: scores(qt, kt, slot, False), update, None)
    _pipelined_pairs(sched_ref, 2, len(diag), lambda qt, kt, slot: scores(qt, kt, slot, True), update, None)

    for qt in range(SEQ // MLA_Q):
        o_t = jnp.concatenate([_flash_finish(m_ref.at[hh, qt], acc_ref.at[hh, qt])
                               for hh in range(2)], axis=0)
        o_ref[qt * MLA_Q:(qt + 1) * MLA_Q, :] = _dot_nt(eye_ref[...], o_t.astype(BF16)).astype(BF16)


def _schedule(*pair_lists):
    n = max(len(p) for p in pair_lists)
    out = np.zeros((2 * len(pair_lists), n), np.int32)
    for i, pairs in enumerate(pair_lists):
        out[2 * i, :len(pairs)] = [a for a, _ in pairs]
        out[2 * i + 1, :len(pairs)] = [b for _, b in pairs]
    return jnp.asarray(out)


def _mla_attention(q, k, vt, w_o, w_up, w_down):
    b = q.shape[0] // SEQ
    nq = SEQ // MLA_Q
    nv = SEQ // V_TILE
    chunks = MLA_K // KEY_CHUNK
    head_pairs = MLA_HEADS // 2
    steps = b * head_pairs
    weights = [w_o, w_up, w_down]
    assert all(w.shape[0] % (steps * BF16_ROWS) == 0 for w in weights)
    w_specs = [pl.BlockSpec((w.shape[0] // steps, w.shape[1]), lambda i, hp: (i * head_pairs + hp, 0))
               for w in weights]
    return pl.pallas_call(
        _mla_kernel,
        grid=(b, head_pairs),
        in_specs=[pl.BlockSpec(memory_space=pltpu.SMEM),
                  pl.BlockSpec((SEQ, 2 * LANES), lambda i, hp: (i, hp)),
                  pl.BlockSpec((SEQ, 2 * LANES), lambda i, hp: (i, hp)),
                  pl.BlockSpec((nv, 2 * V_ROWS, V_TILE), lambda i, hp: (i, hp, 0)),
                  pl.BlockSpec((MLA_Q, MLA_Q), lambda i, hp: (0, 0))] + w_specs,
        out_specs=[pl.BlockSpec((SEQ, LANES), lambda i, hp: (i, hp))] + w_specs,
        out_shape=[jax.ShapeDtypeStruct((q.shape[0], MLA_HEADS * MLA_V), BF16)]
                  + [jax.ShapeDtypeStruct(w.shape, BF16) for w in weights],
        scratch_shapes=[pltpu.VMEM((2, nq, chunks, 1, MLA_Q), F32),
                        pltpu.VMEM((2, nq, chunks, V_ROWS, MLA_Q), F32),
                        pltpu.VMEM((2, 2, MLA_K, MLA_Q), F32)],
        compiler_params=pltpu.CompilerParams(
            dimension_semantics=("parallel", "parallel"), vmem_limit_bytes=VMEM_LIMIT),
        name="mla_attention",
    )(_schedule(*_mla_pairs()), q, k, vt, jnp.eye(MLA_Q, dtype=BF16), *weights)


def _select_kernel(q_ref, kc_ref, vct_ref, tabc_ref, ovt_ref, eye_ref, qpad_ref, qaug_ref, ocmp_ref):
    step = pl.program_id(1)
    t = SELECT_Q
    heads = range(NSA_GROUP)
    tiles = range(t // NSA_Q)

    def group(g):
        q_pad = _dot(q_ref[:, g * 2 * LANES:(g + 1) * 2 * LANES], qpad_ref[...]).astype(BF16)
        low = lax.broadcasted_iota(jnp.int32, (t, LANES), 1) < NSA_HEAD_DIM
        zero = jnp.zeros((t, LANES), BF16)
        pairs = [q_ref[:, (2 * g + j) * LANES:(2 * g + j + 1) * LANES] for j in range(NSA_GROUP // 2)]
        q4 = jnp.concatenate([jnp.where(low == (r % 2 == 0), pairs[r // 2], zero) for r in heads], axis=0)

        per_tile = NSA_Q // CMP_STRIDE
        rows = [pl.ds(pl.multiple_of(CMP_TAB_BASE - (step * len(tiles) + u) * per_tile, per_tile), N_CHUNK)
                for u in tiles]
        bias_c = jnp.concatenate([tabc_ref[g * NSA_GROUP + r, rows[u], :] for r in heads for u in tiles], axis=1)
        valid = bias_c > 0.5 * NEG
        sc = jnp.where(valid, _dot_nt(kc_ref[0, g], q4) + bias_c, NEG)
        mx = jnp.max(sc, axis=0, keepdims=True)
        p = jnp.where(valid, jnp.exp2(sc - mx), 0.0)
        pc = p / jnp.maximum(jnp.sum(p, axis=0, keepdims=True), 1e-30)
        o_cmp = _dot(vct_ref[0, g], pc.astype(BF16)).astype(BF16)
        for u in tiles:
            ocmp_ref[0, g, u] = jnp.concatenate(
                [o_cmp[:, r * t + u * NSA_Q:r * t + (u + 1) * NSA_Q] for r in heads], axis=0)

        psum = sum(pc[:, r * t:(r + 1) * t] for r in heads)
        imp = sum(_dot(ovt_ref[...], piece) for piece in _split(psum, 3))
        jj = lax.broadcasted_iota(jnp.int32, (N_SEL, t), 0)
        blk = (step * t + lax.broadcasted_iota(jnp.int32, (N_SEL, t), 1)) // SEL_BLOCK
        ok = jj <= blk
        forced = ok & ((jj == 0) | (jj == blk) | (jj == blk - 1))
        score = jnp.where(forced, FORCE_SCORE, jnp.where(ok, imp, -jnp.inf))
        rank = jnp.zeros((N_SEL, t), jnp.int32)
        for i in range(N_SEL):
            ci = score[i:i + 1, :]
            beats = (ci > score) | ((ci == score) & (jj > i))
            rank = rank + beats.astype(jnp.int32)
        negsel = jnp.where(ok & (rank < SEL_TOP_N), 0.0, NEG).astype(BF16)
        spread = jnp.concatenate([jnp.zeros((AUG_BLOCK, t), BF16), negsel,
                                  jnp.zeros((LANES - AUG_ONE, t), BF16)], axis=0)
        mask_lanes = _dot_nt(eye_ref[...], spread).astype(BF16)
        lane = lax.broadcasted_iota(jnp.int32, (t, LANES), 1)
        for r in heads:
            far = tabc_ref[g * NSA_GROUP + r, 0:1, :]
            far_hi = far.astype(BF16)
            far_lo = (far - far_hi.astype(F32)).astype(BF16)
            aug = jnp.where(lane == AUG_ONE, far_hi, jnp.where(lane == AUG_ONE + 1, far_lo, mask_lanes))
            q_aug = jnp.where(lane < AUG_BLOCK, q_pad[:, r * LANES:(r + 1) * LANES], aug)
            for u in tiles:
                qaug_ref[0, g, u, r * NSA_Q:(r + 1) * NSA_Q, :] = q_aug[u * NSA_Q:(u + 1) * NSA_Q]

    for g in range(NSA_KV_HEADS):
        group(g)


def _nsa_select(q, kc, vct, tables):
    b = q.shape[0] // SEQ
    t = SELECT_Q
    steps = SEQ // t
    nq = SEQ // NSA_Q
    per = t // NSA_Q
    g_ = NSA_KV_HEADS
    c0 = np.arange(N_CHUNK)[None, :] * CMP_STRIDE
    j0 = np.arange(N_SEL)[:, None] * SEL_BLOCK
    ovt = np.clip(np.minimum(c0 + CMP_BLOCK, j0 + SEL_BLOCK) - np.maximum(c0, j0), 0, None) / CMP_BLOCK
    ovt[:, N_CHUNK - 1:] = 0.0
    return pl.pallas_call(
        _select_kernel,
        grid=(b, steps),
        in_specs=[pl.BlockSpec((t, g_ * 2 * LANES), lambda i, j: (i * steps + j, 0)),
                  pl.BlockSpec((1, g_, N_CHUNK, LANES), lambda i, j: (i, 0, 0, 0)),
                  pl.BlockSpec((1, g_, NSA_HEAD_DIM, N_CHUNK), lambda i, j: (i, 0, 0, 0)),
                  pl.BlockSpec((NSA_HEADS, CMP_TAB_ROWS, LANES), lambda i, j: (0, WIN_ROWS // CMP_TAB_ROWS, 0)),
                  pl.BlockSpec((N_SEL, N_CHUNK), lambda i, j: (0, 0)),
                  pl.BlockSpec((t, t), lambda i, j: (0, 0)),
                  pl.BlockSpec((2 * LANES, NSA_GROUP * LANES), lambda i, j: (0, 0))],
        out_specs=[pl.BlockSpec((1, g_, per, NSA_GROUP * NSA_Q, LANES), lambda i, j: (i, 0, j, 0, 0)),
                   pl.BlockSpec((1, g_, per, NSA_GROUP * NSA_HEAD_DIM, NSA_Q), lambda i, j: (i, 0, j, 0, 0))],
        out_shape=[jax.ShapeDtypeStruct((b, g_, nq, NSA_GROUP * NSA_Q, LANES), BF16),
                   jax.ShapeDtypeStruct((b, g_, nq, NSA_GROUP * NSA_HEAD_DIM, NSA_Q), BF16)],
        compiler_params=pltpu.CompilerParams(
            dimension_semantics=("parallel", "parallel"), vmem_limit_bytes=VMEM_LIMIT),
        name="nsa_select",
    )(q, kc, vct, tables, jnp.asarray(ovt, BF16), jnp.eye(t, dtype=BF16),
      jnp.asarray(_head_padding(NSA_GROUP), BF16))


def _nsa_pairs():
    nq, nk, per = SEQ // NSA_Q, SEQ // NSA_K, NSA_K // NSA_Q
    sel = [(qt, kt) for kt in range(nk) for qt in range(nq) if kt * NSA_K <= qt * NSA_Q]
    far = [(qt, kt) for qt, kt in sel if qt - kt * per >= SEL_FAR_TILE]
    near = [(qt, kt) for qt, kt in sel if qt - kt * per < SEL_FAR_TILE]
    win = [(qt, kt) for kt in range(nk) for qt in range(nq)
           if max(qt * NSA_Q - (WINDOW - 1), 0) // NSA_K <= kt and kt * NSA_K <= qt * NSA_Q]
    return far, near, win


def _attend_kernel(sched_ref, qaug_ref, ksel_ref, kwin_ref, vsel_ref, vwin_ref, ocmp_ref, gate_ref,
                   tabw_ref, eye_ref, o_ref, m_ref, acc_ref, s_ref, p_ref, alpha_ref):
    t = NSA_Q
    heads = range(NSA_GROUP)
    far, near, win = _nsa_pairs()
    _flash_init(m_ref, acc_ref)

    def scores(qt, kt, slot, k_ref, add_bias):
        s = _dot_nt(k_ref[pl.ds(pl.multiple_of(kt * NSA_K, NSA_K), NSA_K), :], qaug_ref[0, 0, qt])
        if add_bias:
            off = qt - kt * (NSA_K // NSA_Q)
            bias_rows = pl.ds(pl.multiple_of(off * NSA_K, NSA_K), NSA_K)
            s = s + jnp.concatenate([tabw_ref[r, bias_rows, :] for r in heads], axis=1)
        s_ref[slot] = s

    def softmax(qt, kt, slot, branch):
        _flash_softmax(s_ref[slot], m_ref.at[branch, qt], p_ref.at[slot], alpha_ref.at[slot], chunk=NSA_K)

    def accumulate(qt, kt, slot, v_ref, branch):
        _flash_accumulate(v_ref[kt], p_ref.at[slot], alpha_ref.at[slot], acc_ref.at[branch, qt], chunk=NSA_K)

    for row, pairs, k_ref, v_ref, branch, add_bias in ((0, far, ksel_ref, vsel_ref, 0, False),
                                                       (2, near, ksel_ref, vsel_ref, 0, True),
                                                       (4, win, kwin_ref, vwin_ref, 1, True)):
        _pipelined_pairs(sched_ref, row, len(pairs),
                         lambda qt, kt, slot, k_ref=k_ref, add_bias=add_bias: scores(qt, kt, slot, k_ref, add_bias),
                         lambda qt, kt, slot, branch=branch: softmax(qt, kt, slot, branch),
                         lambda qt, kt, slot, v_ref=v_ref, branch=branch: accumulate(qt, kt, slot, v_ref, branch),
                         unroll=NSA_UNROLL)

    d = NSA_HEAD_DIM
    group = pl.program_id(1)
    for qt in range(SEQ // t):
        o_cmp = ocmp_ref[0, 0, qt].astype(F32)
        o_sel, o_win = [_flash_finish(m_ref.at[br, qt], acc_ref.at[br, qt]) for br in range(2)]
        mixed = []
        for r in heads:
            gate = [gate_ref[qt, pl.ds(br * NSA_HEADS + group * NSA_GROUP + r, 1), :]
                    for br in range(N_BRANCH)]
            mixed.append(gate[0] * o_cmp[r * d:(r + 1) * d] + gate[1] * o_sel[:, r * t:(r + 1) * t]
                         + gate[2] * o_win[:, r * t:(r + 1) * t])
        o_ref[qt * t:(qt + 1) * t, :] = _dot_nt(eye_ref[...], jnp.concatenate(mixed, axis=0).astype(BF16)).astype(BF16)


def _nsa_attend(qaug, kaug, vt, ocmp, gates, tables):
    b = kaug.shape[0] // SEQ
    t = NSA_Q
    nq = SEQ // t
    nv = SEQ // V_TILE
    g_ = NSA_KV_HEADS
    k_spec = lambda off: pl.BlockSpec((SEQ, LANES), lambda i, g: (i, off + g))
    v_spec = lambda off: pl.BlockSpec((nv, V_ROWS, V_TILE), lambda i, g: (i, off + g, 0))
    return pl.pallas_call(
        _attend_kernel,
        grid=(b, g_),
        in_specs=[pl.BlockSpec(memory_space=pltpu.SMEM),
                  pl.BlockSpec((1, 1, nq, NSA_GROUP * t, LANES), lambda i, g: (i, g, 0, 0, 0)),
                  k_spec(0), k_spec(2), v_spec(0), v_spec(2),
                  pl.BlockSpec((1, 1, nq, NSA_GROUP * NSA_HEAD_DIM, t), lambda i, g: (i, g, 0, 0, 0)),
                  pl.BlockSpec((nq, LANES, t), lambda i, g: (i, 0, 0)),
                  pl.BlockSpec((NSA_GROUP, WIN_ROWS, LANES), lambda i, g: (g, 0, 0)),
                  pl.BlockSpec((t, t), lambda i, g: (0, 0))],
        out_specs=pl.BlockSpec((SEQ, 2 * LANES), lambda i, g: (i, g)),
        out_shape=jax.ShapeDtypeStruct((kaug.shape[0], NSA_HEADS * NSA_HEAD_DIM), BF16),
        scratch_shapes=[pltpu.VMEM((2, nq, 1, 1, NSA_GROUP * t), F32),
                        pltpu.VMEM((2, nq, 1, V_ROWS, NSA_GROUP * t), F32),
                        pltpu.VMEM((2, NSA_K, NSA_GROUP * t), F32),
                        pltpu.VMEM((2, NSA_K, NSA_GROUP * t), BF16),
                        pltpu.VMEM((2, 1, 1, NSA_GROUP * t), F32)],
        compiler_params=pltpu.CompilerParams(
            dimension_semantics=("parallel", "parallel"), vmem_limit_bytes=VMEM_LIMIT),
        name="nsa_attend",
    )(_schedule(*_nsa_pairs()), qaug, kaug, kaug, vt, vt, ocmp, gates, tables, jnp.eye(t, dtype=BF16))


def _tail_kernel(x_ref, oa_ref, ob_ref, ga_ref, gb_ref, wo_ref, ln2_ref, wup_ref, wdn_ref,
                 out_ref, h2_ref):
    width = MLA_HEADS * MLA_V
    a = (_rms(oa_ref[...].astype(F32), width) * ga_ref[...]).astype(BF16)
    b = (_rms(ob_ref[...].astype(F32), width) * gb_ref[...]).astype(BF16)
    x1 = x_ref[...] + _dot(a, wo_ref[:width]) + _dot(b, wo_ref[width:])
    h2_ref[...] = (_rms(x1, D_MODEL) * ln2_ref[...]).astype(BF16)
    out_ref[...] = x1

    def body(f, carry):
        cols = pl.ds(pl.multiple_of(f * FF_CHUNK, FF_CHUNK), FF_CHUNK)
        u = jnp.maximum(_dot(h2_ref[...], wup_ref[:, cols]), 0.0)
        out_ref[...] += _dot((u * u).astype(BF16), wdn_ref[f])
        return carry

    lax.fori_loop(0, D_FF // FF_CHUNK, body, 0)


def _tail(x2d, o_a, o_b, gn_a, gn_b, w_o, ln2_g, w_up, w_down):
    t = x2d.shape[0]
    rows = TAIL_ROWS
    width = MLA_HEADS * MLA_V
    once = lambda a: pl.BlockSpec(a.shape, lambda i: (0,) * a.ndim)
    assert w_o.dtype == w_up.dtype == w_down.dtype == BF16
    consts = [gn_a[None, :], gn_b[None, :], w_o, ln2_g[None, :], w_up,
              w_down.reshape(D_FF // FF_CHUNK, FF_CHUNK, D_MODEL)]
    row_spec = lambda w: pl.BlockSpec((rows, w), lambda i: (i, 0))
    return pl.pallas_call(
        _tail_kernel,
        grid=(t // rows,),
        in_specs=[row_spec(D_MODEL), row_spec(width), row_spec(width)] + [once(a) for a in consts],
        out_specs=row_spec(D_MODEL),
        out_shape=jax.ShapeDtypeStruct((t, D_MODEL), F32),
        scratch_shapes=[pltpu.VMEM((rows, D_MODEL), BF16)],
        compiler_params=pltpu.CompilerParams(dimension_semantics=("parallel",),
                                             vmem_limit_bytes=VMEM_LIMIT),
        name="tail",
    )(x2d, o_a, o_b, *consts)


def kernel(x, ln1_g, w_in, mla_cq_norm_g, mla_ckv_norm_g, mla_w_uq, mla_w_ukv, mla_q_gain, mla_k_gain,
           nsa_q_gain, nsa_k_gain, nsa_cmp_pe, nsa_cmp_w1, nsa_cmp_w2, rel_bias, grp_norm_mla,
           grp_norm_nsa, w_o, ln2_g, w_up, w_down):
    b, s, d = x.shape
    assert (s, d) == (SEQ, D_MODEL) and ln1_g.shape[0] == 1
    x2d = x.reshape(b * s, d)
    tables = _bias_tables(rel_bias)
    q_mla, k_mla, vt_mla, q_nsa, kdup, vt_nsa, kck, kcv, gates = _projections(
        x2d, ln1_g[0], w_in[0], mla_cq_norm_g[0], mla_ckv_norm_g[0], mla_w_uq[0], mla_w_ukv[0],
        mla_q_gain[0], mla_k_gain[0], nsa_q_gain[0], nsa_k_gain[0])
    kc, vct = _compress(kck, kcv, nsa_cmp_pe[0], nsa_cmp_w1[0], nsa_cmp_w2[0], nsa_k_gain[0, 0])
    o_a, w_o16, w_up16, w_down16 = _mla_attention(q_mla, k_mla, vt_mla, w_o[0], w_up[0], w_down[0])
    q_aug, ocmp = _nsa_select(q_nsa, kc, vct, tables)
    o_b = _nsa_attend(q_aug, kdup, vt_nsa, ocmp, gates, tables)
    out = _tail(x2d, o_a, o_b, grp_norm_mla[0], grp_norm_nsa[0], w_o16, ln2_g[0], w_up16, w_down16)
    return out.reshape(b, s, d)
```

```python
import math

import numpy as np
import jax
import jax.numpy as jnp
from jax import lax
from jax.experimental import pallas as pl
from jax.experimental.pallas import tpu as pltpu

F32 = jnp.float32
BF16 = jnp.bfloat16

D_MODEL = 1024
SEQ = 2048
MLA_HEADS = 8
MLA_NOPE = 64
MLA_ROPE = 32
MLA_V = 64
MLA_QK = MLA_NOPE + MLA_ROPE
MLA_Q_RANK = 384
MLA_KV_RANK = 256
ROPE_THETA = 10000.0
NSA_HEADS = 8
NSA_KV_HEADS = 2
NSA_GROUP = NSA_HEADS // NSA_KV_HEADS
NSA_HEAD_DIM = 64
N_BRANCH = 3
CMP_BLOCK = 32
CMP_STRIDE = 16
CMP_HIDDEN = 128
SEL_BLOCK = 64
SEL_TOP_N = 16
WINDOW = 512
FORCE_SCORE = 1e4
REL_BUCKETS = 32
REL_MAX_DIST = 128
D_FF = 4 * D_MODEL
EPS = 1e-6

LANES = 128
SUBLANES = 8
BF16_ROWS = 16
NEG = -1e30
N_CHUNK = SEQ // CMP_STRIDE
N_SEL = SEQ // SEL_BLOCK
NSA_Q = 128
NSA_K = 256
SELECT_Q = 512
N_BIAS_TILES = (WINDOW + NSA_K - NSA_Q) // NSA_Q + 1
SEL_FAR_TILE = (REL_MAX_DIST + NSA_K) // NSA_Q
WIN_ROWS = N_BIAS_TILES * NSA_K
AUG_BLOCK = NSA_HEAD_DIM
AUG_ONE = AUG_BLOCK + N_SEL
CMP_TAB_BASE = (SEQ // NSA_Q - 1) * (NSA_Q // CMP_STRIDE)
CMP_TAB_ROWS = 256
MLA_Q = 256
MLA_K = 512
V_TILE = 256
PROJ_ROWS = 256
PACK_ROWS = 256
TAIL_ROWS = 512
FF_CHUNK = 2048
KEY_CHUNK = 256
PAIR_UNROLL = 6
NSA_UNROLL = 12
V_ROWS = 80
LOG2E = math.log2(math.e)
VMEM_LIMIT = 56 * 1024 * 1024

assert PROJ_ROWS % V_TILE == 0 and MLA_K % V_TILE == 0 and NSA_K == V_TILE


def _dot(a, b):
    return jnp.dot(a, b, preferred_element_type=F32)


def _dot_nt(a, b):
    return lax.dot_general(a, b, (((1,), (1,)), ((), ())), preferred_element_type=F32)


def _split(a, terms):
    pieces = []
    rem = a
    for _ in range(terms):
        piece = rem.astype(BF16)
        pieces.append(piece)
        rem = rem - piece.astype(F32)
    return pieces


def _split_dot(a, b, terms=2):
    return sum(_dot(p, b) for p in _split(a, terms))


def _rms(x, width):
    return x * lax.rsqrt(jnp.sum(x * x, axis=-1, keepdims=True) * (1.0 / width) + EPS)


def _t5_bucket_np(dist):
    n = np.maximum(dist, 0)
    max_exact = REL_BUCKETS // 2
    large = max_exact + (np.log(np.maximum(n, 1).astype(np.float32) / max_exact)
                         / math.log(REL_MAX_DIST / max_exact)
                         * (REL_BUCKETS - max_exact)).astype(np.int32)
    large = np.minimum(large, REL_BUCKETS - 1)
    return np.where(n < max_exact, n, large).astype(np.int32)


def _bias_index_table():
    i = np.arange(NSA_Q)[None, :]
    parts = []
    j = np.arange(NSA_K)[:, None]
    for off in range(N_BIAS_TILES):
        d = off * NSA_Q + i - j
        parts.append(np.where((d >= 0) & (d < WINDOW), _t5_bucket_np(d), -1))
    r = np.arange(CMP_TAB_ROWS)[:, None]
    dist_c = (CMP_TAB_BASE - r) * CMP_STRIDE + i - (CMP_BLOCK - 1)
    parts.append(np.where(dist_c >= 0, _t5_bucket_np(dist_c), -1))
    return np.concatenate(parts, axis=0).astype(np.int32)


def _bias_table_kernel(rb_ref, idx_ref, out_ref):
    idx = idx_ref[...]
    bucket = jnp.maximum(idx, 0)
    is_window = pl.program_id(0) < WIN_ROWS // idx.shape[0]
    for h in range(NSA_HEADS):
        row = rb_ref[h:h + 1, :]
        far = jnp.where(is_window, row[:, REL_BUCKETS - 1:REL_BUCKETS], 0.0)
        values = jnp.broadcast_to((row - far) * LOG2E, idx.shape)
        picked = jnp.take_along_axis(values, bucket, axis=1, mode="promise_in_bounds")
        out_ref[h] = jnp.where(idx >= 0, picked, NEG)


def _bias_tables(rel_bias):
    idx = jnp.asarray(_bias_index_table())
    rows = idx.shape[0]
    blk = CMP_TAB_ROWS
    return pl.pallas_call(
        _bias_table_kernel,
        grid=(rows // blk,),
        in_specs=[pl.BlockSpec((NSA_HEADS, LANES), lambda r: (0, 0)),
                  pl.BlockSpec((blk, LANES), lambda r: (r, 0))],
        out_specs=pl.BlockSpec((NSA_HEADS, blk, LANES), lambda r: (0, r, 0)),
        out_shape=jax.ShapeDtypeStruct((NSA_HEADS, rows, LANES), F32),
        name="bias_tables",
    )(jnp.pad(rel_bias.T, ((0, 0), (0, LANES - REL_BUCKETS))), idx)


ROPE_HALF = MLA_ROPE // 2
MISC_KR = 64


def _rope_tables(scale):
    inv = 1.0 / (ROPE_THETA ** (np.arange(ROPE_HALF, dtype=np.float64) / ROPE_HALF))
    ang = np.arange(SEQ, dtype=np.float64)[:, None] * inv[None, :]
    cos, sin = np.cos(ang), np.sin(ang)
    ones = np.ones((SEQ, MLA_NOPE))
    z32 = np.zeros((SEQ, LANES - MLA_QK))
    z64 = np.zeros((SEQ, MLA_NOPE))
    c = np.concatenate([ones, cos, cos, z32], axis=1) * scale
    s = np.concatenate([z64, -sin, sin, z32], axis=1) * scale
    return jnp.asarray(c, F32), jnp.asarray(s, F32)


def _pair_norm(x, gain_ref):
    low = lax.broadcasted_iota(jnp.int32, (x.shape[0], LANES), 1) < NSA_HEAD_DIM
    outs = []
    for j in range(x.shape[1] // LANES):
        cols = slice(j * LANES, (j + 1) * LANES)
        sl = x[:, cols]
        sq = sl * sl
        tot = jnp.sum(sq, axis=-1, keepdims=True)
        lo = jnp.sum(jnp.where(low, sq, 0.0), axis=-1, keepdims=True)
        rs = jnp.where(low, lax.rsqrt(lo * (1.0 / NSA_HEAD_DIM) + EPS),
                       lax.rsqrt((tot - lo) * (1.0 / NSA_HEAD_DIM) + EPS))
        outs.append(sl * rs * gain_ref[:, cols])
    return jnp.concatenate(outs, axis=1)


def _mla_query_heads(x, gain_ref, c, s):
    in_head = lax.broadcasted_iota(jnp.int32, (x.shape[0], LANES), 1) < MLA_QK
    outs = []
    for h in range(MLA_HEADS):
        cols = slice(h * LANES, (h + 1) * LANES)
        sl = x[:, cols]
        ss = jnp.sum(jnp.where(in_head, sl * sl, 0.0), axis=-1, keepdims=True)
        xn = sl * lax.rsqrt(ss * (1.0 / MLA_QK) + EPS) * gain_ref[:, cols]
        outs.append(xn * c + pltpu.roll(xn, LANES - ROPE_HALF, 1) * s)
    return jnp.concatenate(outs, axis=1)


def _mla_key_heads(x, k_rope, k_swap, c, s):
    in_head = lax.broadcasted_iota(jnp.int32, (x.shape[0], LANES), 1) < MLA_QK
    turned = k_swap * s
    outs = []
    for h in range(MLA_HEADS):
        sl = x[:, h * LANES:(h + 1) * LANES] + k_rope
        ss = jnp.sum(jnp.where(in_head, sl * sl, 0.0), axis=-1, keepdims=True)
        outs.append(lax.rsqrt(ss * (1.0 / MLA_QK) + EPS) * (sl * c + turned))
    return jnp.concatenate(outs, axis=1)


COL_KV = MLA_Q_RANK
COL_MISC = COL_KV + MLA_KV_RANK
COL_SWAP = COL_MISC + LANES
COL_NQ = COL_SWAP + LANES
COL_NK = COL_NQ + NSA_HEADS * NSA_HEAD_DIM
COL_NV = COL_NK + 4 * NSA_HEAD_DIM
COL_CMP = COL_NV + 4 * NSA_HEAD_DIM
COL_END = COL_CMP + 4 * NSA_HEAD_DIM


def _proj_kernel(x_ref, ln1_ref, wp_ref, cqg_ref, ckvg_ref, wuq_ref, wukk_ref, wukvt_ref,
                 qg_ref, kown_ref, kpartner_ref, qc_ref, qs_ref, kc_ref, ks_ref,
                 nq_gain_ref, nk_gain_ref, kpad_ref, kaug_ref, vpick_ref, ones_mla_ref, ones_nsa_ref, eye_ref,
                 qmla_ref, kmla_ref, vmla_ref, qnsa_ref, kdup_ref, vnsa_ref, kck_ref, kcv_ref, gate_ref):
    for sub in range(PROJ_ROWS // V_TILE):
        rows = slice(sub * V_TILE, (sub + 1) * V_TILE)
        x = x_ref[rows, :]
        h = (_rms(x, D_MODEL) * ln1_ref[...]).astype(BF16)
        part = lambda a, b: _dot(h, wp_ref[:, a:b])

        cq = (_rms(part(0, COL_KV), MLA_Q_RANK) * cqg_ref[...]).astype(BF16)
        qmla_ref[rows, :] = _mla_query_heads(_dot(cq, wuq_ref[...]), qg_ref,
                                             qc_ref[rows, :], qs_ref[rows, :]).astype(BF16)

        kv_misc = part(COL_KV, COL_NQ)
        ckv = (_rms(kv_misc[:, :MLA_KV_RANK], MLA_KV_RANK) * ckvg_ref[...]).astype(BF16)
        misc = kv_misc[:, COL_MISC - COL_KV:COL_SWAP - COL_KV]
        lane = lax.broadcasted_iota(jnp.int32, misc.shape, 1)
        k_rope = jnp.where((lane >= MISC_KR) & (lane < MISC_KR + MLA_ROPE), misc, 0.0)
        kmla_ref[rows, :] = _mla_key_heads(_dot(ckv, wukk_ref[...]), k_rope, kv_misc[:, COL_SWAP - COL_KV:],
                                           kc_ref[rows, :] * kown_ref[...],
                                           ks_ref[rows, :] * kpartner_ref[...]).astype(BF16)
        vmla_ref[sub] = (_dot_nt(wukvt_ref[...], ckv) + ones_mla_ref[...]).astype(BF16)

        qnsa_ref[rows, :] = _pair_norm(part(COL_NQ, COL_NK), nq_gain_ref).astype(BF16)

        kn = _pair_norm(part(COL_NK, COL_NV), nk_gain_ref)
        kdup_ref[rows, :] = (_dot(kn.astype(BF16), kpad_ref[...]) + kaug_ref[rows, :]).astype(BF16)
        vnsa_ref[sub] = (_dot_nt(vpick_ref[...], part(COL_NV, COL_CMP).astype(BF16)) + ones_nsa_ref[...]).astype(BF16)
        cmp_kv = part(COL_CMP, COL_END)
        kck_ref[rows, :] = cmp_kv[:, :LANES]
        kcv_ref[rows, :] = cmp_kv[:, LANES:]
        gate = 1.0 / (1.0 + jnp.exp(-misc))
        gate_t = sum(_dot_nt(eye_ref[...], piece) for piece in _split(gate, 2))
        for u in range(V_TILE // NSA_Q):
            gate_ref[sub * (V_TILE // NSA_Q) + u] = gate_t[:, u * NSA_Q:(u + 1) * NSA_Q]


def _head_padding(heads):
    d = NSA_HEAD_DIM
    m = np.zeros((heads * d, heads * LANES), np.float32)
    for hh in range(heads):
        m[hh * d + np.arange(d), hh * LANES + np.arange(d)] = 1.0
    return m


def _packed_segments():
    kv0 = 1184
    seg = lambda a, b: (a, b - a)
    kvb = lambda br, kv, g: (kv0 + ((br * 2 + kv) * 2 + g) * 64, 64)
    zeros = lambda n: (None, n)
    kr = COL_MISC
    gate_cols = N_BRANCH * NSA_HEADS
    pieces = [
        seg(0, COL_MISC),
        seg(1952, 1952 + gate_cols), zeros(MISC_KR - gate_cols), seg(kr, kr + MLA_ROPE),
        zeros(LANES - MISC_KR - MLA_ROPE),
        zeros(MISC_KR), seg(kr + ROPE_HALF, kr + MLA_ROPE), seg(kr, kr + ROPE_HALF), zeros(LANES - MISC_KR - MLA_ROPE),
        seg(672, 1184),
        kvb(1, 0, 0), kvb(1, 0, 1), kvb(2, 0, 0), kvb(2, 0, 1),
        kvb(1, 1, 0), kvb(1, 1, 1), kvb(2, 1, 0), kvb(2, 1, 1),
        kvb(0, 0, 0), kvb(0, 0, 1), kvb(0, 1, 0), kvb(0, 1, 1),
    ]
    assert sum(n for _, n in pieces) == COL_END
    return pieces


def _pack_kernel(wt_ref, out_ref, packed_ref):
    dst = 0
    for src, n in _packed_segments():
        assert n % SUBLANES == 0 and dst % SUBLANES == 0 and (src is None or src % SUBLANES == 0)
        if src is None:
            packed_ref[dst:dst + n, :] = jnp.zeros((n, packed_ref.shape[1]), F32)
        else:
            packed_ref[dst:dst + n, :] = wt_ref[src:src + n, :]
        dst += n
    for g in range(COL_END // LANES):
        cols = slice(g * LANES, (g + 1) * LANES)
        out_ref[:, cols] = packed_ref[cols, :].T.astype(BF16)


def _pack_input_projection(w_in):
    rows = PACK_ROWS
    return pl.pallas_call(
        _pack_kernel,
        grid=(D_MODEL // rows,),
        in_specs=[pl.BlockSpec((w_in.shape[1], rows), lambda i: (0, i))],
        out_specs=pl.BlockSpec((rows, COL_END), lambda i: (i, 0)),
        out_shape=jax.ShapeDtypeStruct((D_MODEL, COL_END), BF16),
        scratch_shapes=[pltpu.VMEM((COL_END, rows), F32)],
        compiler_params=pltpu.CompilerParams(dimension_semantics=("parallel",),
                                             vmem_limit_bytes=VMEM_LIMIT),
        name="pack_input_projection",
    )(w_in.T)


def _projections(x2d, ln1_g, w_in, cq_g, ckv_g, w_uq, w_ukv, q_gain, k_gain, nq_gain, nk_gain):
    t = x2d.shape[0]
    wp = _pack_input_projection(w_in)
    wq3 = w_uq.reshape(MLA_Q_RANK, MLA_HEADS, MLA_QK).astype(BF16)
    wuq = jnp.concatenate([wq3, wq3[:, :, MLA_NOPE:MLA_NOPE + ROPE_HALF],
                           jnp.zeros((MLA_Q_RANK, MLA_HEADS, ROPE_HALF), BF16)], axis=2)
    wuq = wuq.reshape(MLA_Q_RANK, MLA_HEADS * LANES)
    wkv = w_ukv.reshape(MLA_KV_RANK, MLA_HEADS, MLA_NOPE + MLA_V)
    wukk = jnp.pad(wkv[:, :, :MLA_NOPE], ((0, 0), (0, 0), (0, LANES - MLA_NOPE)))
    wukk = wukk.reshape(MLA_KV_RANK, MLA_HEADS * LANES).astype(BF16)
    wukvt = jnp.pad(wkv[:, :, MLA_NOPE:], ((0, 0), (0, 0), (0, V_ROWS - MLA_V)))
    wukvt = wukvt.reshape(MLA_KV_RANK, MLA_HEADS * V_ROWS).T.astype(BF16)
    q_slab_gain = jnp.tile(jnp.concatenate(
        [q_gain, q_gain[MLA_NOPE:MLA_NOPE + ROPE_HALF], jnp.zeros((ROPE_HALF,), F32)]), MLA_HEADS)[None, :]
    qc, qs = _rope_tables(MLA_QK ** -0.5 * LOG2E)
    kc, ks = _rope_tables(1.0)
    k_own = jnp.pad(k_gain, (0, LANES - MLA_QK))[None, :]
    k_partner = jnp.concatenate([k_gain[:MLA_NOPE], k_gain[MLA_NOPE + ROPE_HALF:],
                                 k_gain[MLA_NOPE:MLA_NOPE + ROPE_HALF], jnp.zeros((LANES - MLA_QK,), F32)])[None, :]
    kpad = _head_padding(4)
    kaug = np.zeros((SEQ, 4 * LANES), np.float32)
    for s in range(4):
        kaug[:, s * LANES + AUG_ONE:s * LANES + AUG_ONE + 2] = 1.0
        if s < NSA_KV_HEADS:
            kaug[np.arange(SEQ), s * LANES + AUG_BLOCK + np.arange(SEQ) // SEL_BLOCK] = 1.0
    nqg = (jnp.tile(nq_gain, NSA_HEADS) * (NSA_HEAD_DIM ** -0.5 * LOG2E))[None, :]
    nkg = jnp.concatenate([nk_gain[1], nk_gain[1], nk_gain[2], nk_gain[2]])[None, :]
    vpick = np.zeros((4 * V_ROWS, 256), np.float32)
    for s in range(4):
        vpick[s * V_ROWS + np.arange(64), s * 64 + np.arange(64)] = 1.0

    def ones_col(heads):
        col = np.zeros((heads * V_ROWS, 1), np.float32)
        col[np.arange(heads) * V_ROWS + MLA_V] = 1.0
        return col

    rows = PROJ_ROWS
    n_pos = SEQ // rows
    full = lambda a: pl.BlockSpec(a.shape, lambda i: (0,) * a.ndim)
    tab = pl.BlockSpec((rows, LANES), lambda i: (i % n_pos, 0))
    consts = [ln1_g[None, :], wp, cq_g[None, :], ckv_g[None, :], wuq, wukk, wukvt, q_slab_gain, k_own, k_partner]
    tabs = [qc, qs, kc, ks]
    mats = [nqg, nkg, jnp.asarray(kpad, BF16),
            jnp.asarray(vpick, BF16), jnp.asarray(ones_col(MLA_HEADS)), jnp.asarray(ones_col(4)),
            jnp.eye(LANES, dtype=BF16)]
    row_spec = lambda w: pl.BlockSpec((rows, w), lambda i: (i, 0))
    vt_spec = lambda w, tile: pl.BlockSpec((rows // tile, w, tile), lambda i: (i, 0, 0))
    sds = jax.ShapeDtypeStruct
    return pl.pallas_call(
        _proj_kernel,
        grid=(t // rows,),
        in_specs=([row_spec(D_MODEL)] + [full(a) for a in consts] + [tab] * 4 + [full(a) for a in mats[:3]]
                  + [pl.BlockSpec((rows, 4 * LANES), lambda i: (i % n_pos, 0))] + [full(a) for a in mats[3:]]),
        out_specs=[row_spec(1024), row_spec(1024), vt_spec(MLA_HEADS * V_ROWS, V_TILE), row_spec(512),
                   row_spec(512), vt_spec(4 * V_ROWS, V_TILE), row_spec(128), row_spec(128),
                   vt_spec(LANES, NSA_Q)],
        out_shape=[sds((t, 1024), BF16), sds((t, 1024), BF16), sds((t // V_TILE, MLA_HEADS * V_ROWS, V_TILE), BF16),
                   sds((t, 512), BF16), sds((t, 512), BF16), sds((t // V_TILE, 4 * V_ROWS, V_TILE), BF16),
                   sds((t, 128), F32), sds((t, 128), F32), sds((t // NSA_Q, LANES, NSA_Q), F32)],
        compiler_params=pltpu.CompilerParams(dimension_semantics=("parallel",),
                                             vmem_limit_bytes=VMEM_LIMIT),
        name="projections",
    )(x2d, *consts, *tabs, *mats[:3], jnp.asarray(kaug, BF16), *mats[3:])


def _compress_kernel(xk_ref, xv_ref, pe_ref, w1_ref, w2_ref, kg_ref, dupk_ref, pickv_ref, kc_ref, vct_ref):
    def mlp(x_ref, which):
        top = bot = None
        for t in range(CMP_STRIDE):
            xt = x_ref[pl.ds(t, N_CHUNK, stride=CMP_STRIDE), :]
            a = _dot((xt + pe_ref[which, t]).astype(BF16), w1_ref[which, t])
            b = _dot((xt + pe_ref[which, CMP_STRIDE + t]).astype(BF16), w1_ref[which, CMP_STRIDE + t])
            top = a if top is None else top + a
            bot = b if bot is None else bot + b
        hid = top + pltpu.roll(bot, N_CHUNK - 1, 0)
        act = (hid * (1.0 / (1.0 + jnp.exp(-hid)))).astype(BF16)
        return _dot(act, w2_ref[which])

    k = _pair_norm(mlp(xk_ref, 0), kg_ref).astype(BF16)
    v = mlp(xv_ref, 1).astype(BF16)
    for g in range(NSA_KV_HEADS):
        kc_ref[0, g] = _dot(k, dupk_ref[g]).astype(BF16)
        vct_ref[0, g] = _dot_nt(pickv_ref[g], v).astype(BF16)


def _compress(kck, kcv, cmp_pe, cmp_w1, cmp_w2, k_gain0):
    b = kck.shape[0] // SEQ
    g_ = NSA_KV_HEADS
    d = NSA_HEAD_DIM
    pe = jnp.tile(cmp_pe, (1, 1, g_))[:, :, None, :]
    w1 = cmp_w1.reshape(2, CMP_BLOCK, d, CMP_HIDDEN)
    z1 = jnp.zeros_like(w1)
    w1 = jnp.concatenate([jnp.concatenate([w1, z1], axis=3), jnp.concatenate([z1, w1], axis=3)], axis=2)
    z2 = jnp.zeros_like(cmp_w2)
    w2 = jnp.concatenate([jnp.concatenate([cmp_w2, z2], axis=2), jnp.concatenate([z2, cmp_w2], axis=2)], axis=1)
    dupk = np.zeros((g_, LANES, LANES), np.float32)
    pickv = np.zeros((g_, d, LANES), np.float32)
    for g in range(g_):
        dupk[g, g * d + np.arange(d), np.arange(d)] = 1.0
        dupk[g, g * d + np.arange(d), d + np.arange(d)] = 1.0
        pickv[g, np.arange(d), g * d + np.arange(d)] = 1.0
    full = lambda a: pl.BlockSpec(a.shape, lambda i: (0,) * a.ndim)
    consts = [pe, w1.astype(BF16), w2.astype(BF16), jnp.tile(k_gain0, g_)[None, :],
              jnp.asarray(dupk, BF16), jnp.asarray(pickv, BF16)]
    x_spec = pl.BlockSpec((SEQ, LANES), lambda i: (i, 0))
    return pl.pallas_call(
        _compress_kernel,
        grid=(b,),
        in_specs=[x_spec, x_spec] + [full(a) for a in consts],
        out_specs=[pl.BlockSpec((1, g_, N_CHUNK, LANES), lambda i: (i, 0, 0, 0)),
                   pl.BlockSpec((1, g_, d, N_CHUNK), lambda i: (i, 0, 0, 0))],
        out_shape=[jax.ShapeDtypeStruct((b, g_, N_CHUNK, LANES), BF16),
                   jax.ShapeDtypeStruct((b, g_, d, N_CHUNK), BF16)],
        compiler_params=pltpu.CompilerParams(dimension_semantics=("parallel",)),
        name="compression",
    )(kck, kcv, *consts)


def _flash_init(m_ref, acc_ref):
    m_ref[...] = jnp.full(m_ref.shape, NEG, F32)
    acc_ref[...] = jnp.zeros(acc_ref.shape, F32)


def _flash_step(s, v, m_ref, acc_ref, chunk=KEY_CHUNK):
    for c in range(s.shape[0] // chunk):
        sc = s[c * chunk:(c + 1) * chunk]
        m_old = m_ref[c]
        m_new = jnp.maximum(m_old, jnp.max(sc, axis=0, keepdims=True))
        p = jnp.exp2(sc - m_new).astype(BF16)
        acc_ref[c] = jnp.exp2(m_old - m_new) * acc_ref[c] + _dot(v[:, c * chunk:(c + 1) * chunk], p)
        m_ref[c] = m_new


def _flash_softmax(s, m_ref, p_ref, alpha_ref, chunk=KEY_CHUNK):
    for c in range(s.shape[0] // chunk):
        sc = s[c * chunk:(c + 1) * chunk]
        m_old = m_ref[c]
        m_new = jnp.maximum(m_old, jnp.max(sc, axis=0, keepdims=True))
        alpha_ref[c] = jnp.exp2(m_old - m_new)
        p_ref[c * chunk:(c + 1) * chunk, :] = jnp.exp2(sc - m_new).astype(BF16)
        m_ref[c] = m_new


def _flash_accumulate(v, p_ref, alpha_ref, acc_ref, chunk=KEY_CHUNK):
    for c in range(p_ref.shape[0] // chunk):
        rows = slice(c * chunk, (c + 1) * chunk)
        acc_ref[c] = alpha_ref[c] * acc_ref[c] + _dot(v[:, rows], p_ref[rows, :])


def _flash_finish(m_ref, acc_ref):
    n = m_ref.shape[0]
    m = m_ref[0]
    for c in range(1, n):
        m = jnp.maximum(m, m_ref[c])
    acc = sum(jnp.exp2(m_ref[c] - m) * acc_ref[c] for c in range(n))
    return acc[:MLA_V] / acc[MLA_V:MLA_V + 1]


def _pipelined_pairs(sched_ref, row, n, scores, softmax, accumulate, unroll=PAIR_UNROLL):
    assert n >= 2 and unroll % 2 == 0

    def pair(i):
        i = jnp.minimum(i, n - 1)
        return sched_ref[row, i], sched_ref[row + 1, i]

    def steps(first, count):
        for u in range(count):
            i, slot = first + u, u % 2
            if accumulate is None:
                scores(*pair(i + 1), 1 - slot)
                softmax(*pair(i), slot)
            else:
                scores(*pair(i + 2), slot)
                softmax(*pair(i + 1), 1 - slot)
                accumulate(*pair(i), slot)

    scores(*pair(0), 0)
    if accumulate is not None:
        scores(*pair(1), 1)
        softmax(*pair(0), 0)
    loops = n // unroll
    lax.fori_loop(0, loops, lambda j, carry: (steps(j * unroll, unroll), carry)[1], 0)
    steps(loops * unroll, n - loops * unroll)


def _mla_pairs():
    nq, nk = SEQ // MLA_Q, SEQ // MLA_K
    full = [(qt, kt) for kt in range(nk) for qt in range(nq) if (kt + 1) * MLA_K <= qt * MLA_Q]
    diag = [(qt, (qt * MLA_Q) // MLA_K) for qt in range(nq)]
    return full, diag


def _mla_kernel(sched_ref, q_ref, k_ref, vt_ref, eye_ref, wo_ref, wup_ref, wdn_ref,
                o_ref, wo16_ref, wup16_ref, wdn16_ref, m_ref, acc_ref, s_ref):
    for w_ref, w16_ref in ((wo_ref, wo16_ref), (wup_ref, wup16_ref), (wdn_ref, wdn16_ref)):
        w16_ref[...] = w_ref[...].astype(BF16)
    sub = MLA_K // V_TILE
    full, diag = _mla_pairs()
    rel = (lax.broadcasted_iota(jnp.int32, (MLA_K, MLA_Q), 1)
           - lax.broadcasted_iota(jnp.int32, (MLA_K, MLA_Q), 0))
    _flash_init(m_ref, acc_ref)

    def scores(qt, kt, slot, masked):
        qrows = pl.ds(pl.multiple_of(qt * MLA_Q, MLA_Q), MLA_Q)
        krows = pl.ds(pl.multiple_of(kt * MLA_K, MLA_K), MLA_K)
        for hh in range(2):
            cols = slice(hh * LANES, (hh + 1) * LANES)
            s = _dot_nt(k_ref[krows, cols], q_ref[qrows, cols])
            if masked:
                s = jnp.where(rel >= kt * MLA_K - qt * MLA_Q, s, NEG)
            s_ref[slot, hh] = s

    def update(qt, kt, slot):
        for hh in range(2):
            v = jnp.concatenate([vt_ref[kt * sub + i, hh * V_ROWS:(hh + 1) * V_ROWS, :] for i in range(sub)],
                                axis=1)
            _flash_step(s_ref[slot, hh], v, m_ref.at[hh, qt], acc_ref.at[hh, qt])

    _pipelined_pairs(sched_ref, 0, len(full), lambda qt, kt, slot: scores(qt, kt, slot, False), update, None)
    _pipelined_pairs(sched_ref, 2, len(diag), lambda qt, kt, slot: scores(qt, kt, slot, True), update, None)

    for qt in range(SEQ // MLA_Q):
        o_t = jnp.concatenate([_flash_finish(m_ref.at[hh, qt], acc_ref.at[hh, qt])
                               for hh in range(2)], axis=0)
        o_ref[qt * MLA_Q:(qt + 1) * MLA_Q, :] = _dot_nt(eye_ref[...], o_t.astype(BF16)).astype(BF16)


def _schedule(*pair_lists):
    n = max(len(p) for p in pair_lists)
    out = np.zeros((2 * len(pair_lists), n), np.int32)
    for i, pairs in enumerate(pair_lists):
        out[2 * i, :len(pairs)] = [a for a, _ in pairs]
        out[2 * i + 1, :len(pairs)] = [b for _, b in pairs]
    return jnp.asarray(out)


def _mla_attention(q, k, vt, w_o, w_up, w_down):
    b = q.shape[0] // SEQ
    nq = SEQ // MLA_Q
    nv = SEQ // V_TILE
    chunks = MLA_K // KEY_CHUNK
    head_pairs = MLA_HEADS // 2
    steps = b * head_pairs
    weights = [w_o, w_up, w_down]
    assert all(w.shape[0] % (steps * BF16_ROWS) == 0 for w in weights)
    w_specs = [pl.BlockSpec((w.shape[0] // steps, w.shape[1]), lambda i, hp: (i * head_pairs + hp, 0))
               for w in weights]
    return pl.pallas_call(
        _mla_kernel,
        grid=(b, head_pairs),
        in_specs=[pl.BlockSpec(memory_space=pltpu.SMEM),
                  pl.BlockSpec((SEQ, 2 * LANES), lambda i, hp: (i, hp)),
                  pl.BlockSpec((SEQ, 2 * LANES), lambda i, hp: (i, hp)),
                  pl.BlockSpec((nv, 2 * V_ROWS, V_TILE), lambda i, hp: (i, hp, 0)),
                  pl.BlockSpec((MLA_Q, MLA_Q), lambda i, hp: (0, 0))] + w_specs,
        out_specs=[pl.BlockSpec((SEQ, LANES), lambda i, hp: (i, hp))] + w_specs,
        out_shape=[jax.ShapeDtypeStruct((q.shape[0], MLA_HEADS * MLA_V), BF16)]
                  + [jax.ShapeDtypeStruct(w.shape, BF16) for w in weights],
        scratch_shapes=[pltpu.VMEM((2, nq, chunks, 1, MLA_Q), F32),
                        pltpu.VMEM((2, nq, chunks, V_ROWS, MLA_Q), F32),
                        pltpu.VMEM((2, 2, MLA_K, MLA_Q), F32)],
        compiler_params=pltpu.CompilerParams(
            dimension_semantics=("parallel", "parallel"), vmem_limit_bytes=VMEM_LIMIT),
        name="mla_attention",
    )(_schedule(*_mla_pairs()), q, k, vt, jnp.eye(MLA_Q, dtype=BF16), *weights)


def _select_kernel(q_ref, kc_ref, vct_ref, tabc_ref, ovt_ref, eye_ref, qpad_ref, qaug_ref, ocmp_ref):
    step = pl.program_id(1)
    t = SELECT_Q
    heads = range(NSA_GROUP)
    tiles = range(t // NSA_Q)

    def group(g):
        q_pad = _dot(q_ref[:, g * 2 * LANES:(g + 1) * 2 * LANES], qpad_ref[...]).astype(BF16)
        low = lax.broadcasted_iota(jnp.int32, (t, LANES), 1) < NSA_HEAD_DIM
        zero = jnp.zeros((t, LANES), BF16)
        pairs = [q_ref[:, (2 * g + j) * LANES:(2 * g + j + 1) * LANES] for j in range(NSA_GROUP // 2)]
        q4 = jnp.concatenate([jnp.where(low == (r % 2 == 0), pairs[r // 2], zero) for r in heads], axis=0)

        per_tile = NSA_Q // CMP_STRIDE
        rows = [pl.ds(pl.multiple_of(CMP_TAB_BASE - (step * len(tiles) + u) * per_tile, per_tile), N_CHUNK)
                for u in tiles]
        bias_c = jnp.concatenate([tabc_ref[g * NSA_GROUP + r, rows[u], :] for r in heads for u in tiles], axis=1)
        valid = bias_c > 0.5 * NEG
        sc = jnp.where(valid, _dot_nt(kc_ref[0, g], q4) + bias_c, NEG)
        mx = jnp.max(sc, axis=0, keepdims=True)
        p = jnp.where(valid, jnp.exp2(sc - mx), 0.0)
        pc = p / jnp.maximum(jnp.sum(p, axis=0, keepdims=True), 1e-30)
        o_cmp = _dot(vct_ref[0, g], pc.astype(BF16)).astype(BF16)
        for u in tiles:
            ocmp_ref[0, g, u] = jnp.concatenate(
                [o_cmp[:, r * t + u * NSA_Q:r * t + (u + 1) * NSA_Q] for r in heads], axis=0)

        psum = sum(pc[:, r * t:(r + 1) * t] for r in heads)
        imp = sum(_dot(ovt_ref[...], piece) for piece in _split(psum, 3))
        jj = lax.broadcasted_iota(jnp.int32, (N_SEL, t), 0)
        blk = (step * t + lax.broadcasted_iota(jnp.int32, (N_SEL, t), 1)) // SEL_BLOCK
        ok = jj <= blk
        forced = ok & ((jj == 0) | (jj == blk) | (jj == blk - 1))
        score = jnp.where(forced, FORCE_SCORE, jnp.where(ok, imp, -jnp.inf))
        rank = jnp.zeros((N_SEL, t), jnp.int32)
        for i in range(N_SEL):
            ci = score[i:i + 1, :]
            beats = (ci > score) | ((ci == score) & (jj > i))
            rank = rank + beats.astype(jnp.int32)
        negsel = jnp.where(ok & (rank < SEL_TOP_N), 0.0, NEG).astype(BF16)
        spread = jnp.concatenate([jnp.zeros((AUG_BLOCK, t), BF16), negsel,
                                  jnp.zeros((LANES - AUG_ONE, t), BF16)], axis=0)
        mask_lanes = _dot_nt(eye_ref[...], spread).astype(BF16)
        lane = lax.broadcasted_iota(jnp.int32, (t, LANES), 1)
        for r in heads:
            far = tabc_ref[g * NSA_GROUP + r, 0:1, :]
            far_hi = far.astype(BF16)
            far_lo = (far - far_hi.astype(F32)).astype(BF16)
            aug = jnp.where(lane == AUG_ONE, far_hi, jnp.where(lane == AUG_ONE + 1, far_lo, mask_lanes))
            q_aug = jnp.where(lane < AUG_BLOCK, q_pad[:, r * LANES:(r + 1) * LANES], aug)
            for u in tiles:
                qaug_ref[0, g, u, r * NSA_Q:(r + 1) * NSA_Q, :] = q_aug[u * NSA_Q:(u + 1) * NSA_Q]

    for g in range(NSA_KV_HEADS):
        group(g)


def _nsa_select(q, kc, vct, tables):
    b = q.shape[0] // SEQ
    t = SELECT_Q
    steps = SEQ // t
    nq = SEQ // NSA_Q
    per = t // NSA_Q
    g_ = NSA_KV_HEADS
    c0 = np.arange(N_CHUNK)[None, :] * CMP_STRIDE
    j0 = np.arange(N_SEL)[:, None] * SEL_BLOCK
    ovt = np.clip(np.minimum(c0 + CMP_BLOCK, j0 + SEL_BLOCK) - np.maximum(c0, j0), 0, None) / CMP_BLOCK
    ovt[:, N_CHUNK - 1:] = 0.0
    return pl.pallas_call(
        _select_kernel,
        grid=(b, steps),
        in_specs=[pl.BlockSpec((t, g_ * 2 * LANES), lambda i, j: (i * steps + j, 0)),
                  pl.BlockSpec((1, g_, N_CHUNK, LANES), lambda i, j: (i, 0, 0, 0)),
                  pl.BlockSpec((1, g_, NSA_HEAD_DIM, N_CHUNK), lambda i, j: (i, 0, 0, 0)),
                  pl.BlockSpec((NSA_HEADS, CMP_TAB_ROWS, LANES), lambda i, j: (0, WIN_ROWS // CMP_TAB_ROWS, 0)),
                  pl.BlockSpec((N_SEL, N_CHUNK), lambda i, j: (0, 0)),
                  pl.BlockSpec((t, t), lambda i, j: (0, 0)),
                  pl.BlockSpec((2 * LANES, NSA_GROUP * LANES), lambda i, j: (0, 0))],
        out_specs=[pl.BlockSpec((1, g_, per, NSA_GROUP * NSA_Q, LANES), lambda i, j: (i, 0, j, 0, 0)),
                   pl.BlockSpec((1, g_, per, NSA_GROUP * NSA_HEAD_DIM, NSA_Q), lambda i, j: (i, 0, j, 0, 0))],
        out_shape=[jax.ShapeDtypeStruct((b, g_, nq, NSA_GROUP * NSA_Q, LANES), BF16),
                   jax.ShapeDtypeStruct((b, g_, nq, NSA_GROUP * NSA_HEAD_DIM, NSA_Q), BF16)],
        compiler_params=pltpu.CompilerParams(
            dimension_semantics=("parallel", "parallel"), vmem_limit_bytes=VMEM_LIMIT),
        name="nsa_select",
    )(q, kc, vct, tables, jnp.asarray(ovt, BF16), jnp.eye(t, dtype=BF16),
      jnp.asarray(_head_padding(NSA_GROUP), BF16))


def _nsa_pairs():
    nq, nk, per = SEQ // NSA_Q, SEQ // NSA_K, NSA_K // NSA_Q
    sel = [(qt, kt) for kt in range(nk) for qt in range(nq) if kt * NSA_K <= qt * NSA_Q]
    far = [(qt, kt) for qt, kt in sel if qt - kt * per >= SEL_FAR_TILE]
    near = [(qt, kt) for qt, kt in sel if qt - kt * per < SEL_FAR_TILE]
    win = [(qt, kt) for kt in range(nk) for qt in range(nq)
           if max(qt * NSA_Q - (WINDOW - 1), 0) // NSA_K <= kt and kt * NSA_K <= qt * NSA_Q]
    return far, near, win


def _attend_kernel(sched_ref, qaug_ref, ksel_ref, kwin_ref, vsel_ref, vwin_ref, ocmp_ref, gate_ref,
                   tabw_ref, eye_ref, o_ref, m_ref, acc_ref, s_ref, p_ref, alpha_ref):
    t = NSA_Q
    heads = range(NSA_GROUP)
    far, near, win = _nsa_pairs()
    _flash_init(m_ref, acc_ref)

    def scores(qt, kt, slot, k_ref, add_bias):
        s = _dot_nt(k_ref[pl.ds(pl.multiple_of(kt * NSA_K, NSA_K), NSA_K), :], qaug_ref[0, 0, qt])
        if add_bias:
            off = qt - kt * (NSA_K // NSA_Q)
            bias_rows = pl.ds(pl.multiple_of(off * NSA_K, NSA_K), NSA_K)
            s = s + jnp.concatenate([tabw_ref[r, bias_rows, :] for r in heads], axis=1)
        s_ref[slot] = s

    def softmax(qt, kt, slot, branch):
        _flash_softmax(s_ref[slot], m_ref.at[branch, qt], p_ref.at[slot], alpha_ref.at[slot], chunk=NSA_K)

    def accumulate(qt, kt, slot, v_ref, branch):
        _flash_accumulate(v_ref[kt], p_ref.at[slot], alpha_ref.at[slot], acc_ref.at[branch, qt], chunk=NSA_K)

    for row, pairs, k_ref, v_ref, branch, add_bias in ((0, far, ksel_ref, vsel_ref, 0, False),
                                                       (2, near, ksel_ref, vsel_ref, 0, True),
                                                       (4, win, kwin_ref, vwin_ref, 1, True)):
        _pipelined_pairs(sched_ref, row, len(pairs),
                         lambda qt, kt, slot, k_ref=k_ref, add_bias=add_bias: scores(qt, kt, slot, k_ref, add_bias),
                         lambda qt, kt, slot, branch=branch: softmax(qt, kt, slot, branch),
                         lambda qt, kt, slot, v_ref=v_ref, branch=branch: accumulate(qt, kt, slot, v_ref, branch),
                         unroll=NSA_UNROLL)

    d = NSA_HEAD_DIM
    group = pl.program_id(1)
    for qt in range(SEQ // t):
        o_cmp = ocmp_ref[0, 0, qt].astype(F32)
        o_sel, o_win = [_flash_finish(m_ref.at[br, qt], acc_ref.at[br, qt]) for br in range(2)]
        mixed = []
        for r in heads:
            gate = [gate_ref[qt, pl.ds(br * NSA_HEADS + group * NSA_GROUP + r, 1), :]
                    for br in range(N_BRANCH)]
            mixed.append(gate[0] * o_cmp[r * d:(r + 1) * d] + gate[1] * o_sel[:, r * t:(r + 1) * t]
                         + gate[2] * o_win[:, r * t:(r + 1) * t])
        o_ref[qt * t:(qt + 1) * t, :] = _dot_nt(eye_ref[...], jnp.concatenate(mixed, axis=0).astype(BF16)).astype(BF16)


def _nsa_attend(qaug, kaug, vt, ocmp, gates, tables):
    b = kaug.shape[0] // SEQ
    t = NSA_Q
    nq = SEQ // t
    nv = SEQ // V_TILE
    g_ = NSA_KV_HEADS
    k_spec = lambda off: pl.BlockSpec((SEQ, LANES), lambda i, g: (i, off + g))
    v_spec = lambda off: pl.BlockSpec((nv, V_ROWS, V_TILE), lambda i, g: (i, off + g, 0))
    return pl.pallas_call(
        _attend_kernel,
        grid=(b, g_),
        in_specs=[pl.BlockSpec(memory_space=pltpu.SMEM),
                  pl.BlockSpec((1, 1, nq, NSA_GROUP * t, LANES), lambda i, g: (i, g, 0, 0, 0)),
                  k_spec(0), k_spec(2), v_spec(0), v_spec(2),
                  pl.BlockSpec((1, 1, nq, NSA_GROUP * NSA_HEAD_DIM, t), lambda i, g: (i, g, 0, 0, 0)),
                  pl.BlockSpec((nq, LANES, t), lambda i, g: (i, 0, 0)),
                  pl.BlockSpec((NSA_GROUP, WIN_ROWS, LANES), lambda i, g: (g, 0, 0)),
                  pl.BlockSpec((t, t), lambda i, g: (0, 0))],
        out_specs=pl.BlockSpec((SEQ, 2 * LANES), lambda i, g: (i, g)),
        out_shape=jax.ShapeDtypeStruct((kaug.shape[0], NSA_HEADS * NSA_HEAD_DIM), BF16),
        scratch_shapes=[pltpu.VMEM((2, nq, 1, 1, NSA_GROUP * t), F32),
                        pltpu.VMEM((2, nq, 1, V_ROWS, NSA_GROUP * t), F32),
                        pltpu.VMEM((2, NSA_K, NSA_GROUP * t), F32),
                        pltpu.VMEM((2, NSA_K, NSA_GROUP * t), BF16),
                        pltpu.VMEM((2, 1, 1, NSA_GROUP * t), F32)],
        compiler_params=pltpu.CompilerParams(
            dimension_semantics=("parallel", "parallel"), vmem_limit_bytes=VMEM_LIMIT),
        name="nsa_attend",
    )(_schedule(*_nsa_pairs()), qaug, kaug, kaug, vt, vt, ocmp, gates, tables, jnp.eye(t, dtype=BF16))


def _tail_kernel(x_ref, oa_ref, ob_ref, ga_ref, gb_ref, wo_ref, ln2_ref, wup_ref, wdn_ref,
                 out_ref, h2_ref):
    width = MLA_HEADS * MLA_V
    a = (_rms(oa_ref[...].astype(F32), width) * ga_ref[...]).astype(BF16)
    b = (_rms(ob_ref[...].astype(F32), width) * gb_ref[...]).astype(BF16)
    x1 = x_ref[...] + _dot(a, wo_ref[:width]) + _dot(b, wo_ref[width:])
    h2_ref[...] = (_rms(x1, D_MODEL) * ln2_ref[...]).astype(BF16)
    out_ref[...] = x1

    def body(f, carry):
        cols = pl.ds(pl.multiple_of(f * FF_CHUNK, FF_CHUNK), FF_CHUNK)
        u = jnp.maximum(_dot(h2_ref[...], wup_ref[:, cols]), 0.0)
        out_ref[...] += _dot((u * u).astype(BF16), wdn_ref[f])
        return carry

    lax.fori_loop(0, D_FF // FF_CHUNK, body, 0)


def _tail(x2d, o_a, o_b, gn_a, gn_b, w_o, ln2_g, w_up, w_down):
    t = x2d.shape[0]
    rows = TAIL_ROWS
    width = MLA_HEADS * MLA_V
    once = lambda a: pl.BlockSpec(a.shape, lambda i: (0,) * a.ndim)
    assert w_o.dtype == w_up.dtype == w_down.dtype == BF16
    consts = [gn_a[None, :], gn_b[None, :], w_o, ln2_g[None, :], w_up,
              w_down.reshape(D_FF // FF_CHUNK, FF_CHUNK, D_MODEL)]
    row_spec = lambda w: pl.BlockSpec((rows, w), lambda i: (i, 0))
    return pl.pallas_call(
        _tail_kernel,
        grid=(t // rows,),
        in_specs=[row_spec(D_MODEL), row_spec(width), row_spec(width)] + [once(a) for a in consts],
        out_specs=row_spec(D_MODEL),
        out_shape=jax.ShapeDtypeStruct((t, D_MODEL), F32),
        scratch_shapes=[pltpu.VMEM((rows, D_MODEL), BF16)],
        compiler_params=pltpu.CompilerParams(dimension_semantics=("parallel",),
                                             vmem_limit_bytes=VMEM_LIMIT),
        name="tail",
    )(x2d, o_a, o_b, *consts)


def kernel(x, ln1_g, w_in, mla_cq_norm_g, mla_ckv_norm_g, mla_w_uq, mla_w_ukv, mla_q_gain, mla_k_gain,
           nsa_q_gain, nsa_k_gain, nsa_cmp_pe, nsa_cmp_w1, nsa_cmp_w2, rel_bias, grp_norm_mla,
           grp_norm_nsa, w_o, ln2_g, w_up, w_down):
    b, s, d = x.shape
    assert (s, d) == (SEQ, D_MODEL) and ln1_g.shape[0] == 1
    x2d = x.reshape(b * s, d)
    tables = _bias_tables(rel_bias)
    q_mla, k_mla, vt_mla, q_nsa, kdup, vt_nsa, kck, kcv, gates = _projections(
        x2d, ln1_g[0], w_in[0], mla_cq_norm_g[0], mla_ckv_norm_g[0], mla_w_uq[0], mla_w_ukv[0],
        mla_q_gain[0], mla_k_gain[0], nsa_q_gain[0], nsa_k_gain[0])
    kc, vct = _compress(kck, kcv, nsa_cmp_pe[0], nsa_cmp_w1[0], nsa_cmp_w2[0], nsa_k_gain[0, 0])
    o_a, w_o16, w_up16, w_down16 = _mla_attention(q_mla, k_mla, vt_mla, w_o[0], w_up[0], w_down[0])
    q_aug, ocmp = _nsa_select(q_nsa, kc, vct, tables)
    o_b = _nsa_attend(q_aug, kdup, vt_nsa, ocmp, gates, tables)
    out = _tail(x2d, o_a, o_b, grp_norm_mla[0], grp_norm_nsa[0], w_o16, ln2_g[0], w_up16, w_down16)
    return out.reshape(b, s, d)
```

```python
import math

import numpy as np
import jax
import jax.numpy as jnp
from jax import lax
from jax.experimental import pallas as pl
from jax.experimental.pallas import tpu as pltpu

F32 = jnp.float32
BF16 = jnp.bfloat16

D_MODEL = 1024
SEQ = 2048
MLA_HEADS = 8
MLA_NOPE = 64
MLA_ROPE = 32
MLA_V = 64
MLA_QK = MLA_NOPE + MLA_ROPE
MLA_Q_RANK = 384
MLA_KV_RANK = 256
ROPE_THETA = 10000.0
NSA_HEADS = 8
NSA_KV_HEADS = 2
NSA_GROUP = NSA_HEADS // NSA_KV_HEADS
NSA_HEAD_DIM = 64
N_BRANCH = 3
CMP_BLOCK = 32
CMP_STRIDE = 16
CMP_HIDDEN = 128
SEL_BLOCK = 64
SEL_TOP_N = 16
WINDOW = 512
FORCE_SCORE = 1e4
REL_BUCKETS = 32
REL_MAX_DIST = 128
D_FF = 4 * D_MODEL
EPS = 1e-6

LANES = 128
SUBLANES = 8
BF16_ROWS = 16
NEG = -1e30
N_CHUNK = SEQ // CMP_STRIDE
N_SEL = SEQ // SEL_BLOCK
NSA_Q = 128
NSA_K = 256
SELECT_Q = 512
N_BIAS_TILES = (WINDOW + NSA_K - NSA_Q) // NSA_Q + 1
SEL_FAR_TILE = (REL_MAX_DIST + NSA_K) // NSA_Q
WIN_ROWS = N_BIAS_TILES * NSA_K
AUG_BLOCK = NSA_HEAD_DIM
AUG_ONE = AUG_BLOCK + N_SEL
CMP_TAB_BASE = (SEQ // NSA_Q - 1) * (NSA_Q // CMP_STRIDE)
CMP_TAB_ROWS = 256
MLA_Q = 256
MLA_K = 512
V_TILE = 256
PROJ_ROWS = 256
PACK_ROWS = 256
TAIL_ROWS = 512
FF_CHUNK = 2048
KEY_CHUNK = 256
PAIR_UNROLL = 6
NSA_UNROLL = 6
V_ROWS = 80
LOG2E = math.log2(math.e)
VMEM_LIMIT = 56 * 1024 * 1024

assert PROJ_ROWS % V_TILE == 0 and MLA_K % V_TILE == 0 and NSA_K == V_TILE


def _dot(a, b):
    return jnp.dot(a, b, preferred_element_type=F32)


def _dot_nt(a, b):
    return lax.dot_general(a, b, (((1,), (1,)), ((), ())), preferred_element_type=F32)


def _split(a, terms):
    pieces = []
    rem = a
    for _ in range(terms):
        piece = rem.astype(BF16)
        pieces.append(piece)
        rem = rem - piece.astype(F32)
    return pieces


def _split_dot(a, b, terms=2):
    return sum(_dot(p, b) for p in _split(a, terms))


def _rms(x, width):
    return x * lax.rsqrt(jnp.sum(x * x, axis=-1, keepdims=True) * (1.0 / width) + EPS)


def _t5_bucket_np(dist):
    n = np.maximum(dist, 0)
    max_exact = REL_BUCKETS // 2
    large = max_exact + (np.log(np.maximum(n, 1).astype(np.float32) / max_exact)
                         / math.log(REL_MAX_DIST / max_exact)
                         * (REL_BUCKETS - max_exact)).astype(np.int32)
    large = np.minimum(large, REL_BUCKETS - 1)
    return np.where(n < max_exact, n, large).astype(np.int32)


def _bias_index_table():
    i = np.arange(NSA_Q)[None, :]
    parts = []
    j = np.arange(NSA_K)[:, None]
    for off in range(N_BIAS_TILES):
        d = off * NSA_Q + i - j
        parts.append(np.where((d >= 0) & (d < WINDOW), _t5_bucket_np(d), -1))
    r = np.arange(CMP_TAB_ROWS)[:, None]
    dist_c = (CMP_TAB_BASE - r) * CMP_STRIDE + i - (CMP_BLOCK - 1)
    parts.append(np.where(dist_c >= 0, _t5_bucket_np(dist_c), -1))
    return np.concatenate(parts, axis=0).astype(np.int32)


def _bias_table_kernel(rb_ref, idx_ref, out_ref):
    idx = idx_ref[...]
    bucket = jnp.maximum(idx, 0)
    is_window = pl.program_id(0) < WIN_ROWS // idx.shape[0]
    for h in range(NSA_HEADS):
        row = rb_ref[h:h + 1, :]
        far = jnp.where(is_window, row[:, REL_BUCKETS - 1:REL_BUCKETS], 0.0)
        values = jnp.broadcast_to((row - far) * LOG2E, idx.shape)
        picked = jnp.take_along_axis(values, bucket, axis=1, mode="promise_in_bounds")
        out_ref[h] = jnp.where(idx >= 0, picked, NEG)


def _bias_tables(rel_bias):
    idx = jnp.asarray(_bias_index_table())
    rows = idx.shape[0]
    blk = CMP_TAB_ROWS
    return pl.pallas_call(
        _bias_table_kernel,
        grid=(rows // blk,),
        in_specs=[pl.BlockSpec((NSA_HEADS, LANES), lambda r: (0, 0)),
                  pl.BlockSpec((blk, LANES), lambda r: (r, 0))],
        out_specs=pl.BlockSpec((NSA_HEADS, blk, LANES), lambda r: (0, r, 0)),
        out_shape=jax.ShapeDtypeStruct((NSA_HEADS, rows, LANES), F32),
        name="bias_tables",
    )(jnp.pad(rel_bias.T, ((0, 0), (0, LANES - REL_BUCKETS))), idx)


ROPE_HALF = MLA_ROPE // 2
MISC_KR = 64


def _rope_tables(scale):
    inv = 1.0 / (ROPE_THETA ** (np.arange(ROPE_HALF, dtype=np.float64) / ROPE_HALF))
    ang = np.arange(SEQ, dtype=np.float64)[:, None] * inv[None, :]
    cos, sin = np.cos(ang), np.sin(ang)
    ones = np.ones((SEQ, MLA_NOPE))
    z32 = np.zeros((SEQ, LANES - MLA_QK))
    z64 = np.zeros((SEQ, MLA_NOPE))
    c = np.concatenate([ones, cos, cos, z32], axis=1) * scale
    s = np.concatenate([z64, -sin, sin, z32], axis=1) * scale
    return jnp.asarray(c, F32), jnp.asarray(s, F32)


def _pair_norm(x, gain_ref):
    low = lax.broadcasted_iota(jnp.int32, (x.shape[0], LANES), 1) < NSA_HEAD_DIM
    outs = []
    for j in range(x.shape[1] // LANES):
        cols = slice(j * LANES, (j + 1) * LANES)
        sl = x[:, cols]
        sq = sl * sl
        tot = jnp.sum(sq, axis=-1, keepdims=True)
        lo = jnp.sum(jnp.where(low, sq, 0.0), axis=-1, keepdims=True)
        rs = jnp.where(low, lax.rsqrt(lo * (1.0 / NSA_HEAD_DIM) + EPS),
                       lax.rsqrt((tot - lo) * (1.0 / NSA_HEAD_DIM) + EPS))
        outs.append(sl * rs * gain_ref[:, cols])
    return jnp.concatenate(outs, axis=1)


def _mla_query_heads(x, gain_ref, c, s):
    in_head = lax.broadcasted_iota(jnp.int32, (x.shape[0], LANES), 1) < MLA_QK
    outs = []
    for h in range(MLA_HEADS):
        cols = slice(h * LANES, (h + 1) * LANES)
        sl = x[:, cols]
        ss = jnp.sum(jnp.where(in_head, sl * sl, 0.0), axis=-1, keepdims=True)
        xn = sl * lax.rsqrt(ss * (1.0 / MLA_QK) + EPS) * gain_ref[:, cols]
        outs.append(xn * c + pltpu.roll(xn, LANES - ROPE_HALF, 1) * s)
    return jnp.concatenate(outs, axis=1)


def _mla_key_heads(x, k_rope, k_swap, c, s):
    in_head = lax.broadcasted_iota(jnp.int32, (x.shape[0], LANES), 1) < MLA_QK
    turned = k_swap * s
    outs = []
    for h in range(MLA_HEADS):
        sl = x[:, h * LANES:(h + 1) * LANES] + k_rope
        ss = jnp.sum(jnp.where(in_head, sl * sl, 0.0), axis=-1, keepdims=True)
        outs.append(lax.rsqrt(ss * (1.0 / MLA_QK) + EPS) * (sl * c + turned))
    return jnp.concatenate(outs, axis=1)


COL_KV = MLA_Q_RANK
COL_MISC = COL_KV + MLA_KV_RANK
COL_SWAP = COL_MISC + LANES
COL_NQ = COL_SWAP + LANES
COL_NK = COL_NQ + NSA_HEADS * NSA_HEAD_DIM
COL_NV = COL_NK + 4 * NSA_HEAD_DIM
COL_CMP = COL_NV + 4 * NSA_HEAD_DIM
COL_END = COL_CMP + 4 * NSA_HEAD_DIM


def _proj_kernel(x_ref, ln1_ref, wp_ref, cqg_ref, ckvg_ref, wuq_ref, wukk_ref, wukvt_ref,
                 qg_ref, kown_ref, kpartner_ref, qc_ref, qs_ref, kc_ref, ks_ref,
                 nq_gain_ref, nk_gain_ref, kpad_ref, kaug_ref, vpick_ref, ones_mla_ref, ones_nsa_ref, eye_ref,
                 qmla_ref, kmla_ref, vmla_ref, qnsa_ref, kdup_ref, vnsa_ref, kck_ref, kcv_ref, gate_ref):
    for sub in range(PROJ_ROWS // V_TILE):
        rows = slice(sub * V_TILE, (sub + 1) * V_TILE)
        x = x_ref[rows, :]
        h = (_rms(x, D_MODEL) * ln1_ref[...]).astype(BF16)
        part = lambda a, b: _dot(h, wp_ref[:, a:b])

        cq = (_rms(part(0, COL_KV), MLA_Q_RANK) * cqg_ref[...]).astype(BF16)
        qmla_ref[rows, :] = _mla_query_heads(_dot(cq, wuq_ref[...]), qg_ref,
                                             qc_ref[rows, :], qs_ref[rows, :]).astype(BF16)

        kv_misc = part(COL_KV, COL_NQ)
        ckv = (_rms(kv_misc[:, :MLA_KV_RANK], MLA_KV_RANK) * ckvg_ref[...]).astype(BF16)
        misc = kv_misc[:, COL_MISC - COL_KV:COL_SWAP - COL_KV]
        lane = lax.broadcasted_iota(jnp.int32, misc.shape, 1)
        k_rope = jnp.where((lane >= MISC_KR) & (lane < MISC_KR + MLA_ROPE), misc, 0.0)
        kmla_ref[rows, :] = _mla_key_heads(_dot(ckv, wukk_ref[...]), k_rope, kv_misc[:, COL_SWAP - COL_KV:],
                                           kc_ref[rows, :] * kown_ref[...],
                                           ks_ref[rows, :] * kpartner_ref[...]).astype(BF16)
        vmla_ref[sub] = (_dot_nt(wukvt_ref[...], ckv) + ones_mla_ref[...]).astype(BF16)

        qnsa_ref[rows, :] = _pair_norm(part(COL_NQ, COL_NK), nq_gain_ref).astype(BF16)

        kn = _pair_norm(part(COL_NK, COL_NV), nk_gain_ref)
        kdup_ref[rows, :] = (_dot(kn.astype(BF16), kpad_ref[...]) + kaug_ref[rows, :]).astype(BF16)
        vnsa_ref[sub] = (_dot_nt(vpick_ref[...], part(COL_NV, COL_CMP).astype(BF16)) + ones_nsa_ref[...]).astype(BF16)
        cmp_kv = part(COL_CMP, COL_END)
        kck_ref[rows, :] = cmp_kv[:, :LANES]
        kcv_ref[rows, :] = cmp_kv[:, LANES:]
        gate = 1.0 / (1.0 + jnp.exp(-misc))
        gate_t = sum(_dot_nt(eye_ref[...], piece) for piece in _split(gate, 2))
        for u in range(V_TILE // NSA_Q):
            gate_ref[sub * (V_TILE // NSA_Q) + u] = gate_t[:, u * NSA_Q:(u + 1) * NSA_Q]


def _head_padding(heads):
    d = NSA_HEAD_DIM
    m = np.zeros((heads * d, heads * LANES), np.float32)
    for hh in range(heads):
        m[hh * d + np.arange(d), hh * LANES + np.arange(d)] = 1.0
    return m


def _packed_segments():
    kv0 = 1184
    seg = lambda a, b: (a, b - a)
    kvb = lambda br, kv, g: (kv0 + ((br * 2 + kv) * 2 + g) * 64, 64)
    zeros = lambda n: (None, n)
    kr = COL_MISC
    gate_cols = N_BRANCH * NSA_HEADS
    pieces = [
        seg(0, COL_MISC),
        seg(1952, 1952 + gate_cols), zeros(MISC_KR - gate_cols), seg(kr, kr + MLA_ROPE),
        zeros(LANES - MISC_KR - MLA_ROPE),
        zeros(MISC_KR), seg(kr + ROPE_HALF, kr + MLA_ROPE), seg(kr, kr + ROPE_HALF), zeros(LANES - MISC_KR - MLA_ROPE),
        seg(672, 1184),
        kvb(1, 0, 0), kvb(1, 0, 1), kvb(2, 0, 0), kvb(2, 0, 1),
        kvb(1, 1, 0), kvb(1, 1, 1), kvb(2, 1, 0), kvb(2, 1, 1),
        kvb(0, 0, 0), kvb(0, 0, 1), kvb(0, 1, 0), kvb(0, 1, 1),
    ]
    assert sum(n for _, n in pieces) == COL_END
    return pieces


def _pack_kernel(wt_ref, out_ref, packed_ref):
    dst = 0
    for src, n in _packed_segments():
        assert n % SUBLANES == 0 and dst % SUBLANES == 0 and (src is None or src % SUBLANES == 0)
        if src is None:
            packed_ref[dst:dst + n, :] = jnp.zeros((n, packed_ref.shape[1]), F32)
        else:
            packed_ref[dst:dst + n, :] = wt_ref[src:src + n, :]
        dst += n
    for g in range(COL_END // LANES):
        cols = slice(g * LANES, (g + 1) * LANES)
        out_ref[:, cols] = packed_ref[cols, :].T.astype(BF16)


def _pack_input_projection(w_in):
    rows = PACK_ROWS
    return pl.pallas_call(
        _pack_kernel,
        grid=(D_MODEL // rows,),
        in_specs=[pl.BlockSpec((w_in.shape[1], rows), lambda i: (0, i))],
        out_specs=pl.BlockSpec((rows, COL_END), lambda i: (i, 0)),
        out_shape=jax.ShapeDtypeStruct((D_MODEL, COL_END), BF16),
        scratch_shapes=[pltpu.VMEM((COL_END, rows), F32)],
        compiler_params=pltpu.CompilerParams(dimension_semantics=("parallel",),
                                             vmem_limit_bytes=VMEM_LIMIT),
        name="pack_input_projection",
    )(w_in.T)


def _projections(x2d, ln1_g, w_in, cq_g, ckv_g, w_uq, w_ukv, q_gain, k_gain, nq_gain, nk_gain):
    t = x2d.shape[0]
    wp = _pack_input_projection(w_in)
    wq3 = w_uq.reshape(MLA_Q_RANK, MLA_HEADS, MLA_QK).astype(BF16)
    wuq = jnp.concatenate([wq3, wq3[:, :, MLA_NOPE:MLA_NOPE + ROPE_HALF],
                           jnp.zeros((MLA_Q_RANK, MLA_HEADS, ROPE_HALF), BF16)], axis=2)
    wuq = wuq.reshape(MLA_Q_RANK, MLA_HEADS * LANES)
    wkv = w_ukv.reshape(MLA_KV_RANK, MLA_HEADS, MLA_NOPE + MLA_V)
    wukk = jnp.pad(wkv[:, :, :MLA_NOPE], ((0, 0), (0, 0), (0, LANES - MLA_NOPE)))
    wukk = wukk.reshape(MLA_KV_RANK, MLA_HEADS * LANES).astype(BF16)
    wukvt = jnp.pad(wkv[:, :, MLA_NOPE:], ((0, 0), (0, 0), (0, V_ROWS - MLA_V)))
    wukvt = wukvt.reshape(MLA_KV_RANK, MLA_HEADS * V_ROWS).T.astype(BF16)
    q_slab_gain = jnp.tile(jnp.concatenate(
        [q_gain, q_gain[MLA_NOPE:MLA_NOPE + ROPE_HALF], jnp.zeros((ROPE_HALF,), F32)]), MLA_HEADS)[None, :]
    qc, qs = _rope_tables(MLA_QK ** -0.5 * LOG2E)
    kc, ks = _rope_tables(1.0)
    k_own = jnp.pad(k_gain, (0, LANES - MLA_QK))[None, :]
    k_partner = jnp.concatenate([k_gain[:MLA_NOPE], k_gain[MLA_NOPE + ROPE_HALF:],
                                 k_gain[MLA_NOPE:MLA_NOPE + ROPE_HALF], jnp.zeros((LANES - MLA_QK,), F32)])[None, :]
    kpad = _head_padding(4)
    kaug = np.zeros((SEQ, 4 * LANES), np.float32)
    for s in range(4):
        kaug[:, s * LANES + AUG_ONE:s * LANES + AUG_ONE + 2] = 1.0
        if s < NSA_KV_HEADS:
            kaug[np.arange(SEQ), s * LANES + AUG_BLOCK + np.arange(SEQ) // SEL_BLOCK] = 1.0
    nqg = (jnp.tile(nq_gain, NSA_HEADS) * (NSA_HEAD_DIM ** -0.5 * LOG2E))[None, :]
    nkg = jnp.concatenate([nk_gain[1], nk_gain[1], nk_gain[2], nk_gain[2]])[None, :]
    vpick = np.zeros((4 * V_ROWS, 256), np.float32)
    for s in range(4):
        vpick[s * V_ROWS + np.arange(64), s * 64 + np.arange(64)] = 1.0

    def ones_col(heads):
        col = np.zeros((heads * V_ROWS, 1), np.float32)
        col[np.arange(heads) * V_ROWS + MLA_V] = 1.0
        return col

    rows = PROJ_ROWS
    n_pos = SEQ // rows
    full = lambda a: pl.BlockSpec(a.shape, lambda i: (0,) * a.ndim)
    tab = pl.BlockSpec((rows, LANES), lambda i: (i % n_pos, 0))
    consts = [ln1_g[None, :], wp, cq_g[None, :], ckv_g[None, :], wuq, wukk, wukvt, q_slab_gain, k_own, k_partner]
    tabs = [qc, qs, kc, ks]
    mats = [nqg, nkg, jnp.asarray(kpad, BF16),
            jnp.asarray(vpick, BF16), jnp.asarray(ones_col(MLA_HEADS)), jnp.asarray(ones_col(4)),
            jnp.eye(LANES, dtype=BF16)]
    row_spec = lambda w: pl.BlockSpec((rows, w), lambda i: (i, 0))
    vt_spec = lambda w, tile: pl.BlockSpec((rows // tile, w, tile), lambda i: (i, 0, 0))
    sds = jax.ShapeDtypeStruct
    return pl.pallas_call(
        _proj_kernel,
        grid=(t // rows,),
        in_specs=([row_spec(D_MODEL)] + [full(a) for a in consts] + [tab] * 4 + [full(a) for a in mats[:3]]
                  + [pl.BlockSpec((rows, 4 * LANES), lambda i: (i % n_pos, 0))] + [full(a) for a in mats[3:]]),
        out_specs=[row_spec(1024), row_spec(1024), vt_spec(MLA_HEADS * V_ROWS, V_TILE), row_spec(512),
                   row_spec(512), vt_spec(4 * V_ROWS, V_TILE), row_spec(128), row_spec(128),
                   vt_spec(LANES, NSA_Q)],
        out_shape=[sds((t, 1024), BF16), sds((t, 1024), BF16), sds((t // V_TILE, MLA_HEADS * V_ROWS, V_TILE), BF16),
                   sds((t, 512), BF16), sds((t, 512), BF16), sds((t // V_TILE, 4 * V_ROWS, V_TILE), BF16),
                   sds((t, 128), F32), sds((t, 128), F32), sds((t // NSA_Q, LANES, NSA_Q), F32)],
        compiler_params=pltpu.CompilerParams(dimension_semantics=("parallel",),
                                             vmem_limit_bytes=VMEM_LIMIT),
        name="projections",
    )(x2d, *consts, *tabs, *mats[:3], jnp.asarray(kaug, BF16), *mats[3:])


def _compress_kernel(xk_ref, xv_ref, pe_ref, w1_ref, w2_ref, kg_ref, dupk_ref, pickv_ref, kc_ref, vct_ref):
    def mlp(x_ref, which):
        top = bot = None
        for t in range(CMP_STRIDE):
            xt = x_ref[pl.ds(t, N_CHUNK, stride=CMP_STRIDE), :]
            a = _dot((xt + pe_ref[which, t]).astype(BF16), w1_ref[which, t])
            b = _dot((xt + pe_ref[which, CMP_STRIDE + t]).astype(BF16), w1_ref[which, CMP_STRIDE + t])
            top = a if top is None else top + a
            bot = b if bot is None else bot + b
        hid = top + pltpu.roll(bot, N_CHUNK - 1, 0)
        act = (hid * (1.0 / (1.0 + jnp.exp(-hid)))).astype(BF16)
        return _dot(act, w2_ref[which])

    k = _pair_norm(mlp(xk_ref, 0), kg_ref).astype(BF16)
    v = mlp(xv_ref, 1).astype(BF16)
    for g in range(NSA_KV_HEADS):
        kc_ref[0, g] = _dot(k, dupk_ref[g]).astype(BF16)
        vct_ref[0, g] = _dot_nt(pickv_ref[g], v).astype(BF16)


def _compress(kck, kcv, cmp_pe, cmp_w1, cmp_w2, k_gain0):
    b = kck.shape[0] // SEQ
    g_ = NSA_KV_HEADS
    d = NSA_HEAD_DIM
    pe = jnp.tile(cmp_pe, (1, 1, g_))[:, :, None, :]
    w1 = cmp_w1.reshape(2, CMP_BLOCK, d, CMP_HIDDEN)
    z1 = jnp.zeros_like(w1)
    w1 = jnp.concatenate([jnp.concatenate([w1, z1], axis=3), jnp.concatenate([z1, w1], axis=3)], axis=2)
    z2 = jnp.zeros_like(cmp_w2)
    w2 = jnp.concatenate([jnp.concatenate([cmp_w2, z2], axis=2), jnp.concatenate([z2, cmp_w2], axis=2)], axis=1)
    dupk = np.zeros((g_, LANES, LANES), np.float32)
    pickv = np.zeros((g_, d, LANES), np.float32)
    for g in range(g_):
        dupk[g, g * d + np.arange(d), np.arange(d)] = 1.0
        dupk[g, g * d + np.arange(d), d + np.arange(d)] = 1.0
        pickv[g, np.arange(d), g * d + np.arange(d)] = 1.0
    full = lambda a: pl.BlockSpec(a.shape, lambda i: (0,) * a.ndim)
    consts = [pe, w1.astype(BF16), w2.astype(BF16), jnp.tile(k_gain0, g_)[None, :],
              jnp.asarray(dupk, BF16), jnp.asarray(pickv, BF16)]
    x_spec = pl.BlockSpec((SEQ, LANES), lambda i: (i, 0))
    return pl.pallas_call(
        _compress_kernel,
        grid=(b,),
        in_specs=[x_spec, x_spec] + [full(a) for a in consts],
        out_specs=[pl.BlockSpec((1, g_, N_CHUNK, LANES), lambda i: (i, 0, 0, 0)),
                   pl.BlockSpec((1, g_, d, N_CHUNK), lambda i: (i, 0, 0, 0))],
        out_shape=[jax.ShapeDtypeStruct((b, g_, N_CHUNK, LANES), BF16),
                   jax.ShapeDtypeStruct((b, g_, d, N_CHUNK), BF16)],
        compiler_params=pltpu.CompilerParams(dimension_semantics=("parallel",)),
        name="compression",
    )(kck, kcv, *consts)


def _flash_init(m_ref, acc_ref):
    m_ref[...] = jnp.full(m_ref.shape, NEG, F32)
    acc_ref[...] = jnp.zeros(acc_ref.shape, F32)


def _flash_step(s, v, m_ref, acc_ref, chunk=KEY_CHUNK):
    for c in range(s.shape[0] // chunk):
        sc = s[c * chunk:(c + 1) * chunk]
        m_old = m_ref[c]
        m_new = jnp.maximum(m_old, jnp.max(sc, axis=0, keepdims=True))
        p = jnp.exp2(sc - m_new).astype(BF16)
        acc_ref[c] = jnp.exp2(m_old - m_new) * acc_ref[c] + _dot(v[:, c * chunk:(c + 1) * chunk], p)
        m_ref[c] = m_new


def _flash_softmax(s, m_ref, p_ref, alpha_ref, chunk=KEY_CHUNK):
    for c in range(s.shape[0] // chunk):
        sc = s[c * chunk:(c + 1) * chunk]
        m_old = m_ref[c]
        m_new = jnp.maximum(m_old, jnp.max(sc, axis=0, keepdims=True))
        alpha_ref[c] = jnp.exp2(m_old - m_new)
        p_ref[c * chunk:(c + 1) * chunk, :] = jnp.exp2(sc - m_new).astype(BF16)
        m_ref[c] = m_new


def _flash_accumulate(v, p_ref, alpha_ref, acc_ref, chunk=KEY_CHUNK):
    for c in range(p_ref.shape[0] // chunk):
        rows = slice(c * chunk, (c + 1) * chunk)
        acc_ref[c] = alpha_ref[c] * acc_ref[c] + _dot(v[:, rows], p_ref[rows, :])


def _flash_finish(m_ref, acc_ref):
    n = m_ref.shape[0]
    m = m_ref[0]
    for c in range(1, n):
        m = jnp.maximum(m, m_ref[c])
    acc = sum(jnp.exp2(m_ref[c] - m) * acc_ref[c] for c in range(n))
    return acc[:MLA_V] / acc[MLA_V:MLA_V + 1]


def _pipelined_pairs(sched_ref, row, n, scores, softmax, accumulate, unroll=PAIR_UNROLL):
    assert n >= 2 and unroll % 2 == 0

    def pair(i):
        i = jnp.minimum(i, n - 1)
        return sched_ref[row, i], sched_ref[row + 1, i]

    def steps(first, count):
        for u in range(count):
            i, slot = first + u, u % 2
            if accumulate is None:
                scores(*pair(i + 1), 1 - slot)
                softmax(*pair(i), slot)
            else:
                scores(*pair(i + 2), slot)
                softmax(*pair(i + 1), 1 - slot)
                accumulate(*pair(i), slot)

    scores(*pair(0), 0)
    if accumulate is not None:
        scores(*pair(1), 1)
        softmax(*pair(0), 0)
    loops = n // unroll
    lax.fori_loop(0, loops, lambda j, carry: (steps(j * unroll, unroll), carry)[1], 0)
    steps(loops * unroll, n - loops * unroll)


def _mla_pairs():
    nq, nk = SEQ // MLA_Q, SEQ // MLA_K
    full = [(qt, kt) for kt in range(nk) for qt in range(nq) if (kt + 1) * MLA_K <= qt * MLA_Q]
    diag = [(qt, (qt * MLA_Q) // MLA_K) for qt in range(nq)]
    return full, diag


def _mla_kernel(sched_ref, q_ref, k_ref, vt_ref, eye_ref, wo_ref, wup_ref, wdn_ref,
                o_ref, wo16_ref, wup16_ref, wdn16_ref, m_ref, acc_ref, s_ref):
    for w_ref, w16_ref in ((wo_ref, wo16_ref), (wup_ref, wup16_ref), (wdn_ref, wdn16_ref)):
        w16_ref[...] = w_ref[...].astype(BF16)
    sub = MLA_K // V_TILE
    full, diag = _mla_pairs()
    rel = (lax.broadcasted_iota(jnp.int32, (MLA_K, MLA_Q), 1)
           - lax.broadcasted_iota(jnp.int32, (MLA_K, MLA_Q), 0))
    _flash_init(m_ref, acc_ref)

    def scores(qt, kt, slot, masked):
        qrows = pl.ds(pl.multiple_of(qt * MLA_Q, MLA_Q), MLA_Q)
        krows = pl.ds(pl.multiple_of(kt * MLA_K, MLA_K), MLA_K)
        for hh in range(2):
            cols = slice(hh * LANES, (hh + 1) * LANES)
            s = _dot_nt(k_ref[krows, cols], q_ref[qrows, cols])
            if masked:
                s = jnp.where(rel >= kt * MLA_K - qt * MLA_Q, s, NEG)
            s_ref[slot, hh] = s

    def update(qt, kt, slot):
        for hh in range(2):
            v = jnp.concatenate([vt_ref[kt * sub + i, hh * V_ROWS:(hh + 1) * V_ROWS, :] for i in range(sub)],
                                axis=1)
            _flash_step(s_ref[slot, hh], v, m_ref.at[hh, qt], acc_ref.at[hh, qt])

    _pipelined_pairs(sched_ref, 0, len(full), lambda qt, kt, slot: scores(qt, kt, slot, False), update, None)
    _pipelined_pairs(sched_ref, 2, len(diag), lambda qt, kt, slot: scores(qt, kt, slot, True), update, None)

    for qt in range(SEQ // MLA_Q):
        o_t = jnp.concatenate([_flash_finish(m_ref.at[hh, qt], acc_ref.at[hh, qt])
                               for hh in range(2)], axis=0)
        o_ref[qt * MLA_Q:(qt + 1) * MLA_Q, :] = _dot_nt(eye_ref[...], o_t.astype(BF16)).astype(BF16)


def _schedule(*pair_lists):
    n = max(len(p) for p in pair_lists)
    out = np.zeros((2 * len(pair_lists), n), np.int32)
    for i, pairs in enumerate(pair_lists):
        out[2 * i, :len(pairs)] = [a for a, _ in pairs]
        out[2 * i + 1, :len(pairs)] = [b for _, b in pairs]
    return jnp.asarray(out)


def _mla_attention(q, k, vt, w_o, w_up, w_down):
    b = q.shape[0] // SEQ
    nq = SEQ // MLA_Q
    nv = SEQ // V_TILE
    chunks = MLA_K // KEY_CHUNK
    head_pairs = MLA_HEADS // 2
    steps = b * head_pairs
    weights = [w_o, w_up, w_down]
    assert all(w.shape[0] % (steps * BF16_ROWS) == 0 for w in weights)
    w_specs = [pl.BlockSpec((w.shape[0] // steps, w.shape[1]), lambda i, hp: (i * head_pairs + hp, 0))
               for w in weights]
    return pl.pallas_call(
        _mla_kernel,
        grid=(b, head_pairs),
        in_specs=[pl.BlockSpec(memory_space=pltpu.SMEM),
                  pl.BlockSpec((SEQ, 2 * LANES), lambda i, hp: (i, hp)),
                  pl.BlockSpec((SEQ, 2 * LANES), lambda i, hp: (i, hp)),
                  pl.BlockSpec((nv, 2 * V_ROWS, V_TILE), lambda i, hp: (i, hp, 0)),
                  pl.BlockSpec((MLA_Q, MLA_Q), lambda i, hp: (0, 0))] + w_specs,
        out_specs=[pl.BlockSpec((SEQ, LANES), lambda i, hp: (i, hp))] + w_specs,
        out_shape=[jax.ShapeDtypeStruct((q.shape[0], MLA_HEADS * MLA_V), BF16)]
                  + [jax.ShapeDtypeStruct(w.shape, BF16) for w in weights],
        scratch_shapes=[pltpu.VMEM((2, nq, chunks, 1, MLA_Q), F32),
                        pltpu.VMEM((2, nq, chunks, V_ROWS, MLA_Q), F32),
                        pltpu.VMEM((2, 2, MLA_K, MLA_Q), F32)],
        compiler_params=pltpu.CompilerParams(
            dimension_semantics=("parallel", "parallel"), vmem_limit_bytes=VMEM_LIMIT),
        name="mla_attention",
    )(_schedule(*_mla_pairs()), q, k, vt, jnp.eye(MLA_Q, dtype=BF16), *weights)


def _select_kernel(q_ref, kc_ref, vct_ref, tabc_ref, ovt_ref, eye_ref, qpad_ref, qaug_ref, ocmp_ref):
    step = pl.program_id(1)
    t = SELECT_Q
    heads = range(NSA_GROUP)
    tiles = range(t // NSA_Q)

    def group(g):
        q_pad = _dot(q_ref[:, g * 2 * LANES:(g + 1) * 2 * LANES], qpad_ref[...]).astype(BF16)
        low = lax.broadcasted_iota(jnp.int32, (t, LANES), 1) < NSA_HEAD_DIM
        zero = jnp.zeros((t, LANES), BF16)
        pairs = [q_ref[:, (2 * g + j) * LANES:(2 * g + j + 1) * LANES] for j in range(NSA_GROUP // 2)]
        q4 = jnp.concatenate([jnp.where(low == (r % 2 == 0), pairs[r // 2], zero) for r in heads], axis=0)

        per_tile = NSA_Q // CMP_STRIDE
        rows = [pl.ds(pl.multiple_of(CMP_TAB_BASE - (step * len(tiles) + u) * per_tile, per_tile), N_CHUNK)
                for u in tiles]
        bias_c = jnp.concatenate([tabc_ref[g * NSA_GROUP + r, rows[u], :] for r in heads for u in tiles], axis=1)
        valid = bias_c > 0.5 * NEG
        sc = jnp.where(valid, _dot_nt(kc_ref[0, g], q4) + bias_c, NEG)
        mx = jnp.max(sc, axis=0, keepdims=True)
        p = jnp.where(valid, jnp.exp2(sc - mx), 0.0)
        pc = p / jnp.maximum(jnp.sum(p, axis=0, keepdims=True), 1e-30)
        o_cmp = _dot(vct_ref[0, g], pc.astype(BF16)).astype(BF16)
        for u in tiles:
            ocmp_ref[0, g, u] = jnp.concatenate(
                [o_cmp[:, r * t + u * NSA_Q:r * t + (u + 1) * NSA_Q] for r in heads], axis=0)

        psum = sum(pc[:, r * t:(r + 1) * t] for r in heads)
        imp = sum(_dot(ovt_ref[...], piece) for piece in _split(psum, 3))
        jj = lax.broadcasted_iota(jnp.int32, (N_SEL, t), 0)
        blk = (step * t + lax.broadcasted_iota(jnp.int32, (N_SEL, t), 1)) // SEL_BLOCK
        ok = jj <= blk
        forced = ok & ((jj == 0) | (jj == blk) | (jj == blk - 1))
        score = jnp.where(forced, FORCE_SCORE, jnp.where(ok, imp, -jnp.inf))
        rank = jnp.zeros((N_SEL, t), jnp.int32)
        for i in range(N_SEL):
            ci = score[i:i + 1, :]
            beats = (ci > score) | ((ci == score) & (jj > i))
            rank = rank + beats.astype(jnp.int32)
        negsel = jnp.where(ok & (rank < SEL_TOP_N), 0.0, NEG).astype(BF16)
        spread = jnp.concatenate([jnp.zeros((AUG_BLOCK, t), BF16), negsel,
                                  jnp.zeros((LANES - AUG_ONE, t), BF16)], axis=0)
        mask_lanes = _dot_nt(eye_ref[...], spread).astype(BF16)
        lane = lax.broadcasted_iota(jnp.int32, (t, LANES), 1)
        for r in heads:
            far = tabc_ref[g * NSA_GROUP + r, 0:1, :]
            far_hi = far.astype(BF16)
            far_lo = (far - far_hi.astype(F32)).astype(BF16)
            aug = jnp.where(lane == AUG_ONE, far_hi, jnp.where(lane == AUG_ONE + 1, far_lo, mask_lanes))
            q_aug = jnp.where(lane < AUG_BLOCK, q_pad[:, r * LANES:(r + 1) * LANES], aug)
            for u in tiles:
                qaug_ref[0, g, u, r * NSA_Q:(r + 1) * NSA_Q, :] = q_aug[u * NSA_Q:(u + 1) * NSA_Q]

    for g in range(NSA_KV_HEADS):
        group(g)


def _nsa_select(q, kc, vct, tables):
    b = q.shape[0] // SEQ
    t = SELECT_Q
    steps = SEQ // t
    nq = SEQ // NSA_Q
    per = t // NSA_Q
    g_ = NSA_KV_HEADS
    c0 = np.arange(N_CHUNK)[None, :] * CMP_STRIDE
    j0 = np.arange(N_SEL)[:, None] * SEL_BLOCK
    ovt = np.clip(np.minimum(c0 + CMP_BLOCK, j0 + SEL_BLOCK) - np.maximum(c0, j0), 0, None) / CMP_BLOCK
    ovt[:, N_CHUNK - 1:] = 0.0
    return pl.pallas_call(
        _select_kernel,
        grid=(b, steps),
        in_specs=[pl.BlockSpec((t, g_ * 2 * LANES), lambda i, j: (i * steps + j, 0)),
                  pl.BlockSpec((1, g_, N_CHUNK, LANES), lambda i, j: (i, 0, 0, 0)),
                  pl.BlockSpec((1, g_, NSA_HEAD_DIM, N_CHUNK), lambda i, j: (i, 0, 0, 0)),
                  pl.BlockSpec((NSA_HEADS, CMP_TAB_ROWS, LANES), lambda i, j: (0, WIN_ROWS // CMP_TAB_ROWS, 0)),
                  pl.BlockSpec((N_SEL, N_CHUNK), lambda i, j: (0, 0)),
                  pl.BlockSpec((t, t), lambda i, j: (0, 0)),
                  pl.BlockSpec((2 * LANES, NSA_GROUP * LANES), lambda i, j: (0, 0))],
        out_specs=[pl.BlockSpec((1, g_, per, NSA_GROUP * NSA_Q, LANES), lambda i, j: (i, 0, j, 0, 0)),
                   pl.BlockSpec((1, g_, per, NSA_GROUP * NSA_HEAD_DIM, NSA_Q), lambda i, j: (i, 0, j, 0, 0))],
        out_shape=[jax.ShapeDtypeStruct((b, g_, nq, NSA_GROUP * NSA_Q, LANES), BF16),
                   jax.ShapeDtypeStruct((b, g_, nq, NSA_GROUP * NSA_HEAD_DIM, NSA_Q), BF16)],
        compiler_params=pltpu.CompilerParams(
            dimension_semantics=("parallel", "parallel"), vmem_limit_bytes=VMEM_LIMIT),
        name="nsa_select",
    )(q, kc, vct, tables, jnp.asarray(ovt, BF16), jnp.eye(t, dtype=BF16),
      jnp.asarray(_head_padding(NSA_GROUP), BF16))


def _nsa_pairs():
    nq, nk, per = SEQ // NSA_Q, SEQ // NSA_K, NSA_K // NSA_Q
    sel = [(qt, kt) for kt in range(nk) for qt in range(nq) if kt * NSA_K <= qt * NSA_Q]
    far = [(qt, kt) for qt, kt in sel if qt - kt * per >= SEL_FAR_TILE]
    near = [(qt, kt) for qt, kt in sel if qt - kt * per < SEL_FAR_TILE]
    win = [(qt, kt) for kt in range(nk) for qt in range(nq)
           if max(qt * NSA_Q - (WINDOW - 1), 0) // NSA_K <= kt and kt * NSA_K <= qt * NSA_Q]
    return far, near, win


def _attend_kernel(sched_ref, qaug_ref, ksel_ref, kwin_ref, vsel_ref, vwin_ref, ocmp_ref, gate_ref,
                   tabw_ref, eye_ref, o_ref, m_ref, acc_ref, s_ref, p_ref, alpha_ref):
    t = NSA_Q
    heads = range(NSA_GROUP)
    far, near, win = _nsa_pairs()
    _flash_init(m_ref, acc_ref)

    def scores(qt, kt, slot, k_ref, add_bias):
        s = _dot_nt(k_ref[pl.ds(pl.multiple_of(kt * NSA_K, NSA_K), NSA_K), :], qaug_ref[0, 0, qt])
        if add_bias:
            off = qt - kt * (NSA_K // NSA_Q)
            bias_rows = pl.ds(pl.multiple_of(off * NSA_K, NSA_K), NSA_K)
            s = s + jnp.concatenate([tabw_ref[r, bias_rows, :] for r in heads], axis=1)
        s_ref[slot] = s

    def softmax(qt, kt, slot, branch):
        _flash_softmax(s_ref[slot], m_ref.at[branch, qt], p_ref.at[slot], alpha_ref.at[slot], chunk=NSA_K)

    def accumulate(qt, kt, slot, v_ref, branch):
        _flash_accumulate(v_ref[kt], p_ref.at[slot], alpha_ref.at[slot], acc_ref.at[branch, qt], chunk=NSA_K)

    for row, pairs, k_ref, v_ref, branch, add_bias in ((0, far, ksel_ref, vsel_ref, 0, False),
                                                       (2, near, ksel_ref, vsel_ref, 0, True),
                                                       (4, win, kwin_ref, vwin_ref, 1, True)):
        _pipelined_pairs(sched_ref, row, len(pairs),
                         lambda qt, kt, slot, k_ref=k_ref, add_bias=add_bias: scores(qt, kt, slot, k_ref, add_bias),
                         lambda qt, kt, slot, branch=branch: softmax(qt, kt, slot, branch),
                         lambda qt, kt, slot, v_ref=v_ref, branch=branch: accumulate(qt, kt, slot, v_ref, branch),
                         unroll=NSA_UNROLL)

    d = NSA_HEAD_DIM
    group = pl.program_id(1)
    for qt in range(SEQ // t):
        o_cmp = ocmp_ref[0, 0, qt].astype(F32)
        o_sel, o_win = [_flash_finish(m_ref.at[br, qt], acc_ref.at[br, qt]) for br in range(2)]
        mixed = []
        for r in heads:
            gate = [gate_ref[qt, pl.ds(br * NSA_HEADS + group * NSA_GROUP + r, 1), :]
                    for br in range(N_BRANCH)]
            mixed.append(gate[0] * o_cmp[r * d:(r + 1) * d] + gate[1] * o_sel[:, r * t:(r + 1) * t]
                         + gate[2] * o_win[:, r * t:(r + 1) * t])
        o_ref[qt * t:(qt + 1) * t, :] = _dot_nt(eye_ref[...], jnp.concatenate(mixed, axis=0).astype(BF16)).astype(BF16)


def _nsa_attend(qaug, kaug, vt, ocmp, gates, tables):
    b = kaug.shape[0] // SEQ
    t = NSA_Q
    nq = SEQ // t
    nv = SEQ // V_TILE
    g_ = NSA_KV_HEADS
    k_spec = lambda off: pl.BlockSpec((SEQ, LANES), lambda i, g: (i, off + g))
    v_spec = lambda off: pl.BlockSpec((nv, V_ROWS, V_TILE), lambda i, g: (i, off + g, 0))
    return pl.pallas_call(
        _attend_kernel,
        grid=(b, g_),
        in_specs=[pl.BlockSpec(memory_space=pltpu.SMEM),
                  pl.BlockSpec((1, 1, nq, NSA_GROUP * t, LANES), lambda i, g: (i, g, 0, 0, 0)),
                  k_spec(0), k_spec(2), v_spec(0), v_spec(2),
                  pl.BlockSpec((1, 1, nq, NSA_GROUP * NSA_HEAD_DIM, t), lambda i, g: (i, g, 0, 0, 0)),
                  pl.BlockSpec((nq, LANES, t), lambda i, g: (i, 0, 0)),
                  pl.BlockSpec((NSA_GROUP, WIN_ROWS, LANES), lambda i, g: (g, 0, 0)),
                  pl.BlockSpec((t, t), lambda i, g: (0, 0))],
        out_specs=pl.BlockSpec((SEQ, 2 * LANES), lambda i, g: (i, g)),
        out_shape=jax.ShapeDtypeStruct((kaug.shape[0], NSA_HEADS * NSA_HEAD_DIM), BF16),
        scratch_shapes=[pltpu.VMEM((2, nq, 1, 1, NSA_GROUP * t), F32),
                        pltpu.VMEM((2, nq, 1, V_ROWS, NSA_GROUP * t), F32),
                        pltpu.VMEM((2, NSA_K, NSA_GROUP * t), F32),
                        pltpu.VMEM((2, NSA_K, NSA_GROUP * t), BF16),
                        pltpu.VMEM((2, 1, 1, NSA_GROUP * t), F32)],
        compiler_params=pltpu.CompilerParams(
            dimension_semantics=("parallel", "parallel"), vmem_limit_bytes=VMEM_LIMIT),
        name="nsa_attend",
    )(_schedule(*_nsa_pairs()), qaug, kaug, kaug, vt, vt, ocmp, gates, tables, jnp.eye(t, dtype=BF16))


def _tail_kernel(x_ref, oa_ref, ob_ref, ga_ref, gb_ref, wo_ref, ln2_ref, wup_ref, wdn_ref,
                 out_ref, h2_ref):
    width = MLA_HEADS * MLA_V
    a = (_rms(oa_ref[...].astype(F32), width) * ga_ref[...]).astype(BF16)
    b = (_rms(ob_ref[...].astype(F32), width) * gb_ref[...]).astype(BF16)
    x1 = x_ref[...] + _dot(a, wo_ref[:width]) + _dot(b, wo_ref[width:])
    h2_ref[...] = (_rms(x1, D_MODEL) * ln2_ref[...]).astype(BF16)
    out_ref[...] = x1

    def body(f, carry):
        cols = pl.ds(pl.multiple_of(f * FF_CHUNK, FF_CHUNK), FF_CHUNK)
        u = jnp.maximum(_dot(h2_ref[...], wup_ref[:, cols]), 0.0)
        out_ref[...] += _dot((u * u).astype(BF16), wdn_ref[f])
        return carry

    lax.fori_loop(0, D_FF // FF_CHUNK, body, 0)


def _tail(x2d, o_a, o_b, gn_a, gn_b, w_o, ln2_g, w_up, w_down):
    t = x2d.shape[0]
    rows = TAIL_ROWS
    width = MLA_HEADS * MLA_V
    once = lambda a: pl.BlockSpec(a.shape, lambda i: (0,) * a.ndim)
    assert w_o.dtype == w_up.dtype == w_down.dtype == BF16
    consts = [gn_a[None, :], gn_b[None, :], w_o, ln2_g[None, :], w_up,
              w_down.reshape(D_FF // FF_CHUNK, FF_CHUNK, D_MODEL)]
    row_spec = lambda w: pl.BlockSpec((rows, w), lambda i: (i, 0))
    return pl.pallas_call(
        _tail_kernel,
        grid=(t // rows,),
        in_specs=[row_spec(D_MODEL), row_spec(width), row_spec(width)] + [once(a) for a in consts],
        out_specs=row_spec(D_MODEL),
        out_shape=jax.ShapeDtypeStruct((t, D_MODEL), F32),
        scratch_shapes=[pltpu.VMEM((rows, D_MODEL), BF16)],
        compiler_params=pltpu.CompilerParams(dimension_semantics=("parallel",),
                                             vmem_limit_bytes=VMEM_LIMIT),
        name="tail",
    )(x2d, o_a, o_b, *consts)


def kernel(x, ln1_g, w_in, mla_cq_norm_g, mla_ckv_norm_g, mla_w_uq, mla_w_ukv, mla_q_gain, mla_k_gain,
           nsa_q_gain, nsa_k_gain, nsa_cmp_pe, nsa_cmp_w1, nsa_cmp_w2, rel_bias, grp_norm_mla,
           grp_norm_nsa, w_o, ln2_g, w_up, w_down):
    b, s, d = x.shape
    assert (s, d) == (SEQ, D_MODEL) and ln1_g.shape[0] == 1
    x2d = x.reshape(b * s, d)
    tables = _bias_tables(rel_bias)
    q_mla, k_mla, vt_mla, q_nsa, kdup, vt_nsa, kck, kcv, gates = _projections(
        x2d, ln1_g[0], w_in[0], mla_cq_norm_g[0], mla_ckv_norm_g[0], mla_w_uq[0], mla_w_ukv[0],
        mla_q_gain[0], mla_k_gain[0], nsa_q_gain[0], nsa_k_gain[0])
    kc, vct = _compress(kck, kcv, nsa_cmp_pe[0], nsa_cmp_w1[0], nsa_cmp_w2[0], nsa_k_gain[0, 0])
    o_a, w_o16, w_up16, w_down16 = _mla_attention(q_mla, k_mla, vt_mla, w_o[0], w_up[0], w_down[0])
    q_aug, ocmp = _nsa_select(q_nsa, kc, vct, tables)
    o_b = _nsa_attend(q_aug, kdup, vt_nsa, ocmp, gates, tables)
    out = _tail(x2d, o_a, o_b, grp_norm_mla[0], grp_norm_nsa[0], w_o16, ln2_g[0], w_up16, w_down16)
    return out.reshape(b, s, d)
```

```python
import math

import numpy as np
import jax
import jax.numpy as jnp
from jax import lax
from jax.experimental import pallas as pl
from jax.experimental.pallas import tpu as pltpu

F32 = jnp.float32
BF16 = jnp.bfloat16

D_MODEL = 1024
SEQ = 2048
MLA_HEADS = 8
MLA_NOPE = 64
MLA_ROPE = 32
MLA_V = 64
MLA_QK = MLA_NOPE + MLA_ROPE
MLA_Q_RANK = 384
MLA_KV_RANK = 256
ROPE_THETA = 10000.0
NSA_HEADS = 8
NSA_KV_HEADS = 2
NSA_GROUP = NSA_HEADS // NSA_KV_HEADS
NSA_HEAD_DIM = 64
N_BRANCH = 3
CMP_BLOCK = 32
CMP_STRIDE = 16
CMP_HIDDEN = 128
SEL_BLOCK = 64
SEL_TOP_N = 16
WINDOW = 512
FORCE_SCORE = 1e4
REL_BUCKETS = 32
REL_MAX_DIST = 128
D_FF = 4 * D_MODEL
EPS = 1e-6

LANES = 128
SUBLANES = 8
BF16_ROWS = 16
NEG = -1e30
N_CHUNK = SEQ // CMP_STRIDE
N_SEL = SEQ // SEL_BLOCK
NSA_Q = 128
NSA_K = 256
SELECT_Q = 512
N_BIAS_TILES = (WINDOW + NSA_K - NSA_Q) // NSA_Q + 1
SEL_FAR_TILE = (REL_MAX_DIST + NSA_K) // NSA_Q
WIN_ROWS = N_BIAS_TILES * NSA_K
AUG_BLOCK = NSA_HEAD_DIM
AUG_ONE = AUG_BLOCK + N_SEL
CMP_TAB_BASE = (SEQ // NSA_Q - 1) * (NSA_Q // CMP_STRIDE)
CMP_TAB_ROWS = 256
MLA_Q = 256
MLA_K = 512
V_TILE = 256
PROJ_ROWS = 256
PACK_ROWS = 256
TAIL_ROWS = 512
FF_CHUNK = 2048
KEY_CHUNK = 256
PAIR_UNROLL = 6
NSA_UNROLL = 12
V_ROWS = 80
LOG2E = math.log2(math.e)
VMEM_LIMIT = 56 * 1024 * 1024

assert PROJ_ROWS % V_TILE == 0 and MLA_K % V_TILE == 0 and NSA_K == V_TILE


def _dot(a, b):
    return jnp.dot(a, b, preferred_element_type=F32)


def _dot_nt(a, b):
    return lax.dot_general(a, b, (((1,), (1,)), ((), ())), preferred_element_type=F32)


def _split(a, terms):
    pieces = []
    rem = a
    for _ in range(terms):
        piece = rem.astype(BF16)
        pieces.append(piece)
        rem = rem - piece.astype(F32)
    return pieces


def _split_dot(a, b, terms=2):
    return sum(_dot(p, b) for p in _split(a, terms))


def _rms(x, width):
    return x * lax.rsqrt(jnp.sum(x * x, axis=-1, keepdims=True) * (1.0 / width) + EPS)


def _t5_bucket_np(dist):
    n = np.maximum(dist, 0)
    max_exact = REL_BUCKETS // 2
    large = max_exact + (np.log(np.maximum(n, 1).astype(np.float32) / max_exact)
                         / math.log(REL_MAX_DIST / max_exact)
                         * (REL_BUCKETS - max_exact)).astype(np.int32)
    large = np.minimum(large, REL_BUCKETS - 1)
    return np.where(n < max_exact, n, large).astype(np.int32)


def _bias_index_table():
    i = np.arange(NSA_Q)[None, :]
    parts = []
    j = np.arange(NSA_K)[:, None]
    for off in range(N_BIAS_TILES):
        d = off * NSA_Q + i - j
        parts.append(np.where((d >= 0) & (d < WINDOW), _t5_bucket_np(d), -1))
    r = np.arange(CMP_TAB_ROWS)[:, None]
    dist_c = (CMP_TAB_BASE - r) * CMP_STRIDE + i - (CMP_BLOCK - 1)
    parts.append(np.where(dist_c >= 0, _t5_bucket_np(dist_c), -1))
    return np.concatenate(parts, axis=0).astype(np.int32)


def _bias_table_kernel(rb_ref, idx_ref, out_ref):
    idx = idx_ref[...]
    bucket = jnp.maximum(idx, 0)
    is_window = pl.program_id(0) < WIN_ROWS // idx.shape[0]
    for h in range(NSA_HEADS):
        row = rb_ref[h:h + 1, :]
        far = jnp.where(is_window, row[:, REL_BUCKETS - 1:REL_BUCKETS], 0.0)
        values = jnp.broadcast_to((row - far) * LOG2E, idx.shape)
        picked = jnp.take_along_axis(values, bucket, axis=1, mode="promise_in_bounds")
        out_ref[h] = jnp.where(idx >= 0, picked, NEG)


def _bias_tables(rel_bias):
    idx = jnp.asarray(_bias_index_table())
    rows = idx.shape[0]
    blk = CMP_TAB_ROWS
    return pl.pallas_call(
        _bias_table_kernel,
        grid=(rows // blk,),
        in_specs=[pl.BlockSpec((NSA_HEADS, LANES), lambda r: (0, 0)),
                  pl.BlockSpec((blk, LANES), lambda r: (r, 0))],
        out_specs=pl.BlockSpec((NSA_HEADS, blk, LANES), lambda r: (0, r, 0)),
        out_shape=jax.ShapeDtypeStruct((NSA_HEADS, rows, LANES), F32),
        name="bias_tables",
    )(jnp.pad(rel_bias.T, ((0, 0), (0, LANES - REL_BUCKETS))), idx)


ROPE_HALF = MLA_ROPE // 2
MISC_KR = 64


def _rope_tables(scale):
    inv = 1.0 / (ROPE_THETA ** (np.arange(ROPE_HALF, dtype=np.float64) / ROPE_HALF))
    ang = np.arange(SEQ, dtype=np.float64)[:, None] * inv[None, :]
    cos, sin = np.cos(ang), np.sin(ang)
    ones = np.ones((SEQ, MLA_NOPE))
    z32 = np.zeros((SEQ, LANES - MLA_QK))
    z64 = np.zeros((SEQ, MLA_NOPE))
    c = np.concatenate([ones, cos, cos, z32], axis=1) * scale
    s = np.concatenate([z64, -sin, sin, z32], axis=1) * scale
    return jnp.asarray(c, F32), jnp.asarray(s, F32)


def _pair_norm(x, gain_ref):
    low = lax.broadcasted_iota(jnp.int32, (x.shape[0], LANES), 1) < NSA_HEAD_DIM
    outs = []
    for j in range(x.shape[1] // LANES):
        cols = slice(j * LANES, (j + 1) * LANES)
        sl = x[:, cols]
        sq = sl * sl
        tot = jnp.sum(sq, axis=-1, keepdims=True)
        lo = jnp.sum(jnp.where(low, sq, 0.0), axis=-1, keepdims=True)
        rs = jnp.where(low, lax.rsqrt(lo * (1.0 / NSA_HEAD_DIM) + EPS),
                       lax.rsqrt((tot - lo) * (1.0 / NSA_HEAD_DIM) + EPS))
        outs.append(sl * rs * gain_ref[:, cols])
    return jnp.concatenate(outs, axis=1)


def _mla_query_heads(x, gain_ref, c, s):
    in_head = lax.broadcasted_iota(jnp.int32, (x.shape[0], LANES), 1) < MLA_QK
    outs = []
    for h in range(MLA_HEADS):
        cols = slice(h * LANES, (h + 1) * LANES)
        sl = x[:, cols]
        ss = jnp.sum(jnp.where(in_head, sl * sl, 0.0), axis=-1, keepdims=True)
        xn = sl * lax.rsqrt(ss * (1.0 / MLA_QK) + EPS) * gain_ref[:, cols]
        outs.append(xn * c + pltpu.roll(xn, LANES - ROPE_HALF, 1) * s)
    return jnp.concatenate(outs, axis=1)


def _mla_key_heads(x, k_rope, k_swap, c, s):
    in_head = lax.broadcasted_iota(jnp.int32, (x.shape[0], LANES), 1) < MLA_QK
    turned = k_swap * s
    outs = []
    for h in range(MLA_HEADS):
        sl = x[:, h * LANES:(h + 1) * LANES] + k_rope
        ss = jnp.sum(jnp.where(in_head, sl * sl, 0.0), axis=-1, keepdims=True)
        outs.append(lax.rsqrt(ss * (1.0 / MLA_QK) + EPS) * (sl * c + turned))
    return jnp.concatenate(outs, axis=1)


COL_KV = MLA_Q_RANK
COL_MISC = COL_KV + MLA_KV_RANK
COL_SWAP = COL_MISC + LANES
COL_NQ = COL_SWAP + LANES
COL_NK = COL_NQ + NSA_HEADS * NSA_HEAD_DIM
COL_NV = COL_NK + 4 * NSA_HEAD_DIM
COL_CMP = COL_NV + 4 * NSA_HEAD_DIM
COL_END = COL_CMP + 4 * NSA_HEAD_DIM


def _proj_kernel(x_ref, ln1_ref, wp_ref, cqg_ref, ckvg_ref, wuq_ref, wukk_ref, wukvt_ref,
                 qg_ref, kown_ref, kpartner_ref, qc_ref, qs_ref, kc_ref, ks_ref,
                 nq_gain_ref, nk_gain_ref, kpad_ref, kaug_ref, vpick_ref, ones_mla_ref, ones_nsa_ref, eye_ref,
                 qmla_ref, kmla_ref, vmla_ref, qnsa_ref, kdup_ref, vnsa_ref, kck_ref, kcv_ref, gate_ref):
    for sub in range(PROJ_ROWS // V_TILE):
        rows = slice(sub * V_TILE, (sub + 1) * V_TILE)
        x = x_ref[rows, :]
        h = (_rms(x, D_MODEL) * ln1_ref[...]).astype(BF16)
        part = lambda a, b: _dot(h, wp_ref[:, a:b])

        cq = (_rms(part(0, COL_KV), MLA_Q_RANK) * cqg_ref[...]).astype(BF16)
        qmla_ref[rows, :] = _mla_query_heads(_dot(cq, wuq_ref[...]), qg_ref,
                                             qc_ref[rows, :], qs_ref[rows, :]).astype(BF16)

        kv_misc = part(COL_KV, COL_NQ)
        ckv = (_rms(kv_misc[:, :MLA_KV_RANK], MLA_KV_RANK) * ckvg_ref[...]).astype(BF16)
        misc = kv_misc[:, COL_MISC - COL_KV:COL_SWAP - COL_KV]
        lane = lax.broadcasted_iota(jnp.int32, misc.shape, 1)
        k_rope = jnp.where((lane >= MISC_KR) & (lane < MISC_KR + MLA_ROPE), misc, 0.0)
        kmla_ref[rows, :] = _mla_key_heads(_dot(ckv, wukk_ref[...]), k_rope, kv_misc[:, COL_SWAP - COL_KV:],
                                           kc_ref[rows, :] * kown_ref[...],
                                           ks_ref[rows, :] * kpartner_ref[...]).astype(BF16)
        vmla_ref[sub] = (_dot_nt(wukvt_ref[...], ckv) + ones_mla_ref[...]).astype(BF16)

        qnsa_ref[rows, :] = _pair_norm(part(COL_NQ, COL_NK), nq_gain_ref).astype(BF16)

        kn = _pair_norm(part(COL_NK, COL_NV), nk_gain_ref)
        kdup_ref[rows, :] = (_dot(kn.astype(BF16), kpad_ref[...]) + kaug_ref[rows, :]).astype(BF16)
        vnsa_ref[sub] = (_dot_nt(vpick_ref[...], part(COL_NV, COL_CMP).astype(BF16)) + ones_nsa_ref[...]).astype(BF16)
        cmp_kv = part(COL_CMP, COL_END)
        kck_ref[rows, :] = cmp_kv[:, :LANES]
        kcv_ref[rows, :] = cmp_kv[:, LANES:]
        gate = 1.0 / (1.0 + jnp.exp(-misc))
        gate_t = sum(_dot_nt(eye_ref[...], piece) for piece in _split(gate, 2))
        for u in range(V_TILE // NSA_Q):
            gate_ref[sub * (V_TILE // NSA_Q) + u] = gate_t[:, u * NSA_Q:(u + 1) * NSA_Q]


def _head_padding(heads):
    d = NSA_HEAD_DIM
    m = np.zeros((heads * d, heads * LANES), np.float32)
    for hh in range(heads):
        m[hh * d + np.arange(d), hh * LANES + np.arange(d)] = 1.0
    return m


def _packed_segments():
    kv0 = 1184
    seg = lambda a, b: (a, b - a)
    kvb = lambda br, kv, g: (kv0 + ((br * 2 + kv) * 2 + g) * 64, 64)
    zeros = lambda n: (None, n)
    kr = COL_MISC
    gate_cols = N_BRANCH * NSA_HEADS
    pieces = [
        seg(0, COL_MISC),
        seg(1952, 1952 + gate_cols), zeros(MISC_KR - gate_cols), seg(kr, kr + MLA_ROPE),
        zeros(LANES - MISC_KR - MLA_ROPE),
        zeros(MISC_KR), seg(kr + ROPE_HALF, kr + MLA_ROPE), seg(kr, kr + ROPE_HALF), zeros(LANES - MISC_KR - MLA_ROPE),
        seg(672, 1184),
        kvb(1, 0, 0), kvb(1, 0, 1), kvb(2, 0, 0), kvb(2, 0, 1),
        kvb(1, 1, 0), kvb(1, 1, 1), kvb(2, 1, 0), kvb(2, 1, 1),
        kvb(0, 0, 0), kvb(0, 0, 1), kvb(0, 1, 0), kvb(0, 1, 1),
    ]
    assert sum(n for _, n in pieces) == COL_END
    return pieces


def _pack_kernel(wt_ref, out_ref, packed_ref):
    dst = 0
    for src, n in _packed_segments():
        assert n % SUBLANES == 0 and dst % SUBLANES == 0 and (src is None or src % SUBLANES == 0)
        if src is None:
            packed_ref[dst:dst + n, :] = jnp.zeros((n, packed_ref.shape[1]), F32)
        else:
            packed_ref[dst:dst + n, :] = wt_ref[src:src + n, :]
        dst += n
    for g in range(COL_END // LANES):
        cols = slice(g * LANES, (g + 1) * LANES)
        out_ref[:, cols] = packed_ref[cols, :].T.astype(BF16)


def _pack_input_projection(w_in):
    rows = PACK_ROWS
    return pl.pallas_call(
        _pack_kernel,
        grid=(D_MODEL // rows,),
        in_specs=[pl.BlockSpec((w_in.shape[1], rows), lambda i: (0, i))],
        out_specs=pl.BlockSpec((rows, COL_END), lambda i: (i, 0)),
        out_shape=jax.ShapeDtypeStruct((D_MODEL, COL_END), BF16),
        scratch_shapes=[pltpu.VMEM((COL_END, rows), F32)],
        compiler_params=pltpu.CompilerParams(dimension_semantics=("parallel",),
                                             vmem_limit_bytes=VMEM_LIMIT),
        name="pack_input_projection",
    )(w_in.T)


def _projections(x2d, ln1_g, w_in, cq_g, ckv_g, w_uq, w_ukv, q_gain, k_gain, nq_gain, nk_gain):
    t = x2d.shape[0]
    wp = _pack_input_projection(w_in)
    wq3 = w_uq.reshape(MLA_Q_RANK, MLA_HEADS, MLA_QK).astype(BF16)
    wuq = jnp.concatenate([wq3, wq3[:, :, MLA_NOPE:MLA_NOPE + ROPE_HALF],
                           jnp.zeros((MLA_Q_RANK, MLA_HEADS, ROPE_HALF), BF16)], axis=2)
    wuq = wuq.reshape(MLA_Q_RANK, MLA_HEADS * LANES)
    wkv = w_ukv.reshape(MLA_KV_RANK, MLA_HEADS, MLA_NOPE + MLA_V)
    wukk = jnp.pad(wkv[:, :, :MLA_NOPE], ((0, 0), (0, 0), (0, LANES - MLA_NOPE)))
    wukk = wukk.reshape(MLA_KV_RANK, MLA_HEADS * LANES).astype(BF16)
    wukvt = jnp.pad(wkv[:, :, MLA_NOPE:], ((0, 0), (0, 0), (0, V_ROWS - MLA_V)))
    wukvt = wukvt.reshape(MLA_KV_RANK, MLA_HEADS * V_ROWS).T.astype(BF16)
    q_slab_gain = jnp.tile(jnp.concatenate(
        [q_gain, q_gain[MLA_NOPE:MLA_NOPE + ROPE_HALF], jnp.zeros((ROPE_HALF,), F32)]), MLA_HEADS)[None, :]
    qc, qs = _rope_tables(MLA_QK ** -0.5 * LOG2E)
    kc, ks = _rope_tables(1.0)
    k_own = jnp.pad(k_gain, (0, LANES - MLA_QK))[None, :]
    k_partner = jnp.concatenate([k_gain[:MLA_NOPE], k_gain[MLA_NOPE + ROPE_HALF:],
                                 k_gain[MLA_NOPE:MLA_NOPE + ROPE_HALF], jnp.zeros((LANES - MLA_QK,), F32)])[None, :]
    kpad = _head_padding(4)
    kaug = np.zeros((SEQ, 4 * LANES), np.float32)
    for s in range(4):
        kaug[:, s * LANES + AUG_ONE:s * LANES + AUG_ONE + 2] = 1.0
        if s < NSA_KV_HEADS:
            kaug[np.arange(SEQ), s * LANES + AUG_BLOCK + np.arange(SEQ) // SEL_BLOCK] = 1.0
    nqg = (jnp.tile(nq_gain, NSA_HEADS) * (NSA_HEAD_DIM ** -0.5 * LOG2E))[None, :]
    nkg = jnp.concatenate([nk_gain[1], nk_gain[1], nk_gain[2], nk_gain[2]])[None, :]
    vpick = np.zeros((4 * V_ROWS, 256), np.float32)
    for s in range(4):
        vpick[s * V_ROWS + np.arange(64), s * 64 + np.arange(64)] = 1.0

    def ones_col(heads):
        col = np.zeros((heads * V_ROWS, 1), np.float32)
        col[np.arange(heads) * V_ROWS + MLA_V] = 1.0
        return col

    rows = PROJ_ROWS
    n_pos = SEQ // rows
    full = lambda a: pl.BlockSpec(a.shape, lambda i: (0,) * a.ndim)
    tab = pl.BlockSpec((rows, LANES), lambda i: (i % n_pos, 0))
    consts = [ln1_g[None, :], wp, cq_g[None, :], ckv_g[None, :], wuq, wukk, wukvt, q_slab_gain, k_own, k_partner]
    tabs = [qc, qs, kc, ks]
    mats = [nqg, nkg, jnp.asarray(kpad, BF16),
            jnp.asarray(vpick, BF16), jnp.asarray(ones_col(MLA_HEADS)), jnp.asarray(ones_col(4)),
            jnp.eye(LANES, dtype=BF16)]
    row_spec = lambda w: pl.BlockSpec((rows, w), lambda i: (i, 0))
    vt_spec = lambda w, tile: pl.BlockSpec((rows // tile, w, tile), lambda i: (i, 0, 0))
    sds = jax.ShapeDtypeStruct
    return pl.pallas_call(
        _proj_kernel,
        grid=(t // rows,),
        in_specs=([row_spec(D_MODEL)] + [full(a) for a in consts] + [tab] * 4 + [full(a) for a in mats[:3]]
                  + [pl.BlockSpec((rows, 4 * LANES), lambda i: (i % n_pos, 0))] + [full(a) for a in mats[3:]]),
        out_specs=[row_spec(1024), row_spec(1024), vt_spec(MLA_HEADS * V_ROWS, V_TILE), row_spec(512),
                   row_spec(512), vt_spec(4 * V_ROWS, V_TILE), row_spec(128), row_spec(128),
                   vt_spec(LANES, NSA_Q)],
        out_shape=[sds((t, 1024), BF16), sds((t, 1024), BF16), sds((t // V_TILE, MLA_HEADS * V_ROWS, V_TILE), BF16),
                   sds((t, 512), BF16), sds((t, 512), BF16), sds((t // V_TILE, 4 * V_ROWS, V_TILE), BF16),
                   sds((t, 128), F32), sds((t, 128), F32), sds((t // NSA_Q, LANES, NSA_Q), F32)],
        compiler_params=pltpu.CompilerParams(dimension_semantics=("parallel",),
                                             vmem_limit_bytes=VMEM_LIMIT),
        name="projections",
    )(x2d, *consts, *tabs, *mats[:3], jnp.asarray(kaug, BF16), *mats[3:])


def _compress_kernel(xk_ref, xv_ref, pe_ref, w1_ref, w2_ref, kg_ref, dupk_ref, pickv_ref, kc_ref, vct_ref):
    def mlp(x_ref, which):
        top = bot = None
        for t in range(CMP_STRIDE):
            xt = x_ref[pl.ds(t, N_CHUNK, stride=CMP_STRIDE), :]
            a = _dot((xt + pe_ref[which, t]).astype(BF16), w1_ref[which, t])
            b = _dot((xt + pe_ref[which, CMP_STRIDE + t]).astype(BF16), w1_ref[which, CMP_STRIDE + t])
            top = a if top is None else top + a
            bot = b if bot is None else bot + b
        hid = top + pltpu.roll(bot, N_CHUNK - 1, 0)
        act = (hid * (1.0 / (1.0 + jnp.exp(-hid)))).astype(BF16)
        return _dot(act, w2_ref[which])

    k = _pair_norm(mlp(xk_ref, 0), kg_ref).astype(BF16)
    v = mlp(xv_ref, 1).astype(BF16)
    for g in range(NSA_KV_HEADS):
        kc_ref[0, g] = _dot(k, dupk_ref[g]).astype(BF16)
        vct_ref[0, g] = _dot_nt(pickv_ref[g], v).astype(BF16)


def _compress(kck, kcv, cmp_pe, cmp_w1, cmp_w2, k_gain0):
    b = kck.shape[0] // SEQ
    g_ = NSA_KV_HEADS
    d = NSA_HEAD_DIM
    pe = jnp.tile(cmp_pe, (1, 1, g_))[:, :, None, :]
    w1 = cmp_w1.reshape(2, CMP_BLOCK, d, CMP_HIDDEN)
    z1 = jnp.zeros_like(w1)
    w1 = jnp.concatenate([jnp.concatenate([w1, z1], axis=3), jnp.concatenate([z1, w1], axis=3)], axis=2)
    z2 = jnp.zeros_like(cmp_w2)
    w2 = jnp.concatenate([jnp.concatenate([cmp_w2, z2], axis=2), jnp.concatenate([z2, cmp_w2], axis=2)], axis=1)
    dupk = np.zeros((g_, LANES, LANES), np.float32)
    pickv = np.zeros((g_, d, LANES), np.float32)
    for g in range(g_):
        dupk[g, g * d + np.arange(d), np.arange(d)] = 1.0
        dupk[g, g * d + np.arange(d), d + np.arange(d)] = 1.0
        pickv[g, np.arange(d), g * d + np.arange(d)] = 1.0
    full = lambda a: pl.BlockSpec(a.shape, lambda i: (0,) * a.ndim)
    consts = [pe, w1.astype(BF16), w2.astype(BF16), jnp.tile(k_gain0, g_)[None, :],
              jnp.asarray(dupk, BF16), jnp.asarray(pickv, BF16)]
    x_spec = pl.BlockSpec((SEQ, LANES), lambda i: (i, 0))
    return pl.pallas_call(
        _compress_kernel,
        grid=(b,),
        in_specs=[x_spec, x_spec] + [full(a) for a in consts],
        out_specs=[pl.BlockSpec((1, g_, N_CHUNK, LANES), lambda i: (i, 0, 0, 0)),
                   pl.BlockSpec((1, g_, d, N_CHUNK), lambda i: (i, 0, 0, 0))],
        out_shape=[jax.ShapeDtypeStruct((b, g_, N_CHUNK, LANES), BF16),
                   jax.ShapeDtypeStruct((b, g_, d, N_CHUNK), BF16)],
        compiler_params=pltpu.CompilerParams(dimension_semantics=("parallel",)),
        name="compression",
    )(kck, kcv, *consts)


def _flash_init(m_ref, acc_ref):
    m_ref[...] = jnp.full(m_ref.shape, NEG, F32)
    acc_ref[...] = jnp.zeros(acc_ref.shape, F32)


def _flash_step(s, v, m_ref, acc_ref, chunk=KEY_CHUNK):
    for c in range(s.shape[0] // chunk):
        sc = s[c * chunk:(c + 1) * chunk]
        m_old = m_ref[c]
        m_new = jnp.maximum(m_old, jnp.max(sc, axis=0, keepdims=True))
        p = jnp.exp2(sc - m_new).astype(BF16)
        acc_ref[c] = jnp.exp2(m_old - m_new) * acc_ref[c] + _dot(v[:, c * chunk:(c + 1) * chunk], p)
        m_ref[c] = m_new


def _flash_softmax(s, m_ref, p_ref, alpha_ref, chunk=KEY_CHUNK):
    for c in range(s.shape[0] // chunk):
        sc = s[c * chunk:(c + 1) * chunk]
        m_old = m_ref[c]
        m_new = jnp.maximum(m_old, jnp.max(sc, axis=0, keepdims=True))
        alpha_ref[c] = jnp.exp2(m_old - m_new)
        p_ref[c * chunk:(c + 1) * chunk, :] = jnp.exp2(sc - m_new).astype(BF16)
        m_ref[c] = m_new


def _flash_accumulate(v, p_ref, alpha_ref, acc_ref, chunk=KEY_CHUNK):
    for c in range(p_ref.shape[0] // chunk):
        rows = slice(c * chunk, (c + 1) * chunk)
        acc_ref[c] = alpha_ref[c] * acc_ref[c] + _dot(v[:, rows], p_ref[rows, :])


def _flash_finish(m_ref, acc_ref):
    n = m_ref.shape[0]
    m = m_ref[0]
    for c in range(1, n):
        m = jnp.maximum(m, m_ref[c])
    acc = sum(jnp.exp2(m_ref[c] - m) * acc_ref[c] for c in range(n))
    return acc[:MLA_V] / acc[MLA_V:MLA_V + 1]


def _pipelined_pairs(sched_ref, row, n, scores, softmax, accumulate, unroll=PAIR_UNROLL):
    assert n >= 2 and unroll % 2 == 0

    def pair(i):
        i = jnp.minimum(i, n - 1)
        return sched_ref[row, i], sched_ref[row + 1, i]

    def steps(first, count):
        for u in range(count):
            i, slot = first + u, u % 2
            if accumulate is None:
                scores(*pair(i + 1), 1 - slot)
                softmax(*pair(i), slot)
            else:
                scores(*pair(i + 2), slot)
                softmax(*pair(i + 1), 1 - slot)
                accumulate(*pair(i), slot)

    scores(*pair(0), 0)
    if accumulate is not None:
        scores(*pair(1), 1)
        softmax(*pair(0), 0)
    loops = n // unroll
    lax.fori_loop(0, loops, lambda j, carry: (steps(j * unroll, unroll), carry)[1], 0)
    steps(loops * unroll, n - loops * unroll)


def _mla_pairs():
    nq, nk = SEQ // MLA_Q, SEQ // MLA_K
    full = [(qt, kt) for kt in range(nk) for qt in range(nq) if (kt + 1) * MLA_K <= qt * MLA_Q]
    diag = [(qt, (qt * MLA_Q) // MLA_K) for qt in range(nq)]
    return full, diag


def _mla_kernel(sched_ref, q_ref, k_ref, vt_ref, eye_ref, wo_ref, wup_ref, wdn_ref,
                o_ref, wo16_ref, wup16_ref, wdn16_ref, m_ref, acc_ref, s_ref):
    for w_ref, w16_ref in ((wo_ref, wo16_ref), (wup_ref, wup16_ref), (wdn_ref, wdn16_ref)):
        w16_ref[...] = w_ref[...].astype(BF16)
    sub = MLA_K // V_TILE
    full, diag = _mla_pairs()
    rel = (lax.broadcasted_iota(jnp.int32, (MLA_K, MLA_Q), 1)
           - lax.broadcasted_iota(jnp.int32, (MLA_K, MLA_Q), 0))
    _flash_init(m_ref, acc_ref)

    def scores(qt, kt, slot, masked):
        qrows = pl.ds(pl.multiple_of(qt * MLA_Q, MLA_Q), MLA_Q)
        krows = pl.ds(pl.multiple_of(kt * MLA_K, MLA_K), MLA_K)
        for hh in range(2):
            cols = slice(hh * LANES, (hh + 1) * LANES)
            s = _dot_nt(k_ref[krows, cols], q_ref[qrows, cols])
            if masked:
                s = jnp.where(rel >= kt * MLA_K - qt * MLA_Q, s, NEG)
            s_ref[slot, hh] = s

    def update(qt, kt, slot):
        for hh in range(2):
            v = jnp.concatenate([vt_ref[kt * sub + i, hh * V_ROWS:(hh + 1) * V_ROWS, :] for i in range(sub)],
                                axis=1)
            _flash_step(s_ref[slot, hh], v, m_ref.at[hh, qt], acc_ref.at[hh, qt])

    _pipelined_pairs(sched_ref, 0, len(full), lambda qt, kt, slot: scores(qt, kt, slot, False), update, None)
    _pipelined_pairs(sched_ref, 2, len(diag), lambda qt, kt, slot: scores(qt, kt, slot, True), update, None)

    for qt in range(SEQ // MLA_Q):
        o_t = jnp.concatenate([_flash_finish(m_ref.at[hh, qt], acc_ref.at[hh, qt])
                               for hh in range(2)], axis=0)
        o_ref[qt * MLA_Q:(qt + 1) * MLA_Q, :] = _dot_nt(eye_ref[...], o_t.astype(BF16)).astype(BF16)


def _schedule(*pair_lists):
    n = max(len(p) for p in pair_lists)
    out = np.zeros((2 * len(pair_lists), n), np.int32)
    for i, pairs in enumerate(pair_lists):
        out[2 * i, :len(pairs)] = [a for a, _ in pairs]
        out[2 * i + 1, :len(pairs)] = [b for _, b in pairs]
    return jnp.asarray(out)


def _mla_attention(q, k, vt, w_o, w_up, w_down):
    b = q.shape[0] // SEQ
    nq = SEQ // MLA_Q
    nv = SEQ // V_TILE
    chunks = MLA_K // KEY_CHUNK
    head_pairs = MLA_HEADS // 2
    steps = b * head_pairs
    weights = [w_o, w_up, w_down]
    assert all(w.shape[0] % (steps * BF16_ROWS) == 0 for w in weights)
    w_specs = [pl.BlockSpec((w.shape[0] // steps, w.shape[1]), lambda i, hp: (i * head_pairs + hp, 0))
               for w in weights]
    return pl.pallas_call(
        _mla_kernel,
        grid=(b, head_pairs),
        in_specs=[pl.BlockSpec(memory_space=pltpu.SMEM),
                  pl.BlockSpec((SEQ, 2 * LANES), lambda i, hp: (i, hp)),
                  pl.BlockSpec((SEQ, 2 * LANES), lambda i, hp: (i, hp)),
                  pl.BlockSpec((nv, 2 * V_ROWS, V_TILE), lambda i, hp: (i, hp, 0)),
                  pl.BlockSpec((MLA_Q, MLA_Q), lambda i, hp: (0, 0))] + w_specs,
        out_specs=[pl.BlockSpec((SEQ, LANES), lambda i, hp: (i, hp))] + w_specs,
        out_shape=[jax.ShapeDtypeStruct((q.shape[0], MLA_HEADS * MLA_V), BF16)]
                  + [jax.ShapeDtypeStruct(w.shape, BF16) for w in weights],
        scratch_shapes=[pltpu.VMEM((2, nq, chunks, 1, MLA_Q), F32),
                        pltpu.VMEM((2, nq, chunks, V_ROWS, MLA_Q), F32),
                        pltpu.VMEM((2, 2, MLA_K, MLA_Q), F32)],
        compiler_params=pltpu.CompilerParams(
            dimension_semantics=("parallel", "parallel"), vmem_limit_bytes=VMEM_LIMIT),
        name="mla_attention",
    )(_schedule(*_mla_pairs()), q, k, vt, jnp.eye(MLA_Q, dtype=BF16), *weights)


def _select_kernel(q_ref, kc_ref, vct_ref, tabc_ref, ovt_ref, eye_ref, qpad_ref, qaug_ref, ocmp_ref):
    step = pl.program_id(1)
    t = SELECT_Q
    heads = range(NSA_GROUP)
    tiles = range(t // NSA_Q)

    def group(g):
        q_pad = _dot(q_ref[:, g * 2 * LANES:(g + 1) * 2 * LANES], qpad_ref[...]).astype(BF16)
        low = lax.broadcasted_iota(jnp.int32, (t, LANES), 1) < NSA_HEAD_DIM
        zero = jnp.zeros((t, LANES), BF16)
        pairs = [q_ref[:, (2 * g + j) * LANES:(2 * g + j + 1) * LANES] for j in range(NSA_GROUP // 2)]
        q4 = jnp.concatenate([jnp.where(low == (r % 2 == 0), pairs[r // 2], zero) for r in heads], axis=0)

        per_tile = NSA_Q // CMP_STRIDE
        rows = [pl.ds(pl.multiple_of(CMP_TAB_BASE - (step * len(tiles) + u) * per_tile, per_tile), N_CHUNK)
                for u in tiles]
        bias_c = jnp.concatenate([tabc_ref[g * NSA_GROUP + r, rows[u], :] for r in heads for u in tiles], axis=1)
        valid = bias_c > 0.5 * NEG
        sc = _dot_nt(kc_ref[0, g], q4) + bias_c
        mx = jnp.max(sc, axis=0, keepdims=True)
        p = jnp.where(valid, jnp.exp2(sc - mx), 0.0)
        pc = p / jnp.maximum(jnp.sum(p, axis=0, keepdims=True), 1e-30)
        o_cmp = _dot(vct_ref[0, g], pc.astype(BF16)).astype(BF16)
        for u in tiles:
            ocmp_ref[0, g, u] = jnp.concatenate(
                [o_cmp[:, r * t + u * NSA_Q:r * t + (u + 1) * NSA_Q] for r in heads], axis=0)

        psum = sum(pc[:, r * t:(r + 1) * t] for r in heads)
        imp = sum(_dot(ovt_ref[...], piece) for piece in _split(psum, 3))
        jj = lax.broadcasted_iota(jnp.int32, (N_SEL, t), 0)
        blk = (step * t + lax.broadcasted_iota(jnp.int32, (N_SEL, t), 1)) // SEL_BLOCK
        ok = jj <= blk
        forced = ok & ((jj == 0) | (jj == blk) | (jj == blk - 1))
        score = jnp.where(forced, FORCE_SCORE, jnp.where(ok, imp, -jnp.inf))
        rank = jnp.zeros((N_SEL, t), jnp.int32)
        for i in range(N_SEL):
            ci = score[i:i + 1, :]
            beats = (ci > score) | ((ci == score) & (jj > i))
            rank = rank + beats.astype(jnp.int32)
        negsel = jnp.where(ok & (rank < SEL_TOP_N), 0.0, NEG).astype(BF16)
        spread = jnp.concatenate([jnp.zeros((AUG_BLOCK, t), BF16), negsel,
                                  jnp.zeros((LANES - AUG_ONE, t), BF16)], axis=0)
        mask_lanes = _dot_nt(eye_ref[...], spread).astype(BF16)
        lane = lax.broadcasted_iota(jnp.int32, (t, LANES), 1)
        for r in heads:
            far = tabc_ref[g * NSA_GROUP + r, 0:1, :]
            far_hi = far.astype(BF16)
            far_lo = (far - far_hi.astype(F32)).astype(BF16)
            aug = jnp.where(lane == AUG_ONE, far_hi, jnp.where(lane == AUG_ONE + 1, far_lo, mask_lanes))
            q_aug = jnp.where(lane < AUG_BLOCK, q_pad[:, r * LANES:(r + 1) * LANES], aug)
            for u in tiles:
                qaug_ref[0, g, u, r * NSA_Q:(r + 1) * NSA_Q, :] = q_aug[u * NSA_Q:(u + 1) * NSA_Q]

    for g in range(NSA_KV_HEADS):
        group(g)


def _nsa_select(q, kc, vct, tables):
    b = q.shape[0] // SEQ
    t = SELECT_Q
    steps = SEQ // t
    nq = SEQ // NSA_Q
    per = t // NSA_Q
    g_ = NSA_KV_HEADS
    c0 = np.arange(N_CHUNK)[None, :] * CMP_STRIDE
    j0 = np.arange(N_SEL)[:, None] * SEL_BLOCK
    ovt = np.clip(np.minimum(c0 + CMP_BLOCK, j0 + SEL_BLOCK) - np.maximum(c0, j0), 0, None) / CMP_BLOCK
    ovt[:, N_CHUNK - 1:] = 0.0
    return pl.pallas_call(
        _select_kernel,
        grid=(b, steps),
        in_specs=[pl.BlockSpec((t, g_ * 2 * LANES), lambda i, j: (i * steps + j, 0)),
                  pl.BlockSpec((1, g_, N_CHUNK, LANES), lambda i, j: (i, 0, 0, 0)),
                  pl.BlockSpec((1, g_, NSA_HEAD_DIM, N_CHUNK), lambda i, j: (i, 0, 0, 0)),
                  pl.BlockSpec((NSA_HEADS, CMP_TAB_ROWS, LANES), lambda i, j: (0, WIN_ROWS // CMP_TAB_ROWS, 0)),
                  pl.BlockSpec((N_SEL, N_CHUNK), lambda i, j: (0, 0)),
                  pl.BlockSpec((t, t), lambda i, j: (0, 0)),
                  pl.BlockSpec((2 * LANES, NSA_GROUP * LANES), lambda i, j: (0, 0))],
        out_specs=[pl.BlockSpec((1, g_, per, NSA_GROUP * NSA_Q, LANES), lambda i, j: (i, 0, j, 0, 0)),
                   pl.BlockSpec((1, g_, per, NSA_GROUP * NSA_HEAD_DIM, NSA_Q), lambda i, j: (i, 0, j, 0, 0))],
        out_shape=[jax.ShapeDtypeStruct((b, g_, nq, NSA_GROUP * NSA_Q, LANES), BF16),
                   jax.ShapeDtypeStruct((b, g_, nq, NSA_GROUP * NSA_HEAD_DIM, NSA_Q), BF16)],
        compiler_params=pltpu.CompilerParams(
            dimension_semantics=("parallel", "parallel"), vmem_limit_bytes=VMEM_LIMIT),
        name="nsa_select",
    )(q, kc, vct, tables, jnp.asarray(ovt, BF16), jnp.eye(t, dtype=BF16),
      jnp.asarray(_head_padding(NSA_GROUP), BF16))


def _nsa_pairs():
    nq, nk, per = SEQ // NSA_Q, SEQ // NSA_K, NSA_K // NSA_Q
    sel = [(qt, kt) for kt in range(nk) for qt in range(nq) if kt * NSA_K <= qt * NSA_Q]
    far = [(qt, kt) for qt, kt in sel if qt - kt * per >= SEL_FAR_TILE]
    near = [(qt, kt) for qt, kt in sel if qt - kt * per < SEL_FAR_TILE]
    win = [(qt, kt) for kt in range(nk) for qt in range(nq)
           if max(qt * NSA_Q - (WINDOW - 1), 0) // NSA_K <= kt and kt * NSA_K <= qt * NSA_Q]
    return far, near, win


def _attend_kernel(sched_ref, qaug_ref, ksel_ref, kwin_ref, vsel_ref, vwin_ref, ocmp_ref, gate_ref,
                   tabw_ref, eye_ref, o_ref, m_ref, acc_ref, s_ref, p_ref, alpha_ref):
    t = NSA_Q
    heads = range(NSA_GROUP)
    far, near, win = _nsa_pairs()
    _flash_init(m_ref, acc_ref)

    def scores(qt, kt, slot, k_ref, add_bias):
        s = _dot_nt(k_ref[pl.ds(pl.multiple_of(kt * NSA_K, NSA_K), NSA_K), :], qaug_ref[0, 0, qt])
        if add_bias:
            off = qt - kt * (NSA_K // NSA_Q)
            bias_rows = pl.ds(pl.multiple_of(off * NSA_K, NSA_K), NSA_K)
            s = s + jnp.concatenate([tabw_ref[r, bias_rows, :] for r in heads], axis=1)
        s_ref[slot] = s

    def softmax(qt, kt, slot, branch):
        _flash_softmax(s_ref[slot], m_ref.at[branch, qt], p_ref.at[slot], alpha_ref.at[slot], chunk=NSA_K)

    def accumulate(qt, kt, slot, v_ref, branch):
        _flash_accumulate(v_ref[kt], p_ref.at[slot], alpha_ref.at[slot], acc_ref.at[branch, qt], chunk=NSA_K)

    for row, pairs, k_ref, v_ref, branch, add_bias in ((0, far, ksel_ref, vsel_ref, 0, False),
                                                       (2, near, ksel_ref, vsel_ref, 0, True),
                                                       (4, win, kwin_ref, vwin_ref, 1, True)):
        _pipelined_pairs(sched_ref, row, len(pairs),
                         lambda qt, kt, slot, k_ref=k_ref, add_bias=add_bias: scores(qt, kt, slot, k_ref, add_bias),
                         lambda qt, kt, slot, branch=branch: softmax(qt, kt, slot, branch),
                         lambda qt, kt, slot, v_ref=v_ref, branch=branch: accumulate(qt, kt, slot, v_ref, branch),
                         unroll=NSA_UNROLL)

    d = NSA_HEAD_DIM
    group = pl.program_id(1)
    for qt in range(SEQ // t):
        o_cmp = ocmp_ref[0, 0, qt].astype(F32)
        o_sel, o_win = [_flash_finish(m_ref.at[br, qt], acc_ref.at[br, qt]) for br in range(2)]
        mixed = []
        for r in heads:
            gate = [gate_ref[qt, pl.ds(br * NSA_HEADS + group * NSA_GROUP + r, 1), :]
                    for br in range(N_BRANCH)]
            mixed.append(gate[0] * o_cmp[r * d:(r + 1) * d] + gate[1] * o_sel[:, r * t:(r + 1) * t]
                         + gate[2] * o_win[:, r * t:(r + 1) * t])
        o_ref[qt * t:(qt + 1) * t, :] = _dot_nt(eye_ref[...], jnp.concatenate(mixed, axis=0).astype(BF16)).astype(BF16)


def _nsa_attend(qaug, kaug, vt, ocmp, gates, tables):
    b = kaug.shape[0] // SEQ
    t = NSA_Q
    nq = SEQ // t
    nv = SEQ // V_TILE
    g_ = NSA_KV_HEADS
    k_spec = lambda off: pl.BlockSpec((SEQ, LANES), lambda i, g: (i, off + g))
    v_spec = lambda off: pl.BlockSpec((nv, V_ROWS, V_TILE), lambda i, g: (i, off + g, 0))
    return pl.pallas_call(
        _attend_kernel,
        grid=(b, g_),
        in_specs=[pl.BlockSpec(memory_space=pltpu.SMEM),
                  pl.BlockSpec((1, 1, nq, NSA_GROUP * t, LANES), lambda i, g: (i, g, 0, 0, 0)),
                  k_spec(0), k_spec(2), v_spec(0), v_spec(2),
                  pl.BlockSpec((1, 1, nq, NSA_GROUP * NSA_HEAD_DIM, t), lambda i, g: (i, g, 0, 0, 0)),
                  pl.BlockSpec((nq, LANES, t), lambda i, g: (i, 0, 0)),
                  pl.BlockSpec((NSA_GROUP, WIN_ROWS, LANES), lambda i, g: (g, 0, 0)),
                  pl.BlockSpec((t, t), lambda i, g: (0, 0))],
        out_specs=pl.BlockSpec((SEQ, 2 * LANES), lambda i, g: (i, g)),
        out_shape=jax.ShapeDtypeStruct((kaug.shape[0], NSA_HEADS * NSA_HEAD_DIM), BF16),
        scratch_shapes=[pltpu.VMEM((2, nq, 1, 1, NSA_GROUP * t), F32),
                        pltpu.VMEM((2, nq, 1, V_ROWS, NSA_GROUP * t), F32),
                        pltpu.VMEM((2, NSA_K, NSA_GROUP * t), F32),
                        pltpu.VMEM((2, NSA_K, NSA_GROUP * t), BF16),
                        pltpu.VMEM((2, 1, 1, NSA_GROUP * t), F32)],
        compiler_params=pltpu.CompilerParams(
            dimension_semantics=("parallel", "parallel"), vmem_limit_bytes=VMEM_LIMIT),
        name="nsa_attend",
    )(_schedule(*_nsa_pairs()), qaug, kaug, kaug, vt, vt, ocmp, gates, tables, jnp.eye(t, dtype=BF16))


def _tail_kernel(x_ref, oa_ref, ob_ref, ga_ref, gb_ref, wo_ref, ln2_ref, wup_ref, wdn_ref,
                 out_ref, h2_ref):
    width = MLA_HEADS * MLA_V
    a = (_rms(oa_ref[...].astype(F32), width) * ga_ref[...]).astype(BF16)
    b = (_rms(ob_ref[...].astype(F32), width) * gb_ref[...]).astype(BF16)
    x1 = x_ref[...] + _dot(a, wo_ref[:width]) + _dot(b, wo_ref[width:])
    h2_ref[...] = (_rms(x1, D_MODEL) * ln2_ref[...]).astype(BF16)
    out_ref[...] = x1

    def body(f, carry):
        cols = pl.ds(pl.multiple_of(f * FF_CHUNK, FF_CHUNK), FF_CHUNK)
        u = jnp.maximum(_dot(h2_ref[...], wup_ref[:, cols]), 0.0)
        out_ref[...] += _dot((u * u).astype(BF16), wdn_ref[f])
        return carry

    lax.fori_loop(0, D_FF // FF_CHUNK, body, 0)


def _tail(x2d, o_a, o_b, gn_a, gn_b, w_o, ln2_g, w_up, w_down):
    t = x2d.shape[0]
    rows = TAIL_ROWS
    width = MLA_HEADS * MLA_V
    once = lambda a: pl.BlockSpec(a.shape, lambda i: (0,) * a.ndim)
    assert w_o.dtype == w_up.dtype == w_down.dtype == BF16
    consts = [gn_a[None, :], gn_b[None, :], w_o, ln2_g[None, :], w_up,
              w_down.reshape(D_FF // FF_CHUNK, FF_CHUNK, D_MODEL)]
    row_spec = lambda w: pl.BlockSpec((rows, w), lambda i: (i, 0))
    return pl.pallas_call(
        _tail_kernel,
        grid=(t // rows,),
        in_specs=[row_spec(D_MODEL), row_spec(width), row_spec(width)] + [once(a) for a in consts],
        out_specs=row_spec(D_MODEL),
        out_shape=jax.ShapeDtypeStruct((t, D_MODEL), F32),
        scratch_shapes=[pltpu.VMEM((rows, D_MODEL), BF16)],
        compiler_params=pltpu.CompilerParams(dimension_semantics=("parallel",),
                                             vmem_limit_bytes=VMEM_LIMIT),
        name="tail",
    )(x2d, o_a, o_b, *consts)


def kernel(x, ln1_g, w_in, mla_cq_norm_g, mla_ckv_norm_g, mla_w_uq, mla_w_ukv, mla_q_gain, mla_k_gain,
           nsa_q_gain, nsa_k_gain, nsa_cmp_pe, nsa_cmp_w1, nsa_cmp_w2, rel_bias, grp_norm_mla,
           grp_norm_nsa, w_o, ln2_g, w_up, w_down):
    b, s, d = x.shape
    assert (s, d) == (SEQ, D_MODEL) and ln1_g.shape[0] == 1
    x2d = x.reshape(b * s, d)
    tables = _bias_tables(rel_bias)
    q_mla, k_mla, vt_mla, q_nsa, kdup, vt_nsa, kck, kcv, gates = _projections(
        x2d, ln1_g[0], w_in[0], mla_cq_norm_g[0], mla_ckv_norm_g[0], mla_w_uq[0], mla_w_ukv[0],
        mla_q_gain[0], mla_k_gain[0], nsa_q_gain[0], nsa_k_gain[0])
    kc, vct = _compress(kck, kcv, nsa_cmp_pe[0], nsa_cmp_w1[0], nsa_cmp_w2[0], nsa_k_gain[0, 0])
    o_a, w_o16, w_up16, w_down16 = _mla_attention(q_mla, k_mla, vt_mla, w_o[0], w_up[0], w_down[0])
    q_aug, ocmp = _nsa_select(q_nsa, kc, vct, tables)
    o_b = _nsa_attend(q_aug, kdup, vt_nsa, ocmp, gates, tables)
    out = _tail(x2d, o_a, o_b, grp_norm_mla[0], grp_norm_nsa[0], w_o16, ln2_g[0], w_up16, w_down16)
    return out.reshape(b, s, d)
```

```python
import math

import numpy as np
import jax
import jax.numpy as jnp
from jax import lax
from jax.experimental import pallas as pl
from jax.experimental.pallas import tpu as pltpu

F32 = jnp.float32
BF16 = jnp.bfloat16

D_MODEL = 1024
SEQ = 2048
MLA_HEADS = 8
MLA_NOPE = 64
MLA_ROPE = 32
MLA_V = 64
MLA_QK = MLA_NOPE + MLA_ROPE
MLA_Q_RANK = 384
MLA_KV_RANK = 256
ROPE_THETA = 10000.0
NSA_HEADS = 8
NSA_KV_HEADS = 2
NSA_GROUP = NSA_HEADS // NSA_KV_HEADS
NSA_HEAD_DIM = 64
N_BRANCH = 3
CMP_BLOCK = 32
CMP_STRIDE = 16
CMP_HIDDEN = 128
SEL_BLOCK = 64
SEL_TOP_N = 16
WINDOW = 512
FORCE_SCORE = 1e4
REL_BUCKETS = 32
REL_MAX_DIST = 128
D_FF = 4 * D_MODEL
EPS = 1e-6

LANES = 128
SUBLANES = 8
BF16_ROWS = 16
NEG = -1e30
N_CHUNK = SEQ // CMP_STRIDE
N_SEL = SEQ // SEL_BLOCK
NSA_Q = 128
NSA_K = 256
SELECT_Q = 512
N_BIAS_TILES = (WINDOW + NSA_K - NSA_Q) // NSA_Q + 1
SEL_FAR_TILE = (REL_MAX_DIST + NSA_K) // NSA_Q
WIN_ROWS = N_BIAS_TILES * NSA_K
AUG_BLOCK = NSA_HEAD_DIM
AUG_ONE = AUG_BLOCK + N_SEL
CMP_TAB_BASE = (SEQ // NSA_Q - 1) * (NSA_Q // CMP_STRIDE)
CMP_TAB_ROWS = 256
MLA_Q = 256
MLA_K = 512
V_TILE = 256
PROJ_ROWS = 256
PACK_ROWS = 256
TAIL_ROWS = 512
FF_CHUNK = 4096
KEY_CHUNK = 256
PAIR_UNROLL = 6
NSA_UNROLL = 12
V_ROWS = 80
LOG2E = math.log2(math.e)
VMEM_LIMIT = 56 * 1024 * 1024

assert PROJ_ROWS % V_TILE == 0 and MLA_K % V_TILE == 0 and NSA_K == V_TILE


def _dot(a, b):
    return jnp.dot(a, b, preferred_element_type=F32)


def _dot_nt(a, b):
    return lax.dot_general(a, b, (((1,), (1,)), ((), ())), preferred_element_type=F32)


def _split(a, terms):
    pieces = []
    rem = a
    for _ in range(terms):
        piece = rem.astype(BF16)
        pieces.append(piece)
        rem = rem - piece.astype(F32)
    return pieces


def _split_dot(a, b, terms=2):
    return sum(_dot(p, b) for p in _split(a, terms))


def _rms(x, width):
    return x * lax.rsqrt(jnp.sum(x * x, axis=-1, keepdims=True) * (1.0 / width) + EPS)


def _t5_bucket_np(dist):
    n = np.maximum(dist, 0)
    max_exact = REL_BUCKETS // 2
    large = max_exact + (np.log(np.maximum(n, 1).astype(np.float32) / max_exact)
                         / math.log(REL_MAX_DIST / max_exact)
                         * (REL_BUCKETS - max_exact)).astype(np.int32)
    large = np.minimum(large, REL_BUCKETS - 1)
    return np.where(n < max_exact, n, large).astype(np.int32)


def _bias_index_table():
    i = np.arange(NSA_Q)[None, :]
    parts = []
    j = np.arange(NSA_K)[:, None]
    for off in range(N_BIAS_TILES):
        d = off * NSA_Q + i - j
        parts.append(np.where((d >= 0) & (d < WINDOW), _t5_bucket_np(d), -1))
    r = np.arange(CMP_TAB_ROWS)[:, None]
    dist_c = (CMP_TAB_BASE - r) * CMP_STRIDE + i - (CMP_BLOCK - 1)
    parts.append(np.where(dist_c >= 0, _t5_bucket_np(dist_c), -1))
    return np.concatenate(parts, axis=0).astype(np.int32)


def _bias_table_kernel(rb_ref, idx_ref, out_ref):
    idx = idx_ref[...]
    bucket = jnp.maximum(idx, 0)
    is_window = pl.program_id(0) < WIN_ROWS // idx.shape[0]
    for h in range(NSA_HEADS):
        row = rb_ref[h:h + 1, :]
        far = jnp.where(is_window, row[:, REL_BUCKETS - 1:REL_BUCKETS], 0.0)
        values = jnp.broadcast_to((row - far) * LOG2E, idx.shape)
        picked = jnp.take_along_axis(values, bucket, axis=1, mode="promise_in_bounds")
        out_ref[h] = jnp.where(idx >= 0, picked, NEG)


def _bias_tables(rel_bias):
    idx = jnp.asarray(_bias_index_table())
    rows = idx.shape[0]
    blk = CMP_TAB_ROWS
    return pl.pallas_call(
        _bias_table_kernel,
        grid=(rows // blk,),
        in_specs=[pl.BlockSpec((NSA_HEADS, LANES), lambda r: (0, 0)),
                  pl.BlockSpec((blk, LANES), lambda r: (r, 0))],
        out_specs=pl.BlockSpec((NSA_HEADS, blk, LANES), lambda r: (0, r, 0)),
        out_shape=jax.ShapeDtypeStruct((NSA_HEADS, rows, LANES), F32),
        name="bias_tables",
    )(jnp.pad(rel_bias.T, ((0, 0), (0, LANES - REL_BUCKETS))), idx)


ROPE_HALF = MLA_ROPE // 2
MISC_KR = 64


def _rope_tables(scale):
    inv = 1.0 / (ROPE_THETA ** (np.arange(ROPE_HALF, dtype=np.float64) / ROPE_HALF))
    ang = np.arange(SEQ, dtype=np.float64)[:, None] * inv[None, :]
    cos, sin = np.cos(ang), np.sin(ang)
    ones = np.ones((SEQ, MLA_NOPE))
    z32 = np.zeros((SEQ, LANES - MLA_QK))
    z64 = np.zeros((SEQ, MLA_NOPE))
    c = np.concatenate([ones, cos, cos, z32], axis=1) * scale
    s = np.concatenate([z64, -sin, sin, z32], axis=1) * scale
    return jnp.asarray(c, F32), jnp.asarray(s, F32)


def _pair_norm(x, gain_ref):
    low = lax.broadcasted_iota(jnp.int32, (x.shape[0], LANES), 1) < NSA_HEAD_DIM
    outs = []
    for j in range(x.shape[1] // LANES):
        cols = slice(j * LANES, (j + 1) * LANES)
        sl = x[:, cols]
        sq = sl * sl
        tot = jnp.sum(sq, axis=-1, keepdims=True)
        lo = jnp.sum(jnp.where(low, sq, 0.0), axis=-1, keepdims=True)
        rs = jnp.where(low, lax.rsqrt(lo * (1.0 / NSA_HEAD_DIM) + EPS),
                       lax.rsqrt((tot - lo) * (1.0 / NSA_HEAD_DIM) + EPS))
        outs.append(sl * rs * gain_ref[:, cols])
    return jnp.concatenate(outs, axis=1)


def _mla_query_heads(x, gain_ref, c, s):
    in_head = lax.broadcasted_iota(jnp.int32, (x.shape[0], LANES), 1) < MLA_QK
    outs = []
    for h in range(MLA_HEADS):
        cols = slice(h * LANES, (h + 1) * LANES)
        sl = x[:, cols]
        ss = jnp.sum(jnp.where(in_head, sl * sl, 0.0), axis=-1, keepdims=True)
        xn = sl * lax.rsqrt(ss * (1.0 / MLA_QK) + EPS) * gain_ref[:, cols]
        outs.append(xn * c + pltpu.roll(xn, LANES - ROPE_HALF, 1) * s)
    return jnp.concatenate(outs, axis=1)


def _mla_key_heads(x, k_rope, k_swap, c, s):
    in_head = lax.broadcasted_iota(jnp.int32, (x.shape[0], LANES), 1) < MLA_QK
    turned = k_swap * s
    outs = []
    for h in range(MLA_HEADS):
        sl = x[:, h * LANES:(h + 1) * LANES] + k_rope
        ss = jnp.sum(jnp.where(in_head, sl * sl, 0.0), axis=-1, keepdims=True)
        outs.append(lax.rsqrt(ss * (1.0 / MLA_QK) + EPS) * (sl * c + turned))
    return jnp.concatenate(outs, axis=1)


COL_KV = MLA_Q_RANK
COL_MISC = COL_KV + MLA_KV_RANK
COL_SWAP = COL_MISC + LANES
COL_NQ = COL_SWAP + LANES
COL_NK = COL_NQ + NSA_HEADS * NSA_HEAD_DIM
COL_NV = COL_NK + 4 * NSA_HEAD_DIM
COL_CMP = COL_NV + 4 * NSA_HEAD_DIM
COL_END = COL_CMP + 4 * NSA_HEAD_DIM


def _proj_kernel(x_ref, ln1_ref, wp_ref, cqg_ref, ckvg_ref, wuq_ref, wukk_ref, wukvt_ref,
                 qg_ref, kown_ref, kpartner_ref, qc_ref, qs_ref, kc_ref, ks_ref,
                 nq_gain_ref, nk_gain_ref, kpad_ref, kaug_ref, vpick_ref, ones_mla_ref, ones_nsa_ref, eye_ref,
                 qmla_ref, kmla_ref, vmla_ref, qnsa_ref, kdup_ref, vnsa_ref, kck_ref, kcv_ref, gate_ref):
    for sub in range(PROJ_ROWS // V_TILE):
        rows = slice(sub * V_TILE, (sub + 1) * V_TILE)
        x = x_ref[rows, :]
        h = (_rms(x, D_MODEL) * ln1_ref[...]).astype(BF16)
        part = lambda a, b: _dot(h, wp_ref[:, a:b])

        cq = (_rms(part(0, COL_KV), MLA_Q_RANK) * cqg_ref[...]).astype(BF16)
        qmla_ref[rows, :] = _mla_query_heads(_dot(cq, wuq_ref[...]), qg_ref,
                                             qc_ref[rows, :], qs_ref[rows, :]).astype(BF16)

        kv_misc = part(COL_KV, COL_NQ)
        ckv = (_rms(kv_misc[:, :MLA_KV_RANK], MLA_KV_RANK) * ckvg_ref[...]).astype(BF16)
        misc = kv_misc[:, COL_MISC - COL_KV:COL_SWAP - COL_KV]
        lane = lax.broadcasted_iota(jnp.int32, misc.shape, 1)
        k_rope = jnp.where((lane >= MISC_KR) & (lane < MISC_KR + MLA_ROPE), misc, 0.0)
        kmla_ref[rows, :] = _mla_key_heads(_dot(ckv, wukk_ref[...]), k_rope, kv_misc[:, COL_SWAP - COL_KV:],
                                           kc_ref[rows, :] * kown_ref[...],
                                           ks_ref[rows, :] * kpartner_ref[...]).astype(BF16)
        vmla_ref[sub] = (_dot_nt(wukvt_ref[...], ckv) + ones_mla_ref[...]).astype(BF16)

        qnsa_ref[rows, :] = _pair_norm(part(COL_NQ, COL_NK), nq_gain_ref).astype(BF16)

        kn = _pair_norm(part(COL_NK, COL_NV), nk_gain_ref)
        kdup_ref[rows, :] = (_dot(kn.astype(BF16), kpad_ref[...]) + kaug_ref[rows, :]).astype(BF16)
        vnsa_ref[sub] = (_dot_nt(vpick_ref[...], part(COL_NV, COL_CMP).astype(BF16)) + ones_nsa_ref[...]).astype(BF16)
        cmp_kv = part(COL_CMP, COL_END)
        kck_ref[rows, :] = cmp_kv[:, :LANES]
        kcv_ref[rows, :] = cmp_kv[:, LANES:]
        gate = 1.0 / (1.0 + jnp.exp(-misc))
        gate_t = sum(_dot_nt(eye_ref[...], piece) for piece in _split(gate, 2))
        for u in range(V_TILE // NSA_Q):
            gate_ref[sub * (V_TILE // NSA_Q) + u] = gate_t[:, u * NSA_Q:(u + 1) * NSA_Q]


def _head_padding(heads):
    d = NSA_HEAD_DIM
    m = np.zeros((heads * d, heads * LANES), np.float32)
    for hh in range(heads):
        m[hh * d + np.arange(d), hh * LANES + np.arange(d)] = 1.0
    return m


def _packed_segments():
    kv0 = 1184
    seg = lambda a, b: (a, b - a)
    kvb = lambda br, kv, g: (kv0 + ((br * 2 + kv) * 2 + g) * 64, 64)
    zeros = lambda n: (None, n)
    kr = COL_MISC
    gate_cols = N_BRANCH * NSA_HEADS
    pieces = [
        seg(0, COL_MISC),
        seg(1952, 1952 + gate_cols), zeros(MISC_KR - gate_cols), seg(kr, kr + MLA_ROPE),
        zeros(LANES - MISC_KR - MLA_ROPE),
        zeros(MISC_KR), seg(kr + ROPE_HALF, kr + MLA_ROPE), seg(kr, kr + ROPE_HALF), zeros(LANES - MISC_KR - MLA_ROPE),
        seg(672, 1184),
        kvb(1, 0, 0), kvb(1, 0, 1), kvb(2, 0, 0), kvb(2, 0, 1),
        kvb(1, 1, 0), kvb(1, 1, 1), kvb(2, 1, 0), kvb(2, 1, 1),
        kvb(0, 0, 0), kvb(0, 0, 1), kvb(0, 1, 0), kvb(0, 1, 1),
    ]
    assert sum(n for _, n in pieces) == COL_END
    return pieces


def _pack_kernel(wt_ref, out_ref, packed_ref):
    dst = 0
    for src, n in _packed_segments():
        assert n % SUBLANES == 0 and dst % SUBLANES == 0 and (src is None or src % SUBLANES == 0)
        if src is None:
            packed_ref[dst:dst + n, :] = jnp.zeros((n, packed_ref.shape[1]), F32)
        else:
            packed_ref[dst:dst + n, :] = wt_ref[src:src + n, :]
        dst += n
    for g in range(COL_END // LANES):
        cols = slice(g * LANES, (g + 1) * LANES)
        out_ref[:, cols] = packed_ref[cols, :].T.astype(BF16)


def _pack_input_projection(w_in):
    rows = PACK_ROWS
    return pl.pallas_call(
        _pack_kernel,
        grid=(D_MODEL // rows,),
        in_specs=[pl.BlockSpec((w_in.shape[1], rows), lambda i: (0, i))],
        out_specs=pl.BlockSpec((rows, COL_END), lambda i: (i, 0)),
        out_shape=jax.ShapeDtypeStruct((D_MODEL, COL_END), BF16),
        scratch_shapes=[pltpu.VMEM((COL_END, rows), F32)],
        compiler_params=pltpu.CompilerParams(dimension_semantics=("parallel",),
                                             vmem_limit_bytes=VMEM_LIMIT),
        name="pack_input_projection",
    )(w_in.T)


def _projections(x2d, ln1_g, w_in, cq_g, ckv_g, w_uq, w_ukv, q_gain, k_gain, nq_gain, nk_gain):
    t = x2d.shape[0]
    wp = _pack_input_projection(w_in)
    wq3 = w_uq.reshape(MLA_Q_RANK, MLA_HEADS, MLA_QK).astype(BF16)
    wuq = jnp.concatenate([wq3, wq3[:, :, MLA_NOPE:MLA_NOPE + ROPE_HALF],
                           jnp.zeros((MLA_Q_RANK, MLA_HEADS, ROPE_HALF), BF16)], axis=2)
    wuq = wuq.reshape(MLA_Q_RANK, MLA_HEADS * LANES)
    wkv = w_ukv.reshape(MLA_KV_RANK, MLA_HEADS, MLA_NOPE + MLA_V)
    wukk = jnp.pad(wkv[:, :, :MLA_NOPE], ((0, 0), (0, 0), (0, LANES - MLA_NOPE)))
    wukk = wukk.reshape(MLA_KV_RANK, MLA_HEADS * LANES).astype(BF16)
    wukvt = jnp.pad(wkv[:, :, MLA_NOPE:], ((0, 0), (0, 0), (0, V_ROWS - MLA_V)))
    wukvt = wukvt.reshape(MLA_KV_RANK, MLA_HEADS * V_ROWS).T.astype(BF16)
    q_slab_gain = jnp.tile(jnp.concatenate(
        [q_gain, q_gain[MLA_NOPE:MLA_NOPE + ROPE_HALF], jnp.zeros((ROPE_HALF,), F32)]), MLA_HEADS)[None, :]
    qc, qs = _rope_tables(MLA_QK ** -0.5 * LOG2E)
    kc, ks = _rope_tables(1.0)
    k_own = jnp.pad(k_gain, (0, LANES - MLA_QK))[None, :]
    k_partner = jnp.concatenate([k_gain[:MLA_NOPE], k_gain[MLA_NOPE + ROPE_HALF:],
                                 k_gain[MLA_NOPE:MLA_NOPE + ROPE_HALF], jnp.zeros((LANES - MLA_QK,), F32)])[None, :]
    kpad = _head_padding(4)
    kaug = np.zeros((SEQ, 4 * LANES), np.float32)
    for s in range(4):
        kaug[:, s * LANES + AUG_ONE:s * LANES + AUG_ONE + 2] = 1.0
        if s < NSA_KV_HEADS:
            kaug[np.arange(SEQ), s * LANES + AUG_BLOCK + np.arange(SEQ) // SEL_BLOCK] = 1.0
    nqg = (jnp.tile(nq_gain, NSA_HEADS) * (NSA_HEAD_DIM ** -0.5 * LOG2E))[None, :]
    nkg = jnp.concatenate([nk_gain[1], nk_gain[1], nk_gain[2], nk_gain[2]])[None, :]
    vpick = np.zeros((4 * V_ROWS, 256), np.float32)
    for s in range(4):
        vpick[s * V_ROWS + np.arange(64), s * 64 + np.arange(64)] = 1.0

    def ones_col(heads):
        col = np.zeros((heads * V_ROWS, 1), np.float32)
        col[np.arange(heads) * V_ROWS + MLA_V] = 1.0
        return col

    rows = PROJ_ROWS
    n_pos = SEQ // rows
    full = lambda a: pl.BlockSpec(a.shape, lambda i: (0,) * a.ndim)
    tab = pl.BlockSpec((rows, LANES), lambda i: (i % n_pos, 0))
    consts = [ln1_g[None, :], wp, cq_g[None, :], ckv_g[None, :], wuq, wukk, wukvt, q_slab_gain, k_own, k_partner]
    tabs = [qc, qs, kc, ks]
    mats = [nqg, nkg, jnp.asarray(kpad, BF16),
            jnp.asarray(vpick, BF16), jnp.asarray(ones_col(MLA_HEADS)), jnp.asarray(ones_col(4)),
            jnp.eye(LANES, dtype=BF16)]
    row_spec = lambda w: pl.BlockSpec((rows, w), lambda i: (i, 0))
    vt_spec = lambda w, tile: pl.BlockSpec((rows // tile, w, tile), lambda i: (i, 0, 0))
    sds = jax.ShapeDtypeStruct
    return pl.pallas_call(
        _proj_kernel,
        grid=(t // rows,),
        in_specs=([row_spec(D_MODEL)] + [full(a) for a in consts] + [tab] * 4 + [full(a) for a in mats[:3]]
                  + [pl.BlockSpec((rows, 4 * LANES), lambda i: (i % n_pos, 0))] + [full(a) for a in mats[3:]]),
        out_specs=[row_spec(1024), row_spec(1024), vt_spec(MLA_HEADS * V_ROWS, V_TILE), row_spec(512),
                   row_spec(512), vt_spec(4 * V_ROWS, V_TILE), row_spec(128), row_spec(128),
                   vt_spec(LANES, NSA_Q)],
        out_shape=[sds((t, 1024), BF16), sds((t, 1024), BF16), sds((t // V_TILE, MLA_HEADS * V_ROWS, V_TILE), BF16),
                   sds((t, 512), BF16), sds((t, 512), BF16), sds((t // V_TILE, 4 * V_ROWS, V_TILE), BF16),
                   sds((t, 128), F32), sds((t, 128), F32), sds((t // NSA_Q, LANES, NSA_Q), F32)],
        compiler_params=pltpu.CompilerParams(dimension_semantics=("parallel",),
                                             vmem_limit_bytes=VMEM_LIMIT),
        name="projections",
    )(x2d, *consts, *tabs, *mats[:3], jnp.asarray(kaug, BF16), *mats[3:])


def _compress_kernel(xk_ref, xv_ref, pe_ref, w1_ref, w2_ref, kg_ref, dupk_ref, pickv_ref, kc_ref, vct_ref):
    def mlp(x_ref, which):
        top = bot = None
        for t in range(CMP_STRIDE):
            xt = x_ref[pl.ds(t, N_CHUNK, stride=CMP_STRIDE), :]
            a = _dot((xt + pe_ref[which, t]).astype(BF16), w1_ref[which, t])
            b = _dot((xt + pe_ref[which, CMP_STRIDE + t]).astype(BF16), w1_ref[which, CMP_STRIDE + t])
            top = a if top is None else top + a
            bot = b if bot is None else bot + b
        hid = top + pltpu.roll(bot, N_CHUNK - 1, 0)
        act = (hid * (1.0 / (1.0 + jnp.exp(-hid)))).astype(BF16)
        return _dot(act, w2_ref[which])

    k = _pair_norm(mlp(xk_ref, 0), kg_ref).astype(BF16)
    v = mlp(xv_ref, 1).astype(BF16)
    for g in range(NSA_KV_HEADS):
        kc_ref[0, g] = _dot(k, dupk_ref[g]).astype(BF16)
        vct_ref[0, g] = _dot_nt(pickv_ref[g], v).astype(BF16)


def _compress(kck, kcv, cmp_pe, cmp_w1, cmp_w2, k_gain0):
    b = kck.shape[0] // SEQ
    g_ = NSA_KV_HEADS
    d = NSA_HEAD_DIM
    pe = jnp.tile(cmp_pe, (1, 1, g_))[:, :, None, :]
    w1 = cmp_w1.reshape(2, CMP_BLOCK, d, CMP_HIDDEN)
    z1 = jnp.zeros_like(w1)
    w1 = jnp.concatenate([jnp.concatenate([w1, z1], axis=3), jnp.concatenate([z1, w1], axis=3)], axis=2)
    z2 = jnp.zeros_like(cmp_w2)
    w2 = jnp.concatenate([jnp.concatenate([cmp_w2, z2], axis=2), jnp.concatenate([z2, cmp_w2], axis=2)], axis=1)
    dupk = np.zeros((g_, LANES, LANES), np.float32)
    pickv = np.zeros((g_, d, LANES), np.float32)
    for g in range(g_):
        dupk[g, g * d + np.arange(d), np.arange(d)] = 1.0
        dupk[g, g * d + np.arange(d), d + np.arange(d)] = 1.0
        pickv[g, np.arange(d), g * d + np.arange(d)] = 1.0
    full = lambda a: pl.BlockSpec(a.shape, lambda i: (0,) * a.ndim)
    consts = [pe, w1.astype(BF16), w2.astype(BF16), jnp.tile(k_gain0, g_)[None, :],
              jnp.asarray(dupk, BF16), jnp.asarray(pickv, BF16)]
    x_spec = pl.BlockSpec((SEQ, LANES), lambda i: (i, 0))
    return pl.pallas_call(
        _compress_kernel,
        grid=(b,),
        in_specs=[x_spec, x_spec] + [full(a) for a in consts],
        out_specs=[pl.BlockSpec((1, g_, N_CHUNK, LANES), lambda i: (i, 0, 0, 0)),
                   pl.BlockSpec((1, g_, d, N_CHUNK), lambda i: (i, 0, 0, 0))],
        out_shape=[jax.ShapeDtypeStruct((b, g_, N_CHUNK, LANES), BF16),
                   jax.ShapeDtypeStruct((b, g_, d, N_CHUNK), BF16)],
        compiler_params=pltpu.CompilerParams(dimension_semantics=("parallel",)),
        name="compression",
    )(kck, kcv, *consts)


def _flash_init(m_ref, acc_ref):
    m_ref[...] = jnp.full(m_ref.shape, NEG, F32)
    acc_ref[...] = jnp.zeros(acc_ref.shape, F32)


def _flash_step(s, v, m_ref, acc_ref, chunk=KEY_CHUNK):
    for c in range(s.shape[0] // chunk):
        sc = s[c * chunk:(c + 1) * chunk]
        m_old = m_ref[c]
        m_new = jnp.maximum(m_old, jnp.max(sc, axis=0, keepdims=True))
        p = jnp.exp2(sc - m_new).astype(BF16)
        acc_ref[c] = jnp.exp2(m_old - m_new) * acc_ref[c] + _dot(v[:, c * chunk:(c + 1) * chunk], p)
        m_ref[c] = m_new


def _flash_softmax(s, m_ref, p_ref, alpha_ref, chunk=KEY_CHUNK):
    for c in range(s.shape[0] // chunk):
        sc = s[c * chunk:(c + 1) * chunk]
        m_old = m_ref[c]
        m_new = jnp.maximum(m_old, jnp.max(sc, axis=0, keepdims=True))
        alpha_ref[c] = jnp.exp2(m_old - m_new)
        p_ref[c * chunk:(c + 1) * chunk, :] = jnp.exp2(sc - m_new).astype(BF16)
        m_ref[c] = m_new


def _flash_accumulate(v, p_ref, alpha_ref, acc_ref, chunk=KEY_CHUNK):
    for c in range(p_ref.shape[0] // chunk):
        rows = slice(c * chunk, (c + 1) * chunk)
        acc_ref[c] = alpha_ref[c] * acc_ref[c] + _dot(v[:, rows], p_ref[rows, :])


def _flash_finish(m_ref, acc_ref):
    n = m_ref.shape[0]
    m = m_ref[0]
    for c in range(1, n):
        m = jnp.maximum(m, m_ref[c])
    acc = sum(jnp.exp2(m_ref[c] - m) * acc_ref[c] for c in range(n))
    return acc[:MLA_V] / acc[MLA_V:MLA_V + 1]


def _pipelined_pairs(sched_ref, row, n, scores, softmax, accumulate, unroll=PAIR_UNROLL):
    assert n >= 2 and unroll % 2 == 0

    def pair(i):
        i = jnp.minimum(i, n - 1)
        return sched_ref[row, i], sched_ref[row + 1, i]

    def steps(first, count):
        for u in range(count):
            i, slot = first + u, u % 2
            if accumulate is None:
                scores(*pair(i + 1), 1 - slot)
                softmax(*pair(i), slot)
            else:
                scores(*pair(i + 2), slot)
                softmax(*pair(i + 1), 1 - slot)
                accumulate(*pair(i), slot)

    scores(*pair(0), 0)
    if accumulate is not None:
        scores(*pair(1), 1)
        softmax(*pair(0), 0)
    loops = n // unroll
    lax.fori_loop(0, loops, lambda j, carry: (steps(j * unroll, unroll), carry)[1], 0)
    steps(loops * unroll, n - loops * unroll)


def _mla_pairs():
    nq, nk = SEQ // MLA_Q, SEQ // MLA_K
    full = [(qt, kt) for kt in range(nk) for qt in range(nq) if (kt + 1) * MLA_K <= qt * MLA_Q]
    diag = [(qt, (qt * MLA_Q) // MLA_K) for qt in range(nq)]
    return full, diag


def _mla_kernel(sched_ref, q_ref, k_ref, vt_ref, eye_ref, wo_ref, wup_ref, wdn_ref,
                o_ref, wo16_ref, wup16_ref, wdn16_ref, m_ref, acc_ref, s_ref):
    for w_ref, w16_ref in ((wo_ref, wo16_ref), (wup_ref, wup16_ref), (wdn_ref, wdn16_ref)):
        w16_ref[...] = w_ref[...].astype(BF16)
    sub = MLA_K // V_TILE
    full, diag = _mla_pairs()
    rel = (lax.broadcasted_iota(jnp.int32, (MLA_K, MLA_Q), 1)
           - lax.broadcasted_iota(jnp.int32, (MLA_K, MLA_Q), 0))
    _flash_init(m_ref, acc_ref)

    def scores(qt, kt, slot, masked):
        qrows = pl.ds(pl.multiple_of(qt * MLA_Q, MLA_Q), MLA_Q)
        krows = pl.ds(pl.multiple_of(kt * MLA_K, MLA_K), MLA_K)
        for hh in range(2):
            cols = slice(hh * LANES, (hh + 1) * LANES)
            s = _dot_nt(k_ref[krows, cols], q_ref[qrows, cols])
            if masked:
                s = jnp.where(rel >= kt * MLA_K - qt * MLA_Q, s, NEG)
            s_ref[slot, hh] = s

    def update(qt, kt, slot):
        for hh in range(2):
            v = jnp.concatenate([vt_ref[kt * sub + i, hh * V_ROWS:(hh + 1) * V_ROWS, :] for i in range(sub)],
                                axis=1)
            _flash_step(s_ref[slot, hh], v, m_ref.at[hh, qt], acc_ref.at[hh, qt])

    _pipelined_pairs(sched_ref, 0, len(full), lambda qt, kt, slot: scores(qt, kt, slot, False), update, None)
    _pipelined_pairs(sched_ref, 2, len(diag), lambda qt, kt, slot: scores(qt, kt, slot, True), update, None)

    for qt in range(SEQ // MLA_Q):
        o_t = jnp.concatenate([_flash_finish(m_ref.at[hh, qt], acc_ref.at[hh, qt])
                               for hh in range(2)], axis=0)
        o_ref[qt * MLA_Q:(qt + 1) * MLA_Q, :] = _dot_nt(eye_ref[...], o_t.astype(BF16)).astype(BF16)


def _schedule(*pair_lists):
    n = max(len(p) for p in pair_lists)
    out = np.zeros((2 * len(pair_lists), n), np.int32)
    for i, pairs in enumerate(pair_lists):
        out[2 * i, :len(pairs)] = [a for a, _ in pairs]
        out[2 * i + 1, :len(pairs)] = [b for _, b in pairs]
    return jnp.asarray(out)


def _mla_attention(q, k, vt, w_o, w_up, w_down):
    b = q.shape[0] // SEQ
    nq = SEQ // MLA_Q
    nv = SEQ // V_TILE
    chunks = MLA_K // KEY_CHUNK
    head_pairs = MLA_HEADS // 2
    steps = b * head_pairs
    weights = [w_o, w_up, w_down]
    assert all(w.shape[0] % (steps * BF16_ROWS) == 0 for w in weights)
    w_specs = [pl.BlockSpec((w.shape[0] // steps, w.shape[1]), lambda i, hp: (i * head_pairs + hp, 0))
               for w in weights]
    return pl.pallas_call(
        _mla_kernel,
        grid=(b, head_pairs),
        in_specs=[pl.BlockSpec(memory_space=pltpu.SMEM),
                  pl.BlockSpec((SEQ, 2 * LANES), lambda i, hp: (i, hp)),
                  pl.BlockSpec((SEQ, 2 * LANES), lambda i, hp: (i, hp)),
                  pl.BlockSpec((nv, 2 * V_ROWS, V_TILE), lambda i, hp: (i, hp, 0)),
                  pl.BlockSpec((MLA_Q, MLA_Q), lambda i, hp: (0, 0))] + w_specs,
        out_specs=[pl.BlockSpec((SEQ, LANES), lambda i, hp: (i, hp))] + w_specs,
        out_shape=[jax.ShapeDtypeStruct((q.shape[0], MLA_HEADS * MLA_V), BF16)]
                  + [jax.ShapeDtypeStruct(w.shape, BF16) for w in weights],
        scratch_shapes=[pltpu.VMEM((2, nq, chunks, 1, MLA_Q), F32),
                        pltpu.VMEM((2, nq, chunks, V_ROWS, MLA_Q), F32),
                        pltpu.VMEM((2, 2, MLA_K, MLA_Q), F32)],
        compiler_params=pltpu.CompilerParams(
            dimension_semantics=("parallel", "parallel"), vmem_limit_bytes=VMEM_LIMIT),
        name="mla_attention",
    )(_schedule(*_mla_pairs()), q, k, vt, jnp.eye(MLA_Q, dtype=BF16), *weights)


def _select_kernel(q_ref, kc_ref, vct_ref, tabc_ref, ovt_ref, eye_ref, qpad_ref, qaug_ref, ocmp_ref):
    step = pl.program_id(1)
    t = SELECT_Q
    heads = range(NSA_GROUP)
    tiles = range(t // NSA_Q)

    def group(g):
        q_pad = _dot(q_ref[:, g * 2 * LANES:(g + 1) * 2 * LANES], qpad_ref[...]).astype(BF16)
        low = lax.broadcasted_iota(jnp.int32, (t, LANES), 1) < NSA_HEAD_DIM
        zero = jnp.zeros((t, LANES), BF16)
        pairs = [q_ref[:, (2 * g + j) * LANES:(2 * g + j + 1) * LANES] for j in range(NSA_GROUP // 2)]
        q4 = jnp.concatenate([jnp.where(low == (r % 2 == 0), pairs[r // 2], zero) for r in heads], axis=0)

        per_tile = NSA_Q // CMP_STRIDE
        rows = [pl.ds(pl.multiple_of(CMP_TAB_BASE - (step * len(tiles) + u) * per_tile, per_tile), N_CHUNK)
                for u in tiles]
        bias_c = jnp.concatenate([tabc_ref[g * NSA_GROUP + r, rows[u], :] for r in heads for u in tiles], axis=1)
        valid = bias_c > 0.5 * NEG
        sc = _dot_nt(kc_ref[0, g], q4) + bias_c
        mx = jnp.max(sc, axis=0, keepdims=True)
        p = jnp.where(valid, jnp.exp2(sc - mx), 0.0)
        pc = p / jnp.maximum(jnp.sum(p, axis=0, keepdims=True), 1e-30)
        o_cmp = _dot(vct_ref[0, g], pc.astype(BF16)).astype(BF16)
        for u in tiles:
            ocmp_ref[0, g, u] = jnp.concatenate(
                [o_cmp[:, r * t + u * NSA_Q:r * t + (u + 1) * NSA_Q] for r in heads], axis=0)

        psum = sum(pc[:, r * t:(r + 1) * t] for r in heads)
        imp = sum(_dot(ovt_ref[...], piece) for piece in _split(psum, 3))
        jj = lax.broadcasted_iota(jnp.int32, (N_SEL, t), 0)
        blk = (step * t + lax.broadcasted_iota(jnp.int32, (N_SEL, t), 1)) // SEL_BLOCK
        ok = jj <= blk
        forced = ok & ((jj == 0) | (jj == blk) | (jj == blk - 1))
        score = jnp.where(forced, FORCE_SCORE, jnp.where(ok, imp, -jnp.inf))
        rank = jnp.zeros((N_SEL, t), jnp.int32)
        for i in range(N_SEL):
            ci = score[i:i + 1, :]
            beats = (ci > score) | ((ci == score) & (jj > i))
            rank = rank + beats.astype(jnp.int32)
        negsel = jnp.where(ok & (rank < SEL_TOP_N), 0.0, NEG).astype(BF16)
        spread = jnp.concatenate([jnp.zeros((AUG_BLOCK, t), BF16), negsel,
                                  jnp.zeros((LANES - AUG_ONE, t), BF16)], axis=0)
        mask_lanes = _dot_nt(eye_ref[...], spread).astype(BF16)
        lane = lax.broadcasted_iota(jnp.int32, (t, LANES), 1)
        for r in heads:
            far = tabc_ref[g * NSA_GROUP + r, 0:1, :]
            far_hi = far.astype(BF16)
            far_lo = (far - far_hi.astype(F32)).astype(BF16)
            aug = jnp.where(lane == AUG_ONE, far_hi, jnp.where(lane == AUG_ONE + 1, far_lo, mask_lanes))
            q_aug = jnp.where(lane < AUG_BLOCK, q_pad[:, r * LANES:(r + 1) * LANES], aug)
            for u in tiles:
                qaug_ref[0, g, u, r * NSA_Q:(r + 1) * NSA_Q, :] = q_aug[u * NSA_Q:(u + 1) * NSA_Q]

    for g in range(NSA_KV_HEADS):
        group(g)


def _nsa_select(q, kc, vct, tables):
    b = q.shape[0] // SEQ
    t = SELECT_Q
    steps = SEQ // t
    nq = SEQ // NSA_Q
    per = t // NSA_Q
    g_ = NSA_KV_HEADS
    c0 = np.arange(N_CHUNK)[None, :] * CMP_STRIDE
    j0 = np.arange(N_SEL)[:, None] * SEL_BLOCK
    ovt = np.clip(np.minimum(c0 + CMP_BLOCK, j0 + SEL_BLOCK) - np.maximum(c0, j0), 0, None) / CMP_BLOCK
    ovt[:, N_CHUNK - 1:] = 0.0
    return pl.pallas_call(
        _select_kernel,
        grid=(b, steps),
        in_specs=[pl.BlockSpec((t, g_ * 2 * LANES), lambda i, j: (i * steps + j, 0)),
                  pl.BlockSpec((1, g_, N_CHUNK, LANES), lambda i, j: (i, 0, 0, 0)),
                  pl.BlockSpec((1, g_, NSA_HEAD_DIM, N_CHUNK), lambda i, j: (i, 0, 0, 0)),
                  pl.BlockSpec((NSA_HEADS, CMP_TAB_ROWS, LANES), lambda i, j: (0, WIN_ROWS // CMP_TAB_ROWS, 0)),
                  pl.BlockSpec((N_SEL, N_CHUNK), lambda i, j: (0, 0)),
                  pl.BlockSpec((t, t), lambda i, j: (0, 0)),
                  pl.BlockSpec((2 * LANES, NSA_GROUP * LANES), lambda i, j: (0, 0))],
        out_specs=[pl.BlockSpec((1, g_, per, NSA_GROUP * NSA_Q, LANES), lambda i, j: (i, 0, j, 0, 0)),
                   pl.BlockSpec((1, g_, per, NSA_GROUP * NSA_HEAD_DIM, NSA_Q), lambda i, j: (i, 0, j, 0, 0))],
        out_shape=[jax.ShapeDtypeStruct((b, g_, nq, NSA_GROUP * NSA_Q, LANES), BF16),
                   jax.ShapeDtypeStruct((b, g_, nq, NSA_GROUP * NSA_HEAD_DIM, NSA_Q), BF16)],
        compiler_params=pltpu.CompilerParams(
            dimension_semantics=("parallel", "parallel"), vmem_limit_bytes=VMEM_LIMIT),
        name="nsa_select",
    )(q, kc, vct, tables, jnp.asarray(ovt, BF16), jnp.eye(t, dtype=BF16),
      jnp.asarray(_head_padding(NSA_GROUP), BF16))


def _nsa_pairs():
    nq, nk, per = SEQ // NSA_Q, SEQ // NSA_K, NSA_K // NSA_Q
    sel = [(qt, kt) for kt in range(nk) for qt in range(nq) if kt * NSA_K <= qt * NSA_Q]
    far = [(qt, kt) for qt, kt in sel if qt - kt * per >= SEL_FAR_TILE]
    near = [(qt, kt) for qt, kt in sel if qt - kt * per < SEL_FAR_TILE]
    win = [(qt, kt) for kt in range(nk) for qt in range(nq)
           if max(qt * NSA_Q - (WINDOW - 1), 0) // NSA_K <= kt and kt * NSA_K <= qt * NSA_Q]
    return far, near, win


def _attend_kernel(sched_ref, qaug_ref, ksel_ref, kwin_ref, vsel_ref, vwin_ref, ocmp_ref, gate_ref,
                   tabw_ref, eye_ref, o_ref, m_ref, acc_ref, s_ref, p_ref, alpha_ref):
    t = NSA_Q
    heads = range(NSA_GROUP)
    far, near, win = _nsa_pairs()
    _flash_init(m_ref, acc_ref)

    def scores(qt, kt, slot, k_ref, add_bias):
        s = _dot_nt(k_ref[pl.ds(pl.multiple_of(kt * NSA_K, NSA_K), NSA_K), :], qaug_ref[0, 0, qt])
        if add_bias:
            off = qt - kt * (NSA_K // NSA_Q)
            bias_rows = pl.ds(pl.multiple_of(off * NSA_K, NSA_K), NSA_K)
            s = s + jnp.concatenate([tabw_ref[r, bias_rows, :] for r in heads], axis=1)
        s_ref[slot] = s

    def softmax(qt, kt, slot, branch):
        _flash_softmax(s_ref[slot], m_ref.at[branch, qt], p_ref.at[slot], alpha_ref.at[slot], chunk=NSA_K)

    def accumulate(qt, kt, slot, v_ref, branch):
        _flash_accumulate(v_ref[kt], p_ref.at[slot], alpha_ref.at[slot], acc_ref.at[branch, qt], chunk=NSA_K)

    for row, pairs, k_ref, v_ref, branch, add_bias in ((0, far, ksel_ref, vsel_ref, 0, False),
                                                       (2, near, ksel_ref, vsel_ref, 0, True),
                                                       (4, win, kwin_ref, vwin_ref, 1, True)):
        _pipelined_pairs(sched_ref, row, len(pairs),
                         lambda qt, kt, slot, k_ref=k_ref, add_bias=add_bias: scores(qt, kt, slot, k_ref, add_bias),
                         lambda qt, kt, slot, branch=branch: softmax(qt, kt, slot, branch),
                         lambda qt, kt, slot, v_ref=v_ref, branch=branch: accumulate(qt, kt, slot, v_ref, branch),
                         unroll=NSA_UNROLL)

    d = NSA_HEAD_DIM
    group = pl.program_id(1)
    for qt in range(SEQ // t):
        o_cmp = ocmp_ref[0, 0, qt].astype(F32)
        o_sel, o_win = [_flash_finish(m_ref.at[br, qt], acc_ref.at[br, qt]) for br in range(2)]
        mixed = []
        for r in heads:
            gate = [gate_ref[qt, pl.ds(br * NSA_HEADS + group * NSA_GROUP + r, 1), :]
                    for br in range(N_BRANCH)]
            mixed.append(gate[0] * o_cmp[r * d:(r + 1) * d] + gate[1] * o_sel[:, r * t:(r + 1) * t]
                         + gate[2] * o_win[:, r * t:(r + 1) * t])
        o_ref[qt * t:(qt + 1) * t, :] = _dot_nt(eye_ref[...], jnp.concatenate(mixed, axis=0).astype(BF16)).astype(BF16)


def _nsa_attend(qaug, kaug, vt, ocmp, gates, tables):
    b = kaug.shape[0] // SEQ
    t = NSA_Q
    nq = SEQ // t
    nv = SEQ // V_TILE
    g_ = NSA_KV_HEADS
    k_spec = lambda off: pl.BlockSpec((SEQ, LANES), lambda i, g: (i, off + g))
    v_spec = lambda off: pl.BlockSpec((nv, V_ROWS, V_TILE), lambda i, g: (i, off + g, 0))
    return pl.pallas_call(
        _attend_kernel,
        grid=(b, g_),
        in_specs=[pl.BlockSpec(memory_space=pltpu.SMEM),
                  pl.BlockSpec((1, 1, nq, NSA_GROUP * t, LANES), lambda i, g: (i, g, 0, 0, 0)),
                  k_spec(0), k_spec(2), v_spec(0), v_spec(2),
                  pl.BlockSpec((1, 1, nq, NSA_GROUP * NSA_HEAD_DIM, t), lambda i, g: (i, g, 0, 0, 0)),
                  pl.BlockSpec((nq, LANES, t), lambda i, g: (i, 0, 0)),
                  pl.BlockSpec((NSA_GROUP, WIN_ROWS, LANES), lambda i, g: (g, 0, 0)),
                  pl.BlockSpec((t, t), lambda i, g: (0, 0))],
        out_specs=pl.BlockSpec((SEQ, 2 * LANES), lambda i, g: (i, g)),
        out_shape=jax.ShapeDtypeStruct((kaug.shape[0], NSA_HEADS * NSA_HEAD_DIM), BF16),
        scratch_shapes=[pltpu.VMEM((2, nq, 1, 1, NSA_GROUP * t), F32),
                        pltpu.VMEM((2, nq, 1, V_ROWS, NSA_GROUP * t), F32),
                        pltpu.VMEM((2, NSA_K, NSA_GROUP * t), F32),
                        pltpu.VMEM((2, NSA_K, NSA_GROUP * t), BF16),
                        pltpu.VMEM((2, 1, 1, NSA_GROUP * t), F32)],
        compiler_params=pltpu.CompilerParams(
            dimension_semantics=("parallel", "parallel"), vmem_limit_bytes=VMEM_LIMIT),
        name="nsa_attend",
    )(_schedule(*_nsa_pairs()), qaug, kaug, kaug, vt, vt, ocmp, gates, tables, jnp.eye(t, dtype=BF16))


def _tail_kernel(x_ref, oa_ref, ob_ref, ga_ref, gb_ref, wo_ref, ln2_ref, wup_ref, wdn_ref,
                 out_ref, h2_ref):
    width = MLA_HEADS * MLA_V
    a = (_rms(oa_ref[...].astype(F32), width) * ga_ref[...]).astype(BF16)
    b = (_rms(ob_ref[...].astype(F32), width) * gb_ref[...]).astype(BF16)
    x1 = x_ref[...] + _dot(a, wo_ref[:width]) + _dot(b, wo_ref[width:])
    h2_ref[...] = (_rms(x1, D_MODEL) * ln2_ref[...]).astype(BF16)
    out_ref[...] = x1

    def body(f, carry):
        cols = pl.ds(pl.multiple_of(f * FF_CHUNK, FF_CHUNK), FF_CHUNK)
        u = jnp.maximum(_dot(h2_ref[...], wup_ref[:, cols]), 0.0)
        out_ref[...] += _dot((u * u).astype(BF16), wdn_ref[f])
        return carry

    lax.fori_loop(0, D_FF // FF_CHUNK, body, 0)


def _tail(x2d, o_a, o_b, gn_a, gn_b, w_o, ln2_g, w_up, w_down):
    t = x2d.shape[0]
    rows = TAIL_ROWS
    width = MLA_HEADS * MLA_V
    once = lambda a: pl.BlockSpec(a.shape, lambda i: (0,) * a.ndim)
    assert w_o.dtype == w_up.dtype == w_down.dtype == BF16
    consts = [gn_a[None, :], gn_b[None, :], w_o, ln2_g[None, :], w_up,
              w_down.reshape(D_FF // FF_CHUNK, FF_CHUNK, D_MODEL)]
    row_spec = lambda w: pl.BlockSpec((rows, w), lambda i: (i, 0))
    return pl.pallas_call(
        _tail_kernel,
        grid=(t // rows,),
        in_specs=[row_spec(D_MODEL), row_spec(width), row_spec(width)] + [once(a) for a in consts],
        out_specs=row_spec(D_MODEL),
        out_shape=jax.ShapeDtypeStruct((t, D_MODEL), F32),
        scratch_shapes=[pltpu.VMEM((rows, D_MODEL), BF16)],
        compiler_params=pltpu.CompilerParams(dimension_semantics=("parallel",),
                                             vmem_limit_bytes=VMEM_LIMIT),
        name="tail",
    )(x2d, o_a, o_b, *consts)


def kernel(x, ln1_g, w_in, mla_cq_norm_g, mla_ckv_norm_g, mla_w_uq, mla_w_ukv, mla_q_gain, mla_k_gain,
           nsa_q_gain, nsa_k_gain, nsa_cmp_pe, nsa_cmp_w1, nsa_cmp_w2, rel_bias, grp_norm_mla,
           grp_norm_nsa, w_o, ln2_g, w_up, w_down):
    b, s, d = x.shape
    assert (s, d) == (SEQ, D_MODEL) and ln1_g.shape[0] == 1
    x2d = x.reshape(b * s, d)
    tables = _bias_tables(rel_bias)
    q_mla, k_mla, vt_mla, q_nsa, kdup, vt_nsa, kck, kcv, gates = _projections(
        x2d, ln1_g[0], w_in[0], mla_cq_norm_g[0], mla_ckv_norm_g[0], mla_w_uq[0], mla_w_ukv[0],
        mla_q_gain[0], mla_k_gain[0], nsa_q_gain[0], nsa_k_gain[0])
    kc, vct = _compress(kck, kcv, nsa_cmp_pe[0], nsa_cmp_w1[0], nsa_cmp_w2[0], nsa_k_gain[0, 0])
    o_a, w_o16, w_up16, w_down16 = _mla_attention(q_mla, k_mla, vt_mla, w_o[0], w_up[0], w_down[0])
    q_aug, ocmp = _nsa_select(q_nsa, kc, vct, tables)
    o_b = _nsa_attend(q_aug, kdup, vt_nsa, ocmp, gates, tables)
    out = _tail(x2d, o_a, o_b, grp_norm_mla[0], grp_norm_nsa[0], w_o16, ln2_g[0], w_up16, w_down16)
    return out.reshape(b, s, d)
```

```python
import math

import numpy as np
import jax
import jax.numpy as jnp
from jax import lax
from jax.experimental import pallas as pl
from jax.experimental.pallas import tpu as pltpu

F32 = jnp.float32
BF16 = jnp.bfloat16

D_MODEL = 1024
SEQ = 2048
MLA_HEADS = 8
MLA_NOPE = 64
MLA_ROPE = 32
MLA_V = 64
MLA_QK = MLA_NOPE + MLA_ROPE
MLA_Q_RANK = 384
MLA_KV_RANK = 256
ROPE_THETA = 10000.0
NSA_HEADS = 8
NSA_KV_HEADS = 2
NSA_GROUP = NSA_HEADS // NSA_KV_HEADS
NSA_HEAD_DIM = 64
N_BRANCH = 3
CMP_BLOCK = 32
CMP_STRIDE = 16
CMP_HIDDEN = 128
SEL_BLOCK = 64
SEL_TOP_N = 16
WINDOW = 512
FORCE_SCORE = 1e4
REL_BUCKETS = 32
REL_MAX_DIST = 128
D_FF = 4 * D_MODEL
EPS = 1e-6

LANES = 128
SUBLANES = 8
BF16_ROWS = 16
NEG = -1e30
N_CHUNK = SEQ // CMP_STRIDE
N_SEL = SEQ // SEL_BLOCK
NSA_Q = 128
NSA_K = 256
SELECT_Q = 512
N_BIAS_TILES = (WINDOW + NSA_K - NSA_Q) // NSA_Q + 1
SEL_FAR_TILE = (REL_MAX_DIST + NSA_K) // NSA_Q
WIN_ROWS = N_BIAS_TILES * NSA_K
AUG_BLOCK = NSA_HEAD_DIM
AUG_ONE = AUG_BLOCK + N_SEL
CMP_TAB_BASE = (SEQ // NSA_Q - 1) * (NSA_Q // CMP_STRIDE)
CMP_TAB_ROWS = 256
MLA_Q = 256
MLA_K = 512
V_TILE = 256
PROJ_ROWS = 256
PACK_ROWS = 256
TAIL_ROWS = 512
KEY_CHUNK = 256
PAIR_UNROLL = 6
NSA_UNROLL = 12
V_ROWS = 80
LOG2E = math.log2(math.e)
VMEM_LIMIT = 56 * 1024 * 1024

assert PROJ_ROWS % V_TILE == 0 and MLA_K % V_TILE == 0 and NSA_K == V_TILE


def _dot(a, b):
    return jnp.dot(a, b, preferred_element_type=F32)


def _dot_nt(a, b):
    return lax.dot_general(a, b, (((1,), (1,)), ((), ())), preferred_element_type=F32)


def _split(a, terms):
    pieces = []
    rem = a
    for _ in range(terms):
        piece = rem.astype(BF16)
        pieces.append(piece)
        rem = rem - piece.astype(F32)
    return pieces


def _split_dot(a, b, terms=2):
    return sum(_dot(p, b) for p in _split(a, terms))


def _rms(x, width):
    return x * lax.rsqrt(jnp.sum(x * x, axis=-1, keepdims=True) * (1.0 / width) + EPS)


def _t5_bucket_np(dist):
    n = np.maximum(dist, 0)
    max_exact = REL_BUCKETS // 2
    large = max_exact + (np.log(np.maximum(n, 1).astype(np.float32) / max_exact)
                         / math.log(REL_MAX_DIST / max_exact)
                         * (REL_BUCKETS - max_exact)).astype(np.int32)
    large = np.minimum(large, REL_BUCKETS - 1)
    return np.where(n < max_exact, n, large).astype(np.int32)


def _bias_index_table():
    i = np.arange(NSA_Q)[None, :]
    parts = []
    j = np.arange(NSA_K)[:, None]
    for off in range(N_BIAS_TILES):
        d = off * NSA_Q + i - j
        parts.append(np.where((d >= 0) & (d < WINDOW), _t5_bucket_np(d), -1))
    r = np.arange(CMP_TAB_ROWS)[:, None]
    dist_c = (CMP_TAB_BASE - r) * CMP_STRIDE + i - (CMP_BLOCK - 1)
    parts.append(np.where(dist_c >= 0, _t5_bucket_np(dist_c), -1))
    return np.concatenate(parts, axis=0).astype(np.int32)


def _bias_table_kernel(rb_ref, idx_ref, out_ref):
    idx = idx_ref[...]
    bucket = jnp.maximum(idx, 0)
    is_window = pl.program_id(0) < WIN_ROWS // idx.shape[0]
    for h in range(NSA_HEADS):
        row = rb_ref[h:h + 1, :]
        far = jnp.where(is_window, row[:, REL_BUCKETS - 1:REL_BUCKETS], 0.0)
        values = jnp.broadcast_to((row - far) * LOG2E, idx.shape)
        picked = jnp.take_along_axis(values, bucket, axis=1, mode="promise_in_bounds")
        out_ref[h] = jnp.where(idx >= 0, picked, NEG)


def _bias_tables(rel_bias):
    idx = jnp.asarray(_bias_index_table())
    rows = idx.shape[0]
    blk = CMP_TAB_ROWS
    return pl.pallas_call(
        _bias_table_kernel,
        grid=(rows // blk,),
        in_specs=[pl.BlockSpec((NSA_HEADS, LANES), lambda r: (0, 0)),
                  pl.BlockSpec((blk, LANES), lambda r: (r, 0))],
        out_specs=pl.BlockSpec((NSA_HEADS, blk, LANES), lambda r: (0, r, 0)),
        out_shape=jax.ShapeDtypeStruct((NSA_HEADS, rows, LANES), F32),
        name="bias_tables",
    )(jnp.pad(rel_bias.T, ((0, 0), (0, LANES - REL_BUCKETS))), idx)


ROPE_HALF = MLA_ROPE // 2
MISC_KR = 64


def _rope_tables(scale):
    inv = 1.0 / (ROPE_THETA ** (np.arange(ROPE_HALF, dtype=np.float64) / ROPE_HALF))
    ang = np.arange(SEQ, dtype=np.float64)[:, None] * inv[None, :]
    cos, sin = np.cos(ang), np.sin(ang)
    ones = np.ones((SEQ, MLA_NOPE))
    z32 = np.zeros((SEQ, LANES - MLA_QK))
    z64 = np.zeros((SEQ, MLA_NOPE))
    c = np.concatenate([ones, cos, cos, z32], axis=1) * scale
    s = np.concatenate([z64, -sin, sin, z32], axis=1) * scale
    return jnp.asarray(c, F32), jnp.asarray(s, F32)


def _pair_norm(x, gain_ref):
    low = lax.broadcasted_iota(jnp.int32, (x.shape[0], LANES), 1) < NSA_HEAD_DIM
    outs = []
    for j in range(x.shape[1] // LANES):
        cols = slice(j * LANES, (j + 1) * LANES)
        sl = x[:, cols]
        sq = sl * sl
        tot = jnp.sum(sq, axis=-1, keepdims=True)
        lo = jnp.sum(jnp.where(low, sq, 0.0), axis=-1, keepdims=True)
        rs = jnp.where(low, lax.rsqrt(lo * (1.0 / NSA_HEAD_DIM) + EPS),
                       lax.rsqrt((tot - lo) * (1.0 / NSA_HEAD_DIM) + EPS))
        outs.append(sl * rs * gain_ref[:, cols])
    return jnp.concatenate(outs, axis=1)


def _mla_query_heads(x, gain_ref, c, s):
    in_head = lax.broadcasted_iota(jnp.int32, (x.shape[0], LANES), 1) < MLA_QK
    outs = []
    for h in range(MLA_HEADS):
        cols = slice(h * LANES, (h + 1) * LANES)
        sl = x[:, cols]
        ss = jnp.sum(jnp.where(in_head, sl * sl, 0.0), axis=-1, keepdims=True)
        xn = sl * lax.rsqrt(ss * (1.0 / MLA_QK) + EPS) * gain_ref[:, cols]
        outs.append(xn * c + pltpu.roll(xn, LANES - ROPE_HALF, 1) * s)
    return jnp.concatenate(outs, axis=1)


def _mla_key_heads(x, k_rope, k_swap, c, s):
    in_head = lax.broadcasted_iota(jnp.int32, (x.shape[0], LANES), 1) < MLA_QK
    turned = k_swap * s
    outs = []
    for h in range(MLA_HEADS):
        sl = x[:, h * LANES:(h + 1) * LANES] + k_rope
        ss = jnp.sum(jnp.where(in_head, sl * sl, 0.0), axis=-1, keepdims=True)
        outs.append(lax.rsqrt(ss * (1.0 / MLA_QK) + EPS) * (sl * c + turned))
    return jnp.concatenate(outs, axis=1)


COL_KV = MLA_Q_RANK
COL_MISC = COL_KV + MLA_KV_RANK
COL_SWAP = COL_MISC + LANES
COL_NQ = COL_SWAP + LANES
COL_NK = COL_NQ + NSA_HEADS * NSA_HEAD_DIM
COL_NV = COL_NK + 4 * NSA_HEAD_DIM
COL_CMP = COL_NV + 4 * NSA_HEAD_DIM
COL_END = COL_CMP + 4 * NSA_HEAD_DIM


def _proj_kernel(x_ref, ln1_ref, wp_ref, cqg_ref, ckvg_ref, wuq_ref, wukk_ref, wukvt_ref,
                 qg_ref, kown_ref, kpartner_ref, qc_ref, qs_ref, kc_ref, ks_ref,
                 nq_gain_ref, nk_gain_ref, kpad_ref, kaug_ref, vpick_ref, ones_mla_ref, ones_nsa_ref, eye_ref,
                 qmla_ref, kmla_ref, vmla_ref, qnsa_ref, kdup_ref, vnsa_ref, kck_ref, kcv_ref, gate_ref):
    for sub in range(PROJ_ROWS // V_TILE):
        rows = slice(sub * V_TILE, (sub + 1) * V_TILE)
        x = x_ref[rows, :]
        h = (_rms(x, D_MODEL) * ln1_ref[...]).astype(BF16)
        part = lambda a, b: _dot(h, wp_ref[:, a:b])

        cq = (_rms(part(0, COL_KV), MLA_Q_RANK) * cqg_ref[...]).astype(BF16)
        qmla_ref[rows, :] = _mla_query_heads(_dot(cq, wuq_ref[...]), qg_ref,
                                             qc_ref[rows, :], qs_ref[rows, :]).astype(BF16)

        kv_misc = part(COL_KV, COL_NQ)
        ckv = (_rms(kv_misc[:, :MLA_KV_RANK], MLA_KV_RANK) * ckvg_ref[...]).astype(BF16)
        misc = kv_misc[:, COL_MISC - COL_KV:COL_SWAP - COL_KV]
        lane = lax.broadcasted_iota(jnp.int32, misc.shape, 1)
        k_rope = jnp.where((lane >= MISC_KR) & (lane < MISC_KR + MLA_ROPE), misc, 0.0)
        kmla_ref[rows, :] = _mla_key_heads(_dot(ckv, wukk_ref[...]), k_rope, kv_misc[:, COL_SWAP - COL_KV:],
                                           kc_ref[rows, :] * kown_ref[...],
                                           ks_ref[rows, :] * kpartner_ref[...]).astype(BF16)
        vmla_ref[sub] = (_dot_nt(wukvt_ref[...], ckv) + ones_mla_ref[...]).astype(BF16)

        qnsa_ref[rows, :] = _pair_norm(part(COL_NQ, COL_NK), nq_gain_ref).astype(BF16)

        kn = _pair_norm(part(COL_NK, COL_NV), nk_gain_ref)
        kdup_ref[rows, :] = (_dot(kn.astype(BF16), kpad_ref[...]) + kaug_ref[rows, :]).astype(BF16)
        vnsa_ref[sub] = (_dot_nt(vpick_ref[...], part(COL_NV, COL_CMP).astype(BF16)) + ones_nsa_ref[...]).astype(BF16)
        cmp_kv = part(COL_CMP, COL_END)
        kck_ref[rows, :] = cmp_kv[:, :LANES]
        kcv_ref[rows, :] = cmp_kv[:, LANES:]
        gate = 1.0 / (1.0 + jnp.exp(-misc))
        gate_t = sum(_dot_nt(eye_ref[...], piece) for piece in _split(gate, 2))
        for u in range(V_TILE // NSA_Q):
            gate_ref[sub * (V_TILE // NSA_Q) + u] = gate_t[:, u * NSA_Q:(u + 1) * NSA_Q]


def _head_padding(heads):
    d = NSA_HEAD_DIM
    m = np.zeros((heads * d, heads * LANES), np.float32)
    for hh in range(heads):
        m[hh * d + np.arange(d), hh * LANES + np.arange(d)] = 1.0
    return m


def _packed_segments():
    kv0 = 1184
    seg = lambda a, b: (a, b - a)
    kvb = lambda br, kv, g: (kv0 + ((br * 2 + kv) * 2 + g) * 64, 64)
    zeros = lambda n: (None, n)
    kr = COL_MISC
    gate_cols = N_BRANCH * NSA_HEADS
    pieces = [
        seg(0, COL_MISC),
        seg(1952, 1952 + gate_cols), zeros(MISC_KR - gate_cols), seg(kr, kr + MLA_ROPE),
        zeros(LANES - MISC_KR - MLA_ROPE),
        zeros(MISC_KR), seg(kr + ROPE_HALF, kr + MLA_ROPE), seg(kr, kr + ROPE_HALF), zeros(LANES - MISC_KR - MLA_ROPE),
        seg(672, 1184),
        kvb(1, 0, 0), kvb(1, 0, 1), kvb(2, 0, 0), kvb(2, 0, 1),
        kvb(1, 1, 0), kvb(1, 1, 1), kvb(2, 1, 0), kvb(2, 1, 1),
        kvb(0, 0, 0), kvb(0, 0, 1), kvb(0, 1, 0), kvb(0, 1, 1),
    ]
    assert sum(n for _, n in pieces) == COL_END
    return pieces


def _pack_kernel(wt_ref, out_ref, packed_ref):
    dst = 0
    for src, n in _packed_segments():
        assert n % SUBLANES == 0 and dst % SUBLANES == 0 and (src is None or src % SUBLANES == 0)
        if src is None:
            packed_ref[dst:dst + n, :] = jnp.zeros((n, packed_ref.shape[1]), F32)
        else:
            packed_ref[dst:dst + n, :] = wt_ref[src:src + n, :]
        dst += n
    for g in range(COL_END // LANES):
        cols = slice(g * LANES, (g + 1) * LANES)
        out_ref[:, cols] = packed_ref[cols, :].T.astype(BF16)


def _pack_input_projection(w_in):
    rows = PACK_ROWS
    return pl.pallas_call(
        _pack_kernel,
        grid=(D_MODEL // rows,),
        in_specs=[pl.BlockSpec((w_in.shape[1], rows), lambda i: (0, i))],
        out_specs=pl.BlockSpec((rows, COL_END), lambda i: (i, 0)),
        out_shape=jax.ShapeDtypeStruct((D_MODEL, COL_END), BF16),
        scratch_shapes=[pltpu.VMEM((COL_END, rows), F32)],
        compiler_params=pltpu.CompilerParams(dimension_semantics=("parallel",),
                                             vmem_limit_bytes=VMEM_LIMIT),
        name="pack_input_projection",
    )(w_in.T)


def _projections(x2d, ln1_g, w_in, cq_g, ckv_g, w_uq, w_ukv, q_gain, k_gain, nq_gain, nk_gain):
    t = x2d.shape[0]
    wp = _pack_input_projection(w_in)
    wq3 = w_uq.reshape(MLA_Q_RANK, MLA_HEADS, MLA_QK).astype(BF16)
    wuq = jnp.concatenate([wq3, wq3[:, :, MLA_NOPE:MLA_NOPE + ROPE_HALF],
                           jnp.zeros((MLA_Q_RANK, MLA_HEADS, ROPE_HALF), BF16)], axis=2)
    wuq = wuq.reshape(MLA_Q_RANK, MLA_HEADS * LANES)
    wkv = w_ukv.reshape(MLA_KV_RANK, MLA_HEADS, MLA_NOPE + MLA_V)
    wukk = jnp.pad(wkv[:, :, :MLA_NOPE], ((0, 0), (0, 0), (0, LANES - MLA_NOPE)))
    wukk = wukk.reshape(MLA_KV_RANK, MLA_HEADS * LANES).astype(BF16)
    wukvt = jnp.pad(wkv[:, :, MLA_NOPE:], ((0, 0), (0, 0), (0, V_ROWS - MLA_V)))
    wukvt = wukvt.reshape(MLA_KV_RANK, MLA_HEADS * V_ROWS).T.astype(BF16)
    q_slab_gain = jnp.tile(jnp.concatenate(
        [q_gain, q_gain[MLA_NOPE:MLA_NOPE + ROPE_HALF], jnp.zeros((ROPE_HALF,), F32)]), MLA_HEADS)[None, :]
    qc, qs = _rope_tables(MLA_QK ** -0.5 * LOG2E)
    kc, ks = _rope_tables(1.0)
    k_own = jnp.pad(k_gain, (0, LANES - MLA_QK))[None, :]
    k_partner = jnp.concatenate([k_gain[:MLA_NOPE], k_gain[MLA_NOPE + ROPE_HALF:],
                                 k_gain[MLA_NOPE:MLA_NOPE + ROPE_HALF], jnp.zeros((LANES - MLA_QK,), F32)])[None, :]
    kpad = _head_padding(4)
    kaug = np.zeros((SEQ, 4 * LANES), np.float32)
    for s in range(4):
        kaug[:, s * LANES + AUG_ONE:s * LANES + AUG_ONE + 2] = 1.0
        if s < NSA_KV_HEADS:
            kaug[np.arange(SEQ), s * LANES + AUG_BLOCK + np.arange(SEQ) // SEL_BLOCK] = 1.0
    nqg = (jnp.tile(nq_gain, NSA_HEADS) * (NSA_HEAD_DIM ** -0.5 * LOG2E))[None, :]
    nkg = jnp.concatenate([nk_gain[1], nk_gain[1], nk_gain[2], nk_gain[2]])[None, :]
    vpick = np.zeros((4 * V_ROWS, 256), np.float32)
    for s in range(4):
        vpick[s * V_ROWS + np.arange(64), s * 64 + np.arange(64)] = 1.0

    def ones_col(heads):
        col = np.zeros((heads * V_ROWS, 1), np.float32)
        col[np.arange(heads) * V_ROWS + MLA_V] = 1.0
        return col

    rows = PROJ_ROWS
    n_pos = SEQ // rows
    full = lambda a: pl.BlockSpec(a.shape, lambda i: (0,) * a.ndim)
    tab = pl.BlockSpec((rows, LANES), lambda i: (i % n_pos, 0))
    consts = [ln1_g[None, :], wp, cq_g[None, :], ckv_g[None, :], wuq, wukk, wukvt, q_slab_gain, k_own, k_partner]
    tabs = [qc, qs, kc, ks]
    mats = [nqg, nkg, jnp.asarray(kpad, BF16),
            jnp.asarray(vpick, BF16), jnp.asarray(ones_col(MLA_HEADS)), jnp.asarray(ones_col(4)),
            jnp.eye(LANES, dtype=BF16)]
    row_spec = lambda w: pl.BlockSpec((rows, w), lambda i: (i, 0))
    vt_spec = lambda w, tile: pl.BlockSpec((rows // tile, w, tile), lambda i: (i, 0, 0))
    sds = jax.ShapeDtypeStruct
    return pl.pallas_call(
        _proj_kernel,
        grid=(t // rows,),
        in_specs=([row_spec(D_MODEL)] + [full(a) for a in consts] + [tab] * 4 + [full(a) for a in mats[:3]]
                  + [pl.BlockSpec((rows, 4 * LANES), lambda i: (i % n_pos, 0))] + [full(a) for a in mats[3:]]),
        out_specs=[row_spec(1024), row_spec(1024), vt_spec(MLA_HEADS * V_ROWS, V_TILE), row_spec(512),
                   row_spec(512), vt_spec(4 * V_ROWS, V_TILE), row_spec(128), row_spec(128),
                   vt_spec(LANES, NSA_Q)],
        out_shape=[sds((t, 1024), BF16), sds((t, 1024), BF16), sds((t // V_TILE, MLA_HEADS * V_ROWS, V_TILE), BF16),
                   sds((t, 512), BF16), sds((t, 512), BF16), sds((t // V_TILE, 4 * V_ROWS, V_TILE), BF16),
                   sds((t, 128), F32), sds((t, 128), F32), sds((t // NSA_Q, LANES, NSA_Q), F32)],
        compiler_params=pltpu.CompilerParams(dimension_semantics=("parallel",),
                                             vmem_limit_bytes=VMEM_LIMIT),
        name="projections",
    )(x2d, *consts, *tabs, *mats[:3], jnp.asarray(kaug, BF16), *mats[3:])


def _compress_kernel(xk_ref, xv_ref, pe_ref, w1_ref, w2_ref, kg_ref, dupk_ref, pickv_ref, kc_ref, vct_ref):
    def mlp(x_ref, which):
        top = bot = None
        for t in range(CMP_STRIDE):
            xt = x_ref[pl.ds(t, N_CHUNK, stride=CMP_STRIDE), :]
            a = _dot((xt + pe_ref[which, t]).astype(BF16), w1_ref[which, t])
            b = _dot((xt + pe_ref[which, CMP_STRIDE + t]).astype(BF16), w1_ref[which, CMP_STRIDE + t])
            top = a if top is None else top + a
            bot = b if bot is None else bot + b
        hid = top + pltpu.roll(bot, N_CHUNK - 1, 0)
        act = (hid * (1.0 / (1.0 + jnp.exp(-hid)))).astype(BF16)
        return _dot(act, w2_ref[which])

    k = _pair_norm(mlp(xk_ref, 0), kg_ref).astype(BF16)
    v = mlp(xv_ref, 1).astype(BF16)
    for g in range(NSA_KV_HEADS):
        kc_ref[0, g] = _dot(k, dupk_ref[g]).astype(BF16)
        vct_ref[0, g] = _dot_nt(pickv_ref[g], v).astype(BF16)


def _compress(kck, kcv, cmp_pe, cmp_w1, cmp_w2, k_gain0):
    b = kck.shape[0] // SEQ
    g_ = NSA_KV_HEADS
    d = NSA_HEAD_DIM
    pe = jnp.tile(cmp_pe, (1, 1, g_))[:, :, None, :]
    w1 = cmp_w1.reshape(2, CMP_BLOCK, d, CMP_HIDDEN)
    z1 = jnp.zeros_like(w1)
    w1 = jnp.concatenate([jnp.concatenate([w1, z1], axis=3), jnp.concatenate([z1, w1], axis=3)], axis=2)
    z2 = jnp.zeros_like(cmp_w2)
    w2 = jnp.concatenate([jnp.concatenate([cmp_w2, z2], axis=2), jnp.concatenate([z2, cmp_w2], axis=2)], axis=1)
    dupk = np.zeros((g_, LANES, LANES), np.float32)
    pickv = np.zeros((g_, d, LANES), np.float32)
    for g in range(g_):
        dupk[g, g * d + np.arange(d), np.arange(d)] = 1.0
        dupk[g, g * d + np.arange(d), d + np.arange(d)] = 1.0
        pickv[g, np.arange(d), g * d + np.arange(d)] = 1.0
    full = lambda a: pl.BlockSpec(a.shape, lambda i: (0,) * a.ndim)
    consts = [pe, w1.astype(BF16), w2.astype(BF16), jnp.tile(k_gain0, g_)[None, :],
              jnp.asarray(dupk, BF16), jnp.asarray(pickv, BF16)]
    x_spec = pl.BlockSpec((SEQ, LANES), lambda i: (i, 0))
    return pl.pallas_call(
        _compress_kernel,
        grid=(b,),
        in_specs=[x_spec, x_spec] + [full(a) for a in consts],
        out_specs=[pl.BlockSpec((1, g_, N_CHUNK, LANES), lambda i: (i, 0, 0, 0)),
                   pl.BlockSpec((1, g_, d, N_CHUNK), lambda i: (i, 0, 0, 0))],
        out_shape=[jax.ShapeDtypeStruct((b, g_, N_CHUNK, LANES), BF16),
                   jax.ShapeDtypeStruct((b, g_, d, N_CHUNK), BF16)],
        compiler_params=pltpu.CompilerParams(dimension_semantics=("parallel",)),
        name="compression",
    )(kck, kcv, *consts)


def _flash_init(m_ref, acc_ref):
    m_ref[...] = jnp.full(m_ref.shape, NEG, F32)
    acc_ref[...] = jnp.zeros(acc_ref.shape, F32)


def _flash_step(s, v, m_ref, acc_ref, chunk=KEY_CHUNK):
    for c in range(s.shape[0] // chunk):
        sc = s[c * chunk:(c + 1) * chunk]
        m_old = m_ref[c]
        m_new = jnp.maximum(m_old, jnp.max(sc, axis=0, keepdims=True))
        p = jnp.exp2(sc - m_new).astype(BF16)
        acc_ref[c] = jnp.exp2(m_old - m_new) * acc_ref[c] + _dot(v[:, c * chunk:(c + 1) * chunk], p)
        m_ref[c] = m_new


def _flash_softmax(s, m_ref, p_ref, alpha_ref, chunk=KEY_CHUNK):
    for c in range(s.shape[0] // chunk):
        sc = s[c * chunk:(c + 1) * chunk]
        m_old = m_ref[c]
        m_new = jnp.maximum(m_old, jnp.max(sc, axis=0, keepdims=True))
        alpha_ref[c] = jnp.exp2(m_old - m_new)
        p_ref[c * chunk:(c + 1) * chunk, :] = jnp.exp2(sc - m_new).astype(BF16)
        m_ref[c] = m_new


def _flash_accumulate(v, p_ref, alpha_ref, acc_ref, chunk=KEY_CHUNK):
    for c in range(p_ref.shape[0] // chunk):
        rows = slice(c * chunk, (c + 1) * chunk)
        acc_ref[c] = alpha_ref[c] * acc_ref[c] + _dot(v[:, rows], p_ref[rows, :])


def _flash_finish(m_ref, acc_ref):
    n = m_ref.shape[0]
    m = m_ref[0]
    for c in range(1, n):
        m = jnp.maximum(m, m_ref[c])
    acc = sum(jnp.exp2(m_ref[c] - m) * acc_ref[c] for c in range(n))
    return acc[:MLA_V] / acc[MLA_V:MLA_V + 1]


def _pipelined_pairs(sched_ref, row, n, scores, softmax, accumulate, unroll=PAIR_UNROLL):
    assert n >= 2 and unroll % 2 == 0

    def pair(i):
        i = jnp.minimum(i, n - 1)
        return sched_ref[row, i], sched_ref[row + 1, i]

    def steps(first, count):
        for u in range(count):
            i, slot = first + u, u % 2
            if accumulate is None:
                scores(*pair(i + 1), 1 - slot)
                softmax(*pair(i), slot)
            else:
                scores(*pair(i + 2), slot)
                softmax(*pair(i + 1), 1 - slot)
                accumulate(*pair(i), slot)

    scores(*pair(0), 0)
    if accumulate is not None:
        scores(*pair(1), 1)
        softmax(*pair(0), 0)
    loops = n // unroll
    lax.fori_loop(0, loops, lambda j, carry: (steps(j * unroll, unroll), carry)[1], 0)
    steps(loops * unroll, n - loops * unroll)


def _mla_pairs():
    nq, nk = SEQ // MLA_Q, SEQ // MLA_K
    full = [(qt, kt) for kt in range(nk) for qt in range(nq) if (kt + 1) * MLA_K <= qt * MLA_Q]
    diag = [(qt, (qt * MLA_Q) // MLA_K) for qt in range(nq)]
    return full, diag


def _mla_kernel(sched_ref, q_ref, k_ref, vt_ref, eye_ref, wo_ref, wup_ref, wdn_ref,
                o_ref, wo16_ref, wup16_ref, wdn16_ref, m_ref, acc_ref, s_ref):
    for w_ref, w16_ref in ((wo_ref, wo16_ref), (wup_ref, wup16_ref), (wdn_ref, wdn16_ref)):
        w16_ref[...] = w_ref[...].astype(BF16)
    sub = MLA_K // V_TILE
    full, diag = _mla_pairs()
    rel = (lax.broadcasted_iota(jnp.int32, (MLA_K, MLA_Q), 1)
           - lax.broadcasted_iota(jnp.int32, (MLA_K, MLA_Q), 0))
    _flash_init(m_ref, acc_ref)

    def scores(qt, kt, slot, masked):
        qrows = pl.ds(pl.multiple_of(qt * MLA_Q, MLA_Q), MLA_Q)
        krows = pl.ds(pl.multiple_of(kt * MLA_K, MLA_K), MLA_K)
        for hh in range(2):
            cols = slice(hh * LANES, (hh + 1) * LANES)
            s = _dot_nt(k_ref[krows, cols], q_ref[qrows, cols])
            if masked:
                s = jnp.where(rel >= kt * MLA_K - qt * MLA_Q, s, NEG)
            s_ref[slot, hh] = s

    def update(qt, kt, slot):
        for hh in range(2):
            v = jnp.concatenate([vt_ref[kt * sub + i, hh * V_ROWS:(hh + 1) * V_ROWS, :] for i in range(sub)],
                                axis=1)
            _flash_step(s_ref[slot, hh], v, m_ref.at[hh, qt], acc_ref.at[hh, qt])

    _pipelined_pairs(sched_ref, 0, len(full), lambda qt, kt, slot: scores(qt, kt, slot, False), update, None)
    _pipelined_pairs(sched_ref, 2, len(diag), lambda qt, kt, slot: scores(qt, kt, slot, True), update, None)

    for qt in range(SEQ // MLA_Q):
        o_t = jnp.concatenate([_flash_finish(m_ref.at[hh, qt], acc_ref.at[hh, qt])
                               for hh in range(2)], axis=0)
        o_ref[qt * MLA_Q:(qt + 1) * MLA_Q, :] = _dot_nt(eye_ref[...], o_t.astype(BF16)).astype(BF16)


def _schedule(*pair_lists):
    n = max(len(p) for p in pair_lists)
    out = np.zeros((2 * len(pair_lists), n), np.int32)
    for i, pairs in enumerate(pair_lists):
        out[2 * i, :len(pairs)] = [a for a, _ in pairs]
        out[2 * i + 1, :len(pairs)] = [b for _, b in pairs]
    return jnp.asarray(out)


def _mla_attention(q, k, vt, w_o, w_up, w_down):
    b = q.shape[0] // SEQ
    nq = SEQ // MLA_Q
    nv = SEQ // V_TILE
    chunks = MLA_K // KEY_CHUNK
    head_pairs = MLA_HEADS // 2
    steps = b * head_pairs
    weights = [w_o, w_up, w_down]
    assert all(w.shape[0] % (steps * BF16_ROWS) == 0 for w in weights)
    w_specs = [pl.BlockSpec((w.shape[0] // steps, w.shape[1]), lambda i, hp: (i * head_pairs + hp, 0))
               for w in weights]
    return pl.pallas_call(
        _mla_kernel,
        grid=(b, head_pairs),
        in_specs=[pl.BlockSpec(memory_space=pltpu.SMEM),
                  pl.BlockSpec((SEQ, 2 * LANES), lambda i, hp: (i, hp)),
                  pl.BlockSpec((SEQ, 2 * LANES), lambda i, hp: (i, hp)),
                  pl.BlockSpec((nv, 2 * V_ROWS, V_TILE), lambda i, hp: (i, hp, 0)),
                  pl.BlockSpec((MLA_Q, MLA_Q), lambda i, hp: (0, 0))] + w_specs,
        out_specs=[pl.BlockSpec((SEQ, LANES), lambda i, hp: (i, hp))] + w_specs,
        out_shape=[jax.ShapeDtypeStruct((q.shape[0], MLA_HEADS * MLA_V), BF16)]
                  + [jax.ShapeDtypeStruct(w.shape, BF16) for w in weights],
        scratch_shapes=[pltpu.VMEM((2, nq, chunks, 1, MLA_Q), F32),
                        pltpu.VMEM((2, nq, chunks, V_ROWS, MLA_Q), F32),
                        pltpu.VMEM((2, 2, MLA_K, MLA_Q), F32)],
        compiler_params=pltpu.CompilerParams(
            dimension_semantics=("parallel", "parallel"), vmem_limit_bytes=VMEM_LIMIT),
        name="mla_attention",
    )(_schedule(*_mla_pairs()), q, k, vt, jnp.eye(MLA_Q, dtype=BF16), *weights)


def _select_kernel(q_ref, kc_ref, vct_ref, tabc_ref, ovt_ref, eye_ref, qpad_ref, qaug_ref, ocmp_ref):
    step = pl.program_id(1)
    t = SELECT_Q
    heads = range(NSA_GROUP)
    tiles = range(t // NSA_Q)

    def group(g):
        q_pad = _dot(q_ref[:, g * 2 * LANES:(g + 1) * 2 * LANES], qpad_ref[...]).astype(BF16)
        low = lax.broadcasted_iota(jnp.int32, (t, LANES), 1) < NSA_HEAD_DIM
        zero = jnp.zeros((t, LANES), BF16)
        pairs = [q_ref[:, (2 * g + j) * LANES:(2 * g + j + 1) * LANES] for j in range(NSA_GROUP // 2)]
        q4 = jnp.concatenate([jnp.where(low == (r % 2 == 0), pairs[r // 2], zero) for r in heads], axis=0)

        per_tile = NSA_Q // CMP_STRIDE
        rows = [pl.ds(pl.multiple_of(CMP_TAB_BASE - (step * len(tiles) + u) * per_tile, per_tile), N_CHUNK)
                for u in tiles]
        bias_c = jnp.concatenate([tabc_ref[g * NSA_GROUP + r, rows[u], :] for r in heads for u in tiles], axis=1)
        valid = bias_c > 0.5 * NEG
        sc = _dot_nt(kc_ref[0, g], q4) + bias_c
        mx = jnp.max(sc, axis=0, keepdims=True)
        p = jnp.where(valid, jnp.exp2(sc - mx), 0.0)
        pc = p / jnp.maximum(jnp.sum(p, axis=0, keepdims=True), 1e-30)
        o_cmp = _dot(vct_ref[0, g], pc.astype(BF16)).astype(BF16)
        for u in tiles:
            ocmp_ref[0, g, u] = jnp.concatenate(
                [o_cmp[:, r * t + u * NSA_Q:r * t + (u + 1) * NSA_Q] for r in heads], axis=0)

        psum = sum(pc[:, r * t:(r + 1) * t] for r in heads)
        imp = sum(_dot(ovt_ref[...], piece) for piece in _split(psum, 3))
        jj = lax.broadcasted_iota(jnp.int32, (N_SEL, t), 0)
        blk = (step * t + lax.broadcasted_iota(jnp.int32, (N_SEL, t), 1)) // SEL_BLOCK
        ok = jj <= blk
        forced = ok & ((jj == 0) | (jj == blk) | (jj == blk - 1))
        score = jnp.where(forced, FORCE_SCORE, jnp.where(ok, imp, -jnp.inf))
        rank = jnp.zeros((N_SEL, t), jnp.int32)
        for i in range(N_SEL):
            ci = score[i:i + 1, :]
            beats = (ci > score) | ((ci == score) & (jj > i))
            rank = rank + beats.astype(jnp.int32)
        negsel = jnp.where(ok & (rank < SEL_TOP_N), 0.0, NEG).astype(BF16)
        spread = jnp.concatenate([jnp.zeros((AUG_BLOCK, t), BF16), negsel,
                                  jnp.zeros((LANES - AUG_ONE, t), BF16)], axis=0)
        mask_lanes = _dot_nt(eye_ref[...], spread).astype(BF16)
        lane = lax.broadcasted_iota(jnp.int32, (t, LANES), 1)
        for r in heads:
            far = tabc_ref[g * NSA_GROUP + r, 0:1, :]
            far_hi = far.astype(BF16)
            far_lo = (far - far_hi.astype(F32)).astype(BF16)
            aug = jnp.where(lane == AUG_ONE, far_hi, jnp.where(lane == AUG_ONE + 1, far_lo, mask_lanes))
            q_aug = jnp.where(lane < AUG_BLOCK, q_pad[:, r * LANES:(r + 1) * LANES], aug)
            for u in tiles:
                qaug_ref[0, g, u, r * NSA_Q:(r + 1) * NSA_Q, :] = q_aug[u * NSA_Q:(u + 1) * NSA_Q]

    for g in range(NSA_KV_HEADS):
        group(g)


def _nsa_select(q, kc, vct, tables):
    b = q.shape[0] // SEQ
    t = SELECT_Q
    steps = SEQ // t
    nq = SEQ // NSA_Q
    per = t // NSA_Q
    g_ = NSA_KV_HEADS
    c0 = np.arange(N_CHUNK)[None, :] * CMP_STRIDE
    j0 = np.arange(N_SEL)[:, None] * SEL_BLOCK
    ovt = np.clip(np.minimum(c0 + CMP_BLOCK, j0 + SEL_BLOCK) - np.maximum(c0, j0), 0, None) / CMP_BLOCK
    ovt[:, N_CHUNK - 1:] = 0.0
    return pl.pallas_call(
        _select_kernel,
        grid=(b, steps),
        in_specs=[pl.BlockSpec((t, g_ * 2 * LANES), lambda i, j: (i * steps + j, 0)),
                  pl.BlockSpec((1, g_, N_CHUNK, LANES), lambda i, j: (i, 0, 0, 0)),
                  pl.BlockSpec((1, g_, NSA_HEAD_DIM, N_CHUNK), lambda i, j: (i, 0, 0, 0)),
                  pl.BlockSpec((NSA_HEADS, CMP_TAB_ROWS, LANES), lambda i, j: (0, WIN_ROWS // CMP_TAB_ROWS, 0)),
                  pl.BlockSpec((N_SEL, N_CHUNK), lambda i, j: (0, 0)),
                  pl.BlockSpec((t, t), lambda i, j: (0, 0)),
                  pl.BlockSpec((2 * LANES, NSA_GROUP * LANES), lambda i, j: (0, 0))],
        out_specs=[pl.BlockSpec((1, g_, per, NSA_GROUP * NSA_Q, LANES), lambda i, j: (i, 0, j, 0, 0)),
                   pl.BlockSpec((1, g_, per, NSA_GROUP * NSA_HEAD_DIM, NSA_Q), lambda i, j: (i, 0, j, 0, 0))],
        out_shape=[jax.ShapeDtypeStruct((b, g_, nq, NSA_GROUP * NSA_Q, LANES), BF16),
                   jax.ShapeDtypeStruct((b, g_, nq, NSA_GROUP * NSA_HEAD_DIM, NSA_Q), BF16)],
        compiler_params=pltpu.CompilerParams(
            dimension_semantics=("parallel", "parallel"), vmem_limit_bytes=VMEM_LIMIT),
        name="nsa_select",
    )(q, kc, vct, tables, jnp.asarray(ovt, BF16), jnp.eye(t, dtype=BF16),
      jnp.asarray(_head_padding(NSA_GROUP), BF16))


def _nsa_pairs():
    nq, nk, per = SEQ // NSA_Q, SEQ // NSA_K, NSA_K // NSA_Q
    sel = [(qt, kt) for kt in range(nk) for qt in range(nq) if kt * NSA_K <= qt * NSA_Q]
    far = [(qt, kt) for qt, kt in sel if qt - kt * per >= SEL_FAR_TILE]
    near = [(qt, kt) for qt, kt in sel if qt - kt * per < SEL_FAR_TILE]
    win = [(qt, kt) for kt in range(nk) for qt in range(nq)
           if max(qt * NSA_Q - (WINDOW - 1), 0) // NSA_K <= kt and kt * NSA_K <= qt * NSA_Q]
    return far, near, win


def _attend_kernel(sched_ref, qaug_ref, ksel_ref, kwin_ref, vsel_ref, vwin_ref, ocmp_ref, gate_ref,
                   tabw_ref, eye_ref, o_ref, m_ref, acc_ref, s_ref, p_ref, alpha_ref):
    t = NSA_Q
    heads = range(NSA_GROUP)
    far, near, win = _nsa_pairs()
    _flash_init(m_ref, acc_ref)

    def scores(qt, kt, slot, k_ref, add_bias):
        s = _dot_nt(k_ref[pl.ds(pl.multiple_of(kt * NSA_K, NSA_K), NSA_K), :], qaug_ref[0, 0, qt])
        if add_bias:
            off = qt - kt * (NSA_K // NSA_Q)
            bias_rows = pl.ds(pl.multiple_of(off * NSA_K, NSA_K), NSA_K)
            s = s + jnp.concatenate([tabw_ref[r, bias_rows, :] for r in heads], axis=1)
        s_ref[slot] = s

    def softmax(qt, kt, slot, branch):
        _flash_softmax(s_ref[slot], m_ref.at[branch, qt], p_ref.at[slot], alpha_ref.at[slot], chunk=NSA_K)

    def accumulate(qt, kt, slot, v_ref, branch):
        _flash_accumulate(v_ref[kt], p_ref.at[slot], alpha_ref.at[slot], acc_ref.at[branch, qt], chunk=NSA_K)

    for row, pairs, k_ref, v_ref, branch, add_bias in ((0, far, ksel_ref, vsel_ref, 0, False),
                                                       (2, near, ksel_ref, vsel_ref, 0, True),
                                                       (4, win, kwin_ref, vwin_ref, 1, True)):
        _pipelined_pairs(sched_ref, row, len(pairs),
                         lambda qt, kt, slot, k_ref=k_ref, add_bias=add_bias: scores(qt, kt, slot, k_ref, add_bias),
                         lambda qt, kt, slot, branch=branch: softmax(qt, kt, slot, branch),
                         lambda qt, kt, slot, v_ref=v_ref, branch=branch: accumulate(qt, kt, slot, v_ref, branch),
                         unroll=NSA_UNROLL)

    d = NSA_HEAD_DIM
    group = pl.program_id(1)
    for qt in range(SEQ // t):
        o_cmp = ocmp_ref[0, 0, qt].astype(F32)
        o_sel, o_win = [_flash_finish(m_ref.at[br, qt], acc_ref.at[br, qt]) for br in range(2)]
        mixed = []
        for r in heads:
            gate = [gate_ref[qt, pl.ds(br * NSA_HEADS + group * NSA_GROUP + r, 1), :]
                    for br in range(N_BRANCH)]
            mixed.append(gate[0] * o_cmp[r * d:(r + 1) * d] + gate[1] * o_sel[:, r * t:(r + 1) * t]
                         + gate[2] * o_win[:, r * t:(r + 1) * t])
        o_ref[qt * t:(qt + 1) * t, :] = _dot_nt(eye_ref[...], jnp.concatenate(mixed, axis=0).astype(BF16)).astype(BF16)


def _nsa_attend(qaug, kaug, vt, ocmp, gates, tables):
    b = kaug.shape[0] // SEQ
    t = NSA_Q
    nq = SEQ // t
    nv = SEQ // V_TILE
    g_ = NSA_KV_HEADS
    k_spec = lambda off: pl.BlockSpec((SEQ, LANES), lambda i, g: (i, off + g))
    v_spec = lambda off: pl.BlockSpec((nv, V_ROWS, V_TILE), lambda i, g: (i, off + g, 0))
    return pl.pallas_call(
        _attend_kernel,
        grid=(b, g_),
        in_specs=[pl.BlockSpec(memory_space=pltpu.SMEM),
                  pl.BlockSpec((1, 1, nq, NSA_GROUP * t, LANES), lambda i, g: (i, g, 0, 0, 0)),
                  k_spec(0), k_spec(2), v_spec(0), v_spec(2),
                  pl.BlockSpec((1, 1, nq, NSA_GROUP * NSA_HEAD_DIM, t), lambda i, g: (i, g, 0, 0, 0)),
                  pl.BlockSpec((nq, LANES, t), lambda i, g: (i, 0, 0)),
                  pl.BlockSpec((NSA_GROUP, WIN_ROWS, LANES), lambda i, g: (g, 0, 0)),
                  pl.BlockSpec((t, t), lambda i, g: (0, 0))],
        out_specs=pl.BlockSpec((SEQ, 2 * LANES), lambda i, g: (i, g)),
        out_shape=jax.ShapeDtypeStruct((kaug.shape[0], NSA_HEADS * NSA_HEAD_DIM), BF16),
        scratch_shapes=[pltpu.VMEM((2, nq, 1, 1, NSA_GROUP * t), F32),
                        pltpu.VMEM((2, nq, 1, V_ROWS, NSA_GROUP * t), F32),
                        pltpu.VMEM((2, NSA_K, NSA_GROUP * t), F32),
                        pltpu.VMEM((2, NSA_K, NSA_GROUP * t), BF16),
                        pltpu.VMEM((2, 1, 1, NSA_GROUP * t), F32)],
        compiler_params=pltpu.CompilerParams(
            dimension_semantics=("parallel", "parallel"), vmem_limit_bytes=VMEM_LIMIT),
        name="nsa_attend",
    )(_schedule(*_nsa_pairs()), qaug, kaug, kaug, vt, vt, ocmp, gates, tables, jnp.eye(t, dtype=BF16))


def _tail_kernel(x_ref, oa_ref, ob_ref, ga_ref, gb_ref, wo_ref, ln2_ref, wup_ref, wdn_ref,
                 out_ref):
    width = MLA_HEADS * MLA_V
    a = (_rms(oa_ref[...].astype(F32), width) * ga_ref[...]).astype(BF16)
    b = (_rms(ob_ref[...].astype(F32), width) * gb_ref[...]).astype(BF16)
    x1 = x_ref[...] + _dot(a, wo_ref[:width]) + _dot(b, wo_ref[width:])
    h2 = (_rms(x1, D_MODEL) * ln2_ref[...]).astype(BF16)
    u = jnp.maximum(_dot(h2, wup_ref[...]), 0.0)
    out_ref[...] = x1 + _dot((u * u).astype(BF16), wdn_ref[...])


def _tail(x2d, o_a, o_b, gn_a, gn_b, w_o, ln2_g, w_up, w_down):
    t = x2d.shape[0]
    rows = TAIL_ROWS
    width = MLA_HEADS * MLA_V
    once = lambda a: pl.BlockSpec(a.shape, lambda i: (0,) * a.ndim)
    assert w_o.dtype == w_up.dtype == w_down.dtype == BF16
    consts = [gn_a[None, :], gn_b[None, :], w_o, ln2_g[None, :], w_up, w_down]
    row_spec = lambda w: pl.BlockSpec((rows, w), lambda i: (i, 0))
    return pl.pallas_call(
        _tail_kernel,
        grid=(t // rows,),
        in_specs=[row_spec(D_MODEL), row_spec(width), row_spec(width)] + [once(a) for a in consts],
        out_specs=row_spec(D_MODEL),
        out_shape=jax.ShapeDtypeStruct((t, D_MODEL), F32),
        compiler_params=pltpu.CompilerParams(dimension_semantics=("parallel",),
                                             vmem_limit_bytes=VMEM_LIMIT),
        name="tail",
    )(x2d, o_a, o_b, *consts)


def kernel(x, ln1_g, w_in, mla_cq_norm_g, mla_ckv_norm_g, mla_w_uq, mla_w_ukv, mla_q_gain, mla_k_gain,
           nsa_q_gain, nsa_k_gain, nsa_cmp_pe, nsa_cmp_w1, nsa_cmp_w2, rel_bias, grp_norm_mla,
           grp_norm_nsa, w_o, ln2_g, w_up, w_down):
    b, s, d = x.shape
    assert (s, d) == (SEQ, D_MODEL) and ln1_g.shape[0] == 1
    x2d = x.reshape(b * s, d)
    tables = _bias_tables(rel_bias)
    q_mla, k_mla, vt_mla, q_nsa, kdup, vt_nsa, kck, kcv, gates = _projections(
        x2d, ln1_g[0], w_in[0], mla_cq_norm_g[0], mla_ckv_norm_g[0], mla_w_uq[0], mla_w_ukv[0],
        mla_q_gain[0], mla_k_gain[0], nsa_q_gain[0], nsa_k_gain[0])
    kc, vct = _compress(kck, kcv, nsa_cmp_pe[0], nsa_cmp_w1[0], nsa_cmp_w2[0], nsa_k_gain[0, 0])
    o_a, w_o16, w_up16, w_down16 = _mla_attention(q_mla, k_mla, vt_mla, w_o[0], w_up[0], w_down[0])
    q_aug, ocmp = _nsa_select(q_nsa, kc, vct, tables)
    o_b = _nsa_attend(q_aug, kdup, vt_nsa, ocmp, gates, tables)
    out = _tail(x2d, o_a, o_b, grp_norm_mla[0], grp_norm_nsa[0], w_o16, ln2_g[0], w_up16, w_down16)
    return out.reshape(b, s, d)
```

```python
import math

import numpy as np
import jax
import jax.numpy as jnp
from jax import lax
from jax.experimental import pallas as pl
from jax.experimental.pallas import tpu as pltpu

F32 = jnp.float32
BF16 = jnp.bfloat16

D_MODEL = 1024
SEQ = 2048
MLA_HEADS = 8
MLA_NOPE = 64
MLA_ROPE = 32
MLA_V = 64
MLA_QK = MLA_NOPE + MLA_ROPE
MLA_Q_RANK = 384
MLA_KV_RANK = 256
ROPE_THETA = 10000.0
NSA_HEADS = 8
NSA_KV_HEADS = 2
NSA_GROUP = NSA_HEADS // NSA_KV_HEADS
NSA_HEAD_DIM = 64
N_BRANCH = 3
CMP_BLOCK = 32
CMP_STRIDE = 16
CMP_HIDDEN = 128
SEL_BLOCK = 64
SEL_TOP_N = 16
WINDOW = 512
FORCE_SCORE = 1e4
REL_BUCKETS = 32
REL_MAX_DIST = 128
D_FF = 4 * D_MODEL
EPS = 1e-6

LANES = 128
SUBLANES = 8
BF16_ROWS = 16
NEG = -1e30
N_CHUNK = SEQ // CMP_STRIDE
N_SEL = SEQ // SEL_BLOCK
NSA_Q = 128
NSA_K = 256
SELECT_Q = 512
N_BIAS_TILES = (WINDOW + NSA_K - NSA_Q) // NSA_Q + 1
SEL_FAR_TILE = (REL_MAX_DIST + NSA_K) // NSA_Q
WIN_ROWS = N_BIAS_TILES * NSA_K
AUG_BLOCK = NSA_HEAD_DIM
AUG_ONE = AUG_BLOCK + N_SEL
CMP_TAB_BASE = (SEQ // NSA_Q - 1) * (NSA_Q // CMP_STRIDE)
CMP_TAB_ROWS = 256
MLA_Q = 256
MLA_K = 512
V_TILE = 256
PROJ_ROWS = 512
PACK_ROWS = 256
TAIL_ROWS = 512
KEY_CHUNK = 256
PAIR_UNROLL = 6
NSA_UNROLL = 12
V_ROWS = 80
LOG2E = math.log2(math.e)
VMEM_LIMIT = 56 * 1024 * 1024

assert PROJ_ROWS % V_TILE == 0 and MLA_K % V_TILE == 0 and NSA_K == V_TILE


def _dot(a, b):
    return jnp.dot(a, b, preferred_element_type=F32)


def _dot_nt(a, b):
    return lax.dot_general(a, b, (((1,), (1,)), ((), ())), preferred_element_type=F32)


def _split(a, terms):
    pieces = []
    rem = a
    for _ in range(terms):
        piece = rem.astype(BF16)
        pieces.append(piece)
        rem = rem - piece.astype(F32)
    return pieces


def _split_dot(a, b, terms=2):
    return sum(_dot(p, b) for p in _split(a, terms))


def _rms(x, width):
    return x * lax.rsqrt(jnp.sum(x * x, axis=-1, keepdims=True) * (1.0 / width) + EPS)


def _t5_bucket_np(dist):
    n = np.maximum(dist, 0)
    max_exact = REL_BUCKETS // 2
    large = max_exact + (np.log(np.maximum(n, 1).astype(np.float32) / max_exact)
                         / math.log(REL_MAX_DIST / max_exact)
                         * (REL_BUCKETS - max_exact)).astype(np.int32)
    large = np.minimum(large, REL_BUCKETS - 1)
    return np.where(n < max_exact, n, large).astype(np.int32)


def _bias_index_table():
    i = np.arange(NSA_Q)[None, :]
    parts = []
    j = np.arange(NSA_K)[:, None]
    for off in range(N_BIAS_TILES):
        d = off * NSA_Q + i - j
        parts.append(np.where((d >= 0) & (d < WINDOW), _t5_bucket_np(d), -1))
    r = np.arange(CMP_TAB_ROWS)[:, None]
    dist_c = (CMP_TAB_BASE - r) * CMP_STRIDE + i - (CMP_BLOCK - 1)
    parts.append(np.where(dist_c >= 0, _t5_bucket_np(dist_c), -1))
    return np.concatenate(parts, axis=0).astype(np.int32)


def _bias_table_kernel(rb_ref, idx_ref, out_ref):
    idx = idx_ref[...]
    bucket = jnp.maximum(idx, 0)
    is_window = pl.program_id(0) < WIN_ROWS // idx.shape[0]
    for h in range(NSA_HEADS):
        row = rb_ref[h:h + 1, :]
        far = jnp.where(is_window, row[:, REL_BUCKETS - 1:REL_BUCKETS], 0.0)
        values = jnp.broadcast_to((row - far) * LOG2E, idx.shape)
        picked = jnp.take_along_axis(values, bucket, axis=1, mode="promise_in_bounds")
        out_ref[h] = jnp.where(idx >= 0, picked, NEG)


def _bias_tables(rel_bias):
    idx = jnp.asarray(_bias_index_table())
    rows = idx.shape[0]
    blk = CMP_TAB_ROWS
    return pl.pallas_call(
        _bias_table_kernel,
        grid=(rows // blk,),
        in_specs=[pl.BlockSpec((NSA_HEADS, LANES), lambda r: (0, 0)),
                  pl.BlockSpec((blk, LANES), lambda r: (r, 0))],
        out_specs=pl.BlockSpec((NSA_HEADS, blk, LANES), lambda r: (0, r, 0)),
        out_shape=jax.ShapeDtypeStruct((NSA_HEADS, rows, LANES), F32),
        name="bias_tables",
    )(jnp.pad(rel_bias.T, ((0, 0), (0, LANES - REL_BUCKETS))), idx)


ROPE_HALF = MLA_ROPE // 2
MISC_KR = 64


def _rope_tables(scale):
    inv = 1.0 / (ROPE_THETA ** (np.arange(ROPE_HALF, dtype=np.float64) / ROPE_HALF))
    ang = np.arange(SEQ, dtype=np.float64)[:, None] * inv[None, :]
    cos, sin = np.cos(ang), np.sin(ang)
    ones = np.ones((SEQ, MLA_NOPE))
    z32 = np.zeros((SEQ, LANES - MLA_QK))
    z64 = np.zeros((SEQ, MLA_NOPE))
    c = np.concatenate([ones, cos, cos, z32], axis=1) * scale
    s = np.concatenate([z64, -sin, sin, z32], axis=1) * scale
    return jnp.asarray(c, F32), jnp.asarray(s, F32)


def _pair_norm(x, gain_ref):
    low = lax.broadcasted_iota(jnp.int32, (x.shape[0], LANES), 1) < NSA_HEAD_DIM
    outs = []
    for j in range(x.shape[1] // LANES):
        cols = slice(j * LANES, (j + 1) * LANES)
        sl = x[:, cols]
        sq = sl * sl
        tot = jnp.sum(sq, axis=-1, keepdims=True)
        lo = jnp.sum(jnp.where(low, sq, 0.0), axis=-1, keepdims=True)
        rs = jnp.where(low, lax.rsqrt(lo * (1.0 / NSA_HEAD_DIM) + EPS),
                       lax.rsqrt((tot - lo) * (1.0 / NSA_HEAD_DIM) + EPS))
        outs.append(sl * rs * gain_ref[:, cols])
    return jnp.concatenate(outs, axis=1)


def _mla_query_heads(x, gain_ref, c, s):
    in_head = lax.broadcasted_iota(jnp.int32, (x.shape[0], LANES), 1) < MLA_QK
    outs = []
    for h in range(MLA_HEADS):
        cols = slice(h * LANES, (h + 1) * LANES)
        sl = x[:, cols]
        ss = jnp.sum(jnp.where(in_head, sl * sl, 0.0), axis=-1, keepdims=True)
        xn = sl * lax.rsqrt(ss * (1.0 / MLA_QK) + EPS) * gain_ref[:, cols]
        outs.append(xn * c + pltpu.roll(xn, LANES - ROPE_HALF, 1) * s)
    return jnp.concatenate(outs, axis=1)


def _mla_key_heads(x, k_rope, k_swap, c, s):
    in_head = lax.broadcasted_iota(jnp.int32, (x.shape[0], LANES), 1) < MLA_QK
    turned = k_swap * s
    outs = []
    for h in range(MLA_HEADS):
        sl = x[:, h * LANES:(h + 1) * LANES] + k_rope
        ss = jnp.sum(jnp.where(in_head, sl * sl, 0.0), axis=-1, keepdims=True)
        outs.append(lax.rsqrt(ss * (1.0 / MLA_QK) + EPS) * (sl * c + turned))
    return jnp.concatenate(outs, axis=1)


COL_KV = MLA_Q_RANK
COL_MISC = COL_KV + MLA_KV_RANK
COL_SWAP = COL_MISC + LANES
COL_NQ = COL_SWAP + LANES
COL_NK = COL_NQ + NSA_HEADS * NSA_HEAD_DIM
COL_NV = COL_NK + 4 * NSA_HEAD_DIM
COL_CMP = COL_NV + 4 * NSA_HEAD_DIM
COL_END = COL_CMP + 4 * NSA_HEAD_DIM


def _proj_kernel(x_ref, ln1_ref, wp_ref, cqg_ref, ckvg_ref, wuq_ref, wukk_ref, wukvt_ref,
                 qg_ref, kown_ref, kpartner_ref, qc_ref, qs_ref, kc_ref, ks_ref,
                 nq_gain_ref, nk_gain_ref, kpad_ref, kaug_ref, vpick_ref, ones_mla_ref, ones_nsa_ref, eye_ref,
                 qmla_ref, kmla_ref, vmla_ref, qnsa_ref, kdup_ref, vnsa_ref, kck_ref, kcv_ref, gate_ref):
    for sub in range(PROJ_ROWS // V_TILE):
        rows = slice(sub * V_TILE, (sub + 1) * V_TILE)
        x = x_ref[rows, :]
        h = (_rms(x, D_MODEL) * ln1_ref[...]).astype(BF16)
        part = lambda a, b: _dot(h, wp_ref[:, a:b])

        cq = (_rms(part(0, COL_KV), MLA_Q_RANK) * cqg_ref[...]).astype(BF16)
        qmla_ref[rows, :] = _mla_query_heads(_dot(cq, wuq_ref[...]), qg_ref,
                                             qc_ref[rows, :], qs_ref[rows, :]).astype(BF16)

        kv_misc = part(COL_KV, COL_NQ)
        ckv = (_rms(kv_misc[:, :MLA_KV_RANK], MLA_KV_RANK) * ckvg_ref[...]).astype(BF16)
        misc = kv_misc[:, COL_MISC - COL_KV:COL_SWAP - COL_KV]
        lane = lax.broadcasted_iota(jnp.int32, misc.shape, 1)
        k_rope = jnp.where((lane >= MISC_KR) & (lane < MISC_KR + MLA_ROPE), misc, 0.0)
        kmla_ref[rows, :] = _mla_key_heads(_dot(ckv, wukk_ref[...]), k_rope, kv_misc[:, COL_SWAP - COL_KV:],
                                           kc_ref[rows, :] * kown_ref[...],
                                           ks_ref[rows, :] * kpartner_ref[...]).astype(BF16)
        vmla_ref[sub] = (_dot_nt(wukvt_ref[...], ckv) + ones_mla_ref[...]).astype(BF16)

        qnsa_ref[rows, :] = _pair_norm(part(COL_NQ, COL_NK), nq_gain_ref).astype(BF16)

        kn = _pair_norm(part(COL_NK, COL_NV), nk_gain_ref)
        kdup_ref[rows, :] = (_dot(kn.astype(BF16), kpad_ref[...]) + kaug_ref[rows, :]).astype(BF16)
        vnsa_ref[sub] = (_dot_nt(vpick_ref[...], part(COL_NV, COL_CMP).astype(BF16)) + ones_nsa_ref[...]).astype(BF16)
        cmp_kv = part(COL_CMP, COL_END)
        kck_ref[rows, :] = cmp_kv[:, :LANES]
        kcv_ref[rows, :] = cmp_kv[:, LANES:]
        gate = 1.0 / (1.0 + jnp.exp(-misc))
        gate_t = sum(_dot_nt(eye_ref[...], piece) for piece in _split(gate, 2))
        for u in range(V_TILE // NSA_Q):
            gate_ref[sub * (V_TILE // NSA_Q) + u] = gate_t[:, u * NSA_Q:(u + 1) * NSA_Q]


def _head_padding(heads):
    d = NSA_HEAD_DIM
    m = np.zeros((heads * d, heads * LANES), np.float32)
    for hh in range(heads):
        m[hh * d + np.arange(d), hh * LANES + np.arange(d)] = 1.0
    return m


def _packed_segments():
    kv0 = 1184
    seg = lambda a, b: (a, b - a)
    kvb = lambda br, kv, g: (kv0 + ((br * 2 + kv) * 2 + g) * 64, 64)
    zeros = lambda n: (None, n)
    kr = COL_MISC
    gate_cols = N_BRANCH * NSA_HEADS
    pieces = [
        seg(0, COL_MISC),
        seg(1952, 1952 + gate_cols), zeros(MISC_KR - gate_cols), seg(kr, kr + MLA_ROPE),
        zeros(LANES - MISC_KR - MLA_ROPE),
        zeros(MISC_KR), seg(kr + ROPE_HALF, kr + MLA_ROPE), seg(kr, kr + ROPE_HALF), zeros(LANES - MISC_KR - MLA_ROPE),
        seg(672, 1184),
        kvb(1, 0, 0), kvb(1, 0, 1), kvb(2, 0, 0), kvb(2, 0, 1),
        kvb(1, 1, 0), kvb(1, 1, 1), kvb(2, 1, 0), kvb(2, 1, 1),
        kvb(0, 0, 0), kvb(0, 0, 1), kvb(0, 1, 0), kvb(0, 1, 1),
    ]
    assert sum(n for _, n in pieces) == COL_END
    return pieces


def _pack_kernel(wt_ref, out_ref, packed_ref):
    dst = 0
    for src, n in _packed_segments():
        assert n % SUBLANES == 0 and dst % SUBLANES == 0 and (src is None or src % SUBLANES == 0)
        if src is None:
            packed_ref[dst:dst + n, :] = jnp.zeros((n, packed_ref.shape[1]), F32)
        else:
            packed_ref[dst:dst + n, :] = wt_ref[src:src + n, :]
        dst += n
    for g in range(COL_END // LANES):
        cols = slice(g * LANES, (g + 1) * LANES)
        out_ref[:, cols] = packed_ref[cols, :].T.astype(BF16)


def _pack_input_projection(w_in):
    rows = PACK_ROWS
    return pl.pallas_call(
        _pack_kernel,
        grid=(D_MODEL // rows,),
        in_specs=[pl.BlockSpec((w_in.shape[1], rows), lambda i: (0, i))],
        out_specs=pl.BlockSpec((rows, COL_END), lambda i: (i, 0)),
        out_shape=jax.ShapeDtypeStruct((D_MODEL, COL_END), BF16),
        scratch_shapes=[pltpu.VMEM((COL_END, rows), F32)],
        compiler_params=pltpu.CompilerParams(dimension_semantics=("parallel",),
                                             vmem_limit_bytes=VMEM_LIMIT),
        name="pack_input_projection",
    )(w_in.T)


def _projections(x2d, ln1_g, w_in, cq_g, ckv_g, w_uq, w_ukv, q_gain, k_gain, nq_gain, nk_gain):
    t = x2d.shape[0]
    wp = _pack_input_projection(w_in)
    wq3 = w_uq.reshape(MLA_Q_RANK, MLA_HEADS, MLA_QK).astype(BF16)
    wuq = jnp.concatenate([wq3, wq3[:, :, MLA_NOPE:MLA_NOPE + ROPE_HALF],
                           jnp.zeros((MLA_Q_RANK, MLA_HEADS, ROPE_HALF), BF16)], axis=2)
    wuq = wuq.reshape(MLA_Q_RANK, MLA_HEADS * LANES)
    wkv = w_ukv.reshape(MLA_KV_RANK, MLA_HEADS, MLA_NOPE + MLA_V)
    wukk = jnp.pad(wkv[:, :, :MLA_NOPE], ((0, 0), (0, 0), (0, LANES - MLA_NOPE)))
    wukk = wukk.reshape(MLA_KV_RANK, MLA_HEADS * LANES).astype(BF16)
    wukvt = jnp.pad(wkv[:, :, MLA_NOPE:], ((0, 0), (0, 0), (0, V_ROWS - MLA_V)))
    wukvt = wukvt.reshape(MLA_KV_RANK, MLA_HEADS * V_ROWS).T.astype(BF16)
    q_slab_gain = jnp.tile(jnp.concatenate(
        [q_gain, q_gain[MLA_NOPE:MLA_NOPE + ROPE_HALF], jnp.zeros((ROPE_HALF,), F32)]), MLA_HEADS)[None, :]
    qc, qs = _rope_tables(MLA_QK ** -0.5 * LOG2E)
    kc, ks = _rope_tables(1.0)
    k_own = jnp.pad(k_gain, (0, LANES - MLA_QK))[None, :]
    k_partner = jnp.concatenate([k_gain[:MLA_NOPE], k_gain[MLA_NOPE + ROPE_HALF:],
                                 k_gain[MLA_NOPE:MLA_NOPE + ROPE_HALF], jnp.zeros((LANES - MLA_QK,), F32)])[None, :]
    kpad = _head_padding(4)
    kaug = np.zeros((SEQ, 4 * LANES), np.float32)
    for s in range(4):
        kaug[:, s * LANES + AUG_ONE:s * LANES + AUG_ONE + 2] = 1.0
        if s < NSA_KV_HEADS:
            kaug[np.arange(SEQ), s * LANES + AUG_BLOCK + np.arange(SEQ) // SEL_BLOCK] = 1.0
    nqg = (jnp.tile(nq_gain, NSA_HEADS) * (NSA_HEAD_DIM ** -0.5 * LOG2E))[None, :]
    nkg = jnp.concatenate([nk_gain[1], nk_gain[1], nk_gain[2], nk_gain[2]])[None, :]
    vpick = np.zeros((4 * V_ROWS, 256), np.float32)
    for s in range(4):
        vpick[s * V_ROWS + np.arange(64), s * 64 + np.arange(64)] = 1.0

    def ones_col(heads):
        col = np.zeros((heads * V_ROWS, 1), np.float32)
        col[np.arange(heads) * V_ROWS + MLA_V] = 1.0
        return col

    rows = PROJ_ROWS
    n_pos = SEQ // rows
    full = lambda a: pl.BlockSpec(a.shape, lambda i: (0,) * a.ndim)
    tab = pl.BlockSpec((rows, LANES), lambda i: (i % n_pos, 0))
    consts = [ln1_g[None, :], wp, cq_g[None, :], ckv_g[None, :], wuq, wukk, wukvt, q_slab_gain, k_own, k_partner]
    tabs = [qc, qs, kc, ks]
    mats = [nqg, nkg, jnp.asarray(kpad, BF16),
            jnp.asarray(vpick, BF16), jnp.asarray(ones_col(MLA_HEADS)), jnp.asarray(ones_col(4)),
            jnp.eye(LANES, dtype=BF16)]
    row_spec = lambda w: pl.BlockSpec((rows, w), lambda i: (i, 0))
    vt_spec = lambda w, tile: pl.BlockSpec((rows // tile, w, tile), lambda i: (i, 0, 0))
    sds = jax.ShapeDtypeStruct
    return pl.pallas_call(
        _proj_kernel,
        grid=(t // rows,),
        in_specs=([row_spec(D_MODEL)] + [full(a) for a in consts] + [tab] * 4 + [full(a) for a in mats[:3]]
                  + [pl.BlockSpec((rows, 4 * LANES), lambda i: (i % n_pos, 0))] + [full(a) for a in mats[3:]]),
        out_specs=[row_spec(1024), row_spec(1024), vt_spec(MLA_HEADS * V_ROWS, V_TILE), row_spec(512),
                   row_spec(512), vt_spec(4 * V_ROWS, V_TILE), row_spec(128), row_spec(128),
                   vt_spec(LANES, NSA_Q)],
        out_shape=[sds((t, 1024), BF16), sds((t, 1024), BF16), sds((t // V_TILE, MLA_HEADS * V_ROWS, V_TILE), BF16),
                   sds((t, 512), BF16), sds((t, 512), BF16), sds((t // V_TILE, 4 * V_ROWS, V_TILE), BF16),
                   sds((t, 128), F32), sds((t, 128), F32), sds((t // NSA_Q, LANES, NSA_Q), F32)],
        compiler_params=pltpu.CompilerParams(dimension_semantics=("parallel",),
                                             vmem_limit_bytes=VMEM_LIMIT),
        name="projections",
    )(x2d, *consts, *tabs, *mats[:3], jnp.asarray(kaug, BF16), *mats[3:])


def _compress_kernel(xk_ref, xv_ref, pe_ref, w1_ref, w2_ref, kg_ref, dupk_ref, pickv_ref, kc_ref, vct_ref):
    def mlp(x_ref, which):
        top = bot = None
        for t in range(CMP_STRIDE):
            xt = x_ref[pl.ds(t, N_CHUNK, stride=CMP_STRIDE), :]
            a = _dot((xt + pe_ref[which, t]).astype(BF16), w1_ref[which, t])
            b = _dot((xt + pe_ref[which, CMP_STRIDE + t]).astype(BF16), w1_ref[which, CMP_STRIDE + t])
            top = a if top is None else top + a
            bot = b if bot is None else bot + b
        hid = top + pltpu.roll(bot, N_CHUNK - 1, 0)
        act = (hid * (1.0 / (1.0 + jnp.exp(-hid)))).astype(BF16)
        return _dot(act, w2_ref[which])

    k = _pair_norm(mlp(xk_ref, 0), kg_ref).astype(BF16)
    v = mlp(xv_ref, 1).astype(BF16)
    for g in range(NSA_KV_HEADS):
        kc_ref[0, g] = _dot(k, dupk_ref[g]).astype(BF16)
        vct_ref[0, g] = _dot_nt(pickv_ref[g], v).astype(BF16)


def _compress(kck, kcv, cmp_pe, cmp_w1, cmp_w2, k_gain0):
    b = kck.shape[0] // SEQ
    g_ = NSA_KV_HEADS
    d = NSA_HEAD_DIM
    pe = jnp.tile(cmp_pe, (1, 1, g_))[:, :, None, :]
    w1 = cmp_w1.reshape(2, CMP_BLOCK, d, CMP_HIDDEN)
    z1 = jnp.zeros_like(w1)
    w1 = jnp.concatenate([jnp.concatenate([w1, z1], axis=3), jnp.concatenate([z1, w1], axis=3)], axis=2)
    z2 = jnp.zeros_like(cmp_w2)
    w2 = jnp.concatenate([jnp.concatenate([cmp_w2, z2], axis=2), jnp.concatenate([z2, cmp_w2], axis=2)], axis=1)
    dupk = np.zeros((g_, LANES, LANES), np.float32)
    pickv = np.zeros((g_, d, LANES), np.float32)
    for g in range(g_):
        dupk[g, g * d + np.arange(d), np.arange(d)] = 1.0
        dupk[g, g * d + np.arange(d), d + np.arange(d)] = 1.0
        pickv[g, np.arange(d), g * d + np.arange(d)] = 1.0
    full = lambda a: pl.BlockSpec(a.shape, lambda i: (0,) * a.ndim)
    consts = [pe, w1.astype(BF16), w2.astype(BF16), jnp.tile(k_gain0, g_)[None, :],
              jnp.asarray(dupk, BF16), jnp.asarray(pickv, BF16)]
    x_spec = pl.BlockSpec((SEQ, LANES), lambda i: (i, 0))
    return pl.pallas_call(
        _compress_kernel,
        grid=(b,),
        in_specs=[x_spec, x_spec] + [full(a) for a in consts],
        out_specs=[pl.BlockSpec((1, g_, N_CHUNK, LANES), lambda i: (i, 0, 0, 0)),
                   pl.BlockSpec((1, g_, d, N_CHUNK), lambda i: (i, 0, 0, 0))],
        out_shape=[jax.ShapeDtypeStruct((b, g_, N_CHUNK, LANES), BF16),
                   jax.ShapeDtypeStruct((b, g_, d, N_CHUNK), BF16)],
        compiler_params=pltpu.CompilerParams(dimension_semantics=("parallel",)),
        name="compression",
    )(kck, kcv, *consts)


def _flash_init(m_ref, acc_ref):
    m_ref[...] = jnp.full(m_ref.shape, NEG, F32)
    acc_ref[...] = jnp.zeros(acc_ref.shape, F32)


def _flash_step(s, v, m_ref, acc_ref, chunk=KEY_CHUNK):
    for c in range(s.shape[0] // chunk):
        sc = s[c * chunk:(c + 1) * chunk]
        m_old = m_ref[c]
        m_new = jnp.maximum(m_old, jnp.max(sc, axis=0, keepdims=True))
        p = jnp.exp2(sc - m_new).astype(BF16)
        acc_ref[c] = jnp.exp2(m_old - m_new) * acc_ref[c] + _dot(v[:, c * chunk:(c + 1) * chunk], p)
        m_ref[c] = m_new


def _flash_softmax(s, m_ref, p_ref, alpha_ref, chunk=KEY_CHUNK):
    for c in range(s.shape[0] // chunk):
        sc = s[c * chunk:(c + 1) * chunk]
        m_old = m_ref[c]
        m_new = jnp.maximum(m_old, jnp.max(sc, axis=0, keepdims=True))
        alpha_ref[c] = jnp.exp2(m_old - m_new)
        p_ref[c * chunk:(c + 1) * chunk, :] = jnp.exp2(sc - m_new).astype(BF16)
        m_ref[c] = m_new


def _flash_accumulate(v, p_ref, alpha_ref, acc_ref, chunk=KEY_CHUNK):
    for c in range(p_ref.shape[0] // chunk):
        rows = slice(c * chunk, (c + 1) * chunk)
        acc_ref[c] = alpha_ref[c] * acc_ref[c] + _dot(v[:, rows], p_ref[rows, :])


def _flash_finish(m_ref, acc_ref):
    n = m_ref.shape[0]
    m = m_ref[0]
    for c in range(1, n):
        m = jnp.maximum(m, m_ref[c])
    acc = sum(jnp.exp2(m_ref[c] - m) * acc_ref[c] for c in range(n))
    return acc[:MLA_V] / acc[MLA_V:MLA_V + 1]


def _pipelined_pairs(sched_ref, row, n, scores, softmax, accumulate, unroll=PAIR_UNROLL):
    assert n >= 2 and unroll % 2 == 0

    def pair(i):
        i = jnp.minimum(i, n - 1)
        return sched_ref[row, i], sched_ref[row + 1, i]

    def steps(first, count):
        for u in range(count):
            i, slot = first + u, u % 2
            if accumulate is None:
                scores(*pair(i + 1), 1 - slot)
                softmax(*pair(i), slot)
            else:
                scores(*pair(i + 2), slot)
                softmax(*pair(i + 1), 1 - slot)
                accumulate(*pair(i), slot)

    scores(*pair(0), 0)
    if accumulate is not None:
        scores(*pair(1), 1)
        softmax(*pair(0), 0)
    loops = n // unroll
    lax.fori_loop(0, loops, lambda j, carry: (steps(j * unroll, unroll), carry)[1], 0)
    steps(loops * unroll, n - loops * unroll)


def _mla_pairs():
    nq, nk = SEQ // MLA_Q, SEQ // MLA_K
    full = [(qt, kt) for kt in range(nk) for qt in range(nq) if (kt + 1) * MLA_K <= qt * MLA_Q]
    diag = [(qt, (qt * MLA_Q) // MLA_K) for qt in range(nq)]
    return full, diag


def _mla_kernel(sched_ref, q_ref, k_ref, vt_ref, eye_ref, wo_ref, wup_ref, wdn_ref,
                o_ref, wo16_ref, wup16_ref, wdn16_ref, m_ref, acc_ref, s_ref):
    for w_ref, w16_ref in ((wo_ref, wo16_ref), (wup_ref, wup16_ref), (wdn_ref, wdn16_ref)):
        w16_ref[...] = w_ref[...].astype(BF16)
    sub = MLA_K // V_TILE
    full, diag = _mla_pairs()
    rel = (lax.broadcasted_iota(jnp.int32, (MLA_K, MLA_Q), 1)
           - lax.broadcasted_iota(jnp.int32, (MLA_K, MLA_Q), 0))
    _flash_init(m_ref, acc_ref)

    def scores(qt, kt, slot, masked):
        qrows = pl.ds(pl.multiple_of(qt * MLA_Q, MLA_Q), MLA_Q)
        krows = pl.ds(pl.multiple_of(kt * MLA_K, MLA_K), MLA_K)
        for hh in range(2):
            cols = slice(hh * LANES, (hh + 1) * LANES)
            s = _dot_nt(k_ref[krows, cols], q_ref[qrows, cols])
            if masked:
                s = jnp.where(rel >= kt * MLA_K - qt * MLA_Q, s, NEG)
            s_ref[slot, hh] = s

    def update(qt, kt, slot):
        for hh in range(2):
            v = jnp.concatenate([vt_ref[kt * sub + i, hh * V_ROWS:(hh + 1) * V_ROWS, :] for i in range(sub)],
                                axis=1)
            _flash_step(s_ref[slot, hh], v, m_ref.at[hh, qt], acc_ref.at[hh, qt])

    _pipelined_pairs(sched_ref, 0, len(full), lambda qt, kt, slot: scores(qt, kt, slot, False), update, None)
    _pipelined_pairs(sched_ref, 2, len(diag), lambda qt, kt, slot: scores(qt, kt, slot, True), update, None)

    for qt in range(SEQ // MLA_Q):
        o_t = jnp.concatenate([_flash_finish(m_ref.at[hh, qt], acc_ref.at[hh, qt])
                               for hh in range(2)], axis=0)
        o_ref[qt * MLA_Q:(qt + 1) * MLA_Q, :] = _dot_nt(eye_ref[...], o_t.astype(BF16)).astype(BF16)


def _schedule(*pair_lists):
    n = max(len(p) for p in pair_lists)
    out = np.zeros((2 * len(pair_lists), n), np.int32)
    for i, pairs in enumerate(pair_lists):
        out[2 * i, :len(pairs)] = [a for a, _ in pairs]
        out[2 * i + 1, :len(pairs)] = [b for _, b in pairs]
    return jnp.asarray(out)


def _mla_attention(q, k, vt, w_o, w_up, w_down):
    b = q.shape[0] // SEQ
    nq = SEQ // MLA_Q
    nv = SEQ // V_TILE
    chunks = MLA_K // KEY_CHUNK
    head_pairs = MLA_HEADS // 2
    steps = b * head_pairs
    weights = [w_o, w_up, w_down]
    assert all(w.shape[0] % (steps * BF16_ROWS) == 0 for w in weights)
    w_specs = [pl.BlockSpec((w.shape[0] // steps, w.shape[1]), lambda i, hp: (i * head_pairs + hp, 0))
               for w in weights]
    return pl.pallas_call(
        _mla_kernel,
        grid=(b, head_pairs),
        in_specs=[pl.BlockSpec(memory_space=pltpu.SMEM),
                  pl.BlockSpec((SEQ, 2 * LANES), lambda i, hp: (i, hp)),
                  pl.BlockSpec((SEQ, 2 * LANES), lambda i, hp: (i, hp)),
                  pl.BlockSpec((nv, 2 * V_ROWS, V_TILE), lambda i, hp: (i, hp, 0)),
                  pl.BlockSpec((MLA_Q, MLA_Q), lambda i, hp: (0, 0))] + w_specs,
        out_specs=[pl.BlockSpec((SEQ, LANES), lambda i, hp: (i, hp))] + w_specs,
        out_shape=[jax.ShapeDtypeStruct((q.shape[0], MLA_HEADS * MLA_V), BF16)]
                  + [jax.ShapeDtypeStruct(w.shape, BF16) for w in weights],
        scratch_shapes=[pltpu.VMEM((2, nq, chunks, 1, MLA_Q), F32),
                        pltpu.VMEM((2, nq, chunks, V_ROWS, MLA_Q), F32),
                        pltpu.VMEM((2, 2, MLA_K, MLA_Q), F32)],
        compiler_params=pltpu.CompilerParams(
            dimension_semantics=("parallel", "parallel"), vmem_limit_bytes=VMEM_LIMIT),
        name="mla_attention",
    )(_schedule(*_mla_pairs()), q, k, vt, jnp.eye(MLA_Q, dtype=BF16), *weights)


def _select_kernel(q_ref, kc_ref, vct_ref, tabc_ref, ovt_ref, eye_ref, qpad_ref, qaug_ref, ocmp_ref):
    step = pl.program_id(1)
    t = SELECT_Q
    heads = range(NSA_GROUP)
    tiles = range(t // NSA_Q)

    def group(g):
        q_pad = _dot(q_ref[:, g * 2 * LANES:(g + 1) * 2 * LANES], qpad_ref[...]).astype(BF16)
        low = lax.broadcasted_iota(jnp.int32, (t, LANES), 1) < NSA_HEAD_DIM
        zero = jnp.zeros((t, LANES), BF16)
        pairs = [q_ref[:, (2 * g + j) * LANES:(2 * g + j + 1) * LANES] for j in range(NSA_GROUP // 2)]
        q4 = jnp.concatenate([jnp.where(low == (r % 2 == 0), pairs[r // 2], zero) for r in heads], axis=0)

        per_tile = NSA_Q // CMP_STRIDE
        rows = [pl.ds(pl.multiple_of(CMP_TAB_BASE - (step * len(tiles) + u) * per_tile, per_tile), N_CHUNK)
                for u in tiles]
        bias_c = jnp.concatenate([tabc_ref[g * NSA_GROUP + r, rows[u], :] for r in heads for u in tiles], axis=1)
        valid = bias_c > 0.5 * NEG
        sc = _dot_nt(kc_ref[0, g], q4) + bias_c
        mx = jnp.max(sc, axis=0, keepdims=True)
        p = jnp.where(valid, jnp.exp2(sc - mx), 0.0)
        pc = p / jnp.maximum(jnp.sum(p, axis=0, keepdims=True), 1e-30)
        o_cmp = _dot(vct_ref[0, g], pc.astype(BF16)).astype(BF16)
        for u in tiles:
            ocmp_ref[0, g, u] = jnp.concatenate(
                [o_cmp[:, r * t + u * NSA_Q:r * t + (u + 1) * NSA_Q] for r in heads], axis=0)

        psum = sum(pc[:, r * t:(r + 1) * t] for r in heads)
        imp = sum(_dot(ovt_ref[...], piece) for piece in _split(psum, 3))
        jj = lax.broadcasted_iota(jnp.int32, (N_SEL, t), 0)
        blk = (step * t + lax.broadcasted_iota(jnp.int32, (N_SEL, t), 1)) // SEL_BLOCK
        ok = jj <= blk
        forced = ok & ((jj == 0) | (jj == blk) | (jj == blk - 1))
        score = jnp.where(forced, FORCE_SCORE, jnp.where(ok, imp, -jnp.inf))
        rank = jnp.zeros((N_SEL, t), jnp.int32)
        for i in range(N_SEL):
            ci = score[i:i + 1, :]
            beats = (ci > score) | ((ci == score) & (jj > i))
            rank = rank + beats.astype(jnp.int32)
        negsel = jnp.where(ok & (rank < SEL_TOP_N), 0.0, NEG).astype(BF16)
        spread = jnp.concatenate([jnp.zeros((AUG_BLOCK, t), BF16), negsel,
                                  jnp.zeros((LANES - AUG_ONE, t), BF16)], axis=0)
        mask_lanes = _dot_nt(eye_ref[...], spread).astype(BF16)
        lane = lax.broadcasted_iota(jnp.int32, (t, LANES), 1)
        for r in heads:
            far = tabc_ref[g * NSA_GROUP + r, 0:1, :]
            far_hi = far.astype(BF16)
            far_lo = (far - far_hi.astype(F32)).astype(BF16)
            aug = jnp.where(lane == AUG_ONE, far_hi, jnp.where(lane == AUG_ONE + 1, far_lo, mask_lanes))
            q_aug = jnp.where(lane < AUG_BLOCK, q_pad[:, r * LANES:(r + 1) * LANES], aug)
            for u in tiles:
                qaug_ref[0, g, u, r * NSA_Q:(r + 1) * NSA_Q, :] = q_aug[u * NSA_Q:(u + 1) * NSA_Q]

    for g in range(NSA_KV_HEADS):
        group(g)


def _nsa_select(q, kc, vct, tables):
    b = q.shape[0] // SEQ
    t = SELECT_Q
    steps = SEQ // t
    nq = SEQ // NSA_Q
    per = t // NSA_Q
    g_ = NSA_KV_HEADS
    c0 = np.arange(N_CHUNK)[None, :] * CMP_STRIDE
    j0 = np.arange(N_SEL)[:, None] * SEL_BLOCK
    ovt = np.clip(np.minimum(c0 + CMP_BLOCK, j0 + SEL_BLOCK) - np.maximum(c0, j0), 0, None) / CMP_BLOCK
    ovt[:, N_CHUNK - 1:] = 0.0
    return pl.pallas_call(
        _select_kernel,
        grid=(b, steps),
        in_specs=[pl.BlockSpec((t, g_ * 2 * LANES), lambda i, j: (i * steps + j, 0)),
                  pl.BlockSpec((1, g_, N_CHUNK, LANES), lambda i, j: (i, 0, 0, 0)),
                  pl.BlockSpec((1, g_, NSA_HEAD_DIM, N_CHUNK), lambda i, j: (i, 0, 0, 0)),
                  pl.BlockSpec((NSA_HEADS, CMP_TAB_ROWS, LANES), lambda i, j: (0, WIN_ROWS // CMP_TAB_ROWS, 0)),
                  pl.BlockSpec((N_SEL, N_CHUNK), lambda i, j: (0, 0)),
                  pl.BlockSpec((t, t), lambda i, j: (0, 0)),
                  pl.BlockSpec((2 * LANES, NSA_GROUP * LANES), lambda i, j: (0, 0))],
        out_specs=[pl.BlockSpec((1, g_, per, NSA_GROUP * NSA_Q, LANES), lambda i, j: (i, 0, j, 0, 0)),
                   pl.BlockSpec((1, g_, per, NSA_GROUP * NSA_HEAD_DIM, NSA_Q), lambda i, j: (i, 0, j, 0, 0))],
        out_shape=[jax.ShapeDtypeStruct((b, g_, nq, NSA_GROUP * NSA_Q, LANES), BF16),
                   jax.ShapeDtypeStruct((b, g_, nq, NSA_GROUP * NSA_HEAD_DIM, NSA_Q), BF16)],
        compiler_params=pltpu.CompilerParams(
            dimension_semantics=("parallel", "parallel"), vmem_limit_bytes=VMEM_LIMIT),
        name="nsa_select",
    )(q, kc, vct, tables, jnp.asarray(ovt, BF16), jnp.eye(t, dtype=BF16),
      jnp.asarray(_head_padding(NSA_GROUP), BF16))


def _nsa_pairs():
    nq, nk, per = SEQ // NSA_Q, SEQ // NSA_K, NSA_K // NSA_Q
    sel = [(qt, kt) for kt in range(nk) for qt in range(nq) if kt * NSA_K <= qt * NSA_Q]
    far = [(qt, kt) for qt, kt in sel if qt - kt * per >= SEL_FAR_TILE]
    near = [(qt, kt) for qt, kt in sel if qt - kt * per < SEL_FAR_TILE]
    win = [(qt, kt) for kt in range(nk) for qt in range(nq)
           if max(qt * NSA_Q - (WINDOW - 1), 0) // NSA_K <= kt and kt * NSA_K <= qt * NSA_Q]
    return far, near, win


def _attend_kernel(sched_ref, qaug_ref, ksel_ref, kwin_ref, vsel_ref, vwin_ref, ocmp_ref, gate_ref,
                   tabw_ref, eye_ref, o_ref, m_ref, acc_ref, s_ref, p_ref, alpha_ref):
    t = NSA_Q
    heads = range(NSA_GROUP)
    far, near, win = _nsa_pairs()
    _flash_init(m_ref, acc_ref)

    def scores(qt, kt, slot, k_ref, add_bias):
        s = _dot_nt(k_ref[pl.ds(pl.multiple_of(kt * NSA_K, NSA_K), NSA_K), :], qaug_ref[0, 0, qt])
        if add_bias:
            off = qt - kt * (NSA_K // NSA_Q)
            bias_rows = pl.ds(pl.multiple_of(off * NSA_K, NSA_K), NSA_K)
            s = s + jnp.concatenate([tabw_ref[r, bias_rows, :] for r in heads], axis=1)
        s_ref[slot] = s

    def softmax(qt, kt, slot, branch):
        _flash_softmax(s_ref[slot], m_ref.at[branch, qt], p_ref.at[slot], alpha_ref.at[slot], chunk=NSA_K)

    def accumulate(qt, kt, slot, v_ref, branch):
        _flash_accumulate(v_ref[kt], p_ref.at[slot], alpha_ref.at[slot], acc_ref.at[branch, qt], chunk=NSA_K)

    for row, pairs, k_ref, v_ref, branch, add_bias in ((0, far, ksel_ref, vsel_ref, 0, False),
                                                       (2, near, ksel_ref, vsel_ref, 0, True),
                                                       (4, win, kwin_ref, vwin_ref, 1, True)):
        _pipelined_pairs(sched_ref, row, len(pairs),
                         lambda qt, kt, slot, k_ref=k_ref, add_bias=add_bias: scores(qt, kt, slot, k_ref, add_bias),
                         lambda qt, kt, slot, branch=branch: softmax(qt, kt, slot, branch),
                         lambda qt, kt, slot, v_ref=v_ref, branch=branch: accumulate(qt, kt, slot, v_ref, branch),
                         unroll=NSA_UNROLL)

    d = NSA_HEAD_DIM
    group = pl.program_id(1)
    for qt in range(SEQ // t):
        o_cmp = ocmp_ref[0, 0, qt].astype(F32)
        o_sel, o_win = [_flash_finish(m_ref.at[br, qt], acc_ref.at[br, qt]) for br in range(2)]
        mixed = []
        for r in heads:
            gate = [gate_ref[qt, pl.ds(br * NSA_HEADS + group * NSA_GROUP + r, 1), :]
                    for br in range(N_BRANCH)]
            mixed.append(gate[0] * o_cmp[r * d:(r + 1) * d] + gate[1] * o_sel[:, r * t:(r + 1) * t]
                         + gate[2] * o_win[:, r * t:(r + 1) * t])
        o_ref[qt * t:(qt + 1) * t, :] = _dot_nt(eye_ref[...], jnp.concatenate(mixed, axis=0).astype(BF16)).astype(BF16)


def _nsa_attend(qaug, kaug, vt, ocmp, gates, tables):
    b = kaug.shape[0] // SEQ
    t = NSA_Q
    nq = SEQ // t
    nv = SEQ // V_TILE
    g_ = NSA_KV_HEADS
    k_spec = lambda off: pl.BlockSpec((SEQ, LANES), lambda i, g: (i, off + g))
    v_spec = lambda off: pl.BlockSpec((nv, V_ROWS, V_TILE), lambda i, g: (i, off + g, 0))
    return pl.pallas_call(
        _attend_kernel,
        grid=(b, g_),
        in_specs=[pl.BlockSpec(memory_space=pltpu.SMEM),
                  pl.BlockSpec((1, 1, nq, NSA_GROUP * t, LANES), lambda i, g: (i, g, 0, 0, 0)),
                  k_spec(0), k_spec(2), v_spec(0), v_spec(2),
                  pl.BlockSpec((1, 1, nq, NSA_GROUP * NSA_HEAD_DIM, t), lambda i, g: (i, g, 0, 0, 0)),
                  pl.BlockSpec((nq, LANES, t), lambda i, g: (i, 0, 0)),
                  pl.BlockSpec((NSA_GROUP, WIN_ROWS, LANES), lambda i, g: (g, 0, 0)),
                  pl.BlockSpec((t, t), lambda i, g: (0, 0))],
        out_specs=pl.BlockSpec((SEQ, 2 * LANES), lambda i, g: (i, g)),
        out_shape=jax.ShapeDtypeStruct((kaug.shape[0], NSA_HEADS * NSA_HEAD_DIM), BF16),
        scratch_shapes=[pltpu.VMEM((2, nq, 1, 1, NSA_GROUP * t), F32),
                        pltpu.VMEM((2, nq, 1, V_ROWS, NSA_GROUP * t), F32),
                        pltpu.VMEM((2, NSA_K, NSA_GROUP * t), F32),
                        pltpu.VMEM((2, NSA_K, NSA_GROUP * t), BF16),
                        pltpu.VMEM((2, 1, 1, NSA_GROUP * t), F32)],
        compiler_params=pltpu.CompilerParams(
            dimension_semantics=("parallel", "parallel"), vmem_limit_bytes=VMEM_LIMIT),
        name="nsa_attend",
    )(_schedule(*_nsa_pairs()), qaug, kaug, kaug, vt, vt, ocmp, gates, tables, jnp.eye(t, dtype=BF16))


def _tail_kernel(x_ref, oa_ref, ob_ref, ga_ref, gb_ref, wo_ref, ln2_ref, wup_ref, wdn_ref,
                 out_ref):
    width = MLA_HEADS * MLA_V
    a = (_rms(oa_ref[...].astype(F32), width) * ga_ref[...]).astype(BF16)
    b = (_rms(ob_ref[...].astype(F32), width) * gb_ref[...]).astype(BF16)
    x1 = x_ref[...] + _dot(a, wo_ref[:width]) + _dot(b, wo_ref[width:])
    h2 = (_rms(x1, D_MODEL) * ln2_ref[...]).astype(BF16)
    u = jnp.maximum(_dot(h2, wup_ref[...]), 0.0)
    out_ref[...] = x1 + _dot((u * u).astype(BF16), wdn_ref[...])


def _tail(x2d, o_a, o_b, gn_a, gn_b, w_o, ln2_g, w_up, w_down):
    t = x2d.shape[0]
    rows = TAIL_ROWS
    width = MLA_HEADS * MLA_V
    once = lambda a: pl.BlockSpec(a.shape, lambda i: (0,) * a.ndim)
    assert w_o.dtype == w_up.dtype == w_down.dtype == BF16
    consts = [gn_a[None, :], gn_b[None, :], w_o, ln2_g[None, :], w_up, w_down]
    row_spec = lambda w: pl.BlockSpec((rows, w), lambda i: (i, 0))
    return pl.pallas_call(
        _tail_kernel,
        grid=(t // rows,),
        in_specs=[row_spec(D_MODEL), row_spec(width), row_spec(width)] + [once(a) for a in consts],
        out_specs=row_spec(D_MODEL),
        out_shape=jax.ShapeDtypeStruct((t, D_MODEL), F32),
        compiler_params=pltpu.CompilerParams(dimension_semantics=("parallel",),
                                             vmem_limit_bytes=VMEM_LIMIT),
        name="tail",
    )(x2d, o_a, o_b, *consts)


def kernel(x, ln1_g, w_in, mla_cq_norm_g, mla_ckv_norm_g, mla_w_uq, mla_w_ukv, mla_q_gain, mla_k_gain,
           nsa_q_gain, nsa_k_gain, nsa_cmp_pe, nsa_cmp_w1, nsa_cmp_w2, rel_bias, grp_norm_mla,
           grp_norm_nsa, w_o, ln2_g, w_up, w_down):
    b, s, d = x.shape
    assert (s, d) == (SEQ, D_MODEL) and ln1_g.shape[0] == 1
    x2d = x.reshape(b * s, d)
    tables = _bias_tables(rel_bias)
    q_mla, k_mla, vt_mla, q_nsa, kdup, vt_nsa, kck, kcv, gates = _projections(
        x2d, ln1_g[0], w_in[0], mla_cq_norm_g[0], mla_ckv_norm_g[0], mla_w_uq[0], mla_w_ukv[0],
        mla_q_gain[0], mla_k_gain[0], nsa_q_gain[0], nsa_k_gain[0])
    kc, vct = _compress(kck, kcv, nsa_cmp_pe[0], nsa_cmp_w1[0], nsa_cmp_w2[0], nsa_k_gain[0, 0])
    o_a, w_o16, w_up16, w_down16 = _mla_attention(q_mla, k_mla, vt_mla, w_o[0], w_up[0], w_down[0])
    q_aug, ocmp = _nsa_select(q_nsa, kc, vct, tables)
    o_b = _nsa_attend(q_aug, kdup, vt_nsa, ocmp, gates, tables)
    out = _tail(x2d, o_a, o_b, grp_norm_mla[0], grp_norm_nsa[0], w_o16, ln2_g[0], w_up16, w_down16)
    return out.reshape(b, s, d)
```

```python
import math

import numpy as np
import jax
import jax.numpy as jnp
from jax import lax
from jax.experimental import pallas as pl
from jax.experimental.pallas import tpu as pltpu

F32 = jnp.float32
BF16 = jnp.bfloat16

D_MODEL = 1024
SEQ = 2048
MLA_HEADS = 8
MLA_NOPE = 64
MLA_ROPE = 32
MLA_V = 64
MLA_QK = MLA_NOPE + MLA_ROPE
MLA_Q_RANK = 384
MLA_KV_RANK = 256
ROPE_THETA = 10000.0
NSA_HEADS = 8
NSA_KV_HEADS = 2
NSA_GROUP = NSA_HEADS // NSA_KV_HEADS
NSA_HEAD_DIM = 64
N_BRANCH = 3
CMP_BLOCK = 32
CMP_STRIDE = 16
CMP_HIDDEN = 128
SEL_BLOCK = 64
SEL_TOP_N = 16
WINDOW = 512
FORCE_SCORE = 1e4
REL_BUCKETS = 32
REL_MAX_DIST = 128
D_FF = 4 * D_MODEL
EPS = 1e-6

LANES = 128
SUBLANES = 8
BF16_ROWS = 16
NEG = -1e30
N_CHUNK = SEQ // CMP_STRIDE
N_SEL = SEQ // SEL_BLOCK
NSA_Q = 128
NSA_K = 256
SELECT_Q = 512
N_BIAS_TILES = (WINDOW + NSA_K - NSA_Q) // NSA_Q + 1
SEL_FAR_TILE = (REL_MAX_DIST + NSA_K) // NSA_Q
WIN_ROWS = N_BIAS_TILES * NSA_K
AUG_BLOCK = NSA_HEAD_DIM
AUG_ONE = AUG_BLOCK + N_SEL
CMP_TAB_BASE = (SEQ // NSA_Q - 1) * (NSA_Q // CMP_STRIDE)
CMP_TAB_ROWS = 256
MLA_Q = 256
MLA_K = 512
V_TILE = 256
PROJ_ROWS = 256
PACK_ROWS = 256
TAIL_ROWS = 512
KEY_CHUNK = 256
PAIR_UNROLL = 6
NSA_UNROLL = 12
V_ROWS = 80
LOG2E = math.log2(math.e)
VMEM_LIMIT = 56 * 1024 * 1024

assert PROJ_ROWS % V_TILE == 0 and MLA_K % V_TILE == 0 and NSA_K == V_TILE


def _dot(a, b):
    return jnp.dot(a, b, preferred_element_type=F32)


def _dot_nt(a, b):
    return lax.dot_general(a, b, (((1,), (1,)), ((), ())), preferred_element_type=F32)


def _split(a, terms):
    pieces = []
    rem = a
    for _ in range(terms):
        piece = rem.astype(BF16)
        pieces.append(piece)
        rem = rem - piece.astype(F32)
    return pieces


def _split_dot(a, b, terms=2):
    return sum(_dot(p, b) for p in _split(a, terms))


def _rms(x, width):
    return x * lax.rsqrt(jnp.sum(x * x, axis=-1, keepdims=True) * (1.0 / width) + EPS)


def _t5_bucket_np(dist):
    n = np.maximum(dist, 0)
    max_exact = REL_BUCKETS // 2
    large = max_exact + (np.log(np.maximum(n, 1).astype(np.float32) / max_exact)
                         / math.log(REL_MAX_DIST / max_exact)
                         * (REL_BUCKETS - max_exact)).astype(np.int32)
    large = np.minimum(large, REL_BUCKETS - 1)
    return np.where(n < max_exact, n, large).astype(np.int32)


def _bias_index_table():
    i = np.arange(NSA_Q)[None, :]
    parts = []
    j = np.arange(NSA_K)[:, None]
    for off in range(N_BIAS_TILES):
        d = off * NSA_Q + i - j
        parts.append(np.where((d >= 0) & (d < WINDOW), _t5_bucket_np(d), -1))
    r = np.arange(CMP_TAB_ROWS)[:, None]
    dist_c = (CMP_TAB_BASE - r) * CMP_STRIDE + i - (CMP_BLOCK - 1)
    parts.append(np.where(dist_c >= 0, _t5_bucket_np(dist_c), -1))
    return np.concatenate(parts, axis=0).astype(np.int32)


def _bias_table_kernel(rb_ref, idx_ref, out_ref):
    idx = idx_ref[...]
    bucket = jnp.maximum(idx, 0)
    is_window = pl.program_id(0) < WIN_ROWS // idx.shape[0]
    for h in range(NSA_HEADS):
        row = rb_ref[h:h + 1, :]
        far = jnp.where(is_window, row[:, REL_BUCKETS - 1:REL_BUCKETS], 0.0)
        values = jnp.broadcast_to((row - far) * LOG2E, idx.shape)
        picked = jnp.take_along_axis(values, bucket, axis=1, mode="promise_in_bounds")
        out_ref[h] = jnp.where(idx >= 0, picked, NEG)


def _bias_tables(rel_bias):
    idx = jnp.asarray(_bias_index_table())
    rows = idx.shape[0]
    blk = CMP_TAB_ROWS
    return pl.pallas_call(
        _bias_table_kernel,
        grid=(rows // blk,),
        in_specs=[pl.BlockSpec((NSA_HEADS, LANES), lambda r: (0, 0)),
                  pl.BlockSpec((blk, LANES), lambda r: (r, 0))],
        out_specs=pl.BlockSpec((NSA_HEADS, blk, LANES), lambda r: (0, r, 0)),
        out_shape=jax.ShapeDtypeStruct((NSA_HEADS, rows, LANES), F32),
        name="bias_tables",
    )(jnp.pad(rel_bias.T, ((0, 0), (0, LANES - REL_BUCKETS))), idx)


ROPE_HALF = MLA_ROPE // 2
MISC_KR = 64


def _rope_tables(scale):
    inv = 1.0 / (ROPE_THETA ** (np.arange(ROPE_HALF, dtype=np.float64) / ROPE_HALF))
    ang = np.arange(SEQ, dtype=np.float64)[:, None] * inv[None, :]
    cos, sin = np.cos(ang), np.sin(ang)
    ones = np.ones((SEQ, MLA_NOPE))
    z32 = np.zeros((SEQ, LANES - MLA_QK))
    z64 = np.zeros((SEQ, MLA_NOPE))
    c = np.concatenate([ones, cos, cos, z32], axis=1) * scale
    s = np.concatenate([z64, -sin, sin, z32], axis=1) * scale
    return jnp.asarray(c, F32), jnp.asarray(s, F32)


def _pair_norm(x, gain_ref):
    low = lax.broadcasted_iota(jnp.int32, (x.shape[0], LANES), 1) < NSA_HEAD_DIM
    outs = []
    for j in range(x.shape[1] // LANES):
        cols = slice(j * LANES, (j + 1) * LANES)
        sl = x[:, cols]
        sq = sl * sl
        tot = jnp.sum(sq, axis=-1, keepdims=True)
        lo = jnp.sum(jnp.where(low, sq, 0.0), axis=-1, keepdims=True)
        rs = jnp.where(low, lax.rsqrt(lo * (1.0 / NSA_HEAD_DIM) + EPS),
                       lax.rsqrt((tot - lo) * (1.0 / NSA_HEAD_DIM) + EPS))
        outs.append(sl * rs * gain_ref[:, cols])
    return jnp.concatenate(outs, axis=1)


def _mla_query_heads(x, gain_ref, c, s):
    in_head = lax.broadcasted_iota(jnp.int32, (x.shape[0], LANES), 1) < MLA_QK
    outs = []
    for h in range(MLA_HEADS):
        cols = slice(h * LANES, (h + 1) * LANES)
        sl = x[:, cols]
        ss = jnp.sum(jnp.where(in_head, sl * sl, 0.0), axis=-1, keepdims=True)
        xn = sl * lax.rsqrt(ss * (1.0 / MLA_QK) + EPS) * gain_ref[:, cols]
        outs.append(xn * c + pltpu.roll(xn, LANES - ROPE_HALF, 1) * s)
    return jnp.concatenate(outs, axis=1)


def _mla_key_heads(x, k_rope, k_swap, c, s):
    in_head = lax.broadcasted_iota(jnp.int32, (x.shape[0], LANES), 1) < MLA_QK
    turned = k_swap * s
    outs = []
    for h in range(MLA_HEADS):
        sl = x[:, h * LANES:(h + 1) * LANES] + k_rope
        ss = jnp.sum(jnp.where(in_head, sl * sl, 0.0), axis=-1, keepdims=True)
        outs.append(lax.rsqrt(ss * (1.0 / MLA_QK) + EPS) * (sl * c + turned))
    return jnp.concatenate(outs, axis=1)


COL_KV = MLA_Q_RANK
COL_MISC = COL_KV + MLA_KV_RANK
COL_SWAP = COL_MISC + LANES
COL_NQ = COL_SWAP + LANES
COL_NK = COL_NQ + NSA_HEADS * NSA_HEAD_DIM
COL_NV = COL_NK + 4 * NSA_HEAD_DIM
COL_CMP = COL_NV + 4 * NSA_HEAD_DIM
COL_END = COL_CMP + 4 * NSA_HEAD_DIM


def _proj_kernel(x_ref, ln1_ref, wp_ref, cqg_ref, ckvg_ref, wuq_ref, wukk_ref, wukvt_ref,
                 qg_ref, kown_ref, kpartner_ref, qc_ref, qs_ref, kc_ref, ks_ref,
                 nq_gain_ref, nk_gain_ref, kpad_ref, kaug_ref, vpick_ref, ones_mla_ref, ones_nsa_ref, eye_ref,
                 qmla_ref, kmla_ref, vmla_ref, qnsa_ref, kdup_ref, vnsa_ref, kck_ref, kcv_ref, gate_ref):
    for sub in range(PROJ_ROWS // V_TILE):
        rows = slice(sub * V_TILE, (sub + 1) * V_TILE)
        x = x_ref[rows, :]
        h = (_rms(x, D_MODEL) * ln1_ref[...]).astype(BF16)
        part = lambda a, b: _dot(h, wp_ref[:, a:b])

        cq = (_rms(part(0, COL_KV), MLA_Q_RANK) * cqg_ref[...]).astype(BF16)
        qmla_ref[rows, :] = _mla_query_heads(_dot(cq, wuq_ref[...]), qg_ref,
                                             qc_ref[rows, :], qs_ref[rows, :]).astype(BF16)

        kv_misc = part(COL_KV, COL_NQ)
        ckv = (_rms(kv_misc[:, :MLA_KV_RANK], MLA_KV_RANK) * ckvg_ref[...]).astype(BF16)
        misc = kv_misc[:, COL_MISC - COL_KV:COL_SWAP - COL_KV]
        lane = lax.broadcasted_iota(jnp.int32, misc.shape, 1)
        k_rope = jnp.where((lane >= MISC_KR) & (lane < MISC_KR + MLA_ROPE), misc, 0.0)
        kmla_ref[rows, :] = _mla_key_heads(_dot(ckv, wukk_ref[...]), k_rope, kv_misc[:, COL_SWAP - COL_KV:],
                                           kc_ref[rows, :] * kown_ref[...],
                                           ks_ref[rows, :] * kpartner_ref[...]).astype(BF16)
        vmla_ref[sub] = (_dot_nt(wukvt_ref[...], ckv) + ones_mla_ref[...]).astype(BF16)

        qnsa_ref[rows, :] = _pair_norm(part(COL_NQ, COL_NK), nq_gain_ref).astype(BF16)

        kn = _pair_norm(part(COL_NK, COL_NV), nk_gain_ref)
        kdup_ref[rows, :] = (_dot(kn.astype(BF16), kpad_ref[...]) + kaug_ref[rows, :]).astype(BF16)
        vnsa_ref[sub] = (_dot_nt(vpick_ref[...], part(COL_NV, COL_CMP).astype(BF16)) + ones_nsa_ref[...]).astype(BF16)
        cmp_kv = part(COL_CMP, COL_END)
        kck_ref[rows, :] = cmp_kv[:, :LANES]
        kcv_ref[rows, :] = cmp_kv[:, LANES:]
        gate = 1.0 / (1.0 + jnp.exp(-misc))
        gate_t = sum(_dot_nt(eye_ref[...], piece) for piece in _split(gate, 2))
        for u in range(V_TILE // NSA_Q):
            gate_ref[sub * (V_TILE // NSA_Q) + u] = gate_t[:, u * NSA_Q:(u + 1) * NSA_Q]


def _head_padding(heads):
    d = NSA_HEAD_DIM
    m = np.zeros((heads * d, heads * LANES), np.float32)
    for hh in range(heads):
        m[hh * d + np.arange(d), hh * LANES + np.arange(d)] = 1.0
    return m


def _packed_segments():
    kv0 = 1184
    seg = lambda a, b: (a, b - a)
    kvb = lambda br, kv, g: (kv0 + ((br * 2 + kv) * 2 + g) * 64, 64)
    zeros = lambda n: (None, n)
    kr = COL_MISC
    gate_cols = N_BRANCH * NSA_HEADS
    pieces = [
        seg(0, COL_MISC),
        seg(1952, 1952 + gate_cols), zeros(MISC_KR - gate_cols), seg(kr, kr + MLA_ROPE),
        zeros(LANES - MISC_KR - MLA_ROPE),
        zeros(MISC_KR), seg(kr + ROPE_HALF, kr + MLA_ROPE), seg(kr, kr + ROPE_HALF), zeros(LANES - MISC_KR - MLA_ROPE),
        seg(672, 1184),
        kvb(1, 0, 0), kvb(1, 0, 1), kvb(2, 0, 0), kvb(2, 0, 1),
        kvb(1, 1, 0), kvb(1, 1, 1), kvb(2, 1, 0), kvb(2, 1, 1),
        kvb(0, 0, 0), kvb(0, 0, 1), kvb(0, 1, 0), kvb(0, 1, 1),
    ]
    assert sum(n for _, n in pieces) == COL_END
    return pieces


def _pack_kernel(wt_ref, out_ref, packed_ref):
    dst = 0
    for src, n in _packed_segments():
        assert n % SUBLANES == 0 and dst % SUBLANES == 0 and (src is None or src % SUBLANES == 0)
        if src is None:
            packed_ref[dst:dst + n, :] = jnp.zeros((n, packed_ref.shape[1]), F32)
        else:
            packed_ref[dst:dst + n, :] = wt_ref[src:src + n, :]
        dst += n
    for g in range(COL_END // LANES):
        cols = slice(g * LANES, (g + 1) * LANES)
        out_ref[:, cols] = packed_ref[cols, :].T.astype(BF16)


def _pack_input_projection(w_in):
    rows = PACK_ROWS
    return pl.pallas_call(
        _pack_kernel,
        grid=(D_MODEL // rows,),
        in_specs=[pl.BlockSpec((w_in.shape[1], rows), lambda i: (0, i))],
        out_specs=pl.BlockSpec((rows, COL_END), lambda i: (i, 0)),
        out_shape=jax.ShapeDtypeStruct((D_MODEL, COL_END), BF16),
        scratch_shapes=[pltpu.VMEM((COL_END, rows), F32)],
        compiler_params=pltpu.CompilerParams(dimension_semantics=("parallel",),
                                             vmem_limit_bytes=VMEM_LIMIT),
        name="pack_input_projection",
    )(w_in.T)


def _projections(x2d, ln1_g, w_in, cq_g, ckv_g, w_uq, w_ukv, q_gain, k_gain, nq_gain, nk_gain):
    t = x2d.shape[0]
    wp = _pack_input_projection(w_in)
    wq3 = w_uq.reshape(MLA_Q_RANK, MLA_HEADS, MLA_QK).astype(BF16)
    wuq = jnp.concatenate([wq3, wq3[:, :, MLA_NOPE:MLA_NOPE + ROPE_HALF],
                           jnp.zeros((MLA_Q_RANK, MLA_HEADS, ROPE_HALF), BF16)], axis=2)
    wuq = wuq.reshape(MLA_Q_RANK, MLA_HEADS * LANES)
    wkv = w_ukv.reshape(MLA_KV_RANK, MLA_HEADS, MLA_NOPE + MLA_V)
    wukk = jnp.pad(wkv[:, :, :MLA_NOPE], ((0, 0), (0, 0), (0, LANES - MLA_NOPE)))
    wukk = wukk.reshape(MLA_KV_RANK, MLA_HEADS * LANES).astype(BF16)
    wukvt = jnp.pad(wkv[:, :, MLA_NOPE:], ((0, 0), (0, 0), (0, V_ROWS - MLA_V)))
    wukvt = wukvt.reshape(MLA_KV_RANK, MLA_HEADS * V_ROWS).T.astype(BF16)
    q_slab_gain = jnp.tile(jnp.concatenate(
        [q_gain, q_gain[MLA_NOPE:MLA_NOPE + ROPE_HALF], jnp.zeros((ROPE_HALF,), F32)]), MLA_HEADS)[None, :]
    qc, qs = _rope_tables(MLA_QK ** -0.5 * LOG2E)
    kc, ks = _rope_tables(1.0)
    k_own = jnp.pad(k_gain, (0, LANES - MLA_QK))[None, :]
    k_partner = jnp.concatenate([k_gain[:MLA_NOPE], k_gain[MLA_NOPE + ROPE_HALF:],
                                 k_gain[MLA_NOPE:MLA_NOPE + ROPE_HALF], jnp.zeros((LANES - MLA_QK,), F32)])[None, :]
    kpad = _head_padding(4)
    kaug = np.zeros((SEQ, 4 * LANES), np.float32)
    for s in range(4):
        kaug[:, s * LANES + AUG_ONE:s * LANES + AUG_ONE + 2] = 1.0
        if s < NSA_KV_HEADS:
            kaug[np.arange(SEQ), s * LANES + AUG_BLOCK + np.arange(SEQ) // SEL_BLOCK] = 1.0
    nqg = (jnp.tile(nq_gain, NSA_HEADS) * (NSA_HEAD_DIM ** -0.5 * LOG2E))[None, :]
    nkg = jnp.concatenate([nk_gain[1], nk_gain[1], nk_gain[2], nk_gain[2]])[None, :]
    vpick = np.zeros((4 * V_ROWS, 256), np.float32)
    for s in range(4):
        vpick[s * V_ROWS + np.arange(64), s * 64 + np.arange(64)] = 1.0

    def ones_col(heads):
        col = np.zeros((heads * V_ROWS, 1), np.float32)
        col[np.arange(heads) * V_ROWS + MLA_V] = 1.0
        return col

    rows = PROJ_ROWS
    n_pos = SEQ // rows
    full = lambda a: pl.BlockSpec(a.shape, lambda i: (0,) * a.ndim)
    tab = pl.BlockSpec((rows, LANES), lambda i: (i % n_pos, 0))
    consts = [ln1_g[None, :], wp, cq_g[None, :], ckv_g[None, :], wuq, wukk, wukvt, q_slab_gain, k_own, k_partner]
    tabs = [qc, qs, kc, ks]
    mats = [nqg, nkg, jnp.asarray(kpad, BF16),
            jnp.asarray(vpick, BF16), jnp.asarray(ones_col(MLA_HEADS)), jnp.asarray(ones_col(4)),
            jnp.eye(LANES, dtype=BF16)]
    row_spec = lambda w: pl.BlockSpec((rows, w), lambda i: (i, 0))
    vt_spec = lambda w, tile: pl.BlockSpec((rows // tile, w, tile), lambda i: (i, 0, 0))
    sds = jax.ShapeDtypeStruct
    return pl.pallas_call(
        _proj_kernel,
        grid=(t // rows,),
        in_specs=([row_spec(D_MODEL)] + [full(a) for a in consts] + [tab] * 4 + [full(a) for a in mats[:3]]
                  + [pl.BlockSpec((rows, 4 * LANES), lambda i: (i % n_pos, 0))] + [full(a) for a in mats[3:]]),
        out_specs=[row_spec(1024), row_spec(1024), vt_spec(MLA_HEADS * V_ROWS, V_TILE), row_spec(512),
                   row_spec(512), vt_spec(4 * V_ROWS, V_TILE), row_spec(128), row_spec(128),
                   vt_spec(LANES, NSA_Q)],
        out_shape=[sds((t, 1024), BF16), sds((t, 1024), BF16), sds((t // V_TILE, MLA_HEADS * V_ROWS, V_TILE), BF16),
                   sds((t, 512), BF16), sds((t, 512), BF16), sds((t // V_TILE, 4 * V_ROWS, V_TILE), BF16),
                   sds((t, 128), F32), sds((t, 128), F32), sds((t // NSA_Q, LANES, NSA_Q), F32)],
        compiler_params=pltpu.CompilerParams(dimension_semantics=("parallel",),
                                             vmem_limit_bytes=VMEM_LIMIT),
        name="projections",
    )(x2d, *consts, *tabs, *mats[:3], jnp.asarray(kaug, BF16), *mats[3:])


def _compress_kernel(xk_ref, xv_ref, pe_ref, w1_ref, w2_ref, kg_ref, dupk_ref, pickv_ref, kc_ref, vct_ref):
    def mlp(x_ref, which):
        top = bot = None
        for t in range(CMP_STRIDE):
            xt = x_ref[pl.ds(t, N_CHUNK, stride=CMP_STRIDE), :]
            a = _dot((xt + pe_ref[which, t]).astype(BF16), w1_ref[which, t])
            b = _dot((xt + pe_ref[which, CMP_STRIDE + t]).astype(BF16), w1_ref[which, CMP_STRIDE + t])
            top = a if top is None else top + a
            bot = b if bot is None else bot + b
        hid = top + pltpu.roll(bot, N_CHUNK - 1, 0)
        act = (hid * (1.0 / (1.0 + jnp.exp(-hid)))).astype(BF16)
        return _dot(act, w2_ref[which])

    k = _pair_norm(mlp(xk_ref, 0), kg_ref).astype(BF16)
    v = mlp(xv_ref, 1).astype(BF16)
    for g in range(NSA_KV_HEADS):
        kc_ref[0, g] = _dot(k, dupk_ref[g]).astype(BF16)
        vct_ref[0, g] = _dot_nt(pickv_ref[g], v).astype(BF16)


def _compress(kck, kcv, cmp_pe, cmp_w1, cmp_w2, k_gain0):
    b = kck.shape[0] // SEQ
    g_ = NSA_KV_HEADS
    d = NSA_HEAD_DIM
    pe = jnp.tile(cmp_pe, (1, 1, g_))[:, :, None, :]
    w1 = cmp_w1.reshape(2, CMP_BLOCK, d, CMP_HIDDEN)
    z1 = jnp.zeros_like(w1)
    w1 = jnp.concatenate([jnp.concatenate([w1, z1], axis=3), jnp.concatenate([z1, w1], axis=3)], axis=2)
    z2 = jnp.zeros_like(cmp_w2)
    w2 = jnp.concatenate([jnp.concatenate([cmp_w2, z2], axis=2), jnp.concatenate([z2, cmp_w2], axis=2)], axis=1)
    dupk = np.zeros((g_, LANES, LANES), np.float32)
    pickv = np.zeros((g_, d, LANES), np.float32)
    for g in range(g_):
        dupk[g, g * d + np.arange(d), np.arange(d)] = 1.0
        dupk[g, g * d + np.arange(d), d + np.arange(d)] = 1.0
        pickv[g, np.arange(d), g * d + np.arange(d)] = 1.0
    full = lambda a: pl.BlockSpec(a.shape, lambda i: (0,) * a.ndim)
    consts = [pe, w1.astype(BF16), w2.astype(BF16), jnp.tile(k_gain0, g_)[None, :],
              jnp.asarray(dupk, BF16), jnp.asarray(pickv, BF16)]
    x_spec = pl.BlockSpec((SEQ, LANES), lambda i: (i, 0))
    return pl.pallas_call(
        _compress_kernel,
        grid=(b,),
        in_specs=[x_spec, x_spec] + [full(a) for a in consts],
        out_specs=[pl.BlockSpec((1, g_, N_CHUNK, LANES), lambda i: (i, 0, 0, 0)),
                   pl.BlockSpec((1, g_, d, N_CHUNK), lambda i: (i, 0, 0, 0))],
        out_shape=[jax.ShapeDtypeStruct((b, g_, N_CHUNK, LANES), BF16),
                   jax.ShapeDtypeStruct((b, g_, d, N_CHUNK), BF16)],
        compiler_params=pltpu.CompilerParams(dimension_semantics=("parallel",)),
        name="compression",
    )(kck, kcv, *consts)


def _flash_init(m_ref, acc_ref):
    m_ref[...] = jnp.full(m_ref.shape, NEG, F32)
    acc_ref[...] = jnp.zeros(acc_ref.shape, F32)


def _flash_step(s, v, m_ref, acc_ref, chunk=KEY_CHUNK):
    for c in range(s.shape[0] // chunk):
        sc = s[c * chunk:(c + 1) * chunk]
        m_old = m_ref[c]
        m_new = jnp.maximum(m_old, jnp.max(sc, axis=0, keepdims=True))
        p = jnp.exp2(sc - m_new).astype(BF16)
        acc_ref[c] = jnp.exp2(m_old - m_new) * acc_ref[c] + _dot(v[:, c * chunk:(c + 1) * chunk], p)
        m_ref[c] = m_new


def _flash_softmax(s, m_ref, p_ref, alpha_ref, chunk=KEY_CHUNK):
    for c in range(s.shape[0] // chunk):
        sc = s[c * chunk:(c + 1) * chunk]
        m_old = m_ref[c]
        m_new = jnp.maximum(m_old, jnp.max(sc, axis=0, keepdims=True))
        alpha_ref[c] = jnp.exp2(m_old - m_new)
        p_ref[c * chunk:(c + 1) * chunk, :] = jnp.exp2(sc - m_new).astype(BF16)
        m_ref[c] = m_new


def _flash_accumulate(v, p_ref, alpha_ref, acc_ref, chunk=KEY_CHUNK):
    for c in range(p_ref.shape[0] // chunk):
        rows = slice(c * chunk, (c + 1) * chunk)
        acc_ref[c] = alpha_ref[c] * acc_ref[c] + _dot(v[:, rows], p_ref[rows, :])


def _flash_finish(m_ref, acc_ref):
    n = m_ref.shape[0]
    m = m_ref[0]
    for c in range(1, n):
        m = jnp.maximum(m, m_ref[c])
    acc = sum(jnp.exp2(m_ref[c] - m) * acc_ref[c] for c in range(n))
    return acc[:MLA_V] / acc[MLA_V:MLA_V + 1]


def _pipelined_pairs(sched_ref, row, n, scores, softmax, accumulate, unroll=PAIR_UNROLL):
    assert n >= 2 and unroll % 2 == 0

    def pair(i):
        i = jnp.minimum(i, n - 1)
        return sched_ref[row, i], sched_ref[row + 1, i]

    def steps(first, count):
        for u in range(count):
            i, slot = first + u, u % 2
            if accumulate is None:
                scores(*pair(i + 1), 1 - slot)
                softmax(*pair(i), slot)
            else:
                scores(*pair(i + 2), slot)
                softmax(*pair(i + 1), 1 - slot)
                accumulate(*pair(i), slot)

    scores(*pair(0), 0)
    if accumulate is not None:
        scores(*pair(1), 1)
        softmax(*pair(0), 0)
    loops = n // unroll
    lax.fori_loop(0, loops, lambda j, carry: (steps(j * unroll, unroll), carry)[1], 0)
    steps(loops * unroll, n - loops * unroll)


def _mla_pairs():
    nq, nk = SEQ // MLA_Q, SEQ // MLA_K
    full = [(qt, kt) for kt in range(nk) for qt in range(nq) if (kt + 1) * MLA_K <= qt * MLA_Q]
    diag = [(qt, (qt * MLA_Q) // MLA_K) for qt in range(nq)]
    return full, diag


def _mla_kernel(sched_ref, q_ref, k_ref, vt_ref, eye_ref, wo_ref, wup_ref, wdn_ref,
                o_ref, wo16_ref, wup16_ref, wdn16_ref, m_ref, acc_ref, s_ref):
    for w_ref, w16_ref in ((wo_ref, wo16_ref), (wup_ref, wup16_ref), (wdn_ref, wdn16_ref)):
        w16_ref[...] = w_ref[...].astype(BF16)
    sub = MLA_K // V_TILE
    full, diag = _mla_pairs()
    rel = (lax.broadcasted_iota(jnp.int32, (MLA_K, MLA_Q), 1)
           - lax.broadcasted_iota(jnp.int32, (MLA_K, MLA_Q), 0))
    _flash_init(m_ref, acc_ref)

    def scores(qt, kt, slot, masked):
        qrows = pl.ds(pl.multiple_of(qt * MLA_Q, MLA_Q), MLA_Q)
        krows = pl.ds(pl.multiple_of(kt * MLA_K, MLA_K), MLA_K)
        for hh in range(2):
            cols = slice(hh * LANES, (hh + 1) * LANES)
            s = _dot_nt(k_ref[krows, cols], q_ref[qrows, cols])
            if masked:
                s = jnp.where(rel >= kt * MLA_K - qt * MLA_Q, s, NEG)
            s_ref[slot, hh] = s

    def update(qt, kt, slot):
        for hh in range(2):
            v = jnp.concatenate([vt_ref[kt * sub + i, hh * V_ROWS:(hh + 1) * V_ROWS, :] for i in range(sub)],
                                axis=1)
            _flash_step(s_ref[slot, hh], v, m_ref.at[hh, qt], acc_ref.at[hh, qt])

    _pipelined_pairs(sched_ref, 0, len(full), lambda qt, kt, slot: scores(qt, kt, slot, False), update, None)
    _pipelined_pairs(sched_ref, 2, len(diag), lambda qt, kt, slot: scores(qt, kt, slot, True), update, None)

    for qt in range(SEQ // MLA_Q):
        o_t = jnp.concatenate([_flash_finish(m_ref.at[hh, qt], acc_ref.at[hh, qt])
                               for hh in range(2)], axis=0)
        o_ref[qt * MLA_Q:(qt + 1) * MLA_Q, :] = _dot_nt(eye_ref[...], o_t.astype(BF16)).astype(BF16)


def _schedule(*pair_lists):
    n = max(len(p) for p in pair_lists)
    out = np.zeros((2 * len(pair_lists), n), np.int32)
    for i, pairs in enumerate(pair_lists):
        out[2 * i, :len(pairs)] = [a for a, _ in pairs]
        out[2 * i + 1, :len(pairs)] = [b for _, b in pairs]
    return jnp.asarray(out)


def _mla_attention(q, k, vt, w_o, w_up, w_down):
    b = q.shape[0] // SEQ
    nq = SEQ // MLA_Q
    nv = SEQ // V_TILE
    chunks = MLA_K // KEY_CHUNK
    head_pairs = MLA_HEADS // 2
    steps = b * head_pairs
    weights = [w_o, w_up, w_down]
    assert all(w.shape[0] % (steps * BF16_ROWS) == 0 for w in weights)
    w_specs = [pl.BlockSpec((w.shape[0] // steps, w.shape[1]), lambda i, hp: (i * head_pairs + hp, 0))
               for w in weights]
    return pl.pallas_call(
        _mla_kernel,
        grid=(b, head_pairs),
        in_specs=[pl.BlockSpec(memory_space=pltpu.SMEM),
                  pl.BlockSpec((SEQ, 2 * LANES), lambda i, hp: (i, hp)),
                  pl.BlockSpec((SEQ, 2 * LANES), lambda i, hp: (i, hp)),
                  pl.BlockSpec((nv, 2 * V_ROWS, V_TILE), lambda i, hp: (i, hp, 0)),
                  pl.BlockSpec((MLA_Q, MLA_Q), lambda i, hp: (0, 0))] + w_specs,
        out_specs=[pl.BlockSpec((SEQ, LANES), lambda i, hp: (i, hp))] + w_specs,
        out_shape=[jax.ShapeDtypeStruct((q.shape[0], MLA_HEADS * MLA_V), BF16)]
                  + [jax.ShapeDtypeStruct(w.shape, BF16) for w in weights],
        scratch_shapes=[pltpu.VMEM((2, nq, chunks, 1, MLA_Q), F32),
                        pltpu.VMEM((2, nq, chunks, V_ROWS, MLA_Q), F32),
                        pltpu.VMEM((2, 2, MLA_K, MLA_Q), F32)],
        compiler_params=pltpu.CompilerParams(
            dimension_semantics=("parallel", "parallel"), vmem_limit_bytes=VMEM_LIMIT),
        name="mla_attention",
    )(_schedule(*_mla_pairs()), q, k, vt, jnp.eye(MLA_Q, dtype=BF16), *weights)


def _select_kernel(q_ref, kc_ref, vct_ref, tabc_ref, ovt_ref, eye_ref, qpad_ref, qaug_ref, ocmp_ref):
    step = pl.program_id(1)
    t = SELECT_Q
    heads = range(NSA_GROUP)
    tiles = range(t // NSA_Q)

    def group(g):
        q_pad = _dot(q_ref[:, g * 2 * LANES:(g + 1) * 2 * LANES], qpad_ref[...]).astype(BF16)
        low = lax.broadcasted_iota(jnp.int32, (t, LANES), 1) < NSA_HEAD_DIM
        zero = jnp.zeros((t, LANES), BF16)
        pairs = [q_ref[:, (2 * g + j) * LANES:(2 * g + j + 1) * LANES] for j in range(NSA_GROUP // 2)]
        q4 = jnp.concatenate([jnp.where(low == (r % 2 == 0), pairs[r // 2], zero) for r in heads], axis=0)

        per_tile = NSA_Q // CMP_STRIDE
        rows = [pl.ds(pl.multiple_of(CMP_TAB_BASE - (step * len(tiles) + u) * per_tile, per_tile), N_CHUNK)
                for u in tiles]
        bias_c = jnp.concatenate([tabc_ref[g * NSA_GROUP + r, rows[u], :] for r in heads for u in tiles], axis=1)
        valid = bias_c > 0.5 * NEG
        sc = _dot_nt(kc_ref[0, g], q4) + bias_c
        mx = jnp.max(sc, axis=0, keepdims=True)
        p = jnp.where(valid, jnp.exp2(sc - mx), 0.0)
        pc = p / jnp.maximum(jnp.sum(p, axis=0, keepdims=True), 1e-30)
        o_cmp = _dot(vct_ref[0, g], pc.astype(BF16)).astype(BF16)
        for u in tiles:
            ocmp_ref[0, g, u] = jnp.concatenate(
                [o_cmp[:, r * t + u * NSA_Q:r * t + (u + 1) * NSA_Q] for r in heads], axis=0)

        psum = sum(pc[:, r * t:(r + 1) * t] for r in heads)
        imp = sum(_dot(ovt_ref[...], piece) for piece in _split(psum, 3))
        jj = lax.broadcasted_iota(jnp.int32, (N_SEL, t), 0)
        blk = (step * t + lax.broadcasted_iota(jnp.int32, (N_SEL, t), 1)) // SEL_BLOCK
        ok = jj <= blk
        forced = ok & ((jj == 0) | (jj == blk) | (jj == blk - 1))
        score = jnp.where(forced, FORCE_SCORE, jnp.where(ok, imp, -jnp.inf))
        rank = jnp.zeros((N_SEL, t), jnp.int32)
        for i in range(N_SEL):
            ci = score[i:i + 1, :]
            beats = (ci > score) | ((ci == score) & (jj > i))
            rank = rank + beats.astype(jnp.int32)
        negsel = jnp.where(ok & (rank < SEL_TOP_N), 0.0, NEG).astype(BF16)
        spread = jnp.concatenate([jnp.zeros((AUG_BLOCK, t), BF16), negsel,
                                  jnp.zeros((LANES - AUG_ONE, t), BF16)], axis=0)
        mask_lanes = _dot_nt(eye_ref[...], spread).astype(BF16)
        lane = lax.broadcasted_iota(jnp.int32, (t, LANES), 1)
        for r in heads:
            far = tabc_ref[g * NSA_GROUP + r, 0:1, :]
            far_hi = far.astype(BF16)
            far_lo = (far - far_hi.astype(F32)).astype(BF16)
            aug = jnp.where(lane == AUG_ONE, far_hi, jnp.where(lane == AUG_ONE + 1, far_lo, mask_lanes))
            q_aug = jnp.where(lane < AUG_BLOCK, q_pad[:, r * LANES:(r + 1) * LANES], aug)
            for u in tiles:
                qaug_ref[0, g, u, r * NSA_Q:(r + 1) * NSA_Q, :] = q_aug[u * NSA_Q:(u + 1) * NSA_Q]

    for g in range(NSA_KV_HEADS):
        group(g)


def _nsa_select(q, kc, vct, tables):
    b = q.shape[0] // SEQ
    t = SELECT_Q
    steps = SEQ // t
    nq = SEQ // NSA_Q
    per = t // NSA_Q
    g_ = NSA_KV_HEADS
    c0 = np.arange(N_CHUNK)[None, :] * CMP_STRIDE
    j0 = np.arange(N_SEL)[:, None] * SEL_BLOCK
    ovt = np.clip(np.minimum(c0 + CMP_BLOCK, j0 + SEL_BLOCK) - np.maximum(c0, j0), 0, None) / CMP_BLOCK
    ovt[:, N_CHUNK - 1:] = 0.0
    return pl.pallas_call(
        _select_kernel,
        grid=(b, steps),
        in_specs=[pl.BlockSpec((t, g_ * 2 * LANES), lambda i, j: (i * steps + j, 0)),
                  pl.BlockSpec((1, g_, N_CHUNK, LANES), lambda i, j: (i, 0, 0, 0)),
                  pl.BlockSpec((1, g_, NSA_HEAD_DIM, N_CHUNK), lambda i, j: (i, 0, 0, 0)),
                  pl.BlockSpec((NSA_HEADS, CMP_TAB_ROWS, LANES), lambda i, j: (0, WIN_ROWS // CMP_TAB_ROWS, 0)),
                  pl.BlockSpec((N_SEL, N_CHUNK), lambda i, j: (0, 0)),
                  pl.BlockSpec((t, t), lambda i, j: (0, 0)),
                  pl.BlockSpec((2 * LANES, NSA_GROUP * LANES), lambda i, j: (0, 0))],
        out_specs=[pl.BlockSpec((1, g_, per, NSA_GROUP * NSA_Q, LANES), lambda i, j: (i, 0, j, 0, 0)),
                   pl.BlockSpec((1, g_, per, NSA_GROUP * NSA_HEAD_DIM, NSA_Q), lambda i, j: (i, 0, j, 0, 0))],
        out_shape=[jax.ShapeDtypeStruct((b, g_, nq, NSA_GROUP * NSA_Q, LANES), BF16),
                   jax.ShapeDtypeStruct((b, g_, nq, NSA_GROUP * NSA_HEAD_DIM, NSA_Q), BF16)],
        compiler_params=pltpu.CompilerParams(
            dimension_semantics=("parallel", "parallel"), vmem_limit_bytes=VMEM_LIMIT),
        name="nsa_select",
    )(q, kc, vct, tables, jnp.asarray(ovt, BF16), jnp.eye(t, dtype=BF16),
      jnp.asarray(_head_padding(NSA_GROUP), BF16))


def _nsa_pairs():
    nq, nk, per = SEQ // NSA_Q, SEQ // NSA_K, NSA_K // NSA_Q
    sel = [(qt, kt) for kt in range(nk) for qt in range(nq) if kt * NSA_K <= qt * NSA_Q]
    far = [(qt, kt) for qt, kt in sel if qt - kt * per >= SEL_FAR_TILE]
    near = [(qt, kt) for qt, kt in sel if qt - kt * per < SEL_FAR_TILE]
    win = [(qt, kt) for kt in range(nk) for qt in range(nq)
           if max(qt * NSA_Q - (WINDOW - 1), 0) // NSA_K <= kt and kt * NSA_K <= qt * NSA_Q]
    return far, near, win


def _attend_kernel(sched_ref, qaug_ref, ksel_ref, kwin_ref, vsel_ref, vwin_ref, ocmp_ref, gate_ref,
                   tabw_ref, eye_ref, o_ref, m_ref, acc_ref, s_ref, p_ref, alpha_ref):
    t = NSA_Q
    heads = range(NSA_GROUP)
    far, near, win = _nsa_pairs()
    _flash_init(m_ref, acc_ref)

    def scores(qt, kt, slot, k_ref, add_bias):
        s = _dot_nt(k_ref[pl.ds(pl.multiple_of(kt * NSA_K, NSA_K), NSA_K), :], qaug_ref[0, 0, qt])
        if add_bias:
            off = qt - kt * (NSA_K // NSA_Q)
            bias_rows = pl.ds(pl.multiple_of(off * NSA_K, NSA_K), NSA_K)
            s = s + jnp.concatenate([tabw_ref[r, bias_rows, :] for r in heads], axis=1)
        s_ref[slot] = s

    def softmax(qt, kt, slot, branch):
        _flash_softmax(s_ref[slot], m_ref.at[branch, qt], p_ref.at[slot], alpha_ref.at[slot], chunk=NSA_K)

    def accumulate(qt, kt, slot, v_ref, branch):
        _flash_accumulate(v_ref[kt], p_ref.at[slot], alpha_ref.at[slot], acc_ref.at[branch, qt], chunk=NSA_K)

    for row, pairs, k_ref, v_ref, branch, add_bias in ((0, far, ksel_ref, vsel_ref, 0, False),
                                                       (2, near, ksel_ref, vsel_ref, 0, True),
                                                       (4, win, kwin_ref, vwin_ref, 1, True)):
        _pipelined_pairs(sched_ref, row, len(pairs),
                         lambda qt, kt, slot, k_ref=k_ref, add_bias=add_bias: scores(qt, kt, slot, k_ref, add_bias),
                         lambda qt, kt, slot, branch=branch: softmax(qt, kt, slot, branch),
                         lambda qt, kt, slot, v_ref=v_ref, branch=branch: accumulate(qt, kt, slot, v_ref, branch),
                         unroll=NSA_UNROLL)

    d = NSA_HEAD_DIM
    group = pl.program_id(1)
    for qt in range(SEQ // t):
        o_cmp = ocmp_ref[0, 0, qt].astype(F32)
        o_sel, o_win = [_flash_finish(m_ref.at[br, qt], acc_ref.at[br, qt]) for br in range(2)]
        mixed = []
        for r in heads:
            gate = [gate_ref[qt, pl.ds(br * NSA_HEADS + group * NSA_GROUP + r, 1), :]
                    for br in range(N_BRANCH)]
            mixed.append(gate[0] * o_cmp[r * d:(r + 1) * d] + gate[1] * o_sel[:, r * t:(r + 1) * t]
                         + gate[2] * o_win[:, r * t:(r + 1) * t])
        o_ref[qt * t:(qt + 1) * t, :] = _dot_nt(eye_ref[...], jnp.concatenate(mixed, axis=0).astype(BF16)).astype(BF16)


def _nsa_attend(qaug, kaug, vt, ocmp, gates, tables):
    b = kaug.shape[0] // SEQ
    t = NSA_Q
    nq = SEQ // t
    nv = SEQ // V_TILE
    g_ = NSA_KV_HEADS
    k_spec = lambda off: pl.BlockSpec((SEQ, LANES), lambda i, g: (i, off + g))
    v_spec = lambda off: pl.BlockSpec((nv, V_ROWS, V_TILE), lambda i, g: (i, off + g, 0))
    return pl.pallas_call(
        _attend_kernel,
        grid=(b, g_),
        in_specs=[pl.BlockSpec(memory_space=pltpu.SMEM),
                  pl.BlockSpec((1, 1, nq, NSA_GROUP * t, LANES), lambda i, g: (i, g, 0, 0, 0)),
                  k_spec(0), k_spec(2), v_spec(0), v_spec(2),
                  pl.BlockSpec((1, 1, nq, NSA_GROUP * NSA_HEAD_DIM, t), lambda i, g: (i, g, 0, 0, 0)),
                  pl.BlockSpec((nq, LANES, t), lambda i, g: (i, 0, 0)),
                  pl.BlockSpec((NSA_GROUP, WIN_ROWS, LANES), lambda i, g: (g, 0, 0)),
                  pl.BlockSpec((t, t), lambda i, g: (0, 0))],
        out_specs=pl.BlockSpec((SEQ, 2 * LANES), lambda i, g: (i, g)),
        out_shape=jax.ShapeDtypeStruct((kaug.shape[0], NSA_HEADS * NSA_HEAD_DIM), BF16),
        scratch_shapes=[pltpu.VMEM((2, nq, 1, 1, NSA_GROUP * t), F32),
                        pltpu.VMEM((2, nq, 1, V_ROWS, NSA_GROUP * t), F32),
                        pltpu.VMEM((2, NSA_K, NSA_GROUP * t), F32),
                        pltpu.VMEM((2, NSA_K, NSA_GROUP * t), BF16),
                        pltpu.VMEM((2, 1, 1, NSA_GROUP * t), F32)],
        compiler_params=pltpu.CompilerParams(
            dimension_semantics=("parallel", "parallel"), vmem_limit_bytes=VMEM_LIMIT),
        name="nsa_attend",
    )(_schedule(*_nsa_pairs()), qaug, kaug, kaug, vt, vt, ocmp, gates, tables, jnp.eye(t, dtype=BF16))


def _tail_kernel(x_ref, oa_ref, ob_ref, ga_ref, gb_ref, wo_ref, ln2_ref, wup_ref, wdn_ref,
                 out_ref):
    width = MLA_HEADS * MLA_V
    a = (_rms(oa_ref[...].astype(F32), width) * ga_ref[...]).astype(BF16)
    b = (_rms(ob_ref[...].astype(F32), width) * gb_ref[...]).astype(BF16)
    x1 = x_ref[...] + _dot(a, wo_ref[:width]) + _dot(b, wo_ref[width:])
    h2 = (_rms(x1, D_MODEL) * ln2_ref[...]).astype(BF16)
    u = jnp.maximum(_dot(h2, wup_ref[...]), 0.0)
    out_ref[...] = x1 + _dot((u * u).astype(BF16), wdn_ref[...])


def _tail(x2d, o_a, o_b, gn_a, gn_b, w_o, ln2_g, w_up, w_down):
    t = x2d.shape[0]
    rows = TAIL_ROWS
    width = MLA_HEADS * MLA_V
    once = lambda a: pl.BlockSpec(a.shape, lambda i: (0,) * a.ndim, pipeline_mode=pl.Buffered(1))
    assert w_o.dtype == w_up.dtype == w_down.dtype == BF16
    consts = [gn_a[None, :], gn_b[None, :], w_o, ln2_g[None, :], w_up, w_down]
    row_spec = lambda w: pl.BlockSpec((rows, w), lambda i: (i, 0))
    return pl.pallas_call(
        _tail_kernel,
        grid=(t // rows,),
        in_specs=[row_spec(D_MODEL), row_spec(width), row_spec(width)] + [once(a) for a in consts],
        out_specs=row_spec(D_MODEL),
        out_shape=jax.ShapeDtypeStruct((t, D_MODEL), F32),
        compiler_params=pltpu.CompilerParams(dimension_semantics=("parallel",),
                                             vmem_limit_bytes=VMEM_LIMIT),
        name="tail",
    )(x2d, o_a, o_b, *consts)


def kernel(x, ln1_g, w_in, mla_cq_norm_g, mla_ckv_norm_g, mla_w_uq, mla_w_ukv, mla_q_gain, mla_k_gain,
           nsa_q_gain, nsa_k_gain, nsa_cmp_pe, nsa_cmp_w1, nsa_cmp_w2, rel_bias, grp_norm_mla,
           grp_norm_nsa, w_o, ln2_g, w_up, w_down):
    b, s, d = x.shape
    assert (s, d) == (SEQ, D_MODEL) and ln1_g.shape[0] == 1
    x2d = x.reshape(b * s, d)
    tables = _bias_tables(rel_bias)
    q_mla, k_mla, vt_mla, q_nsa, kdup, vt_nsa, kck, kcv, gates = _projections(
        x2d, ln1_g[0], w_in[0], mla_cq_norm_g[0], mla_ckv_norm_g[0], mla_w_uq[0], mla_w_ukv[0],
        mla_q_gain[0], mla_k_gain[0], nsa_q_gain[0], nsa_k_gain[0])
    kc, vct = _compress(kck, kcv, nsa_cmp_pe[0], nsa_cmp_w1[0], nsa_cmp_w2[0], nsa_k_gain[0, 0])
    o_a, w_o16, w_up16, w_down16 = _mla_attention(q_mla, k_mla, vt_mla, w_o[0], w_up[0], w_down[0])
    q_aug, ocmp = _nsa_select(q_nsa, kc, vct, tables)
    o_b = _nsa_attend(q_aug, kdup, vt_nsa, ocmp, gates, tables)
    out = _tail(x2d, o_a, o_b, grp_norm_mla[0], grp_norm_nsa[0], w_o16, ln2_g[0], w_up16, w_down16)
    return out.reshape(b, s, d)
```
